```python
import math, functools
import jax, jax.numpy as jnp
from jax import lax
import numpy as np

D_MODEL = 2048
BATCH = 2
SEQ = 4096
DEPTH = 4
DEC_BATCH = 8
DEC_SEQ = 4
PAST_LEN = 16384
PAGE_SIZE = 128

D_FF = 5504
EPS = 1e-6
N_BRANCH = 4
CONV_WIDTH = 3
W_CONV = 512
ATTN_GROUPS = ((128, 1), (512, 4), (2048, 16))
HEADS_PER_GROUP = 4
ATTN_HEAD_DIM = 64
N_ATTN_HEADS = HEADS_PER_GROUP * len(ATTN_GROUPS)
W_ATTN = N_ATTN_HEADS * ATTN_HEAD_DIM
ATTN_OUT = HEADS_PER_GROUP * ATTN_HEAD_DIM
GLA_HEADS = 4
GLA_DK = 64
GLA_DV = 128
GLA_LOW_RANK = 16
GLA_TAU = 16.0
GLA_CHUNK = 64
POOL_WINDOWS = (2, 4, 8, 16)
POOL_GROUP = 128
W_POOL = POOL_GROUP * len(POOL_WINDOWS)
POOL_HIST = max(POOL_WINDOWS) - 1
IN_SIZES = (W_CONV, W_CONV, W_CONV, W_ATTN, W_ATTN, W_ATTN,
            GLA_HEADS * GLA_DK, GLA_HEADS * GLA_DK, GLA_HEADS * GLA_DV, GLA_HEADS * GLA_DV,
            GLA_LOW_RANK, W_POOL, N_BRANCH * D_MODEL)
N_IN = sum(IN_SIZES)

kernel_name = 'hybrid_gated_branch_decoder_step'


def rmsnorm(x, g):
    xf = x.astype(jnp.float32)
    xf = xf * lax.rsqrt(jnp.mean(xf * xf, axis=-1, keepdims=True) + EPS)
    return xf.astype(x.dtype) * g


def swiglu(x, w_gate, w_up, w_down):
    return (jax.nn.silu(x @ w_gate) * (x @ w_up)) @ w_down


def alibi_slopes():
    i = jnp.arange(1, N_ATTN_HEADS + 1, dtype=jnp.float32)
    return jnp.exp2(-8.0 * i / N_ATTN_HEADS)


def short_conv(u, buf, w):
    T = u.shape[1]
    ext = jnp.concatenate([buf.astype(u.dtype), u], axis=1)
    z = ext[:, 0:T] * w[0]
    for i in range(1, CONV_WIDTH):
        z = z + ext[:, i:i + T] * w[i]
    return z, ext[:, T:]


def pool_mix(u, buf, pos0, w_grp, scale):
    B, T, _ = u.shape
    G = len(POOL_WINDOWS)
    ext = jnp.concatenate([buf.astype(u.dtype), u], axis=1)
    cs = jnp.cumsum(ext.astype(jnp.float32), axis=1)
    cs = jnp.pad(cs, ((0, 0), (1, 0), (0, 0))).reshape(B, POOL_HIST + T + 1, G, POOL_GROUP)
    pos = (pos0 + jnp.arange(T)).astype(jnp.float32)
    hi = cs[:, POOL_HIST + 1:]
    means = []
    for g, w in enumerate(POOL_WINDOWS):
        lo = cs[:, POOL_HIST + 1 - w: POOL_HIST + 1 - w + T, g]
        cnt = jnp.minimum(float(w), pos + 1.0)
        means.append((hi[:, :, g] - lo) / cnt[None, :, None])
    d = jnp.stack(means, axis=2) - u.reshape(B, T, G, POOL_GROUP).astype(jnp.float32)
    y = jnp.einsum('btgc,gcd->btgd', d.astype(u.dtype), w_grp).reshape(B, T, W_POOL) * scale
    return y, ext[:, T:]


def dilated_attn_prompt(q, k, v, slopes, window, dil):
    B, S, H, E = q.shape
    nk = window // dil
    M = S // dil
    nb = -(-M // nk)
    Mp = nb * nk

    def sub(a):
        a = a.reshape(B, M, dil, H, E).transpose(0, 2, 1, 3, 4)
        a = jnp.pad(a, ((0, 0), (0, 0), (0, Mp - M), (0, 0), (0, 0)))
        return a.reshape(B, dil, nb, nk, H, E)

    def with_prev(a):
        prev = jnp.pad(a, ((0, 0), (0, 0), (1, 0), (0, 0), (0, 0), (0, 0)))[:, :, :-1]
        return jnp.concatenate([prev, a], axis=3)

    qb = sub(q)
    kb = with_prev(sub(k))
    vb = with_prev(sub(v))
    s = jnp.einsum('brnqhe,brnkhe->brnhqk', qb, kb).astype(jnp.float32) / math.sqrt(E)
    qi = jnp.arange(nk)[:, None]
    ki = jnp.arange(2 * nk)[None, :]
    dist = qi - ki + nk
    key_sub = jnp.arange(nb)[:, None, None] * nk + ki[None] - nk
    valid = (dist >= 0) & (dist <= nk) & (key_sub >= 0)
    s = s - slopes[:, None, None] * (dist * dil).astype(jnp.float32)
    s = jnp.where(valid[:, None], s, -jnp.inf)
    m = jnp.max(s, axis=-1, keepdims=True)
    p = jnp.exp(s - m)
    l = jnp.sum(p, axis=-1, keepdims=True)
    o = jnp.einsum('brnhqk,brnkhe->brnhqe', p, vb.astype(jnp.float32)) / l
    lse = (m + jnp.log(l))[..., 0]

    def unsub(a):
        a = jnp.moveaxis(a, 3, 4)
        a = a.reshape((B, dil, Mp) + a.shape[4:])[:, :, :M]
        return jnp.moveaxis(a, 1, 2).reshape((B, S) + a.shape[3:])

    return unsub(o), unsub(lse)


def dilated_attn_sample(q, k_all, v_all, slopes, window, dil):
    B, T, H, E = q.shape
    L = k_all.shape[1] - T
    nk = window // dil
    steps = jnp.arange(nk + 1)
    idx = L + jnp.arange(T)[:, None] - steps[None, :] * dil
    valid = idx >= 0
    idx = jnp.maximum(idx, 0)
    kk = k_all[:, idx]
    vv = v_all[:, idx]
    s = jnp.einsum('bthe,btkhe->bthk', q, kk).astype(jnp.float32) / math.sqrt(E)
    s = s - slopes[:, None] * (steps * dil).astype(jnp.float32)[None, :]
    s = jnp.where(valid[:, None, :], s, -jnp.inf)
    m = jnp.max(s, axis=-1, keepdims=True)
    p = jnp.exp(s - m)
    l = jnp.sum(p, axis=-1, keepdims=True)
    o = jnp.einsum('bthk,btkhe->bthe', p, vv.astype(jnp.float32)) / l
    lse = (m + jnp.log(l))[..., 0]
    return o, lse


def merge_groups(outs, lses):
    w = jax.nn.softmax(jnp.stack(lses, 0), axis=0)
    o = jnp.sum(w[..., None] * jnp.stack(outs, 0), axis=0)
    return o.reshape(o.shape[:2] + (-1,))


def attend_prompt(q, k, v, slopes):
    outs, lses, bufs = [], [], []
    for g, (window, dil) in enumerate(ATTN_GROUPS):
        hs = slice(g * HEADS_PER_GROUP, (g + 1) * HEADS_PER_GROUP)
        o, lse = dilated_attn_prompt(q[:, :, hs], k[:, :, hs], v[:, :, hs], slopes[hs], window, dil)
        outs.append(o)
        lses.append(lse)
        n_keep = min(window, q.shape[1])
        bufs.append(jnp.stack([k[:, -n_keep:, hs], v[:, -n_keep:, hs]], axis=2))
    return merge_groups(outs, lses).astype(q.dtype), bufs


def attend_sample(q, k, v, slopes, caches):
    outs, lses, bufs = [], [], []
    for g, (window, dil) in enumerate(ATTN_GROUPS):
        hs = slice(g * HEADS_PER_GROUP, (g + 1) * HEADS_PER_GROUP)
        buf = caches[g]
        L = buf.shape[1]
        k_all = jnp.concatenate([buf[:, :, 0].astype(k.dtype), k[:, :, hs]], axis=1)
        v_all = jnp.concatenate([buf[:, :, 1].astype(v.dtype), v[:, :, hs]], axis=1)
        o, lse = dilated_attn_sample(q[:, :, hs], k_all, v_all, slopes[hs], window, dil)
        outs.append(o)
        lses.append(lse)
        bufs.append(jnp.stack([k_all[:, -L:], v_all[:, -L:]], axis=2))
    return merge_groups(outs, lses).astype(q.dtype), bufs


def gla_chunked(q, k, v, log_a, S0):
    B, T, H, K = q.shape
    V = v.shape[-1]
    C = min(GLA_CHUNK, T)
    n = -(-T // C)
    Tp = n * C

    def blocks(a):
        a = jnp.pad(a.astype(jnp.float32), ((0, 0), (0, Tp - T), (0, 0), (0, 0)))
        return a.reshape(B, n, C, H, a.shape[-1]).transpose(1, 0, 3, 2, 4)

    qf, kf, vf = blocks(q), blocks(k), blocks(v)
    b = jnp.cumsum(blocks(log_a), axis=3)
    b_last = b[:, :, :, -1:]
    q_t = qf * jnp.exp(b)
    k_t = kf * jnp.exp(-b)
    k_end = kf * jnp.exp(b_last - b)
    mask = jnp.tril(jnp.ones((C, C), dtype=bool))
    A = jnp.where(mask, jnp.einsum('nbhtk,nbhsk->nbhts', q_t, k_t), 0.0)
    o_intra = jnp.einsum('nbhts,nbhsv->nbhtv', A, vf)
    dS = jnp.einsum('nbhsk,nbhsv->nbhkv', k_end, vf)
    decay = jnp.exp(b_last[:, :, :, 0])

    def step(S, inp):
        dec, d = inp
        return dec[..., None] * S + d, S

    S_fin, S_prev = lax.scan(step, S0.astype(jnp.float32), (decay, dS))
    o = o_intra + jnp.einsum('nbhtk,nbhkv->nbhtv', q_t, S_prev)
    o = o.transpose(1, 0, 3, 2, 4).reshape(B, Tp, H, V)[:, :T]
    return o.astype(q.dtype), S_fin.astype(S0.dtype)


def token_mix(xn, p, conv_buf, pool_buf, gla_state, pos0, attend):
    B, T, _ = xn.shape
    split_at = np.cumsum(IN_SIZES)[:-1].tolist()
    (c_b, c_c, c_h, a_q, a_k, a_v, g_q, g_k, g_v, g_r, g_lr, pool_in, gate_in) = jnp.split(
        xn @ p['w_in'], split_at, axis=-1)
    z, conv_new = short_conv(c_c * c_h, conv_buf, p['conv_w'])
    y_a = c_b * z
    hd = (B, T, N_ATTN_HEADS, ATTN_HEAD_DIM)
    q = rmsnorm(a_q.reshape(hd), p['q_gain'])
    k = rmsnorm(a_k.reshape(hd), p['k_gain'])
    y_b, win_new = attend(q, k, a_v.reshape(hd))
    gq = g_q.reshape(B, T, GLA_HEADS, GLA_DK) * (GLA_DK ** -0.5)
    gk = g_k.reshape(B, T, GLA_HEADS, GLA_DK)
    gv = g_v.reshape(B, T, GLA_HEADS, GLA_DV)
    log_a = jax.nn.log_sigmoid((g_lr @ p['gla_w_a2'] + p['gla_b_a']).astype(jnp.float32)) / GLA_TAU
    o, gla_new = gla_chunked(gq, gk, gv, log_a.reshape(B, T, GLA_HEADS, GLA_DK), gla_state)
    y_c = rmsnorm(o, p['gla_norm']).reshape(B, T, GLA_HEADS * GLA_DV) * jax.nn.silu(g_r)
    y_d, pool_new = pool_mix(pool_in, pool_buf, pos0, p['pool_w'], p['pool_scale'])
    gates = jax.nn.sigmoid(gate_in.reshape(B, T, N_BRANCH, D_MODEL))
    merged = (gates[:, :, 0] * (y_a @ p['up_a']) + gates[:, :, 1] * (y_b @ p['up_b'])
              + gates[:, :, 2] * (y_c @ p['up_c']) + gates[:, :, 3] * (y_d @ p['up_d']))
    return merged @ p['w_out'], (conv_new, win_new, gla_new, pool_new)


def decoder_layer(x, p, conv_buf, pool_buf, gla_state, pos0, attend):
    x = x + 0.5 * swiglu(rmsnorm(x, p['n_ff1']), p['ff1_gate'], p['ff1_up'], p['ff1_down'])
    mix, new_state = token_mix(rmsnorm(x, p['n_mix']), p, conv_buf, pool_buf, gla_state, pos0, attend)
    x = x + mix
    x = x + 0.5 * swiglu(rmsnorm(x, p['n_ff2']), p['ff2_gate'], p['ff2_up'], p['ff2_down'])
    return x, new_state


def setup_inputs(seed: int = 0) -> dict:
    key = jax.random.key(seed)
    keys = iter(jax.random.split(key, 40))

    def nrm(shape, scale=1.0):
        return jax.random.normal(next(keys), shape, jnp.float32) * scale

    def gain(shape):
        return 1.0 + 0.02 * jax.random.normal(next(keys), shape, jnp.float32)

    d = {}
    d['x_prompt'] = nrm((BATCH, SEQ, D_MODEL))
    d['x_sample'] = nrm((DEC_BATCH, DEC_SEQ, D_MODEL))
    d['state_conv'] = nrm((DEPTH, DEC_BATCH, CONV_WIDTH - 1, W_CONV))
    for window, _ in ATTN_GROUPS:
        d['cache_w%d_kv' % window] = nrm((DEPTH, DEC_BATCH, min(window, PAST_LEN), 2, HEADS_PER_GROUP, ATTN_HEAD_DIM))
    d['state_gla'] = nrm((DEPTH, DEC_BATCH, GLA_HEADS, GLA_DK, GLA_DV))
    d['state_pool'] = nrm((DEPTH, DEC_BATCH, POOL_HIST, W_POOL))
    d['norm_ff1'] = gain((DEPTH, D_MODEL))
    d['ff1_gate'] = nrm((DEPTH, D_MODEL, D_FF), D_MODEL ** -0.5)
    d['ff1_up'] = nrm((DEPTH, D_MODEL, D_FF), D_MODEL ** -0.5)
    d['ff1_down'] = nrm((DEPTH, D_FF, D_MODEL), D_FF ** -0.5)
    d['norm_mix'] = gain((DEPTH, D_MODEL))
    d['w_in'] = nrm((DEPTH, D_MODEL, N_IN), D_MODEL ** -0.5)
    d['conv_w'] = nrm((DEPTH, CONV_WIDTH, W_CONV), CONV_WIDTH ** -0.5)
    d['attn_q_gain'] = gain((DEPTH, N_ATTN_HEADS, ATTN_HEAD_DIM))
    d['attn_k_gain'] = gain((DEPTH, N_ATTN_HEADS, ATTN_HEAD_DIM))
    d['gla_w_a2'] = nrm((DEPTH, GLA_LOW_RANK, GLA_HEADS * GLA_DK), GLA_LOW_RANK ** -0.5)
    d['gla_b_a'] = nrm((DEPTH, GLA_HEADS * GLA_DK), 0.1)
    d['gla_norm'] = gain((DEPTH, GLA_DV))
    d['pool_w'] = nrm((DEPTH, len(POOL_WINDOWS), POOL_GROUP, POOL_GROUP), POOL_GROUP ** -0.5)
    d['pool_scale'] = gain((DEPTH, W_POOL))
    d['w_up_conv'] = nrm((DEPTH, W_CONV, D_MODEL), W_CONV ** -0.5)
    d['w_up_attn'] = nrm((DEPTH, ATTN_OUT, D_MODEL), ATTN_OUT ** -0.5)
    d['w_up_gla'] = nrm((DEPTH, GLA_HEADS * GLA_DV, D_MODEL), (GLA_HEADS * GLA_DV) ** -0.5)
    d['w_up_pool'] = nrm((DEPTH, W_POOL, D_MODEL), W_POOL ** -0.5)
    d['w_out'] = nrm((DEPTH, D_MODEL, D_MODEL), D_MODEL ** -0.5)
    d['norm_ff2'] = gain((DEPTH, D_MODEL))
    d['ff2_gate'] = nrm((DEPTH, D_MODEL, D_FF), D_MODEL ** -0.5)
    d['ff2_up'] = nrm((DEPTH, D_MODEL, D_FF), D_MODEL ** -0.5)
    d['ff2_down'] = nrm((DEPTH, D_FF, D_MODEL), D_FF ** -0.5)
    return d


def reference(x_prompt, x_sample, state_conv, cache_w128_kv, cache_w512_kv, cache_w2048_kv, state_gla, state_pool,
              norm_ff1, ff1_gate, ff1_up, ff1_down, norm_mix, w_in, conv_w, attn_q_gain, attn_k_gain,
              gla_w_a2, gla_b_a, gla_norm, pool_w, pool_scale, w_up_conv, w_up_attn, w_up_gla, w_up_pool, w_out,
              norm_ff2, ff2_gate, ff2_up, ff2_down):
    slopes = alibi_slopes()
    win_caches = (cache_w128_kv, cache_w512_kv, cache_w2048_kv)
    bp = x_prompt.shape[0]
    dt = x_prompt.dtype
    yp, ys = x_prompt, x_sample
    conv_p, conv_s, gla_p, gla_s, pool_p, pool_s = [], [], [], [], [], []
    win_p = [[] for _ in ATTN_GROUPS]
    win_s = [[] for _ in ATTN_GROUPS]
    for l in range(DEPTH):
        p = {'n_ff1': norm_ff1[l], 'ff1_gate': ff1_gate[l], 'ff1_up': ff1_up[l], 'ff1_down': ff1_down[l],
             'n_mix': norm_mix[l], 'w_in': w_in[l], 'conv_w': conv_w[l],
             'q_gain': attn_q_gain[l], 'k_gain': attn_k_gain[l],
             'gla_w_a2': gla_w_a2[l], 'gla_b_a': gla_b_a[l], 'gla_norm': gla_norm[l],
             'pool_w': pool_w[l], 'pool_scale': pool_scale[l],
             'up_a': w_up_conv[l], 'up_b': w_up_attn[l], 'up_c': w_up_gla[l], 'up_d': w_up_pool[l],
             'w_out': w_out[l], 'n_ff2': norm_ff2[l], 'ff2_gate': ff2_gate[l], 'ff2_up': ff2_up[l],
             'ff2_down': ff2_down[l]}
        yp, (c_new, w_new, g_new, q_new) = decoder_layer(
            yp, p,
            jnp.zeros((bp, CONV_WIDTH - 1, W_CONV), dt),
            jnp.zeros((bp, POOL_HIST, W_POOL), dt),
            jnp.zeros((bp, GLA_HEADS, GLA_DK, GLA_DV), jnp.float32),
            0, functools.partial(attend_prompt, slopes=slopes))
        conv_p.append(c_new)
        gla_p.append(g_new)
        pool_p.append(q_new)
        for g in range(len(ATTN_GROUPS)):
            win_p[g].append(w_new[g])
        caches_l = tuple(c[l] for c in win_caches)
        ys, (c_new, w_new, g_new, q_new) = decoder_layer(
            ys, p, state_conv[l], state_pool[l], state_gla[l], PAST_LEN,
            functools.partial(attend_sample, slopes=slopes, caches=caches_l))
        conv_s.append(c_new)
        gla_s.append(g_new)
        pool_s.append(q_new)
        for g in range(len(ATTN_GROUPS)):
            win_s[g].append(w_new[g])
    conv_prompt = jnp.stack(conv_p)
    conv_sample = jnp.stack(conv_s)
    w128_prompt = jnp.stack(win_p[0])
    w128_sample = jnp.stack(win_s[0])
    w512_prompt = jnp.stack(win_p[1])
    w512_sample = jnp.stack(win_s[1])
    w2048_prompt = jnp.stack(win_p[2])
    w2048_sample = jnp.stack(win_s[2])
    gla_prompt = jnp.stack(gla_p)
    gla_sample = jnp.stack(gla_s)
    pool_prompt = jnp.stack(pool_p)
    pool_sample = jnp.stack(pool_s)
    return (yp, ys, conv_prompt, conv_sample, w128_prompt, w128_sample, w512_prompt, w512_sample,
            w2048_prompt, w2048_sample, gla_prompt, gla_sample, pool_prompt, pool_sample)
```

```python
import functools

import numpy as np
import jax
import jax.numpy as jnp
from jax import lax
from jax.experimental import pallas as pl
from jax.experimental.pallas import tpu as pltpu

BF16 = jnp.bfloat16
F32 = jnp.float32

D_MODEL = 2048
DEPTH = 4
PAST_LEN = 16384
D_FF = 5504
EPS = 1e-6
N_BRANCH = 4
W_CONV = 512
ATTN_GROUPS = ((128, 1), (512, 4), (2048, 16))
N_ATTN_HEADS = 12
HEAD_DIM = 64
GLA_DK = 64
GLA_TAU = 16.0
GLA_CHUNK = 64
POOL_WINDOWS = (2, 4, 8, 16)
W_POOL = 512
NK = 128
LANES = 128
NEG = -1e30

COL_CB, COL_CC, COL_CH, COL_GV, COL_GR, COL_POOL = 0, 512, 1024, 1536, 2048, 2560
COL_AQ, COL_AK, COL_AV = 3072, 3840, 4608
COL_GQ, COL_GK, COL_LR = 5376, 5632, 5888
N_MIX = 6016

VMEM_LIMIT = 56 * 1024 * 1024


def _cparams(sem):
    return pltpu.CompilerParams(dimension_semantics=sem, vmem_limit_bytes=VMEM_LIMIT)


def _rms(x):
    return x * lax.rsqrt(jnp.mean(x * x, axis=-1, keepdims=True) + EPS)


def _ffn_up_kernel(x_ref, g_ref, wg_ref, wu_ref, h_ref, xn_scr):
    @pl.when(pl.program_id(1) == 0)
    def _():
        xn_scr[...] = (_rms(x_ref[...]) * g_ref[...]).astype(BF16)

    xn = xn_scr[...]
    a = jnp.dot(xn, wg_ref[...], preferred_element_type=F32)
    b = jnp.dot(xn, wu_ref[...], preferred_element_type=F32)
    h_ref[...] = (a * jax.nn.sigmoid(a) * b).astype(BF16)


def ffn_up(x, gain, wg, wu, l, tm, tn):
    M = x.shape[0]
    return pl.pallas_call(
        _ffn_up_kernel,
        grid=(M // tm, pl.cdiv(D_FF, tn)),
        in_specs=[
            pl.BlockSpec((tm, D_MODEL), lambda m, n: (m, 0)),
            pl.BlockSpec((None, 1, D_MODEL), lambda m, n: (l, 0, 0)),
            pl.BlockSpec((None, D_MODEL, tn), lambda m, n: (l, 0, n)),
            pl.BlockSpec((None, D_MODEL, tn), lambda m, n: (l, 0, n)),
        ],
        out_specs=pl.BlockSpec((tm, tn), lambda m, n: (m, n)),
        out_shape=jax.ShapeDtypeStruct((M, D_FF), BF16),
        scratch_shapes=[pltpu.VMEM((tm, D_MODEL), BF16)],
        compiler_params=_cparams(("parallel", "arbitrary")),
    )(x, gain, wg, wu)


def _matmul_res_kernel(x_ref, a_ref, w_ref, o_ref, *, scale):
    o_ref[...] = x_ref[...] + scale * jnp.dot(a_ref[...], w_ref[...], preferred_element_type=F32)


def matmul_res(x, a, w, l, scale, tm, tn):
    M, K = a.shape
    N = x.shape[1]
    return pl.pallas_call(
        functools.partial(_matmul_res_kernel, scale=scale),
        grid=(M // tm, N // tn),
        in_specs=[
            pl.BlockSpec((tm, tn), lambda m, n: (m, n)),
            pl.BlockSpec((tm, K), lambda m, n: (m, 0)),
            pl.BlockSpec((None, K, tn), lambda m, n: (l, 0, n)),
        ],
        out_specs=pl.BlockSpec((tm, tn), lambda m, n: (m, n)),
        out_shape=jax.ShapeDtypeStruct((M, N), F32),
        compiler_params=_cparams(("parallel", "arbitrary")),
    )(x, a, w)


def _mix_in_kernel(x_ref, g_ref, w_ref, p_ref, xn_ref):
    @pl.when(pl.program_id(1) == 0)
    def _():
        xn_ref[...] = (_rms(x_ref[...]) * g_ref[...]).astype(BF16)

    p_ref[...] = jnp.dot(xn_ref[...], w_ref[...], preferred_element_type=F32)


def mix_in(x, gain, w_mix, l, tm, tn):
    M = x.shape[0]
    return pl.pallas_call(
        _mix_in_kernel,
        grid=(M // tm, pl.cdiv(N_MIX, tn)),
        in_specs=[
            pl.BlockSpec((tm, D_MODEL), lambda m, n: (m, 0)),
            pl.BlockSpec((None, 1, D_MODEL), lambda m, n: (l, 0, 0)),
            pl.BlockSpec((None, D_MODEL, tn), lambda m, n: (l, 0, n)),
        ],
        out_specs=[
            pl.BlockSpec((tm, tn), lambda m, n: (m, n)),
            pl.BlockSpec((tm, D_MODEL), lambda m, n: (m, 0)),
        ],
        out_shape=[jax.ShapeDtypeStruct((M, N_MIX), F32), jax.ShapeDtypeStruct((M, D_MODEL), BF16)],
        compiler_params=_cparams(("parallel", "arbitrary")),
    )(x, gain, w_mix)


CONV_HIST = 8
POOL_HIST16 = 16


def _conv_pool_kernel(cb_ref, cc_ref, ch_ref, pin_ref, cbuf_ref, pbuf_ref, cw_ref, pw_ref, ps_ref,
                      ya_ref, yd_ref, cnew_ref, pnew_ref, uext, pext, *, tt, t_true, pos0):
    t = pl.program_id(1)

    @pl.when(t == 0)
    def _():
        uext[0:CONV_HIST, :] = cbuf_ref[...]
        pext[0:POOL_HIST16, :] = pbuf_ref[...]

    u = cc_ref[...] * ch_ref[...]
    uext[CONV_HIST:CONV_HIST + tt, :] = u
    w = cw_ref[...]
    z = uext[CONV_HIST - 2:CONV_HIST - 2 + tt, :] * w[0:1, :]
    z = z + uext[CONV_HIST - 1:CONV_HIST - 1 + tt, :] * w[1:2, :]
    z = z + u * w[2:3, :]
    ya_ref[...] = (cb_ref[...] * z).astype(BF16)
    cnew_ref[...] = uext[CONV_HIST + t_true - 2:CONV_HIST + t_true, :]
    uext[0:CONV_HIST, :] = uext[tt:tt + CONV_HIST, :]

    pin = pin_ref[...]
    pext[POOL_HIST16:POOL_HIST16 + tt, :] = pin
    pos = (pos0 + t * tt + lax.broadcasted_iota(jnp.int32, (tt, 1), 0)).astype(F32)
    for g, win in enumerate(POOL_WINDOWS):
        ls = slice(g * LANES, (g + 1) * LANES)
        acc = pin[:, ls]
        for i in range(1, win):
            acc = acc + pext[POOL_HIST16 - i:POOL_HIST16 - i + tt, ls]
        cnt = jnp.minimum(float(win), pos + 1.0)
        d = acc / cnt - pin[:, ls]
        y = jnp.dot(d.astype(BF16), pw_ref[g], preferred_element_type=F32) * ps_ref[:, ls]
        yd_ref[:, ls] = y.astype(BF16)
    pnew_ref[...] = pext[t_true:t_true + POOL_HIST16, :]
    pext[0:POOL_HIST16, :] = pext[tt:tt + POOL_HIST16, :]


def conv_pool(P3, cbuf8, pbuf16, conv_w, pool_w, pool_scale, l, tt, t_true, pos0):
    B, T, _ = P3.shape

    def col(c):
        return pl.BlockSpec((None, tt, 512), lambda b, t: (b, t, c // 512))

    return pl.pallas_call(
        functools.partial(_conv_pool_kernel, tt=tt, t_true=t_true, pos0=pos0),
        grid=(B, T // tt),
        in_specs=[
            col(COL_CB), col(COL_CC), col(COL_CH), col(COL_POOL),
            pl.BlockSpec((None, CONV_HIST, 512), lambda b, t: (b, 0, 0)),
            pl.BlockSpec((None, POOL_HIST16, 512), lambda b, t: (b, 0, 0)),
            pl.BlockSpec((None, 3, 512), lambda b, t: (l, 0, 0)),
            pl.BlockSpec((None, 4, LANES, LANES), lambda b, t: (l, 0, 0, 0)),
            pl.BlockSpec((None, 1, 512), lambda b, t: (l, 0, 0)),
        ],
        out_specs=[
            pl.BlockSpec((None, tt, 512), lambda b, t: (b, t, 0)),
            pl.BlockSpec((None, tt, 512), lambda b, t: (b, t, 0)),
            pl.BlockSpec((None, 2, 512), lambda b, t: (b, 0, 0)),
            pl.BlockSpec((None, POOL_HIST16, 512), lambda b, t: (b, 0, 0)),
        ],
        out_shape=[
            jax.ShapeDtypeStruct((B, T, 512), BF16),
            jax.ShapeDtypeStruct((B, T, 512), BF16),
            jax.ShapeDtypeStruct((B, 2, 512), F32),
            jax.ShapeDtypeStruct((B, POOL_HIST16, 512), F32),
        ],
        scratch_shapes=[pltpu.VMEM((CONV_HIST + tt, 512), F32), pltpu.VMEM((POOL_HIST16 + tt, 512), F32)],
        compiler_params=_cparams(("parallel", "arbitrary")),
    )(P3, P3, P3, P3, cbuf8, pbuf16, conv_w, pool_w, pool_scale)


def _gla_kernel(q_ref, k_ref, v_ref, r_ref, lr_ref, wa_ref, ba_ref, gn_ref, s0_ref,
                yc_ref, sfin_ref, s_scr, *, tt, C, c_true):
    t = pl.program_id(1)

    @pl.when(t == 0)
    def _():
        s_scr[...] = s0_ref[...]

    row = lax.broadcasted_iota(jnp.int32, (C, 1), 0)
    colc = lax.broadcasted_iota(jnp.int32, (1, C), 1)
    tril = row >= colc
    tril_f = tril.astype(F32)
    lane = lax.broadcasted_iota(jnp.int32, (1, LANES), 1)
    head0 = lane < GLA_DK
    srow = lax.broadcasted_iota(jnp.int32, (LANES, 1), 0)
    eye = srow == lane
    gn = gn_ref[...]

    def chunk(c, carry):
        r0 = pl.multiple_of(c * C, C)
        rows = pl.ds(r0, C)
        q = q_ref[rows, :] * (GLA_DK ** -0.5)
        k = k_ref[rows, :]
        la = jnp.dot(lr_ref[rows, :].astype(BF16), wa_ref[...], preferred_element_type=F32) + ba_ref[...]
        la = jax.nn.log_sigmoid(la) / GLA_TAU
        if c_true < C:
            la = jnp.where(row < c_true, la, 0.0)
        b = jnp.dot(tril_f, la, preferred_element_type=F32, precision=lax.Precision.HIGHEST)
        b_last = b[C - 1:C, :]
        qt = q * jnp.exp(b)
        kt = k * jnp.exp(-b)
        kend = k * jnp.exp(b_last - b)
        for p in range(2):
            ls = slice(p * LANES, (p + 1) * LANES)
            qt_p = qt[:, ls]
            kt_p = kt[:, ls].astype(BF16)
            kend_p = kend[:, ls].astype(BF16)
            s_p = s_scr[ls, :]
            s_pb = s_p.astype(BF16)
            dec = jnp.exp(jnp.sum(jnp.where(eye, b_last[:, ls], 0.0), axis=1, keepdims=True))
            s_new = dec * s_p
            for hh in range(2):
                h = 2 * p + hh
                hs = slice(h * LANES, (h + 1) * LANES)
                hmask = head0 if hh == 0 else jnp.logical_not(head0)
                qm = jnp.where(hmask, qt_p, 0.0).astype(BF16)
                a = lax.dot_general(qm, kt_p, (((1,), (1,)), ((), ())), preferred_element_type=F32)
                a = jnp.where(tril, a, 0.0)
                v_h = v_ref[rows, hs]
                if c_true < C:
                    v_h = jnp.where(row < c_true, v_h, 0.0)
                v_hb = v_h.astype(BF16)
                o = jnp.dot(a.astype(BF16), v_hb, preferred_element_type=F32)
                o = o + jnp.dot(qm, s_pb, preferred_element_type=F32)
                ds = lax.dot_general(kend_p, v_hb, (((0,), (0,)), ((), ())), preferred_element_type=F32)
                rmask = (srow < GLA_DK) if hh == 0 else (srow >= GLA_DK)
                s_new = s_new + jnp.where(rmask, ds, 0.0)
                r_h = r_ref[rows, hs]
                yc_ref[rows, hs] = (_rms(o) * gn * (r_h * jax.nn.sigmoid(r_h))).astype(BF16)
            s_scr[ls, :] = s_new
        return carry

    lax.fori_loop(0, tt // C, chunk, 0)
    sfin_ref[...] = s_scr[...]


def gla(P3, wa_pad, b_a, gla_norm, s0, l, tt, C, c_true):
    B, T, _ = P3.shape
    return pl.pallas_call(
        functools.partial(_gla_kernel, tt=tt, C=C, c_true=c_true),
        grid=(B, T // tt),
        in_specs=[
            pl.BlockSpec((None, tt, 256), lambda b, t: (b, t, COL_GQ // 256)),
            pl.BlockSpec((None, tt, 256), lambda b, t: (b, t, COL_GK // 256)),
            pl.BlockSpec((None, tt, 512), lambda b, t: (b, t, COL_GV // 512)),
            pl.BlockSpec((None, tt, 512), lambda b, t: (b, t, COL_GR // 512)),
            pl.BlockSpec((None, tt, LANES), lambda b, t: (b, t, COL_LR // LANES)),
            pl.BlockSpec((None, LANES, 256), lambda b, t: (l, 0, 0)),
            pl.BlockSpec((None, 1, 256), lambda b, t: (l, 0, 0)),
            pl.BlockSpec((None, 1, LANES), lambda b, t: (l, 0, 0)),
            pl.BlockSpec((None, 256, LANES), lambda b, t: (b, 0, 0)),
        ],
        out_specs=[
            pl.BlockSpec((None, tt, 512), lambda b, t: (b, t, 0)),
            pl.BlockSpec((None, 256, LANES), lambda b, t: (b, 0, 0)),
        ],
        out_shape=[jax.ShapeDtypeStruct((B, T, 512), BF16), jax.ShapeDtypeStruct((B, 256, LANES), F32)],
        scratch_shapes=[pltpu.VMEM((256, LANES), F32)],
        compiler_params=_cparams(("parallel", "arbitrary")),
    )(P3, P3, P3, P3, P3, wa_pad, b_a, gla_norm, s0)


def _head_norm(x, gain, head0):
    x2 = x * x
    s0 = jnp.sum(jnp.where(head0, x2, 0.0), axis=-1, keepdims=True)
    s1 = jnp.sum(jnp.where(head0, 0.0, x2), axis=-1, keepdims=True)
    ms = jnp.where(head0, s0, s1) / float(HEAD_DIM)
    return x * lax.rsqrt(ms + EPS) * gain


NORM_ROWS = 512


def _attn_prompt_kernel(q_ref, k_ref, v_ref, qg_ref, kg_ref, sl_ref,
                        o_ref, lse_ref, kc_ref, vc_ref,
                        tmp, qd, kd, vd, od, ld, *, S, dil, n_keep, head_base):
    msub = S // dil
    nb = msub // NK
    kstride = msub + NK
    lane = lax.broadcasted_iota(jnp.int32, (1, LANES), 1)
    head0 = lane < HEAD_DIM

    def norm_into_tmp(src_ref, g_ref):
        def body(i, carry):
            rows = pl.ds(pl.multiple_of(i * NORM_ROWS, NORM_ROWS), NORM_ROWS)
            tmp[rows, :] = _head_norm(src_ref[rows, :], g_ref[...], head0)
            return carry
        lax.fori_loop(0, S // NORM_ROWS, body, 0)

    norm_into_tmp(q_ref, qg_ref)
    for r in range(dil):
        qd[r * msub:(r + 1) * msub, :] = tmp[pl.ds(r, msub, stride=dil), :].astype(BF16)
    norm_into_tmp(k_ref, kg_ref)
    kc_ref[...] = tmp[S - n_keep:S, :]
    vc_ref[...] = v_ref[S - n_keep:S, :]
    zeros = jnp.zeros((NK, LANES), BF16)
    for r in range(dil):
        kd[r * kstride:r * kstride + NK, :] = zeros
        vd[r * kstride:r * kstride + NK, :] = zeros
        kd[r * kstride + NK:(r + 1) * kstride, :] = tmp[pl.ds(r, msub, stride=dil), :].astype(BF16)
        vd[r * kstride + NK:(r + 1) * kstride, :] = v_ref[pl.ds(r, msub, stride=dil), :].astype(BF16)

    qi = lax.broadcasted_iota(jnp.int32, (NK, 2 * NK), 0)
    ki = lax.broadcasted_iota(jnp.int32, (NK, 2 * NK), 1)
    dist = qi - ki + NK
    band = (dist >= 0) & (dist <= NK)
    alibi = (dist * dil).astype(F32)
    cur = ki >= NK

    def block(j, carry):
        r = j // nb
        n = j - r * nb
        qb = pl.multiple_of(j * NK, NK)
        kb = pl.multiple_of(j * NK + r * NK, NK)
        valid = band & (cur | (n > 0))
        q2 = qd[pl.ds(qb, NK), :]
        k2 = kd[pl.ds(kb, 2 * NK), :]
        v2 = vd[pl.ds(kb, 2 * NK), :]
        outs, lses = [], []
        for hh in range(2):
            hmask = head0 if hh == 0 else jnp.logical_not(head0)
            qm = jnp.where(hmask, q2, jnp.zeros_like(q2))
            s = lax.dot_general(qm, k2, (((1,), (1,)), ((), ())), preferred_element_type=F32)
            slope = sl_ref[head_base + 2 * pl.program_id(1) + hh]
            s = s / float(np.sqrt(HEAD_DIM)) - slope * alibi
            s = jnp.where(valid, s, NEG)
            m = jnp.max(s, axis=-1, keepdims=True)
            p = jnp.exp(s - m)
            lsum = jnp.sum(p, axis=-1, keepdims=True)
            outs.append(jnp.dot(p.astype(BF16), v2, preferred_element_type=F32) / lsum)
            lses.append(m + jnp.log(lsum))
        od[pl.ds(qb, NK), :] = jnp.where(head0, outs[0], outs[1])
        ld[pl.ds(qb, NK), :] = jnp.where(head0, lses[0], lses[1])
        return carry

    lax.fori_loop(0, dil * nb, block, 0)

    for r in range(dil):
        o_ref[pl.ds(r, msub, stride=dil), :] = od[r * msub:(r + 1) * msub, :]
        lse_ref[pl.ds(r, msub, stride=dil), :] = ld[r * msub:(r + 1) * msub, :]


def attn_prompt(P3, q_gain, k_gain, slopes, l, g):
    B, S, _ = P3.shape
    window, dil = ATTN_GROUPS[g]
    n_keep = min(window, S)
    msub = S // dil

    def col(c):
        return pl.BlockSpec((None, S, LANES), lambda b, p: (b, 0, c // LANES + 2 * g + p))

    def par(stack):
        return pl.BlockSpec((None, 1, LANES), lambda b, p: (stack * 6 + 2 * g + p, 0, 0))

    return pl.pallas_call(
        functools.partial(_attn_prompt_kernel, S=S, dil=dil, n_keep=n_keep, head_base=4 * g),
        grid=(B, 2),
        in_specs=[col(COL_AQ), col(COL_AK), col(COL_AV), par(l), par(l),
                  pl.BlockSpec(memory_space=pltpu.SMEM)],
        out_specs=[
            pl.BlockSpec((None, S, LANES), lambda b, p: (b, 0, p)),
            pl.BlockSpec((None, S, LANES), lambda b, p: (b, 0, p)),
            pl.BlockSpec((None, n_keep, LANES), lambda b, p: (b, 0, p)),
            pl.BlockSpec((None, n_keep, LANES), lambda b, p: (b, 0, p)),
        ],
        out_shape=[
            jax.ShapeDtypeStruct((B, S, 256), F32),
            jax.ShapeDtypeStruct((B, S, 256), F32),
            jax.ShapeDtypeStruct((B, n_keep, 256), F32),
            jax.ShapeDtypeStruct((B, n_keep, 256), F32),
        ],
        scratch_shapes=[
            pltpu.VMEM((S, LANES), F32),
            pltpu.VMEM((S, LANES), BF16),
            pltpu.VMEM((S + dil * NK, LANES), BF16),
            pltpu.VMEM((S + dil * NK, LANES), BF16),
            pltpu.VMEM((S, LANES), F32),
            pltpu.VMEM((S, LANES), F32),
        ],
        compiler_params=_cparams(("parallel", "arbitrary")),
    )(P3, P3, P3, q_gain, k_gain, slopes)


T_PAD = 8


def _attn_sample_kernel(q_ref, k_ref, v_ref, qg_ref, kg_ref, sl_ref, cache_ref,
                        o_ref, lse_ref, cnew_ref, *, L, dil, t_true, head_base):
    lane = lax.broadcasted_iota(jnp.int32, (1, LANES), 1)
    head0 = lane < HEAD_DIM
    cnew_ref[0:4 * (L - t_true), :] = cache_ref[4 * t_true:4 * L, :]

    trow = lax.broadcasted_iota(jnp.int32, (T_PAD, 1), 0)
    scol = lax.broadcasted_iota(jnp.int32, (1, L), 1)
    dist = L + trow - scol
    valid_c = ((dist & (dil - 1)) == 0) & (dist <= NK * dil)
    alibi_c = dist.astype(F32)

    for p in range(2):
        ls = slice(p * LANES, (p + 1) * LANES)
        qn = _head_norm(q_ref[:, ls], qg_ref[:, ls], head0)
        kn = _head_norm(k_ref[:, ls], kg_ref[:, ls], head0)
        vn = v_ref[:, ls]
        for t in range(t_true):
            base = 4 * (L - t_true + t)
            cnew_ref[base + p:base + p + 1, :] = kn[t:t + 1, :]
            cnew_ref[base + 2 + p:base + 3 + p, :] = vn[t:t + 1, :]
        kc = cache_ref[pl.ds(p, L, stride=4), :].astype(BF16)
        vc = cache_ref[pl.ds(2 + p, L, stride=4), :].astype(BF16)
        knr = kn.astype(BF16).astype(F32)
        vnr = vn.astype(BF16).astype(F32)
        outs, lses = [], []
        for hh in range(2):
            hmask = head0 if hh == 0 else jnp.logical_not(head0)
            qm = jnp.where(hmask, qn, 0.0).astype(BF16)
            qmr = qm.astype(F32)
            slope = sl_ref[head_base + 2 * p + hh]
            sc = lax.dot_general(qm, kc, (((1,), (1,)), ((), ())), preferred_element_type=F32)
            sc = sc / float(np.sqrt(HEAD_DIM)) - slope * alibi_c
            sc = jnp.where(valid_c, sc, NEG)
            m = jnp.max(sc, axis=-1, keepdims=True)
            sn = []
            for u in range(t_true):
                du = trow - u
                su = jnp.sum(qmr * knr[u:u + 1, :], axis=-1, keepdims=True) / float(np.sqrt(HEAD_DIM))
                su = su - slope * du.astype(F32)
                su = jnp.where((du >= 0) & ((du & (dil - 1)) == 0), su, NEG)
                sn.append(su)
                m = jnp.maximum(m, su)
            pc = jnp.exp(sc - m)
            lsum = jnp.sum(pc, axis=-1, keepdims=True)
            acc = jnp.dot(pc.astype(BF16), vc, preferred_element_type=F32)
            for u in range(t_true):
                pu = jnp.exp(sn[u] - m)
                lsum = lsum + pu
                acc = acc + pu.astype(BF16).astype(F32) * vnr[u:u + 1, :]
            outs.append(acc / lsum)
            lses.append(m + jnp.log(lsum))
        o_ref[:, ls] = jnp.where(head0, outs[0], outs[1])
        lse_ref[:, ls] = jnp.where(head0, lses[0], lses[1])


def attn_sample(P3, q_gain, k_gain, slopes, cache, l, g, t_true):
    B = P3.shape[0]
    window, dil = ATTN_GROUPS[g]
    L = cache.shape[1] // 4

    def col(c):
        return pl.BlockSpec((None, T_PAD, 256), lambda b: (b, 0, c // 256 + g))

    def par(stack):
        return pl.BlockSpec((None, 1, 256), lambda b: (stack * 3 + g, 0, 0))

    return pl.pallas_call(
        functools.partial(_attn_sample_kernel, L=L, dil=dil, t_true=t_true, head_base=4 * g),
        grid=(B,),
        in_specs=[col(COL_AQ), col(COL_AK), col(COL_AV), par(l), par(l),
                  pl.BlockSpec(memory_space=pltpu.SMEM),
                  pl.BlockSpec((None, 4 * L, LANES), lambda b: (l * B + b, 0, 0))],
        out_specs=[
            pl.BlockSpec((None, T_PAD, 256), lambda b: (b, 0, 0)),
            pl.BlockSpec((None, T_PAD, 256), lambda b: (b, 0, 0)),
            pl.BlockSpec((None, 4 * L, LANES), lambda b: (b, 0, 0)),
        ],
        out_shape=[
            jax.ShapeDtypeStruct((B, T_PAD, 256), F32),
            jax.ShapeDtypeStruct((B, T_PAD, 256), F32),
            jax.ShapeDtypeStruct((B, 4 * L, LANES), F32),
        ],
        compiler_params=_cparams(("parallel",)),
    )(P3, P3, P3, q_gain, k_gain, slopes, cache)


def _attn_merge_kernel(o0, l0, o1, l1, o2, l2, y_ref):
    a0, a1, a2 = l0[...], l1[...], l2[...]
    m = jnp.maximum(jnp.maximum(a0, a1), a2)
    e0, e1, e2 = jnp.exp(a0 - m), jnp.exp(a1 - m), jnp.exp(a2 - m)
    den = e0 + e1 + e2
    y = (e0 / den) * o0[...] + (e1 / den) * o1[...] + (e2 / den) * o2[...]
    y_ref[...] = y.astype(BF16)


def attn_merge(parts, tm):
    M = parts[0].shape[0]
    spec = pl.BlockSpec((tm, 256), lambda m: (m, 0))
    return pl.pallas_call(
        _attn_merge_kernel,
        grid=(M // tm,),
        in_specs=[spec] * 6,
        out_specs=spec,
        out_shape=jax.ShapeDtypeStruct((M, 256), BF16),
        compiler_params=_cparams(("parallel",)),
    )(*parts)


def _merge_kernel(xn_ref, ya_ref, yb_ref, yc_ref, yd_ref, g0, g1, g2, g3, u0, u1, u2, u3, o_ref):
    xn = xn_ref[...]
    acc = None
    for y_ref, g_ref, u_ref in ((ya_ref, g0, u0), (yb_ref, g1, u1), (yc_ref, g2, u2), (yd_ref, g3, u3)):
        gate = jax.nn.sigmoid(jnp.dot(xn, g_ref[...], preferred_element_type=F32))
        term = gate * jnp.dot(y_ref[...], u_ref[...], preferred_element_type=F32)
        acc = term if acc is None else acc + term
    o_ref[...] = acc.astype(BF16)


def merge(xn, ya, yb, yc, yd, w_gate, ups, l, tm, tn):
    M = xn.shape[0]
    nblk = D_MODEL // tn

    def act(width):
        return pl.BlockSpec((tm, width), lambda m, n: (m, 0))

    def gate(b):
        return pl.BlockSpec((None, D_MODEL, tn), lambda m, n: (l, 0, b * nblk + n))

    def up(width):
        return pl.BlockSpec((None, width, tn), lambda m, n: (l, 0, n))

    return pl.pallas_call(
        _merge_kernel,
        grid=(M // tm, nblk),
        in_specs=[act(D_MODEL), act(512), act(256), act(512), act(512),
                  gate(0), gate(1), gate(2), gate(3), up(512), up(256), up(512), up(512)],
        out_specs=pl.BlockSpec((tm, tn), lambda m, n: (m, n)),
        out_shape=jax.ShapeDtypeStruct((M, D_MODEL), BF16),
        compiler_params=_cparams(("parallel", "arbitrary")),
    )(xn, ya, yb, yc, yd, w_gate, w_gate, w_gate, w_gate, *ups)


def _layer(x, W, l, tiles, mixers):
    tm = tiles
    h = ffn_up(x, W['n_ff1'], W['ff1_gate'], W['ff1_up'], l, tm, 512)
    x = matmul_res(x, h, W['ff1_down'], l, 0.5, min(tm, 512), 1024)
    P, xn = mix_in(x, W['n_mix'], W['w_mix'], l, tm, 512)
    (ya, yb, yc, yd), states = mixers(P, l)
    mg = merge(xn, ya, yb, yc, yd, W['w_gate'], (W['up_a'], W['up_b'], W['up_c'], W['up_d']), l, tm, 256)
    x = matmul_res(x, mg, W['w_out'], l, 1.0, tm, 1024)
    h = ffn_up(x, W['n_ff2'], W['ff2_gate'], W['ff2_up'], l, tm, 512)
    x = matmul_res(x, h, W['ff2_down'], l, 0.5, min(tm, 512), 1024)
    return x, states


def kernel(x_prompt, x_sample, state_conv, cache_w128_kv, cache_w512_kv, cache_w2048_kv, state_gla, state_pool,
           norm_ff1, ff1_gate, ff1_up, ff1_down, norm_mix, w_in, conv_w, attn_q_gain, attn_k_gain,
           gla_w_a2, gla_b_a, gla_norm, pool_w, pool_scale, w_up_conv, w_up_attn, w_up_gla, w_up_pool, w_out,
           norm_ff2, ff2_gate, ff2_up, ff2_down):
    BP, S, _ = x_prompt.shape
    BS, TS, _ = x_sample.shape
    caches = (cache_w128_kv, cache_w512_kv, cache_w2048_kv)

    def c(a, b):
        return w_in[:, :, a:b]

    w_mix = jnp.concatenate([
        c(0, 1536),
        c(4352, 4864), c(4864, 5376),
        c(5392, 5904),
        c(1536, 3840),
        c(3840, 4096), c(4096, 4352),
        c(5376, 5392),
        jnp.zeros((DEPTH, D_MODEL, LANES - 16), w_in.dtype),
    ], axis=-1).astype(BF16)
    W = {
        'n_ff1': norm_ff1.reshape(DEPTH, 1, D_MODEL), 'n_mix': norm_mix.reshape(DEPTH, 1, D_MODEL),
        'n_ff2': norm_ff2.reshape(DEPTH, 1, D_MODEL),
        'ff1_gate': ff1_gate.astype(BF16), 'ff1_up': ff1_up.astype(BF16), 'ff1_down': ff1_down.astype(BF16),
        'ff2_gate': ff2_gate.astype(BF16), 'ff2_up': ff2_up.astype(BF16), 'ff2_down': ff2_down.astype(BF16),
        'w_mix': w_mix, 'w_gate': w_in[:, :, 5904:].astype(BF16),
        'up_a': w_up_conv.astype(BF16), 'up_b': w_up_attn.astype(BF16),
        'up_c': w_up_gla.astype(BF16), 'up_d': w_up_pool.astype(BF16), 'w_out': w_out.astype(BF16),
    }
    wa_pad = jnp.pad(gla_w_a2, ((0, 0), (0, LANES - gla_w_a2.shape[1]), (0, 0))).astype(BF16)
    b_a = gla_b_a.reshape(DEPTH, 1, 256)
    gn = gla_norm.reshape(DEPTH, 1, LANES)
    pool_wb = pool_w.astype(BF16)
    pool_sc = pool_scale.reshape(DEPTH, 1, W_POOL)
    i = jnp.arange(1, N_ATTN_HEADS + 1, dtype=F32)
    slopes = jnp.exp2(-8.0 * i / N_ATTN_HEADS)
    qg6 = attn_q_gain.reshape(DEPTH * 6, 1, LANES)
    kg6 = attn_k_gain.reshape(DEPTH * 6, 1, LANES)
    qg3 = attn_q_gain.reshape(DEPTH * 3, 1, 256)
    kg3 = attn_k_gain.reshape(DEPTH * 3, 1, 256)

    zc = jnp.zeros((BP, CONV_HIST, W_CONV), F32)
    zp = jnp.zeros((BP, POOL_HIST16, W_POOL), F32)
    zs = jnp.zeros((BP, 256, LANES), F32)

    def prompt_mixers(P, l):
        P3 = P.reshape(BP, S, N_MIX)
        ya, yd, cnew, pnew = conv_pool(P3, zc, zp, conv_w, pool_wb, pool_sc, l, 512, 512, 0)
        yc, sfin = gla(P3, wa_pad, b_a, gn, zs, l, 512, GLA_CHUNK, GLA_CHUNK)
        parts, wins = [], []
        for g in range(3):
            o, lse, kc, vc = attn_prompt(P3, qg6, kg6, slopes, l, g)
            parts += [o.reshape(BP * S, 256), lse.reshape(BP * S, 256)]
            n_keep = kc.shape[1]
            wins.append(jnp.stack([kc, vc], axis=2).reshape(BP, n_keep, 2, 4, HEAD_DIM))
        yb = attn_merge(parts, 1024)
        M = BP * S
        ys = (ya.reshape(M, 512), yb, yc.reshape(M, 512), yd.reshape(M, 512))
        return ys, (cnew, wins, sfin.reshape(BP, 4, GLA_DK, LANES), pnew[:, 1:])

    sconv8 = jnp.pad(state_conv, ((0, 0), (0, 0), (CONV_HIST - 2, 0), (0, 0)))
    spool16 = jnp.pad(state_pool, ((0, 0), (0, 0), (1, 0), (0, 0)))
    sgla = state_gla.reshape(DEPTH, BS, 256, LANES)
    cviews = [cc.reshape(DEPTH * BS, 4 * cc.shape[2], LANES) for cc in caches]

    def sample_mixers(P, l):
        P3 = jnp.pad(P.reshape(BS, TS, N_MIX), ((0, 0), (0, T_PAD - TS), (0, 0)))
        ya, yd, cnew, pnew = conv_pool(P3, sconv8[l], spool16[l], conv_w, pool_wb, pool_sc, l, T_PAD, TS, PAST_LEN)
        yc, sfin = gla(P3, wa_pad, b_a, gn, sgla[l], l, T_PAD, T_PAD, TS)
        parts, wins = [], []
        for g in range(3):
            o, lse, cn = attn_sample(P3, qg3, kg3, slopes, cviews[g], l, g, TS)
            parts += [o[:, :TS].reshape(BS * TS, 256), lse[:, :TS].reshape(BS * TS, 256)]
            wins.append(cn.reshape(BS, cn.shape[1] // 4, 2, 4, HEAD_DIM))
        yb = attn_merge(parts, BS * TS)
        M = BS * TS
        ys = (ya[:, :TS].reshape(M, 512), yb, yc[:, :TS].reshape(M, 512), yd[:, :TS].reshape(M, 512))
        return ys, (cnew, wins, sfin.reshape(BS, 4, GLA_DK, LANES), pnew[:, 1:])

    yp = x_prompt.reshape(BP * S, D_MODEL)
    ys = x_sample.reshape(BS * TS, D_MODEL)
    st_p, st_s = [], []
    for l in range(DEPTH):
        yp, sp = _layer(yp, W, l, 1024, prompt_mixers)
        ys, ss = _layer(ys, W, l, BS * TS, sample_mixers)
        st_p.append(sp)
        st_s.append(ss)

    def stack(sts, f):
        return jnp.stack([f(s) for s in sts])

    outs = [yp.reshape(BP, S, D_MODEL), ys.reshape(BS, TS, D_MODEL),
            stack(st_p, lambda s: s[0]), stack(st_s, lambda s: s[0])]
    for g in range(3):
        outs += [stack(st_p, lambda s: s[1][g]), stack(st_s, lambda s: s[1][g])]
    outs += [stack(st_p, lambda s: s[2]), stack(st_s, lambda s: s[2]),
             stack(st_p, lambda s: s[3]), stack(st_s, lambda s: s[3])]
    return tuple(outs)
```

```python
import functools

import numpy as np
import jax
import jax.numpy as jnp
from jax import lax
from jax.experimental import pallas as pl
from jax.experimental.pallas import tpu as pltpu

BF16 = jnp.bfloat16
F32 = jnp.float32

D_MODEL = 2048
DEPTH = 4
PAST_LEN = 16384
D_FF = 5504
EPS = 1e-6
N_BRANCH = 4
W_CONV = 512
ATTN_GROUPS = ((128, 1), (512, 4), (2048, 16))
N_ATTN_HEADS = 12
HEAD_DIM = 64
GLA_DK = 64
GLA_TAU = 16.0
GLA_CHUNK = 64
POOL_WINDOWS = (2, 4, 8, 16)
W_POOL = 512
NK = 128
LANES = 128
NEG = -1e30

COL_CB, COL_CC, COL_CH, COL_GV, COL_GR, COL_POOL = 0, 512, 1024, 1536, 2048, 2560
COL_AQ, COL_AK, COL_AV = 3072, 3840, 4608
COL_GQ, COL_GK, COL_LR = 5376, 5632, 5888
N_MIX = 6144
N_IN = 14096
N_GATE = N_BRANCH * D_MODEL
PREP_ROWS = 256
MIX_TILE_SRC = ([256 * j for j in range(6)] + [4352, 4608, 4864, 5120, 5392, 5648]
                + [1536 + 256 * j for j in range(9)] + [3840, 4096, 5376])
GATE_TILE_SRC = [5904 + 256 * j for j in range(N_GATE // PREP_ROWS)]
GATE_ROW0 = N_MIX
SRC_ALIGN = 16
assert all(r % SRC_ALIGN == 0 for r in MIX_TILE_SRC + GATE_TILE_SRC)

VMEM_LIMIT = 56 * 1024 * 1024


def _cparams(sem):
    return pltpu.CompilerParams(dimension_semantics=sem, vmem_limit_bytes=VMEM_LIMIT)


def _rms(x):
    return x * lax.rsqrt(jnp.mean(x * x, axis=-1, keepdims=True) + EPS)


def _ffn_up_kernel(x_ref, g_ref, wg_ref, wu_ref, h_ref, xn_scr):
    @pl.when(pl.program_id(1) == 0)
    def _():
        xn_scr[...] = (_rms(x_ref[...]) * g_ref[...]).astype(BF16)

    xn = xn_scr[...]
    a = jnp.dot(xn, wg_ref[...], preferred_element_type=F32)
    b = jnp.dot(xn, wu_ref[...], preferred_element_type=F32)
    h_ref[...] = (a * jax.nn.sigmoid(a) * b).astype(BF16)


def ffn_up(x, gain, wg, wu, l, tm, tn):
    M = x.shape[0]
    return pl.pallas_call(
        _ffn_up_kernel,
        grid=(M // tm, pl.cdiv(D_FF, tn)),
        in_specs=[
            pl.BlockSpec((tm, D_MODEL), lambda m, n: (m, 0)),
            pl.BlockSpec((None, 1, D_MODEL), lambda m, n: (l, 0, 0)),
            pl.BlockSpec((None, D_MODEL, tn), lambda m, n: (l, 0, n)),
            pl.BlockSpec((None, D_MODEL, tn), lambda m, n: (l, 0, n)),
        ],
        out_specs=pl.BlockSpec((tm, tn), lambda m, n: (m, n)),
        out_shape=jax.ShapeDtypeStruct((M, D_FF), BF16),
        scratch_shapes=[pltpu.VMEM((tm, D_MODEL), BF16)],
        compiler_params=_cparams(("parallel", "arbitrary")),
    )(x, gain, wg, wu)


def _matmul_res_kernel(x_ref, a_ref, w_ref, o_ref, *, scale):
    o_ref[...] = x_ref[...] + scale * jnp.dot(a_ref[...], w_ref[...], preferred_element_type=F32)


def matmul_res(x, a, w, l, scale, tm, tn):
    M, K = a.shape
    N = x.shape[1]
    return pl.pallas_call(
        functools.partial(_matmul_res_kernel, scale=scale),
        grid=(M // tm, N // tn),
        in_specs=[
            pl.BlockSpec((tm, tn), lambda m, n: (m, n)),
            pl.BlockSpec((tm, K), lambda m, n: (m, 0)),
            pl.BlockSpec((None, K, tn), lambda m, n: (l, 0, n)),
        ],
        out_specs=pl.BlockSpec((tm, tn), lambda m, n: (m, n)),
        out_shape=jax.ShapeDtypeStruct((M, N), F32),
        compiler_params=_cparams(("parallel", "arbitrary")),
    )(x, a, w)


def _prep_kernel(src_ref, w_ref, o_ref):
    del src_ref
    o_ref[...] = w_ref[...].astype(BF16)


def prep_w_in(w_in_t):
    src = jnp.asarray(MIX_TILE_SRC + GATE_TILE_SRC, jnp.int32) // SRC_ALIGN
    nt = len(MIX_TILE_SRC) + len(GATE_TILE_SRC)
    return pl.pallas_call(
        _prep_kernel,
        grid_spec=pltpu.PrefetchScalarGridSpec(
            num_scalar_prefetch=1,
            grid=(DEPTH, nt),
            in_specs=[pl.BlockSpec((None, pl.Element(PREP_ROWS), pl.Element(D_MODEL)),
                                   lambda l, j, src: (l, src[j] * SRC_ALIGN, 0))],
            out_specs=pl.BlockSpec((None, PREP_ROWS, D_MODEL), lambda l, j, src: (l, j, 0)),
        ),
        out_shape=jax.ShapeDtypeStruct((DEPTH, nt * PREP_ROWS, D_MODEL), BF16),
        compiler_params=_cparams(("parallel", "arbitrary")),
    )(src, w_in_t)


def _dot_nt(a, b):
    return lax.dot_general(a, b, (((1,), (1,)), ((), ())), preferred_element_type=F32)


def _mix_in_kernel(x_ref, g_ref, w_ref, p_ref, xn_ref):
    @pl.when(pl.program_id(1) == 0)
    def _():
        xn_ref[...] = (_rms(x_ref[...]) * g_ref[...]).astype(BF16)

    p_ref[...] = _dot_nt(xn_ref[...], w_ref[...])


def mix_in(x, gain, w_t, l, tm, tn):
    M = x.shape[0]
    return pl.pallas_call(
        _mix_in_kernel,
        grid=(M // tm, N_MIX // tn),
        in_specs=[
            pl.BlockSpec((tm, D_MODEL), lambda m, n: (m, 0)),
            pl.BlockSpec((None, 1, D_MODEL), lambda m, n: (l, 0, 0)),
            pl.BlockSpec((None, tn, D_MODEL), lambda m, n: (l, n, 0)),
        ],
        out_specs=[
            pl.BlockSpec((tm, tn), lambda m, n: (m, n)),
            pl.BlockSpec((tm, D_MODEL), lambda m, n: (m, 0)),
        ],
        out_shape=[jax.ShapeDtypeStruct((M, N_MIX), F32), jax.ShapeDtypeStruct((M, D_MODEL), BF16)],
        compiler_params=_cparams(("parallel", "arbitrary")),
    )(x, gain, w_t)


CONV_HIST = 8
POOL_HIST16 = 16


def _conv_pool_kernel(cb_ref, cc_ref, ch_ref, pin_ref, cbuf_ref, pbuf_ref, cw_ref, pw_ref, ps_ref,
                      ya_ref, yd_ref, cnew_ref, pnew_ref, uext, pext, *, tt, t_true, pos0):
    t = pl.program_id(1)

    @pl.when(t == 0)
    def _():
        uext[0:CONV_HIST, :] = cbuf_ref[...]
        pext[0:POOL_HIST16, :] = pbuf_ref[...]

    u = cc_ref[...] * ch_ref[...]
    uext[CONV_HIST:CONV_HIST + tt, :] = u
    w = cw_ref[...]
    z = uext[CONV_HIST - 2:CONV_HIST - 2 + tt, :] * w[0:1, :]
    z = z + uext[CONV_HIST - 1:CONV_HIST - 1 + tt, :] * w[1:2, :]
    z = z + u * w[2:3, :]
    ya_ref[...] = (cb_ref[...] * z).astype(BF16)
    cnew_ref[...] = uext[CONV_HIST + t_true - 2:CONV_HIST + t_true, :]
    uext[0:CONV_HIST, :] = uext[tt:tt + CONV_HIST, :]

    pin = pin_ref[...]
    pext[POOL_HIST16:POOL_HIST16 + tt, :] = pin
    pos = (pos0 + t * tt + lax.broadcasted_iota(jnp.int32, (tt, 1), 0)).astype(F32)
    for g, win in enumerate(POOL_WINDOWS):
        ls = slice(g * LANES, (g + 1) * LANES)
        acc = pin[:, ls]
        for i in range(1, win):
            acc = acc + pext[POOL_HIST16 - i:POOL_HIST16 - i + tt, ls]
        cnt = jnp.minimum(float(win), pos + 1.0)
        d = acc / cnt - pin[:, ls]
        y = jnp.dot(d.astype(BF16), pw_ref[g], preferred_element_type=F32) * ps_ref[:, ls]
        yd_ref[:, ls] = y.astype(BF16)
    pnew_ref[...] = pext[t_true:t_true + POOL_HIST16, :]
    pext[0:POOL_HIST16, :] = pext[tt:tt + POOL_HIST16, :]


def conv_pool(P3, cbuf8, pbuf16, conv_w, pool_w, pool_scale, l, tt, t_true, pos0):
    B, T, _ = P3.shape

    def col(c):
        return pl.BlockSpec((None, tt, 512), lambda b, t: (b, t, c // 512))

    return pl.pallas_call(
        functools.partial(_conv_pool_kernel, tt=tt, t_true=t_true, pos0=pos0),
        grid=(B, T // tt),
        in_specs=[
            col(COL_CB), col(COL_CC), col(COL_CH), col(COL_POOL),
            pl.BlockSpec((None, CONV_HIST, 512), lambda b, t: (b, 0, 0)),
            pl.BlockSpec((None, POOL_HIST16, 512), lambda b, t: (b, 0, 0)),
            pl.BlockSpec((None, 3, 512), lambda b, t: (l, 0, 0)),
            pl.BlockSpec((None, 4, LANES, LANES), lambda b, t: (l, 0, 0, 0)),
            pl.BlockSpec((None, 1, 512), lambda b, t: (l, 0, 0)),
        ],
        out_specs=[
            pl.BlockSpec((None, tt, 512), lambda b, t: (b, t, 0)),
            pl.BlockSpec((None, tt, 512), lambda b, t: (b, t, 0)),
            pl.BlockSpec((None, 2, 512), lambda b, t: (b, 0, 0)),
            pl.BlockSpec((None, POOL_HIST16, 512), lambda b, t: (b, 0, 0)),
        ],
        out_shape=[
            jax.ShapeDtypeStruct((B, T, 512), BF16),
            jax.ShapeDtypeStruct((B, T, 512), BF16),
            jax.ShapeDtypeStruct((B, 2, 512), F32),
            jax.ShapeDtypeStruct((B, POOL_HIST16, 512), F32),
        ],
        scratch_shapes=[pltpu.VMEM((CONV_HIST + tt, 512), F32), pltpu.VMEM((POOL_HIST16 + tt, 512), F32)],
        compiler_params=_cparams(("parallel", "arbitrary")),
    )(P3, P3, P3, P3, cbuf8, pbuf16, conv_w, pool_w, pool_scale)


def _gla_kernel(q_ref, k_ref, v_ref, r_ref, lr_ref, wa_ref, ba_ref, gn_ref, s0_ref,
                yc_ref, sfin_ref, s_scr, *, tt, C, c_true):
    t = pl.program_id(1)

    @pl.when(t == 0)
    def _():
        s_scr[...] = s0_ref[...]

    row = lax.broadcasted_iota(jnp.int32, (C, 1), 0)
    colc = lax.broadcasted_iota(jnp.int32, (1, C), 1)
    tril = row >= colc
    tril_f = tril.astype(F32)
    lane = lax.broadcasted_iota(jnp.int32, (1, LANES), 1)
    head0 = lane < GLA_DK
    srow = lax.broadcasted_iota(jnp.int32, (LANES, 1), 0)
    eye = srow == lane
    gn = gn_ref[...]

    def chunk(c, carry):
        r0 = pl.multiple_of(c * C, C)
        rows = pl.ds(r0, C)
        q = q_ref[rows, :] * (GLA_DK ** -0.5)
        k = k_ref[rows, :]
        la = jnp.dot(lr_ref[rows, :].astype(BF16), wa_ref[...], preferred_element_type=F32) + ba_ref[...]
        la = jax.nn.log_sigmoid(la) / GLA_TAU
        if c_true < C:
            la = jnp.where(row < c_true, la, 0.0)
        b = jnp.dot(tril_f, la, preferred_element_type=F32, precision=lax.Precision.HIGHEST)
        b_last = b[C - 1:C, :]
        qt = q * jnp.exp(b)
        kt = k * jnp.exp(-b)
        kend = k * jnp.exp(b_last - b)
        for p in range(2):
            ls = slice(p * LANES, (p + 1) * LANES)
            qt_p = qt[:, ls]
            kt_p = kt[:, ls].astype(BF16)
            kend_p = kend[:, ls].astype(BF16)
            s_p = s_scr[ls, :]
            s_pb = s_p.astype(BF16)
            dec = jnp.exp(jnp.sum(jnp.where(eye, b_last[:, ls], 0.0), axis=1, keepdims=True))
            s_new = dec * s_p
            for hh in range(2):
                h = 2 * p + hh
                hs = slice(h * LANES, (h + 1) * LANES)
                hmask = head0 if hh == 0 else jnp.logical_not(head0)
                qm = jnp.where(hmask, qt_p, 0.0).astype(BF16)
                a = lax.dot_general(qm, kt_p, (((1,), (1,)), ((), ())), preferred_element_type=F32)
                a = jnp.where(tril, a, 0.0)
                v_h = v_ref[rows, hs]
                if c_true < C:
                    v_h = jnp.where(row < c_true, v_h, 0.0)
                v_hb = v_h.astype(BF16)
                o = jnp.dot(a.astype(BF16), v_hb, preferred_element_type=F32)
                o = o + jnp.dot(qm, s_pb, preferred_element_type=F32)
                ds = lax.dot_general(kend_p, v_hb, (((0,), (0,)), ((), ())), preferred_element_type=F32)
                rmask = (srow < GLA_DK) if hh == 0 else (srow >= GLA_DK)
                s_new = s_new + jnp.where(rmask, ds, 0.0)
                r_h = r_ref[rows, hs]
                yc_ref[rows, hs] = (_rms(o) * gn * (r_h * jax.nn.sigmoid(r_h))).astype(BF16)
            s_scr[ls, :] = s_new
        return carry

    lax.fori_loop(0, tt // C, chunk, 0)
    sfin_ref[...] = s_scr[...]


def gla(P3, wa_pad, b_a, gla_norm, s0, l, tt, C, c_true):
    B, T, _ = P3.shape
    return pl.pallas_call(
        functools.partial(_gla_kernel, tt=tt, C=C, c_true=c_true),
        grid=(B, T // tt),
        in_specs=[
            pl.BlockSpec((None, tt, 256), lambda b, t: (b, t, COL_GQ // 256)),
            pl.BlockSpec((None, tt, 256), lambda b, t: (b, t, COL_GK // 256)),
            pl.BlockSpec((None, tt, 512), lambda b, t: (b, t, COL_GV // 512)),
            pl.BlockSpec((None, tt, 512), lambda b, t: (b, t, COL_GR // 512)),
            pl.BlockSpec((None, tt, LANES), lambda b, t: (b, t, COL_LR // LANES)),
            pl.BlockSpec((None, LANES, 256), lambda b, t: (l, 0, 0)),
            pl.BlockSpec((None, 1, 256), lambda b, t: (l, 0, 0)),
            pl.BlockSpec((None, 1, LANES), lambda b, t: (l, 0, 0)),
            pl.BlockSpec((None, 256, LANES), lambda b, t: (b, 0, 0)),
        ],
        out_specs=[
            pl.BlockSpec((None, tt, 512), lambda b, t: (b, t, 0)),
            pl.BlockSpec((None, 256, LANES), lambda b, t: (b, 0, 0)),
        ],
        out_shape=[jax.ShapeDtypeStruct((B, T, 512), BF16), jax.ShapeDtypeStruct((B, 256, LANES), F32)],
        scratch_shapes=[pltpu.VMEM((256, LANES), F32)],
        compiler_params=_cparams(("parallel", "arbitrary")),
    )(P3, P3, P3, P3, P3, wa_pad, b_a, gla_norm, s0)


def _head_norm(x, gain, head0):
    x2 = x * x
    s0 = jnp.sum(jnp.where(head0, x2, 0.0), axis=-1, keepdims=True)
    s1 = jnp.sum(jnp.where(head0, 0.0, x2), axis=-1, keepdims=True)
    ms = jnp.where(head0, s0, s1) / float(HEAD_DIM)
    return x * lax.rsqrt(ms + EPS) * gain


NORM_ROWS = 512


def _attn_prompt_kernel(q_ref, k_ref, v_ref, qg_ref, kg_ref, sl_ref, *rest, S, dil, n_keep, head_base, chained):
    if chained:
        rest = rest[1:]
    o_ref, lse_ref, kv_ref, tmp, qd, kd, vd, od, ld = rest
    msub = S // dil
    nb = msub // NK
    kstride = msub + NK
    lane = lax.broadcasted_iota(jnp.int32, (1, LANES), 1)
    head0 = lane < HEAD_DIM

    def norm_into_tmp(src_ref, g_ref):
        def body(i, carry):
            rows = pl.ds(pl.multiple_of(i * NORM_ROWS, NORM_ROWS), NORM_ROWS)
            tmp[rows, :] = _head_norm(src_ref[rows, :], g_ref[...], head0)
            return carry
        lax.fori_loop(0, S // NORM_ROWS, body, 0)

    norm_into_tmp(q_ref, qg_ref)
    for r in range(dil):
        qd[r * msub:(r + 1) * msub, :] = tmp[pl.ds(r, msub, stride=dil), :].astype(BF16)
    norm_into_tmp(k_ref, kg_ref)
    kv_ref[0] = tmp[S - n_keep:S, :].T
    kv_ref[1] = v_ref[S - n_keep:S, :].T
    zeros = jnp.zeros((NK, LANES), BF16)
    for r in range(dil):
        kd[r * kstride:r * kstride + NK, :] = zeros
        vd[r * kstride:r * kstride + NK, :] = zeros
        kd[r * kstride + NK:(r + 1) * kstride, :] = tmp[pl.ds(r, msub, stride=dil), :].astype(BF16)
        vd[r * kstride + NK:(r + 1) * kstride, :] = v_ref[pl.ds(r, msub, stride=dil), :].astype(BF16)

    qi = lax.broadcasted_iota(jnp.int32, (NK, 2 * NK), 0)
    ki = lax.broadcasted_iota(jnp.int32, (NK, 2 * NK), 1)
    dist = qi - ki + NK
    band = (dist >= 0) & (dist <= NK)
    alibi = (dist * dil).astype(F32)
    cur = ki >= NK

    def block(j, carry):
        r = j // nb
        n = j - r * nb
        qb = pl.multiple_of(j * NK, NK)
        kb = pl.multiple_of(j * NK + r * NK, NK)
        valid = band & (cur | (n > 0))
        q2 = qd[pl.ds(qb, NK), :]
        k2 = kd[pl.ds(kb, 2 * NK), :]
        v2 = vd[pl.ds(kb, 2 * NK), :]
        outs, lses = [], []
        for hh in range(2):
            hmask = head0 if hh == 0 else jnp.logical_not(head0)
            qm = jnp.where(hmask, q2, jnp.zeros_like(q2))
            s = lax.dot_general(qm, k2, (((1,), (1,)), ((), ())), preferred_element_type=F32)
            slope = sl_ref[head_base + 2 * pl.program_id(1) + hh]
            s = s / float(np.sqrt(HEAD_DIM)) - slope * alibi
            s = jnp.where(valid, s, NEG)
            m = jnp.max(s, axis=-1, keepdims=True)
            p = jnp.exp(s - m)
            lsum = jnp.sum(p, axis=-1, keepdims=True)
            outs.append(jnp.dot(p.astype(BF16), v2, preferred_element_type=F32) / lsum)
            lses.append(m + jnp.log(lsum))
        od[pl.ds(qb, NK), :] = jnp.where(head0, outs[0], outs[1])
        ld[pl.ds(qb, NK), :] = jnp.where(head0, lses[0], lses[1])
        return carry

    lax.fori_loop(0, dil * nb, block, 0)

    for r in range(dil):
        o_ref[pl.ds(r, msub, stride=dil), :] = od[r * msub:(r + 1) * msub, :]
        lse_ref[pl.ds(r, msub, stride=dil), :] = ld[r * msub:(r + 1) * msub, :]


def attn_prompt(P3, q_gain, k_gain, slopes, l, g, win_prev):
    B, S, _ = P3.shape
    window, dil = ATTN_GROUPS[g]
    n_keep = min(window, S)
    chained = win_prev is not None

    def col(c):
        return pl.BlockSpec((None, S, LANES), lambda b, p: (b, 0, c // LANES + 2 * g + p))

    def par(stack):
        return pl.BlockSpec((None, 1, LANES), lambda b, p: (stack * 6 + 2 * g + p, 0, 0))

    in_specs = [col(COL_AQ), col(COL_AK), col(COL_AV), par(l), par(l), pl.BlockSpec(memory_space=pltpu.SMEM)]
    args = [P3, P3, P3, q_gain, k_gain, slopes]
    if chained:
        in_specs.append(pl.BlockSpec(memory_space=pl.ANY))
        args.append(win_prev)
    return pl.pallas_call(
        functools.partial(_attn_prompt_kernel, S=S, dil=dil, n_keep=n_keep, head_base=4 * g, chained=chained),
        grid=(B, 2),
        in_specs=in_specs,
        out_specs=[
            pl.BlockSpec((None, S, LANES), lambda b, p: (b, 0, p)),
            pl.BlockSpec((None, S, LANES), lambda b, p: (b, 0, p)),
            pl.BlockSpec((None, 2, LANES, n_keep), lambda b, p: (l * B + b, 0, p, 0)),
        ],
        out_shape=[
            jax.ShapeDtypeStruct((B, S, 256), F32),
            jax.ShapeDtypeStruct((B, S, 256), F32),
            jax.ShapeDtypeStruct((DEPTH * B, 2, 256, n_keep), F32),
        ],
        input_output_aliases={6: 2} if chained else {},
        scratch_shapes=[
            pltpu.VMEM((S, LANES), F32),
            pltpu.VMEM((S, LANES), BF16),
            pltpu.VMEM((S + dil * NK, LANES), BF16),
            pltpu.VMEM((S + dil * NK, LANES), BF16),
            pltpu.VMEM((S, LANES), F32),
            pltpu.VMEM((S, LANES), F32),
        ],
        compiler_params=_cparams(("parallel", "arbitrary")),
    )(*args)


T_PAD = 8


def _attn_sample_kernel(q_ref, k_ref, v_ref, qg_ref, kg_ref, sl_ref, cache_ref, *rest,
                        L, dil, t_true, head_base, chained):
    if chained:
        rest = rest[1:]
    o_ref, lse_ref, cnew_ref = rest
    lane = lax.broadcasted_iota(jnp.int32, (1, LANES), 1)
    head0 = lane < HEAD_DIM
    trow8 = lax.broadcasted_iota(jnp.int32, (T_PAD, LANES), 0)
    lane8 = lax.broadcasted_iota(jnp.int32, (T_PAD, LANES), 1)
    place = ((lane8 == LANES - t_true + trow8) & (trow8 < t_true)).astype(F32)
    is_new = lane >= LANES - t_true

    def rolled_with_new(rows, new):
        shifted = pltpu.roll(cache_ref[rows, :], L - t_true, 1)
        tail = lax.dot_general(new, place, (((0,), (0,)), ((), ())), preferred_element_type=F32,
                               precision=lax.Precision.HIGHEST)
        if L > LANES:
            cnew_ref[rows, 0:L - LANES] = shifted[:, 0:L - LANES]
        cnew_ref[rows, L - LANES:L] = jnp.where(is_new, tail, shifted[:, L - LANES:L])

    trow = lax.broadcasted_iota(jnp.int32, (T_PAD, 1), 0)
    scol = lax.broadcasted_iota(jnp.int32, (1, L), 1)
    dist = L + trow - scol
    valid_c = ((dist & (dil - 1)) == 0) & (dist <= NK * dil)
    alibi_c = dist.astype(F32)

    for p in range(2):
        ls = slice(p * LANES, (p + 1) * LANES)
        qn = _head_norm(q_ref[:, ls], qg_ref[:, ls], head0)
        kn = _head_norm(k_ref[:, ls], kg_ref[:, ls], head0)
        vn = v_ref[:, ls]
        krows = slice(p * LANES, (p + 1) * LANES)
        vrows = slice(256 + p * LANES, 256 + (p + 1) * LANES)
        rolled_with_new(krows, kn)
        rolled_with_new(vrows, vn)
        kc_t = cache_ref[krows, :].astype(BF16)
        vc_t = cache_ref[vrows, :].astype(BF16)
        knr = kn.astype(BF16).astype(F32)
        vnr = vn.astype(BF16).astype(F32)
        outs, lses = [], []
        for hh in range(2):
            hmask = head0 if hh == 0 else jnp.logical_not(head0)
            qm = jnp.where(hmask, qn, 0.0).astype(BF16)
            qmr = qm.astype(F32)
            slope = sl_ref[head_base + 2 * p + hh]
            sc = jnp.dot(qm, kc_t, preferred_element_type=F32)
            sc = sc / float(np.sqrt(HEAD_DIM)) - slope * alibi_c
            sc = jnp.where(valid_c, sc, NEG)
            m = jnp.max(sc, axis=-1, keepdims=True)
            sn = []
            for u in range(t_true):
                du = trow - u
                su = jnp.sum(qmr * knr[u:u + 1, :], axis=-1, keepdims=True) / float(np.sqrt(HEAD_DIM))
                su = su - slope * du.astype(F32)
                su = jnp.where((du >= 0) & ((du & (dil - 1)) == 0), su, NEG)
                sn.append(su)
                m = jnp.maximum(m, su)
            pc = jnp.exp(sc - m)
            lsum = jnp.sum(pc, axis=-1, keepdims=True)
            acc = _dot_nt(pc.astype(BF16), vc_t)
            for u in range(t_true):
                pu = jnp.exp(sn[u] - m)
                lsum = lsum + pu
                acc = acc + pu.astype(BF16).astype(F32) * vnr[u:u + 1, :]
            outs.append(acc / lsum)
            lses.append(m + jnp.log(lsum))
        o_ref[:, ls] = jnp.where(head0, outs[0], outs[1])
        lse_ref[:, ls] = jnp.where(head0, lses[0], lses[1])


def attn_sample(P3, q_gain, k_gain, slopes, cache_t, l, g, t_true, win_prev):
    B = P3.shape[0]
    window, dil = ATTN_GROUPS[g]
    L = cache_t.shape[2]
    chained = win_prev is not None

    def col(c):
        return pl.BlockSpec((None, T_PAD, 256), lambda b: (b, 0, c // 256 + g))

    def par(stack):
        return pl.BlockSpec((None, 1, 256), lambda b: (stack * 3 + g, 0, 0))

    in_specs = [col(COL_AQ), col(COL_AK), col(COL_AV), par(l), par(l), pl.BlockSpec(memory_space=pltpu.SMEM),
                pl.BlockSpec((None, 512, L), lambda b: (l * B + b, 0, 0))]
    args = [P3, P3, P3, q_gain, k_gain, slopes, cache_t]
    if chained:
        in_specs.append(pl.BlockSpec(memory_space=pl.ANY))
        args.append(win_prev)
    return pl.pallas_call(
        functools.partial(_attn_sample_kernel, L=L, dil=dil, t_true=t_true, head_base=4 * g, chained=chained),
        grid=(B,),
        in_specs=in_specs,
        out_specs=[
            pl.BlockSpec((None, T_PAD, 256), lambda b: (b, 0, 0)),
            pl.BlockSpec((None, T_PAD, 256), lambda b: (b, 0, 0)),
            pl.BlockSpec((None, 512, L), lambda b: (l * B + b, 0, 0)),
        ],
        out_shape=[
            jax.ShapeDtypeStruct((B, T_PAD, 256), F32),
            jax.ShapeDtypeStruct((B, T_PAD, 256), F32),
            jax.ShapeDtypeStruct((DEPTH * B, 512, L), F32),
        ],
        input_output_aliases={7: 2} if chained else {},
        compiler_params=_cparams(("parallel",)),
    )(*args)


def _attn_merge_kernel(o0, l0, o1, l1, o2, l2, y_ref):
    a0, a1, a2 = l0[...], l1[...], l2[...]
    m = jnp.maximum(jnp.maximum(a0, a1), a2)
    e0, e1, e2 = jnp.exp(a0 - m), jnp.exp(a1 - m), jnp.exp(a2 - m)
    den = e0 + e1 + e2
    y = (e0 / den) * o0[...] + (e1 / den) * o1[...] + (e2 / den) * o2[...]
    y_ref[...] = y.astype(BF16)


def attn_merge(parts, tm):
    M = parts[0].shape[0]
    spec = pl.BlockSpec((tm, 256), lambda m: (m, 0))
    return pl.pallas_call(
        _attn_merge_kernel,
        grid=(M // tm,),
        in_specs=[spec] * 6,
        out_specs=spec,
        out_shape=jax.ShapeDtypeStruct((M, 256), BF16),
        compiler_params=_cparams(("parallel",)),
    )(*parts)


def _merge_kernel(xn_ref, ya_ref, yb_ref, yc_ref, yd_ref, g0, g1, g2, g3, u0, u1, u2, u3, o_ref):
    xn = xn_ref[...]
    acc = None
    for y_ref, g_ref, u_ref in ((ya_ref, g0, u0), (yb_ref, g1, u1), (yc_ref, g2, u2), (yd_ref, g3, u3)):
        gate = jax.nn.sigmoid(_dot_nt(xn, g_ref[...]))
        term = gate * jnp.dot(y_ref[...], u_ref[...], preferred_element_type=F32)
        acc = term if acc is None else acc + term
    o_ref[...] = acc.astype(BF16)


def merge(xn, ya, yb, yc, yd, w_t, ups, l, tm, tn):
    M = xn.shape[0]
    nblk = D_MODEL // tn

    def act(width):
        return pl.BlockSpec((tm, width), lambda m, n: (m, 0))

    def gate(b):
        return pl.BlockSpec((None, tn, D_MODEL), lambda m, n: (l, GATE_ROW0 // tn + b * nblk + n, 0))

    def up(width):
        return pl.BlockSpec((None, width, tn), lambda m, n: (l, 0, n))

    return pl.pallas_call(
        _merge_kernel,
        grid=(M // tm, nblk),
        in_specs=[act(D_MODEL), act(512), act(256), act(512), act(512),
                  gate(0), gate(1), gate(2), gate(3), up(512), up(256), up(512), up(512)],
        out_specs=pl.BlockSpec((tm, tn), lambda m, n: (m, n)),
        out_shape=jax.ShapeDtypeStruct((M, D_MODEL), BF16),
        compiler_params=_cparams(("parallel", "arbitrary")),
    )(xn, ya, yb, yc, yd, w_t, w_t, w_t, w_t, *ups)


def _layer(x, W, l, tiles, mixers):
    tm = tiles
    h = ffn_up(x, W['n_ff1'], W['ff1_gate'], W['ff1_up'], l, tm, 512)
    x = matmul_res(x, h, W['ff1_down'], l, 0.5, min(tm, 512), 1024)
    P, xn = mix_in(x, W['n_mix'], W['w_t'], l, tm, 512)
    (ya, yb, yc, yd), states = mixers(P, l)
    mg = merge(xn, ya, yb, yc, yd, W['w_t'], (W['up_a'], W['up_b'], W['up_c'], W['up_d']), l, tm, 256)
    x = matmul_res(x, mg, W['w_out'], l, 1.0, tm, 1024)
    h = ffn_up(x, W['n_ff2'], W['ff2_gate'], W['ff2_up'], l, tm, 512)
    x = matmul_res(x, h, W['ff2_down'], l, 0.5, min(tm, 512), 1024)
    return x, states


def kernel(x_prompt, x_sample, state_conv, cache_w128_kv, cache_w512_kv, cache_w2048_kv, state_gla, state_pool,
           norm_ff1, ff1_gate, ff1_up, ff1_down, norm_mix, w_in, conv_w, attn_q_gain, attn_k_gain,
           gla_w_a2, gla_b_a, gla_norm, pool_w, pool_scale, w_up_conv, w_up_attn, w_up_gla, w_up_pool, w_out,
           norm_ff2, ff2_gate, ff2_up, ff2_down):
    BP, S, _ = x_prompt.shape
    BS, TS, _ = x_sample.shape
    caches = (cache_w128_kv, cache_w512_kv, cache_w2048_kv)

    W = {
        'n_ff1': norm_ff1.reshape(DEPTH, 1, D_MODEL), 'n_mix': norm_mix.reshape(DEPTH, 1, D_MODEL),
        'n_ff2': norm_ff2.reshape(DEPTH, 1, D_MODEL),
        'ff1_gate': ff1_gate.astype(BF16), 'ff1_up': ff1_up.astype(BF16), 'ff1_down': ff1_down.astype(BF16),
        'ff2_gate': ff2_gate.astype(BF16), 'ff2_up': ff2_up.astype(BF16), 'ff2_down': ff2_down.astype(BF16),
        'w_t': prep_w_in(jnp.transpose(w_in, (0, 2, 1))),
        'up_a': w_up_conv.astype(BF16), 'up_b': w_up_attn.astype(BF16),
        'up_c': w_up_gla.astype(BF16), 'up_d': w_up_pool.astype(BF16), 'w_out': w_out.astype(BF16),
    }
    wa_pad = jnp.pad(gla_w_a2, ((0, 0), (0, LANES - gla_w_a2.shape[1]), (0, 0))).astype(BF16)
    b_a = gla_b_a.reshape(DEPTH, 1, 256)
    gn = gla_norm.reshape(DEPTH, 1, LANES)
    pool_wb = pool_w.astype(BF16)
    pool_sc = pool_scale.reshape(DEPTH, 1, W_POOL)
    i = jnp.arange(1, N_ATTN_HEADS + 1, dtype=F32)
    slopes = jnp.exp2(-8.0 * i / N_ATTN_HEADS)
    qg6 = attn_q_gain.reshape(DEPTH * 6, 1, LANES)
    kg6 = attn_k_gain.reshape(DEPTH * 6, 1, LANES)
    qg3 = attn_q_gain.reshape(DEPTH * 3, 1, 256)
    kg3 = attn_k_gain.reshape(DEPTH * 3, 1, 256)

    zc = jnp.zeros((BP, CONV_HIST, W_CONV), F32)
    zp = jnp.zeros((BP, POOL_HIST16, W_POOL), F32)
    zs = jnp.zeros((BP, 256, LANES), F32)

    win_p = [None, None, None]
    win_s = [None, None, None]

    def prompt_mixers(P, l):
        P3 = P.reshape(BP, S, N_MIX)
        ya, yd, cnew, pnew = conv_pool(P3, zc, zp, conv_w, pool_wb, pool_sc, l, 512, 512, 0)
        yc, sfin = gla(P3, wa_pad, b_a, gn, zs, l, 512, GLA_CHUNK, GLA_CHUNK)
        parts = []
        for g in range(3):
            o, lse, win_p[g] = attn_prompt(P3, qg6, kg6, slopes, l, g, win_p[g])
            parts += [o.reshape(BP * S, 256), lse.reshape(BP * S, 256)]
        yb = attn_merge(parts, 1024)
        M = BP * S
        ys = (ya.reshape(M, 512), yb, yc.reshape(M, 512), yd.reshape(M, 512))
        return ys, (cnew, sfin.reshape(BP, 4, GLA_DK, LANES), pnew[:, 1:])

    sconv8 = jnp.pad(state_conv, ((0, 0), (0, 0), (CONV_HIST - 2, 0), (0, 0)))
    spool16 = jnp.pad(state_pool, ((0, 0), (0, 0), (1, 0), (0, 0)))
    sgla = state_gla.reshape(DEPTH, BS, 256, LANES)
    cviews = [jnp.transpose(cc, (0, 1, 3, 4, 5, 2)).reshape(DEPTH * BS, 512, cc.shape[2]) for cc in caches]

    def sample_mixers(P, l):
        P3 = jnp.pad(P.reshape(BS, TS, N_MIX), ((0, 0), (0, T_PAD - TS), (0, 0)))
        ya, yd, cnew, pnew = conv_pool(P3, sconv8[l], spool16[l], conv_w, pool_wb, pool_sc, l, T_PAD, TS, PAST_LEN)
        yc, sfin = gla(P3, wa_pad, b_a, gn, sgla[l], l, T_PAD, T_PAD, TS)
        parts = []
        for g in range(3):
            o, lse, win_s[g] = attn_sample(P3, qg3, kg3, slopes, cviews[g], l, g, TS, win_s[g])
            parts += [o[:, :TS].reshape(BS * TS, 256), lse[:, :TS].reshape(BS * TS, 256)]
        yb = attn_merge(parts, BS * TS)
        M = BS * TS
        ys = (ya[:, :TS].reshape(M, 512), yb, yc[:, :TS].reshape(M, 512), yd[:, :TS].reshape(M, 512))
        return ys, (cnew, sfin.reshape(BS, 4, GLA_DK, LANES), pnew[:, 1:])

    yp = x_prompt.reshape(BP * S, D_MODEL)
    ys = x_sample.reshape(BS * TS, D_MODEL)
    st_p, st_s = [], []
    for l in range(DEPTH):
        yp, sp = _layer(yp, W, l, 1024, prompt_mixers)
        ys, ss = _layer(ys, W, l, BS * TS, sample_mixers)
        st_p.append(sp)
        st_s.append(ss)

    def stack(sts, f):
        return jnp.stack([f(s) for s in sts])

    def window_out(buf, B):
        n = buf.shape[-1]
        return jnp.transpose(buf.reshape(DEPTH, B, 2, 4, HEAD_DIM, n), (0, 1, 5, 2, 3, 4))

    outs = [yp.reshape(BP, S, D_MODEL), ys.reshape(BS, TS, D_MODEL),
            stack(st_p, lambda s: s[0]), stack(st_s, lambda s: s[0])]
    for g in range(3):
        outs += [window_out(win_p[g], BP), window_out(win_s[g], BS)]
    outs += [stack(st_p, lambda s: s[1]), stack(st_s, lambda s: s[1]),
             stack(st_p, lambda s: s[2]), stack(st_s, lambda s: s[2])]
    return tuple(outs)
```

```python
import functools

import numpy as np
import jax
import jax.numpy as jnp
from jax import lax
from jax.experimental import pallas as pl
from jax.experimental.pallas import tpu as pltpu

BF16 = jnp.bfloat16
F32 = jnp.float32

D_MODEL = 2048
DEPTH = 4
PAST_LEN = 16384
D_FF = 5504
EPS = 1e-6
N_BRANCH = 4
W_CONV = 512
ATTN_GROUPS = ((128, 1), (512, 4), (2048, 16))
N_ATTN_HEADS = 12
HEAD_DIM = 64
GLA_DK = 64
GLA_TAU = 16.0
GLA_CHUNK = 64
POOL_WINDOWS = (2, 4, 8, 16)
W_POOL = 512
NK = 128
LANES = 128
NEG = -1e30

COL_CB, COL_CC, COL_CH, COL_GV, COL_GR, COL_POOL = 0, 512, 1024, 1536, 2048, 2560
COL_AQ, COL_AK, COL_AV = 3072, 3840, 4608
COL_GQ, COL_GK, COL_LR = 5376, 5632, 5888
N_MIX = 6144
N_IN = 14096
N_GATE = N_BRANCH * D_MODEL
PREP_ROWS = 256
MIX_TILE_SRC = ([256 * j for j in range(6)] + [4352, 4608, 4864, 5120, 5392, 5648]
                + [1536 + 256 * j for j in range(9)] + [3840, 4096, 5376])
GATE_TILE_SRC = [5904 + 256 * j for j in range(N_GATE // PREP_ROWS)]
GATE_ROW0 = N_MIX
SRC_ALIGN = 16
assert all(r % SRC_ALIGN == 0 for r in MIX_TILE_SRC + GATE_TILE_SRC)

VMEM_LIMIT = 56 * 1024 * 1024


def _cparams(sem):
    return pltpu.CompilerParams(dimension_semantics=sem, vmem_limit_bytes=VMEM_LIMIT)


def _rms(x):
    return x * lax.rsqrt(jnp.mean(x * x, axis=-1, keepdims=True) + EPS)


def _ffn_up_kernel(x_ref, g_ref, wg_ref, wu_ref, h_ref, xn_scr):
    @pl.when(pl.program_id(1) == 0)
    def _():
        xn_scr[...] = (_rms(x_ref[...]) * g_ref[...]).astype(BF16)

    xn = xn_scr[...]
    a = jnp.dot(xn, wg_ref[...], preferred_element_type=F32)
    b = jnp.dot(xn, wu_ref[...], preferred_element_type=F32)
    h_ref[...] = (a * jax.nn.sigmoid(a) * b).astype(BF16)


def ffn_up(x, gain, wg, wu, l, tm, tn):
    M = x.shape[0]
    return pl.pallas_call(
        _ffn_up_kernel,
        grid=(M // tm, pl.cdiv(D_FF, tn)),
        in_specs=[
            pl.BlockSpec((tm, D_MODEL), lambda m, n: (m, 0)),
            pl.BlockSpec((None, 1, D_MODEL), lambda m, n: (l, 0, 0)),
            pl.BlockSpec((None, D_MODEL, tn), lambda m, n: (l, 0, n)),
            pl.BlockSpec((None, D_MODEL, tn), lambda m, n: (l, 0, n)),
        ],
        out_specs=pl.BlockSpec((tm, tn), lambda m, n: (m, n)),
        out_shape=jax.ShapeDtypeStruct((M, D_FF), BF16),
        scratch_shapes=[pltpu.VMEM((tm, D_MODEL), BF16)],
        compiler_params=_cparams(("parallel", "arbitrary")),
    )(x, gain, wg, wu)


def _matmul_res_kernel(x_ref, a_ref, w_ref, o_ref, *, scale):
    o_ref[...] = x_ref[...] + scale * jnp.dot(a_ref[...], w_ref[...], preferred_element_type=F32)


def matmul_res(x, a, w, l, scale, tm, tn):
    M, K = a.shape
    N = x.shape[1]
    return pl.pallas_call(
        functools.partial(_matmul_res_kernel, scale=scale),
        grid=(M // tm, N // tn),
        in_specs=[
            pl.BlockSpec((tm, tn), lambda m, n: (m, n)),
            pl.BlockSpec((tm, K), lambda m, n: (m, 0)),
            pl.BlockSpec((None, K, tn), lambda m, n: (l, 0, n)),
        ],
        out_specs=pl.BlockSpec((tm, tn), lambda m, n: (m, n)),
        out_shape=jax.ShapeDtypeStruct((M, N), F32),
        compiler_params=_cparams(("parallel", "arbitrary")),
    )(x, a, w)


def _prep_kernel(src_ref, w_ref, o_ref):
    del src_ref
    o_ref[...] = w_ref[...].astype(BF16)


def prep_w_in(w_in_t):
    src = jnp.asarray(MIX_TILE_SRC + GATE_TILE_SRC, jnp.int32) // SRC_ALIGN
    nt = len(MIX_TILE_SRC) + len(GATE_TILE_SRC)
    return pl.pallas_call(
        _prep_kernel,
        grid_spec=pltpu.PrefetchScalarGridSpec(
            num_scalar_prefetch=1,
            grid=(DEPTH, nt),
            in_specs=[pl.BlockSpec((None, pl.Element(PREP_ROWS), pl.Element(D_MODEL)),
                                   lambda l, j, src: (l, src[j] * SRC_ALIGN, 0))],
            out_specs=pl.BlockSpec((None, PREP_ROWS, D_MODEL), lambda l, j, src: (l, j, 0)),
        ),
        out_shape=jax.ShapeDtypeStruct((DEPTH, nt * PREP_ROWS, D_MODEL), BF16),
        compiler_params=_cparams(("parallel", "arbitrary")),
    )(src, w_in_t)


def _dot_nt(a, b):
    return lax.dot_general(a, b, (((1,), (1,)), ((), ())), preferred_element_type=F32)


def _mix_in_kernel(x_ref, g_ref, w_ref, p_ref, xn_ref):
    @pl.when(pl.program_id(1) == 0)
    def _():
        xn_ref[...] = (_rms(x_ref[...]) * g_ref[...]).astype(BF16)

    p_ref[...] = _dot_nt(xn_ref[...], w_ref[...])


def mix_in(x, gain, w_t, l, tm, tn):
    M = x.shape[0]
    return pl.pallas_call(
        _mix_in_kernel,
        grid=(M // tm, N_MIX // tn),
        in_specs=[
            pl.BlockSpec((tm, D_MODEL), lambda m, n: (m, 0)),
            pl.BlockSpec((None, 1, D_MODEL), lambda m, n: (l, 0, 0)),
            pl.BlockSpec((None, tn, D_MODEL), lambda m, n: (l, n, 0)),
        ],
        out_specs=[
            pl.BlockSpec((tm, tn), lambda m, n: (m, n)),
            pl.BlockSpec((tm, D_MODEL), lambda m, n: (m, 0)),
        ],
        out_shape=[jax.ShapeDtypeStruct((M, N_MIX), F32), jax.ShapeDtypeStruct((M, D_MODEL), BF16)],
        compiler_params=_cparams(("parallel", "arbitrary")),
    )(x, gain, w_t)


CONV_HIST = 8
POOL_HIST16 = 16


def _conv_pool_kernel(cb_ref, cc_ref, ch_ref, pin_ref, cbuf_ref, pbuf_ref, cw_ref, pw_ref, ps_ref,
                      ya_ref, yd_ref, cnew_ref, pnew_ref, uext, pext, *, tt, t_true, pos0):
    t = pl.program_id(1)

    @pl.when(t == 0)
    def _():
        uext[0:CONV_HIST, :] = cbuf_ref[...]
        pext[0:POOL_HIST16, :] = pbuf_ref[...]

    u = cc_ref[...] * ch_ref[...]
    uext[CONV_HIST:CONV_HIST + tt, :] = u
    w = cw_ref[...]
    z = uext[CONV_HIST - 2:CONV_HIST - 2 + tt, :] * w[0:1, :]
    z = z + uext[CONV_HIST - 1:CONV_HIST - 1 + tt, :] * w[1:2, :]
    z = z + u * w[2:3, :]
    ya_ref[...] = (cb_ref[...] * z).astype(BF16)
    cnew_ref[...] = uext[CONV_HIST + t_true - 2:CONV_HIST + t_true, :]
    uext[0:CONV_HIST, :] = uext[tt:tt + CONV_HIST, :]

    pin = pin_ref[...]
    pext[POOL_HIST16:POOL_HIST16 + tt, :] = pin
    pos = (pos0 + t * tt + lax.broadcasted_iota(jnp.int32, (tt, 1), 0)).astype(F32)
    for g, win in enumerate(POOL_WINDOWS):
        ls = slice(g * LANES, (g + 1) * LANES)
        acc = pin[:, ls]
        for i in range(1, win):
            acc = acc + pext[POOL_HIST16 - i:POOL_HIST16 - i + tt, ls]
        cnt = jnp.minimum(float(win), pos + 1.0)
        d = acc / cnt - pin[:, ls]
        y = jnp.dot(d.astype(BF16), pw_ref[g], preferred_element_type=F32) * ps_ref[:, ls]
        yd_ref[:, ls] = y.astype(BF16)
    pnew_ref[...] = pext[t_true:t_true + POOL_HIST16, :]
    pext[0:POOL_HIST16, :] = pext[tt:tt + POOL_HIST16, :]


def conv_pool(P3, cbuf8, pbuf16, conv_w, pool_w, pool_scale, l, tt, t_true, pos0):
    B, T, _ = P3.shape

    def col(c):
        return pl.BlockSpec((None, tt, 512), lambda b, t: (b, t, c // 512))

    return pl.pallas_call(
        functools.partial(_conv_pool_kernel, tt=tt, t_true=t_true, pos0=pos0),
        grid=(B, T // tt),
        in_specs=[
            col(COL_CB), col(COL_CC), col(COL_CH), col(COL_POOL),
            pl.BlockSpec((None, CONV_HIST, 512), lambda b, t: (b, 0, 0)),
            pl.BlockSpec((None, POOL_HIST16, 512), lambda b, t: (b, 0, 0)),
            pl.BlockSpec((None, 3, 512), lambda b, t: (l, 0, 0)),
            pl.BlockSpec((None, 4, LANES, LANES), lambda b, t: (l, 0, 0, 0)),
            pl.BlockSpec((None, 1, 512), lambda b, t: (l, 0, 0)),
        ],
        out_specs=[
            pl.BlockSpec((None, tt, 512), lambda b, t: (b, t, 0)),
            pl.BlockSpec((None, tt, 512), lambda b, t: (b, t, 0)),
            pl.BlockSpec((None, 2, 512), lambda b, t: (b, 0, 0)),
            pl.BlockSpec((None, POOL_HIST16, 512), lambda b, t: (b, 0, 0)),
        ],
        out_shape=[
            jax.ShapeDtypeStruct((B, T, 512), BF16),
            jax.ShapeDtypeStruct((B, T, 512), BF16),
            jax.ShapeDtypeStruct((B, 2, 512), F32),
            jax.ShapeDtypeStruct((B, POOL_HIST16, 512), F32),
        ],
        scratch_shapes=[pltpu.VMEM((CONV_HIST + tt, 512), F32), pltpu.VMEM((POOL_HIST16 + tt, 512), F32)],
        compiler_params=_cparams(("parallel", "arbitrary")),
    )(P3, P3, P3, P3, cbuf8, pbuf16, conv_w, pool_w, pool_scale)


GLA_UNROLL = 4

def _gla_kernel(q_ref, k_ref, v_ref, r_ref, lr_ref, wa_ref, ba_ref, gn_ref, s0_ref,
                yc_ref, sfin_ref, s_scr, b_scr, sprev_scr, *, tt, C, c_true):
    assert c_true == C or tt == C
    t = pl.program_id(1)

    @pl.when(t == 0)
    def _():
        s_scr[...] = s0_ref[...]

    row = lax.broadcasted_iota(jnp.int32, (C, 1), 0)
    colc = lax.broadcasted_iota(jnp.int32, (1, C), 1)
    tril = row >= colc
    tril_f = tril.astype(F32)
    lane = lax.broadcasted_iota(jnp.int32, (1, LANES), 1)
    head0 = lane < GLA_DK
    srow = lax.broadcasted_iota(jnp.int32, (LANES, 1), 0)
    eye = srow == lane
    gn = gn_ref[...]
    nchunk = tt // C

    def chunk_rows(c):
        return pl.ds(pl.multiple_of(c * C, C), C)

    def v_head(rows, h):
        v_h = v_ref[rows, h * LANES:(h + 1) * LANES]
        if c_true < C:
            v_h = jnp.where(row < c_true, v_h, 0.0)
        return v_h.astype(BF16)

    la = jnp.dot(lr_ref[...].astype(BF16), wa_ref[...], preferred_element_type=F32) + ba_ref[...]
    la = jax.nn.log_sigmoid(la) / GLA_TAU
    if c_true < C:
        la = jnp.where(row < c_true, la, 0.0)
    b_scr[...] = la

    def scan(c, carry):
        rows = chunk_rows(c)
        b = jnp.dot(tril_f, b_scr[rows, :], preferred_element_type=F32, precision=lax.Precision.HIGHEST)
        b_scr[rows, :] = b
        b_last = b[C - 1:C, :]
        kend = (k_ref[rows, :] * jnp.exp(b_last - b)).astype(BF16)
        for p in range(2):
            ls = slice(p * LANES, (p + 1) * LANES)
            s_p = s_scr[ls, :]
            sprev_scr[c, ls, :] = s_p.astype(BF16)
            dec = jnp.exp(jnp.sum(jnp.where(eye, b_last[:, ls], 0.0), axis=1, keepdims=True))
            s_new = dec * s_p
            for hh in range(2):
                ds = lax.dot_general(kend[:, ls], v_head(rows, 2 * p + hh), (((0,), (0,)), ((), ())),
                                     preferred_element_type=F32)
                rmask = (srow < GLA_DK) if hh == 0 else (srow >= GLA_DK)
                s_new = s_new + jnp.where(rmask, ds, 0.0)
            s_scr[ls, :] = s_new
        return carry

    lax.fori_loop(0, nchunk, scan, 0, unroll=min(2, nchunk))

    def emit(c, carry):
        rows = chunk_rows(c)
        b = b_scr[rows, :]
        qt = q_ref[rows, :] * (GLA_DK ** -0.5) * jnp.exp(b)
        kt = (k_ref[rows, :] * jnp.exp(-b)).astype(BF16)
        for p in range(2):
            ls = slice(p * LANES, (p + 1) * LANES)
            s_pb = sprev_scr[c, ls, :]
            for hh in range(2):
                h = 2 * p + hh
                hs = slice(h * LANES, (h + 1) * LANES)
                hmask = head0 if hh == 0 else jnp.logical_not(head0)
                qm = jnp.where(hmask, qt[:, ls], 0.0).astype(BF16)
                a = jnp.where(tril, _dot_nt(qm, kt[:, ls]), 0.0)
                o = jnp.dot(a.astype(BF16), v_head(rows, h), preferred_element_type=F32)
                o = o + jnp.dot(qm, s_pb, preferred_element_type=F32)
                r_h = r_ref[rows, hs]
                yc_ref[rows, hs] = (_rms(o) * gn * (r_h * jax.nn.sigmoid(r_h))).astype(BF16)
        return carry

    lax.fori_loop(0, nchunk, emit, 0, unroll=min(GLA_UNROLL, nchunk))
    sfin_ref[...] = s_scr[...]


def gla(P3, wa_pad, b_a, gla_norm, s0, l, tt, C, c_true):
    B, T, _ = P3.shape
    return pl.pallas_call(
        functools.partial(_gla_kernel, tt=tt, C=C, c_true=c_true),
        grid=(B, T // tt),
        in_specs=[
            pl.BlockSpec((None, tt, 256), lambda b, t: (b, t, COL_GQ // 256)),
            pl.BlockSpec((None, tt, 256), lambda b, t: (b, t, COL_GK // 256)),
            pl.BlockSpec((None, tt, 512), lambda b, t: (b, t, COL_GV // 512)),
            pl.BlockSpec((None, tt, 512), lambda b, t: (b, t, COL_GR // 512)),
            pl.BlockSpec((None, tt, LANES), lambda b, t: (b, t, COL_LR // LANES)),
            pl.BlockSpec((None, LANES, 256), lambda b, t: (l, 0, 0)),
            pl.BlockSpec((None, 1, 256), lambda b, t: (l, 0, 0)),
            pl.BlockSpec((None, 1, LANES), lambda b, t: (l, 0, 0)),
            pl.BlockSpec((None, 256, LANES), lambda b, t: (b, 0, 0)),
        ],
        out_specs=[
            pl.BlockSpec((None, tt, 512), lambda b, t: (b, t, 0)),
            pl.BlockSpec((None, 256, LANES), lambda b, t: (b, 0, 0)),
        ],
        out_shape=[jax.ShapeDtypeStruct((B, T, 512), BF16), jax.ShapeDtypeStruct((B, 256, LANES), F32)],
        scratch_shapes=[pltpu.VMEM((256, LANES), F32), pltpu.VMEM((tt, 256), F32),
                        pltpu.VMEM((tt // C, 256, LANES), BF16)],
        compiler_params=_cparams(("parallel", "arbitrary")),
    )(P3, P3, P3, P3, P3, wa_pad, b_a, gla_norm, s0)


def _head_norm(x, gain, head0):
    x2 = x * x
    s0 = jnp.sum(jnp.where(head0, x2, 0.0), axis=-1, keepdims=True)
    s1 = jnp.sum(jnp.where(head0, 0.0, x2), axis=-1, keepdims=True)
    ms = jnp.where(head0, s0, s1) / float(HEAD_DIM)
    return x * lax.rsqrt(ms + EPS) * gain


NORM_ROWS = 512
ATTN_UNROLL = 4
QK_SCALE = 0.125


def _attn_prompt_kernel(q_ref, k_ref, v_ref, qg_ref, kg_ref, sl_ref, *rest, S, dil, n_keep, head_base, chained):
    if chained:
        rest = rest[1:]
    o_ref, lse_ref, kv_ref, tmp, qd, kd, vd, od, ld, bias = rest
    msub = S // dil
    nb = msub // NK
    kstride = msub + NK
    lane = lax.broadcasted_iota(jnp.int32, (1, LANES), 1)
    head0 = lane < HEAD_DIM

    def norm_into_tmp(src_ref, g_ref):
        def body(i, carry):
            rows = pl.ds(pl.multiple_of(i * NORM_ROWS, NORM_ROWS), NORM_ROWS)
            tmp[rows, :] = _head_norm(src_ref[rows, :], g_ref[...], head0)
            return carry
        lax.fori_loop(0, S // NORM_ROWS, body, 0)

    norm_into_tmp(q_ref, qg_ref)
    for r in range(dil):
        qd[r * msub:(r + 1) * msub, :] = (tmp[pl.ds(r, msub, stride=dil), :] * QK_SCALE).astype(BF16)
    norm_into_tmp(k_ref, kg_ref)
    kv_ref[0] = tmp[S - n_keep:S, :].T
    kv_ref[1] = v_ref[S - n_keep:S, :].T
    zeros = jnp.zeros((NK, LANES), BF16)
    for r in range(dil):
        kd[r * kstride:r * kstride + NK, :] = zeros
        vd[r * kstride:r * kstride + NK, :] = zeros
        kd[r * kstride + NK:(r + 1) * kstride, :] = tmp[pl.ds(r, msub, stride=dil), :].astype(BF16)
        vd[r * kstride + NK:(r + 1) * kstride, :] = v_ref[pl.ds(r, msub, stride=dil), :].astype(BF16)

    qi = lax.broadcasted_iota(jnp.int32, (NK, 2 * NK), 0)
    ki = lax.broadcasted_iota(jnp.int32, (NK, 2 * NK), 1)
    dist = qi - ki + NK
    band = (dist >= 0) & (dist <= NK)
    alibi = (dist * dil).astype(F32)
    cur = ki >= NK
    for hh in range(2):
        slope = sl_ref[head_base + 2 * pl.program_id(1) + hh]
        bias[2 * hh] = jnp.where(band, -(slope * alibi), NEG)
        bias[2 * hh + 1] = jnp.where(band & cur, -(slope * alibi), NEG)

    def block(j, carry):
        r = j // nb
        n = j - r * nb
        first = (n == 0).astype(jnp.int32)
        qb = pl.multiple_of(j * NK, NK)
        kb = pl.multiple_of(j * NK + r * NK, NK)
        q2 = qd[pl.ds(qb, NK), :]
        k2 = kd[pl.ds(kb, 2 * NK), :]
        v2 = vd[pl.ds(kb, 2 * NK), :]
        outs, lses = [], []
        for hh in range(2):
            hmask = head0 if hh == 0 else jnp.logical_not(head0)
            qm = jnp.where(hmask, q2, jnp.zeros_like(q2))
            s = lax.dot_general(qm, k2, (((1,), (1,)), ((), ())), preferred_element_type=F32)
            s = s + bias[2 * hh + first]
            m = jnp.max(s, axis=-1, keepdims=True)
            p = jnp.exp(s - m)
            lsum = jnp.sum(p, axis=-1, keepdims=True)
            outs.append(jnp.dot(p.astype(BF16), v2, preferred_element_type=F32) / lsum)
            lses.append(m + jnp.log(lsum))
        od[pl.ds(qb, NK), :] = jnp.where(head0, outs[0], outs[1])
        ld[pl.ds(qb, NK), :] = jnp.where(head0, lses[0], lses[1])
        return carry

    lax.fori_loop(0, dil * nb, block, 0, unroll=ATTN_UNROLL)

    for r in range(dil):
        o_ref[pl.ds(r, msub, stride=dil), :] = od[r * msub:(r + 1) * msub, :]
        lse_ref[pl.ds(r, msub, stride=dil), :] = ld[r * msub:(r + 1) * msub, :]


def attn_prompt(P3, q_gain, k_gain, slopes, l, g, win_prev):
    B, S, _ = P3.shape
    window, dil = ATTN_GROUPS[g]
    n_keep = min(window, S)
    chained = win_prev is not None

    def col(c):
        return pl.BlockSpec((None, S, LANES), lambda b, p: (b, 0, c // LANES + 2 * g + p))

    def par(stack):
        return pl.BlockSpec((None, 1, LANES), lambda b, p: (stack * 6 + 2 * g + p, 0, 0))

    in_specs = [col(COL_AQ), col(COL_AK), col(COL_AV), par(l), par(l), pl.BlockSpec(memory_space=pltpu.SMEM)]
    args = [P3, P3, P3, q_gain, k_gain, slopes]
    if chained:
        in_specs.append(pl.BlockSpec(memory_space=pl.ANY))
        args.append(win_prev)
    return pl.pallas_call(
        functools.partial(_attn_prompt_kernel, S=S, dil=dil, n_keep=n_keep, head_base=4 * g, chained=chained),
        grid=(B, 2),
        in_specs=in_specs,
        out_specs=[
            pl.BlockSpec((None, S, LANES), lambda b, p: (b, 0, p)),
            pl.BlockSpec((None, S, LANES), lambda b, p: (b, 0, p)),
            pl.BlockSpec((None, 2, LANES, n_keep), lambda b, p: (l * B + b, 0, p, 0)),
        ],
        out_shape=[
            jax.ShapeDtypeStruct((B, S, 256), F32),
            jax.ShapeDtypeStruct((B, S, 256), F32),
            jax.ShapeDtypeStruct((DEPTH * B, 2, 256, n_keep), F32),
        ],
        input_output_aliases={6: 2} if chained else {},
        scratch_shapes=[
            pltpu.VMEM((S, LANES), F32),
            pltpu.VMEM((S, LANES), BF16),
            pltpu.VMEM((S + dil * NK, LANES), BF16),
            pltpu.VMEM((S + dil * NK, LANES), BF16),
            pltpu.VMEM((S, LANES), F32),
            pltpu.VMEM((S, LANES), F32),
            pltpu.VMEM((4, NK, 2 * NK), F32),
        ],
        compiler_params=_cparams(("parallel", "arbitrary")),
    )(*args)


T_PAD = 8


def _attn_sample_kernel(q_ref, k_ref, v_ref, qg_ref, kg_ref, sl_ref, cache_ref, *rest,
                        L, dil, t_true, head_base, chained):
    if chained:
        rest = rest[1:]
    o_ref, lse_ref, cnew_ref = rest
    lane = lax.broadcasted_iota(jnp.int32, (1, LANES), 1)
    head0 = lane < HEAD_DIM
    trow8 = lax.broadcasted_iota(jnp.int32, (T_PAD, LANES), 0)
    lane8 = lax.broadcasted_iota(jnp.int32, (T_PAD, LANES), 1)
    place = ((lane8 == LANES - t_true + trow8) & (trow8 < t_true)).astype(F32)
    is_new = lane >= LANES - t_true

    def rolled_with_new(rows, new):
        shifted = pltpu.roll(cache_ref[rows, :], L - t_true, 1)
        tail = lax.dot_general(new, place, (((0,), (0,)), ((), ())), preferred_element_type=F32,
                               precision=lax.Precision.HIGHEST)
        if L > LANES:
            cnew_ref[rows, 0:L - LANES] = shifted[:, 0:L - LANES]
        cnew_ref[rows, L - LANES:L] = jnp.where(is_new, tail, shifted[:, L - LANES:L])

    trow = lax.broadcasted_iota(jnp.int32, (T_PAD, 1), 0)
    scol = lax.broadcasted_iota(jnp.int32, (1, L), 1)
    dist = L + trow - scol
    valid_c = ((dist & (dil - 1)) == 0) & (dist <= NK * dil)
    alibi_c = dist.astype(F32)

    for p in range(2):
        ls = slice(p * LANES, (p + 1) * LANES)
        qn = _head_norm(q_ref[:, ls], qg_ref[:, ls], head0)
        kn = _head_norm(k_ref[:, ls], kg_ref[:, ls], head0)
        vn = v_ref[:, ls]
        krows = slice(p * LANES, (p + 1) * LANES)
        vrows = slice(256 + p * LANES, 256 + (p + 1) * LANES)
        rolled_with_new(krows, kn)
        rolled_with_new(vrows, vn)
        kc_t = cache_ref[krows, :].astype(BF16)
        vc_t = cache_ref[vrows, :].astype(BF16)
        knr = kn.astype(BF16).astype(F32)
        vnr = vn.astype(BF16).astype(F32)
        outs, lses = [], []
        for hh in range(2):
            hmask = head0 if hh == 0 else jnp.logical_not(head0)
            qm = jnp.where(hmask, qn, 0.0).astype(BF16)
            qmr = qm.astype(F32)
            slope = sl_ref[head_base + 2 * p + hh]
            sc = jnp.dot(qm, kc_t, preferred_element_type=F32)
            sc = sc / float(np.sqrt(HEAD_DIM)) - slope * alibi_c
            sc = jnp.where(valid_c, sc, NEG)
            m = jnp.max(sc, axis=-1, keepdims=True)
            sn = []
            for u in range(t_true):
                du = trow - u
                su = jnp.sum(qmr * knr[u:u + 1, :], axis=-1, keepdims=True) / float(np.sqrt(HEAD_DIM))
                su = su - slope * du.astype(F32)
                su = jnp.where((du >= 0) & ((du & (dil - 1)) == 0), su, NEG)
                sn.append(su)
                m = jnp.maximum(m, su)
            pc = jnp.exp(sc - m)
            lsum = jnp.sum(pc, axis=-1, keepdims=True)
            acc = _dot_nt(pc.astype(BF16), vc_t)
            for u in range(t_true):
                pu = jnp.exp(sn[u] - m)
                lsum = lsum + pu
                acc = acc + pu.astype(BF16).astype(F32) * vnr[u:u + 1, :]
            outs.append(acc / lsum)
            lses.append(m + jnp.log(lsum))
        o_ref[:, ls] = jnp.where(head0, outs[0], outs[1])
        lse_ref[:, ls] = jnp.where(head0, lses[0], lses[1])


def attn_sample(P3, q_gain, k_gain, slopes, cache_t, l, g, t_true, win_prev):
    B = P3.shape[0]
    window, dil = ATTN_GROUPS[g]
    L = cache_t.shape[2]
    chained = win_prev is not None

    def col(c):
        return pl.BlockSpec((None, T_PAD, 256), lambda b: (b, 0, c // 256 + g))

    def par(stack):
        return pl.BlockSpec((None, 1, 256), lambda b: (stack * 3 + g, 0, 0))

    in_specs = [col(COL_AQ), col(COL_AK), col(COL_AV), par(l), par(l), pl.BlockSpec(memory_space=pltpu.SMEM),
                pl.BlockSpec((None, 512, L), lambda b: (l * B + b, 0, 0))]
    args = [P3, P3, P3, q_gain, k_gain, slopes, cache_t]
    if chained:
        in_specs.append(pl.BlockSpec(memory_space=pl.ANY))
        args.append(win_prev)
    return pl.pallas_call(
        functools.partial(_attn_sample_kernel, L=L, dil=dil, t_true=t_true, head_base=4 * g, chained=chained),
        grid=(B,),
        in_specs=in_specs,
        out_specs=[
            pl.BlockSpec((None, T_PAD, 256), lambda b: (b, 0, 0)),
            pl.BlockSpec((None, T_PAD, 256), lambda b: (b, 0, 0)),
            pl.BlockSpec((None, 512, L), lambda b: (l * B + b, 0, 0)),
        ],
        out_shape=[
            jax.ShapeDtypeStruct((B, T_PAD, 256), F32),
            jax.ShapeDtypeStruct((B, T_PAD, 256), F32),
            jax.ShapeDtypeStruct((DEPTH * B, 512, L), F32),
        ],
        input_output_aliases={7: 2} if chained else {},
        compiler_params=_cparams(("parallel",)),
    )(*args)


def _attn_merge_kernel(o0, l0, o1, l1, o2, l2, y_ref):
    a0, a1, a2 = l0[...], l1[...], l2[...]
    m = jnp.maximum(jnp.maximum(a0, a1), a2)
    e0, e1, e2 = jnp.exp(a0 - m), jnp.exp(a1 - m), jnp.exp(a2 - m)
    den = e0 + e1 + e2
    y = (e0 / den) * o0[...] + (e1 / den) * o1[...] + (e2 / den) * o2[...]
    y_ref[...] = y.astype(BF16)


def attn_merge(parts, tm):
    M = parts[0].shape[0]
    spec = pl.BlockSpec((tm, 256), lambda m: (m, 0))
    return pl.pallas_call(
        _attn_merge_kernel,
        grid=(M // tm,),
        in_specs=[spec] * 6,
        out_specs=spec,
        out_shape=jax.ShapeDtypeStruct((M, 256), BF16),
        compiler_params=_cparams(("parallel",)),
    )(*parts)


def _merge_kernel(xn_ref, ya_ref, yb_ref, yc_ref, yd_ref, g0, g1, g2, g3, u0, u1, u2, u3, o_ref):
    xn = xn_ref[...]
    acc = None
    for y_ref, g_ref, u_ref in ((ya_ref, g0, u0), (yb_ref, g1, u1), (yc_ref, g2, u2), (yd_ref, g3, u3)):
        gate = jax.nn.sigmoid(_dot_nt(xn, g_ref[...]))
        term = gate * jnp.dot(y_ref[...], u_ref[...], preferred_element_type=F32)
        acc = term if acc is None else acc + term
    o_ref[...] = acc.astype(BF16)


def merge(xn, ya, yb, yc, yd, w_t, ups, l, tm, tn):
    M = xn.shape[0]
    nblk = D_MODEL // tn

    def act(width):
        return pl.BlockSpec((tm, width), lambda m, n: (m, 0))

    def gate(b):
        return pl.BlockSpec((None, tn, D_MODEL), lambda m, n: (l, GATE_ROW0 // tn + b * nblk + n, 0))

    def up(width):
        return pl.BlockSpec((None, width, tn), lambda m, n: (l, 0, n))

    return pl.pallas_call(
        _merge_kernel,
        grid=(M // tm, nblk),
        in_specs=[act(D_MODEL), act(512), act(256), act(512), act(512),
                  gate(0), gate(1), gate(2), gate(3), up(512), up(256), up(512), up(512)],
        out_specs=pl.BlockSpec((tm, tn), lambda m, n: (m, n)),
        out_shape=jax.ShapeDtypeStruct((M, D_MODEL), BF16),
        compiler_params=_cparams(("parallel", "arbitrary")),
    )(xn, ya, yb, yc, yd, w_t, w_t, w_t, w_t, *ups)


def _layer(x, W, l, tiles, mixers):
    tm = tiles
    h = ffn_up(x, W['n_ff1'], W['ff1_gate'], W['ff1_up'], l, tm, 512)
    x = matmul_res(x, h, W['ff1_down'], l, 0.5, min(tm, 512), 1024)
    P, xn = mix_in(x, W['n_mix'], W['w_t'], l, tm, 512)
    (ya, yb, yc, yd), states = mixers(P, l)
    mg = merge(xn, ya, yb, yc, yd, W['w_t'], (W['up_a'], W['up_b'], W['up_c'], W['up_d']), l, tm, 256)
    x = matmul_res(x, mg, W['w_out'], l, 1.0, tm, 1024)
    h = ffn_up(x, W['n_ff2'], W['ff2_gate'], W['ff2_up'], l, tm, 512)
    x = matmul_res(x, h, W['ff2_down'], l, 0.5, min(tm, 512), 1024)
    return x, states


def kernel(x_prompt, x_sample, state_conv, cache_w128_kv, cache_w512_kv, cache_w2048_kv, state_gla, state_pool,
           norm_ff1, ff1_gate, ff1_up, ff1_down, norm_mix, w_in, conv_w, attn_q_gain, attn_k_gain,
           gla_w_a2, gla_b_a, gla_norm, pool_w, pool_scale, w_up_conv, w_up_attn, w_up_gla, w_up_pool, w_out,
           norm_ff2, ff2_gate, ff2_up, ff2_down):
    BP, S, _ = x_prompt.shape
    BS, TS, _ = x_sample.shape
    caches = (cache_w128_kv, cache_w512_kv, cache_w2048_kv)

    W = {
        'n_ff1': norm_ff1.reshape(DEPTH, 1, D_MODEL), 'n_mix': norm_mix.reshape(DEPTH, 1, D_MODEL),
        'n_ff2': norm_ff2.reshape(DEPTH, 1, D_MODEL),
        'ff1_gate': ff1_gate.astype(BF16), 'ff1_up': ff1_up.astype(BF16), 'ff1_down': ff1_down.astype(BF16),
        'ff2_gate': ff2_gate.astype(BF16), 'ff2_up': ff2_up.astype(BF16), 'ff2_down': ff2_down.astype(BF16),
        'w_t': prep_w_in(jnp.transpose(w_in, (0, 2, 1))),
        'up_a': w_up_conv.astype(BF16), 'up_b': w_up_attn.astype(BF16),
        'up_c': w_up_gla.astype(BF16), 'up_d': w_up_pool.astype(BF16), 'w_out': w_out.astype(BF16),
    }
    wa_pad = jnp.pad(gla_w_a2, ((0, 0), (0, LANES - gla_w_a2.shape[1]), (0, 0))).astype(BF16)
    b_a = gla_b_a.reshape(DEPTH, 1, 256)
    gn = gla_norm.reshape(DEPTH, 1, LANES)
    pool_wb = pool_w.astype(BF16)
    pool_sc = pool_scale.reshape(DEPTH, 1, W_POOL)
    i = jnp.arange(1, N_ATTN_HEADS + 1, dtype=F32)
    slopes = jnp.exp2(-8.0 * i / N_ATTN_HEADS)
    qg6 = attn_q_gain.reshape(DEPTH * 6, 1, LANES)
    kg6 = attn_k_gain.reshape(DEPTH * 6, 1, LANES)
    qg3 = attn_q_gain.reshape(DEPTH * 3, 1, 256)
    kg3 = attn_k_gain.reshape(DEPTH * 3, 1, 256)

    zc = jnp.zeros((BP, CONV_HIST, W_CONV), F32)
    zp = jnp.zeros((BP, POOL_HIST16, W_POOL), F32)
    zs = jnp.zeros((BP, 256, LANES), F32)

    win_p = [None, None, None]
    win_s = [None, None, None]

    def prompt_mixers(P, l):
        P3 = P.reshape(BP, S, N_MIX)
        ya, yd, cnew, pnew = conv_pool(P3, zc, zp, conv_w, pool_wb, pool_sc, l, 512, 512, 0)
        yc, sfin = gla(P3, wa_pad, b_a, gn, zs, l, 512, GLA_CHUNK, GLA_CHUNK)
        parts = []
        for g in range(3):
            o, lse, win_p[g] = attn_prompt(P3, qg6, kg6, slopes, l, g, win_p[g])
            parts += [o.reshape(BP * S, 256), lse.reshape(BP * S, 256)]
        yb = attn_merge(parts, 1024)
        M = BP * S
        ys = (ya.reshape(M, 512), yb, yc.reshape(M, 512), yd.reshape(M, 512))
        return ys, (cnew, sfin.reshape(BP, 4, GLA_DK, LANES), pnew[:, 1:])

    sconv8 = jnp.pad(state_conv, ((0, 0), (0, 0), (CONV_HIST - 2, 0), (0, 0)))
    spool16 = jnp.pad(state_pool, ((0, 0), (0, 0), (1, 0), (0, 0)))
    sgla = state_gla.reshape(DEPTH, BS, 256, LANES)
    cviews = [jnp.transpose(cc, (0, 1, 3, 4, 5, 2)).reshape(DEPTH * BS, 512, cc.shape[2]) for cc in caches]

    def sample_mixers(P, l):
        P3 = jnp.pad(P.reshape(BS, TS, N_MIX), ((0, 0), (0, T_PAD - TS), (0, 0)))
        ya, yd, cnew, pnew = conv_pool(P3, sconv8[l], spool16[l], conv_w, pool_wb, pool_sc, l, T_PAD, TS, PAST_LEN)
        yc, sfin = gla(P3, wa_pad, b_a, gn, sgla[l], l, T_PAD, T_PAD, TS)
        parts = []
        for g in range(3):
            o, lse, win_s[g] = attn_sample(P3, qg3, kg3, slopes, cviews[g], l, g, TS, win_s[g])
            parts += [o[:, :TS].reshape(BS * TS, 256), lse[:, :TS].reshape(BS * TS, 256)]
        yb = attn_merge(parts, BS * TS)
        M = BS * TS
        ys = (ya[:, :TS].reshape(M, 512), yb, yc[:, :TS].reshape(M, 512), yd[:, :TS].reshape(M, 512))
        return ys, (cnew, sfin.reshape(BS, 4, GLA_DK, LANES), pnew[:, 1:])

    yp = x_prompt.reshape(BP * S, D_MODEL)
    ys = x_sample.reshape(BS * TS, D_MODEL)
    st_p, st_s = [], []
    for l in range(DEPTH):
        yp, sp = _layer(yp, W, l, 1024, prompt_mixers)
        ys, ss = _layer(ys, W, l, BS * TS, sample_mixers)
        st_p.append(sp)
        st_s.append(ss)

    def stack(sts, f):
        return jnp.stack([f(s) for s in sts])

    def window_out(buf, B):
        n = buf.shape[-1]
        return jnp.transpose(buf.reshape(DEPTH, B, 2, 4, HEAD_DIM, n), (0, 1, 5, 2, 3, 4))

    outs = [yp.reshape(BP, S, D_MODEL), ys.reshape(BS, TS, D_MODEL),
            stack(st_p, lambda s: s[0]), stack(st_s, lambda s: s[0])]
    for g in range(3):
        outs += [window_out(win_p[g], BP), window_out(win_s[g], BS)]
    outs += [stack(st_p, lambda s: s[1]), stack(st_s, lambda s: s[1]),
             stack(st_p, lambda s: s[2]), stack(st_s, lambda s: s[2])]
    return tuple(outs)
```

```python
import functools

import numpy as np
import jax
import jax.numpy as jnp
from jax import lax
from jax.experimental import pallas as pl
from jax.experimental.pallas import tpu as pltpu

BF16 = jnp.bfloat16
F32 = jnp.float32

D_MODEL = 2048
DEPTH = 4
PAST_LEN = 16384
D_FF = 5504
EPS = 1e-6
N_BRANCH = 4
W_CONV = 512
ATTN_GROUPS = ((128, 1), (512, 4), (2048, 16))
N_ATTN_HEADS = 12
HEAD_DIM = 64
GLA_DK = 64
GLA_TAU = 16.0
GLA_CHUNK = 64
POOL_WINDOWS = (2, 4, 8, 16)
W_POOL = 512
NK = 128
LANES = 128
NEG = -1e30

COL_CB, COL_CC, COL_CH, COL_GV, COL_GR, COL_POOL = 0, 512, 1024, 1536, 2048, 2560
COL_AQ, COL_AK, COL_AV = 3072, 3840, 4608
COL_GQ, COL_GK, COL_LR = 5376, 5632, 5888
N_MIX = 6144
N_IN = 14096
N_GATE = N_BRANCH * D_MODEL
PREP_ROWS = 256
MIX_TILE_SRC = ([256 * j for j in range(6)] + [4352, 4608, 4864, 5120, 5392, 5648]
                + [1536 + 256 * j for j in range(9)] + [3840, 4096, 5376])
GATE_TILE_SRC = [5904 + 256 * j for j in range(N_GATE // PREP_ROWS)]
GATE_ROW0 = N_MIX
SRC_ALIGN = 16
assert all(r % SRC_ALIGN == 0 for r in MIX_TILE_SRC + GATE_TILE_SRC)

VMEM_LIMIT = 56 * 1024 * 1024


def _cparams(sem):
    return pltpu.CompilerParams(dimension_semantics=sem, vmem_limit_bytes=VMEM_LIMIT)


def _rms(x):
    return x * lax.rsqrt(jnp.mean(x * x, axis=-1, keepdims=True) + EPS)


def _norm_kernel(x_ref, g_ref, xn_ref):
    xn_ref[...] = (_rms(x_ref[...]) * g_ref[...]).astype(BF16)


def norm(x, gain, l, tm):
    M = x.shape[0]
    return pl.pallas_call(
        _norm_kernel,
        grid=(M // tm,),
        in_specs=[pl.BlockSpec((tm, D_MODEL), lambda m: (m, 0)),
                  pl.BlockSpec((None, 1, D_MODEL), lambda m: (l, 0, 0))],
        out_specs=pl.BlockSpec((tm, D_MODEL), lambda m: (m, 0)),
        out_shape=jax.ShapeDtypeStruct((M, D_MODEL), BF16),
        compiler_params=_cparams(("parallel",)),
    )(x, gain)


def _ffn_up_kernel(xn_ref, wg_ref, wu_ref, h_ref):
    xn = xn_ref[...]
    a = jnp.dot(xn, wg_ref[...], preferred_element_type=F32)
    b = jnp.dot(xn, wu_ref[...], preferred_element_type=F32)
    h_ref[...] = (a * jax.nn.sigmoid(a) * b).astype(BF16)


def ffn_up(xn, wg, wu, l, tm, tn):
    M = xn.shape[0]
    return pl.pallas_call(
        _ffn_up_kernel,
        grid=(M // tm, pl.cdiv(D_FF, tn)),
        in_specs=[
            pl.BlockSpec((tm, D_MODEL), lambda m, n: (m, 0)),
            pl.BlockSpec((None, D_MODEL, tn), lambda m, n: (l, 0, n)),
            pl.BlockSpec((None, D_MODEL, tn), lambda m, n: (l, 0, n)),
        ],
        out_specs=pl.BlockSpec((tm, tn), lambda m, n: (m, n)),
        out_shape=jax.ShapeDtypeStruct((M, D_FF), BF16),
        compiler_params=_cparams(("parallel", "arbitrary")),
    )(xn, wg, wu)


def _matmul_res_kernel(x_ref, a_ref, w_ref, *rest, scale, with_norm):
    y = x_ref[...] + scale * jnp.dot(a_ref[...], w_ref[...], preferred_element_type=F32)
    if with_norm:
        g_ref, o_ref, xn_ref = rest
        xn_ref[...] = (_rms(y) * g_ref[...]).astype(BF16)
    else:
        (o_ref,) = rest
    o_ref[...] = y


def matmul_res(x, a, w, l, scale, tm, next_gain=None, next_l=0):
    M, K = a.shape
    with_norm = next_gain is not None
    row = pl.BlockSpec((tm, D_MODEL), lambda m: (m, 0))
    in_specs = [row, pl.BlockSpec((tm, K), lambda m: (m, 0)),
                pl.BlockSpec((None, K, D_MODEL), lambda m: (l, 0, 0), pipeline_mode=pl.Buffered(1))]
    args = [x, a, w]
    out_specs, out_shape = [row], [jax.ShapeDtypeStruct((M, D_MODEL), F32)]
    if with_norm:
        in_specs.append(pl.BlockSpec((None, 1, D_MODEL), lambda m: (next_l, 0, 0)))
        args.append(next_gain)
        out_specs.append(row)
        out_shape.append(jax.ShapeDtypeStruct((M, D_MODEL), BF16))
    res = pl.pallas_call(
        functools.partial(_matmul_res_kernel, scale=scale, with_norm=with_norm),
        grid=(M // tm,),
        in_specs=in_specs,
        out_specs=out_specs,
        out_shape=out_shape,
        compiler_params=_cparams(("parallel",)),
    )(*args)
    return res if with_norm else (res[0], None)


def _prep_kernel(src_ref, w_ref, o_ref):
    del src_ref
    o_ref[...] = w_ref[...].astype(BF16)


def prep_w_in(w_in_t):
    src = jnp.asarray(MIX_TILE_SRC + GATE_TILE_SRC, jnp.int32) // SRC_ALIGN
    nt = len(MIX_TILE_SRC) + len(GATE_TILE_SRC)
    return pl.pallas_call(
        _prep_kernel,
        grid_spec=pltpu.PrefetchScalarGridSpec(
            num_scalar_prefetch=1,
            grid=(DEPTH, nt),
            in_specs=[pl.BlockSpec((None, pl.Element(PREP_ROWS), pl.Element(D_MODEL)),
                                   lambda l, j, src: (l, src[j] * SRC_ALIGN, 0))],
            out_specs=pl.BlockSpec((None, PREP_ROWS, D_MODEL), lambda l, j, src: (l, j, 0)),
        ),
        out_shape=jax.ShapeDtypeStruct((DEPTH, nt * PREP_ROWS, D_MODEL), BF16),
        compiler_params=_cparams(("parallel", "arbitrary")),
    )(src, w_in_t)


def _dot_nt(a, b):
    return lax.dot_general(a, b, (((1,), (1,)), ((), ())), preferred_element_type=F32)


def _mix_in_kernel(xn_ref, w_ref, p_ref):
    p_ref[...] = _dot_nt(xn_ref[...], w_ref[...])


def mix_in(xn, w_t, l, tm, tn):
    M = xn.shape[0]
    return pl.pallas_call(
        _mix_in_kernel,
        grid=(M // tm, N_MIX // tn),
        in_specs=[
            pl.BlockSpec((tm, D_MODEL), lambda m, n: (m, 0)),
            pl.BlockSpec((None, tn, D_MODEL), lambda m, n: (l, n, 0)),
        ],
        out_specs=pl.BlockSpec((tm, tn), lambda m, n: (m, n)),
        out_shape=jax.ShapeDtypeStruct((M, N_MIX), F32),
        compiler_params=_cparams(("parallel", "arbitrary")),
    )(xn, w_t)


CONV_HIST = 8
POOL_HIST16 = 16


def _conv_pool_kernel(cb_ref, cc_ref, ch_ref, pin_ref, cbuf_ref, pbuf_ref, cw_ref, pw_ref, ps_ref,
                      ya_ref, yd_ref, cnew_ref, pnew_ref, uext, pext, *, tt, t_true, pos0):
    t = pl.program_id(1)

    @pl.when(t == 0)
    def _():
        uext[0:CONV_HIST, :] = cbuf_ref[...]
        pext[0:POOL_HIST16, :] = pbuf_ref[...]

    u = cc_ref[...] * ch_ref[...]
    uext[CONV_HIST:CONV_HIST + tt, :] = u
    w = cw_ref[...]
    z = uext[CONV_HIST - 2:CONV_HIST - 2 + tt, :] * w[0:1, :]
    z = z + uext[CONV_HIST - 1:CONV_HIST - 1 + tt, :] * w[1:2, :]
    z = z + u * w[2:3, :]
    ya_ref[...] = (cb_ref[...] * z).astype(BF16)
    cnew_ref[...] = uext[CONV_HIST + t_true - 2:CONV_HIST + t_true, :]
    uext[0:CONV_HIST, :] = uext[tt:tt + CONV_HIST, :]

    pin = pin_ref[...]
    pext[POOL_HIST16:POOL_HIST16 + tt, :] = pin
    pos = (pos0 + t * tt + lax.broadcasted_iota(jnp.int32, (tt, 1), 0)).astype(F32)
    for g, win in enumerate(POOL_WINDOWS):
        ls = slice(g * LANES, (g + 1) * LANES)
        acc = pin[:, ls]
        for i in range(1, win):
            acc = acc + pext[POOL_HIST16 - i:POOL_HIST16 - i + tt, ls]
        cnt = jnp.minimum(float(win), pos + 1.0)
        d = acc / cnt - pin[:, ls]
        y = jnp.dot(d.astype(BF16), pw_ref[g], preferred_element_type=F32) * ps_ref[:, ls]
        yd_ref[:, ls] = y.astype(BF16)
    pnew_ref[...] = pext[t_true:t_true + POOL_HIST16, :]
    pext[0:POOL_HIST16, :] = pext[tt:tt + POOL_HIST16, :]


def conv_pool(P3, cbuf8, pbuf16, conv_w, pool_w, pool_scale, l, tt, t_true, pos0):
    B, T, _ = P3.shape

    def col(c):
        return pl.BlockSpec((None, tt, 512), lambda b, t: (b, t, c // 512))

    return pl.pallas_call(
        functools.partial(_conv_pool_kernel, tt=tt, t_true=t_true, pos0=pos0),
        grid=(B, T // tt),
        in_specs=[
            col(COL_CB), col(COL_CC), col(COL_CH), col(COL_POOL),
            pl.BlockSpec((None, CONV_HIST, 512), lambda b, t: (b, 0, 0)),
            pl.BlockSpec((None, POOL_HIST16, 512), lambda b, t: (b, 0, 0)),
            pl.BlockSpec((None, 3, 512), lambda b, t: (l, 0, 0)),
            pl.BlockSpec((None, 4, LANES, LANES), lambda b, t: (l, 0, 0, 0)),
            pl.BlockSpec((None, 1, 512), lambda b, t: (l, 0, 0)),
        ],
        out_specs=[
            pl.BlockSpec((None, tt, 512), lambda b, t: (b, t, 0)),
            pl.BlockSpec((None, tt, 512), lambda b, t: (b, t, 0)),
            pl.BlockSpec((None, 2, 512), lambda b, t: (b, 0, 0)),
            pl.BlockSpec((None, POOL_HIST16, 512), lambda b, t: (b, 0, 0)),
        ],
        out_shape=[
            jax.ShapeDtypeStruct((B, T, 512), BF16),
            jax.ShapeDtypeStruct((B, T, 512), BF16),
            jax.ShapeDtypeStruct((B, 2, 512), F32),
            jax.ShapeDtypeStruct((B, POOL_HIST16, 512), F32),
        ],
        scratch_shapes=[pltpu.VMEM((CONV_HIST + tt, 512), F32), pltpu.VMEM((POOL_HIST16 + tt, 512), F32)],
        compiler_params=_cparams(("parallel", "arbitrary")),
    )(P3, P3, P3, P3, cbuf8, pbuf16, conv_w, pool_w, pool_scale)


GLA_UNROLL = 4

def _gla_kernel(q_ref, k_ref, v_ref, r_ref, lr_ref, wa_ref, ba_ref, gn_ref, s0_ref,
                yc_ref, sfin_ref, s_scr, b_scr, sprev_scr, *, tt, C, c_true):
    assert c_true == C or tt == C
    t = pl.program_id(1)

    @pl.when(t == 0)
    def _():
        s_scr[...] = s0_ref[...]

    row = lax.broadcasted_iota(jnp.int32, (C, 1), 0)
    colc = lax.broadcasted_iota(jnp.int32, (1, C), 1)
    tril = row >= colc
    tril_f = tril.astype(F32)
    lane = lax.broadcasted_iota(jnp.int32, (1, LANES), 1)
    head0 = lane < GLA_DK
    srow = lax.broadcasted_iota(jnp.int32, (LANES, 1), 0)
    eye = srow == lane
    gn = gn_ref[...]
    nchunk = tt // C

    def chunk_rows(c):
        return pl.ds(pl.multiple_of(c * C, C), C)

    def v_head(rows, h):
        v_h = v_ref[rows, h * LANES:(h + 1) * LANES]
        if c_true < C:
            v_h = jnp.where(row < c_true, v_h, 0.0)
        return v_h.astype(BF16)

    la = jnp.dot(lr_ref[...].astype(BF16), wa_ref[...], preferred_element_type=F32) + ba_ref[...]
    la = jax.nn.log_sigmoid(la) / GLA_TAU
    if c_true < C:
        la = jnp.where(row < c_true, la, 0.0)
    b_scr[...] = la

    def scan(c, carry):
        rows = chunk_rows(c)
        b = jnp.dot(tril_f, b_scr[rows, :], preferred_element_type=F32, precision=lax.Precision.HIGHEST)
        b_scr[rows, :] = b
        b_last = b[C - 1:C, :]
        kend = (k_ref[rows, :] * jnp.exp(b_last - b)).astype(BF16)
        for p in range(2):
            ls = slice(p * LANES, (p + 1) * LANES)
            s_p = s_scr[ls, :]
            sprev_scr[c, ls, :] = s_p.astype(BF16)
            dec = jnp.exp(jnp.sum(jnp.where(eye, b_last[:, ls], 0.0), axis=1, keepdims=True))
            s_new = dec * s_p
            for hh in range(2):
                ds = lax.dot_general(kend[:, ls], v_head(rows, 2 * p + hh), (((0,), (0,)), ((), ())),
                                     preferred_element_type=F32)
                rmask = (srow < GLA_DK) if hh == 0 else (srow >= GLA_DK)
                s_new = s_new + jnp.where(rmask, ds, 0.0)
            s_scr[ls, :] = s_new
        return carry

    lax.fori_loop(0, nchunk, scan, 0, unroll=min(2, nchunk))

    def emit(c, carry):
        rows = chunk_rows(c)
        b = b_scr[rows, :]
        qt = q_ref[rows, :] * (GLA_DK ** -0.5) * jnp.exp(b)
        kt = (k_ref[rows, :] * jnp.exp(-b)).astype(BF16)
        for p in range(2):
            ls = slice(p * LANES, (p + 1) * LANES)
            s_pb = sprev_scr[c, ls, :]
            for hh in range(2):
                h = 2 * p + hh
                hs = slice(h * LANES, (h + 1) * LANES)
                hmask = head0 if hh == 0 else jnp.logical_not(head0)
                qm = jnp.where(hmask, qt[:, ls], 0.0).astype(BF16)
                a = jnp.where(tril, _dot_nt(qm, kt[:, ls]), 0.0)
                o = jnp.dot(a.astype(BF16), v_head(rows, h), preferred_element_type=F32)
                o = o + jnp.dot(qm, s_pb, preferred_element_type=F32)
                r_h = r_ref[rows, hs]
                yc_ref[rows, hs] = (_rms(o) * gn * (r_h * jax.nn.sigmoid(r_h))).astype(BF16)
        return carry

    lax.fori_loop(0, nchunk, emit, 0, unroll=min(GLA_UNROLL, nchunk))
    sfin_ref[...] = s_scr[...]


def gla(P3, wa_pad, b_a, gla_norm, s0, l, tt, C, c_true):
    B, T, _ = P3.shape
    return pl.pallas_call(
        functools.partial(_gla_kernel, tt=tt, C=C, c_true=c_true),
        grid=(B, T // tt),
        in_specs=[
            pl.BlockSpec((None, tt, 256), lambda b, t: (b, t, COL_GQ // 256)),
            pl.BlockSpec((None, tt, 256), lambda b, t: (b, t, COL_GK // 256)),
            pl.BlockSpec((None, tt, 512), lambda b, t: (b, t, COL_GV // 512)),
            pl.BlockSpec((None, tt, 512), lambda b, t: (b, t, COL_GR // 512)),
            pl.BlockSpec((None, tt, LANES), lambda b, t: (b, t, COL_LR // LANES)),
            pl.BlockSpec((None, LANES, 256), lambda b, t: (l, 0, 0)),
            pl.BlockSpec((None, 1, 256), lambda b, t: (l, 0, 0)),
            pl.BlockSpec((None, 1, LANES), lambda b, t: (l, 0, 0)),
            pl.BlockSpec((None, 256, LANES), lambda b, t: (b, 0, 0)),
        ],
        out_specs=[
            pl.BlockSpec((None, tt, 512), lambda b, t: (b, t, 0)),
            pl.BlockSpec((None, 256, LANES), lambda b, t: (b, 0, 0)),
        ],
        out_shape=[jax.ShapeDtypeStruct((B, T, 512), BF16), jax.ShapeDtypeStruct((B, 256, LANES), F32)],
        scratch_shapes=[pltpu.VMEM((256, LANES), F32), pltpu.VMEM((tt, 256), F32),
                        pltpu.VMEM((tt // C, 256, LANES), BF16)],
        compiler_params=_cparams(("parallel", "arbitrary")),
    )(P3, P3, P3, P3, P3, wa_pad, b_a, gla_norm, s0)


def _head_norm(x, gain, head0):
    x2 = x * x
    s0 = jnp.sum(jnp.where(head0, x2, 0.0), axis=-1, keepdims=True)
    s1 = jnp.sum(jnp.where(head0, 0.0, x2), axis=-1, keepdims=True)
    ms = jnp.where(head0, s0, s1) / float(HEAD_DIM)
    return x * lax.rsqrt(ms + EPS) * gain


NORM_ROWS = 512
ATTN_UNROLL = 4
QK_SCALE = 0.125


def _attn_prompt_kernel(q_ref, k_ref, v_ref, qg_ref, kg_ref, sl_ref, *rest, S, dil, n_keep, head_base, chained):
    if chained:
        rest = rest[1:]
    o_ref, lse_ref, kv_ref, tmp, qd, kd, vd, od, ld, bias = rest
    msub = S // dil
    nb = msub // NK
    kstride = msub + NK
    lane = lax.broadcasted_iota(jnp.int32, (1, LANES), 1)
    head0 = lane < HEAD_DIM

    def norm_into_tmp(src_ref, g_ref):
        def body(i, carry):
            rows = pl.ds(pl.multiple_of(i * NORM_ROWS, NORM_ROWS), NORM_ROWS)
            tmp[rows, :] = _head_norm(src_ref[rows, :], g_ref[...], head0)
            return carry
        lax.fori_loop(0, S // NORM_ROWS, body, 0)

    norm_into_tmp(q_ref, qg_ref)
    for r in range(dil):
        qd[r * msub:(r + 1) * msub, :] = (tmp[pl.ds(r, msub, stride=dil), :] * QK_SCALE).astype(BF16)
    norm_into_tmp(k_ref, kg_ref)
    kv_ref[0] = tmp[S - n_keep:S, :].T
    kv_ref[1] = v_ref[S - n_keep:S, :].T
    zeros = jnp.zeros((NK, LANES), BF16)
    for r in range(dil):
        kd[r * kstride:r * kstride + NK, :] = zeros
        vd[r * kstride:r * kstride + NK, :] = zeros
        kd[r * kstride + NK:(r + 1) * kstride, :] = tmp[pl.ds(r, msub, stride=dil), :].astype(BF16)
        vd[r * kstride + NK:(r + 1) * kstride, :] = v_ref[pl.ds(r, msub, stride=dil), :].astype(BF16)

    qi = lax.broadcasted_iota(jnp.int32, (NK, 2 * NK), 0)
    ki = lax.broadcasted_iota(jnp.int32, (NK, 2 * NK), 1)
    dist = qi - ki + NK
    band = (dist >= 0) & (dist <= NK)
    alibi = (dist * dil).astype(F32)
    cur = ki >= NK
    for hh in range(2):
        slope = sl_ref[head_base + 2 * pl.program_id(1) + hh]
        bias[2 * hh] = jnp.where(band, -(slope * alibi), NEG)
        bias[2 * hh + 1] = jnp.where(band & cur, -(slope * alibi), NEG)

    def block(j, carry):
        r = j // nb
        n = j - r * nb
        first = (n == 0).astype(jnp.int32)
        qb = pl.multiple_of(j * NK, NK)
        kb = pl.multiple_of(j * NK + r * NK, NK)
        q2 = qd[pl.ds(qb, NK), :]
        k2 = kd[pl.ds(kb, 2 * NK), :]
        v2 = vd[pl.ds(kb, 2 * NK), :]
        outs, lses = [], []
        for hh in range(2):
            hmask = head0 if hh == 0 else jnp.logical_not(head0)
            qm = jnp.where(hmask, q2, jnp.zeros_like(q2))
            s = lax.dot_general(qm, k2, (((1,), (1,)), ((), ())), preferred_element_type=F32)
            s = s + bias[2 * hh + first]
            m = jnp.max(s, axis=-1, keepdims=True)
            p = jnp.exp(s - m)
            lsum = jnp.sum(p, axis=-1, keepdims=True)
            outs.append(jnp.dot(p.astype(BF16), v2, preferred_element_type=F32) / lsum)
            lses.append(m + jnp.log(lsum))
        od[pl.ds(qb, NK), :] = jnp.where(head0, outs[0], outs[1])
        ld[pl.ds(qb, NK), :] = jnp.where(head0, lses[0], lses[1])
        return carry

    lax.fori_loop(0, dil * nb, block, 0, unroll=ATTN_UNROLL)

    for r in range(dil):
        o_ref[pl.ds(r, msub, stride=dil), :] = od[r * msub:(r + 1) * msub, :]
        lse_ref[pl.ds(r, msub, stride=dil), :] = ld[r * msub:(r + 1) * msub, :]


def attn_prompt(P3, q_gain, k_gain, slopes, l, g, win_prev):
    B, S, _ = P3.shape
    window, dil = ATTN_GROUPS[g]
    n_keep = min(window, S)
    chained = win_prev is not None

    def col(c):
        return pl.BlockSpec((None, S, LANES), lambda b, p: (b, 0, c // LANES + 2 * g + p))

    def par(stack):
        return pl.BlockSpec((None, 1, LANES), lambda b, p: (stack * 6 + 2 * g + p, 0, 0))

    in_specs = [col(COL_AQ), col(COL_AK), col(COL_AV), par(l), par(l), pl.BlockSpec(memory_space=pltpu.SMEM)]
    args = [P3, P3, P3, q_gain, k_gain, slopes]
    if chained:
        in_specs.append(pl.BlockSpec(memory_space=pl.ANY))
        args.append(win_prev)
    return pl.pallas_call(
        functools.partial(_attn_prompt_kernel, S=S, dil=dil, n_keep=n_keep, head_base=4 * g, chained=chained),
        grid=(B, 2),
        in_specs=in_specs,
        out_specs=[
            pl.BlockSpec((None, S, LANES), lambda b, p: (b, 0, p)),
            pl.BlockSpec((None, S, LANES), lambda b, p: (b, 0, p)),
            pl.BlockSpec((None, 2, LANES, n_keep), lambda b, p: (l * B + b, 0, p, 0)),
        ],
        out_shape=[
            jax.ShapeDtypeStruct((B, S, 256), F32),
            jax.ShapeDtypeStruct((B, S, 256), F32),
            jax.ShapeDtypeStruct((DEPTH * B, 2, 256, n_keep), F32),
        ],
        input_output_aliases={6: 2} if chained else {},
        scratch_shapes=[
            pltpu.VMEM((S, LANES), F32),
            pltpu.VMEM((S, LANES), BF16),
            pltpu.VMEM((S + dil * NK, LANES), BF16),
            pltpu.VMEM((S + dil * NK, LANES), BF16),
            pltpu.VMEM((S, LANES), F32),
            pltpu.VMEM((S, LANES), F32),
            pltpu.VMEM((4, NK, 2 * NK), F32),
        ],
        compiler_params=_cparams(("parallel", "arbitrary")),
    )(*args)


T_PAD = 8


def _attn_sample_kernel(q_ref, k_ref, v_ref, qg_ref, kg_ref, sl_ref, cache_ref, *rest,
                        L, dil, t_true, head_base, chained):
    if chained:
        rest = rest[1:]
    o_ref, lse_ref, cnew_ref = rest
    lane = lax.broadcasted_iota(jnp.int32, (1, LANES), 1)
    head0 = lane < HEAD_DIM
    trow8 = lax.broadcasted_iota(jnp.int32, (T_PAD, LANES), 0)
    lane8 = lax.broadcasted_iota(jnp.int32, (T_PAD, LANES), 1)
    place = ((lane8 == LANES - t_true + trow8) & (trow8 < t_true)).astype(F32)
    is_new = lane >= LANES - t_true

    def rolled_with_new(rows, new):
        shifted = pltpu.roll(cache_ref[rows, :], L - t_true, 1)
        tail = lax.dot_general(new, place, (((0,), (0,)), ((), ())), preferred_element_type=F32,
                               precision=lax.Precision.HIGHEST)
        if L > LANES:
            cnew_ref[rows, 0:L - LANES] = shifted[:, 0:L - LANES]
        cnew_ref[rows, L - LANES:L] = jnp.where(is_new, tail, shifted[:, L - LANES:L])

    trow = lax.broadcasted_iota(jnp.int32, (T_PAD, 1), 0)
    scol = lax.broadcasted_iota(jnp.int32, (1, L), 1)
    dist = L + trow - scol
    valid_c = ((dist & (dil - 1)) == 0) & (dist <= NK * dil)
    alibi_c = dist.astype(F32)

    for p in range(2):
        ls = slice(p * LANES, (p + 1) * LANES)
        qn = _head_norm(q_ref[:, ls], qg_ref[:, ls], head0)
        kn = _head_norm(k_ref[:, ls], kg_ref[:, ls], head0)
        vn = v_ref[:, ls]
        krows = slice(p * LANES, (p + 1) * LANES)
        vrows = slice(256 + p * LANES, 256 + (p + 1) * LANES)
        rolled_with_new(krows, kn)
        rolled_with_new(vrows, vn)
        kc_t = cache_ref[krows, :].astype(BF16)
        vc_t = cache_ref[vrows, :].astype(BF16)
        knr = kn.astype(BF16).astype(F32)
        vnr = vn.astype(BF16).astype(F32)
        outs, lses = [], []
        for hh in range(2):
            hmask = head0 if hh == 0 else jnp.logical_not(head0)
            qm = jnp.where(hmask, qn, 0.0).astype(BF16)
            qmr = qm.astype(F32)
            slope = sl_ref[head_base + 2 * p + hh]
            sc = jnp.dot(qm, kc_t, preferred_element_type=F32)
            sc = sc / float(np.sqrt(HEAD_DIM)) - slope * alibi_c
            sc = jnp.where(valid_c, sc, NEG)
            m = jnp.max(sc, axis=-1, keepdims=True)
            sn = []
            for u in range(t_true):
                du = trow - u
                su = jnp.sum(qmr * knr[u:u + 1, :], axis=-1, keepdims=True) / float(np.sqrt(HEAD_DIM))
                su = su - slope * du.astype(F32)
                su = jnp.where((du >= 0) & ((du & (dil - 1)) == 0), su, NEG)
                sn.append(su)
                m = jnp.maximum(m, su)
            pc = jnp.exp(sc - m)
            lsum = jnp.sum(pc, axis=-1, keepdims=True)
            acc = _dot_nt(pc.astype(BF16), vc_t)
            for u in range(t_true):
                pu = jnp.exp(sn[u] - m)
                lsum = lsum + pu
                acc = acc + pu.astype(BF16).astype(F32) * vnr[u:u + 1, :]
            outs.append(acc / lsum)
            lses.append(m + jnp.log(lsum))
        o_ref[:, ls] = jnp.where(head0, outs[0], outs[1])
        lse_ref[:, ls] = jnp.where(head0, lses[0], lses[1])


def attn_sample(P3, q_gain, k_gain, slopes, cache_t, l, g, t_true, win_prev):
    B = P3.shape[0]
    window, dil = ATTN_GROUPS[g]
    L = cache_t.shape[2]
    chained = win_prev is not None

    def col(c):
        return pl.BlockSpec((None, T_PAD, 256), lambda b: (b, 0, c // 256 + g))

    def par(stack):
        return pl.BlockSpec((None, 1, 256), lambda b: (stack * 3 + g, 0, 0))

    in_specs = [col(COL_AQ), col(COL_AK), col(COL_AV), par(l), par(l), pl.BlockSpec(memory_space=pltpu.SMEM),
                pl.BlockSpec((None, 512, L), lambda b: (l * B + b, 0, 0))]
    args = [P3, P3, P3, q_gain, k_gain, slopes, cache_t]
    if chained:
        in_specs.append(pl.BlockSpec(memory_space=pl.ANY))
        args.append(win_prev)
    return pl.pallas_call(
        functools.partial(_attn_sample_kernel, L=L, dil=dil, t_true=t_true, head_base=4 * g, chained=chained),
        grid=(B,),
        in_specs=in_specs,
        out_specs=[
            pl.BlockSpec((None, T_PAD, 256), lambda b: (b, 0, 0)),
            pl.BlockSpec((None, T_PAD, 256), lambda b: (b, 0, 0)),
            pl.BlockSpec((None, 512, L), lambda b: (l * B + b, 0, 0)),
        ],
        out_shape=[
            jax.ShapeDtypeStruct((B, T_PAD, 256), F32),
            jax.ShapeDtypeStruct((B, T_PAD, 256), F32),
            jax.ShapeDtypeStruct((DEPTH * B, 512, L), F32),
        ],
        input_output_aliases={7: 2} if chained else {},
        compiler_params=_cparams(("parallel",)),
    )(*args)


def _attn_merge_kernel(o0, l0, o1, l1, o2, l2, y_ref):
    a0, a1, a2 = l0[...], l1[...], l2[...]
    m = jnp.maximum(jnp.maximum(a0, a1), a2)
    e0, e1, e2 = jnp.exp(a0 - m), jnp.exp(a1 - m), jnp.exp(a2 - m)
    den = e0 + e1 + e2
    y = (e0 / den) * o0[...] + (e1 / den) * o1[...] + (e2 / den) * o2[...]
    y_ref[...] = y.astype(BF16)


def attn_merge(parts, tm):
    M = parts[0].shape[0]
    spec = pl.BlockSpec((tm, 256), lambda m: (m, 0))
    return pl.pallas_call(
        _attn_merge_kernel,
        grid=(M // tm,),
        in_specs=[spec] * 6,
        out_specs=spec,
        out_shape=jax.ShapeDtypeStruct((M, 256), BF16),
        compiler_params=_cparams(("parallel",)),
    )(*parts)


def _merge_kernel(xn_ref, ya_ref, yb_ref, yc_ref, yd_ref, g0, g1, g2, g3, u0, u1, u2, u3, o_ref):
    xn = xn_ref[...]
    acc = None
    for y_ref, g_ref, u_ref in ((ya_ref, g0, u0), (yb_ref, g1, u1), (yc_ref, g2, u2), (yd_ref, g3, u3)):
        gate = jax.nn.sigmoid(_dot_nt(xn, g_ref[...]))
        term = gate * jnp.dot(y_ref[...], u_ref[...], preferred_element_type=F32)
        acc = term if acc is None else acc + term
    o_ref[...] = acc.astype(BF16)


def merge(xn, ya, yb, yc, yd, w_t, ups, l, tm, tn):
    M = xn.shape[0]
    nblk = D_MODEL // tn

    def act(width):
        return pl.BlockSpec((tm, width), lambda m, n: (m, 0))

    def gate(b):
        return pl.BlockSpec((None, tn, D_MODEL), lambda m, n: (l, GATE_ROW0 // tn + b * nblk + n, 0))

    def up(width):
        return pl.BlockSpec((None, width, tn), lambda m, n: (l, 0, n))

    return pl.pallas_call(
        _merge_kernel,
        grid=(M // tm, nblk),
        in_specs=[act(D_MODEL), act(512), act(256), act(512), act(512),
                  gate(0), gate(1), gate(2), gate(3), up(512), up(256), up(512), up(512)],
        out_specs=pl.BlockSpec((tm, tn), lambda m, n: (m, n)),
        out_shape=jax.ShapeDtypeStruct((M, D_MODEL), BF16),
        compiler_params=_cparams(("parallel", "arbitrary")),
    )(xn, ya, yb, yc, yd, w_t, w_t, w_t, w_t, *ups)


def _layer(x, xn, W, l, T, mixers):
    h = ffn_up(xn, W['ff1_gate'], W['ff1_up'], l, T['up_m'], T['up_n'])
    x, xn = matmul_res(x, h, W['ff1_down'], l, 0.5, T['down_m'], W['n_mix'], l)
    P = mix_in(xn, W['w_t'], l, T['up_m'], T['mix_n'])
    (ya, yb, yc, yd), states = mixers(P, l)
    mg = merge(xn, ya, yb, yc, yd, W['w_t'], (W['up_a'], W['up_b'], W['up_c'], W['up_d']), l,
               T['merge_m'], T['merge_n'])
    x, xn = matmul_res(x, mg, W['w_out'], l, 1.0, T['out_m'], W['n_ff2'], l)
    h = ffn_up(xn, W['ff2_gate'], W['ff2_up'], l, T['up_m'], T['up_n'])
    if l + 1 < DEPTH:
        x, xn = matmul_res(x, h, W['ff2_down'], l, 0.5, T['down_m'], W['n_ff1'], l + 1)
    else:
        x, xn = matmul_res(x, h, W['ff2_down'], l, 0.5, T['down_m'])
    return x, xn, states


PROMPT_TILES = dict(up_m=2048, up_n=512, mix_n=512, down_m=256, out_m=512, merge_m=1024, merge_n=256)


def kernel(x_prompt, x_sample, state_conv, cache_w128_kv, cache_w512_kv, cache_w2048_kv, state_gla, state_pool,
           norm_ff1, ff1_gate, ff1_up, ff1_down, norm_mix, w_in, conv_w, attn_q_gain, attn_k_gain,
           gla_w_a2, gla_b_a, gla_norm, pool_w, pool_scale, w_up_conv, w_up_attn, w_up_gla, w_up_pool, w_out,
           norm_ff2, ff2_gate, ff2_up, ff2_down):
    BP, S, _ = x_prompt.shape
    BS, TS, _ = x_sample.shape
    caches = (cache_w128_kv, cache_w512_kv, cache_w2048_kv)

    W = {
        'n_ff1': norm_ff1.reshape(DEPTH, 1, D_MODEL), 'n_mix': norm_mix.reshape(DEPTH, 1, D_MODEL),
        'n_ff2': norm_ff2.reshape(DEPTH, 1, D_MODEL),
        'ff1_gate': ff1_gate.astype(BF16), 'ff1_up': ff1_up.astype(BF16), 'ff1_down': ff1_down.astype(BF16),
        'ff2_gate': ff2_gate.astype(BF16), 'ff2_up': ff2_up.astype(BF16), 'ff2_down': ff2_down.astype(BF16),
        'w_t': prep_w_in(jnp.transpose(w_in, (0, 2, 1))),
        'up_a': w_up_conv.astype(BF16), 'up_b': w_up_attn.astype(BF16),
        'up_c': w_up_gla.astype(BF16), 'up_d': w_up_pool.astype(BF16), 'w_out': w_out.astype(BF16),
    }
    wa_pad = jnp.pad(gla_w_a2, ((0, 0), (0, LANES - gla_w_a2.shape[1]), (0, 0))).astype(BF16)
    b_a = gla_b_a.reshape(DEPTH, 1, 256)
    gn = gla_norm.reshape(DEPTH, 1, LANES)
    pool_wb = pool_w.astype(BF16)
    pool_sc = pool_scale.reshape(DEPTH, 1, W_POOL)
    i = jnp.arange(1, N_ATTN_HEADS + 1, dtype=F32)
    slopes = jnp.exp2(-8.0 * i / N_ATTN_HEADS)
    qg6 = attn_q_gain.reshape(DEPTH * 6, 1, LANES)
    kg6 = attn_k_gain.reshape(DEPTH * 6, 1, LANES)
    qg3 = attn_q_gain.reshape(DEPTH * 3, 1, 256)
    kg3 = attn_k_gain.reshape(DEPTH * 3, 1, 256)

    zc = jnp.zeros((BP, CONV_HIST, W_CONV), F32)
    zp = jnp.zeros((BP, POOL_HIST16, W_POOL), F32)
    zs = jnp.zeros((BP, 256, LANES), F32)

    win_p = [None, None, None]
    win_s = [None, None, None]

    def prompt_mixers(P, l):
        P3 = P.reshape(BP, S, N_MIX)
        ya, yd, cnew, pnew = conv_pool(P3, zc, zp, conv_w, pool_wb, pool_sc, l, 512, 512, 0)
        yc, sfin = gla(P3, wa_pad, b_a, gn, zs, l, 512, GLA_CHUNK, GLA_CHUNK)
        parts = []
        for g in range(3):
            o, lse, win_p[g] = attn_prompt(P3, qg6, kg6, slopes, l, g, win_p[g])
            parts += [o.reshape(BP * S, 256), lse.reshape(BP * S, 256)]
        yb = attn_merge(parts, 1024)
        M = BP * S
        ys = (ya.reshape(M, 512), yb, yc.reshape(M, 512), yd.reshape(M, 512))
        return ys, (cnew, sfin.reshape(BP, 4, GLA_DK, LANES), pnew[:, 1:])

    sconv8 = jnp.pad(state_conv, ((0, 0), (0, 0), (CONV_HIST - 2, 0), (0, 0)))
    spool16 = jnp.pad(state_pool, ((0, 0), (0, 0), (1, 0), (0, 0)))
    sgla = state_gla.reshape(DEPTH, BS, 256, LANES)
    cviews = [jnp.transpose(cc, (0, 1, 3, 4, 5, 2)).reshape(DEPTH * BS, 512, cc.shape[2]) for cc in caches]

    def sample_mixers(P, l):
        P3 = jnp.pad(P.reshape(BS, TS, N_MIX), ((0, 0), (0, T_PAD - TS), (0, 0)))
        ya, yd, cnew, pnew = conv_pool(P3, sconv8[l], spool16[l], conv_w, pool_wb, pool_sc, l, T_PAD, TS, PAST_LEN)
        yc, sfin = gla(P3, wa_pad, b_a, gn, sgla[l], l, T_PAD, T_PAD, TS)
        parts = []
        for g in range(3):
            o, lse, win_s[g] = attn_sample(P3, qg3, kg3, slopes, cviews[g], l, g, TS, win_s[g])
            parts += [o[:, :TS].reshape(BS * TS, 256), lse[:, :TS].reshape(BS * TS, 256)]
        yb = attn_merge(parts, BS * TS)
        M = BS * TS
        ys = (ya[:, :TS].reshape(M, 512), yb, yc[:, :TS].reshape(M, 512), yd[:, :TS].reshape(M, 512))
        return ys, (cnew, sfin.reshape(BS, 4, GLA_DK, LANES), pnew[:, 1:])

    yp = x_prompt.reshape(BP * S, D_MODEL)
    ys = x_sample.reshape(BS * TS, D_MODEL)
    ms = BS * TS
    sample_tiles = dict(up_m=ms, up_n=512, mix_n=512, down_m=ms, out_m=ms, merge_m=ms, merge_n=256)
    ypn = norm(yp, W['n_ff1'], 0, 1024)
    ysn = norm(ys, W['n_ff1'], 0, ms)
    st_p, st_s = [], []
    for l in range(DEPTH):
        yp, ypn, sp = _layer(yp, ypn, W, l, PROMPT_TILES, prompt_mixers)
        ys, ysn, ss = _layer(ys, ysn, W, l, sample_tiles, sample_mixers)
        st_p.append(sp)
        st_s.append(ss)

    def stack(sts, f):
        return jnp.stack([f(s) for s in sts])

    def window_out(buf, B):
        n = buf.shape[-1]
        return jnp.transpose(buf.reshape(DEPTH, B, 2, 4, HEAD_DIM, n), (0, 1, 5, 2, 3, 4))

    outs = [yp.reshape(BP, S, D_MODEL), ys.reshape(BS, TS, D_MODEL),
            stack(st_p, lambda s: s[0]), stack(st_s, lambda s: s[0])]
    for g in range(3):
        outs += [window_out(win_p[g], BP), window_out(win_s[g], BS)]
    outs += [stack(st_p, lambda s: s[1]), stack(st_s, lambda s: s[1]),
             stack(st_p, lambda s: s[2]), stack(st_s, lambda s: s[2])]
    return tuple(outs)
```

```python
import functools

import numpy as np
import jax
import jax.numpy as jnp
from jax import lax
from jax.experimental import pallas as pl
from jax.experimental.pallas import tpu as pltpu

BF16 = jnp.bfloat16
F32 = jnp.float32

D_MODEL = 2048
DEPTH = 4
PAST_LEN = 16384
D_FF = 5504
EPS = 1e-6
N_BRANCH = 4
W_CONV = 512
ATTN_GROUPS = ((128, 1), (512, 4), (2048, 16))
N_ATTN_HEADS = 12
HEAD_DIM = 64
GLA_DK = 64
GLA_TAU = 16.0
GLA_CHUNK = 64
POOL_WINDOWS = (2, 4, 8, 16)
W_POOL = 512
NK = 128
LANES = 128
NEG = -1e30

COL_CB, COL_CC, COL_CH, COL_GV, COL_GR, COL_POOL = 0, 512, 1024, 1536, 2048, 2560
COL_AQ, COL_AK, COL_AV = 3072, 3840, 4608
COL_GQ, COL_GK, COL_LR = 5376, 5632, 5888
N_MIX = 6144
N_IN = 14096
N_GATE = N_BRANCH * D_MODEL
PREP_ROWS = 256
MIX_TILE_SRC = ([256 * j for j in range(6)] + [4352, 4608, 4864, 5120, 5392, 5648]
                + [1536 + 256 * j for j in range(9)] + [3840, 4096, 5376])
SRC_ALIGN = 16
assert all(r % SRC_ALIGN == 0 for r in MIX_TILE_SRC)

VMEM_LIMIT = 56 * 1024 * 1024


def _cparams(sem):
    return pltpu.CompilerParams(dimension_semantics=sem, vmem_limit_bytes=VMEM_LIMIT)


def _rms(x):
    return x * lax.rsqrt(jnp.mean(x * x, axis=-1, keepdims=True) + EPS)


def _norm_kernel(x_ref, g_ref, xn_ref):
    xn_ref[...] = (_rms(x_ref[...]) * g_ref[...]).astype(BF16)


def norm(x, gain, l, tm):
    M = x.shape[0]
    return pl.pallas_call(
        _norm_kernel,
        grid=(M // tm,),
        in_specs=[pl.BlockSpec((tm, D_MODEL), lambda m: (m, 0)),
                  pl.BlockSpec((None, 1, D_MODEL), lambda m: (l, 0, 0))],
        out_specs=pl.BlockSpec((tm, D_MODEL), lambda m: (m, 0)),
        out_shape=jax.ShapeDtypeStruct((M, D_MODEL), BF16),
        compiler_params=_cparams(("parallel",)),
    )(x, gain)


def _ffn_up_kernel(xn_ref, wg_ref, wu_ref, h_ref, wg_s, wu_s):
    @pl.when(pl.program_id(1) == 0)
    def _():
        wg_s[...] = wg_ref[...].astype(BF16)
        wu_s[...] = wu_ref[...].astype(BF16)

    xn = xn_ref[...]
    a = jnp.dot(xn, wg_s[...], preferred_element_type=F32)
    b = jnp.dot(xn, wu_s[...], preferred_element_type=F32)
    h_ref[...] = (a * jax.nn.sigmoid(a) * b).astype(BF16)


def ffn_up(xn, wg, wu, l, tm, tn):
    M = xn.shape[0]
    return pl.pallas_call(
        _ffn_up_kernel,
        grid=(pl.cdiv(D_FF, tn), M // tm),
        in_specs=[
            pl.BlockSpec((tm, D_MODEL), lambda n, m: (m, 0)),
            pl.BlockSpec((None, D_MODEL, tn), lambda n, m: (l, 0, n)),
            pl.BlockSpec((None, D_MODEL, tn), lambda n, m: (l, 0, n)),
        ],
        out_specs=pl.BlockSpec((tm, tn), lambda n, m: (m, n)),
        out_shape=jax.ShapeDtypeStruct((M, D_FF), BF16),
        scratch_shapes=[pltpu.VMEM((D_MODEL, tn), BF16), pltpu.VMEM((D_MODEL, tn), BF16)],
        compiler_params=_cparams(("parallel", "arbitrary")),
    )(xn, wg, wu)


def _matmul_res_kernel(x_ref, a_ref, w_ref, *rest, scale, with_norm):
    y = x_ref[...] + scale * jnp.dot(a_ref[...], w_ref[...], preferred_element_type=F32)
    if with_norm:
        g_ref, o_ref, xn_ref = rest
        xn_ref[...] = (_rms(y) * g_ref[...]).astype(BF16)
    else:
        (o_ref,) = rest
    o_ref[...] = y


def matmul_res(x, a, w, l, scale, tm, next_gain=None, next_l=0):
    M, K = a.shape
    with_norm = next_gain is not None
    row = pl.BlockSpec((tm, D_MODEL), lambda m: (m, 0))
    in_specs = [row, pl.BlockSpec((tm, K), lambda m: (m, 0)),
                pl.BlockSpec((None, K, D_MODEL), lambda m: (l, 0, 0), pipeline_mode=pl.Buffered(1))]
    args = [x, a, w]
    out_specs, out_shape = [row], [jax.ShapeDtypeStruct((M, D_MODEL), F32)]
    if with_norm:
        in_specs.append(pl.BlockSpec((None, 1, D_MODEL), lambda m: (next_l, 0, 0)))
        args.append(next_gain)
        out_specs.append(row)
        out_shape.append(jax.ShapeDtypeStruct((M, D_MODEL), BF16))
    res = pl.pallas_call(
        functools.partial(_matmul_res_kernel, scale=scale, with_norm=with_norm),
        grid=(M // tm,),
        in_specs=in_specs,
        out_specs=out_specs,
        out_shape=out_shape,
        compiler_params=_cparams(("parallel",)),
    )(*args)
    return res if with_norm else (res[0], None)


def _dot_nt(a, b):
    return lax.dot_general(a, b, (((1,), (1,)), ((), ())), preferred_element_type=F32)


def _w_in_rows(l, row_of):
    return pl.BlockSpec((None, pl.Element(PREP_ROWS), pl.Element(D_MODEL)),
                        lambda *idx: (l, row_of(*idx) * SRC_ALIGN, 0))


def _mix_in_kernel(src_ref, xn_ref, wa_ref, wb_ref, p_ref, w_s):
    del src_ref

    @pl.when(pl.program_id(1) == 0)
    def _():
        w_s[0:PREP_ROWS, :] = wa_ref[...].astype(BF16)
        w_s[PREP_ROWS:2 * PREP_ROWS, :] = wb_ref[...].astype(BF16)

    p_ref[...] = _dot_nt(xn_ref[...], w_s[...])


def mix_in(xn, w_in_t, l, tm):
    M = xn.shape[0]
    tn = 2 * PREP_ROWS
    src = jnp.asarray(MIX_TILE_SRC, jnp.int32) // SRC_ALIGN
    return pl.pallas_call(
        _mix_in_kernel,
        grid_spec=pltpu.PrefetchScalarGridSpec(
            num_scalar_prefetch=1,
            grid=(N_MIX // tn, M // tm),
            in_specs=[
                pl.BlockSpec((tm, D_MODEL), lambda n, m, src: (m, 0)),
                _w_in_rows(l, lambda n, m, src: src[2 * n]),
                _w_in_rows(l, lambda n, m, src: src[2 * n + 1]),
            ],
            out_specs=pl.BlockSpec((tm, tn), lambda n, m, src: (m, n)),
            scratch_shapes=[pltpu.VMEM((tn, D_MODEL), BF16)],
        ),
        out_shape=jax.ShapeDtypeStruct((M, N_MIX), F32),
        compiler_params=_cparams(("parallel", "arbitrary")),
    )(src, xn, w_in_t, w_in_t)


CONV_HIST = 8
POOL_HIST16 = 16


def _conv_pool_kernel(cb_ref, cc_ref, ch_ref, pin_ref, cbuf_ref, pbuf_ref, cw_ref, pw_ref, ps_ref,
                      ya_ref, yd_ref, cnew_ref, pnew_ref, uext, pext, *, tt, t_true, pos0):
    t = pl.program_id(1)

    @pl.when(t == 0)
    def _():
        uext[0:CONV_HIST, :] = cbuf_ref[...]
        pext[0:POOL_HIST16, :] = pbuf_ref[...]

    u = cc_ref[...] * ch_ref[...]
    uext[CONV_HIST:CONV_HIST + tt, :] = u
    w = cw_ref[...]
    z = uext[CONV_HIST - 2:CONV_HIST - 2 + tt, :] * w[0:1, :]
    z = z + uext[CONV_HIST - 1:CONV_HIST - 1 + tt, :] * w[1:2, :]
    z = z + u * w[2:3, :]
    ya_ref[...] = (cb_ref[...] * z).astype(BF16)
    cnew_ref[...] = uext[CONV_HIST + t_true - 2:CONV_HIST + t_true, :]
    uext[0:CONV_HIST, :] = uext[tt:tt + CONV_HIST, :]

    pin = pin_ref[...]
    pext[POOL_HIST16:POOL_HIST16 + tt, :] = pin
    pos = (pos0 + t * tt + lax.broadcasted_iota(jnp.int32, (tt, 1), 0)).astype(F32)
    for g, win in enumerate(POOL_WINDOWS):
        ls = slice(g * LANES, (g + 1) * LANES)
        acc = pin[:, ls]
        for i in range(1, win):
            acc = acc + pext[POOL_HIST16 - i:POOL_HIST16 - i + tt, ls]
        cnt = jnp.minimum(float(win), pos + 1.0)
        d = acc / cnt - pin[:, ls]
        y = jnp.dot(d.astype(BF16), pw_ref[g], preferred_element_type=F32) * ps_ref[:, ls]
        yd_ref[:, ls] = y.astype(BF16)
    pnew_ref[...] = pext[t_true:t_true + POOL_HIST16, :]
    pext[0:POOL_HIST16, :] = pext[tt:tt + POOL_HIST16, :]


def conv_pool(P3, cbuf8, pbuf16, conv_w, pool_w, pool_scale, l, tt, t_true, pos0):
    B, T, _ = P3.shape

    def col(c):
        return pl.BlockSpec((None, tt, 512), lambda b, t: (b, t, c // 512))

    return pl.pallas_call(
        functools.partial(_conv_pool_kernel, tt=tt, t_true=t_true, pos0=pos0),
        grid=(B, T // tt),
        in_specs=[
            col(COL_CB), col(COL_CC), col(COL_CH), col(COL_POOL),
            pl.BlockSpec((None, CONV_HIST, 512), lambda b, t: (b, 0, 0)),
            pl.BlockSpec((None, POOL_HIST16, 512), lambda b, t: (b, 0, 0)),
            pl.BlockSpec((None, 3, 512), lambda b, t: (l, 0, 0)),
            pl.BlockSpec((None, 4, LANES, LANES), lambda b, t: (l, 0, 0, 0)),
            pl.BlockSpec((None, 1, 512), lambda b, t: (l, 0, 0)),
        ],
        out_specs=[
            pl.BlockSpec((None, tt, 512), lambda b, t: (b, t, 0)),
            pl.BlockSpec((None, tt, 512), lambda b, t: (b, t, 0)),
            pl.BlockSpec((None, 2, 512), lambda b, t: (b, 0, 0)),
            pl.BlockSpec((None, POOL_HIST16, 512), lambda b, t: (b, 0, 0)),
        ],
        out_shape=[
            jax.ShapeDtypeStruct((B, T, 512), BF16),
            jax.ShapeDtypeStruct((B, T, 512), BF16),
            jax.ShapeDtypeStruct((B, 2, 512), F32),
            jax.ShapeDtypeStruct((B, POOL_HIST16, 512), F32),
        ],
        scratch_shapes=[pltpu.VMEM((CONV_HIST + tt, 512), F32), pltpu.VMEM((POOL_HIST16 + tt, 512), F32)],
        compiler_params=_cparams(("parallel", "arbitrary")),
    )(P3, P3, P3, P3, cbuf8, pbuf16, conv_w, pool_w, pool_scale)


GLA_UNROLL = 4

def _gla_kernel(q_ref, k_ref, v_ref, r_ref, lr_ref, wa_ref, ba_ref, gn_ref, s0_ref,
                yc_ref, sfin_ref, s_scr, b_scr, sprev_scr, *, tt, C, c_true):
    assert c_true == C or tt == C
    t = pl.program_id(1)

    @pl.when(t == 0)
    def _():
        s_scr[...] = s0_ref[...]

    row = lax.broadcasted_iota(jnp.int32, (C, 1), 0)
    colc = lax.broadcasted_iota(jnp.int32, (1, C), 1)
    tril = row >= colc
    tril_f = tril.astype(F32)
    lane = lax.broadcasted_iota(jnp.int32, (1, LANES), 1)
    head0 = lane < GLA_DK
    srow = lax.broadcasted_iota(jnp.int32, (LANES, 1), 0)
    eye = srow == lane
    gn = gn_ref[...]
    nchunk = tt // C

    def chunk_rows(c):
        return pl.ds(pl.multiple_of(c * C, C), C)

    def v_head(rows, h):
        v_h = v_ref[rows, h * LANES:(h + 1) * LANES]
        if c_true < C:
            v_h = jnp.where(row < c_true, v_h, 0.0)
        return v_h.astype(BF16)

    la = jnp.dot(lr_ref[...].astype(BF16), wa_ref[...], preferred_element_type=F32) + ba_ref[...]
    la = jax.nn.log_sigmoid(la) / GLA_TAU
    if c_true < C:
        la = jnp.where(row < c_true, la, 0.0)
    b_scr[...] = la

    def scan(c, carry):
        rows = chunk_rows(c)
        b = jnp.dot(tril_f, b_scr[rows, :], preferred_element_type=F32, precision=lax.Precision.HIGHEST)
        b_scr[rows, :] = b
        b_last = b[C - 1:C, :]
        kend = (k_ref[rows, :] * jnp.exp(b_last - b)).astype(BF16)
        for p in range(2):
            ls = slice(p * LANES, (p + 1) * LANES)
            s_p = s_scr[ls, :]
            sprev_scr[c, ls, :] = s_p.astype(BF16)
            dec = jnp.exp(jnp.sum(jnp.where(eye, b_last[:, ls], 0.0), axis=1, keepdims=True))
            s_new = dec * s_p
            for hh in range(2):
                ds = lax.dot_general(kend[:, ls], v_head(rows, 2 * p + hh), (((0,), (0,)), ((), ())),
                                     preferred_element_type=F32)
                rmask = (srow < GLA_DK) if hh == 0 else (srow >= GLA_DK)
                s_new = s_new + jnp.where(rmask, ds, 0.0)
            s_scr[ls, :] = s_new
        return carry

    lax.fori_loop(0, nchunk, scan, 0, unroll=min(2, nchunk))

    def emit(c, carry):
        rows = chunk_rows(c)
        b = b_scr[rows, :]
        qt = q_ref[rows, :] * (GLA_DK ** -0.5) * jnp.exp(b)
        kt = (k_ref[rows, :] * jnp.exp(-b)).astype(BF16)
        for p in range(2):
            ls = slice(p * LANES, (p + 1) * LANES)
            s_pb = sprev_scr[c, ls, :]
            for hh in range(2):
                h = 2 * p + hh
                hs = slice(h * LANES, (h + 1) * LANES)
                hmask = head0 if hh == 0 else jnp.logical_not(head0)
                qm = jnp.where(hmask, qt[:, ls], 0.0).astype(BF16)
                a = jnp.where(tril, _dot_nt(qm, kt[:, ls]), 0.0)
                o = jnp.dot(a.astype(BF16), v_head(rows, h), preferred_element_type=F32)
                o = o + jnp.dot(qm, s_pb, preferred_element_type=F32)
                r_h = r_ref[rows, hs]
                yc_ref[rows, hs] = (_rms(o) * gn * (r_h * jax.nn.sigmoid(r_h))).astype(BF16)
        return carry

    lax.fori_loop(0, nchunk, emit, 0, unroll=min(GLA_UNROLL, nchunk))
    sfin_ref[...] = s_scr[...]


def gla(P3, wa_pad, b_a, gla_norm, s0, l, tt, C, c_true):
    B, T, _ = P3.shape
    return pl.pallas_call(
        functools.partial(_gla_kernel, tt=tt, C=C, c_true=c_true),
        grid=(B, T // tt),
        in_specs=[
            pl.BlockSpec((None, tt, 256), lambda b, t: (b, t, COL_GQ // 256)),
            pl.BlockSpec((None, tt, 256), lambda b, t: (b, t, COL_GK // 256)),
            pl.BlockSpec((None, tt, 512), lambda b, t: (b, t, COL_GV // 512)),
            pl.BlockSpec((None, tt, 512), lambda b, t: (b, t, COL_GR // 512)),
            pl.BlockSpec((None, tt, LANES), lambda b, t: (b, t, COL_LR // LANES)),
            pl.BlockSpec((None, LANES, 256), lambda b, t: (l, 0, 0)),
            pl.BlockSpec((None, 1, 256), lambda b, t: (l, 0, 0)),
            pl.BlockSpec((None, 1, LANES), lambda b, t: (l, 0, 0)),
            pl.BlockSpec((None, 256, LANES), lambda b, t: (b, 0, 0)),
        ],
        out_specs=[
            pl.BlockSpec((None, tt, 512), lambda b, t: (b, t, 0)),
            pl.BlockSpec((None, 256, LANES), lambda b, t: (b, 0, 0)),
        ],
        out_shape=[jax.ShapeDtypeStruct((B, T, 512), BF16), jax.ShapeDtypeStruct((B, 256, LANES), F32)],
        scratch_shapes=[pltpu.VMEM((256, LANES), F32), pltpu.VMEM((tt, 256), F32),
                        pltpu.VMEM((tt // C, 256, LANES), BF16)],
        compiler_params=_cparams(("parallel", "arbitrary")),
    )(P3, P3, P3, P3, P3, wa_pad, b_a, gla_norm, s0)


def _head_norm(x, gain, head0):
    x2 = x * x
    s0 = jnp.sum(jnp.where(head0, x2, 0.0), axis=-1, keepdims=True)
    s1 = jnp.sum(jnp.where(head0, 0.0, x2), axis=-1, keepdims=True)
    ms = jnp.where(head0, s0, s1) / float(HEAD_DIM)
    return x * lax.rsqrt(ms + EPS) * gain


NORM_ROWS = 512
ATTN_UNROLL = 4
QK_SCALE = 0.125


def _attn_prompt_kernel(q_ref, k_ref, v_ref, qg_ref, kg_ref, sl_ref, *rest, S, dil, n_keep, head_base, chained):
    if chained:
        rest = rest[1:]
    o_ref, lse_ref, kv_ref, tmp, qd, kd, vd, od, ld, bias = rest
    msub = S // dil
    nb = msub // NK
    kstride = msub + NK
    lane = lax.broadcasted_iota(jnp.int32, (1, LANES), 1)
    head0 = lane < HEAD_DIM

    def norm_into_tmp(src_ref, g_ref):
        def body(i, carry):
            rows = pl.ds(pl.multiple_of(i * NORM_ROWS, NORM_ROWS), NORM_ROWS)
            tmp[rows, :] = _head_norm(src_ref[rows, :], g_ref[...], head0)
            return carry
        lax.fori_loop(0, S // NORM_ROWS, body, 0)

    norm_into_tmp(q_ref, qg_ref)
    for r in range(dil):
        qd[r * msub:(r + 1) * msub, :] = (tmp[pl.ds(r, msub, stride=dil), :] * QK_SCALE).astype(BF16)
    norm_into_tmp(k_ref, kg_ref)
    kv_ref[0] = tmp[S - n_keep:S, :].T
    kv_ref[1] = v_ref[S - n_keep:S, :].T
    zeros = jnp.zeros((NK, LANES), BF16)
    for r in range(dil):
        kd[r * kstride:r * kstride + NK, :] = zeros
        vd[r * kstride:r * kstride + NK, :] = zeros
        kd[r * kstride + NK:(r + 1) * kstride, :] = tmp[pl.ds(r, msub, stride=dil), :].astype(BF16)
        vd[r * kstride + NK:(r + 1) * kstride, :] = v_ref[pl.ds(r, msub, stride=dil), :].astype(BF16)

    qi = lax.broadcasted_iota(jnp.int32, (NK, 2 * NK), 0)
    ki = lax.broadcasted_iota(jnp.int32, (NK, 2 * NK), 1)
    dist = qi - ki + NK
    band = (dist >= 0) & (dist <= NK)
    alibi = (dist * dil).astype(F32)
    cur = ki >= NK
    for hh in range(2):
        slope = sl_ref[head_base + 2 * pl.program_id(1) + hh]
        bias[2 * hh] = jnp.where(band, -(slope * alibi), NEG)
        bias[2 * hh + 1] = jnp.where(band & cur, -(slope * alibi), NEG)

    def block(j, carry):
        r = j // nb
        n = j - r * nb
        first = jnp.where(n == 0, 1, 0)
        qb = pl.multiple_of(j * NK, NK)
        kb = pl.multiple_of(j * NK + r * NK, NK)
        q2 = qd[pl.ds(qb, NK), :]
        k2 = kd[pl.ds(kb, 2 * NK), :]
        v2 = vd[pl.ds(kb, 2 * NK), :]
        outs, lses = [], []
        for hh in range(2):
            hmask = head0 if hh == 0 else jnp.logical_not(head0)
            qm = jnp.where(hmask, q2, jnp.zeros_like(q2))
            s = lax.dot_general(qm, k2, (((1,), (1,)), ((), ())), preferred_element_type=F32)
            s = s + bias[2 * hh + first]
            m = jnp.max(s, axis=-1, keepdims=True)
            p = jnp.exp(s - m)
            lsum = jnp.sum(p, axis=-1, keepdims=True)
            outs.append(jnp.dot(p.astype(BF16), v2, preferred_element_type=F32) / lsum)
            lses.append(m + jnp.log(lsum))
        od[pl.ds(qb, NK), :] = jnp.where(head0, outs[0], outs[1])
        ld[pl.ds(qb, NK), :] = jnp.where(head0, lses[0], lses[1])
        return carry

    lax.fori_loop(0, dil * nb, block, 0, unroll=ATTN_UNROLL)

    for r in range(dil):
        o_ref[pl.ds(r, msub, stride=dil), :] = od[r * msub:(r + 1) * msub, :]
        lse_ref[pl.ds(r, msub, stride=dil), :] = ld[r * msub:(r + 1) * msub, :]


def attn_prompt(P3, q_gain, k_gain, slopes, l, g, win_prev):
    B, S, _ = P3.shape
    window, dil = ATTN_GROUPS[g]
    n_keep = min(window, S)
    chained = win_prev is not None

    def col(c):
        return pl.BlockSpec((None, S, LANES), lambda b, p: (b, 0, c // LANES + 2 * g + p))

    def par(stack):
        return pl.BlockSpec((None, 1, LANES), lambda b, p: (stack * 6 + 2 * g + p, 0, 0))

    in_specs = [col(COL_AQ), col(COL_AK), col(COL_AV), par(l), par(l), pl.BlockSpec(memory_space=pltpu.SMEM)]
    args = [P3, P3, P3, q_gain, k_gain, slopes]
    if chained:
        in_specs.append(pl.BlockSpec(memory_space=pl.ANY))
        args.append(win_prev)
    return pl.pallas_call(
        functools.partial(_attn_prompt_kernel, S=S, dil=dil, n_keep=n_keep, head_base=4 * g, chained=chained),
        grid=(B, 2),
        in_specs=in_specs,
        out_specs=[
            pl.BlockSpec((None, S, LANES), lambda b, p: (b, 0, p)),
            pl.BlockSpec((None, S, LANES), lambda b, p: (b, 0, p)),
            pl.BlockSpec((None, 2, LANES, n_keep), lambda b, p: (l * B + b, 0, p, 0)),
        ],
        out_shape=[
            jax.ShapeDtypeStruct((B, S, 256), F32),
            jax.ShapeDtypeStruct((B, S, 256), F32),
            jax.ShapeDtypeStruct((DEPTH * B, 2, 256, n_keep), F32),
        ],
        input_output_aliases={6: 2} if chained else {},
        scratch_shapes=[
            pltpu.VMEM((S, LANES), F32),
            pltpu.VMEM((S, LANES), BF16),
            pltpu.VMEM((S + dil * NK, LANES), BF16),
            pltpu.VMEM((S + dil * NK, LANES), BF16),
            pltpu.VMEM((S, LANES), F32),
            pltpu.VMEM((S, LANES), F32),
            pltpu.VMEM((4, NK, 2 * NK), F32),
        ],
        compiler_params=_cparams(("parallel", "arbitrary")),
    )(*args)


T_PAD = 8


def _attn_sample_kernel(q_ref, k_ref, v_ref, qg_ref, kg_ref, sl_ref, cache_ref, *rest,
                        L, dil, t_true, head_base, chained):
    if chained:
        rest = rest[1:]
    o_ref, lse_ref, cnew_ref = rest
    lane = lax.broadcasted_iota(jnp.int32, (1, LANES), 1)
    head0 = lane < HEAD_DIM
    trow8 = lax.broadcasted_iota(jnp.int32, (T_PAD, LANES), 0)
    lane8 = lax.broadcasted_iota(jnp.int32, (T_PAD, LANES), 1)
    place = ((lane8 == LANES - t_true + trow8) & (trow8 < t_true)).astype(F32)
    is_new = lane >= LANES - t_true

    def rolled_with_new(rows, new):
        shifted = pltpu.roll(cache_ref[rows, :], L - t_true, 1)
        tail = lax.dot_general(new, place, (((0,), (0,)), ((), ())), preferred_element_type=F32,
                               precision=lax.Precision.HIGHEST)
        if L > LANES:
            cnew_ref[rows, 0:L - LANES] = shifted[:, 0:L - LANES]
        cnew_ref[rows, L - LANES:L] = jnp.where(is_new, tail, shifted[:, L - LANES:L])

    trow = lax.broadcasted_iota(jnp.int32, (T_PAD, 1), 0)
    scol = lax.broadcasted_iota(jnp.int32, (1, L), 1)
    dist = L + trow - scol
    valid_c = ((dist & (dil - 1)) == 0) & (dist <= NK * dil)
    alibi_c = dist.astype(F32)

    for p in range(2):
        ls = slice(p * LANES, (p + 1) * LANES)
        qn = _head_norm(q_ref[:, ls], qg_ref[:, ls], head0)
        kn = _head_norm(k_ref[:, ls], kg_ref[:, ls], head0)
        vn = v_ref[:, ls]
        krows = slice(p * LANES, (p + 1) * LANES)
        vrows = slice(256 + p * LANES, 256 + (p + 1) * LANES)
        rolled_with_new(krows, kn)
        rolled_with_new(vrows, vn)
        kc_t = cache_ref[krows, :].astype(BF16)
        vc_t = cache_ref[vrows, :].astype(BF16)
        knr = kn.astype(BF16).astype(F32)
        vnr = vn.astype(BF16).astype(F32)
        outs, lses = [], []
        for hh in range(2):
            hmask = head0 if hh == 0 else jnp.logical_not(head0)
            qm = jnp.where(hmask, qn, 0.0).astype(BF16)
            qmr = qm.astype(F32)
            slope = sl_ref[head_base + 2 * p + hh]
            sc = jnp.dot(qm, kc_t, preferred_element_type=F32)
            sc = sc / float(np.sqrt(HEAD_DIM)) - slope * alibi_c
            sc = jnp.where(valid_c, sc, NEG)
            m = jnp.max(sc, axis=-1, keepdims=True)
            sn = []
            for u in range(t_true):
                du = trow - u
                su = jnp.sum(qmr * knr[u:u + 1, :], axis=-1, keepdims=True) / float(np.sqrt(HEAD_DIM))
                su = su - slope * du.astype(F32)
                su = jnp.where((du >= 0) & ((du & (dil - 1)) == 0), su, NEG)
                sn.append(su)
                m = jnp.maximum(m, su)
            pc = jnp.exp(sc - m)
            lsum = jnp.sum(pc, axis=-1, keepdims=True)
            acc = _dot_nt(pc.astype(BF16), vc_t)
            for u in range(t_true):
                pu = jnp.exp(sn[u] - m)
                lsum = lsum + pu
                acc = acc + pu.astype(BF16).astype(F32) * vnr[u:u + 1, :]
            outs.append(acc / lsum)
            lses.append(m + jnp.log(lsum))
        o_ref[:, ls] = jnp.where(head0, outs[0], outs[1])
        lse_ref[:, ls] = jnp.where(head0, lses[0], lses[1])


def attn_sample(P3, q_gain, k_gain, slopes, cache_t, l, g, t_true, win_prev):
    B = P3.shape[0]
    window, dil = ATTN_GROUPS[g]
    L = cache_t.shape[2]
    chained = win_prev is not None

    def col(c):
        return pl.BlockSpec((None, T_PAD, 256), lambda b: (b, 0, c // 256 + g))

    def par(stack):
        return pl.BlockSpec((None, 1, 256), lambda b: (stack * 3 + g, 0, 0))

    in_specs = [col(COL_AQ), col(COL_AK), col(COL_AV), par(l), par(l), pl.BlockSpec(memory_space=pltpu.SMEM),
                pl.BlockSpec((None, 512, L), lambda b: (l * B + b, 0, 0))]
    args = [P3, P3, P3, q_gain, k_gain, slopes, cache_t]
    if chained:
        in_specs.append(pl.BlockSpec(memory_space=pl.ANY))
        args.append(win_prev)
    return pl.pallas_call(
        functools.partial(_attn_sample_kernel, L=L, dil=dil, t_true=t_true, head_base=4 * g, chained=chained),
        grid=(B,),
        in_specs=in_specs,
        out_specs=[
            pl.BlockSpec((None, T_PAD, 256), lambda b: (b, 0, 0)),
            pl.BlockSpec((None, T_PAD, 256), lambda b: (b, 0, 0)),
            pl.BlockSpec((None, 512, L), lambda b: (l * B + b, 0, 0)),
        ],
        out_shape=[
            jax.ShapeDtypeStruct((B, T_PAD, 256), F32),
            jax.ShapeDtypeStruct((B, T_PAD, 256), F32),
            jax.ShapeDtypeStruct((DEPTH * B, 512, L), F32),
        ],
        input_output_aliases={7: 2} if chained else {},
        compiler_params=_cparams(("parallel",)),
    )(*args)


def _attn_merge_kernel(o0, l0, o1, l1, o2, l2, y_ref):
    a0, a1, a2 = l0[...], l1[...], l2[...]
    m = jnp.maximum(jnp.maximum(a0, a1), a2)
    e0, e1, e2 = jnp.exp(a0 - m), jnp.exp(a1 - m), jnp.exp(a2 - m)
    den = e0 + e1 + e2
    y = (e0 / den) * o0[...] + (e1 / den) * o1[...] + (e2 / den) * o2[...]
    y_ref[...] = y.astype(BF16)


def attn_merge(parts, tm):
    M = parts[0].shape[0]
    spec = pl.BlockSpec((tm, 256), lambda m: (m, 0))
    return pl.pallas_call(
        _attn_merge_kernel,
        grid=(M // tm,),
        in_specs=[spec] * 6,
        out_specs=spec,
        out_shape=jax.ShapeDtypeStruct((M, 256), BF16),
        compiler_params=_cparams(("parallel",)),
    )(*parts)


GATE_SRC0 = 5904
assert GATE_SRC0 % SRC_ALIGN == 0 and D_MODEL % SRC_ALIGN == 0


def _merge_kernel(xn_ref, ya_ref, yb_ref, yc_ref, yd_ref, g0, g1, g2, g3, u0, u1, u2, u3, o_ref,
                  gs0, gs1, gs2, gs3, us0, us1, us2, us3):
    branches = ((ya_ref, g0, u0, gs0, us0), (yb_ref, g1, u1, gs1, us1),
                (yc_ref, g2, u2, gs2, us2), (yd_ref, g3, u3, gs3, us3))

    @pl.when(pl.program_id(1) == 0)
    def _():
        for _, g_ref, u_ref, g_s, u_s in branches:
            g_s[...] = g_ref[...].astype(BF16)
            u_s[...] = u_ref[...].astype(BF16)

    xn = xn_ref[...]
    acc = None
    for y_ref, _, _, g_s, u_s in branches:
        gate = jax.nn.sigmoid(_dot_nt(xn, g_s[...]))
        term = gate * jnp.dot(y_ref[...], u_s[...], preferred_element_type=F32)
        acc = term if acc is None else acc + term
    o_ref[...] = acc.astype(BF16)


def merge(xn, ya, yb, yc, yd, w_in_t, ups, l, tm):
    M = xn.shape[0]
    tn = PREP_ROWS
    widths = (512, 256, 512, 512)

    def act(width):
        return pl.BlockSpec((tm, width), lambda n, m: (m, 0))

    def gate(b):
        base = (GATE_SRC0 + b * D_MODEL) // SRC_ALIGN
        return _w_in_rows(l, lambda n, m: base + n * (tn // SRC_ALIGN))

    def up(width):
        return pl.BlockSpec((None, width, tn), lambda n, m: (l, 0, n))

    return pl.pallas_call(
        _merge_kernel,
        grid=(D_MODEL // tn, M // tm),
        in_specs=[act(D_MODEL)] + [act(w) for w in widths] + [gate(b) for b in range(N_BRANCH)]
                 + [up(w) for w in widths],
        out_specs=pl.BlockSpec((tm, tn), lambda n, m: (m, n)),
        out_shape=jax.ShapeDtypeStruct((M, D_MODEL), BF16),
        scratch_shapes=[pltpu.VMEM((tn, D_MODEL), BF16)] * N_BRANCH + [pltpu.VMEM((w, tn), BF16) for w in widths],
        compiler_params=_cparams(("parallel", "arbitrary")),
    )(xn, ya, yb, yc, yd, w_in_t, w_in_t, w_in_t, w_in_t, *ups)


def _layer(x, xn, W, l, T, mixers):
    h = ffn_up(xn, W['ff1_gate'], W['ff1_up'], l, T['up_m'], T['up_n'])
    x, xn = matmul_res(x, h, W['ff1_down'], l, 0.5, T['down_m'], W['n_mix'], l)
    P = mix_in(xn, W['w_in_t'], l, T['mix_m'])
    (ya, yb, yc, yd), states = mixers(P, l)
    mg = merge(xn, ya, yb, yc, yd, W['w_in_t'], (W['up_a'], W['up_b'], W['up_c'], W['up_d']), l, T['merge_m'])
    x, xn = matmul_res(x, mg, W['w_out'], l, 1.0, T['out_m'], W['n_ff2'], l)
    h = ffn_up(xn, W['ff2_gate'], W['ff2_up'], l, T['up_m'], T['up_n'])
    if l + 1 < DEPTH:
        x, xn = matmul_res(x, h, W['ff2_down'], l, 0.5, T['down_m'], W['n_ff1'], l + 1)
    else:
        x, xn = matmul_res(x, h, W['ff2_down'], l, 0.5, T['down_m'])
    return x, xn, states


PROMPT_TILES = dict(up_m=1024, up_n=512, mix_m=2048, down_m=256, out_m=512, merge_m=1024)


def kernel(x_prompt, x_sample, state_conv, cache_w128_kv, cache_w512_kv, cache_w2048_kv, state_gla, state_pool,
           norm_ff1, ff1_gate, ff1_up, ff1_down, norm_mix, w_in, conv_w, attn_q_gain, attn_k_gain,
           gla_w_a2, gla_b_a, gla_norm, pool_w, pool_scale, w_up_conv, w_up_attn, w_up_gla, w_up_pool, w_out,
           norm_ff2, ff2_gate, ff2_up, ff2_down):
    BP, S, _ = x_prompt.shape
    BS, TS, _ = x_sample.shape
    caches = (cache_w128_kv, cache_w512_kv, cache_w2048_kv)

    W = {
        'n_ff1': norm_ff1.reshape(DEPTH, 1, D_MODEL), 'n_mix': norm_mix.reshape(DEPTH, 1, D_MODEL),
        'n_ff2': norm_ff2.reshape(DEPTH, 1, D_MODEL),
        'ff1_gate': ff1_gate, 'ff1_up': ff1_up, 'ff1_down': ff1_down.astype(BF16),
        'ff2_gate': ff2_gate, 'ff2_up': ff2_up, 'ff2_down': ff2_down.astype(BF16),
        'w_in_t': jnp.transpose(w_in, (0, 2, 1)),
        'up_a': w_up_conv, 'up_b': w_up_attn, 'up_c': w_up_gla, 'up_d': w_up_pool, 'w_out': w_out.astype(BF16),
    }
    wa_pad = jnp.pad(gla_w_a2, ((0, 0), (0, LANES - gla_w_a2.shape[1]), (0, 0))).astype(BF16)
    b_a = gla_b_a.reshape(DEPTH, 1, 256)
    gn = gla_norm.reshape(DEPTH, 1, LANES)
    pool_wb = pool_w.astype(BF16)
    pool_sc = pool_scale.reshape(DEPTH, 1, W_POOL)
    i = jnp.arange(1, N_ATTN_HEADS + 1, dtype=F32)
    slopes = jnp.exp2(-8.0 * i / N_ATTN_HEADS)
    qg6 = attn_q_gain.reshape(DEPTH * 6, 1, LANES)
    kg6 = attn_k_gain.reshape(DEPTH * 6, 1, LANES)
    qg3 = attn_q_gain.reshape(DEPTH * 3, 1, 256)
    kg3 = attn_k_gain.reshape(DEPTH * 3, 1, 256)

    zc = jnp.zeros((BP, CONV_HIST, W_CONV), F32)
    zp = jnp.zeros((BP, POOL_HIST16, W_POOL), F32)
    zs = jnp.zeros((BP, 256, LANES), F32)

    win_p = [None, None, None]
    win_s = [None, None, None]

    def prompt_mixers(P, l):
        P3 = P.reshape(BP, S, N_MIX)
        ya, yd, cnew, pnew = conv_pool(P3, zc, zp, conv_w, pool_wb, pool_sc, l, 512, 512, 0)
        yc, sfin = gla(P3, wa_pad, b_a, gn, zs, l, 512, GLA_CHUNK, GLA_CHUNK)
        parts = []
        for g in range(3):
            o, lse, win_p[g] = attn_prompt(P3, qg6, kg6, slopes, l, g, win_p[g])
            parts += [o.reshape(BP * S, 256), lse.reshape(BP * S, 256)]
        yb = attn_merge(parts, 1024)
        M = BP * S
        ys = (ya.reshape(M, 512), yb, yc.reshape(M, 512), yd.reshape(M, 512))
        return ys, (cnew, sfin.reshape(BP, 4, GLA_DK, LANES), pnew[:, 1:])

    sconv8 = jnp.pad(state_conv, ((0, 0), (0, 0), (CONV_HIST - 2, 0), (0, 0)))
    spool16 = jnp.pad(state_pool, ((0, 0), (0, 0), (1, 0), (0, 0)))
    sgla = state_gla.reshape(DEPTH, BS, 256, LANES)
    cviews = [jnp.transpose(cc, (0, 1, 3, 4, 5, 2)).reshape(DEPTH * BS, 512, cc.shape[2]) for cc in caches]

    def sample_mixers(P, l):
        P3 = jnp.pad(P.reshape(BS, TS, N_MIX), ((0, 0), (0, T_PAD - TS), (0, 0)))
        ya, yd, cnew, pnew = conv_pool(P3, sconv8[l], spool16[l], conv_w, pool_wb, pool_sc, l, T_PAD, TS, PAST_LEN)
        yc, sfin = gla(P3, wa_pad, b_a, gn, sgla[l], l, T_PAD, T_PAD, TS)
        parts = []
        for g in range(3):
            o, lse, win_s[g] = attn_sample(P3, qg3, kg3, slopes, cviews[g], l, g, TS, win_s[g])
            parts += [o[:, :TS].reshape(BS * TS, 256), lse[:, :TS].reshape(BS * TS, 256)]
        yb = attn_merge(parts, BS * TS)
        M = BS * TS
        ys = (ya[:, :TS].reshape(M, 512), yb, yc[:, :TS].reshape(M, 512), yd[:, :TS].reshape(M, 512))
        return ys, (cnew, sfin.reshape(BS, 4, GLA_DK, LANES), pnew[:, 1:])

    yp = x_prompt.reshape(BP * S, D_MODEL)
    ys = x_sample.reshape(BS * TS, D_MODEL)
    ms = BS * TS
    sample_tiles = dict(up_m=ms, up_n=512, mix_m=ms, down_m=ms, out_m=ms, merge_m=ms)
    ypn = norm(yp, W['n_ff1'], 0, 1024)
    ysn = norm(ys, W['n_ff1'], 0, ms)
    st_p, st_s = [], []
    for l in range(DEPTH):
        yp, ypn, sp = _layer(yp, ypn, W, l, PROMPT_TILES, prompt_mixers)
        ys, ysn, ss = _layer(ys, ysn, W, l, sample_tiles, sample_mixers)
        st_p.append(sp)
        st_s.append(ss)

    def stack(sts, f):
        return jnp.stack([f(s) for s in sts])

    def window_out(buf, B):
        n = buf.shape[-1]
        return jnp.transpose(buf.reshape(DEPTH, B, 2, 4, HEAD_DIM, n), (0, 1, 5, 2, 3, 4))

    outs = [yp.reshape(BP, S, D_MODEL), ys.reshape(BS, TS, D_MODEL),
            stack(st_p, lambda s: s[0]), stack(st_s, lambda s: s[0])]
    for g in range(3):
        outs += [window_out(win_p[g], BP), window_out(win_s[g], BS)]
    outs += [stack(st_p, lambda s: s[1]), stack(st_s, lambda s: s[1]),
             stack(st_p, lambda s: s[2]), stack(st_s, lambda s: s[2])]
    return tuple(outs)
```

```python
import functools

import numpy as np
import jax
import jax.numpy as jnp
from jax import lax
from jax.experimental import pallas as pl
from jax.experimental.pallas import tpu as pltpu

BF16 = jnp.bfloat16
F32 = jnp.float32

D_MODEL = 2048
DEPTH = 4
PAST_LEN = 16384
D_FF = 5504
EPS = 1e-6
N_BRANCH = 4
W_CONV = 512
ATTN_GROUPS = ((128, 1), (512, 4), (2048, 16))
N_ATTN_HEADS = 12
HEAD_DIM = 64
GLA_DK = 64
GLA_TAU = 16.0
GLA_CHUNK = 64
POOL_WINDOWS = (2, 4, 8, 16)
W_POOL = 512
NK = 128
LANES = 128
NEG = -1e30

COL_CB, COL_CC, COL_CH, COL_GV, COL_GR, COL_POOL = 0, 512, 1024, 1536, 2048, 2560
COL_AQ, COL_AK, COL_AV = 3072, 3840, 4608
COL_GQ, COL_GK, COL_LR = 5376, 5632, 5888
N_MIX = 6144
N_IN = 14096
N_GATE = N_BRANCH * D_MODEL
PREP_ROWS = 256
MIX_TILE_SRC = ([256 * j for j in range(6)] + [4352, 4608, 4864, 5120, 5392, 5648]
                + [1536 + 256 * j for j in range(9)] + [3840, 4096, 5376])
SRC_ALIGN = 16
assert all(r % SRC_ALIGN == 0 for r in MIX_TILE_SRC)

VMEM_LIMIT = 56 * 1024 * 1024


def _cparams(sem):
    return pltpu.CompilerParams(dimension_semantics=sem, vmem_limit_bytes=VMEM_LIMIT)


def _rms(x):
    return x * lax.rsqrt(jnp.mean(x * x, axis=-1, keepdims=True) + EPS)


def _norm_kernel(x_ref, g_ref, xn_ref):
    xn_ref[...] = (_rms(x_ref[...]) * g_ref[...]).astype(BF16)


def norm(x, gain, l, tm):
    M = x.shape[0]
    return pl.pallas_call(
        _norm_kernel,
        grid=(M // tm,),
        in_specs=[pl.BlockSpec((tm, D_MODEL), lambda m: (m, 0)),
                  pl.BlockSpec((None, 1, D_MODEL), lambda m: (l, 0, 0))],
        out_specs=pl.BlockSpec((tm, D_MODEL), lambda m: (m, 0)),
        out_shape=jax.ShapeDtypeStruct((M, D_MODEL), BF16),
        compiler_params=_cparams(("parallel",)),
    )(x, gain)


def _ffn_up_kernel(xn_ref, wg_ref, wu_ref, h_ref, wg_s, wu_s):
    @pl.when(pl.program_id(1) == 0)
    def _():
        wg_s[...] = wg_ref[...].astype(BF16)
        wu_s[...] = wu_ref[...].astype(BF16)

    xn = xn_ref[...]
    a = jnp.dot(xn, wg_s[...], preferred_element_type=F32)
    b = jnp.dot(xn, wu_s[...], preferred_element_type=F32)
    h_ref[...] = (a * jax.nn.sigmoid(a) * b).astype(BF16)


def ffn_up(xn, wg, wu, l, tm, tn):
    M = xn.shape[0]
    return pl.pallas_call(
        _ffn_up_kernel,
        grid=(pl.cdiv(D_FF, tn), M // tm),
        in_specs=[
            pl.BlockSpec((tm, D_MODEL), lambda n, m: (m, 0)),
            pl.BlockSpec((None, D_MODEL, tn), lambda n, m: (l, 0, n)),
            pl.BlockSpec((None, D_MODEL, tn), lambda n, m: (l, 0, n)),
        ],
        out_specs=pl.BlockSpec((tm, tn), lambda n, m: (m, n)),
        out_shape=jax.ShapeDtypeStruct((M, D_FF), BF16),
        scratch_shapes=[pltpu.VMEM((D_MODEL, tn), BF16), pltpu.VMEM((D_MODEL, tn), BF16)],
        compiler_params=_cparams(("parallel", "arbitrary")),
    )(xn, wg, wu)


def _matmul_res_kernel(x_ref, a_ref, w_ref, *rest, scale, with_norm):
    y = x_ref[...] + scale * jnp.dot(a_ref[...], w_ref[...], preferred_element_type=F32)
    if with_norm:
        g_ref, o_ref, xn_ref = rest
        xn_ref[...] = (_rms(y) * g_ref[...]).astype(BF16)
    else:
        (o_ref,) = rest
    o_ref[...] = y


def matmul_res(x, a, w, l, scale, tm, next_gain=None, next_l=0):
    M, K = a.shape
    with_norm = next_gain is not None
    row = pl.BlockSpec((tm, D_MODEL), lambda m: (m, 0))
    in_specs = [row, pl.BlockSpec((tm, K), lambda m: (m, 0)),
                pl.BlockSpec((None, K, D_MODEL), lambda m: (l, 0, 0), pipeline_mode=pl.Buffered(1))]
    args = [x, a, w]
    out_specs, out_shape = [row], [jax.ShapeDtypeStruct((M, D_MODEL), F32)]
    if with_norm:
        in_specs.append(pl.BlockSpec((None, 1, D_MODEL), lambda m: (next_l, 0, 0)))
        args.append(next_gain)
        out_specs.append(row)
        out_shape.append(jax.ShapeDtypeStruct((M, D_MODEL), BF16))
    res = pl.pallas_call(
        functools.partial(_matmul_res_kernel, scale=scale, with_norm=with_norm),
        grid=(M // tm,),
        in_specs=in_specs,
        out_specs=out_specs,
        out_shape=out_shape,
        compiler_params=_cparams(("parallel",)),
    )(*args)
    return res if with_norm else (res[0], None)


def _dot_nt(a, b):
    return lax.dot_general(a, b, (((1,), (1,)), ((), ())), preferred_element_type=F32)


def _w_in_rows(l, row_of):
    return pl.BlockSpec((None, pl.Element(PREP_ROWS), pl.Element(D_MODEL)),
                        lambda *idx: (l, row_of(*idx) * SRC_ALIGN, 0))


def _mix_in_kernel(src_ref, xn_ref, wa_ref, wb_ref, p_ref, w_s):
    del src_ref

    @pl.when(pl.program_id(1) == 0)
    def _():
        w_s[0:PREP_ROWS, :] = wa_ref[...].astype(BF16)
        w_s[PREP_ROWS:2 * PREP_ROWS, :] = wb_ref[...].astype(BF16)

    p_ref[...] = _dot_nt(xn_ref[...], w_s[...])


def mix_in(xn, w_in_t, l, tm):
    M = xn.shape[0]
    tn = 2 * PREP_ROWS
    src = jnp.asarray(MIX_TILE_SRC, jnp.int32) // SRC_ALIGN
    return pl.pallas_call(
        _mix_in_kernel,
        grid_spec=pltpu.PrefetchScalarGridSpec(
            num_scalar_prefetch=1,
            grid=(N_MIX // tn, M // tm),
            in_specs=[
                pl.BlockSpec((tm, D_MODEL), lambda n, m, src: (m, 0)),
                _w_in_rows(l, lambda n, m, src: src[2 * n]),
                _w_in_rows(l, lambda n, m, src: src[2 * n + 1]),
            ],
            out_specs=pl.BlockSpec((tm, tn), lambda n, m, src: (m, n)),
            scratch_shapes=[pltpu.VMEM((tn, D_MODEL), BF16)],
        ),
        out_shape=jax.ShapeDtypeStruct((M, N_MIX), F32),
        compiler_params=_cparams(("parallel", "arbitrary")),
    )(src, xn, w_in_t, w_in_t)


CONV_HIST = 8
POOL_HIST16 = 16


def _conv_pool_kernel(cb_ref, cc_ref, ch_ref, pin_ref, cbuf_ref, pbuf_ref, cw_ref, pw_ref, ps_ref,
                      ya_ref, yd_ref, cnew_ref, pnew_ref, uext, pext, *, tt, t_true, pos0):
    t = pl.program_id(1)

    @pl.when(t == 0)
    def _():
        uext[0:CONV_HIST, :] = cbuf_ref[...]
        pext[0:POOL_HIST16, :] = pbuf_ref[...]

    u = cc_ref[...] * ch_ref[...]
    uext[CONV_HIST:CONV_HIST + tt, :] = u
    w = cw_ref[...]
    z = uext[CONV_HIST - 2:CONV_HIST - 2 + tt, :] * w[0:1, :]
    z = z + uext[CONV_HIST - 1:CONV_HIST - 1 + tt, :] * w[1:2, :]
    z = z + u * w[2:3, :]
    ya_ref[...] = (cb_ref[...] * z).astype(BF16)
    cnew_ref[...] = uext[CONV_HIST + t_true - 2:CONV_HIST + t_true, :]
    uext[0:CONV_HIST, :] = uext[tt:tt + CONV_HIST, :]

    pin = pin_ref[...]
    pext[POOL_HIST16:POOL_HIST16 + tt, :] = pin
    pos = (pos0 + t * tt + lax.broadcasted_iota(jnp.int32, (tt, 1), 0)).astype(F32)
    for g, win in enumerate(POOL_WINDOWS):
        ls = slice(g * LANES, (g + 1) * LANES)
        acc = pin[:, ls]
        for i in range(1, win):
            acc = acc + pext[POOL_HIST16 - i:POOL_HIST16 - i + tt, ls]
        cnt = jnp.minimum(float(win), pos + 1.0)
        d = acc / cnt - pin[:, ls]
        y = jnp.dot(d.astype(BF16), pw_ref[g], preferred_element_type=F32) * ps_ref[:, ls]
        yd_ref[:, ls] = y.astype(BF16)
    pnew_ref[...] = pext[t_true:t_true + POOL_HIST16, :]
    pext[0:POOL_HIST16, :] = pext[tt:tt + POOL_HIST16, :]


def conv_pool(P3, cbuf8, pbuf16, conv_w, pool_w, pool_scale, l, tt, t_true, pos0):
    B, T, _ = P3.shape

    def col(c):
        return pl.BlockSpec((None, tt, 512), lambda b, t: (b, t, c // 512))

    return pl.pallas_call(
        functools.partial(_conv_pool_kernel, tt=tt, t_true=t_true, pos0=pos0),
        grid=(B, T // tt),
        in_specs=[
            col(COL_CB), col(COL_CC), col(COL_CH), col(COL_POOL),
            pl.BlockSpec((None, CONV_HIST, 512), lambda b, t: (b, 0, 0)),
            pl.BlockSpec((None, POOL_HIST16, 512), lambda b, t: (b, 0, 0)),
            pl.BlockSpec((None, 3, 512), lambda b, t: (l, 0, 0)),
            pl.BlockSpec((None, 4, LANES, LANES), lambda b, t: (l, 0, 0, 0)),
            pl.BlockSpec((None, 1, 512), lambda b, t: (l, 0, 0)),
        ],
        out_specs=[
            pl.BlockSpec((None, tt, 512), lambda b, t: (b, t, 0)),
            pl.BlockSpec((None, tt, 512), lambda b, t: (b, t, 0)),
            pl.BlockSpec((None, 2, 512), lambda b, t: (b, 0, 0)),
            pl.BlockSpec((None, POOL_HIST16, 512), lambda b, t: (b, 0, 0)),
        ],
        out_shape=[
            jax.ShapeDtypeStruct((B, T, 512), BF16),
            jax.ShapeDtypeStruct((B, T, 512), BF16),
            jax.ShapeDtypeStruct((B, 2, 512), F32),
            jax.ShapeDtypeStruct((B, POOL_HIST16, 512), F32),
        ],
        scratch_shapes=[pltpu.VMEM((CONV_HIST + tt, 512), F32), pltpu.VMEM((POOL_HIST16 + tt, 512), F32)],
        compiler_params=_cparams(("parallel", "arbitrary")),
    )(P3, P3, P3, P3, cbuf8, pbuf16, conv_w, pool_w, pool_scale)


GLA_UNROLL = 4

def _gla_kernel(q_ref, k_ref, v_ref, r_ref, lr_ref, wa_ref, ba_ref, gn_ref, s0_ref,
                yc_ref, sfin_ref, s_scr, b_scr, sprev_scr, *, tt, C, c_true):
    assert c_true == C or tt == C
    t = pl.program_id(1)

    @pl.when(t == 0)
    def _():
        s_scr[...] = s0_ref[...]

    row = lax.broadcasted_iota(jnp.int32, (C, 1), 0)
    lane = lax.broadcasted_iota(jnp.int32, (1, LANES), 1)
    head0 = lane < GLA_DK
    srow = lax.broadcasted_iota(jnp.int32, (LANES, 1), 0)
    eye = srow == lane
    gn = gn_ref[...]
    nchunk = tt // C

    def chunk_rows(c):
        return pl.ds(pl.multiple_of(c * C, C), C)

    def v_head(rows, h):
        v_h = v_ref[rows, h * LANES:(h + 1) * LANES]
        if c_true < C:
            v_h = jnp.where(row < c_true, v_h, 0.0)
        return v_h

    la = jnp.dot(lr_ref[...].astype(BF16), wa_ref[...], preferred_element_type=F32) + ba_ref[...]
    la = jax.nn.log_sigmoid(la) / GLA_TAU
    if c_true < C:
        la = jnp.where(row < c_true, la, 0.0)
    cb = 2 * C if nchunk % 2 == 0 else C
    bi = lax.broadcasted_iota(jnp.int32, (cb, 1), 0)
    bj = lax.broadcasted_iota(jnp.int32, (1, cb), 1)
    ones_tri = (((bi >= C) == (bj >= C)) & (bj <= bi)).astype(BF16)
    for blk in range(tt // cb):
        part = la[blk * cb:(blk + 1) * cb, :]
        acc = None
        for _ in range(3):
            term = part.astype(BF16)
            d = jnp.dot(ones_tri, term, preferred_element_type=F32)
            acc = d if acc is None else acc + d
            part = part - term.astype(F32)
        b_scr[blk * cb:(blk + 1) * cb, :] = acc

    def by_head(x):
        return jnp.concatenate([jnp.where(head0, x, 0.0), jnp.where(head0, 0.0, x)], axis=0)

    def v_pair(rows, p):
        return jnp.concatenate([v_head(rows, 2 * p), v_head(rows, 2 * p + 1)], axis=0).astype(BF16)

    def scan(c, carry):
        rows = chunk_rows(c)
        b = b_scr[rows, :]
        b_last = b[C - 1:C, :]
        kend = k_ref[rows, :] * jnp.exp(b_last - b)
        for p in range(2):
            ls = slice(p * LANES, (p + 1) * LANES)
            s_p = s_scr[ls, :]
            sprev_scr[c, ls, :] = s_p.astype(BF16)
            dec = jnp.exp(jnp.sum(jnp.where(eye, b_last[:, ls], 0.0), axis=1, keepdims=True))
            ds = lax.dot_general(by_head(kend[:, ls]).astype(BF16), v_pair(rows, p), (((0,), (0,)), ((), ())),
                                 preferred_element_type=F32)
            s_scr[ls, :] = dec * s_p + ds
        return carry

    lax.fori_loop(0, nchunk, scan, 0, unroll=min(2, nchunk))

    ri = lax.broadcasted_iota(jnp.int32, (2 * C, 1), 0)
    ci = lax.broadcasted_iota(jnp.int32, (1, 2 * C), 1)
    same_head_causal = ((ri >= C) == (ci >= C)) & ((ri % C) >= (ci % C))

    def emit(c, carry):
        rows = chunk_rows(c)
        b = b_scr[rows, :]
        qt = q_ref[rows, :] * (GLA_DK ** -0.5) * jnp.exp(b)
        kt = k_ref[rows, :] * jnp.exp(-b)
        for p in range(2):
            ls = slice(p * LANES, (p + 1) * LANES)
            qm = by_head(qt[:, ls]).astype(BF16)
            kt2 = jnp.concatenate([kt[:, ls], kt[:, ls]], axis=0).astype(BF16)
            a = jnp.where(same_head_causal, _dot_nt(qm, kt2), 0.0)
            o = jnp.dot(a.astype(BF16), v_pair(rows, p), preferred_element_type=F32)
            o = o + jnp.dot(qm, sprev_scr[c, ls, :], preferred_element_type=F32)
            on = _rms(o) * gn
            for hh in range(2):
                hs = slice((2 * p + hh) * LANES, (2 * p + hh + 1) * LANES)
                r_h = r_ref[rows, hs]
                yc_ref[rows, hs] = (on[hh * C:(hh + 1) * C, :] * (r_h * jax.nn.sigmoid(r_h))).astype(BF16)
        return carry

    lax.fori_loop(0, nchunk, emit, 0, unroll=min(GLA_UNROLL, nchunk))
    sfin_ref[...] = s_scr[...]


def gla(P3, wa_pad, b_a, gla_norm, s0, l, tt, C, c_true):
    B, T, _ = P3.shape
    return pl.pallas_call(
        functools.partial(_gla_kernel, tt=tt, C=C, c_true=c_true),
        grid=(B, T // tt),
        in_specs=[
            pl.BlockSpec((None, tt, 256), lambda b, t: (b, t, COL_GQ // 256)),
            pl.BlockSpec((None, tt, 256), lambda b, t: (b, t, COL_GK // 256)),
            pl.BlockSpec((None, tt, 512), lambda b, t: (b, t, COL_GV // 512)),
            pl.BlockSpec((None, tt, 512), lambda b, t: (b, t, COL_GR // 512)),
            pl.BlockSpec((None, tt, LANES), lambda b, t: (b, t, COL_LR // LANES)),
            pl.BlockSpec((None, LANES, 256), lambda b, t: (l, 0, 0)),
            pl.BlockSpec((None, 1, 256), lambda b, t: (l, 0, 0)),
            pl.BlockSpec((None, 1, LANES), lambda b, t: (l, 0, 0)),
            pl.BlockSpec((None, 256, LANES), lambda b, t: (b, 0, 0)),
        ],
        out_specs=[
            pl.BlockSpec((None, tt, 512), lambda b, t: (b, t, 0)),
            pl.BlockSpec((None, 256, LANES), lambda b, t: (b, 0, 0)),
        ],
        out_shape=[jax.ShapeDtypeStruct((B, T, 512), BF16), jax.ShapeDtypeStruct((B, 256, LANES), F32)],
        scratch_shapes=[pltpu.VMEM((256, LANES), F32), pltpu.VMEM((tt, 256), F32),
                        pltpu.VMEM((tt // C, 256, LANES), BF16)],
        compiler_params=_cparams(("parallel", "arbitrary")),
    )(P3, P3, P3, P3, P3, wa_pad, b_a, gla_norm, s0)


def _head_norm(x, gain, head0):
    x2 = x * x
    s0 = jnp.sum(jnp.where(head0, x2, 0.0), axis=-1, keepdims=True)
    s1 = jnp.sum(jnp.where(head0, 0.0, x2), axis=-1, keepdims=True)
    ms = jnp.where(head0, s0, s1) / float(HEAD_DIM)
    return x * lax.rsqrt(ms + EPS) * gain


NORM_ROWS = 512
ATTN_UNROLL = 4
QK_SCALE = 0.125


def _attn_prompt_kernel(q_ref, k_ref, v_ref, qg_ref, kg_ref, sl_ref, *rest, S, dil, n_keep, head_base, chained):
    if chained:
        rest = rest[1:]
    o_ref, lse_ref, kv_ref, tmp, qd, kd, vd, od, ld, bias = rest
    msub = S // dil
    nb = msub // NK
    kstride = msub + NK
    lane = lax.broadcasted_iota(jnp.int32, (1, LANES), 1)
    head0 = lane < HEAD_DIM

    def norm_into_tmp(src_ref, g_ref):
        def body(i, carry):
            rows = pl.ds(pl.multiple_of(i * NORM_ROWS, NORM_ROWS), NORM_ROWS)
            tmp[rows, :] = _head_norm(src_ref[rows, :], g_ref[...], head0)
            return carry
        lax.fori_loop(0, S // NORM_ROWS, body, 0)

    norm_into_tmp(q_ref, qg_ref)
    for r in range(dil):
        qd[r * msub:(r + 1) * msub, :] = (tmp[pl.ds(r, msub, stride=dil), :] * QK_SCALE).astype(BF16)
    norm_into_tmp(k_ref, kg_ref)
    kv_ref[0] = tmp[S - n_keep:S, :].T
    kv_ref[1] = v_ref[S - n_keep:S, :].T
    zeros = jnp.zeros((NK, LANES), BF16)
    for r in range(dil):
        kd[r * kstride:r * kstride + NK, :] = zeros
        vd[r * kstride:r * kstride + NK, :] = zeros
        kd[r * kstride + NK:(r + 1) * kstride, :] = tmp[pl.ds(r, msub, stride=dil), :].astype(BF16)
        vd[r * kstride + NK:(r + 1) * kstride, :] = v_ref[pl.ds(r, msub, stride=dil), :].astype(BF16)

    qi = lax.broadcasted_iota(jnp.int32, (NK, 2 * NK), 0)
    ki = lax.broadcasted_iota(jnp.int32, (NK, 2 * NK), 1)
    dist = qi - ki + NK
    band = (dist >= 0) & (dist <= NK)
    alibi = (dist * dil).astype(F32)
    cur = ki >= NK
    for hh in range(2):
        slope = sl_ref[head_base + 2 * pl.program_id(1) + hh]
        bias[2 * hh] = jnp.where(band, -(slope * alibi), NEG)
        bias[2 * hh + 1] = jnp.where(band & cur, -(slope * alibi), NEG)

    def block(j, carry):
        r = j // nb
        n = j - r * nb
        first = jnp.where(n == 0, 1, 0)
        qb = pl.multiple_of(j * NK, NK)
        kb = pl.multiple_of(j * NK + r * NK, NK)
        q2 = qd[pl.ds(qb, NK), :]
        k2 = kd[pl.ds(kb, 2 * NK), :]
        v2 = vd[pl.ds(kb, 2 * NK), :]
        outs, lses = [], []
        for hh in range(2):
            hmask = head0 if hh == 0 else jnp.logical_not(head0)
            qm = jnp.where(hmask, q2, jnp.zeros_like(q2))
            s = lax.dot_general(qm, k2, (((1,), (1,)), ((), ())), preferred_element_type=F32)
            s = s + bias[2 * hh + first]
            m = jnp.max(s, axis=-1, keepdims=True)
            p = jnp.exp(s - m)
            lsum = jnp.sum(p, axis=-1, keepdims=True)
            outs.append(jnp.dot(p.astype(BF16), v2, preferred_element_type=F32) / lsum)
            lses.append(m + jnp.log(lsum))
        od[pl.ds(qb, NK), :] = jnp.where(head0, outs[0], outs[1])
        ld[pl.ds(qb, NK), :] = jnp.where(head0, lses[0], lses[1])
        return carry

    lax.fori_loop(0, dil * nb, block, 0, unroll=ATTN_UNROLL)

    for r in range(dil):
        o_ref[pl.ds(r, msub, stride=dil), :] = od[r * msub:(r + 1) * msub, :]
        lse_ref[pl.ds(r, msub, stride=dil), :] = ld[r * msub:(r + 1) * msub, :]


def attn_prompt(P3, q_gain, k_gain, slopes, l, g, win_prev):
    B, S, _ = P3.shape
    window, dil = ATTN_GROUPS[g]
    n_keep = min(window, S)
    chained = win_prev is not None

    def col(c):
        return pl.BlockSpec((None, S, LANES), lambda b, p: (b, 0, c // LANES + 2 * g + p))

    def par(stack):
        return pl.BlockSpec((None, 1, LANES), lambda b, p: (stack * 6 + 2 * g + p, 0, 0))

    in_specs = [col(COL_AQ), col(COL_AK), col(COL_AV), par(l), par(l), pl.BlockSpec(memory_space=pltpu.SMEM)]
    args = [P3, P3, P3, q_gain, k_gain, slopes]
    if chained:
        in_specs.append(pl.BlockSpec(memory_space=pl.ANY))
        args.append(win_prev)
    return pl.pallas_call(
        functools.partial(_attn_prompt_kernel, S=S, dil=dil, n_keep=n_keep, head_base=4 * g, chained=chained),
        grid=(B, 2),
        in_specs=in_specs,
        out_specs=[
            pl.BlockSpec((None, S, LANES), lambda b, p: (b, 0, p)),
            pl.BlockSpec((None, S, LANES), lambda b, p: (b, 0, p)),
            pl.BlockSpec((None, 2, LANES, n_keep), lambda b, p: (l * B + b, 0, p, 0)),
        ],
        out_shape=[
            jax.ShapeDtypeStruct((B, S, 256), F32),
            jax.ShapeDtypeStruct((B, S, 256), F32),
            jax.ShapeDtypeStruct((DEPTH * B, 2, 256, n_keep), F32),
        ],
        input_output_aliases={6: 2} if chained else {},
        scratch_shapes=[
            pltpu.VMEM((S, LANES), F32),
            pltpu.VMEM((S, LANES), BF16),
            pltpu.VMEM((S + dil * NK, LANES), BF16),
            pltpu.VMEM((S + dil * NK, LANES), BF16),
            pltpu.VMEM((S, LANES), F32),
            pltpu.VMEM((S, LANES), F32),
            pltpu.VMEM((4, NK, 2 * NK), F32),
        ],
        compiler_params=_cparams(("parallel", "arbitrary")),
    )(*args)


T_PAD = 8


def _attn_sample_kernel(q_ref, k_ref, v_ref, qg_ref, kg_ref, sl_ref, cache_ref, *rest,
                        L, dil, t_true, head_base, chained):
    if chained:
        rest = rest[1:]
    o_ref, lse_ref, cnew_ref = rest
    lane = lax.broadcasted_iota(jnp.int32, (1, LANES), 1)
    head0 = lane < HEAD_DIM
    trow8 = lax.broadcasted_iota(jnp.int32, (T_PAD, LANES), 0)
    lane8 = lax.broadcasted_iota(jnp.int32, (T_PAD, LANES), 1)
    place = ((lane8 == LANES - t_true + trow8) & (trow8 < t_true)).astype(F32)
    is_new = lane >= LANES - t_true

    def rolled_with_new(rows, new):
        shifted = pltpu.roll(cache_ref[rows, :], L - t_true, 1)
        tail = lax.dot_general(new, place, (((0,), (0,)), ((), ())), preferred_element_type=F32,
                               precision=lax.Precision.HIGHEST)
        if L > LANES:
            cnew_ref[rows, 0:L - LANES] = shifted[:, 0:L - LANES]
        cnew_ref[rows, L - LANES:L] = jnp.where(is_new, tail, shifted[:, L - LANES:L])

    trow = lax.broadcasted_iota(jnp.int32, (T_PAD, 1), 0)
    scol = lax.broadcasted_iota(jnp.int32, (1, L), 1)
    dist = L + trow - scol
    valid_c = ((dist & (dil - 1)) == 0) & (dist <= NK * dil)
    alibi_c = dist.astype(F32)

    for p in range(2):
        ls = slice(p * LANES, (p + 1) * LANES)
        qn = _head_norm(q_ref[:, ls], qg_ref[:, ls], head0)
        kn = _head_norm(k_ref[:, ls], kg_ref[:, ls], head0)
        vn = v_ref[:, ls]
        krows = slice(p * LANES, (p + 1) * LANES)
        vrows = slice(256 + p * LANES, 256 + (p + 1) * LANES)
        rolled_with_new(krows, kn)
        rolled_with_new(vrows, vn)
        kc_t = cache_ref[krows, :].astype(BF16)
        vc_t = cache_ref[vrows, :].astype(BF16)
        knr = kn.astype(BF16).astype(F32)
        vnr = vn.astype(BF16).astype(F32)
        outs, lses = [], []
        for hh in range(2):
            hmask = head0 if hh == 0 else jnp.logical_not(head0)
            qm = jnp.where(hmask, qn, 0.0).astype(BF16)
            qmr = qm.astype(F32)
            slope = sl_ref[head_base + 2 * p + hh]
            sc = jnp.dot(qm, kc_t, preferred_element_type=F32)
            sc = sc / float(np.sqrt(HEAD_DIM)) - slope * alibi_c
            sc = jnp.where(valid_c, sc, NEG)
            m = jnp.max(sc, axis=-1, keepdims=True)
            sn = []
            for u in range(t_true):
                du = trow - u
                su = jnp.sum(qmr * knr[u:u + 1, :], axis=-1, keepdims=True) / float(np.sqrt(HEAD_DIM))
                su = su - slope * du.astype(F32)
                su = jnp.where((du >= 0) & ((du & (dil - 1)) == 0), su, NEG)
                sn.append(su)
                m = jnp.maximum(m, su)
            pc = jnp.exp(sc - m)
            lsum = jnp.sum(pc, axis=-1, keepdims=True)
            acc = _dot_nt(pc.astype(BF16), vc_t)
            for u in range(t_true):
                pu = jnp.exp(sn[u] - m)
                lsum = lsum + pu
                acc = acc + pu.astype(BF16).astype(F32) * vnr[u:u + 1, :]
            outs.append(acc / lsum)
            lses.append(m + jnp.log(lsum))
        o_ref[:, ls] = jnp.where(head0, outs[0], outs[1])
        lse_ref[:, ls] = jnp.where(head0, lses[0], lses[1])


def attn_sample(P3, q_gain, k_gain, slopes, cache_t, l, g, t_true, win_prev):
    B = P3.shape[0]
    window, dil = ATTN_GROUPS[g]
    L = cache_t.shape[2]
    chained = win_prev is not None

    def col(c):
        return pl.BlockSpec((None, T_PAD, 256), lambda b: (b, 0, c // 256 + g))

    def par(stack):
        return pl.BlockSpec((None, 1, 256), lambda b: (stack * 3 + g, 0, 0))

    in_specs = [col(COL_AQ), col(COL_AK), col(COL_AV), par(l), par(l), pl.BlockSpec(memory_space=pltpu.SMEM),
                pl.BlockSpec((None, 512, L), lambda b: (l * B + b, 0, 0))]
    args = [P3, P3, P3, q_gain, k_gain, slopes, cache_t]
    if chained:
        in_specs.append(pl.BlockSpec(memory_space=pl.ANY))
        args.append(win_prev)
    return pl.pallas_call(
        functools.partial(_attn_sample_kernel, L=L, dil=dil, t_true=t_true, head_base=4 * g, chained=chained),
        grid=(B,),
        in_specs=in_specs,
        out_specs=[
            pl.BlockSpec((None, T_PAD, 256), lambda b: (b, 0, 0)),
            pl.BlockSpec((None, T_PAD, 256), lambda b: (b, 0, 0)),
            pl.BlockSpec((None, 512, L), lambda b: (l * B + b, 0, 0)),
        ],
        out_shape=[
            jax.ShapeDtypeStruct((B, T_PAD, 256), F32),
            jax.ShapeDtypeStruct((B, T_PAD, 256), F32),
            jax.ShapeDtypeStruct((DEPTH * B, 512, L), F32),
        ],
        input_output_aliases={7: 2} if chained else {},
        compiler_params=_cparams(("parallel",)),
    )(*args)


def _attn_merge_kernel(o0, l0, o1, l1, o2, l2, y_ref):
    a0, a1, a2 = l0[...], l1[...], l2[...]
    m = jnp.maximum(jnp.maximum(a0, a1), a2)
    e0, e1, e2 = jnp.exp(a0 - m), jnp.exp(a1 - m), jnp.exp(a2 - m)
    den = e0 + e1 + e2
    y = (e0 / den) * o0[...] + (e1 / den) * o1[...] + (e2 / den) * o2[...]
    y_ref[...] = y.astype(BF16)


def attn_merge(parts, tm):
    M = parts[0].shape[0]
    spec = pl.BlockSpec((tm, 256), lambda m: (m, 0))
    return pl.pallas_call(
        _attn_merge_kernel,
        grid=(M // tm,),
        in_specs=[spec] * 6,
        out_specs=spec,
        out_shape=jax.ShapeDtypeStruct((M, 256), BF16),
        compiler_params=_cparams(("parallel",)),
    )(*parts)


GATE_SRC0 = 5904
assert GATE_SRC0 % SRC_ALIGN == 0 and D_MODEL % SRC_ALIGN == 0


def _merge_kernel(xn_ref, ya_ref, yb_ref, yc_ref, yd_ref, g0, g1, g2, g3, u0, u1, u2, u3, o_ref,
                  gs0, gs1, gs2, gs3, us0, us1, us2, us3):
    branches = ((ya_ref, g0, u0, gs0, us0), (yb_ref, g1, u1, gs1, us1),
                (yc_ref, g2, u2, gs2, us2), (yd_ref, g3, u3, gs3, us3))

    @pl.when(pl.program_id(1) == 0)
    def _():
        for _, g_ref, u_ref, g_s, u_s in branches:
            g_s[...] = g_ref[...].astype(BF16)
            u_s[...] = u_ref[...].astype(BF16)

    xn = xn_ref[...]
    acc = None
    for y_ref, _, _, g_s, u_s in branches:
        gate = jax.nn.sigmoid(_dot_nt(xn, g_s[...]))
        term = gate * jnp.dot(y_ref[...], u_s[...], preferred_element_type=F32)
        acc = term if acc is None else acc + term
    o_ref[...] = acc.astype(BF16)


def merge(xn, ya, yb, yc, yd, w_in_t, ups, l, tm):
    M = xn.shape[0]
    tn = PREP_ROWS
    widths = (512, 256, 512, 512)

    def act(width):
        return pl.BlockSpec((tm, width), lambda n, m: (m, 0))

    def gate(b):
        base = (GATE_SRC0 + b * D_MODEL) // SRC_ALIGN
        return _w_in_rows(l, lambda n, m: base + n * (tn // SRC_ALIGN))

    def up(width):
        return pl.BlockSpec((None, width, tn), lambda n, m: (l, 0, n))

    return pl.pallas_call(
        _merge_kernel,
        grid=(D_MODEL // tn, M // tm),
        in_specs=[act(D_MODEL)] + [act(w) for w in widths] + [gate(b) for b in range(N_BRANCH)]
                 + [up(w) for w in widths],
        out_specs=pl.BlockSpec((tm, tn), lambda n, m: (m, n)),
        out_shape=jax.ShapeDtypeStruct((M, D_MODEL), BF16),
        scratch_shapes=[pltpu.VMEM((tn, D_MODEL), BF16)] * N_BRANCH + [pltpu.VMEM((w, tn), BF16) for w in widths],
        compiler_params=_cparams(("parallel", "arbitrary")),
    )(xn, ya, yb, yc, yd, w_in_t, w_in_t, w_in_t, w_in_t, *ups)


def _layer(x, xn, W, l, T, mixers):
    h = ffn_up(xn, W['ff1_gate'], W['ff1_up'], l, T['up_m'], T['up_n'])
    x, xn = matmul_res(x, h, W['ff1_down'], l, 0.5, T['down_m'], W['n_mix'], l)
    P = mix_in(xn, W['w_in_t'], l, T['mix_m'])
    (ya, yb, yc, yd), states = mixers(P, l)
    mg = merge(xn, ya, yb, yc, yd, W['w_in_t'], (W['up_a'], W['up_b'], W['up_c'], W['up_d']), l, T['merge_m'])
    x, xn = matmul_res(x, mg, W['w_out'], l, 1.0, T['out_m'], W['n_ff2'], l)
    h = ffn_up(xn, W['ff2_gate'], W['ff2_up'], l, T['up_m'], T['up_n'])
    if l + 1 < DEPTH:
        x, xn = matmul_res(x, h, W['ff2_down'], l, 0.5, T['down_m'], W['n_ff1'], l + 1)
    else:
        x, xn = matmul_res(x, h, W['ff2_down'], l, 0.5, T['down_m'])
    return x, xn, states


PROMPT_TILES = dict(up_m=1024, up_n=512, mix_m=2048, down_m=256, out_m=512, merge_m=1024)


def kernel(x_prompt, x_sample, state_conv, cache_w128_kv, cache_w512_kv, cache_w2048_kv, state_gla, state_pool,
           norm_ff1, ff1_gate, ff1_up, ff1_down, norm_mix, w_in, conv_w, attn_q_gain, attn_k_gain,
           gla_w_a2, gla_b_a, gla_norm, pool_w, pool_scale, w_up_conv, w_up_attn, w_up_gla, w_up_pool, w_out,
           norm_ff2, ff2_gate, ff2_up, ff2_down):
    BP, S, _ = x_prompt.shape
    BS, TS, _ = x_sample.shape
    caches = (cache_w128_kv, cache_w512_kv, cache_w2048_kv)

    W = {
        'n_ff1': norm_ff1.reshape(DEPTH, 1, D_MODEL), 'n_mix': norm_mix.reshape(DEPTH, 1, D_MODEL),
        'n_ff2': norm_ff2.reshape(DEPTH, 1, D_MODEL),
        'ff1_gate': ff1_gate, 'ff1_up': ff1_up, 'ff1_down': ff1_down.astype(BF16),
        'ff2_gate': ff2_gate, 'ff2_up': ff2_up, 'ff2_down': ff2_down.astype(BF16),
        'w_in_t': jnp.transpose(w_in, (0, 2, 1)),
        'up_a': w_up_conv, 'up_b': w_up_attn, 'up_c': w_up_gla, 'up_d': w_up_pool, 'w_out': w_out.astype(BF16),
    }
    wa_pad = jnp.pad(gla_w_a2, ((0, 0), (0, LANES - gla_w_a2.shape[1]), (0, 0))).astype(BF16)
    b_a = gla_b_a.reshape(DEPTH, 1, 256)
    gn = gla_norm.reshape(DEPTH, 1, LANES)
    pool_wb = pool_w.astype(BF16)
    pool_sc = pool_scale.reshape(DEPTH, 1, W_POOL)
    i = jnp.arange(1, N_ATTN_HEADS + 1, dtype=F32)
    slopes = jnp.exp2(-8.0 * i / N_ATTN_HEADS)
    qg6 = attn_q_gain.reshape(DEPTH * 6, 1, LANES)
    kg6 = attn_k_gain.reshape(DEPTH * 6, 1, LANES)
    qg3 = attn_q_gain.reshape(DEPTH * 3, 1, 256)
    kg3 = attn_k_gain.reshape(DEPTH * 3, 1, 256)

    zc = jnp.zeros((BP, CONV_HIST, W_CONV), F32)
    zp = jnp.zeros((BP, POOL_HIST16, W_POOL), F32)
    zs = jnp.zeros((BP, 256, LANES), F32)

    win_p = [None, None, None]
    win_s = [None, None, None]

    def prompt_mixers(P, l):
        P3 = P.reshape(BP, S, N_MIX)
        ya, yd, cnew, pnew = conv_pool(P3, zc, zp, conv_w, pool_wb, pool_sc, l, 512, 512, 0)
        yc, sfin = gla(P3, wa_pad, b_a, gn, zs, l, 512, GLA_CHUNK, GLA_CHUNK)
        parts = []
        for g in range(3):
            o, lse, win_p[g] = attn_prompt(P3, qg6, kg6, slopes, l, g, win_p[g])
            parts += [o.reshape(BP * S, 256), lse.reshape(BP * S, 256)]
        yb = attn_merge(parts, 1024)
        M = BP * S
        ys = (ya.reshape(M, 512), yb, yc.reshape(M, 512), yd.reshape(M, 512))
        return ys, (cnew, sfin.reshape(BP, 4, GLA_DK, LANES), pnew[:, 1:])

    sconv8 = jnp.pad(state_conv, ((0, 0), (0, 0), (CONV_HIST - 2, 0), (0, 0)))
    spool16 = jnp.pad(state_pool, ((0, 0), (0, 0), (1, 0), (0, 0)))
    sgla = state_gla.reshape(DEPTH, BS, 256, LANES)
    cviews = [jnp.transpose(cc, (0, 1, 3, 4, 5, 2)).reshape(DEPTH * BS, 512, cc.shape[2]) for cc in caches]

    def sample_mixers(P, l):
        P3 = jnp.pad(P.reshape(BS, TS, N_MIX), ((0, 0), (0, T_PAD - TS), (0, 0)))
        ya, yd, cnew, pnew = conv_pool(P3, sconv8[l], spool16[l], conv_w, pool_wb, pool_sc, l, T_PAD, TS, PAST_LEN)
        yc, sfin = gla(P3, wa_pad, b_a, gn, sgla[l], l, T_PAD, T_PAD, TS)
        parts = []
        for g in range(3):
            o, lse, win_s[g] = attn_sample(P3, qg3, kg3, slopes, cviews[g], l, g, TS, win_s[g])
            parts += [o[:, :TS].reshape(BS * TS, 256), lse[:, :TS].reshape(BS * TS, 256)]
        yb = attn_merge(parts, BS * TS)
        M = BS * TS
        ys = (ya[:, :TS].reshape(M, 512), yb, yc[:, :TS].reshape(M, 512), yd[:, :TS].reshape(M, 512))
        return ys, (cnew, sfin.reshape(BS, 4, GLA_DK, LANES), pnew[:, 1:])

    yp = x_prompt.reshape(BP * S, D_MODEL)
    ys = x_sample.reshape(BS * TS, D_MODEL)
    ms = BS * TS
    sample_tiles = dict(up_m=ms, up_n=512, mix_m=ms, down_m=ms, out_m=ms, merge_m=ms)
    ypn = norm(yp, W['n_ff1'], 0, 1024)
    ysn = norm(ys, W['n_ff1'], 0, ms)
    st_p, st_s = [], []
    for l in range(DEPTH):
        yp, ypn, sp = _layer(yp, ypn, W, l, PROMPT_TILES, prompt_mixers)
        ys, ysn, ss = _layer(ys, ysn, W, l, sample_tiles, sample_mixers)
        st_p.append(sp)
        st_s.append(ss)

    def stack(sts, f):
        return jnp.stack([f(s) for s in sts])

    def window_out(buf, B):
        n = buf.shape[-1]
        return jnp.transpose(buf.reshape(DEPTH, B, 2, 4, HEAD_DIM, n), (0, 1, 5, 2, 3, 4))

    outs = [yp.reshape(BP, S, D_MODEL), ys.reshape(BS, TS, D_MODEL),
            stack(st_p, lambda s: s[0]), stack(st_s, lambda s: s[0])]
    for g in range(3):
        outs += [window_out(win_p[g], BP), window_out(win_s[g], BS)]
    outs += [stack(st_p, lambda s: s[1]), stack(st_s, lambda s: s[1]),
             stack(st_p, lambda s: s[2]), stack(st_s, lambda s: s[2])]
    return tuple(outs)
```

```python
import functools

import numpy as np
import jax
import jax.numpy as jnp
from jax import lax
from jax.experimental import pallas as pl
from jax.experimental.pallas import tpu as pltpu

BF16 = jnp.bfloat16
F32 = jnp.float32

D_MODEL = 2048
DEPTH = 4
PAST_LEN = 16384
D_FF = 5504
EPS = 1e-6
N_BRANCH = 4
W_CONV = 512
ATTN_GROUPS = ((128, 1), (512, 4), (2048, 16))
N_ATTN_HEADS = 12
HEAD_DIM = 64
GLA_DK = 64
GLA_TAU = 16.0
GLA_CHUNK = 64
POOL_WINDOWS = (2, 4, 8, 16)
W_POOL = 512
NK = 128
LANES = 128
NEG = -1e30

COL_CB, COL_CC, COL_CH, COL_GV, COL_GR, COL_POOL = 0, 512, 1024, 1536, 2048, 2560
COL_AQ, COL_AK, COL_AV = 3072, 3840, 4608
COL_GQ, COL_GK, COL_LR = 5376, 5632, 5888
N_MIX = 6144
N_IN = 14096
N_GATE = N_BRANCH * D_MODEL
PREP_ROWS = 256
MIX_TILE_SRC = ([256 * j for j in range(6)] + [4352, 4608, 4864, 5120, 5392, 5648]
                + [1536 + 256 * j for j in range(9)] + [3840, 4096, 5376])
SRC_ALIGN = 16
assert all(r % SRC_ALIGN == 0 for r in MIX_TILE_SRC)

VMEM_LIMIT = 56 * 1024 * 1024


def _cparams(sem):
    return pltpu.CompilerParams(dimension_semantics=sem, vmem_limit_bytes=VMEM_LIMIT)


def _rms(x):
    return x * lax.rsqrt(jnp.mean(x * x, axis=-1, keepdims=True) + EPS)


def _row_tiles(M, tm):
    n = pl.cdiv(M, tm)
    return n, M - (n - 1) * tm


def _on_tile_rows(m, tiles, tm, fn):
    n, tail = tiles
    if tail == tm:
        fn(slice(0, tm))
        return
    pl.when(m < n - 1)(lambda: fn(slice(0, tm)))
    pl.when(m == n - 1)(lambda: fn(slice(0, tail)))


def _norm_kernel(x_ref, g_ref, xn_ref, *, tiles, tm):
    def rows_fn(rows):
        xn_ref[rows, :] = (_rms(x_ref[rows, :]) * g_ref[...]).astype(BF16)

    _on_tile_rows(pl.program_id(0), tiles, tm, rows_fn)


def norm(x, gain, l, tm):
    M = x.shape[0]
    tiles = _row_tiles(M, tm)
    return pl.pallas_call(
        functools.partial(_norm_kernel, tiles=tiles, tm=tm),
        grid=(tiles[0],),
        in_specs=[pl.BlockSpec((tm, D_MODEL), lambda m: (m, 0)),
                  pl.BlockSpec((None, 1, D_MODEL), lambda m: (l, 0, 0))],
        out_specs=pl.BlockSpec((tm, D_MODEL), lambda m: (m, 0)),
        out_shape=jax.ShapeDtypeStruct((M, D_MODEL), BF16),
        compiler_params=_cparams(("parallel",)),
    )(x, gain)


def _ffn_up_kernel(xn_ref, wg_ref, wu_ref, h_ref, wg_s, wu_s, *, tiles, tm):
    m = pl.program_id(1)

    @pl.when(m == 0)
    def _():
        wg_s[...] = wg_ref[...].astype(BF16)
        wu_s[...] = wu_ref[...].astype(BF16)

    def rows_fn(rows):
        xn = xn_ref[rows, :]
        a = jnp.dot(xn, wg_s[...], preferred_element_type=F32)
        b = jnp.dot(xn, wu_s[...], preferred_element_type=F32)
        h_ref[rows, :] = (a * jax.nn.sigmoid(a) * b).astype(BF16)

    _on_tile_rows(m, tiles, tm, rows_fn)


def ffn_up(xn, wg, wu, l, tm, tn):
    M = xn.shape[0]
    tiles = _row_tiles(M, tm)
    return pl.pallas_call(
        functools.partial(_ffn_up_kernel, tiles=tiles, tm=tm),
        grid=(pl.cdiv(D_FF, tn), tiles[0]),
        in_specs=[
            pl.BlockSpec((tm, D_MODEL), lambda n, m: (m, 0)),
            pl.BlockSpec((None, D_MODEL, tn), lambda n, m: (l, 0, n)),
            pl.BlockSpec((None, D_MODEL, tn), lambda n, m: (l, 0, n)),
        ],
        out_specs=pl.BlockSpec((tm, tn), lambda n, m: (m, n)),
        out_shape=jax.ShapeDtypeStruct((M, D_FF), BF16),
        scratch_shapes=[pltpu.VMEM((D_MODEL, tn), BF16), pltpu.VMEM((D_MODEL, tn), BF16)],
        compiler_params=_cparams(("parallel", "arbitrary")),
    )(xn, wg, wu)


def _matmul_res_kernel(x_ref, a_ref, w_ref, *rest, scale, with_norm, tiles, tm):
    def rows_fn(rows):
        y = x_ref[rows, :] + scale * jnp.dot(a_ref[rows, :], w_ref[...], preferred_element_type=F32)
        if with_norm:
            g_ref, o_ref, xn_ref = rest
            xn_ref[rows, :] = (_rms(y) * g_ref[...]).astype(BF16)
        else:
            (o_ref,) = rest
        o_ref[rows, :] = y

    _on_tile_rows(pl.program_id(0), tiles, tm, rows_fn)


def matmul_res(x, a, w, l, scale, tm, next_gain=None, next_l=0):
    M, K = a.shape
    with_norm = next_gain is not None
    row = pl.BlockSpec((tm, D_MODEL), lambda m: (m, 0))
    in_specs = [row, pl.BlockSpec((tm, K), lambda m: (m, 0)),
                pl.BlockSpec((None, K, D_MODEL), lambda m: (l, 0, 0), pipeline_mode=pl.Buffered(1))]
    args = [x, a, w]
    out_specs, out_shape = [row], [jax.ShapeDtypeStruct((M, D_MODEL), F32)]
    if with_norm:
        in_specs.append(pl.BlockSpec((None, 1, D_MODEL), lambda m: (next_l, 0, 0)))
        args.append(next_gain)
        out_specs.append(row)
        out_shape.append(jax.ShapeDtypeStruct((M, D_MODEL), BF16))
    tiles = _row_tiles(M, tm)
    res = pl.pallas_call(
        functools.partial(_matmul_res_kernel, scale=scale, with_norm=with_norm, tiles=tiles, tm=tm),
        grid=(tiles[0],),
        in_specs=in_specs,
        out_specs=out_specs,
        out_shape=out_shape,
        compiler_params=_cparams(("parallel",)),
    )(*args)
    return res if with_norm else (res[0], None)


def _dot_nt(a, b):
    return lax.dot_general(a, b, (((1,), (1,)), ((), ())), preferred_element_type=F32)


def _w_in_rows(l, row_of):
    return pl.BlockSpec((None, pl.Element(PREP_ROWS), pl.Element(D_MODEL)),
                        lambda *idx: (l, row_of(*idx) * SRC_ALIGN, 0))


def _mix_in_kernel(src_ref, xn_ref, wa_ref, wb_ref, p_ref, w_s, *, tiles, tm):
    del src_ref
    m = pl.program_id(1)

    @pl.when(m == 0)
    def _():
        w_s[0:PREP_ROWS, :] = wa_ref[...].astype(BF16)
        w_s[PREP_ROWS:2 * PREP_ROWS, :] = wb_ref[...].astype(BF16)

    def rows_fn(rows):
        p_ref[rows, :] = _dot_nt(xn_ref[rows, :], w_s[...])

    _on_tile_rows(m, tiles, tm, rows_fn)


def mix_in(xn, w_in_t, l, tm):
    M = xn.shape[0]
    tn = 2 * PREP_ROWS
    tiles = _row_tiles(M, tm)
    src = jnp.asarray(MIX_TILE_SRC, jnp.int32) // SRC_ALIGN
    return pl.pallas_call(
        functools.partial(_mix_in_kernel, tiles=tiles, tm=tm),
        grid_spec=pltpu.PrefetchScalarGridSpec(
            num_scalar_prefetch=1,
            grid=(N_MIX // tn, tiles[0]),
            in_specs=[
                pl.BlockSpec((tm, D_MODEL), lambda n, m, src: (m, 0)),
                _w_in_rows(l, lambda n, m, src: src[2 * n]),
                _w_in_rows(l, lambda n, m, src: src[2 * n + 1]),
            ],
            out_specs=pl.BlockSpec((tm, tn), lambda n, m, src: (m, n)),
            scratch_shapes=[pltpu.VMEM((tn, D_MODEL), BF16)],
        ),
        out_shape=jax.ShapeDtypeStruct((M, N_MIX), F32),
        compiler_params=_cparams(("parallel", "arbitrary")),
    )(src, xn, w_in_t, w_in_t)


CONV_HIST = 8
POOL_HIST16 = 16


def _conv_pool_kernel(cb_ref, cc_ref, ch_ref, pin_ref, cbuf_ref, pbuf_ref, cw_ref, pw_ref, ps_ref,
                      ya_ref, yd_ref, cnew_ref, pnew_ref, uext, pext, *, tt, t_true, pos0):
    t = pl.program_id(1)

    @pl.when(t == 0)
    def _():
        uext[0:CONV_HIST, :] = cbuf_ref[...]
        pext[0:POOL_HIST16, :] = pbuf_ref[...]

    u = cc_ref[...] * ch_ref[...]
    uext[CONV_HIST:CONV_HIST + tt, :] = u
    w = cw_ref[...]
    z = uext[CONV_HIST - 2:CONV_HIST - 2 + tt, :] * w[0:1, :]
    z = z + uext[CONV_HIST - 1:CONV_HIST - 1 + tt, :] * w[1:2, :]
    z = z + u * w[2:3, :]
    ya_ref[...] = (cb_ref[...] * z).astype(BF16)
    cnew_ref[...] = uext[CONV_HIST + t_true - 2:CONV_HIST + t_true, :]
    uext[0:CONV_HIST, :] = uext[tt:tt + CONV_HIST, :]

    pin = pin_ref[...]
    pext[POOL_HIST16:POOL_HIST16 + tt, :] = pin
    pos = (pos0 + t * tt + lax.broadcasted_iota(jnp.int32, (tt, 1), 0)).astype(F32)
    for g, win in enumerate(POOL_WINDOWS):
        ls = slice(g * LANES, (g + 1) * LANES)
        acc = pin[:, ls]
        for i in range(1, win):
            acc = acc + pext[POOL_HIST16 - i:POOL_HIST16 - i + tt, ls]
        cnt = jnp.minimum(float(win), pos + 1.0)
        d = acc / cnt - pin[:, ls]
        y = jnp.dot(d.astype(BF16), pw_ref[g], preferred_element_type=F32) * ps_ref[:, ls]
        yd_ref[:, ls] = y.astype(BF16)
    pnew_ref[...] = pext[t_true:t_true + POOL_HIST16, :]
    pext[0:POOL_HIST16, :] = pext[tt:tt + POOL_HIST16, :]


def conv_pool(P, B, T, cbuf8, pbuf16, conv_w, pool_w, pool_scale, l, tt, t_true, pos0, out_rows):
    nt = T // tt

    def col(c):
        return pl.BlockSpec((tt, 512), lambda b, t: (b * nt + t, c // 512))

    return pl.pallas_call(
        functools.partial(_conv_pool_kernel, tt=tt, t_true=t_true, pos0=pos0),
        grid=(B, nt),
        in_specs=[
            col(COL_CB), col(COL_CC), col(COL_CH), col(COL_POOL),
            pl.BlockSpec((None, CONV_HIST, 512), lambda b, t: (b, 0, 0)),
            pl.BlockSpec((None, POOL_HIST16, 512), lambda b, t: (b, 0, 0)),
            pl.BlockSpec((None, 3, 512), lambda b, t: (l, 0, 0)),
            pl.BlockSpec((None, 4, LANES, LANES), lambda b, t: (l, 0, 0, 0)),
            pl.BlockSpec((None, 1, 512), lambda b, t: (l, 0, 0)),
        ],
        out_specs=[
            pl.BlockSpec((tt, 512), lambda b, t: (b * nt + t, 0)),
            pl.BlockSpec((tt, 512), lambda b, t: (b * nt + t, 0)),
            pl.BlockSpec((None, 2, 512), lambda b, t: (b, 0, 0)),
            pl.BlockSpec((None, POOL_HIST16, 512), lambda b, t: (b, 0, 0)),
        ],
        out_shape=[
            jax.ShapeDtypeStruct((out_rows, 512), BF16),
            jax.ShapeDtypeStruct((out_rows, 512), BF16),
            jax.ShapeDtypeStruct((B, 2, 512), F32),
            jax.ShapeDtypeStruct((B, POOL_HIST16, 512), F32),
        ],
        scratch_shapes=[pltpu.VMEM((CONV_HIST + tt, 512), F32), pltpu.VMEM((POOL_HIST16 + tt, 512), F32)],
        compiler_params=_cparams(("parallel", "arbitrary")),
    )(P, P, P, P, cbuf8, pbuf16, conv_w, pool_w, pool_scale)


GLA_UNROLL = 4

def _gla_kernel(q_ref, k_ref, v_ref, r_ref, lr_ref, wa_ref, ba_ref, gn_ref, s0_ref,
                yc_ref, sfin_ref, s_scr, b_scr, sprev_scr, *, tt, C, c_true):
    assert c_true == C or tt == C
    t = pl.program_id(1)

    @pl.when(t == 0)
    def _():
        s_scr[...] = s0_ref[...]

    row = lax.broadcasted_iota(jnp.int32, (C, 1), 0)
    lane = lax.broadcasted_iota(jnp.int32, (1, LANES), 1)
    head0 = lane < GLA_DK
    srow = lax.broadcasted_iota(jnp.int32, (LANES, 1), 0)
    eye = srow == lane
    gn = gn_ref[...]
    nchunk = tt // C

    def chunk_rows(c):
        return pl.ds(pl.multiple_of(c * C, C), C)

    def v_head(rows, h):
        v_h = v_ref[rows, h * LANES:(h + 1) * LANES]
        if c_true < C:
            v_h = jnp.where(row < c_true, v_h, 0.0)
        return v_h

    la = jnp.dot(lr_ref[...].astype(BF16), wa_ref[...], preferred_element_type=F32) + ba_ref[...]
    la = jax.nn.log_sigmoid(la) / GLA_TAU
    if c_true < C:
        la = jnp.where(row < c_true, la, 0.0)
    cb = 2 * C if nchunk % 2 == 0 else C
    bi = lax.broadcasted_iota(jnp.int32, (cb, 1), 0)
    bj = lax.broadcasted_iota(jnp.int32, (1, cb), 1)
    ones_tri = (((bi >= C) == (bj >= C)) & (bj <= bi)).astype(BF16)
    for blk in range(tt // cb):
        part = la[blk * cb:(blk + 1) * cb, :]
        acc = None
        for _ in range(3):
            term = part.astype(BF16)
            d = jnp.dot(ones_tri, term, preferred_element_type=F32)
            acc = d if acc is None else acc + d
            part = part - term.astype(F32)
        b_scr[blk * cb:(blk + 1) * cb, :] = acc

    def by_head(x):
        return jnp.concatenate([jnp.where(head0, x, 0.0), jnp.where(head0, 0.0, x)], axis=0)

    def v_pair(rows, p):
        return jnp.concatenate([v_head(rows, 2 * p), v_head(rows, 2 * p + 1)], axis=0).astype(BF16)

    def scan(c, carry):
        rows = chunk_rows(c)
        b = b_scr[rows, :]
        b_last = b[C - 1:C, :]
        kend = k_ref[rows, :] * jnp.exp(b_last - b)
        for p in range(2):
            ls = slice(p * LANES, (p + 1) * LANES)
            s_p = s_scr[ls, :]
            sprev_scr[c, ls, :] = s_p.astype(BF16)
            dec = jnp.exp(jnp.sum(jnp.where(eye, b_last[:, ls], 0.0), axis=1, keepdims=True))
            ds = lax.dot_general(by_head(kend[:, ls]).astype(BF16), v_pair(rows, p), (((0,), (0,)), ((), ())),
                                 preferred_element_type=F32)
            s_scr[ls, :] = dec * s_p + ds
        return carry

    lax.fori_loop(0, nchunk, scan, 0, unroll=min(2, nchunk))

    ri = lax.broadcasted_iota(jnp.int32, (2 * C, 1), 0)
    ci = lax.broadcasted_iota(jnp.int32, (1, 2 * C), 1)
    same_head_causal = ((ri >= C) == (ci >= C)) & ((ri % C) >= (ci % C))

    def emit(c, carry):
        rows = chunk_rows(c)
        b = b_scr[rows, :]
        qt = q_ref[rows, :] * (GLA_DK ** -0.5) * jnp.exp(b)
        kt = k_ref[rows, :] * jnp.exp(-b)
        for p in range(2):
            ls = slice(p * LANES, (p + 1) * LANES)
            qm = by_head(qt[:, ls]).astype(BF16)
            kt2 = jnp.concatenate([kt[:, ls], kt[:, ls]], axis=0).astype(BF16)
            a = jnp.where(same_head_causal, _dot_nt(qm, kt2), 0.0)
            o = jnp.dot(a.astype(BF16), v_pair(rows, p), preferred_element_type=F32)
            o = o + jnp.dot(qm, sprev_scr[c, ls, :], preferred_element_type=F32)
            on = _rms(o) * gn
            for hh in range(2):
                hs = slice((2 * p + hh) * LANES, (2 * p + hh + 1) * LANES)
                r_h = r_ref[rows, hs]
                yc_ref[rows, hs] = (on[hh * C:(hh + 1) * C, :] * (r_h * jax.nn.sigmoid(r_h))).astype(BF16)
        return carry

    lax.fori_loop(0, nchunk, emit, 0, unroll=min(GLA_UNROLL, nchunk))
    sfin_ref[...] = s_scr[...]


def gla(P, B, T, wa_pad, b_a, gla_norm, s0, l, tt, C, c_true, out_rows):
    nt = T // tt
    return pl.pallas_call(
        functools.partial(_gla_kernel, tt=tt, C=C, c_true=c_true),
        grid=(B, nt),
        in_specs=[
            pl.BlockSpec((tt, 256), lambda b, t: (b * nt + t, COL_GQ // 256)),
            pl.BlockSpec((tt, 256), lambda b, t: (b * nt + t, COL_GK // 256)),
            pl.BlockSpec((tt, 512), lambda b, t: (b * nt + t, COL_GV // 512)),
            pl.BlockSpec((tt, 512), lambda b, t: (b * nt + t, COL_GR // 512)),
            pl.BlockSpec((tt, LANES), lambda b, t: (b * nt + t, COL_LR // LANES)),
            pl.BlockSpec((None, LANES, 256), lambda b, t: (l, 0, 0)),
            pl.BlockSpec((None, 1, 256), lambda b, t: (l, 0, 0)),
            pl.BlockSpec((None, 1, LANES), lambda b, t: (l, 0, 0)),
            pl.BlockSpec((None, 256, LANES), lambda b, t: (b, 0, 0)),
        ],
        out_specs=[
            pl.BlockSpec((tt, 512), lambda b, t: (b * nt + t, 0)),
            pl.BlockSpec((None, 256, LANES), lambda b, t: (b, 0, 0)),
        ],
        out_shape=[jax.ShapeDtypeStruct((out_rows, 512), BF16), jax.ShapeDtypeStruct((B, 256, LANES), F32)],
        scratch_shapes=[pltpu.VMEM((256, LANES), F32), pltpu.VMEM((tt, 256), F32),
                        pltpu.VMEM((tt // C, 256, LANES), BF16)],
        compiler_params=_cparams(("parallel", "arbitrary")),
    )(P, P, P, P, P, wa_pad, b_a, gla_norm, s0)


def _head_norm(x, gain, head0):
    x2 = x * x
    s0 = jnp.sum(jnp.where(head0, x2, 0.0), axis=-1, keepdims=True)
    s1 = jnp.sum(jnp.where(head0, 0.0, x2), axis=-1, keepdims=True)
    ms = jnp.where(head0, s0, s1) / float(HEAD_DIM)
    return x * lax.rsqrt(ms + EPS) * gain


NORM_ROWS = 512
ATTN_UNROLL = 4
QK_SCALE = 0.125


def _attn_prompt_kernel(q_ref, k_ref, v_ref, qg_ref, kg_ref, sl_ref, *rest, S, dil, n_keep, head_base, chained):
    if chained:
        rest = rest[1:]
    o_ref, lse_ref, kv_ref, tmp, qd, kd, vd, od, ld, bias = rest
    msub = S // dil
    nb = msub // NK
    kstride = msub + NK
    lane = lax.broadcasted_iota(jnp.int32, (1, LANES), 1)
    head0 = lane < HEAD_DIM

    def norm_into_tmp(src_ref, g_ref):
        def body(i, carry):
            rows = pl.ds(pl.multiple_of(i * NORM_ROWS, NORM_ROWS), NORM_ROWS)
            tmp[rows, :] = _head_norm(src_ref[rows, :], g_ref[...], head0)
            return carry
        lax.fori_loop(0, S // NORM_ROWS, body, 0)

    norm_into_tmp(q_ref, qg_ref)
    for r in range(dil):
        qd[r * msub:(r + 1) * msub, :] = (tmp[pl.ds(r, msub, stride=dil), :] * QK_SCALE).astype(BF16)
    norm_into_tmp(k_ref, kg_ref)
    kv_ref[0] = tmp[S - n_keep:S, :].T
    kv_ref[1] = v_ref[S - n_keep:S, :].T
    zeros = jnp.zeros((NK, LANES), BF16)
    for r in range(dil):
        kd[r * kstride:r * kstride + NK, :] = zeros
        vd[r * kstride:r * kstride + NK, :] = zeros
        kd[r * kstride + NK:(r + 1) * kstride, :] = tmp[pl.ds(r, msub, stride=dil), :].astype(BF16)
        vd[r * kstride + NK:(r + 1) * kstride, :] = v_ref[pl.ds(r, msub, stride=dil), :].astype(BF16)

    qi = lax.broadcasted_iota(jnp.int32, (NK, 2 * NK), 0)
    ki = lax.broadcasted_iota(jnp.int32, (NK, 2 * NK), 1)
    dist = qi - ki + NK
    band = (dist >= 0) & (dist <= NK)
    alibi = (dist * dil).astype(F32)
    cur = ki >= NK
    for hh in range(2):
        slope = sl_ref[head_base + 2 * pl.program_id(1) + hh]
        bias[2 * hh] = jnp.where(band, -(slope * alibi), NEG)
        bias[2 * hh + 1] = jnp.where(band & cur, -(slope * alibi), NEG)

    def block(j, carry):
        r = j // nb
        n = j - r * nb
        first = jnp.where(n == 0, 1, 0)
        qb = pl.multiple_of(j * NK, NK)
        kb = pl.multiple_of(j * NK + r * NK, NK)
        q2 = qd[pl.ds(qb, NK), :]
        k2 = kd[pl.ds(kb, 2 * NK), :]
        v2 = vd[pl.ds(kb, 2 * NK), :]
        outs, lses = [], []
        for hh in range(2):
            hmask = head0 if hh == 0 else jnp.logical_not(head0)
            qm = jnp.where(hmask, q2, jnp.zeros_like(q2))
            s = lax.dot_general(qm, k2, (((1,), (1,)), ((), ())), preferred_element_type=F32)
            s = s + bias[2 * hh + first]
            m = jnp.max(s, axis=-1, keepdims=True)
            p = jnp.exp(s - m)
            lsum = jnp.sum(p, axis=-1, keepdims=True)
            outs.append(jnp.dot(p.astype(BF16), v2, preferred_element_type=F32) / lsum)
            lses.append(m + jnp.log(lsum))
        od[pl.ds(qb, NK), :] = jnp.where(head0, outs[0], outs[1])
        ld[pl.ds(qb, NK), :] = jnp.where(head0, lses[0], lses[1])
        return carry

    lax.fori_loop(0, dil * nb, block, 0, unroll=ATTN_UNROLL)

    for r in range(dil):
        o_ref[pl.ds(r, msub, stride=dil), :] = od[r * msub:(r + 1) * msub, :]
        lse_ref[pl.ds(r, msub, stride=dil), :] = ld[r * msub:(r + 1) * msub, :]


def attn_prompt(P, B, S, q_gain, k_gain, slopes, l, g, win_prev):
    window, dil = ATTN_GROUPS[g]
    n_keep = min(window, S)
    chained = win_prev is not None

    def col(c):
        return pl.BlockSpec((S, LANES), lambda b, p: (b, c // LANES + 2 * g + p))

    def par(stack):
        return pl.BlockSpec((None, 1, LANES), lambda b, p: (stack * 6 + 2 * g + p, 0, 0))

    in_specs = [col(COL_AQ), col(COL_AK), col(COL_AV), par(l), par(l), pl.BlockSpec(memory_space=pltpu.SMEM)]
    args = [P, P, P, q_gain, k_gain, slopes]
    if chained:
        in_specs.append(pl.BlockSpec(memory_space=pl.ANY))
        args.append(win_prev)
    return pl.pallas_call(
        functools.partial(_attn_prompt_kernel, S=S, dil=dil, n_keep=n_keep, head_base=4 * g, chained=chained),
        grid=(B, 2),
        in_specs=in_specs,
        out_specs=[
            pl.BlockSpec((None, S, LANES), lambda b, p: (b, 0, p)),
            pl.BlockSpec((None, S, LANES), lambda b, p: (b, 0, p)),
            pl.BlockSpec((None, 2, LANES, n_keep), lambda b, p: (l * B + b, 0, p, 0)),
        ],
        out_shape=[
            jax.ShapeDtypeStruct((B, S, 256), F32),
            jax.ShapeDtypeStruct((B, S, 256), F32),
            jax.ShapeDtypeStruct((DEPTH * B, 2, 256, n_keep), F32),
        ],
        input_output_aliases={6: 2} if chained else {},
        scratch_shapes=[
            pltpu.VMEM((S, LANES), F32),
            pltpu.VMEM((S, LANES), BF16),
            pltpu.VMEM((S + dil * NK, LANES), BF16),
            pltpu.VMEM((S + dil * NK, LANES), BF16),
            pltpu.VMEM((S, LANES), F32),
            pltpu.VMEM((S, LANES), F32),
            pltpu.VMEM((4, NK, 2 * NK), F32),
        ],
        compiler_params=_cparams(("parallel", "arbitrary")),
    )(*args)


T_PAD = 8


def _attn_sample_kernel(q_ref, k_ref, v_ref, qg_ref, kg_ref, sl_ref, cache_ref, *rest,
                        L, dil, t_true, head_base, chained):
    if chained:
        rest = rest[1:]
    o_ref, lse_ref, cnew_ref = rest
    lane = lax.broadcasted_iota(jnp.int32, (1, LANES), 1)
    head0 = lane < HEAD_DIM
    trow8 = lax.broadcasted_iota(jnp.int32, (T_PAD, LANES), 0)
    lane8 = lax.broadcasted_iota(jnp.int32, (T_PAD, LANES), 1)
    place = ((lane8 == LANES - t_true + trow8) & (trow8 < t_true)).astype(F32)
    is_new = lane >= LANES - t_true

    def rolled_with_new(rows, new):
        shifted = pltpu.roll(cache_ref[rows, :], L - t_true, 1)
        tail = lax.dot_general(new, place, (((0,), (0,)), ((), ())), preferred_element_type=F32,
                               precision=lax.Precision.HIGHEST)
        if L > LANES:
            cnew_ref[rows, 0:L - LANES] = shifted[:, 0:L - LANES]
        cnew_ref[rows, L - LANES:L] = jnp.where(is_new, tail, shifted[:, L - LANES:L])

    trow = lax.broadcasted_iota(jnp.int32, (T_PAD, 1), 0)
    scol = lax.broadcasted_iota(jnp.int32, (1, L), 1)
    dist = L + trow - scol
    valid_c = ((dist & (dil - 1)) == 0) & (dist <= NK * dil)
    alibi_c = dist.astype(F32)

    for p in range(2):
        ls = slice(p * LANES, (p + 1) * LANES)
        qn = _head_norm(q_ref[:, ls], qg_ref[:, ls], head0)
        kn = _head_norm(k_ref[:, ls], kg_ref[:, ls], head0)
        vn = v_ref[:, ls]
        krows = slice(p * LANES, (p + 1) * LANES)
        vrows = slice(256 + p * LANES, 256 + (p + 1) * LANES)
        rolled_with_new(krows, kn)
        rolled_with_new(vrows, vn)
        kc_t = cache_ref[krows, :].astype(BF16)
        vc_t = cache_ref[vrows, :].astype(BF16)
        knr = kn.astype(BF16).astype(F32)
        vnr = vn.astype(BF16).astype(F32)
        outs, lses = [], []
        for hh in range(2):
            hmask = head0 if hh == 0 else jnp.logical_not(head0)
            qm = jnp.where(hmask, qn, 0.0).astype(BF16)
            qmr = qm.astype(F32)
            slope = sl_ref[head_base + 2 * p + hh]
            sc = jnp.dot(qm, kc_t, preferred_element_type=F32)
            sc = sc / float(np.sqrt(HEAD_DIM)) - slope * alibi_c
            sc = jnp.where(valid_c, sc, NEG)
            m = jnp.max(sc, axis=-1, keepdims=True)
            sn = []
            for u in range(t_true):
                du = trow - u
                su = jnp.sum(qmr * knr[u:u + 1, :], axis=-1, keepdims=True) / float(np.sqrt(HEAD_DIM))
                su = su - slope * du.astype(F32)
                su = jnp.where((du >= 0) & ((du & (dil - 1)) == 0), su, NEG)
                sn.append(su)
                m = jnp.maximum(m, su)
            pc = jnp.exp(sc - m)
            lsum = jnp.sum(pc, axis=-1, keepdims=True)
            acc = _dot_nt(pc.astype(BF16), vc_t)
            for u in range(t_true):
                pu = jnp.exp(sn[u] - m)
                lsum = lsum + pu
                acc = acc + pu.astype(BF16).astype(F32) * vnr[u:u + 1, :]
            outs.append(acc / lsum)
            lses.append(m + jnp.log(lsum))
        o_ref[:, ls] = jnp.where(head0, outs[0], outs[1])
        lse_ref[:, ls] = jnp.where(head0, lses[0], lses[1])


def attn_sample(P, B, q_gain, k_gain, slopes, cache_t, l, g, t_true, win_prev):
    window, dil = ATTN_GROUPS[g]
    L = cache_t.shape[2]
    chained = win_prev is not None

    def col(c):
        return pl.BlockSpec((T_PAD, 256), lambda b: (b, c // 256 + g))

    def par(stack):
        return pl.BlockSpec((None, 1, 256), lambda b: (stack * 3 + g, 0, 0))

    in_specs = [col(COL_AQ), col(COL_AK), col(COL_AV), par(l), par(l), pl.BlockSpec(memory_space=pltpu.SMEM),
                pl.BlockSpec((None, 512, L), lambda b: (l * B + b, 0, 0))]
    args = [P, P, P, q_gain, k_gain, slopes, cache_t]
    if chained:
        in_specs.append(pl.BlockSpec(memory_space=pl.ANY))
        args.append(win_prev)
    return pl.pallas_call(
        functools.partial(_attn_sample_kernel, L=L, dil=dil, t_true=t_true, head_base=4 * g, chained=chained),
        grid=(B,),
        in_specs=in_specs,
        out_specs=[
            pl.BlockSpec((None, T_PAD, 256), lambda b: (b, 0, 0)),
            pl.BlockSpec((None, T_PAD, 256), lambda b: (b, 0, 0)),
            pl.BlockSpec((None, 512, L), lambda b: (l * B + b, 0, 0)),
        ],
        out_shape=[
            jax.ShapeDtypeStruct((B, T_PAD, 256), F32),
            jax.ShapeDtypeStruct((B, T_PAD, 256), F32),
            jax.ShapeDtypeStruct((DEPTH * B, 512, L), F32),
        ],
        input_output_aliases={7: 2} if chained else {},
        compiler_params=_cparams(("parallel",)),
    )(*args)


def _attn_merge_kernel(o0, l0, o1, l1, o2, l2, y_ref):
    a0, a1, a2 = l0[...], l1[...], l2[...]
    m = jnp.maximum(jnp.maximum(a0, a1), a2)
    e0, e1, e2 = jnp.exp(a0 - m), jnp.exp(a1 - m), jnp.exp(a2 - m)
    den = e0 + e1 + e2
    y = (e0 / den) * o0[...] + (e1 / den) * o1[...] + (e2 / den) * o2[...]
    y_ref[...] = y.astype(BF16)


def attn_merge(parts, tm, out_rows):
    M = parts[0].shape[0]
    spec = pl.BlockSpec((tm, 256), lambda m: (m, 0))
    return pl.pallas_call(
        _attn_merge_kernel,
        grid=(M // tm,),
        in_specs=[spec] * 6,
        out_specs=spec,
        out_shape=jax.ShapeDtypeStruct((out_rows, 256), BF16),
        compiler_params=_cparams(("parallel",)),
    )(*parts)


GATE_SRC0 = 5904
assert GATE_SRC0 % SRC_ALIGN == 0 and D_MODEL % SRC_ALIGN == 0


def _merge_kernel(xn_ref, ya_ref, yb_ref, yc_ref, yd_ref, g0, g1, g2, g3, u0, u1, u2, u3, o_ref,
                  gs0, gs1, gs2, gs3, us0, us1, us2, us3, *, tiles, tm):
    branches = ((ya_ref, g0, u0, gs0, us0), (yb_ref, g1, u1, gs1, us1),
                (yc_ref, g2, u2, gs2, us2), (yd_ref, g3, u3, gs3, us3))
    m = pl.program_id(1)

    @pl.when(m == 0)
    def _():
        for _, g_ref, u_ref, g_s, u_s in branches:
            g_s[...] = g_ref[...].astype(BF16)
            u_s[...] = u_ref[...].astype(BF16)

    def rows_fn(rows):
        xn = xn_ref[rows, :]
        acc = None
        for y_ref, _, _, g_s, u_s in branches:
            gate = jax.nn.sigmoid(_dot_nt(xn, g_s[...]))
            term = gate * jnp.dot(y_ref[rows, :], u_s[...], preferred_element_type=F32)
            acc = term if acc is None else acc + term
        o_ref[rows, :] = acc.astype(BF16)

    _on_tile_rows(m, tiles, tm, rows_fn)


def merge(xn, ya, yb, yc, yd, w_in_t, ups, l, tm):
    M = xn.shape[0]
    tiles = _row_tiles(M, tm)
    tn = PREP_ROWS
    widths = (512, 256, 512, 512)

    def act(width):
        return pl.BlockSpec((tm, width), lambda n, m: (m, 0))

    def gate(b):
        base = (GATE_SRC0 + b * D_MODEL) // SRC_ALIGN
        return _w_in_rows(l, lambda n, m: base + n * (tn // SRC_ALIGN))

    def up(width):
        return pl.BlockSpec((None, width, tn), lambda n, m: (l, 0, n))

    return pl.pallas_call(
        functools.partial(_merge_kernel, tiles=tiles, tm=tm),
        grid=(D_MODEL // tn, tiles[0]),
        in_specs=[act(D_MODEL)] + [act(w) for w in widths] + [gate(b) for b in range(N_BRANCH)]
                 + [up(w) for w in widths],
        out_specs=pl.BlockSpec((tm, tn), lambda n, m: (m, n)),
        out_shape=jax.ShapeDtypeStruct((M, D_MODEL), BF16),
        scratch_shapes=[pltpu.VMEM((tn, D_MODEL), BF16)] * N_BRANCH + [pltpu.VMEM((w, tn), BF16) for w in widths],
        compiler_params=_cparams(("parallel", "arbitrary")),
    )(xn, ya, yb, yc, yd, w_in_t, w_in_t, w_in_t, w_in_t, *ups)


def _layer(x, xn, W, l, T, mixers):
    h = ffn_up(xn, W['ff1_gate'], W['ff1_up'], l, T['up_m'], T['up_n'])
    x, xn = matmul_res(x, h, W['ff1_down'], l, 0.5, T['down_m'], W['n_mix'], l)
    P = mix_in(xn, W['w_in_t'], l, T['mix_m'])
    (ya, yb, yc, yd), states = mixers(P, l)
    mg = merge(xn, ya, yb, yc, yd, W['w_in_t'], (W['up_a'], W['up_b'], W['up_c'], W['up_d']), l, T['merge_m'])
    x, xn = matmul_res(x, mg, W['w_out'], l, 1.0, T['out_m'], W['n_ff2'], l)
    h = ffn_up(xn, W['ff2_gate'], W['ff2_up'], l, T['up_m'], T['up_n'])
    if l + 1 < DEPTH:
        x, xn = matmul_res(x, h, W['ff2_down'], l, 0.5, T['down_m'], W['n_ff1'], l + 1)
    else:
        x, xn = matmul_res(x, h, W['ff2_down'], l, 0.5, T['down_m'])
    return x, xn, states


PROMPT_TILES = dict(up_m=1024, up_n=512, mix_m=2048, down_m=256, out_m=512, merge_m=1024)


def kernel(x_prompt, x_sample, state_conv, cache_w128_kv, cache_w512_kv, cache_w2048_kv, state_gla, state_pool,
           norm_ff1, ff1_gate, ff1_up, ff1_down, norm_mix, w_in, conv_w, attn_q_gain, attn_k_gain,
           gla_w_a2, gla_b_a, gla_norm, pool_w, pool_scale, w_up_conv, w_up_attn, w_up_gla, w_up_pool, w_out,
           norm_ff2, ff2_gate, ff2_up, ff2_down):
    BP, S, _ = x_prompt.shape
    BS, TS, _ = x_sample.shape
    caches = (cache_w128_kv, cache_w512_kv, cache_w2048_kv)

    W = {
        'n_ff1': norm_ff1.reshape(DEPTH, 1, D_MODEL), 'n_mix': norm_mix.reshape(DEPTH, 1, D_MODEL),
        'n_ff2': norm_ff2.reshape(DEPTH, 1, D_MODEL),
        'ff1_gate': ff1_gate, 'ff1_up': ff1_up, 'ff1_down': ff1_down.astype(BF16),
        'ff2_gate': ff2_gate, 'ff2_up': ff2_up, 'ff2_down': ff2_down.astype(BF16),
        'w_in_t': jnp.transpose(w_in, (0, 2, 1)),
        'up_a': w_up_conv, 'up_b': w_up_attn, 'up_c': w_up_gla, 'up_d': w_up_pool, 'w_out': w_out.astype(BF16),
    }
    wa_pad = jnp.pad(gla_w_a2, ((0, 0), (0, LANES - gla_w_a2.shape[1]), (0, 0))).astype(BF16)
    b_a = gla_b_a.reshape(DEPTH, 1, 256)
    gn = gla_norm.reshape(DEPTH, 1, LANES)
    pool_wb = pool_w.astype(BF16)
    pool_sc = pool_scale.reshape(DEPTH, 1, W_POOL)
    i = jnp.arange(1, N_ATTN_HEADS + 1, dtype=F32)
    slopes = jnp.exp2(-8.0 * i / N_ATTN_HEADS)
    qg6 = attn_q_gain.reshape(DEPTH * 6, 1, LANES)
    kg6 = attn_k_gain.reshape(DEPTH * 6, 1, LANES)
    qg3 = attn_q_gain.reshape(DEPTH * 3, 1, 256)
    kg3 = attn_k_gain.reshape(DEPTH * 3, 1, 256)

    zc = jnp.zeros((BP, CONV_HIST, W_CONV), F32)
    zp = jnp.zeros((BP, POOL_HIST16, W_POOL), F32)
    zs = jnp.zeros((BP, 256, LANES), F32)

    win_p = [None, None, None]
    win_s = [None, None, None]

    MP, MS = BP * S, BS * TS
    M_ALL = MP + MS

    def prompt_mixers(P, l):
        ya, yd, cnew, pnew = conv_pool(P, BP, S, zc, zp, conv_w, pool_wb, pool_sc, l, 512, 512, 0, M_ALL)
        yc, sfin = gla(P, BP, S, wa_pad, b_a, gn, zs, l, 512, GLA_CHUNK, GLA_CHUNK, M_ALL)
        parts = []
        for g in range(3):
            o, lse, win_p[g] = attn_prompt(P, BP, S, qg6, kg6, slopes, l, g, win_p[g])
            parts += [o.reshape(MP, 256), lse.reshape(MP, 256)]
        yb = attn_merge(parts, 1024, M_ALL)
        return (ya, yb, yc, yd), (cnew, sfin.reshape(BP, 4, GLA_DK, LANES), pnew[:, 1:])

    sconv8 = jnp.pad(state_conv, ((0, 0), (0, 0), (CONV_HIST - 2, 0), (0, 0)))
    spool16 = jnp.pad(state_pool, ((0, 0), (0, 0), (1, 0), (0, 0)))
    sgla = state_gla.reshape(DEPTH, BS, 256, LANES)
    cviews = [jnp.transpose(cc, (0, 1, 3, 4, 5, 2)).reshape(DEPTH * BS, 512, cc.shape[2]) for cc in caches]

    def sample_mixers(P, l):
        MSP = BS * T_PAD
        Ps = jnp.pad(P[MP:].reshape(BS, TS, N_MIX), ((0, 0), (0, T_PAD - TS), (0, 0))).reshape(MSP, N_MIX)
        ya, yd, cnew, pnew = conv_pool(Ps, BS, T_PAD, sconv8[l], spool16[l], conv_w, pool_wb, pool_sc, l,
                                       T_PAD, TS, PAST_LEN, MSP)
        yc, sfin = gla(Ps, BS, T_PAD, wa_pad, b_a, gn, sgla[l], l, T_PAD, T_PAD, TS, MSP)
        parts = []
        for g in range(3):
            o, lse, win_s[g] = attn_sample(Ps, BS, qg3, kg3, slopes, cviews[g], l, g, TS, win_s[g])
            parts += [o[:, :TS].reshape(MS, 256), lse[:, :TS].reshape(MS, 256)]
        yb = attn_merge(parts, MS, MS)

        def valid(y):
            return y.reshape(BS, T_PAD, -1)[:, :TS].reshape(MS, -1)

        return (valid(ya), yb, valid(yc), valid(yd)), (cnew, sfin.reshape(BS, 4, GLA_DK, LANES), pnew[:, 1:])

    def mixers(P, l):
        ys_p, st_prompt = prompt_mixers(P, l)
        ys_s, st_sample = sample_mixers(P, l)
        ys = tuple(lax.dynamic_update_slice(yp_, ys_, (MP, 0)) for yp_, ys_ in zip(ys_p, ys_s))
        return ys, (st_prompt, st_sample)

    x = jnp.concatenate([x_prompt.reshape(MP, D_MODEL), x_sample.reshape(MS, D_MODEL)], axis=0)
    xn = norm(x, W['n_ff1'], 0, 1024)
    st_p, st_s = [], []
    for l in range(DEPTH):
        x, xn, (sp, ss) = _layer(x, xn, W, l, PROMPT_TILES, mixers)
        st_p.append(sp)
        st_s.append(ss)
    yp, ys = x[:MP], x[MP:]

    def stack(sts, f):
        return jnp.stack([f(s) for s in sts])

    def window_out(buf, B):
        n = buf.shape[-1]
        return jnp.transpose(buf.reshape(DEPTH, B, 2, 4, HEAD_DIM, n), (0, 1, 5, 2, 3, 4))

    outs = [yp.reshape(BP, S, D_MODEL), ys.reshape(BS, TS, D_MODEL),
            stack(st_p, lambda s: s[0]), stack(st_s, lambda s: s[0])]
    for g in range(3):
        outs += [window_out(win_p[g], BP), window_out(win_s[g], BS)]
    outs += [stack(st_p, lambda s: s[1]), stack(st_s, lambda s: s[1]),
             stack(st_p, lambda s: s[2]), stack(st_s, lambda s: s[2])]
    return tuple(outs)
```

```python
import functools

import numpy as np
import jax
import jax.numpy as jnp
from jax import lax
from jax.experimental import pallas as pl
from jax.experimental.pallas import tpu as pltpu

BF16 = jnp.bfloat16
F32 = jnp.float32

D_MODEL = 2048
DEPTH = 4
PAST_LEN = 16384
D_FF = 5504
EPS = 1e-6
N_BRANCH = 4
W_CONV = 512
ATTN_GROUPS = ((128, 1), (512, 4), (2048, 16))
N_ATTN_HEADS = 12
HEAD_DIM = 64
GLA_DK = 64
GLA_TAU = 16.0
GLA_CHUNK = 64
POOL_WINDOWS = (2, 4, 8, 16)
W_POOL = 512
NK = 128
LANES = 128
NEG = -1e30

COL_CB, COL_CC, COL_CH, COL_GV, COL_GR, COL_POOL = 0, 512, 1024, 1536, 2048, 2560
COL_AQ, COL_AK, COL_AV = 3072, 3840, 4608
COL_GQ, COL_GK, COL_LR = 5376, 5632, 5888
N_MIX = 6144
N_IN = 14096
N_GATE = N_BRANCH * D_MODEL
PREP_ROWS = 256
MIX_TILE_SRC = ([256 * j for j in range(6)] + [4352, 4608, 4864, 5120, 5392, 5648]
                + [1536 + 256 * j for j in range(9)] + [3840, 4096, 5376])
SRC_ALIGN = 16
assert all(r % SRC_ALIGN == 0 for r in MIX_TILE_SRC)

VMEM_LIMIT = 56 * 1024 * 1024


def _cparams(sem):
    return pltpu.CompilerParams(dimension_semantics=sem, vmem_limit_bytes=VMEM_LIMIT)


def _rms(x):
    return x * lax.rsqrt(jnp.mean(x * x, axis=-1, keepdims=True) + EPS)


def _row_tiles(M, tm):
    n = pl.cdiv(M, tm)
    return n, M - (n - 1) * tm


def _on_tile_rows(m, tiles, tm, fn):
    n, tail = tiles
    if tail == tm:
        fn(slice(0, tm))
        return
    pl.when(m < n - 1)(lambda: fn(slice(0, tm)))
    pl.when(m == n - 1)(lambda: fn(slice(0, tail)))


def _norm_kernel(x_ref, g_ref, xn_ref, *, tiles, tm):
    def rows_fn(rows):
        xn_ref[rows, :] = (_rms(x_ref[rows, :]) * g_ref[...]).astype(BF16)

    _on_tile_rows(pl.program_id(0), tiles, tm, rows_fn)


def norm(x, gain, l, tm):
    M = x.shape[0]
    tiles = _row_tiles(M, tm)
    return pl.pallas_call(
        functools.partial(_norm_kernel, tiles=tiles, tm=tm),
        grid=(tiles[0],),
        in_specs=[pl.BlockSpec((tm, D_MODEL), lambda m: (m, 0)),
                  pl.BlockSpec((None, 1, D_MODEL), lambda m: (l, 0, 0))],
        out_specs=pl.BlockSpec((tm, D_MODEL), lambda m: (m, 0)),
        out_shape=jax.ShapeDtypeStruct((M, D_MODEL), BF16),
        compiler_params=_cparams(("parallel",)),
    )(x, gain)


def _ffn_up_kernel(xn_ref, wg_ref, wu_ref, h_ref, wg_s, wu_s, *, tiles, tm):
    m = pl.program_id(1)

    @pl.when(m == 0)
    def _():
        wg_s[...] = wg_ref[...].astype(BF16)
        wu_s[...] = wu_ref[...].astype(BF16)

    def rows_fn(rows):
        xn = xn_ref[rows, :]
        a = jnp.dot(xn, wg_s[...], preferred_element_type=F32)
        b = jnp.dot(xn, wu_s[...], preferred_element_type=F32)
        h_ref[rows, :] = (a * jax.nn.sigmoid(a) * b).astype(BF16)

    _on_tile_rows(m, tiles, tm, rows_fn)


def ffn_up(xn, wg, wu, l, tm, tn):
    M = xn.shape[0]
    tiles = _row_tiles(M, tm)
    return pl.pallas_call(
        functools.partial(_ffn_up_kernel, tiles=tiles, tm=tm),
        grid=(pl.cdiv(D_FF, tn), tiles[0]),
        in_specs=[
            pl.BlockSpec((tm, D_MODEL), lambda n, m: (m, 0)),
            pl.BlockSpec((None, D_MODEL, tn), lambda n, m: (l, 0, n)),
            pl.BlockSpec((None, D_MODEL, tn), lambda n, m: (l, 0, n)),
        ],
        out_specs=pl.BlockSpec((tm, tn), lambda n, m: (m, n)),
        out_shape=jax.ShapeDtypeStruct((M, D_FF), BF16),
        scratch_shapes=[pltpu.VMEM((D_MODEL, tn), BF16), pltpu.VMEM((D_MODEL, tn), BF16)],
        compiler_params=_cparams(("parallel", "arbitrary")),
    )(xn, wg, wu)


def _matmul_res_kernel(x_ref, a_ref, w_ref, *rest, scale, with_norm, tiles, tm):
    def rows_fn(rows):
        y = x_ref[rows, :] + scale * jnp.dot(a_ref[rows, :], w_ref[...], preferred_element_type=F32)
        if with_norm:
            g_ref, o_ref, xn_ref = rest
            xn_ref[rows, :] = (_rms(y) * g_ref[...]).astype(BF16)
        else:
            (o_ref,) = rest
        o_ref[rows, :] = y

    _on_tile_rows(pl.program_id(0), tiles, tm, rows_fn)


def matmul_res(x, a, w, l, scale, tm, next_gain=None, next_l=0):
    M, K = a.shape
    with_norm = next_gain is not None
    row = pl.BlockSpec((tm, D_MODEL), lambda m: (m, 0))
    in_specs = [row, pl.BlockSpec((tm, K), lambda m: (m, 0)),
                pl.BlockSpec((None, K, D_MODEL), lambda m: (l, 0, 0), pipeline_mode=pl.Buffered(1))]
    args = [x, a, w]
    out_specs, out_shape = [row], [jax.ShapeDtypeStruct((M, D_MODEL), F32)]
    if with_norm:
        in_specs.append(pl.BlockSpec((None, 1, D_MODEL), lambda m: (next_l, 0, 0)))
        args.append(next_gain)
        out_specs.append(row)
        out_shape.append(jax.ShapeDtypeStruct((M, D_MODEL), BF16))
    tiles = _row_tiles(M, tm)
    res = pl.pallas_call(
        functools.partial(_matmul_res_kernel, scale=scale, with_norm=with_norm, tiles=tiles, tm=tm),
        grid=(tiles[0],),
        in_specs=in_specs,
        out_specs=out_specs,
        out_shape=out_shape,
        compiler_params=_cparams(("parallel",)),
    )(*args)
    return res if with_norm else (res[0], None)


def _dot_nt(a, b):
    return lax.dot_general(a, b, (((1,), (1,)), ((), ())), preferred_element_type=F32)


def _w_in_rows(l, row_of):
    return pl.BlockSpec((None, pl.Element(PREP_ROWS), pl.Element(D_MODEL)),
                        lambda *idx: (l, row_of(*idx) * SRC_ALIGN, 0))


def _mix_in_kernel(src_ref, xn_ref, wa_ref, wb_ref, p_ref, w_s, *, tiles, tm):
    del src_ref
    m = pl.program_id(1)

    @pl.when(m == 0)
    def _():
        w_s[0:PREP_ROWS, :] = wa_ref[...].astype(BF16)
        w_s[PREP_ROWS:2 * PREP_ROWS, :] = wb_ref[...].astype(BF16)

    def rows_fn(rows):
        p_ref[rows, :] = _dot_nt(xn_ref[rows, :], w_s[...])

    _on_tile_rows(m, tiles, tm, rows_fn)


def mix_in(xn, w_in_t, l, tm):
    M = xn.shape[0]
    tn = 2 * PREP_ROWS
    tiles = _row_tiles(M, tm)
    src = jnp.asarray(MIX_TILE_SRC, jnp.int32) // SRC_ALIGN
    return pl.pallas_call(
        functools.partial(_mix_in_kernel, tiles=tiles, tm=tm),
        grid_spec=pltpu.PrefetchScalarGridSpec(
            num_scalar_prefetch=1,
            grid=(N_MIX // tn, tiles[0]),
            in_specs=[
                pl.BlockSpec((tm, D_MODEL), lambda n, m, src: (m, 0)),
                _w_in_rows(l, lambda n, m, src: src[2 * n]),
                _w_in_rows(l, lambda n, m, src: src[2 * n + 1]),
            ],
            out_specs=pl.BlockSpec((tm, tn), lambda n, m, src: (m, n)),
            scratch_shapes=[pltpu.VMEM((tn, D_MODEL), BF16)],
        ),
        out_shape=jax.ShapeDtypeStruct((M, N_MIX), F32),
        compiler_params=_cparams(("parallel", "arbitrary")),
    )(src, xn, w_in_t, w_in_t)


CONV_HIST = 8
POOL_HIST16 = 16


def _conv_pool_kernel(cb_ref, cc_ref, ch_ref, pin_ref, cbuf_ref, pbuf_ref, cw_ref, pw_ref, ps_ref,
                      ya_ref, yd_ref, cnew_ref, pnew_ref, uext, pext, *, tt, t_true, pos0):
    t = pl.program_id(1)

    @pl.when(t == 0)
    def _():
        uext[0:CONV_HIST, :] = cbuf_ref[...]
        pext[0:POOL_HIST16, :] = pbuf_ref[...]

    u = cc_ref[...] * ch_ref[...]
    uext[CONV_HIST:CONV_HIST + tt, :] = u
    w = cw_ref[...]
    z = uext[CONV_HIST - 2:CONV_HIST - 2 + tt, :] * w[0:1, :]
    z = z + uext[CONV_HIST - 1:CONV_HIST - 1 + tt, :] * w[1:2, :]
    z = z + u * w[2:3, :]
    ya_ref[...] = (cb_ref[...] * z).astype(BF16)
    cnew_ref[...] = uext[CONV_HIST + t_true - 2:CONV_HIST + t_true, :]
    uext[0:CONV_HIST, :] = uext[tt:tt + CONV_HIST, :]

    pin = pin_ref[...]
    pext[POOL_HIST16:POOL_HIST16 + tt, :] = pin
    pos = (pos0 + t * tt + lax.broadcasted_iota(jnp.int32, (tt, 1), 0)).astype(F32)
    for g, win in enumerate(POOL_WINDOWS):
        ls = slice(g * LANES, (g + 1) * LANES)
        acc = pin[:, ls]
        for i in range(1, win):
            acc = acc + pext[POOL_HIST16 - i:POOL_HIST16 - i + tt, ls]
        cnt = jnp.minimum(float(win), pos + 1.0)
        d = acc / cnt - pin[:, ls]
        y = jnp.dot(d.astype(BF16), pw_ref[g], preferred_element_type=F32) * ps_ref[:, ls]
        yd_ref[:, ls] = y.astype(BF16)
    pnew_ref[...] = pext[t_true:t_true + POOL_HIST16, :]
    pext[0:POOL_HIST16, :] = pext[tt:tt + POOL_HIST16, :]


def conv_pool(P, B, T, cbuf8, pbuf16, conv_w, pool_w, pool_scale, l, tt, t_true, pos0, out_rows):
    nt = T // tt

    def col(c):
        return pl.BlockSpec((tt, 512), lambda b, t: (b * nt + t, c // 512))

    return pl.pallas_call(
        functools.partial(_conv_pool_kernel, tt=tt, t_true=t_true, pos0=pos0),
        grid=(B, nt),
        in_specs=[
            col(COL_CB), col(COL_CC), col(COL_CH), col(COL_POOL),
            pl.BlockSpec((None, CONV_HIST, 512), lambda b, t: (b, 0, 0)),
            pl.BlockSpec((None, POOL_HIST16, 512), lambda b, t: (b, 0, 0)),
            pl.BlockSpec((None, 3, 512), lambda b, t: (l, 0, 0)),
            pl.BlockSpec((None, 4, LANES, LANES), lambda b, t: (l, 0, 0, 0)),
            pl.BlockSpec((None, 1, 512), lambda b, t: (l, 0, 0)),
        ],
        out_specs=[
            pl.BlockSpec((tt, 512), lambda b, t: (b * nt + t, 0)),
            pl.BlockSpec((tt, 512), lambda b, t: (b * nt + t, 0)),
            pl.BlockSpec((None, 2, 512), lambda b, t: (b, 0, 0)),
            pl.BlockSpec((None, POOL_HIST16, 512), lambda b, t: (b, 0, 0)),
        ],
        out_shape=[
            jax.ShapeDtypeStruct((out_rows, 512), BF16),
            jax.ShapeDtypeStruct((out_rows, 512), BF16),
            jax.ShapeDtypeStruct((B, 2, 512), F32),
            jax.ShapeDtypeStruct((B, POOL_HIST16, 512), F32),
        ],
        scratch_shapes=[pltpu.VMEM((CONV_HIST + tt, 512), F32), pltpu.VMEM((POOL_HIST16 + tt, 512), F32)],
        compiler_params=_cparams(("parallel", "arbitrary")),
    )(P, P, P, P, cbuf8, pbuf16, conv_w, pool_w, pool_scale)


GLA_UNROLL = 4

def _gla_kernel(q_ref, k_ref, v_ref, r_ref, lr_ref, wa_ref, ba_ref, gn_ref, s0_ref,
                yc_ref, sfin_ref, s_scr, b_scr, sprev_scr, *, tt, C, c_true):
    assert c_true == C or tt == C
    t = pl.program_id(1)

    @pl.when(t == 0)
    def _():
        s_scr[...] = s0_ref[...]

    row = lax.broadcasted_iota(jnp.int32, (C, 1), 0)
    lane = lax.broadcasted_iota(jnp.int32, (1, LANES), 1)
    head0 = lane < GLA_DK
    srow = lax.broadcasted_iota(jnp.int32, (LANES, 1), 0)
    eye = srow == lane
    gn = gn_ref[...]
    nchunk = tt // C

    def chunk_rows(c):
        return pl.ds(pl.multiple_of(c * C, C), C)

    def v_head(rows, h):
        v_h = v_ref[rows, h * LANES:(h + 1) * LANES]
        if c_true < C:
            v_h = jnp.where(row < c_true, v_h, 0.0)
        return v_h

    la = jnp.dot(lr_ref[...].astype(BF16), wa_ref[...], preferred_element_type=F32) + ba_ref[...]
    la = jax.nn.log_sigmoid(la) / GLA_TAU
    if c_true < C:
        la = jnp.where(row < c_true, la, 0.0)
    cb = 2 * C if nchunk % 2 == 0 else C
    bi = lax.broadcasted_iota(jnp.int32, (cb, 1), 0)
    bj = lax.broadcasted_iota(jnp.int32, (1, cb), 1)
    ones_tri = (((bi >= C) == (bj >= C)) & (bj <= bi)).astype(BF16)
    for blk in range(tt // cb):
        part = la[blk * cb:(blk + 1) * cb, :]
        acc = None
        for _ in range(3):
            term = part.astype(BF16)
            d = jnp.dot(ones_tri, term, preferred_element_type=F32)
            acc = d if acc is None else acc + d
            part = part - term.astype(F32)
        b_scr[blk * cb:(blk + 1) * cb, :] = acc

    def by_head(x):
        return jnp.concatenate([jnp.where(head0, x, 0.0), jnp.where(head0, 0.0, x)], axis=0)

    def v_pair(rows, p):
        return jnp.concatenate([v_head(rows, 2 * p), v_head(rows, 2 * p + 1)], axis=0).astype(BF16)

    def scan(c, carry):
        rows = chunk_rows(c)
        b = b_scr[rows, :]
        b_last = b[C - 1:C, :]
        kend = k_ref[rows, :] * jnp.exp(b_last - b)
        for p in range(2):
            ls = slice(p * LANES, (p + 1) * LANES)
            s_p = s_scr[ls, :]
            sprev_scr[c, ls, :] = s_p.astype(BF16)
            dec = jnp.exp(jnp.sum(jnp.where(eye, b_last[:, ls], 0.0), axis=1, keepdims=True))
            ds = lax.dot_general(by_head(kend[:, ls]).astype(BF16), v_pair(rows, p), (((0,), (0,)), ((), ())),
                                 preferred_element_type=F32)
            s_scr[ls, :] = dec * s_p + ds
        return carry

    lax.fori_loop(0, nchunk, scan, 0, unroll=min(2, nchunk))

    ri = lax.broadcasted_iota(jnp.int32, (2 * C, 1), 0)
    ci = lax.broadcasted_iota(jnp.int32, (1, 2 * C), 1)
    same_head_causal = ((ri >= C) == (ci >= C)) & ((ri % C) >= (ci % C))

    def emit(c, carry):
        rows = chunk_rows(c)
        b = b_scr[rows, :]
        qt = q_ref[rows, :] * (GLA_DK ** -0.5) * jnp.exp(b)
        kt = k_ref[rows, :] * jnp.exp(-b)
        for p in range(2):
            ls = slice(p * LANES, (p + 1) * LANES)
            qm = by_head(qt[:, ls]).astype(BF16)
            kt2 = jnp.concatenate([kt[:, ls], kt[:, ls]], axis=0).astype(BF16)
            a = jnp.where(same_head_causal, _dot_nt(qm, kt2), 0.0)
            o = jnp.dot(a.astype(BF16), v_pair(rows, p), preferred_element_type=F32)
            o = o + jnp.dot(qm, sprev_scr[c, ls, :], preferred_element_type=F32)
            on = _rms(o) * gn
            for hh in range(2):
                hs = slice((2 * p + hh) * LANES, (2 * p + hh + 1) * LANES)
                r_h = r_ref[rows, hs]
                yc_ref[rows, hs] = (on[hh * C:(hh + 1) * C, :] * (r_h * jax.nn.sigmoid(r_h))).astype(BF16)
        return carry

    lax.fori_loop(0, nchunk, emit, 0, unroll=min(GLA_UNROLL, nchunk))
    sfin_ref[...] = s_scr[...]


def gla(P, B, T, wa_pad, b_a, gla_norm, s0, l, tt, C, c_true, out_rows):
    nt = T // tt
    return pl.pallas_call(
        functools.partial(_gla_kernel, tt=tt, C=C, c_true=c_true),
        grid=(B, nt),
        in_specs=[
            pl.BlockSpec((tt, 256), lambda b, t: (b * nt + t, COL_GQ // 256)),
            pl.BlockSpec((tt, 256), lambda b, t: (b * nt + t, COL_GK // 256)),
            pl.BlockSpec((tt, 512), lambda b, t: (b * nt + t, COL_GV // 512)),
            pl.BlockSpec((tt, 512), lambda b, t: (b * nt + t, COL_GR // 512)),
            pl.BlockSpec((tt, LANES), lambda b, t: (b * nt + t, COL_LR // LANES)),
            pl.BlockSpec((None, LANES, 256), lambda b, t: (l, 0, 0)),
            pl.BlockSpec((None, 1, 256), lambda b, t: (l, 0, 0)),
            pl.BlockSpec((None, 1, LANES), lambda b, t: (l, 0, 0)),
            pl.BlockSpec((None, 256, LANES), lambda b, t: (b, 0, 0)),
        ],
        out_specs=[
            pl.BlockSpec((tt, 512), lambda b, t: (b * nt + t, 0)),
            pl.BlockSpec((None, 256, LANES), lambda b, t: (b, 0, 0)),
        ],
        out_shape=[jax.ShapeDtypeStruct((out_rows, 512), BF16), jax.ShapeDtypeStruct((B, 256, LANES), F32)],
        scratch_shapes=[pltpu.VMEM((256, LANES), F32), pltpu.VMEM((tt, 256), F32),
                        pltpu.VMEM((tt // C, 256, LANES), BF16)],
        compiler_params=_cparams(("parallel", "arbitrary")),
    )(P, P, P, P, P, wa_pad, b_a, gla_norm, s0)


def _head_norm(x, gain, head0):
    x2 = x * x
    s0 = jnp.sum(jnp.where(head0, x2, 0.0), axis=-1, keepdims=True)
    s1 = jnp.sum(jnp.where(head0, 0.0, x2), axis=-1, keepdims=True)
    ms = jnp.where(head0, s0, s1) / float(HEAD_DIM)
    return x * lax.rsqrt(ms + EPS) * gain


NORM_ROWS = 512
ATTN_UNROLL = 4
QK_SCALE = 0.125


def _attn_prompt_kernel(q_ref, k_ref, v_ref, qg_ref, kg_ref, sl_ref, *rest, S, dil, n_keep, head_base, chained):
    if chained:
        rest = rest[1:]
    o_ref, lse_ref, kv_ref, tmp, qd, kd, vd, od, ld, bias = rest
    msub = S // dil
    nb = msub // NK
    kstride = msub + NK
    lane = lax.broadcasted_iota(jnp.int32, (1, LANES), 1)
    head0 = lane < HEAD_DIM

    def norm_into_tmp(src_ref, g_ref):
        def body(i, carry):
            rows = pl.ds(pl.multiple_of(i * NORM_ROWS, NORM_ROWS), NORM_ROWS)
            tmp[rows, :] = _head_norm(src_ref[rows, :], g_ref[...], head0)
            return carry
        lax.fori_loop(0, S // NORM_ROWS, body, 0)

    norm_into_tmp(q_ref, qg_ref)
    for r in range(dil):
        qd[r * msub:(r + 1) * msub, :] = (tmp[pl.ds(r, msub, stride=dil), :] * QK_SCALE).astype(BF16)
    norm_into_tmp(k_ref, kg_ref)
    kv_ref[0] = tmp[S - n_keep:S, :].T
    kv_ref[1] = v_ref[S - n_keep:S, :].T
    zeros = jnp.zeros((NK, LANES), BF16)
    for r in range(dil):
        kd[r * kstride:r * kstride + NK, :] = zeros
        vd[r * kstride:r * kstride + NK, :] = zeros
        kd[r * kstride + NK:(r + 1) * kstride, :] = tmp[pl.ds(r, msub, stride=dil), :].astype(BF16)
        vd[r * kstride + NK:(r + 1) * kstride, :] = v_ref[pl.ds(r, msub, stride=dil), :].astype(BF16)

    qi = lax.broadcasted_iota(jnp.int32, (NK, 2 * NK), 0)
    ki = lax.broadcasted_iota(jnp.int32, (NK, 2 * NK), 1)
    dist = qi - ki + NK
    band = (dist >= 0) & (dist <= NK)
    alibi = (dist * dil).astype(F32)
    cur = ki >= NK
    for hh in range(2):
        slope = sl_ref[head_base + 2 * pl.program_id(1) + hh]
        bias[2 * hh] = jnp.where(band, -(slope * alibi), NEG)
        bias[2 * hh + 1] = jnp.where(band & cur, -(slope * alibi), NEG)

    def block(j, carry):
        r = j // nb
        n = j - r * nb
        first = jnp.where(n == 0, 1, 0)
        qb = pl.multiple_of(j * NK, NK)
        kb = pl.multiple_of(j * NK + r * NK, NK)
        q2 = qd[pl.ds(qb, NK), :]
        k2 = kd[pl.ds(kb, 2 * NK), :]
        v2 = vd[pl.ds(kb, 2 * NK), :]
        outs, lses = [], []
        for hh in range(2):
            hmask = head0 if hh == 0 else jnp.logical_not(head0)
            qm = jnp.where(hmask, q2, jnp.zeros_like(q2))
            s = lax.dot_general(qm, k2, (((1,), (1,)), ((), ())), preferred_element_type=F32)
            s = s + bias[2 * hh + first]
            m = jnp.max(s, axis=-1, keepdims=True)
            p = jnp.exp(s - m)
            lsum = jnp.sum(p, axis=-1, keepdims=True)
            outs.append(jnp.dot(p.astype(BF16), v2, preferred_element_type=F32) / lsum)
            lses.append(m + jnp.log(lsum))
        od[pl.ds(qb, NK), :] = jnp.where(head0, outs[0], outs[1])
        ld[pl.ds(qb, NK), :] = jnp.where(head0, lses[0], lses[1])
        return carry

    lax.fori_loop(0, dil * nb, block, 0, unroll=ATTN_UNROLL)

    for r in range(dil):
        o_ref[pl.ds(r, msub, stride=dil), :] = od[r * msub:(r + 1) * msub, :]
        lse_ref[pl.ds(r, msub, stride=dil), :] = ld[r * msub:(r + 1) * msub, :]


def attn_prompt(P, B, S, q_gain, k_gain, slopes, l, g, win_prev):
    window, dil = ATTN_GROUPS[g]
    n_keep = min(window, S)
    chained = win_prev is not None

    def col(c):
        return pl.BlockSpec((S, LANES), lambda b, p: (b, c // LANES + 2 * g + p))

    def par(stack):
        return pl.BlockSpec((None, 1, LANES), lambda b, p: (stack * 6 + 2 * g + p, 0, 0))

    in_specs = [col(COL_AQ), col(COL_AK), col(COL_AV), par(l), par(l), pl.BlockSpec(memory_space=pltpu.SMEM)]
    args = [P, P, P, q_gain, k_gain, slopes]
    if chained:
        in_specs.append(pl.BlockSpec(memory_space=pl.ANY))
        args.append(win_prev)
    return pl.pallas_call(
        functools.partial(_attn_prompt_kernel, S=S, dil=dil, n_keep=n_keep, head_base=4 * g, chained=chained),
        grid=(B, 2),
        in_specs=in_specs,
        out_specs=[
            pl.BlockSpec((None, S, LANES), lambda b, p: (b, 0, p)),
            pl.BlockSpec((None, S, LANES), lambda b, p: (b, 0, p)),
            pl.BlockSpec((None, 2, LANES, n_keep), lambda b, p: (l * B + b, 0, p, 0)),
        ],
        out_shape=[
            jax.ShapeDtypeStruct((B, S, 256), F32),
            jax.ShapeDtypeStruct((B, S, 256), F32),
            jax.ShapeDtypeStruct((DEPTH * B, 2, 256, n_keep), F32),
        ],
        input_output_aliases={6: 2} if chained else {},
        scratch_shapes=[
            pltpu.VMEM((S, LANES), F32),
            pltpu.VMEM((S, LANES), BF16),
            pltpu.VMEM((S + dil * NK, LANES), BF16),
            pltpu.VMEM((S + dil * NK, LANES), BF16),
            pltpu.VMEM((S, LANES), F32),
            pltpu.VMEM((S, LANES), F32),
            pltpu.VMEM((4, NK, 2 * NK), F32),
        ],
        compiler_params=_cparams(("parallel", "arbitrary")),
    )(*args)


T_PAD = 8


def _attn_sample_kernel(q_ref, k_ref, v_ref, qg_ref, kg_ref, sl_ref, cache_ref, *rest,
                        L, dil, t_true, head_base, chained):
    if chained:
        rest = rest[1:]
    o_ref, lse_ref, cnew_ref = rest
    lane = lax.broadcasted_iota(jnp.int32, (1, LANES), 1)
    head0 = lane < HEAD_DIM
    trow8 = lax.broadcasted_iota(jnp.int32, (T_PAD, LANES), 0)
    lane8 = lax.broadcasted_iota(jnp.int32, (T_PAD, LANES), 1)
    place = ((lane8 == LANES - t_true + trow8) & (trow8 < t_true)).astype(F32)
    is_new = lane >= LANES - t_true

    def rolled_with_new(rows, new):
        shifted = pltpu.roll(cache_ref[rows, :], L - t_true, 1)
        tail = lax.dot_general(new, place, (((0,), (0,)), ((), ())), preferred_element_type=F32,
                               precision=lax.Precision.HIGHEST)
        if L > LANES:
            cnew_ref[rows, 0:L - LANES] = shifted[:, 0:L - LANES]
        cnew_ref[rows, L - LANES:L] = jnp.where(is_new, tail, shifted[:, L - LANES:L])

    trow = lax.broadcasted_iota(jnp.int32, (T_PAD, 1), 0)
    scol = lax.broadcasted_iota(jnp.int32, (1, L), 1)
    dist = L + trow - scol
    valid_c = ((dist & (dil - 1)) == 0) & (dist <= NK * dil)
    alibi_c = dist.astype(F32)

    for p in range(2):
        ls = slice(p * LANES, (p + 1) * LANES)
        qn = _head_norm(q_ref[:, ls], qg_ref[:, ls], head0)
        kn = _head_norm(k_ref[:, ls], kg_ref[:, ls], head0)
        vn = v_ref[:, ls]
        krows = slice(p * LANES, (p + 1) * LANES)
        vrows = slice(256 + p * LANES, 256 + (p + 1) * LANES)
        rolled_with_new(krows, kn)
        rolled_with_new(vrows, vn)
        kc_t = cache_ref[krows, :].astype(BF16)
        vc_t = cache_ref[vrows, :].astype(BF16)
        knr = kn.astype(BF16).astype(F32)
        vnr = vn.astype(BF16).astype(F32)
        outs, lses = [], []
        for hh in range(2):
            hmask = head0 if hh == 0 else jnp.logical_not(head0)
            qm = jnp.where(hmask, qn, 0.0).astype(BF16)
            qmr = qm.astype(F32)
            slope = sl_ref[head_base + 2 * p + hh]
            sc = jnp.dot(qm, kc_t, preferred_element_type=F32)
            sc = sc / float(np.sqrt(HEAD_DIM)) - slope * alibi_c
            sc = jnp.where(valid_c, sc, NEG)
            m = jnp.max(sc, axis=-1, keepdims=True)
            sn = []
            for u in range(t_true):
                du = trow - u
                su = jnp.sum(qmr * knr[u:u + 1, :], axis=-1, keepdims=True) / float(np.sqrt(HEAD_DIM))
                su = su - slope * du.astype(F32)
                su = jnp.where((du >= 0) & ((du & (dil - 1)) == 0), su, NEG)
                sn.append(su)
                m = jnp.maximum(m, su)
            pc = jnp.exp(sc - m)
            lsum = jnp.sum(pc, axis=-1, keepdims=True)
            acc = _dot_nt(pc.astype(BF16), vc_t)
            for u in range(t_true):
                pu = jnp.exp(sn[u] - m)
                lsum = lsum + pu
                acc = acc + pu.astype(BF16).astype(F32) * vnr[u:u + 1, :]
            outs.append(acc / lsum)
            lses.append(m + jnp.log(lsum))
        o_ref[:, ls] = jnp.where(head0, outs[0], outs[1])
        lse_ref[:, ls] = jnp.where(head0, lses[0], lses[1])


def attn_sample(P, B, q_gain, k_gain, slopes, cache_t, l, g, t_true, win_prev):
    window, dil = ATTN_GROUPS[g]
    L = cache_t.shape[2]
    chained = win_prev is not None

    def col(c):
        return pl.BlockSpec((T_PAD, 256), lambda b: (b, c // 256 + g))

    def par(stack):
        return pl.BlockSpec((None, 1, 256), lambda b: (stack * 3 + g, 0, 0))

    in_specs = [col(COL_AQ), col(COL_AK), col(COL_AV), par(l), par(l), pl.BlockSpec(memory_space=pltpu.SMEM),
                pl.BlockSpec((None, 512, L), lambda b: (l * B + b, 0, 0))]
    args = [P, P, P, q_gain, k_gain, slopes, cache_t]
    if chained:
        in_specs.append(pl.BlockSpec(memory_space=pl.ANY))
        args.append(win_prev)
    return pl.pallas_call(
        functools.partial(_attn_sample_kernel, L=L, dil=dil, t_true=t_true, head_base=4 * g, chained=chained),
        grid=(B,),
        in_specs=in_specs,
        out_specs=[
            pl.BlockSpec((None, T_PAD, 256), lambda b: (b, 0, 0)),
            pl.BlockSpec((None, T_PAD, 256), lambda b: (b, 0, 0)),
            pl.BlockSpec((None, 512, L), lambda b: (l * B + b, 0, 0)),
        ],
        out_shape=[
            jax.ShapeDtypeStruct((B, T_PAD, 256), F32),
            jax.ShapeDtypeStruct((B, T_PAD, 256), F32),
            jax.ShapeDtypeStruct((DEPTH * B, 512, L), F32),
        ],
        input_output_aliases={7: 2} if chained else {},
        compiler_params=_cparams(("parallel",)),
    )(*args)


def _attn_merge_kernel(o0, l0, o1, l1, o2, l2, y_ref):
    a0, a1, a2 = l0[...], l1[...], l2[...]
    m = jnp.maximum(jnp.maximum(a0, a1), a2)
    e0, e1, e2 = jnp.exp(a0 - m), jnp.exp(a1 - m), jnp.exp(a2 - m)
    den = e0 + e1 + e2
    y = (e0 / den) * o0[...] + (e1 / den) * o1[...] + (e2 / den) * o2[...]
    y_ref[...] = y.astype(BF16)


def attn_merge(parts, tm, out_rows):
    M = parts[0].shape[0]
    spec = pl.BlockSpec((tm, 256), lambda m: (m, 0))
    return pl.pallas_call(
        _attn_merge_kernel,
        grid=(M // tm,),
        in_specs=[spec] * 6,
        out_specs=spec,
        out_shape=jax.ShapeDtypeStruct((out_rows, 256), BF16),
        compiler_params=_cparams(("parallel",)),
    )(*parts)


GATE_SRC0 = 5904
assert GATE_SRC0 % SRC_ALIGN == 0 and D_MODEL % SRC_ALIGN == 0


def _merge_kernel(xn_ref, ya_ref, yb_ref, yc_ref, yd_ref, g0, g1, g2, g3, u0, u1, u2, u3, o_ref,
                  gs0, gs1, gs2, gs3, us0, us1, us2, us3, *, tiles, tm):
    branches = ((ya_ref, g0, u0, gs0, us0), (yb_ref, g1, u1, gs1, us1),
                (yc_ref, g2, u2, gs2, us2), (yd_ref, g3, u3, gs3, us3))
    m = pl.program_id(1)

    @pl.when(m == 0)
    def _():
        for _, g_ref, u_ref, g_s, u_s in branches:
            g_s[...] = g_ref[...].astype(BF16)
            u_s[...] = u_ref[...].astype(BF16)

    def rows_fn(rows):
        xn = xn_ref[rows, :]
        acc = None
        for y_ref, _, _, g_s, u_s in branches:
            gate = jax.nn.sigmoid(_dot_nt(xn, g_s[...]))
            term = gate * jnp.dot(y_ref[rows, :], u_s[...], preferred_element_type=F32)
            acc = term if acc is None else acc + term
        o_ref[rows, :] = acc.astype(BF16)

    _on_tile_rows(m, tiles, tm, rows_fn)


def merge(xn, ya, yb, yc, yd, w_in_t, ups, l, tm):
    M = xn.shape[0]
    tiles = _row_tiles(M, tm)
    tn = PREP_ROWS
    widths = (512, 256, 512, 512)

    def act(width):
        return pl.BlockSpec((tm, width), lambda n, m: (m, 0))

    def gate(b):
        base = (GATE_SRC0 + b * D_MODEL) // SRC_ALIGN
        return _w_in_rows(l, lambda n, m: base + n * (tn // SRC_ALIGN))

    def up(width):
        return pl.BlockSpec((None, width, tn), lambda n, m: (l, 0, n))

    return pl.pallas_call(
        functools.partial(_merge_kernel, tiles=tiles, tm=tm),
        grid=(D_MODEL // tn, tiles[0]),
        in_specs=[act(D_MODEL)] + [act(w) for w in widths] + [gate(b) for b in range(N_BRANCH)]
                 + [up(w) for w in widths],
        out_specs=pl.BlockSpec((tm, tn), lambda n, m: (m, n)),
        out_shape=jax.ShapeDtypeStruct((M, D_MODEL), BF16),
        scratch_shapes=[pltpu.VMEM((tn, D_MODEL), BF16)] * N_BRANCH + [pltpu.VMEM((w, tn), BF16) for w in widths],
        compiler_params=_cparams(("parallel", "arbitrary")),
    )(xn, ya, yb, yc, yd, w_in_t, w_in_t, w_in_t, w_in_t, *ups)


def _layer(x, xn, W, l, T, mixers):
    h = ffn_up(xn, W['ff1_gate'], W['ff1_up'], l, T['up_m'], T['up_n'])
    x, xn = matmul_res(x, h, W['ff1_down'], l, 0.5, T['down_m'], W['n_mix'], l)
    P = mix_in(xn, W['w_in_t'], l, T['mix_m'])
    (ya, yb, yc, yd), states = mixers(P, l)
    mg = merge(xn, ya, yb, yc, yd, W['w_in_t'], (W['up_a'], W['up_b'], W['up_c'], W['up_d']), l, T['merge_m'])
    x, xn = matmul_res(x, mg, W['w_out'], l, 1.0, T['out_m'], W['n_ff2'], l)
    h = ffn_up(xn, W['ff2_gate'], W['ff2_up'], l, T['up_m'], T['up_n'])
    if l + 1 < DEPTH:
        x, xn = matmul_res(x, h, W['ff2_down'], l, 0.5, T['down_m'], W['n_ff1'], l + 1)
    else:
        x, xn = matmul_res(x, h, W['ff2_down'], l, 0.5, T['down_m'])
    return x, xn, states


SAMPLE_SLOT = 128
TILES = dict(up_m=1040, up_n=512, mix_m=2080, down_m=256, out_m=512, merge_m=1040)


def kernel(x_prompt, x_sample, state_conv, cache_w128_kv, cache_w512_kv, cache_w2048_kv, state_gla, state_pool,
           norm_ff1, ff1_gate, ff1_up, ff1_down, norm_mix, w_in, conv_w, attn_q_gain, attn_k_gain,
           gla_w_a2, gla_b_a, gla_norm, pool_w, pool_scale, w_up_conv, w_up_attn, w_up_gla, w_up_pool, w_out,
           norm_ff2, ff2_gate, ff2_up, ff2_down):
    BP, S, _ = x_prompt.shape
    BS, TS, _ = x_sample.shape
    caches = (cache_w128_kv, cache_w512_kv, cache_w2048_kv)

    W = {
        'n_ff1': norm_ff1.reshape(DEPTH, 1, D_MODEL), 'n_mix': norm_mix.reshape(DEPTH, 1, D_MODEL),
        'n_ff2': norm_ff2.reshape(DEPTH, 1, D_MODEL),
        'ff1_gate': ff1_gate, 'ff1_up': ff1_up, 'ff1_down': ff1_down.astype(BF16),
        'ff2_gate': ff2_gate, 'ff2_up': ff2_up, 'ff2_down': ff2_down.astype(BF16),
        'w_in_t': jnp.transpose(w_in, (0, 2, 1)),
        'up_a': w_up_conv, 'up_b': w_up_attn, 'up_c': w_up_gla, 'up_d': w_up_pool, 'w_out': w_out.astype(BF16),
    }
    wa_pad = jnp.pad(gla_w_a2, ((0, 0), (0, LANES - gla_w_a2.shape[1]), (0, 0))).astype(BF16)
    b_a = gla_b_a.reshape(DEPTH, 1, 256)
    gn = gla_norm.reshape(DEPTH, 1, LANES)
    pool_wb = pool_w.astype(BF16)
    pool_sc = pool_scale.reshape(DEPTH, 1, W_POOL)
    i = jnp.arange(1, N_ATTN_HEADS + 1, dtype=F32)
    slopes = jnp.exp2(-8.0 * i / N_ATTN_HEADS)
    qg6 = attn_q_gain.reshape(DEPTH * 6, 1, LANES)
    kg6 = attn_k_gain.reshape(DEPTH * 6, 1, LANES)
    qg3 = attn_q_gain.reshape(DEPTH * 3, 1, 256)
    kg3 = attn_k_gain.reshape(DEPTH * 3, 1, 256)

    zc = jnp.zeros((BP, CONV_HIST, W_CONV), F32)
    zp = jnp.zeros((BP, POOL_HIST16, W_POOL), F32)
    zs = jnp.zeros((BP, 256, LANES), F32)

    win_p = [None, None, None]
    win_s = [None, None, None]

    MP, MS = BP * S, BS * TS
    M_ALL = MP + SAMPLE_SLOT
    assert MS <= SAMPLE_SLOT

    def prompt_mixers(P, l):
        ya, yd, cnew, pnew = conv_pool(P, BP, S, zc, zp, conv_w, pool_wb, pool_sc, l, 512, 512, 0, M_ALL)
        yc, sfin = gla(P, BP, S, wa_pad, b_a, gn, zs, l, 512, GLA_CHUNK, GLA_CHUNK, M_ALL)
        parts = []
        for g in range(3):
            o, lse, win_p[g] = attn_prompt(P, BP, S, qg6, kg6, slopes, l, g, win_p[g])
            parts += [o.reshape(MP, 256), lse.reshape(MP, 256)]
        yb = attn_merge(parts, 1024, M_ALL)
        return (ya, yb, yc, yd), (cnew, sfin.reshape(BP, 4, GLA_DK, LANES), pnew[:, 1:])

    sconv8 = jnp.pad(state_conv, ((0, 0), (0, 0), (CONV_HIST - 2, 0), (0, 0)))
    spool16 = jnp.pad(state_pool, ((0, 0), (0, 0), (1, 0), (0, 0)))
    sgla = state_gla.reshape(DEPTH, BS, 256, LANES)
    cviews = [jnp.transpose(cc, (0, 1, 3, 4, 5, 2)).reshape(DEPTH * BS, 512, cc.shape[2]) for cc in caches]

    def sample_mixers(P, l):
        MSP = BS * T_PAD
        Ps = jnp.pad(P[MP:MP + MS].reshape(BS, TS, N_MIX), ((0, 0), (0, T_PAD - TS), (0, 0))).reshape(MSP, N_MIX)
        ya, yd, cnew, pnew = conv_pool(Ps, BS, T_PAD, sconv8[l], spool16[l], conv_w, pool_wb, pool_sc, l,
                                       T_PAD, TS, PAST_LEN, MSP)
        yc, sfin = gla(Ps, BS, T_PAD, wa_pad, b_a, gn, sgla[l], l, T_PAD, T_PAD, TS, MSP)
        parts = []
        for g in range(3):
            o, lse, win_s[g] = attn_sample(Ps, BS, qg3, kg3, slopes, cviews[g], l, g, TS, win_s[g])
            parts += [o[:, :TS].reshape(MS, 256), lse[:, :TS].reshape(MS, 256)]
        yb = attn_merge(parts, MS, MS)

        def valid(y):
            return y.reshape(BS, T_PAD, -1)[:, :TS].reshape(MS, -1)

        return (valid(ya), yb, valid(yc), valid(yd)), (cnew, sfin.reshape(BS, 4, GLA_DK, LANES), pnew[:, 1:])

    def mixers(P, l):
        ys_p, st_prompt = prompt_mixers(P, l)
        ys_s, st_sample = sample_mixers(P, l)
        ys = tuple(lax.dynamic_update_slice(yp_, jnp.pad(ys_, ((0, SAMPLE_SLOT - MS), (0, 0))), (MP, 0))
                   for yp_, ys_ in zip(ys_p, ys_s))
        return ys, (st_prompt, st_sample)

    x = jnp.concatenate([x_prompt.reshape(MP, D_MODEL), x_sample.reshape(MS, D_MODEL),
                         jnp.zeros((SAMPLE_SLOT - MS, D_MODEL), F32)], axis=0)
    xn = norm(x, W['n_ff1'], 0, 1024)
    st_p, st_s = [], []
    for l in range(DEPTH):
        x, xn, (sp, ss) = _layer(x, xn, W, l, TILES, mixers)
        st_p.append(sp)
        st_s.append(ss)
    yp, ys = x[:MP], x[MP:MP + MS]

    def stack(sts, f):
        return jnp.stack([f(s) for s in sts])

    def window_out(buf, B):
        n = buf.shape[-1]
        return jnp.transpose(buf.reshape(DEPTH, B, 2, 4, HEAD_DIM, n), (0, 1, 5, 2, 3, 4))

    outs = [yp.reshape(BP, S, D_MODEL), ys.reshape(BS, TS, D_MODEL),
            stack(st_p, lambda s: s[0]), stack(st_s, lambda s: s[0])]
    for g in range(3):
        outs += [window_out(win_p[g], BP), window_out(win_s[g], BS)]
    outs += [stack(st_p, lambda s: s[1]), stack(st_s, lambda s: s[1]),
             stack(st_p, lambda s: s[2]), stack(st_s, lambda s: s[2])]
    return tuple(outs)
```

```python
import functools

import numpy as np
import jax
import jax.numpy as jnp
from jax import lax
from jax.experimental import pallas as pl
from jax.experimental.pallas import tpu as pltpu

BF16 = jnp.bfloat16
F32 = jnp.float32

D_MODEL = 2048
DEPTH = 4
PAST_LEN = 16384
D_FF = 5504
EPS = 1e-6
N_BRANCH = 4
W_CONV = 512
ATTN_GROUPS = ((128, 1), (512, 4), (2048, 16))
N_ATTN_HEADS = 12
HEAD_DIM = 64
GLA_DK = 64
GLA_TAU = 16.0
GLA_CHUNK = 64
POOL_WINDOWS = (2, 4, 8, 16)
W_POOL = 512
NK = 128
LANES = 128
NEG = -1e30

COL_CB, COL_CC, COL_CH, COL_GV, COL_GR, COL_POOL = 0, 512, 1024, 1536, 2048, 2560
COL_AQ, COL_AK, COL_AV = 3072, 3840, 4608
COL_GQ, COL_GK, COL_LR = 5376, 5632, 5888
N_MIX = 6144
N_IN = 14096
N_GATE = N_BRANCH * D_MODEL
PREP_ROWS = 256
MIX_TILE_SRC = ([256 * j for j in range(6)] + [4352, 4608, 4864, 5120, 5392, 5648]
                + [1536 + 256 * j for j in range(9)] + [3840, 4096, 5376])
SRC_ALIGN = 16
assert all(r % SRC_ALIGN == 0 for r in MIX_TILE_SRC)

VMEM_LIMIT = 56 * 1024 * 1024


def _cparams(sem):
    return pltpu.CompilerParams(dimension_semantics=sem, vmem_limit_bytes=VMEM_LIMIT)


def _rms(x):
    return x * lax.rsqrt(jnp.mean(x * x, axis=-1, keepdims=True) + EPS)


def _row_tiles(M, tm):
    n = pl.cdiv(M, tm)
    return n, M - (n - 1) * tm


def _on_tile_rows(m, tiles, tm, fn):
    n, tail = tiles
    if tail == tm:
        fn(slice(0, tm))
        return
    pl.when(m < n - 1)(lambda: fn(slice(0, tm)))
    pl.when(m == n - 1)(lambda: fn(slice(0, tail)))


def _split_rows_specs(tm, n_tiles, slot):
    return (pl.BlockSpec((tm, D_MODEL), lambda m: (jnp.minimum(m, n_tiles - 2), 0)),
            pl.BlockSpec((slot, D_MODEL), lambda m: (0, 0)))


def _norm_kernel(xp_ref, xs_ref, g_ref, xn_ref, *, tiles, tm):
    n, tail = tiles
    m = pl.program_id(0)

    @pl.when(m < n - 1)
    def _():
        xn_ref[...] = (_rms(xp_ref[...]) * g_ref[...]).astype(BF16)

    @pl.when(m == n - 1)
    def _():
        xn_ref[0:tail, :] = (_rms(xs_ref[...]) * g_ref[...]).astype(BF16)


def norm(xp, xs, gain, l, tm):
    MP, slot = xp.shape[0], xs.shape[0]
    assert MP % tm == 0 and slot < tm
    tiles = (MP // tm + 1, slot)
    return pl.pallas_call(
        functools.partial(_norm_kernel, tiles=tiles, tm=tm),
        grid=(tiles[0],),
        in_specs=[*_split_rows_specs(tm, tiles[0], slot),
                  pl.BlockSpec((None, 1, D_MODEL), lambda m: (l, 0, 0))],
        out_specs=pl.BlockSpec((tm, D_MODEL), lambda m: (m, 0)),
        out_shape=jax.ShapeDtypeStruct((MP + slot, D_MODEL), BF16),
        compiler_params=_cparams(("parallel",)),
    )(xp, xs, gain)


def _ffn_up_kernel(xn_ref, wg_ref, wu_ref, h_ref, wg_s, wu_s, *, tiles, tm):
    m = pl.program_id(1)

    @pl.when(m == 0)
    def _():
        wg_s[...] = wg_ref[...].astype(BF16)
        wu_s[...] = wu_ref[...].astype(BF16)

    def rows_fn(rows):
        xn = xn_ref[rows, :]
        a = jnp.dot(xn, wg_s[...], preferred_element_type=F32)
        b = jnp.dot(xn, wu_s[...], preferred_element_type=F32)
        h_ref[rows, :] = (a * jax.nn.sigmoid(a) * b).astype(BF16)

    _on_tile_rows(m, tiles, tm, rows_fn)


def ffn_up(xn, wg, wu, l, tm, tn):
    M = xn.shape[0]
    tiles = _row_tiles(M, tm)
    return pl.pallas_call(
        functools.partial(_ffn_up_kernel, tiles=tiles, tm=tm),
        grid=(pl.cdiv(D_FF, tn), tiles[0]),
        in_specs=[
            pl.BlockSpec((tm, D_MODEL), lambda n, m: (m, 0)),
            pl.BlockSpec((None, D_MODEL, tn), lambda n, m: (l, 0, n)),
            pl.BlockSpec((None, D_MODEL, tn), lambda n, m: (l, 0, n)),
        ],
        out_specs=pl.BlockSpec((tm, tn), lambda n, m: (m, n)),
        out_shape=jax.ShapeDtypeStruct((M, D_FF), BF16),
        scratch_shapes=[pltpu.VMEM((D_MODEL, tn), BF16), pltpu.VMEM((D_MODEL, tn), BF16)],
        compiler_params=_cparams(("parallel", "arbitrary")),
    )(xn, wg, wu)


def _matmul_res_kernel(*refs, scale, with_norm, tiles, tm, x_split, out_split):
    refs = list(refs)
    x_refs = [refs.pop(0) for _ in range(2 if x_split else 1)]
    a_ref, w_ref = refs.pop(0), refs.pop(0)
    g_ref = refs.pop(0) if with_norm else None
    o_refs = [refs.pop(0) for _ in range(2 if out_split else 1)]
    xn_ref = refs.pop(0) if with_norm else None
    n, tail = tiles
    m = pl.program_id(0)

    def rows_fn(rows, last):
        x_ref = x_refs[-1] if last else x_refs[0]
        o_ref = o_refs[-1] if last else o_refs[0]
        y = x_ref[rows, :] + scale * jnp.dot(a_ref[rows, :], w_ref[...], preferred_element_type=F32)
        if with_norm:
            xn_ref[rows, :] = (_rms(y) * g_ref[...]).astype(BF16)
        o_ref[rows, :] = y

    pl.when(m < n - 1)(lambda: rows_fn(slice(0, tm), False))
    pl.when(m == n - 1)(lambda: rows_fn(slice(0, tail), True))


def matmul_res(x, a, w, l, scale, tm, next_gain=None, next_l=0, out_split=False):
    M, K = a.shape
    with_norm = next_gain is not None
    x_split = isinstance(x, tuple)
    tiles = _row_tiles(M, tm)
    assert tiles[1] == SAMPLE_SLOT and (M - SAMPLE_SLOT) % tm == 0
    row = pl.BlockSpec((tm, D_MODEL), lambda m: (m, 0))
    split = _split_rows_specs(tm, tiles[0], SAMPLE_SLOT)
    in_specs = (list(split) if x_split else [row]) + [
        pl.BlockSpec((tm, K), lambda m: (m, 0)),
        pl.BlockSpec((None, K, D_MODEL), lambda m: (l, 0, 0), pipeline_mode=pl.Buffered(1))]
    args = (list(x) if x_split else [x]) + [a, w]
    if out_split:
        out_specs = list(split)
        out_shape = [jax.ShapeDtypeStruct((M - SAMPLE_SLOT, D_MODEL), F32),
                     jax.ShapeDtypeStruct((SAMPLE_SLOT, D_MODEL), F32)]
    else:
        out_specs, out_shape = [row], [jax.ShapeDtypeStruct((M, D_MODEL), F32)]
    if with_norm:
        in_specs.append(pl.BlockSpec((None, 1, D_MODEL), lambda m: (next_l, 0, 0)))
        args.append(next_gain)
        out_specs.append(row)
        out_shape.append(jax.ShapeDtypeStruct((M, D_MODEL), BF16))
    res = pl.pallas_call(
        functools.partial(_matmul_res_kernel, scale=scale, with_norm=with_norm, tiles=tiles, tm=tm,
                          x_split=x_split, out_split=out_split),
        grid=(tiles[0],),
        in_specs=in_specs,
        out_specs=out_specs,
        out_shape=out_shape,
        compiler_params=_cparams(("parallel",)),
    )(*args)
    y = tuple(res[:2]) if out_split else res[0]
    return y, (res[-1] if with_norm else None)


def _dot_nt(a, b):
    return lax.dot_general(a, b, (((1,), (1,)), ((), ())), preferred_element_type=F32)


def _w_in_rows(l, row_of):
    return pl.BlockSpec((None, pl.Element(PREP_ROWS), pl.Element(D_MODEL)),
                        lambda *idx: (l, row_of(*idx) * SRC_ALIGN, 0))


MIX_SLABS = 4


def _mix_in_kernel(src_ref, xn_ref, *rest, tiles, tm):
    del src_ref
    w_refs, (p_ref, w_s) = rest[:MIX_SLABS], rest[MIX_SLABS:]
    m = pl.program_id(1)

    @pl.when(m == 0)
    def _():
        for j, w_ref in enumerate(w_refs):
            w_s[j * PREP_ROWS:(j + 1) * PREP_ROWS, :] = w_ref[...].astype(BF16)

    def rows_fn(rows):
        p_ref[rows, :] = _dot_nt(xn_ref[rows, :], w_s[...])

    _on_tile_rows(m, tiles, tm, rows_fn)


def mix_in(xn, w_in_t, l, tm):
    M = xn.shape[0]
    tn = MIX_SLABS * PREP_ROWS
    tiles = _row_tiles(M, tm)
    src = jnp.asarray(MIX_TILE_SRC, jnp.int32) // SRC_ALIGN

    def slab(j):
        return _w_in_rows(l, lambda n, m, src: src[MIX_SLABS * n + j])

    return pl.pallas_call(
        functools.partial(_mix_in_kernel, tiles=tiles, tm=tm),
        grid_spec=pltpu.PrefetchScalarGridSpec(
            num_scalar_prefetch=1,
            grid=(N_MIX // tn, tiles[0]),
            in_specs=[pl.BlockSpec((tm, D_MODEL), lambda n, m, src: (m, 0))] + [slab(j) for j in range(MIX_SLABS)],
            out_specs=pl.BlockSpec((tm, tn), lambda n, m, src: (m, n)),
            scratch_shapes=[pltpu.VMEM((tn, D_MODEL), BF16)],
        ),
        out_shape=jax.ShapeDtypeStruct((M, N_MIX), F32),
        compiler_params=_cparams(("parallel", "arbitrary")),
    )(src, xn, *([w_in_t] * MIX_SLABS))


CONV_HIST = 8
POOL_HIST16 = 16


def _conv_pool_kernel(cb_ref, cc_ref, ch_ref, pin_ref, cbuf_ref, pbuf_ref, cw_ref, pw_ref, ps_ref,
                      ya_ref, yd_ref, cnew_ref, pnew_ref, uext, pext, *, tt, t_true, pos0):
    t = pl.program_id(1)

    @pl.when(t == 0)
    def _():
        uext[0:CONV_HIST, :] = cbuf_ref[...]
        pext[0:POOL_HIST16, :] = pbuf_ref[...]

    u = cc_ref[...] * ch_ref[...]
    uext[CONV_HIST:CONV_HIST + tt, :] = u
    w = cw_ref[...]
    z = uext[CONV_HIST - 2:CONV_HIST - 2 + tt, :] * w[0:1, :]
    z = z + uext[CONV_HIST - 1:CONV_HIST - 1 + tt, :] * w[1:2, :]
    z = z + u * w[2:3, :]
    ya_ref[...] = (cb_ref[...] * z).astype(BF16)
    cnew_ref[...] = uext[CONV_HIST + t_true - 2:CONV_HIST + t_true, :]
    uext[0:CONV_HIST, :] = uext[tt:tt + CONV_HIST, :]

    pin = pin_ref[...]
    pext[POOL_HIST16:POOL_HIST16 + tt, :] = pin
    pos = (pos0 + t * tt + lax.broadcasted_iota(jnp.int32, (tt, 1), 0)).astype(F32)
    for g, win in enumerate(POOL_WINDOWS):
        ls = slice(g * LANES, (g + 1) * LANES)
        acc = pin[:, ls]
        for i in range(1, win):
            acc = acc + pext[POOL_HIST16 - i:POOL_HIST16 - i + tt, ls]
        cnt = jnp.minimum(float(win), pos + 1.0)
        d = acc / cnt - pin[:, ls]
        y = jnp.dot(d.astype(BF16), pw_ref[g], preferred_element_type=F32) * ps_ref[:, ls]
        yd_ref[:, ls] = y.astype(BF16)
    pnew_ref[...] = pext[t_true:t_true + POOL_HIST16, :]
    pext[0:POOL_HIST16, :] = pext[tt:tt + POOL_HIST16, :]


def conv_pool(P, B, T, cbuf8, pbuf16, conv_w, pool_w, pool_scale, l, tt, t_true, pos0, out_rows):
    nt = T // tt

    def col(c):
        return pl.BlockSpec((tt, 512), lambda b, t: (b * nt + t, c // 512))

    return pl.pallas_call(
        functools.partial(_conv_pool_kernel, tt=tt, t_true=t_true, pos0=pos0),
        grid=(B, nt),
        in_specs=[
            col(COL_CB), col(COL_CC), col(COL_CH), col(COL_POOL),
            pl.BlockSpec((None, CONV_HIST, 512), lambda b, t: (b, 0, 0)),
            pl.BlockSpec((None, POOL_HIST16, 512), lambda b, t: (b, 0, 0)),
            pl.BlockSpec((None, 3, 512), lambda b, t: (l, 0, 0)),
            pl.BlockSpec((None, 4, LANES, LANES), lambda b, t: (l, 0, 0, 0)),
            pl.BlockSpec((None, 1, 512), lambda b, t: (l, 0, 0)),
        ],
        out_specs=[
            pl.BlockSpec((tt, 512), lambda b, t: (b * nt + t, 0)),
            pl.BlockSpec((tt, 512), lambda b, t: (b * nt + t, 0)),
            pl.BlockSpec((None, 2, 512), lambda b, t: (b, 0, 0)),
            pl.BlockSpec((None, POOL_HIST16, 512), lambda b, t: (b, 0, 0)),
        ],
        out_shape=[
            jax.ShapeDtypeStruct((out_rows, 512), BF16),
            jax.ShapeDtypeStruct((out_rows, 512), BF16),
            jax.ShapeDtypeStruct((B, 2, 512), F32),
            jax.ShapeDtypeStruct((B, POOL_HIST16, 512), F32),
        ],
        scratch_shapes=[pltpu.VMEM((CONV_HIST + tt, 512), F32), pltpu.VMEM((POOL_HIST16 + tt, 512), F32)],
        compiler_params=_cparams(("parallel", "arbitrary")),
    )(P, P, P, P, cbuf8, pbuf16, conv_w, pool_w, pool_scale)


GLA_UNROLL = 4

def _gla_kernel(q_ref, k_ref, v_ref, r_ref, lr_ref, wa_ref, ba_ref, gn_ref, s0_ref,
                yc_ref, sfin_ref, s_scr, b_scr, sprev_scr, *, tt, C, c_true):
    assert c_true == C or tt == C
    t = pl.program_id(1)

    @pl.when(t == 0)
    def _():
        s_scr[...] = s0_ref[...]

    row = lax.broadcasted_iota(jnp.int32, (C, 1), 0)
    lane = lax.broadcasted_iota(jnp.int32, (1, LANES), 1)
    head0 = lane < GLA_DK
    srow = lax.broadcasted_iota(jnp.int32, (LANES, 1), 0)
    eye = srow == lane
    gn = gn_ref[...]
    nchunk = tt // C

    def chunk_rows(c):
        return pl.ds(pl.multiple_of(c * C, C), C)

    def v_head(rows, h):
        v_h = v_ref[rows, h * LANES:(h + 1) * LANES]
        if c_true < C:
            v_h = jnp.where(row < c_true, v_h, 0.0)
        return v_h

    la = jnp.dot(lr_ref[...].astype(BF16), wa_ref[...], preferred_element_type=F32) + ba_ref[...]
    la = jax.nn.log_sigmoid(la) / GLA_TAU
    if c_true < C:
        la = jnp.where(row < c_true, la, 0.0)
    cb = 2 * C if nchunk % 2 == 0 else C
    bi = lax.broadcasted_iota(jnp.int32, (cb, 1), 0)
    bj = lax.broadcasted_iota(jnp.int32, (1, cb), 1)
    ones_tri = (((bi >= C) == (bj >= C)) & (bj <= bi)).astype(BF16)
    for blk in range(tt // cb):
        part = la[blk * cb:(blk + 1) * cb, :]
        acc = None
        for _ in range(3):
            term = part.astype(BF16)
            d = jnp.dot(ones_tri, term, preferred_element_type=F32)
            acc = d if acc is None else acc + d
            part = part - term.astype(F32)
        b_scr[blk * cb:(blk + 1) * cb, :] = acc

    def by_head(x):
        return jnp.concatenate([jnp.where(head0, x, 0.0), jnp.where(head0, 0.0, x)], axis=0)

    def v_pair(rows, p):
        return jnp.concatenate([v_head(rows, 2 * p), v_head(rows, 2 * p + 1)], axis=0).astype(BF16)

    def scan(c, carry):
        rows = chunk_rows(c)
        b = b_scr[rows, :]
        b_last = b[C - 1:C, :]
        kend = k_ref[rows, :] * jnp.exp(b_last - b)
        for p in range(2):
            ls = slice(p * LANES, (p + 1) * LANES)
            s_p = s_scr[ls, :]
            sprev_scr[c, ls, :] = s_p.astype(BF16)
            dec = jnp.exp(jnp.sum(jnp.where(eye, b_last[:, ls], 0.0), axis=1, keepdims=True))
            ds = lax.dot_general(by_head(kend[:, ls]).astype(BF16), v_pair(rows, p), (((0,), (0,)), ((), ())),
                                 preferred_element_type=F32)
            s_scr[ls, :] = dec * s_p + ds
        return carry

    lax.fori_loop(0, nchunk, scan, 0, unroll=min(2, nchunk))

    ri = lax.broadcasted_iota(jnp.int32, (2 * C, 1), 0)
    ci = lax.broadcasted_iota(jnp.int32, (1, 2 * C), 1)
    same_head_causal = ((ri >= C) == (ci >= C)) & ((ri % C) >= (ci % C))

    def emit(c, carry):
        rows = chunk_rows(c)
        b = b_scr[rows, :]
        qt = q_ref[rows, :] * (GLA_DK ** -0.5) * jnp.exp(b)
        kt = k_ref[rows, :] * jnp.exp(-b)
        for p in range(2):
            ls = slice(p * LANES, (p + 1) * LANES)
            qm = by_head(qt[:, ls]).astype(BF16)
            kt2 = jnp.concatenate([kt[:, ls], kt[:, ls]], axis=0).astype(BF16)
            a = jnp.where(same_head_causal, _dot_nt(qm, kt2), 0.0)
            o = jnp.dot(a.astype(BF16), v_pair(rows, p), preferred_element_type=F32)
            o = o + jnp.dot(qm, sprev_scr[c, ls, :], preferred_element_type=F32)
            on = _rms(o) * gn
            for hh in range(2):
                hs = slice((2 * p + hh) * LANES, (2 * p + hh + 1) * LANES)
                r_h = r_ref[rows, hs]
                yc_ref[rows, hs] = (on[hh * C:(hh + 1) * C, :] * (r_h * jax.nn.sigmoid(r_h))).astype(BF16)
        return carry

    lax.fori_loop(0, nchunk, emit, 0, unroll=min(GLA_UNROLL, nchunk))
    sfin_ref[...] = s_scr[...]


def gla(P, B, T, wa_pad, b_a, gla_norm, s0, l, tt, C, c_true, out_rows):
    nt = T // tt
    return pl.pallas_call(
        functools.partial(_gla_kernel, tt=tt, C=C, c_true=c_true),
        grid=(B, nt),
        in_specs=[
            pl.BlockSpec((tt, 256), lambda b, t: (b * nt + t, COL_GQ // 256)),
            pl.BlockSpec((tt, 256), lambda b, t: (b * nt + t, COL_GK // 256)),
            pl.BlockSpec((tt, 512), lambda b, t: (b * nt + t, COL_GV // 512)),
            pl.BlockSpec((tt, 512), lambda b, t: (b * nt + t, COL_GR // 512)),
            pl.BlockSpec((tt, LANES), lambda b, t: (b * nt + t, COL_LR // LANES)),
            pl.BlockSpec((None, LANES, 256), lambda b, t: (l, 0, 0)),
            pl.BlockSpec((None, 1, 256), lambda b, t: (l, 0, 0)),
            pl.BlockSpec((None, 1, LANES), lambda b, t: (l, 0, 0)),
            pl.BlockSpec((None, 256, LANES), lambda b, t: (b, 0, 0)),
        ],
        out_specs=[
            pl.BlockSpec((tt, 512), lambda b, t: (b * nt + t, 0)),
            pl.BlockSpec((None, 256, LANES), lambda b, t: (b, 0, 0)),
        ],
        out_shape=[jax.ShapeDtypeStruct((out_rows, 512), BF16), jax.ShapeDtypeStruct((B, 256, LANES), F32)],
        scratch_shapes=[pltpu.VMEM((256, LANES), F32), pltpu.VMEM((tt, 256), F32),
                        pltpu.VMEM((tt // C, 256, LANES), BF16)],
        compiler_params=_cparams(("parallel", "arbitrary")),
    )(P, P, P, P, P, wa_pad, b_a, gla_norm, s0)


def _head_norm(x, gain, head0):
    x2 = x * x
    s0 = jnp.sum(jnp.where(head0, x2, 0.0), axis=-1, keepdims=True)
    s1 = jnp.sum(jnp.where(head0, 0.0, x2), axis=-1, keepdims=True)
    ms = jnp.where(head0, s0, s1) / float(HEAD_DIM)
    return x * lax.rsqrt(ms + EPS) * gain


NORM_ROWS = 512
ATTN_UNROLL = 4
QK_SCALE = 0.125


def _attn_prompt_kernel(q_ref, k_ref, v_ref, qg_ref, kg_ref, sl_ref, *rest, S, dil, n_keep, head_base, chained):
    if chained:
        rest = rest[1:]
    o_ref, lse_ref, kv_ref, tmp, qd, kd, vd, od, ld, bias = rest
    msub = S // dil
    nb = msub // NK
    kstride = msub + NK
    lane = lax.broadcasted_iota(jnp.int32, (1, LANES), 1)
    head0 = lane < HEAD_DIM

    def norm_into_tmp(src_ref, g_ref):
        def body(i, carry):
            rows = pl.ds(pl.multiple_of(i * NORM_ROWS, NORM_ROWS), NORM_ROWS)
            tmp[rows, :] = _head_norm(src_ref[rows, :], g_ref[...], head0)
            return carry
        lax.fori_loop(0, S // NORM_ROWS, body, 0)

    norm_into_tmp(q_ref, qg_ref)
    for r in range(dil):
        qd[r * msub:(r + 1) * msub, :] = (tmp[pl.ds(r, msub, stride=dil), :] * QK_SCALE).astype(BF16)
    norm_into_tmp(k_ref, kg_ref)
    kv_ref[0] = tmp[S - n_keep:S, :].T
    kv_ref[1] = v_ref[S - n_keep:S, :].T
    zeros = jnp.zeros((NK, LANES), BF16)
    for r in range(dil):
        kd[r * kstride:r * kstride + NK, :] = zeros
        vd[r * kstride:r * kstride + NK, :] = zeros
        kd[r * kstride + NK:(r + 1) * kstride, :] = tmp[pl.ds(r, msub, stride=dil), :].astype(BF16)
        vd[r * kstride + NK:(r + 1) * kstride, :] = v_ref[pl.ds(r, msub, stride=dil), :].astype(BF16)

    qi = lax.broadcasted_iota(jnp.int32, (NK, 2 * NK), 0)
    ki = lax.broadcasted_iota(jnp.int32, (NK, 2 * NK), 1)
    dist = qi - ki + NK
    band = (dist >= 0) & (dist <= NK)
    alibi = (dist * dil).astype(F32)
    cur = ki >= NK
    for hh in range(2):
        slope = sl_ref[head_base + 2 * pl.program_id(1) + hh]
        bias[2 * hh] = jnp.where(band, -(slope * alibi), NEG)
        bias[2 * hh + 1] = jnp.where(band & cur, -(slope * alibi), NEG)

    def block(j, carry):
        r = j // nb
        n = j - r * nb
        first = jnp.where(n == 0, 1, 0)
        qb = pl.multiple_of(j * NK, NK)
        kb = pl.multiple_of(j * NK + r * NK, NK)
        q2 = qd[pl.ds(qb, NK), :]
        k2 = kd[pl.ds(kb, 2 * NK), :]
        v2 = vd[pl.ds(kb, 2 * NK), :]
        outs, lses = [], []
        for hh in range(2):
            hmask = head0 if hh == 0 else jnp.logical_not(head0)
            qm = jnp.where(hmask, q2, jnp.zeros_like(q2))
            s = lax.dot_general(qm, k2, (((1,), (1,)), ((), ())), preferred_element_type=F32)
            s = s + bias[2 * hh + first]
            m = jnp.max(s, axis=-1, keepdims=True)
            p = jnp.exp(s - m)
            lsum = jnp.sum(p, axis=-1, keepdims=True)
            outs.append(jnp.dot(p.astype(BF16), v2, preferred_element_type=F32) / lsum)
            lses.append(m + jnp.log(lsum))
        od[pl.ds(qb, NK), :] = jnp.where(head0, outs[0], outs[1])
        ld[pl.ds(qb, NK), :] = jnp.where(head0, lses[0], lses[1])
        return carry

    lax.fori_loop(0, dil * nb, block, 0, unroll=ATTN_UNROLL)

    for r in range(dil):
        o_ref[pl.ds(r, msub, stride=dil), :] = od[r * msub:(r + 1) * msub, :]
        lse_ref[pl.ds(r, msub, stride=dil), :] = ld[r * msub:(r + 1) * msub, :]


def attn_prompt(P, B, S, q_gain, k_gain, slopes, l, g, win_prev):
    window, dil = ATTN_GROUPS[g]
    n_keep = min(window, S)
    chained = win_prev is not None

    def col(c):
        return pl.BlockSpec((S, LANES), lambda b, p: (b, c // LANES + 2 * g + p))

    def par(stack):
        return pl.BlockSpec((None, 1, LANES), lambda b, p: (stack * 6 + 2 * g + p, 0, 0))

    in_specs = [col(COL_AQ), col(COL_AK), col(COL_AV), par(l), par(l), pl.BlockSpec(memory_space=pltpu.SMEM)]
    args = [P, P, P, q_gain, k_gain, slopes]
    if chained:
        in_specs.append(pl.BlockSpec(memory_space=pl.ANY))
        args.append(win_prev)
    return pl.pallas_call(
        functools.partial(_attn_prompt_kernel, S=S, dil=dil, n_keep=n_keep, head_base=4 * g, chained=chained),
        grid=(B, 2),
        in_specs=in_specs,
        out_specs=[
            pl.BlockSpec((None, S, LANES), lambda b, p: (b, 0, p)),
            pl.BlockSpec((None, S, LANES), lambda b, p: (b, 0, p)),
            pl.BlockSpec((None, 2, LANES, n_keep), lambda b, p: (l * B + b, 0, p, 0)),
        ],
        out_shape=[
            jax.ShapeDtypeStruct((B, S, 256), F32),
            jax.ShapeDtypeStruct((B, S, 256), F32),
            jax.ShapeDtypeStruct((DEPTH * B, 2, 256, n_keep), F32),
        ],
        input_output_aliases={6: 2} if chained else {},
        scratch_shapes=[
            pltpu.VMEM((S, LANES), F32),
            pltpu.VMEM((S, LANES), BF16),
            pltpu.VMEM((S + dil * NK, LANES), BF16),
            pltpu.VMEM((S + dil * NK, LANES), BF16),
            pltpu.VMEM((S, LANES), F32),
            pltpu.VMEM((S, LANES), F32),
            pltpu.VMEM((4, NK, 2 * NK), F32),
        ],
        compiler_params=_cparams(("parallel", "arbitrary")),
    )(*args)


T_PAD = 8


def _attn_sample_kernel(q_ref, k_ref, v_ref, qg_ref, kg_ref, sl_ref, cache_ref, *rest,
                        L, dil, t_true, head_base, chained):
    if chained:
        rest = rest[1:]
    o_ref, lse_ref, cnew_ref = rest
    lane = lax.broadcasted_iota(jnp.int32, (1, LANES), 1)
    head0 = lane < HEAD_DIM
    trow8 = lax.broadcasted_iota(jnp.int32, (T_PAD, LANES), 0)
    lane8 = lax.broadcasted_iota(jnp.int32, (T_PAD, LANES), 1)
    place = ((lane8 == LANES - t_true + trow8) & (trow8 < t_true)).astype(F32)
    is_new = lane >= LANES - t_true

    def rolled_with_new(rows, new):
        shifted = pltpu.roll(cache_ref[rows, :], L - t_true, 1)
        tail = lax.dot_general(new, place, (((0,), (0,)), ((), ())), preferred_element_type=F32,
                               precision=lax.Precision.HIGHEST)
        if L > LANES:
            cnew_ref[rows, 0:L - LANES] = shifted[:, 0:L - LANES]
        cnew_ref[rows, L - LANES:L] = jnp.where(is_new, tail, shifted[:, L - LANES:L])

    trow = lax.broadcasted_iota(jnp.int32, (T_PAD, 1), 0)
    scol = lax.broadcasted_iota(jnp.int32, (1, L), 1)
    dist = L + trow - scol
    valid_c = ((dist & (dil - 1)) == 0) & (dist <= NK * dil)
    alibi_c = dist.astype(F32)

    for p in range(2):
        ls = slice(p * LANES, (p + 1) * LANES)
        qn = _head_norm(q_ref[:, ls], qg_ref[:, ls], head0)
        kn = _head_norm(k_ref[:, ls], kg_ref[:, ls], head0)
        vn = v_ref[:, ls]
        krows = slice(p * LANES, (p + 1) * LANES)
        vrows = slice(256 + p * LANES, 256 + (p + 1) * LANES)
        rolled_with_new(krows, kn)
        rolled_with_new(vrows, vn)
        kc_t = cache_ref[krows, :].astype(BF16)
        vc_t = cache_ref[vrows, :].astype(BF16)
        knr = kn.astype(BF16).astype(F32)
        vnr = vn.astype(BF16).astype(F32)
        outs, lses = [], []
        for hh in range(2):
            hmask = head0 if hh == 0 else jnp.logical_not(head0)
            qm = jnp.where(hmask, qn, 0.0).astype(BF16)
            qmr = qm.astype(F32)
            slope = sl_ref[head_base + 2 * p + hh]
            sc = jnp.dot(qm, kc_t, preferred_element_type=F32)
            sc = sc / float(np.sqrt(HEAD_DIM)) - slope * alibi_c
            sc = jnp.where(valid_c, sc, NEG)
            m = jnp.max(sc, axis=-1, keepdims=True)
            sn = []
            for u in range(t_true):
                du = trow - u
                su = jnp.sum(qmr * knr[u:u + 1, :], axis=-1, keepdims=True) / float(np.sqrt(HEAD_DIM))
                su = su - slope * du.astype(F32)
                su = jnp.where((du >= 0) & ((du & (dil - 1)) == 0), su, NEG)
                sn.append(su)
                m = jnp.maximum(m, su)
            pc = jnp.exp(sc - m)
            lsum = jnp.sum(pc, axis=-1, keepdims=True)
            acc = _dot_nt(pc.astype(BF16), vc_t)
            for u in range(t_true):
                pu = jnp.exp(sn[u] - m)
                lsum = lsum + pu
                acc = acc + pu.astype(BF16).astype(F32) * vnr[u:u + 1, :]
            outs.append(acc / lsum)
            lses.append(m + jnp.log(lsum))
        o_ref[:, ls] = jnp.where(head0, outs[0], outs[1])
        lse_ref[:, ls] = jnp.where(head0, lses[0], lses[1])


def attn_sample(P, B, q_gain, k_gain, slopes, cache_t, l, g, t_true, win_prev):
    window, dil = ATTN_GROUPS[g]
    L = cache_t.shape[2]
    chained = win_prev is not None

    def col(c):
        return pl.BlockSpec((T_PAD, 256), lambda b: (b, c // 256 + g))

    def par(stack):
        return pl.BlockSpec((None, 1, 256), lambda b: (stack * 3 + g, 0, 0))

    in_specs = [col(COL_AQ), col(COL_AK), col(COL_AV), par(l), par(l), pl.BlockSpec(memory_space=pltpu.SMEM),
                pl.BlockSpec((None, 512, L), lambda b: (l * B + b, 0, 0))]
    args = [P, P, P, q_gain, k_gain, slopes, cache_t]
    if chained:
        in_specs.append(pl.BlockSpec(memory_space=pl.ANY))
        args.append(win_prev)
    return pl.pallas_call(
        functools.partial(_attn_sample_kernel, L=L, dil=dil, t_true=t_true, head_base=4 * g, chained=chained),
        grid=(B,),
        in_specs=in_specs,
        out_specs=[
            pl.BlockSpec((None, T_PAD, 256), lambda b: (b, 0, 0)),
            pl.BlockSpec((None, T_PAD, 256), lambda b: (b, 0, 0)),
            pl.BlockSpec((None, 512, L), lambda b: (l * B + b, 0, 0)),
        ],
        out_shape=[
            jax.ShapeDtypeStruct((B, T_PAD, 256), F32),
            jax.ShapeDtypeStruct((B, T_PAD, 256), F32),
            jax.ShapeDtypeStruct((DEPTH * B, 512, L), F32),
        ],
        input_output_aliases={7: 2} if chained else {},
        compiler_params=_cparams(("parallel",)),
    )(*args)


def _attn_merge_kernel(o0, l0, o1, l1, o2, l2, y_ref):
    a0, a1, a2 = l0[...], l1[...], l2[...]
    m = jnp.maximum(jnp.maximum(a0, a1), a2)
    e0, e1, e2 = jnp.exp(a0 - m), jnp.exp(a1 - m), jnp.exp(a2 - m)
    den = e0 + e1 + e2
    y = (e0 / den) * o0[...] + (e1 / den) * o1[...] + (e2 / den) * o2[...]
    y_ref[...] = y.astype(BF16)


def attn_merge(parts, tm, out_rows):
    M = parts[0].shape[0]
    spec = pl.BlockSpec((tm, 256), lambda m: (m, 0))
    return pl.pallas_call(
        _attn_merge_kernel,
        grid=(M // tm,),
        in_specs=[spec] * 6,
        out_specs=spec,
        out_shape=jax.ShapeDtypeStruct((out_rows, 256), BF16),
        compiler_params=_cparams(("parallel",)),
    )(*parts)


GATE_SRC0 = 5904
assert GATE_SRC0 % SRC_ALIGN == 0 and D_MODEL % SRC_ALIGN == 0


def _merge_kernel(xn_ref, ya_ref, yb_ref, yc_ref, yd_ref, g0, g1, g2, g3, u0, u1, u2, u3, o_ref,
                  gs0, gs1, gs2, gs3, us0, us1, us2, us3, *, tiles, tm):
    branches = ((ya_ref, g0, u0, gs0, us0), (yb_ref, g1, u1, gs1, us1),
                (yc_ref, g2, u2, gs2, us2), (yd_ref, g3, u3, gs3, us3))
    m = pl.program_id(1)

    @pl.when(m == 0)
    def _():
        for _, g_ref, u_ref, g_s, u_s in branches:
            g_s[...] = g_ref[...].astype(BF16)
            u_s[...] = u_ref[...].astype(BF16)

    def rows_fn(rows):
        xn = xn_ref[rows, :]
        acc = None
        for y_ref, _, _, g_s, u_s in branches:
            gate = jax.nn.sigmoid(_dot_nt(xn, g_s[...]))
            term = gate * jnp.dot(y_ref[rows, :], u_s[...], preferred_element_type=F32)
            acc = term if acc is None else acc + term
        o_ref[rows, :] = acc.astype(BF16)

    _on_tile_rows(m, tiles, tm, rows_fn)


def merge(xn, ya, yb, yc, yd, w_in_t, ups, l, tm):
    M = xn.shape[0]
    tiles = _row_tiles(M, tm)
    tn = PREP_ROWS
    widths = (512, 256, 512, 512)

    def act(width):
        return pl.BlockSpec((tm, width), lambda n, m: (m, 0))

    def gate(b):
        base = (GATE_SRC0 + b * D_MODEL) // SRC_ALIGN
        return _w_in_rows(l, lambda n, m: base + n * (tn // SRC_ALIGN))

    def up(width):
        return pl.BlockSpec((None, width, tn), lambda n, m: (l, 0, n))

    return pl.pallas_call(
        functools.partial(_merge_kernel, tiles=tiles, tm=tm),
        grid=(D_MODEL // tn, tiles[0]),
        in_specs=[act(D_MODEL)] + [act(w) for w in widths] + [gate(b) for b in range(N_BRANCH)]
                 + [up(w) for w in widths],
        out_specs=pl.BlockSpec((tm, tn), lambda n, m: (m, n)),
        out_shape=jax.ShapeDtypeStruct((M, D_MODEL), BF16),
        scratch_shapes=[pltpu.VMEM((tn, D_MODEL), BF16)] * N_BRANCH + [pltpu.VMEM((w, tn), BF16) for w in widths],
        compiler_params=_cparams(("parallel", "arbitrary")),
    )(xn, ya, yb, yc, yd, w_in_t, w_in_t, w_in_t, w_in_t, *ups)


def _layer(x, xn, W, l, T, mixers):
    h = ffn_up(xn, W['ff1_gate'], W['ff1_up'], l, T['up_m'], T['up_n'])
    x, xn = matmul_res(x, h, W['ff1_down'], l, 0.5, T['down_m'], W['n_mix'], l)
    P = mix_in(xn, W['w_in_t'], l, T['mix_m'])
    (ya, yb, yc, yd), states = mixers(P, l)
    mg = merge(xn, ya, yb, yc, yd, W['w_in_t'], (W['up_a'], W['up_b'], W['up_c'], W['up_d']), l, T['merge_m'])
    x, xn = matmul_res(x, mg, W['w_out'], l, 1.0, T['out_m'], W['n_ff2'], l)
    h = ffn_up(xn, W['ff2_gate'], W['ff2_up'], l, T['up_m'], T['up_n'])
    if l + 1 < DEPTH:
        x, xn = matmul_res(x, h, W['ff2_down'], l, 0.5, T['down_m'], W['n_ff1'], l + 1)
    else:
        x, xn = matmul_res(x, h, W['ff2_down'], l, 0.5, T['down_m'], out_split=True)
    return x, xn, states


SAMPLE_SLOT = 128
TILES = dict(up_m=2080, up_n=512, mix_m=1040, down_m=256, out_m=512, merge_m=1040)


def kernel(x_prompt, x_sample, state_conv, cache_w128_kv, cache_w512_kv, cache_w2048_kv, state_gla, state_pool,
           norm_ff1, ff1_gate, ff1_up, ff1_down, norm_mix, w_in, conv_w, attn_q_gain, attn_k_gain,
           gla_w_a2, gla_b_a, gla_norm, pool_w, pool_scale, w_up_conv, w_up_attn, w_up_gla, w_up_pool, w_out,
           norm_ff2, ff2_gate, ff2_up, ff2_down):
    BP, S, _ = x_prompt.shape
    BS, TS, _ = x_sample.shape
    caches = (cache_w128_kv, cache_w512_kv, cache_w2048_kv)

    W = {
        'n_ff1': norm_ff1.reshape(DEPTH, 1, D_MODEL), 'n_mix': norm_mix.reshape(DEPTH, 1, D_MODEL),
        'n_ff2': norm_ff2.reshape(DEPTH, 1, D_MODEL),
        'ff1_gate': ff1_gate, 'ff1_up': ff1_up, 'ff1_down': ff1_down.astype(BF16),
        'ff2_gate': ff2_gate, 'ff2_up': ff2_up, 'ff2_down': ff2_down.astype(BF16),
        'w_in_t': jnp.transpose(w_in, (0, 2, 1)),
        'up_a': w_up_conv, 'up_b': w_up_attn, 'up_c': w_up_gla, 'up_d': w_up_pool, 'w_out': w_out.astype(BF16),
    }
    wa_pad = jnp.pad(gla_w_a2, ((0, 0), (0, LANES - gla_w_a2.shape[1]), (0, 0))).astype(BF16)
    b_a = gla_b_a.reshape(DEPTH, 1, 256)
    gn = gla_norm.reshape(DEPTH, 1, LANES)
    pool_wb = pool_w.astype(BF16)
    pool_sc = pool_scale.reshape(DEPTH, 1, W_POOL)
    i = jnp.arange(1, N_ATTN_HEADS + 1, dtype=F32)
    slopes = jnp.exp2(-8.0 * i / N_ATTN_HEADS)
    qg6 = attn_q_gain.reshape(DEPTH * 6, 1, LANES)
    kg6 = attn_k_gain.reshape(DEPTH * 6, 1, LANES)
    qg3 = attn_q_gain.reshape(DEPTH * 3, 1, 256)
    kg3 = attn_k_gain.reshape(DEPTH * 3, 1, 256)

    zc = jnp.zeros((BP, CONV_HIST, W_CONV), F32)
    zp = jnp.zeros((BP, POOL_HIST16, W_POOL), F32)
    zs = jnp.zeros((BP, 256, LANES), F32)

    win_p = [None, None, None]
    win_s = [None, None, None]

    MP, MS = BP * S, BS * TS
    M_ALL = MP + SAMPLE_SLOT
    assert MS <= SAMPLE_SLOT

    def prompt_mixers(P, l):
        ya, yd, cnew, pnew = conv_pool(P, BP, S, zc, zp, conv_w, pool_wb, pool_sc, l, 512, 512, 0, M_ALL)
        yc, sfin = gla(P, BP, S, wa_pad, b_a, gn, zs, l, 512, GLA_CHUNK, GLA_CHUNK, M_ALL)
        parts = []
        for g in range(3):
            o, lse, win_p[g] = attn_prompt(P, BP, S, qg6, kg6, slopes, l, g, win_p[g])
            parts += [o.reshape(MP, 256), lse.reshape(MP, 256)]
        yb = attn_merge(parts, 1024, M_ALL)
        return (ya, yb, yc, yd), (cnew, sfin.reshape(BP, 4, GLA_DK, LANES), pnew[:, 1:])

    sconv8 = jnp.pad(state_conv, ((0, 0), (0, 0), (CONV_HIST - 2, 0), (0, 0)))
    spool16 = jnp.pad(state_pool, ((0, 0), (0, 0), (1, 0), (0, 0)))
    sgla = state_gla.reshape(DEPTH, BS, 256, LANES)
    cviews = [jnp.transpose(cc, (0, 1, 3, 4, 5, 2)).reshape(DEPTH * BS, 512, cc.shape[2]) for cc in caches]

    def sample_mixers(P, l):
        MSP = BS * T_PAD
        Ps = jnp.pad(P[MP:MP + MS].reshape(BS, TS, N_MIX), ((0, 0), (0, T_PAD - TS), (0, 0))).reshape(MSP, N_MIX)
        ya, yd, cnew, pnew = conv_pool(Ps, BS, T_PAD, sconv8[l], spool16[l], conv_w, pool_wb, pool_sc, l,
                                       T_PAD, TS, PAST_LEN, MSP)
        yc, sfin = gla(Ps, BS, T_PAD, wa_pad, b_a, gn, sgla[l], l, T_PAD, T_PAD, TS, MSP)
        parts = []
        for g in range(3):
            o, lse, win_s[g] = attn_sample(Ps, BS, qg3, kg3, slopes, cviews[g], l, g, TS, win_s[g])
            parts += [o[:, :TS].reshape(MS, 256), lse[:, :TS].reshape(MS, 256)]
        yb = attn_merge(parts, MS, MS)

        def valid(y):
            return y.reshape(BS, T_PAD, -1)[:, :TS].reshape(MS, -1)

        return (valid(ya), yb, valid(yc), valid(yd)), (cnew, sfin.reshape(BS, 4, GLA_DK, LANES), pnew[:, 1:])

    def mixers(P, l):
        ys_p, st_prompt = prompt_mixers(P, l)
        ys_s, st_sample = sample_mixers(P, l)
        ys = tuple(lax.dynamic_update_slice(yp_, jnp.pad(ys_, ((0, SAMPLE_SLOT - MS), (0, 0))), (MP, 0))
                   for yp_, ys_ in zip(ys_p, ys_s))
        return ys, (st_prompt, st_sample)

    x = (x_prompt.reshape(MP, D_MODEL), jnp.pad(x_sample.reshape(MS, D_MODEL), ((0, SAMPLE_SLOT - MS), (0, 0))))
    xn = norm(x[0], x[1], W['n_ff1'], 0, 1024)
    st_p, st_s = [], []
    for l in range(DEPTH):
        x, xn, (sp, ss) = _layer(x, xn, W, l, TILES, mixers)
        st_p.append(sp)
        st_s.append(ss)
    yp, ys = x[0], x[1][:MS]

    def stack(sts, f):
        return jnp.stack([f(s) for s in sts])

    def window_out(buf, B):
        n = buf.shape[-1]
        return jnp.transpose(buf.reshape(DEPTH, B, 2, 4, HEAD_DIM, n), (0, 1, 5, 2, 3, 4))

    outs = [yp.reshape(BP, S, D_MODEL), ys.reshape(BS, TS, D_MODEL),
            stack(st_p, lambda s: s[0]), stack(st_s, lambda s: s[0])]
    for g in range(3):
        outs += [window_out(win_p[g], BP), window_out(win_s[g], BS)]
    outs += [stack(st_p, lambda s: s[1]), stack(st_s, lambda s: s[1]),
             stack(st_p, lambda s: s[2]), stack(st_s, lambda s: s[2])]
    return tuple(outs)
```

```python
import functools

import numpy as np
import jax
import jax.numpy as jnp
from jax import lax
from jax.experimental import pallas as pl
from jax.experimental.pallas import tpu as pltpu

BF16 = jnp.bfloat16
F32 = jnp.float32

D_MODEL = 2048
DEPTH = 4
PAST_LEN = 16384
D_FF = 5504
EPS = 1e-6
N_BRANCH = 4
W_CONV = 512
ATTN_GROUPS = ((128, 1), (512, 4), (2048, 16))
N_ATTN_HEADS = 12
HEAD_DIM = 64
GLA_DK = 64
GLA_TAU = 16.0
GLA_CHUNK = 64
POOL_WINDOWS = (2, 4, 8, 16)
W_POOL = 512
NK = 128
LANES = 128
NEG = -1e30

COL_CB, COL_CC, COL_CH, COL_GV, COL_GR, COL_POOL = 0, 512, 1024, 1536, 2048, 2560
COL_AQ, COL_AK, COL_AV = 3072, 3840, 4608
COL_GQ, COL_GK, COL_LR = 5376, 5632, 5888
N_MIX = 6144
N_IN = 14096
N_GATE = N_BRANCH * D_MODEL
PREP_ROWS = 256
MIX_TILE_SRC = ([256 * j for j in range(6)] + [4352, 4608, 4864, 5120, 5392, 5648]
                + [1536 + 256 * j for j in range(9)] + [3840, 4096, 5376])
SRC_ALIGN = 16
assert all(r % SRC_ALIGN == 0 for r in MIX_TILE_SRC)

VMEM_LIMIT = 56 * 1024 * 1024


def _cparams(sem):
    return pltpu.CompilerParams(dimension_semantics=sem, vmem_limit_bytes=VMEM_LIMIT)


def _rms(x):
    return x * lax.rsqrt(jnp.mean(x * x, axis=-1, keepdims=True) + EPS)


def _row_tiles(M, tm):
    n = pl.cdiv(M, tm)
    return n, M - (n - 1) * tm


def _on_tile_rows(m, tiles, tm, fn):
    n, tail = tiles
    if tail == tm:
        fn(slice(0, tm))
        return
    pl.when(m < n - 1)(lambda: fn(slice(0, tm)))
    pl.when(m == n - 1)(lambda: fn(slice(0, tail)))


def _split_rows_specs(tm, n_tiles, slot):
    return (pl.BlockSpec((tm, D_MODEL), lambda m: (jnp.minimum(m, n_tiles - 2), 0)),
            pl.BlockSpec((slot, D_MODEL), lambda m: (0, 0)))


def _norm_kernel(xp_ref, xs_ref, g_ref, xn_ref, *, tiles, tm):
    n, tail = tiles
    m = pl.program_id(0)

    @pl.when(m < n - 1)
    def _():
        xn_ref[...] = (_rms(xp_ref[...]) * g_ref[...]).astype(BF16)

    @pl.when(m == n - 1)
    def _():
        xn_ref[0:tail, :] = (_rms(xs_ref[...]) * g_ref[...]).astype(BF16)


def norm(xp, xs, gain, l, tm):
    MP, slot = xp.shape[0], xs.shape[0]
    assert MP % tm == 0 and slot < tm
    tiles = (MP // tm + 1, slot)
    return pl.pallas_call(
        functools.partial(_norm_kernel, tiles=tiles, tm=tm),
        grid=(tiles[0],),
        in_specs=[*_split_rows_specs(tm, tiles[0], slot),
                  pl.BlockSpec((None, 1, D_MODEL), lambda m: (l, 0, 0))],
        out_specs=pl.BlockSpec((tm, D_MODEL), lambda m: (m, 0)),
        out_shape=jax.ShapeDtypeStruct((MP + slot, D_MODEL), BF16),
        compiler_params=_cparams(("parallel",)),
    )(xp, xs, gain)


SUB_ROWS = 208


def _sub_blocks(rows):
    n_rows = rows.stop - rows.start
    sub = SUB_ROWS if n_rows % SUB_ROWS == 0 else n_rows
    return [slice(r0, r0 + sub) for r0 in range(rows.start, rows.stop, sub)]


def _ffn_up_kernel(xn_ref, wg_ref, wu_ref, h_ref, wg_s, wu_s, *, tiles, tm):
    m = pl.program_id(1)

    @pl.when(m == 0)
    def _():
        wg_s[...] = wg_ref[...].astype(BF16)
        wu_s[...] = wu_ref[...].astype(BF16)

    def rows_fn(rows):
        for r in _sub_blocks(rows):
            xn = xn_ref[r, :]
            a = jnp.dot(xn, wg_s[...], preferred_element_type=F32)
            b = jnp.dot(xn, wu_s[...], preferred_element_type=F32)
            h_ref[r, :] = (a * jax.nn.sigmoid(a) * b).astype(BF16)

    _on_tile_rows(m, tiles, tm, rows_fn)


def ffn_up(xn, wg, wu, l, tm, tn):
    M = xn.shape[0]
    tiles = _row_tiles(M, tm)
    return pl.pallas_call(
        functools.partial(_ffn_up_kernel, tiles=tiles, tm=tm),
        grid=(pl.cdiv(D_FF, tn), tiles[0]),
        in_specs=[
            pl.BlockSpec((tm, D_MODEL), lambda n, m: (m, 0)),
            pl.BlockSpec((None, D_MODEL, tn), lambda n, m: (l, 0, n)),
            pl.BlockSpec((None, D_MODEL, tn), lambda n, m: (l, 0, n)),
        ],
        out_specs=pl.BlockSpec((tm, tn), lambda n, m: (m, n)),
        out_shape=jax.ShapeDtypeStruct((M, D_FF), BF16),
        scratch_shapes=[pltpu.VMEM((D_MODEL, tn), BF16), pltpu.VMEM((D_MODEL, tn), BF16)],
        compiler_params=_cparams(("parallel", "arbitrary")),
    )(xn, wg, wu)


def _matmul_res_kernel(*refs, scale, with_norm, tiles, tm, x_split, out_split):
    refs = list(refs)
    x_refs = [refs.pop(0) for _ in range(2 if x_split else 1)]
    a_ref, w_ref = refs.pop(0), refs.pop(0)
    g_ref = refs.pop(0) if with_norm else None
    o_refs = [refs.pop(0) for _ in range(2 if out_split else 1)]
    xn_ref = refs.pop(0) if with_norm else None
    n, tail = tiles
    m = pl.program_id(0)

    def rows_fn(rows, last):
        x_ref = x_refs[-1] if last else x_refs[0]
        o_ref = o_refs[-1] if last else o_refs[0]
        y = x_ref[rows, :] + scale * jnp.dot(a_ref[rows, :], w_ref[...], preferred_element_type=F32)
        if with_norm:
            xn_ref[rows, :] = (_rms(y) * g_ref[...]).astype(BF16)
        o_ref[rows, :] = y

    pl.when(m < n - 1)(lambda: rows_fn(slice(0, tm), False))
    pl.when(m == n - 1)(lambda: rows_fn(slice(0, tail), True))


def matmul_res(x, a, w, l, scale, tm, next_gain=None, next_l=0, out_split=False):
    M, K = a.shape
    with_norm = next_gain is not None
    x_split = isinstance(x, tuple)
    tiles = _row_tiles(M, tm)
    assert tiles[1] == SAMPLE_SLOT and (M - SAMPLE_SLOT) % tm == 0
    row = pl.BlockSpec((tm, D_MODEL), lambda m: (m, 0))
    split = _split_rows_specs(tm, tiles[0], SAMPLE_SLOT)
    in_specs = (list(split) if x_split else [row]) + [
        pl.BlockSpec((tm, K), lambda m: (m, 0)),
        pl.BlockSpec((None, K, D_MODEL), lambda m: (l, 0, 0), pipeline_mode=pl.Buffered(1))]
    args = (list(x) if x_split else [x]) + [a, w]
    if out_split:
        out_specs = list(split)
        out_shape = [jax.ShapeDtypeStruct((M - SAMPLE_SLOT, D_MODEL), F32),
                     jax.ShapeDtypeStruct((SAMPLE_SLOT, D_MODEL), F32)]
    else:
        out_specs, out_shape = [row], [jax.ShapeDtypeStruct((M, D_MODEL), F32)]
    if with_norm:
        in_specs.append(pl.BlockSpec((None, 1, D_MODEL), lambda m: (next_l, 0, 0)))
        args.append(next_gain)
        out_specs.append(row)
        out_shape.append(jax.ShapeDtypeStruct((M, D_MODEL), BF16))
    res = pl.pallas_call(
        functools.partial(_matmul_res_kernel, scale=scale, with_norm=with_norm, tiles=tiles, tm=tm,
                          x_split=x_split, out_split=out_split),
        grid=(tiles[0],),
        in_specs=in_specs,
        out_specs=out_specs,
        out_shape=out_shape,
        compiler_params=_cparams(("parallel",)),
    )(*args)
    y = tuple(res[:2]) if out_split else res[0]
    return y, (res[-1] if with_norm else None)


def _dot_nt(a, b):
    return lax.dot_general(a, b, (((1,), (1,)), ((), ())), preferred_element_type=F32)


def _w_in_rows(l, row_of):
    return pl.BlockSpec((None, pl.Element(PREP_ROWS), pl.Element(D_MODEL)),
                        lambda *idx: (l, row_of(*idx) * SRC_ALIGN, 0))


MIX_SLABS = 4


def _mix_in_kernel(src_ref, xn_ref, *rest, tiles, tm):
    del src_ref
    w_refs, (p_ref, w_s) = rest[:MIX_SLABS], rest[MIX_SLABS:]
    m = pl.program_id(1)

    @pl.when(m == 0)
    def _():
        for j, w_ref in enumerate(w_refs):
            w_s[j * PREP_ROWS:(j + 1) * PREP_ROWS, :] = w_ref[...].astype(BF16)

    def rows_fn(rows):
        p_ref[rows, :] = _dot_nt(xn_ref[rows, :], w_s[...])

    _on_tile_rows(m, tiles, tm, rows_fn)


def mix_in(xn, w_in_t, l, tm):
    M = xn.shape[0]
    tn = MIX_SLABS * PREP_ROWS
    tiles = _row_tiles(M, tm)
    src = jnp.asarray(MIX_TILE_SRC, jnp.int32) // SRC_ALIGN

    def slab(j):
        return _w_in_rows(l, lambda n, m, src: src[MIX_SLABS * n + j])

    return pl.pallas_call(
        functools.partial(_mix_in_kernel, tiles=tiles, tm=tm),
        grid_spec=pltpu.PrefetchScalarGridSpec(
            num_scalar_prefetch=1,
            grid=(N_MIX // tn, tiles[0]),
            in_specs=[pl.BlockSpec((tm, D_MODEL), lambda n, m, src: (m, 0))] + [slab(j) for j in range(MIX_SLABS)],
            out_specs=pl.BlockSpec((tm, tn), lambda n, m, src: (m, n)),
            scratch_shapes=[pltpu.VMEM((tn, D_MODEL), BF16)],
        ),
        out_shape=jax.ShapeDtypeStruct((M, N_MIX), F32),
        compiler_params=_cparams(("parallel", "arbitrary")),
    )(src, xn, *([w_in_t] * MIX_SLABS))


CONV_HIST = 8
POOL_HIST16 = 16


def _conv_pool_kernel(cb_ref, cc_ref, ch_ref, pin_ref, cbuf_ref, pbuf_ref, cw_ref, pw_ref, ps_ref,
                      ya_ref, yd_ref, cnew_ref, pnew_ref, uext, pext, *, tt, t_true, pos0):
    t = pl.program_id(1)

    @pl.when(t == 0)
    def _():
        uext[0:CONV_HIST, :] = cbuf_ref[...]
        pext[0:POOL_HIST16, :] = pbuf_ref[...]

    u = cc_ref[...] * ch_ref[...]
    uext[CONV_HIST:CONV_HIST + tt, :] = u
    w = cw_ref[...]
    z = uext[CONV_HIST - 2:CONV_HIST - 2 + tt, :] * w[0:1, :]
    z = z + uext[CONV_HIST - 1:CONV_HIST - 1 + tt, :] * w[1:2, :]
    z = z + u * w[2:3, :]
    ya_ref[...] = (cb_ref[...] * z).astype(BF16)
    cnew_ref[...] = uext[CONV_HIST + t_true - 2:CONV_HIST + t_true, :]
    uext[0:CONV_HIST, :] = uext[tt:tt + CONV_HIST, :]

    pin = pin_ref[...]
    pext[POOL_HIST16:POOL_HIST16 + tt, :] = pin
    pos = (pos0 + t * tt + lax.broadcasted_iota(jnp.int32, (tt, 1), 0)).astype(F32)
    for g, win in enumerate(POOL_WINDOWS):
        ls = slice(g * LANES, (g + 1) * LANES)
        acc = pin[:, ls]
        for i in range(1, win):
            acc = acc + pext[POOL_HIST16 - i:POOL_HIST16 - i + tt, ls]
        cnt = jnp.minimum(float(win), pos + 1.0)
        d = acc / cnt - pin[:, ls]
        y = jnp.dot(d.astype(BF16), pw_ref[g], preferred_element_type=F32) * ps_ref[:, ls]
        yd_ref[:, ls] = y.astype(BF16)
    pnew_ref[...] = pext[t_true:t_true + POOL_HIST16, :]
    pext[0:POOL_HIST16, :] = pext[tt:tt + POOL_HIST16, :]


def conv_pool(P, B, T, cbuf8, pbuf16, conv_w, pool_w, pool_scale, l, tt, t_true, pos0, out_rows):
    nt = T // tt

    def col(c):
        return pl.BlockSpec((tt, 512), lambda b, t: (b * nt + t, c // 512))

    return pl.pallas_call(
        functools.partial(_conv_pool_kernel, tt=tt, t_true=t_true, pos0=pos0),
        grid=(B, nt),
        in_specs=[
            col(COL_CB), col(COL_CC), col(COL_CH), col(COL_POOL),
            pl.BlockSpec((None, CONV_HIST, 512), lambda b, t: (b, 0, 0)),
            pl.BlockSpec((None, POOL_HIST16, 512), lambda b, t: (b, 0, 0)),
            pl.BlockSpec((None, 3, 512), lambda b, t: (l, 0, 0)),
            pl.BlockSpec((None, 4, LANES, LANES), lambda b, t: (l, 0, 0, 0)),
            pl.BlockSpec((None, 1, 512), lambda b, t: (l, 0, 0)),
        ],
        out_specs=[
            pl.BlockSpec((tt, 512), lambda b, t: (b * nt + t, 0)),
            pl.BlockSpec((tt, 512), lambda b, t: (b * nt + t, 0)),
            pl.BlockSpec((None, 2, 512), lambda b, t: (b, 0, 0)),
            pl.BlockSpec((None, POOL_HIST16, 512), lambda b, t: (b, 0, 0)),
        ],
        out_shape=[
            jax.ShapeDtypeStruct((out_rows, 512), BF16),
            jax.ShapeDtypeStruct((out_rows, 512), BF16),
            jax.ShapeDtypeStruct((B, 2, 512), F32),
            jax.ShapeDtypeStruct((B, POOL_HIST16, 512), F32),
        ],
        scratch_shapes=[pltpu.VMEM((CONV_HIST + tt, 512), F32), pltpu.VMEM((POOL_HIST16 + tt, 512), F32)],
        compiler_params=_cparams(("parallel", "arbitrary")),
    )(P, P, P, P, cbuf8, pbuf16, conv_w, pool_w, pool_scale)


GLA_UNROLL = 4

def _gla_kernel(q_ref, k_ref, v_ref, r_ref, lr_ref, wa_ref, ba_ref, gn_ref, s0_ref,
                yc_ref, sfin_ref, s_scr, b_scr, sprev_scr, *, tt, C, c_true):
    assert c_true == C or tt == C
    t = pl.program_id(1)

    @pl.when(t == 0)
    def _():
        s_scr[...] = s0_ref[...]

    row = lax.broadcasted_iota(jnp.int32, (C, 1), 0)
    lane = lax.broadcasted_iota(jnp.int32, (1, LANES), 1)
    head0 = lane < GLA_DK
    srow = lax.broadcasted_iota(jnp.int32, (LANES, 1), 0)
    eye = srow == lane
    gn = gn_ref[...]
    nchunk = tt // C

    def chunk_rows(c):
        return pl.ds(pl.multiple_of(c * C, C), C)

    def v_head(rows, h):
        v_h = v_ref[rows, h * LANES:(h + 1) * LANES]
        if c_true < C:
            v_h = jnp.where(row < c_true, v_h, 0.0)
        return v_h

    la = jnp.dot(lr_ref[...].astype(BF16), wa_ref[...], preferred_element_type=F32) + ba_ref[...]
    la = jax.nn.log_sigmoid(la) / GLA_TAU
    if c_true < C:
        la = jnp.where(row < c_true, la, 0.0)
    cb = 2 * C if nchunk % 2 == 0 else C
    bi = lax.broadcasted_iota(jnp.int32, (cb, 1), 0)
    bj = lax.broadcasted_iota(jnp.int32, (1, cb), 1)
    ones_tri = (((bi >= C) == (bj >= C)) & (bj <= bi)).astype(BF16)
    for blk in range(tt // cb):
        part = la[blk * cb:(blk + 1) * cb, :]
        acc = None
        for _ in range(3):
            term = part.astype(BF16)
            d = jnp.dot(ones_tri, term, preferred_element_type=F32)
            acc = d if acc is None else acc + d
            part = part - term.astype(F32)
        b_scr[blk * cb:(blk + 1) * cb, :] = acc

    def by_head(x):
        return jnp.concatenate([jnp.where(head0, x, 0.0), jnp.where(head0, 0.0, x)], axis=0)

    def v_pair(rows, p):
        return jnp.concatenate([v_head(rows, 2 * p), v_head(rows, 2 * p + 1)], axis=0).astype(BF16)

    def scan(c, carry):
        rows = chunk_rows(c)
        b = b_scr[rows, :]
        b_last = b[C - 1:C, :]
        kend = k_ref[rows, :] * jnp.exp(b_last - b)
        for p in range(2):
            ls = slice(p * LANES, (p + 1) * LANES)
            s_p = s_scr[ls, :]
            sprev_scr[c, ls, :] = s_p.astype(BF16)
            dec = jnp.exp(jnp.sum(jnp.where(eye, b_last[:, ls], 0.0), axis=1, keepdims=True))
            ds = lax.dot_general(by_head(kend[:, ls]).astype(BF16), v_pair(rows, p), (((0,), (0,)), ((), ())),
                                 preferred_element_type=F32)
            s_scr[ls, :] = dec * s_p + ds
        return carry

    lax.fori_loop(0, nchunk, scan, 0, unroll=min(2, nchunk))

    ri = lax.broadcasted_iota(jnp.int32, (2 * C, 1), 0)
    ci = lax.broadcasted_iota(jnp.int32, (1, 2 * C), 1)
    same_head_causal = ((ri >= C) == (ci >= C)) & ((ri % C) >= (ci % C))

    def emit(c, carry):
        rows = chunk_rows(c)
        b = b_scr[rows, :]
        qt = q_ref[rows, :] * (GLA_DK ** -0.5) * jnp.exp(b)
        kt = k_ref[rows, :] * jnp.exp(-b)
        for p in range(2):
            ls = slice(p * LANES, (p + 1) * LANES)
            qm = by_head(qt[:, ls]).astype(BF16)
            kt2 = jnp.concatenate([kt[:, ls], kt[:, ls]], axis=0).astype(BF16)
            a = jnp.where(same_head_causal, _dot_nt(qm, kt2), 0.0)
            o = jnp.dot(a.astype(BF16), v_pair(rows, p), preferred_element_type=F32)
            o = o + jnp.dot(qm, sprev_scr[c, ls, :], preferred_element_type=F32)
            on = _rms(o) * gn
            for hh in range(2):
                hs = slice((2 * p + hh) * LANES, (2 * p + hh + 1) * LANES)
                r_h = r_ref[rows, hs]
                yc_ref[rows, hs] = (on[hh * C:(hh + 1) * C, :] * (r_h * jax.nn.sigmoid(r_h))).astype(BF16)
        return carry

    lax.fori_loop(0, nchunk, emit, 0, unroll=min(GLA_UNROLL, nchunk))
    sfin_ref[...] = s_scr[...]


def gla(P, B, T, wa_pad, b_a, gla_norm, s0, l, tt, C, c_true, out_rows):
    nt = T // tt
    return pl.pallas_call(
        functools.partial(_gla_kernel, tt=tt, C=C, c_true=c_true),
        grid=(B, nt),
        in_specs=[
            pl.BlockSpec((tt, 256), lambda b, t: (b * nt + t, COL_GQ // 256)),
            pl.BlockSpec((tt, 256), lambda b, t: (b * nt + t, COL_GK // 256)),
            pl.BlockSpec((tt, 512), lambda b, t: (b * nt + t, COL_GV // 512)),
            pl.BlockSpec((tt, 512), lambda b, t: (b * nt + t, COL_GR // 512)),
            pl.BlockSpec((tt, LANES), lambda b, t: (b * nt + t, COL_LR // LANES)),
            pl.BlockSpec((None, LANES, 256), lambda b, t: (l, 0, 0)),
            pl.BlockSpec((None, 1, 256), lambda b, t: (l, 0, 0)),
            pl.BlockSpec((None, 1, LANES), lambda b, t: (l, 0, 0)),
            pl.BlockSpec((None, 256, LANES), lambda b, t: (b, 0, 0)),
        ],
        out_specs=[
            pl.BlockSpec((tt, 512), lambda b, t: (b * nt + t, 0)),
            pl.BlockSpec((None, 256, LANES), lambda b, t: (b, 0, 0)),
        ],
        out_shape=[jax.ShapeDtypeStruct((out_rows, 512), BF16), jax.ShapeDtypeStruct((B, 256, LANES), F32)],
        scratch_shapes=[pltpu.VMEM((256, LANES), F32), pltpu.VMEM((tt, 256), F32),
                        pltpu.VMEM((tt // C, 256, LANES), BF16)],
        compiler_params=_cparams(("parallel", "arbitrary")),
    )(P, P, P, P, P, wa_pad, b_a, gla_norm, s0)


def _head_norm(x, gain, head0):
    x2 = x * x
    s0 = jnp.sum(jnp.where(head0, x2, 0.0), axis=-1, keepdims=True)
    s1 = jnp.sum(jnp.where(head0, 0.0, x2), axis=-1, keepdims=True)
    ms = jnp.where(head0, s0, s1) / float(HEAD_DIM)
    return x * lax.rsqrt(ms + EPS) * gain


NORM_ROWS = 512
ATTN_UNROLL = 8
QK_SCALE = 0.125


def _attn_prompt_kernel(q_ref, k_ref, v_ref, qg_ref, kg_ref, sl_ref, *rest, S, dil, n_keep, head_base, chained):
    if chained:
        rest = rest[1:]
    o_ref, lse_ref, kv_ref, tmp, qd, kd, vd, od, ld, bias = rest
    msub = S // dil
    nb = msub // NK
    kstride = msub + NK
    lane = lax.broadcasted_iota(jnp.int32, (1, LANES), 1)
    head0 = lane < HEAD_DIM

    def norm_into_tmp(src_ref, g_ref):
        def body(i, carry):
            rows = pl.ds(pl.multiple_of(i * NORM_ROWS, NORM_ROWS), NORM_ROWS)
            tmp[rows, :] = _head_norm(src_ref[rows, :], g_ref[...], head0)
            return carry
        lax.fori_loop(0, S // NORM_ROWS, body, 0)

    norm_into_tmp(q_ref, qg_ref)
    for r in range(dil):
        qd[r * msub:(r + 1) * msub, :] = (tmp[pl.ds(r, msub, stride=dil), :] * QK_SCALE).astype(BF16)
    norm_into_tmp(k_ref, kg_ref)
    kv_ref[0] = tmp[S - n_keep:S, :].T
    kv_ref[1] = v_ref[S - n_keep:S, :].T
    zeros = jnp.zeros((NK, LANES), BF16)
    for r in range(dil):
        kd[r * kstride:r * kstride + NK, :] = zeros
        vd[r * kstride:r * kstride + NK, :] = zeros
        kd[r * kstride + NK:(r + 1) * kstride, :] = tmp[pl.ds(r, msub, stride=dil), :].astype(BF16)
        vd[r * kstride + NK:(r + 1) * kstride, :] = v_ref[pl.ds(r, msub, stride=dil), :].astype(BF16)

    qi = lax.broadcasted_iota(jnp.int32, (NK, 2 * NK), 0)
    ki = lax.broadcasted_iota(jnp.int32, (NK, 2 * NK), 1)
    dist = qi - ki + NK
    band = (dist >= 0) & (dist <= NK)
    alibi = (dist * dil).astype(F32)
    cur = ki >= NK
    for hh in range(2):
        slope = sl_ref[head_base + 2 * pl.program_id(1) + hh]
        bias[2 * hh] = jnp.where(band, -(slope * alibi), NEG)
        bias[2 * hh + 1] = jnp.where(band & cur, -(slope * alibi), NEG)

    def block(j, carry):
        r = j // nb
        n = j - r * nb
        first = jnp.where(n == 0, 1, 0)
        qb = pl.multiple_of(j * NK, NK)
        kb = pl.multiple_of(j * NK + r * NK, NK)
        q2 = qd[pl.ds(qb, NK), :]
        k2 = kd[pl.ds(kb, 2 * NK), :]
        v2 = vd[pl.ds(kb, 2 * NK), :]
        outs, lses = [], []
        for hh in range(2):
            hmask = head0 if hh == 0 else jnp.logical_not(head0)
            qm = jnp.where(hmask, q2, jnp.zeros_like(q2))
            s = lax.dot_general(qm, k2, (((1,), (1,)), ((), ())), preferred_element_type=F32)
            s = s + bias[2 * hh + first]
            m = jnp.max(s, axis=-1, keepdims=True)
            p = jnp.exp(s - m)
            lsum = jnp.sum(p, axis=-1, keepdims=True)
            outs.append(jnp.dot(p.astype(BF16), v2, preferred_element_type=F32) / lsum)
            lses.append(m + jnp.log(lsum))
        od[pl.ds(qb, NK), :] = jnp.where(head0, outs[0], outs[1])
        ld[pl.ds(qb, NK), :] = jnp.where(head0, lses[0], lses[1])
        return carry

    lax.fori_loop(0, dil * nb, block, 0, unroll=ATTN_UNROLL)

    for r in range(dil):
        o_ref[pl.ds(r, msub, stride=dil), :] = od[r * msub:(r + 1) * msub, :]
        lse_ref[pl.ds(r, msub, stride=dil), :] = ld[r * msub:(r + 1) * msub, :]


def attn_prompt(P, B, S, q_gain, k_gain, slopes, l, g, win_prev):
    window, dil = ATTN_GROUPS[g]
    n_keep = min(window, S)
    chained = win_prev is not None

    def col(c):
        return pl.BlockSpec((S, LANES), lambda b, p: (b, c // LANES + 2 * g + p))

    def par(stack):
        return pl.BlockSpec((None, 1, LANES), lambda b, p: (stack * 6 + 2 * g + p, 0, 0))

    in_specs = [col(COL_AQ), col(COL_AK), col(COL_AV), par(l), par(l), pl.BlockSpec(memory_space=pltpu.SMEM)]
    args = [P, P, P, q_gain, k_gain, slopes]
    if chained:
        in_specs.append(pl.BlockSpec(memory_space=pl.ANY))
        args.append(win_prev)
    return pl.pallas_call(
        functools.partial(_attn_prompt_kernel, S=S, dil=dil, n_keep=n_keep, head_base=4 * g, chained=chained),
        grid=(B, 2),
        in_specs=in_specs,
        out_specs=[
            pl.BlockSpec((None, S, LANES), lambda b, p: (b, 0, p)),
            pl.BlockSpec((None, S, LANES), lambda b, p: (b, 0, p)),
            pl.BlockSpec((None, 2, LANES, n_keep), lambda b, p: (l * B + b, 0, p, 0)),
        ],
        out_shape=[
            jax.ShapeDtypeStruct((B, S, 256), F32),
            jax.ShapeDtypeStruct((B, S, 256), F32),
            jax.ShapeDtypeStruct((DEPTH * B, 2, 256, n_keep), F32),
        ],
        input_output_aliases={6: 2} if chained else {},
        scratch_shapes=[
            pltpu.VMEM((S, LANES), F32),
            pltpu.VMEM((S, LANES), BF16),
            pltpu.VMEM((S + dil * NK, LANES), BF16),
            pltpu.VMEM((S + dil * NK, LANES), BF16),
            pltpu.VMEM((S, LANES), F32),
            pltpu.VMEM((S, LANES), F32),
            pltpu.VMEM((4, NK, 2 * NK), F32),
        ],
        compiler_params=_cparams(("parallel", "arbitrary")),
    )(*args)


T_PAD = 8


def _attn_sample_kernel(q_ref, k_ref, v_ref, qg_ref, kg_ref, sl_ref, cache_ref, *rest,
                        L, dil, t_true, head_base, chained):
    if chained:
        rest = rest[1:]
    o_ref, lse_ref, cnew_ref = rest
    lane = lax.broadcasted_iota(jnp.int32, (1, LANES), 1)
    head0 = lane < HEAD_DIM
    trow8 = lax.broadcasted_iota(jnp.int32, (T_PAD, LANES), 0)
    lane8 = lax.broadcasted_iota(jnp.int32, (T_PAD, LANES), 1)
    place = ((lane8 == LANES - t_true + trow8) & (trow8 < t_true)).astype(F32)
    is_new = lane >= LANES - t_true

    def rolled_with_new(rows, new):
        shifted = pltpu.roll(cache_ref[rows, :], L - t_true, 1)
        tail = lax.dot_general(new, place, (((0,), (0,)), ((), ())), preferred_element_type=F32,
                               precision=lax.Precision.HIGHEST)
        if L > LANES:
            cnew_ref[rows, 0:L - LANES] = shifted[:, 0:L - LANES]
        cnew_ref[rows, L - LANES:L] = jnp.where(is_new, tail, shifted[:, L - LANES:L])

    trow = lax.broadcasted_iota(jnp.int32, (T_PAD, 1), 0)
    scol = lax.broadcasted_iota(jnp.int32, (1, L), 1)
    dist = L + trow - scol
    valid_c = ((dist & (dil - 1)) == 0) & (dist <= NK * dil)
    alibi_c = dist.astype(F32)

    for p in range(2):
        ls = slice(p * LANES, (p + 1) * LANES)
        qn = _head_norm(q_ref[:, ls], qg_ref[:, ls], head0)
        kn = _head_norm(k_ref[:, ls], kg_ref[:, ls], head0)
        vn = v_ref[:, ls]
        krows = slice(p * LANES, (p + 1) * LANES)
        vrows = slice(256 + p * LANES, 256 + (p + 1) * LANES)
        rolled_with_new(krows, kn)
        rolled_with_new(vrows, vn)
        kc_t = cache_ref[krows, :].astype(BF16)
        vc_t = cache_ref[vrows, :].astype(BF16)
        knr = kn.astype(BF16).astype(F32)
        vnr = vn.astype(BF16).astype(F32)
        outs, lses = [], []
        for hh in range(2):
            hmask = head0 if hh == 0 else jnp.logical_not(head0)
            qm = jnp.where(hmask, qn, 0.0).astype(BF16)
            qmr = qm.astype(F32)
            slope = sl_ref[head_base + 2 * p + hh]
            sc = jnp.dot(qm, kc_t, preferred_element_type=F32)
            sc = sc / float(np.sqrt(HEAD_DIM)) - slope * alibi_c
            sc = jnp.where(valid_c, sc, NEG)
            m = jnp.max(sc, axis=-1, keepdims=True)
            sn = []
            for u in range(t_true):
                du = trow - u
                su = jnp.sum(qmr * knr[u:u + 1, :], axis=-1, keepdims=True) / float(np.sqrt(HEAD_DIM))
                su = su - slope * du.astype(F32)
                su = jnp.where((du >= 0) & ((du & (dil - 1)) == 0), su, NEG)
                sn.append(su)
                m = jnp.maximum(m, su)
            pc = jnp.exp(sc - m)
            lsum = jnp.sum(pc, axis=-1, keepdims=True)
            acc = _dot_nt(pc.astype(BF16), vc_t)
            for u in range(t_true):
                pu = jnp.exp(sn[u] - m)
                lsum = lsum + pu
                acc = acc + pu.astype(BF16).astype(F32) * vnr[u:u + 1, :]
            outs.append(acc / lsum)
            lses.append(m + jnp.log(lsum))
        o_ref[:, ls] = jnp.where(head0, outs[0], outs[1])
        lse_ref[:, ls] = jnp.where(head0, lses[0], lses[1])


def attn_sample(P, B, q_gain, k_gain, slopes, cache_t, l, g, t_true, win_prev):
    window, dil = ATTN_GROUPS[g]
    L = cache_t.shape[2]
    chained = win_prev is not None

    def col(c):
        return pl.BlockSpec((T_PAD, 256), lambda b: (b, c // 256 + g))

    def par(stack):
        return pl.BlockSpec((None, 1, 256), lambda b: (stack * 3 + g, 0, 0))

    in_specs = [col(COL_AQ), col(COL_AK), col(COL_AV), par(l), par(l), pl.BlockSpec(memory_space=pltpu.SMEM),
                pl.BlockSpec((None, 512, L), lambda b: (l * B + b, 0, 0))]
    args = [P, P, P, q_gain, k_gain, slopes, cache_t]
    if chained:
        in_specs.append(pl.BlockSpec(memory_space=pl.ANY))
        args.append(win_prev)
    return pl.pallas_call(
        functools.partial(_attn_sample_kernel, L=L, dil=dil, t_true=t_true, head_base=4 * g, chained=chained),
        grid=(B,),
        in_specs=in_specs,
        out_specs=[
            pl.BlockSpec((None, T_PAD, 256), lambda b: (b, 0, 0)),
            pl.BlockSpec((None, T_PAD, 256), lambda b: (b, 0, 0)),
            pl.BlockSpec((None, 512, L), lambda b: (l * B + b, 0, 0)),
        ],
        out_shape=[
            jax.ShapeDtypeStruct((B, T_PAD, 256), F32),
            jax.ShapeDtypeStruct((B, T_PAD, 256), F32),
            jax.ShapeDtypeStruct((DEPTH * B, 512, L), F32),
        ],
        input_output_aliases={7: 2} if chained else {},
        compiler_params=_cparams(("parallel",)),
    )(*args)


def _attn_merge_kernel(o0, l0, o1, l1, o2, l2, y_ref):
    a0, a1, a2 = l0[...], l1[...], l2[...]
    m = jnp.maximum(jnp.maximum(a0, a1), a2)
    e0, e1, e2 = jnp.exp(a0 - m), jnp.exp(a1 - m), jnp.exp(a2 - m)
    den = e0 + e1 + e2
    y = (e0 / den) * o0[...] + (e1 / den) * o1[...] + (e2 / den) * o2[...]
    y_ref[...] = y.astype(BF16)


def attn_merge(parts, tm, out_rows):
    M = parts[0].shape[0]
    spec = pl.BlockSpec((tm, 256), lambda m: (m, 0))
    return pl.pallas_call(
        _attn_merge_kernel,
        grid=(M // tm,),
        in_specs=[spec] * 6,
        out_specs=spec,
        out_shape=jax.ShapeDtypeStruct((out_rows, 256), BF16),
        compiler_params=_cparams(("parallel",)),
    )(*parts)


GATE_SRC0 = 5904
assert GATE_SRC0 % SRC_ALIGN == 0 and D_MODEL % SRC_ALIGN == 0


def _merge_kernel(xn_ref, ya_ref, yb_ref, yc_ref, yd_ref, g0, g1, g2, g3, u0, u1, u2, u3, o_ref,
                  gs0, gs1, gs2, gs3, us0, us1, us2, us3, *, tiles, tm):
    branches = ((ya_ref, g0, u0, gs0, us0), (yb_ref, g1, u1, gs1, us1),
                (yc_ref, g2, u2, gs2, us2), (yd_ref, g3, u3, gs3, us3))
    m = pl.program_id(1)

    @pl.when(m == 0)
    def _():
        for _, g_ref, u_ref, g_s, u_s in branches:
            g_s[...] = g_ref[...].astype(BF16)
            u_s[...] = u_ref[...].astype(BF16)

    def rows_fn(rows):
        xn = xn_ref[rows, :]
        acc = None
        for y_ref, _, _, g_s, u_s in branches:
            gate = jax.nn.sigmoid(_dot_nt(xn, g_s[...]))
            term = gate * jnp.dot(y_ref[rows, :], u_s[...], preferred_element_type=F32)
            acc = term if acc is None else acc + term
        o_ref[rows, :] = acc.astype(BF16)

    _on_tile_rows(m, tiles, tm, rows_fn)


def merge(xn, ya, yb, yc, yd, w_in_t, ups, l, tm):
    M = xn.shape[0]
    tiles = _row_tiles(M, tm)
    tn = PREP_ROWS
    widths = (512, 256, 512, 512)

    def act(width):
        return pl.BlockSpec((tm, width), lambda n, m: (m, 0))

    def gate(b):
        base = (GATE_SRC0 + b * D_MODEL) // SRC_ALIGN
        return _w_in_rows(l, lambda n, m: base + n * (tn // SRC_ALIGN))

    def up(width):
        return pl.BlockSpec((None, width, tn), lambda n, m: (l, 0, n))

    return pl.pallas_call(
        functools.partial(_merge_kernel, tiles=tiles, tm=tm),
        grid=(D_MODEL // tn, tiles[0]),
        in_specs=[act(D_MODEL)] + [act(w) for w in widths] + [gate(b) for b in range(N_BRANCH)]
                 + [up(w) for w in widths],
        out_specs=pl.BlockSpec((tm, tn), lambda n, m: (m, n)),
        out_shape=jax.ShapeDtypeStruct((M, D_MODEL), BF16),
        scratch_shapes=[pltpu.VMEM((tn, D_MODEL), BF16)] * N_BRANCH + [pltpu.VMEM((w, tn), BF16) for w in widths],
        compiler_params=_cparams(("parallel", "arbitrary")),
    )(xn, ya, yb, yc, yd, w_in_t, w_in_t, w_in_t, w_in_t, *ups)


def _layer(x, xn, W, l, T, mixers):
    h = ffn_up(xn, W['ff1_gate'], W['ff1_up'], l, T['up_m'], T['up_n'])
    x, xn = matmul_res(x, h, W['ff1_down'], l, 0.5, T['down_m'], W['n_mix'], l)
    P = mix_in(xn, W['w_in_t'], l, T['mix_m'])
    (ya, yb, yc, yd), states = mixers(P, l)
    mg = merge(xn, ya, yb, yc, yd, W['w_in_t'], (W['up_a'], W['up_b'], W['up_c'], W['up_d']), l, T['merge_m'])
    x, xn = matmul_res(x, mg, W['w_out'], l, 1.0, T['out_m'], W['n_ff2'], l)
    h = ffn_up(xn, W['ff2_gate'], W['ff2_up'], l, T['up_m'], T['up_n'])
    if l + 1 < DEPTH:
        x, xn = matmul_res(x, h, W['ff2_down'], l, 0.5, T['down_m'], W['n_ff1'], l + 1)
    else:
        x, xn = matmul_res(x, h, W['ff2_down'], l, 0.5, T['down_m'], out_split=True)
    return x, xn, states


SAMPLE_SLOT = 128
TILES = dict(up_m=2080, up_n=512, mix_m=1040, down_m=256, out_m=512, merge_m=1040)


def kernel(x_prompt, x_sample, state_conv, cache_w128_kv, cache_w512_kv, cache_w2048_kv, state_gla, state_pool,
           norm_ff1, ff1_gate, ff1_up, ff1_down, norm_mix, w_in, conv_w, attn_q_gain, attn_k_gain,
           gla_w_a2, gla_b_a, gla_norm, pool_w, pool_scale, w_up_conv, w_up_attn, w_up_gla, w_up_pool, w_out,
           norm_ff2, ff2_gate, ff2_up, ff2_down):
    BP, S, _ = x_prompt.shape
    BS, TS, _ = x_sample.shape
    caches = (cache_w128_kv, cache_w512_kv, cache_w2048_kv)

    W = {
        'n_ff1': norm_ff1.reshape(DEPTH, 1, D_MODEL), 'n_mix': norm_mix.reshape(DEPTH, 1, D_MODEL),
        'n_ff2': norm_ff2.reshape(DEPTH, 1, D_MODEL),
        'ff1_gate': ff1_gate, 'ff1_up': ff1_up, 'ff1_down': ff1_down.astype(BF16),
        'ff2_gate': ff2_gate, 'ff2_up': ff2_up, 'ff2_down': ff2_down.astype(BF16),
        'w_in_t': jnp.transpose(w_in, (0, 2, 1)),
        'up_a': w_up_conv, 'up_b': w_up_attn, 'up_c': w_up_gla, 'up_d': w_up_pool, 'w_out': w_out.astype(BF16),
    }
    wa_pad = jnp.pad(gla_w_a2, ((0, 0), (0, LANES - gla_w_a2.shape[1]), (0, 0))).astype(BF16)
    b_a = gla_b_a.reshape(DEPTH, 1, 256)
    gn = gla_norm.reshape(DEPTH, 1, LANES)
    pool_wb = pool_w.astype(BF16)
    pool_sc = pool_scale.reshape(DEPTH, 1, W_POOL)
    i = jnp.arange(1, N_ATTN_HEADS + 1, dtype=F32)
    slopes = jnp.exp2(-8.0 * i / N_ATTN_HEADS)
    qg6 = attn_q_gain.reshape(DEPTH * 6, 1, LANES)
    kg6 = attn_k_gain.reshape(DEPTH * 6, 1, LANES)
    qg3 = attn_q_gain.reshape(DEPTH * 3, 1, 256)
    kg3 = attn_k_gain.reshape(DEPTH * 3, 1, 256)

    zc = jnp.zeros((BP, CONV_HIST, W_CONV), F32)
    zp = jnp.zeros((BP, POOL_HIST16, W_POOL), F32)
    zs = jnp.zeros((BP, 256, LANES), F32)

    win_p = [None, None, None]
    win_s = [None, None, None]

    MP, MS = BP * S, BS * TS
    M_ALL = MP + SAMPLE_SLOT
    assert MS <= SAMPLE_SLOT

    def prompt_mixers(P, l):
        ya, yd, cnew, pnew = conv_pool(P, BP, S, zc, zp, conv_w, pool_wb, pool_sc, l, 512, 512, 0, M_ALL)
        yc, sfin = gla(P, BP, S, wa_pad, b_a, gn, zs, l, 512, GLA_CHUNK, GLA_CHUNK, M_ALL)
        parts = []
        for g in range(3):
            o, lse, win_p[g] = attn_prompt(P, BP, S, qg6, kg6, slopes, l, g, win_p[g])
            parts += [o.reshape(MP, 256), lse.reshape(MP, 256)]
        yb = attn_merge(parts, 1024, M_ALL)
        return (ya, yb, yc, yd), (cnew, sfin.reshape(BP, 4, GLA_DK, LANES), pnew[:, 1:])

    sconv8 = jnp.pad(state_conv, ((0, 0), (0, 0), (CONV_HIST - 2, 0), (0, 0)))
    spool16 = jnp.pad(state_pool, ((0, 0), (0, 0), (1, 0), (0, 0)))
    sgla = state_gla.reshape(DEPTH, BS, 256, LANES)
    cviews = [jnp.transpose(cc, (0, 1, 3, 4, 5, 2)).reshape(DEPTH * BS, 512, cc.shape[2]) for cc in caches]

    def sample_mixers(P, l):
        MSP = BS * T_PAD
        Ps = jnp.pad(P[MP:MP + MS].reshape(BS, TS, N_MIX), ((0, 0), (0, T_PAD - TS), (0, 0))).reshape(MSP, N_MIX)
        ya, yd, cnew, pnew = conv_pool(Ps, BS, T_PAD, sconv8[l], spool16[l], conv_w, pool_wb, pool_sc, l,
                                       T_PAD, TS, PAST_LEN, MSP)
        yc, sfin = gla(Ps, BS, T_PAD, wa_pad, b_a, gn, sgla[l], l, T_PAD, T_PAD, TS, MSP)
        parts = []
        for g in range(3):
            o, lse, win_s[g] = attn_sample(Ps, BS, qg3, kg3, slopes, cviews[g], l, g, TS, win_s[g])
            parts += [o[:, :TS].reshape(MS, 256), lse[:, :TS].reshape(MS, 256)]
        yb = attn_merge(parts, MS, MS)

        def valid(y):
            return y.reshape(BS, T_PAD, -1)[:, :TS].reshape(MS, -1)

        return (valid(ya), yb, valid(yc), valid(yd)), (cnew, sfin.reshape(BS, 4, GLA_DK, LANES), pnew[:, 1:])

    def mixers(P, l):
        ys_p, st_prompt = prompt_mixers(P, l)
        ys_s, st_sample = sample_mixers(P, l)
        ys = tuple(lax.dynamic_update_slice(yp_, jnp.pad(ys_, ((0, SAMPLE_SLOT - MS), (0, 0))), (MP, 0))
                   for yp_, ys_ in zip(ys_p, ys_s))
        return ys, (st_prompt, st_sample)

    x = (x_prompt.reshape(MP, D_MODEL), jnp.pad(x_sample.reshape(MS, D_MODEL), ((0, SAMPLE_SLOT - MS), (0, 0))))
    xn = norm(x[0], x[1], W['n_ff1'], 0, 1024)
    st_p, st_s = [], []
    for l in range(DEPTH):
        x, xn, (sp, ss) = _layer(x, xn, W, l, TILES, mixers)
        st_p.append(sp)
        st_s.append(ss)
    yp, ys = x[0], x[1][:MS]

    def stack(sts, f):
        return jnp.stack([f(s) for s in sts])

    def window_out(buf, B):
        n = buf.shape[-1]
        return jnp.transpose(buf.reshape(DEPTH, B, 2, 4, HEAD_DIM, n), (0, 1, 5, 2, 3, 4))

    outs = [yp.reshape(BP, S, D_MODEL), ys.reshape(BS, TS, D_MODEL),
            stack(st_p, lambda s: s[0]), stack(st_s, lambda s: s[0])]
    for g in range(3):
        outs += [window_out(win_p[g], BP), window_out(win_s[g], BS)]
    outs += [stack(st_p, lambda s: s[1]), stack(st_s, lambda s: s[1]),
             stack(st_p, lambda s: s[2]), stack(st_s, lambda s: s[2])]
    return tuple(outs)
```

```python
import functools

import numpy as np
import jax
import jax.numpy as jnp
from jax import lax
from jax.experimental import pallas as pl
from jax.experimental.pallas import tpu as pltpu

BF16 = jnp.bfloat16
F32 = jnp.float32

D_MODEL = 2048
DEPTH = 4
PAST_LEN = 16384
D_FF = 5504
EPS = 1e-6
N_BRANCH = 4
W_CONV = 512
ATTN_GROUPS = ((128, 1), (512, 4), (2048, 16))
N_ATTN_HEADS = 12
HEAD_DIM = 64
GLA_DK = 64
GLA_TAU = 16.0
GLA_CHUNK = 64
POOL_WINDOWS = (2, 4, 8, 16)
W_POOL = 512
NK = 128
LANES = 128
NEG = -1e30

COL_CB, COL_CC, COL_CH, COL_GV, COL_GR, COL_POOL = 0, 512, 1024, 1536, 2048, 2560
COL_AQ, COL_AK, COL_AV = 3072, 3840, 4608
COL_GQ, COL_GK, COL_LR = 5376, 5632, 5888
N_MIX = 6144
N_IN = 14096
N_GATE = N_BRANCH * D_MODEL
PREP_ROWS = 256
MIX_TILE_SRC = ([256 * j for j in range(6)] + [4352, 4608, 4864, 5120, 5392, 5648]
                + [1536 + 256 * j for j in range(9)] + [3840, 4096, 5376])
SRC_ALIGN = 16
assert all(r % SRC_ALIGN == 0 for r in MIX_TILE_SRC)

VMEM_LIMIT = 56 * 1024 * 1024


def _cparams(sem):
    return pltpu.CompilerParams(dimension_semantics=sem, vmem_limit_bytes=VMEM_LIMIT)


def _rms(x):
    return x * lax.rsqrt(jnp.mean(x * x, axis=-1, keepdims=True) + EPS)


def _row_tiles(M, tm):
    n = pl.cdiv(M, tm)
    return n, M - (n - 1) * tm


def _on_tile_rows(m, tiles, tm, fn):
    n, tail = tiles
    if tail == tm:
        fn(slice(0, tm))
        return
    pl.when(m < n - 1)(lambda: fn(slice(0, tm)))
    pl.when(m == n - 1)(lambda: fn(slice(0, tail)))


def _split_rows_specs(tm, n_tiles, slot):
    return (pl.BlockSpec((tm, D_MODEL), lambda m: (jnp.minimum(m, n_tiles - 2), 0)),
            pl.BlockSpec((slot, D_MODEL), lambda m: (0, 0)))


def _norm_kernel(xp_ref, xs_ref, g_ref, xn_ref, *, tiles, tm):
    n, tail = tiles
    m = pl.program_id(0)

    @pl.when(m < n - 1)
    def _():
        xn_ref[...] = (_rms(xp_ref[...]) * g_ref[...]).astype(BF16)

    @pl.when(m == n - 1)
    def _():
        xn_ref[0:tail, :] = (_rms(xs_ref[...]) * g_ref[...]).astype(BF16)


def norm(xp, xs, gain, l, tm):
    MP, slot = xp.shape[0], xs.shape[0]
    assert MP % tm == 0 and slot < tm
    tiles = (MP // tm + 1, slot)
    return pl.pallas_call(
        functools.partial(_norm_kernel, tiles=tiles, tm=tm),
        grid=(tiles[0],),
        in_specs=[*_split_rows_specs(tm, tiles[0], slot),
                  pl.BlockSpec((None, 1, D_MODEL), lambda m: (l, 0, 0))],
        out_specs=pl.BlockSpec((tm, D_MODEL), lambda m: (m, 0)),
        out_shape=jax.ShapeDtypeStruct((MP + slot, D_MODEL), BF16),
        compiler_params=_cparams(("parallel",)),
    )(xp, xs, gain)


SUB_ROWS = 208


def _sub_blocks(rows):
    n_rows = rows.stop - rows.start
    sub = SUB_ROWS if n_rows % SUB_ROWS == 0 else n_rows
    return [slice(r0, r0 + sub) for r0 in range(rows.start, rows.stop, sub)]


def _ffn_up_kernel(xn_ref, wg_ref, wu_ref, h_ref, wg_s, wu_s, *, tiles, tm):
    m = pl.program_id(1)

    @pl.when(m == 0)
    def _():
        wg_s[...] = wg_ref[...].astype(BF16)
        wu_s[...] = wu_ref[...].astype(BF16)

    def rows_fn(rows):
        for r in _sub_blocks(rows):
            xn = xn_ref[r, :]
            a = jnp.dot(xn, wg_s[...], preferred_element_type=F32)
            b = jnp.dot(xn, wu_s[...], preferred_element_type=F32)
            h_ref[r, :] = (a * jax.nn.sigmoid(a) * b).astype(BF16)

    _on_tile_rows(m, tiles, tm, rows_fn)


def ffn_up(xn, wg, wu, l, tm, tn):
    M = xn.shape[0]
    tiles = _row_tiles(M, tm)
    return pl.pallas_call(
        functools.partial(_ffn_up_kernel, tiles=tiles, tm=tm),
        grid=(pl.cdiv(D_FF, tn), tiles[0]),
        in_specs=[
            pl.BlockSpec((tm, D_MODEL), lambda n, m: (m, 0)),
            pl.BlockSpec((None, D_MODEL, tn), lambda n, m: (l, 0, n)),
            pl.BlockSpec((None, D_MODEL, tn), lambda n, m: (l, 0, n)),
        ],
        out_specs=pl.BlockSpec((tm, tn), lambda n, m: (m, n)),
        out_shape=jax.ShapeDtypeStruct((M, D_FF), BF16),
        scratch_shapes=[pltpu.VMEM((D_MODEL, tn), BF16), pltpu.VMEM((D_MODEL, tn), BF16)],
        compiler_params=_cparams(("parallel", "arbitrary")),
    )(xn, wg, wu)


def _matmul_res_kernel(*refs, scale, with_norm, tiles, tm, x_split, out_split):
    refs = list(refs)
    x_refs = [refs.pop(0) for _ in range(2 if x_split else 1)]
    a_ref, w_ref = refs.pop(0), refs.pop(0)
    g_ref = refs.pop(0) if with_norm else None
    o_refs = [refs.pop(0) for _ in range(2 if out_split else 1)]
    xn_ref = refs.pop(0) if with_norm else None
    n, tail = tiles
    m = pl.program_id(0)

    def rows_fn(rows, last):
        x_ref = x_refs[-1] if last else x_refs[0]
        o_ref = o_refs[-1] if last else o_refs[0]
        y = x_ref[rows, :] + scale * jnp.dot(a_ref[rows, :], w_ref[...], preferred_element_type=F32)
        if with_norm:
            xn_ref[rows, :] = (_rms(y) * g_ref[...]).astype(BF16)
        o_ref[rows, :] = y

    pl.when(m < n - 1)(lambda: rows_fn(slice(0, tm), False))
    pl.when(m == n - 1)(lambda: rows_fn(slice(0, tail), True))


def matmul_res(x, a, w, l, scale, tm, next_gain=None, next_l=0, out_split=False):
    M, K = a.shape
    with_norm = next_gain is not None
    x_split = isinstance(x, tuple)
    tiles = _row_tiles(M, tm)
    assert tiles[1] == SAMPLE_SLOT and (M - SAMPLE_SLOT) % tm == 0
    row = pl.BlockSpec((tm, D_MODEL), lambda m: (m, 0))
    split = _split_rows_specs(tm, tiles[0], SAMPLE_SLOT)
    in_specs = (list(split) if x_split else [row]) + [
        pl.BlockSpec((tm, K), lambda m: (m, 0)),
        pl.BlockSpec((None, K, D_MODEL), lambda m: (l, 0, 0), pipeline_mode=pl.Buffered(1))]
    args = (list(x) if x_split else [x]) + [a, w]
    if out_split:
        out_specs = list(split)
        out_shape = [jax.ShapeDtypeStruct((M - SAMPLE_SLOT, D_MODEL), F32),
                     jax.ShapeDtypeStruct((SAMPLE_SLOT, D_MODEL), F32)]
    else:
        out_specs, out_shape = [row], [jax.ShapeDtypeStruct((M, D_MODEL), F32)]
    if with_norm:
        in_specs.append(pl.BlockSpec((None, 1, D_MODEL), lambda m: (next_l, 0, 0)))
        args.append(next_gain)
        out_specs.append(row)
        out_shape.append(jax.ShapeDtypeStruct((M, D_MODEL), BF16))
    res = pl.pallas_call(
        functools.partial(_matmul_res_kernel, scale=scale, with_norm=with_norm, tiles=tiles, tm=tm,
                          x_split=x_split, out_split=out_split),
        grid=(tiles[0],),
        in_specs=in_specs,
        out_specs=out_specs,
        out_shape=out_shape,
        compiler_params=_cparams(("parallel",)),
    )(*args)
    y = tuple(res[:2]) if out_split else res[0]
    return y, (res[-1] if with_norm else None)


def _dot_nt(a, b):
    return lax.dot_general(a, b, (((1,), (1,)), ((), ())), preferred_element_type=F32)


def _w_in_rows(l, row_of):
    return pl.BlockSpec((None, pl.Element(PREP_ROWS), pl.Element(D_MODEL)),
                        lambda *idx: (l, row_of(*idx) * SRC_ALIGN, 0))


MIX_SLABS = 4


def _mix_in_kernel(src_ref, xn_ref, *rest, tiles, tm):
    del src_ref
    w_refs, (p_ref, w_s) = rest[:MIX_SLABS], rest[MIX_SLABS:]
    m = pl.program_id(1)

    @pl.when(m == 0)
    def _():
        for j, w_ref in enumerate(w_refs):
            w_s[j * PREP_ROWS:(j + 1) * PREP_ROWS, :] = w_ref[...].astype(BF16)

    def rows_fn(rows):
        p_ref[rows, :] = _dot_nt(xn_ref[rows, :], w_s[...])

    _on_tile_rows(m, tiles, tm, rows_fn)


def mix_in(xn, w_in_t, l, tm):
    M = xn.shape[0]
    tn = MIX_SLABS * PREP_ROWS
    tiles = _row_tiles(M, tm)
    src = jnp.asarray(MIX_TILE_SRC, jnp.int32) // SRC_ALIGN

    def slab(j):
        return _w_in_rows(l, lambda n, m, src: src[MIX_SLABS * n + j])

    return pl.pallas_call(
        functools.partial(_mix_in_kernel, tiles=tiles, tm=tm),
        grid_spec=pltpu.PrefetchScalarGridSpec(
            num_scalar_prefetch=1,
            grid=(N_MIX // tn, tiles[0]),
            in_specs=[pl.BlockSpec((tm, D_MODEL), lambda n, m, src: (m, 0))] + [slab(j) for j in range(MIX_SLABS)],
            out_specs=pl.BlockSpec((tm, tn), lambda n, m, src: (m, n)),
            scratch_shapes=[pltpu.VMEM((tn, D_MODEL), BF16)],
        ),
        out_shape=jax.ShapeDtypeStruct((M, N_MIX), F32),
        compiler_params=_cparams(("parallel", "arbitrary")),
    )(src, xn, *([w_in_t] * MIX_SLABS))


CONV_HIST = 8
POOL_HIST16 = 16


def _conv_pool_kernel(cb_ref, cc_ref, ch_ref, pin_ref, cbuf_ref, pbuf_ref, cw_ref, pw_ref, ps_ref, *rest,
                      tt, t_true, pos0, n_seq, slot_rows):
    if slot_rows:
        ya_slot_ref, yd_slot_ref, ya_ref, yd_ref = rest[:4]

        @pl.when((pl.program_id(0) == n_seq) & (pl.program_id(1) == 0))
        def _():
            ya_ref[0:slot_rows, :] = ya_slot_ref[...]
            yd_ref[0:slot_rows, :] = yd_slot_ref[...]

        pl.when(pl.program_id(0) < n_seq)(
            lambda: _conv_pool_body(cb_ref, cc_ref, ch_ref, pin_ref, cbuf_ref, pbuf_ref, cw_ref, pw_ref, ps_ref,
                                    *rest[2:], tt=tt, t_true=t_true, pos0=pos0))
    else:
        _conv_pool_body(cb_ref, cc_ref, ch_ref, pin_ref, cbuf_ref, pbuf_ref, cw_ref, pw_ref, ps_ref, *rest,
                        tt=tt, t_true=t_true, pos0=pos0)


def _conv_pool_body(cb_ref, cc_ref, ch_ref, pin_ref, cbuf_ref, pbuf_ref, cw_ref, pw_ref, ps_ref,
                    ya_ref, yd_ref, cnew_ref, pnew_ref, uext, pext, *, tt, t_true, pos0):
    t = pl.program_id(1)

    @pl.when(t == 0)
    def _():
        uext[0:CONV_HIST, :] = cbuf_ref[...]
        pext[0:POOL_HIST16, :] = pbuf_ref[...]

    u = cc_ref[...] * ch_ref[...]
    uext[CONV_HIST:CONV_HIST + tt, :] = u
    w = cw_ref[...]
    z = uext[CONV_HIST - 2:CONV_HIST - 2 + tt, :] * w[0:1, :]
    z = z + uext[CONV_HIST - 1:CONV_HIST - 1 + tt, :] * w[1:2, :]
    z = z + u * w[2:3, :]
    ya_ref[...] = (cb_ref[...] * z).astype(BF16)
    cnew_ref[...] = uext[CONV_HIST + t_true - 2:CONV_HIST + t_true, :]
    uext[0:CONV_HIST, :] = uext[tt:tt + CONV_HIST, :]

    pin = pin_ref[...]
    pext[POOL_HIST16:POOL_HIST16 + tt, :] = pin
    pos = (pos0 + t * tt + lax.broadcasted_iota(jnp.int32, (tt, 1), 0)).astype(F32)
    for g, win in enumerate(POOL_WINDOWS):
        ls = slice(g * LANES, (g + 1) * LANES)
        acc = pin[:, ls]
        for i in range(1, win):
            acc = acc + pext[POOL_HIST16 - i:POOL_HIST16 - i + tt, ls]
        cnt = jnp.minimum(float(win), pos + 1.0)
        d = acc / cnt - pin[:, ls]
        y = jnp.dot(d.astype(BF16), pw_ref[g], preferred_element_type=F32) * ps_ref[:, ls]
        yd_ref[:, ls] = y.astype(BF16)
    pnew_ref[...] = pext[t_true:t_true + POOL_HIST16, :]
    pext[0:POOL_HIST16, :] = pext[tt:tt + POOL_HIST16, :]


def _seq_tiles(B, nt, with_slot):
    if not with_slot:
        return B, (lambda b, t: b * nt + t), (lambda b, t: b * nt + t), (lambda b: b)
    return (B + 1,
            lambda b, t: jnp.where(b < B, b * nt + t, B * nt - 1),
            lambda b, t: jnp.where(b < B, b * nt + t, B * nt),
            lambda b: jnp.minimum(b, B - 1))


def conv_pool(P, B, T, cbuf8, pbuf16, conv_w, pool_w, pool_scale, l, tt, t_true, pos0, slot=None):
    nt = T // tt
    slot_rows = slot[0].shape[0] if slot else 0
    rows_b, in_blk, out_blk, seq_blk = _seq_tiles(B, nt, bool(slot))

    def col(c):
        return pl.BlockSpec((tt, 512), lambda b, t: (in_blk(b, t), c // 512))

    slot_spec = pl.BlockSpec((slot_rows, 512), lambda b, t: (0, 0))
    return pl.pallas_call(
        functools.partial(_conv_pool_kernel, tt=tt, t_true=t_true, pos0=pos0, n_seq=B, slot_rows=slot_rows),
        grid=(rows_b, nt),
        in_specs=[
            col(COL_CB), col(COL_CC), col(COL_CH), col(COL_POOL),
            pl.BlockSpec((None, CONV_HIST, 512), lambda b, t: (seq_blk(b), 0, 0)),
            pl.BlockSpec((None, POOL_HIST16, 512), lambda b, t: (seq_blk(b), 0, 0)),
            pl.BlockSpec((None, 3, 512), lambda b, t: (l, 0, 0)),
            pl.BlockSpec((None, 4, LANES, LANES), lambda b, t: (l, 0, 0, 0)),
            pl.BlockSpec((None, 1, 512), lambda b, t: (l, 0, 0)),
        ] + ([slot_spec, slot_spec] if slot else []),
        out_specs=[
            pl.BlockSpec((tt, 512), lambda b, t: (out_blk(b, t), 0)),
            pl.BlockSpec((tt, 512), lambda b, t: (out_blk(b, t), 0)),
            pl.BlockSpec((None, 2, 512), lambda b, t: (seq_blk(b), 0, 0)),
            pl.BlockSpec((None, POOL_HIST16, 512), lambda b, t: (seq_blk(b), 0, 0)),
        ],
        out_shape=[
            jax.ShapeDtypeStruct((B * T + slot_rows, 512), BF16),
            jax.ShapeDtypeStruct((B * T + slot_rows, 512), BF16),
            jax.ShapeDtypeStruct((B, 2, 512), F32),
            jax.ShapeDtypeStruct((B, POOL_HIST16, 512), F32),
        ],
        scratch_shapes=[pltpu.VMEM((CONV_HIST + tt, 512), F32), pltpu.VMEM((POOL_HIST16 + tt, 512), F32)],
        compiler_params=_cparams(("arbitrary", "arbitrary")),
    )(P, P, P, P, cbuf8, pbuf16, conv_w, pool_w, pool_scale, *(slot or ()))


GLA_UNROLL = 4

def _gla_kernel(q_ref, k_ref, v_ref, r_ref, lr_ref, wa_ref, ba_ref, gn_ref, s0_ref, *rest,
                tt, C, c_true, n_seq, slot_rows):
    if slot_rows:
        yc_slot_ref, yc_ref = rest[:2]

        @pl.when((pl.program_id(0) == n_seq) & (pl.program_id(1) == 0))
        def _():
            yc_ref[0:slot_rows, :] = yc_slot_ref[...]

        pl.when(pl.program_id(0) < n_seq)(
            lambda: _gla_body(q_ref, k_ref, v_ref, r_ref, lr_ref, wa_ref, ba_ref, gn_ref, s0_ref, *rest[1:],
                              tt=tt, C=C, c_true=c_true))
    else:
        _gla_body(q_ref, k_ref, v_ref, r_ref, lr_ref, wa_ref, ba_ref, gn_ref, s0_ref, *rest,
                  tt=tt, C=C, c_true=c_true)


def _gla_body(q_ref, k_ref, v_ref, r_ref, lr_ref, wa_ref, ba_ref, gn_ref, s0_ref,
              yc_ref, sfin_ref, s_scr, b_scr, sprev_scr, *, tt, C, c_true):
    assert c_true == C or tt == C
    t = pl.program_id(1)

    @pl.when(t == 0)
    def _():
        s_scr[...] = s0_ref[...]

    row = lax.broadcasted_iota(jnp.int32, (C, 1), 0)
    lane = lax.broadcasted_iota(jnp.int32, (1, LANES), 1)
    head0 = lane < GLA_DK
    srow = lax.broadcasted_iota(jnp.int32, (LANES, 1), 0)
    eye = srow == lane
    gn = gn_ref[...]
    nchunk = tt // C

    def chunk_rows(c):
        return pl.ds(pl.multiple_of(c * C, C), C)

    def v_head(rows, h):
        v_h = v_ref[rows, h * LANES:(h + 1) * LANES]
        if c_true < C:
            v_h = jnp.where(row < c_true, v_h, 0.0)
        return v_h

    la = jnp.dot(lr_ref[...].astype(BF16), wa_ref[...], preferred_element_type=F32) + ba_ref[...]
    la = jax.nn.log_sigmoid(la) / GLA_TAU
    if c_true < C:
        la = jnp.where(row < c_true, la, 0.0)
    cb = 2 * C if nchunk % 2 == 0 else C
    bi = lax.broadcasted_iota(jnp.int32, (cb, 1), 0)
    bj = lax.broadcasted_iota(jnp.int32, (1, cb), 1)
    ones_tri = (((bi >= C) == (bj >= C)) & (bj <= bi)).astype(BF16)
    for blk in range(tt // cb):
        part = la[blk * cb:(blk + 1) * cb, :]
        acc = None
        for _ in range(3):
            term = part.astype(BF16)
            d = jnp.dot(ones_tri, term, preferred_element_type=F32)
            acc = d if acc is None else acc + d
            part = part - term.astype(F32)
        b_scr[blk * cb:(blk + 1) * cb, :] = acc

    def by_head(x):
        return jnp.concatenate([jnp.where(head0, x, 0.0), jnp.where(head0, 0.0, x)], axis=0)

    def v_pair(rows, p):
        return jnp.concatenate([v_head(rows, 2 * p), v_head(rows, 2 * p + 1)], axis=0).astype(BF16)

    def scan(c, carry):
        rows = chunk_rows(c)
        b = b_scr[rows, :]
        b_last = b[C - 1:C, :]
        kend = k_ref[rows, :] * jnp.exp(b_last - b)
        for p in range(2):
            ls = slice(p * LANES, (p + 1) * LANES)
            s_p = s_scr[ls, :]
            sprev_scr[c, ls, :] = s_p.astype(BF16)
            dec = jnp.exp(jnp.sum(jnp.where(eye, b_last[:, ls], 0.0), axis=1, keepdims=True))
            ds = lax.dot_general(by_head(kend[:, ls]).astype(BF16), v_pair(rows, p), (((0,), (0,)), ((), ())),
                                 preferred_element_type=F32)
            s_scr[ls, :] = dec * s_p + ds
        return carry

    lax.fori_loop(0, nchunk, scan, 0, unroll=min(2, nchunk))

    ri = lax.broadcasted_iota(jnp.int32, (2 * C, 1), 0)
    ci = lax.broadcasted_iota(jnp.int32, (1, 2 * C), 1)
    same_head_causal = ((ri >= C) == (ci >= C)) & ((ri % C) >= (ci % C))

    def emit(c, carry):
        rows = chunk_rows(c)
        b = b_scr[rows, :]
        qt = q_ref[rows, :] * (GLA_DK ** -0.5) * jnp.exp(b)
        kt = k_ref[rows, :] * jnp.exp(-b)
        for p in range(2):
            ls = slice(p * LANES, (p + 1) * LANES)
            qm = by_head(qt[:, ls]).astype(BF16)
            kt2 = jnp.concatenate([kt[:, ls], kt[:, ls]], axis=0).astype(BF16)
            a = jnp.where(same_head_causal, _dot_nt(qm, kt2), 0.0)
            o = jnp.dot(a.astype(BF16), v_pair(rows, p), preferred_element_type=F32)
            o = o + jnp.dot(qm, sprev_scr[c, ls, :], preferred_element_type=F32)
            on = _rms(o) * gn
            for hh in range(2):
                hs = slice((2 * p + hh) * LANES, (2 * p + hh + 1) * LANES)
                r_h = r_ref[rows, hs]
                yc_ref[rows, hs] = (on[hh * C:(hh + 1) * C, :] * (r_h * jax.nn.sigmoid(r_h))).astype(BF16)
        return carry

    lax.fori_loop(0, nchunk, emit, 0, unroll=min(GLA_UNROLL, nchunk))
    sfin_ref[...] = s_scr[...]


def gla(P, B, T, wa_pad, b_a, gla_norm, s0, l, tt, C, c_true, slot=None):
    nt = T // tt
    slot_rows = slot.shape[0] if slot is not None else 0
    rows_b, in_blk, out_blk, seq_blk = _seq_tiles(B, nt, slot is not None)
    return pl.pallas_call(
        functools.partial(_gla_kernel, tt=tt, C=C, c_true=c_true, n_seq=B, slot_rows=slot_rows),
        grid=(rows_b, nt),
        in_specs=[
            pl.BlockSpec((tt, 256), lambda b, t: (in_blk(b, t), COL_GQ // 256)),
            pl.BlockSpec((tt, 256), lambda b, t: (in_blk(b, t), COL_GK // 256)),
            pl.BlockSpec((tt, 512), lambda b, t: (in_blk(b, t), COL_GV // 512)),
            pl.BlockSpec((tt, 512), lambda b, t: (in_blk(b, t), COL_GR // 512)),
            pl.BlockSpec((tt, LANES), lambda b, t: (in_blk(b, t), COL_LR // LANES)),
            pl.BlockSpec((None, LANES, 256), lambda b, t: (l, 0, 0)),
            pl.BlockSpec((None, 1, 256), lambda b, t: (l, 0, 0)),
            pl.BlockSpec((None, 1, LANES), lambda b, t: (l, 0, 0)),
            pl.BlockSpec((None, 256, LANES), lambda b, t: (seq_blk(b), 0, 0)),
        ] + ([pl.BlockSpec((slot_rows, 512), lambda b, t: (0, 0))] if slot is not None else []),
        out_specs=[
            pl.BlockSpec((tt, 512), lambda b, t: (out_blk(b, t), 0)),
            pl.BlockSpec((None, 256, LANES), lambda b, t: (seq_blk(b), 0, 0)),
        ],
        out_shape=[jax.ShapeDtypeStruct((B * T + slot_rows, 512), BF16),
                   jax.ShapeDtypeStruct((B, 256, LANES), F32)],
        scratch_shapes=[pltpu.VMEM((256, LANES), F32), pltpu.VMEM((tt, 256), F32),
                        pltpu.VMEM((tt // C, 256, LANES), BF16)],
        compiler_params=_cparams(("arbitrary", "arbitrary")),
    )(P, P, P, P, P, wa_pad, b_a, gla_norm, s0, *(() if slot is None else (slot,)))


def _head_norm(x, gain, head0):
    x2 = x * x
    s0 = jnp.sum(jnp.where(head0, x2, 0.0), axis=-1, keepdims=True)
    s1 = jnp.sum(jnp.where(head0, 0.0, x2), axis=-1, keepdims=True)
    ms = jnp.where(head0, s0, s1) / float(HEAD_DIM)
    return x * lax.rsqrt(ms + EPS) * gain


NORM_ROWS = 512
ATTN_UNROLL = 8
QK_SCALE = 0.125


def _attn_prompt_kernel(q_ref, k_ref, v_ref, qg_ref, kg_ref, sl_ref, *rest, S, dil, n_keep, head_base, chained):
    if chained:
        rest = rest[1:]
    o_ref, lse_ref, kv_ref, tmp, qd, kd, vd, od, ld, bias = rest
    msub = S // dil
    nb = msub // NK
    kstride = msub + NK
    lane = lax.broadcasted_iota(jnp.int32, (1, LANES), 1)
    head0 = lane < HEAD_DIM

    def norm_into_tmp(src_ref, g_ref):
        def body(i, carry):
            rows = pl.ds(pl.multiple_of(i * NORM_ROWS, NORM_ROWS), NORM_ROWS)
            tmp[rows, :] = _head_norm(src_ref[rows, :], g_ref[...], head0)
            return carry
        lax.fori_loop(0, S // NORM_ROWS, body, 0)

    norm_into_tmp(q_ref, qg_ref)
    for r in range(dil):
        qd[r * msub:(r + 1) * msub, :] = (tmp[pl.ds(r, msub, stride=dil), :] * QK_SCALE).astype(BF16)
    norm_into_tmp(k_ref, kg_ref)
    kv_ref[0] = tmp[S - n_keep:S, :].T
    kv_ref[1] = v_ref[S - n_keep:S, :].T
    zeros = jnp.zeros((NK, LANES), BF16)
    for r in range(dil):
        kd[r * kstride:r * kstride + NK, :] = zeros
        vd[r * kstride:r * kstride + NK, :] = zeros
        kd[r * kstride + NK:(r + 1) * kstride, :] = tmp[pl.ds(r, msub, stride=dil), :].astype(BF16)
        vd[r * kstride + NK:(r + 1) * kstride, :] = v_ref[pl.ds(r, msub, stride=dil), :].astype(BF16)

    qi = lax.broadcasted_iota(jnp.int32, (NK, 2 * NK), 0)
    ki = lax.broadcasted_iota(jnp.int32, (NK, 2 * NK), 1)
    dist = qi - ki + NK
    band = (dist >= 0) & (dist <= NK)
    alibi = (dist * dil).astype(F32)
    cur = ki >= NK
    for hh in range(2):
        slope = sl_ref[head_base + 2 * pl.program_id(1) + hh]
        bias[2 * hh] = jnp.where(band, -(slope * alibi), NEG)
        bias[2 * hh + 1] = jnp.where(band & cur, -(slope * alibi), NEG)

    def block(j, carry):
        r = j // nb
        n = j - r * nb
        first = jnp.where(n == 0, 1, 0)
        qb = pl.multiple_of(j * NK, NK)
        kb = pl.multiple_of(j * NK + r * NK, NK)
        q2 = qd[pl.ds(qb, NK), :]
        k2 = kd[pl.ds(kb, 2 * NK), :]
        v2 = vd[pl.ds(kb, 2 * NK), :]
        outs, lses = [], []
        for hh in range(2):
            hmask = head0 if hh == 0 else jnp.logical_not(head0)
            qm = jnp.where(hmask, q2, jnp.zeros_like(q2))
            s = lax.dot_general(qm, k2, (((1,), (1,)), ((), ())), preferred_element_type=F32)
            s = s + bias[2 * hh + first]
            m = jnp.max(s, axis=-1, keepdims=True)
            p = jnp.exp(s - m)
            lsum = jnp.sum(p, axis=-1, keepdims=True)
            outs.append(jnp.dot(p.astype(BF16), v2, preferred_element_type=F32) / lsum)
            lses.append(m + jnp.log(lsum))
        od[pl.ds(qb, NK), :] = jnp.where(head0, outs[0], outs[1])
        ld[pl.ds(qb, NK), :] = jnp.where(head0, lses[0], lses[1])
        return carry

    lax.fori_loop(0, dil * nb, block, 0, unroll=ATTN_UNROLL)

    for r in range(dil):
        o_ref[pl.ds(r, msub, stride=dil), :] = od[r * msub:(r + 1) * msub, :]
        lse_ref[pl.ds(r, msub, stride=dil), :] = ld[r * msub:(r + 1) * msub, :]


def attn_prompt(P, B, S, q_gain, k_gain, slopes, l, g, win_prev):
    window, dil = ATTN_GROUPS[g]
    n_keep = min(window, S)
    chained = win_prev is not None

    def col(c):
        return pl.BlockSpec((S, LANES), lambda b, p: (b, c // LANES + 2 * g + p))

    def par(stack):
        return pl.BlockSpec((None, 1, LANES), lambda b, p: (stack * 6 + 2 * g + p, 0, 0))

    in_specs = [col(COL_AQ), col(COL_AK), col(COL_AV), par(l), par(l), pl.BlockSpec(memory_space=pltpu.SMEM)]
    args = [P, P, P, q_gain, k_gain, slopes]
    if chained:
        in_specs.append(pl.BlockSpec(memory_space=pl.ANY))
        args.append(win_prev)
    return pl.pallas_call(
        functools.partial(_attn_prompt_kernel, S=S, dil=dil, n_keep=n_keep, head_base=4 * g, chained=chained),
        grid=(B, 2),
        in_specs=in_specs,
        out_specs=[
            pl.BlockSpec((None, S, LANES), lambda b, p: (b, 0, p)),
            pl.BlockSpec((None, S, LANES), lambda b, p: (b, 0, p)),
            pl.BlockSpec((None, 2, LANES, n_keep), lambda b, p: (l * B + b, 0, p, 0)),
        ],
        out_shape=[
            jax.ShapeDtypeStruct((B, S, 256), F32),
            jax.ShapeDtypeStruct((B, S, 256), F32),
            jax.ShapeDtypeStruct((DEPTH * B, 2, 256, n_keep), F32),
        ],
        input_output_aliases={6: 2} if chained else {},
        scratch_shapes=[
            pltpu.VMEM((S, LANES), F32),
            pltpu.VMEM((S, LANES), BF16),
            pltpu.VMEM((S + dil * NK, LANES), BF16),
            pltpu.VMEM((S + dil * NK, LANES), BF16),
            pltpu.VMEM((S, LANES), F32),
            pltpu.VMEM((S, LANES), F32),
            pltpu.VMEM((4, NK, 2 * NK), F32),
        ],
        compiler_params=_cparams(("parallel", "arbitrary")),
    )(*args)


T_PAD = 8


def _attn_sample_kernel(q_ref, k_ref, v_ref, qg_ref, kg_ref, sl_ref, cache_ref, *rest,
                        L, dil, t_true, head_base, chained):
    if chained:
        rest = rest[1:]
    o_ref, lse_ref, cnew_ref = rest
    lane = lax.broadcasted_iota(jnp.int32, (1, LANES), 1)
    head0 = lane < HEAD_DIM
    trow8 = lax.broadcasted_iota(jnp.int32, (T_PAD, LANES), 0)
    lane8 = lax.broadcasted_iota(jnp.int32, (T_PAD, LANES), 1)
    place = ((lane8 == LANES - t_true + trow8) & (trow8 < t_true)).astype(F32)
    is_new = lane >= LANES - t_true

    def rolled_with_new(rows, new):
        shifted = pltpu.roll(cache_ref[rows, :], L - t_true, 1)
        tail = lax.dot_general(new, place, (((0,), (0,)), ((), ())), preferred_element_type=F32,
                               precision=lax.Precision.HIGHEST)
        if L > LANES:
            cnew_ref[rows, 0:L - LANES] = shifted[:, 0:L - LANES]
        cnew_ref[rows, L - LANES:L] = jnp.where(is_new, tail, shifted[:, L - LANES:L])

    trow = lax.broadcasted_iota(jnp.int32, (T_PAD, 1), 0)
    scol = lax.broadcasted_iota(jnp.int32, (1, L), 1)
    dist = L + trow - scol
    valid_c = ((dist & (dil - 1)) == 0) & (dist <= NK * dil)
    alibi_c = dist.astype(F32)

    for p in range(2):
        ls = slice(p * LANES, (p + 1) * LANES)
        qn = _head_norm(q_ref[:, ls], qg_ref[:, ls], head0)
        kn = _head_norm(k_ref[:, ls], kg_ref[:, ls], head0)
        vn = v_ref[:, ls]
        krows = slice(p * LANES, (p + 1) * LANES)
        vrows = slice(256 + p * LANES, 256 + (p + 1) * LANES)
        rolled_with_new(krows, kn)
        rolled_with_new(vrows, vn)
        kc_t = cache_ref[krows, :].astype(BF16)
        vc_t = cache_ref[vrows, :].astype(BF16)
        knr = kn.astype(BF16).astype(F32)
        vnr = vn.astype(BF16).astype(F32)
        outs, lses = [], []
        for hh in range(2):
            hmask = head0 if hh == 0 else jnp.logical_not(head0)
            qm = jnp.where(hmask, qn, 0.0).astype(BF16)
            qmr = qm.astype(F32)
            slope = sl_ref[head_base + 2 * p + hh]
            sc = jnp.dot(qm, kc_t, preferred_element_type=F32)
            sc = sc / float(np.sqrt(HEAD_DIM)) - slope * alibi_c
            sc = jnp.where(valid_c, sc, NEG)
            m = jnp.max(sc, axis=-1, keepdims=True)
            sn = []
            for u in range(t_true):
                du = trow - u
                su = jnp.sum(qmr * knr[u:u + 1, :], axis=-1, keepdims=True) / float(np.sqrt(HEAD_DIM))
                su = su - slope * du.astype(F32)
                su = jnp.where((du >= 0) & ((du & (dil - 1)) == 0), su, NEG)
                sn.append(su)
                m = jnp.maximum(m, su)
            pc = jnp.exp(sc - m)
            lsum = jnp.sum(pc, axis=-1, keepdims=True)
            acc = _dot_nt(pc.astype(BF16), vc_t)
            for u in range(t_true):
                pu = jnp.exp(sn[u] - m)
                lsum = lsum + pu
                acc = acc + pu.astype(BF16).astype(F32) * vnr[u:u + 1, :]
            outs.append(acc / lsum)
            lses.append(m + jnp.log(lsum))
        o_ref[:, ls] = jnp.where(head0, outs[0], outs[1])
        lse_ref[:, ls] = jnp.where(head0, lses[0], lses[1])


def attn_sample(P, B, q_gain, k_gain, slopes, cache_t, l, g, t_true, win_prev):
    window, dil = ATTN_GROUPS[g]
    L = cache_t.shape[2]
    chained = win_prev is not None

    def col(c):
        return pl.BlockSpec((T_PAD, 256), lambda b: (b, c // 256 + g))

    def par(stack):
        return pl.BlockSpec((None, 1, 256), lambda b: (stack * 3 + g, 0, 0))

    in_specs = [col(COL_AQ), col(COL_AK), col(COL_AV), par(l), par(l), pl.BlockSpec(memory_space=pltpu.SMEM),
                pl.BlockSpec((None, 512, L), lambda b: (l * B + b, 0, 0))]
    args = [P, P, P, q_gain, k_gain, slopes, cache_t]
    if chained:
        in_specs.append(pl.BlockSpec(memory_space=pl.ANY))
        args.append(win_prev)
    return pl.pallas_call(
        functools.partial(_attn_sample_kernel, L=L, dil=dil, t_true=t_true, head_base=4 * g, chained=chained),
        grid=(B,),
        in_specs=in_specs,
        out_specs=[
            pl.BlockSpec((None, T_PAD, 256), lambda b: (b, 0, 0)),
            pl.BlockSpec((None, T_PAD, 256), lambda b: (b, 0, 0)),
            pl.BlockSpec((None, 512, L), lambda b: (l * B + b, 0, 0)),
        ],
        out_shape=[
            jax.ShapeDtypeStruct((B, T_PAD, 256), F32),
            jax.ShapeDtypeStruct((B, T_PAD, 256), F32),
            jax.ShapeDtypeStruct((DEPTH * B, 512, L), F32),
        ],
        input_output_aliases={7: 2} if chained else {},
        compiler_params=_cparams(("parallel",)),
    )(*args)


def _attn_merge_kernel(o0, l0, o1, l1, o2, l2, *rest, n_tiles, slot_rows):
    def merge_tile(y_ref):
        a0, a1, a2 = l0[...], l1[...], l2[...]
        m = jnp.maximum(jnp.maximum(a0, a1), a2)
        e0, e1, e2 = jnp.exp(a0 - m), jnp.exp(a1 - m), jnp.exp(a2 - m)
        den = e0 + e1 + e2
        y = (e0 / den) * o0[...] + (e1 / den) * o1[...] + (e2 / den) * o2[...]
        y_ref[...] = y.astype(BF16)

    if slot_rows:
        slot_ref, y_ref = rest
        pl.when(pl.program_id(0) < n_tiles)(lambda: merge_tile(y_ref))

        @pl.when(pl.program_id(0) == n_tiles)
        def _():
            y_ref[0:slot_rows, :] = slot_ref[...]
    else:
        merge_tile(*rest)


def attn_merge(parts, tm, slot=None):
    M = parts[0].shape[0]
    n_tiles = M // tm
    slot_rows = slot.shape[0] if slot is not None else 0
    spec = pl.BlockSpec((tm, 256), lambda m: (jnp.minimum(m, n_tiles - 1), 0))
    return pl.pallas_call(
        functools.partial(_attn_merge_kernel, n_tiles=n_tiles, slot_rows=slot_rows),
        grid=(n_tiles + (1 if slot_rows else 0),),
        in_specs=[spec] * 6 + ([pl.BlockSpec((slot_rows, 256), lambda m: (0, 0))] if slot_rows else []),
        out_specs=pl.BlockSpec((tm, 256), lambda m: (m, 0)),
        out_shape=jax.ShapeDtypeStruct((M + slot_rows, 256), BF16),
        compiler_params=_cparams(("arbitrary",)),
    )(*parts, *(() if slot is None else (slot,)))


GATE_SRC0 = 5904
assert GATE_SRC0 % SRC_ALIGN == 0 and D_MODEL % SRC_ALIGN == 0


def _merge_kernel(xn_ref, ya_ref, yb_ref, yc_ref, yd_ref, g0, g1, g2, g3, u0, u1, u2, u3, o_ref,
                  gs0, gs1, gs2, gs3, us0, us1, us2, us3, *, tiles, tm):
    branches = ((ya_ref, g0, u0, gs0, us0), (yb_ref, g1, u1, gs1, us1),
                (yc_ref, g2, u2, gs2, us2), (yd_ref, g3, u3, gs3, us3))
    m = pl.program_id(1)

    @pl.when(m == 0)
    def _():
        for _, g_ref, u_ref, g_s, u_s in branches:
            g_s[...] = g_ref[...].astype(BF16)
            u_s[...] = u_ref[...].astype(BF16)

    def rows_fn(rows):
        xn = xn_ref[rows, :]
        acc = None
        for y_ref, _, _, g_s, u_s in branches:
            gate = jax.nn.sigmoid(_dot_nt(xn, g_s[...]))
            term = gate * jnp.dot(y_ref[rows, :], u_s[...], preferred_element_type=F32)
            acc = term if acc is None else acc + term
        o_ref[rows, :] = acc.astype(BF16)

    _on_tile_rows(m, tiles, tm, rows_fn)


def merge(xn, ya, yb, yc, yd, w_in_t, ups, l, tm):
    M = xn.shape[0]
    tiles = _row_tiles(M, tm)
    tn = PREP_ROWS
    widths = (512, 256, 512, 512)

    def act(width):
        return pl.BlockSpec((tm, width), lambda n, m: (m, 0))

    def gate(b):
        base = (GATE_SRC0 + b * D_MODEL) // SRC_ALIGN
        return _w_in_rows(l, lambda n, m: base + n * (tn // SRC_ALIGN))

    def up(width):
        return pl.BlockSpec((None, width, tn), lambda n, m: (l, 0, n))

    return pl.pallas_call(
        functools.partial(_merge_kernel, tiles=tiles, tm=tm),
        grid=(D_MODEL // tn, tiles[0]),
        in_specs=[act(D_MODEL)] + [act(w) for w in widths] + [gate(b) for b in range(N_BRANCH)]
                 + [up(w) for w in widths],
        out_specs=pl.BlockSpec((tm, tn), lambda n, m: (m, n)),
        out_shape=jax.ShapeDtypeStruct((M, D_MODEL), BF16),
        scratch_shapes=[pltpu.VMEM((tn, D_MODEL), BF16)] * N_BRANCH + [pltpu.VMEM((w, tn), BF16) for w in widths],
        compiler_params=_cparams(("parallel", "arbitrary")),
    )(xn, ya, yb, yc, yd, w_in_t, w_in_t, w_in_t, w_in_t, *ups)


def _layer(x, xn, W, l, T, mixers):
    h = ffn_up(xn, W['ff1_gate'], W['ff1_up'], l, T['up_m'], T['up_n'])
    x, xn = matmul_res(x, h, W['ff1_down'], l, 0.5, T['down_m'], W['n_mix'], l)
    P = mix_in(xn, W['w_in_t'], l, T['mix_m'])
    (ya, yb, yc, yd), states = mixers(P, l)
    mg = merge(xn, ya, yb, yc, yd, W['w_in_t'], (W['up_a'], W['up_b'], W['up_c'], W['up_d']), l, T['merge_m'])
    x, xn = matmul_res(x, mg, W['w_out'], l, 1.0, T['out_m'], W['n_ff2'], l)
    h = ffn_up(xn, W['ff2_gate'], W['ff2_up'], l, T['up_m'], T['up_n'])
    if l + 1 < DEPTH:
        x, xn = matmul_res(x, h, W['ff2_down'], l, 0.5, T['down_m'], W['n_ff1'], l + 1)
    else:
        x, xn = matmul_res(x, h, W['ff2_down'], l, 0.5, T['down_m'], out_split=True)
    return x, xn, states


SAMPLE_SLOT = 128
TILES = dict(up_m=2080, up_n=512, mix_m=1040, down_m=256, out_m=512, merge_m=1040)


def kernel(x_prompt, x_sample, state_conv, cache_w128_kv, cache_w512_kv, cache_w2048_kv, state_gla, state_pool,
           norm_ff1, ff1_gate, ff1_up, ff1_down, norm_mix, w_in, conv_w, attn_q_gain, attn_k_gain,
           gla_w_a2, gla_b_a, gla_norm, pool_w, pool_scale, w_up_conv, w_up_attn, w_up_gla, w_up_pool, w_out,
           norm_ff2, ff2_gate, ff2_up, ff2_down):
    BP, S, _ = x_prompt.shape
    BS, TS, _ = x_sample.shape
    caches = (cache_w128_kv, cache_w512_kv, cache_w2048_kv)

    W = {
        'n_ff1': norm_ff1.reshape(DEPTH, 1, D_MODEL), 'n_mix': norm_mix.reshape(DEPTH, 1, D_MODEL),
        'n_ff2': norm_ff2.reshape(DEPTH, 1, D_MODEL),
        'ff1_gate': ff1_gate, 'ff1_up': ff1_up, 'ff1_down': ff1_down.astype(BF16),
        'ff2_gate': ff2_gate, 'ff2_up': ff2_up, 'ff2_down': ff2_down.astype(BF16),
        'w_in_t': jnp.transpose(w_in, (0, 2, 1)),
        'up_a': w_up_conv, 'up_b': w_up_attn, 'up_c': w_up_gla, 'up_d': w_up_pool, 'w_out': w_out.astype(BF16),
    }
    wa_pad = jnp.pad(gla_w_a2, ((0, 0), (0, LANES - gla_w_a2.shape[1]), (0, 0))).astype(BF16)
    b_a = gla_b_a.reshape(DEPTH, 1, 256)
    gn = gla_norm.reshape(DEPTH, 1, LANES)
    pool_wb = pool_w.astype(BF16)
    pool_sc = pool_scale.reshape(DEPTH, 1, W_POOL)
    i = jnp.arange(1, N_ATTN_HEADS + 1, dtype=F32)
    slopes = jnp.exp2(-8.0 * i / N_ATTN_HEADS)
    qg6 = attn_q_gain.reshape(DEPTH * 6, 1, LANES)
    kg6 = attn_k_gain.reshape(DEPTH * 6, 1, LANES)
    qg3 = attn_q_gain.reshape(DEPTH * 3, 1, 256)
    kg3 = attn_k_gain.reshape(DEPTH * 3, 1, 256)

    zc = jnp.zeros((BP, CONV_HIST, W_CONV), F32)
    zp = jnp.zeros((BP, POOL_HIST16, W_POOL), F32)
    zs = jnp.zeros((BP, 256, LANES), F32)

    win_p = [jnp.zeros((DEPTH * BP, 2, 256, min(w, S)), F32) for w, _ in ATTN_GROUPS]
    win_s = [jnp.zeros((DEPTH * BS, 512, cc.shape[2]), F32) for cc in caches]

    MP, MS = BP * S, BS * TS
    M_ALL = MP + SAMPLE_SLOT
    assert MS <= SAMPLE_SLOT

    def prompt_mixers(P, l, slot):
        ya, yd, cnew, pnew = conv_pool(P, BP, S, zc, zp, conv_w, pool_wb, pool_sc, l, 512, 512, 0,
                                       slot=(slot[0], slot[3]))
        yc, sfin = gla(P, BP, S, wa_pad, b_a, gn, zs, l, 512, GLA_CHUNK, GLA_CHUNK, slot=slot[2])
        parts = []
        for g in range(3):
            o, lse, win_p[g] = attn_prompt(P, BP, S, qg6, kg6, slopes, l, g, win_p[g])
            parts += [o.reshape(MP, 256), lse.reshape(MP, 256)]
        yb = attn_merge(parts, 1024, slot=slot[1])
        return (ya, yb, yc, yd), (cnew, sfin.reshape(BP, 4, GLA_DK, LANES), pnew[:, 1:])

    sconv8 = jnp.pad(state_conv, ((0, 0), (0, 0), (CONV_HIST - 2, 0), (0, 0)))
    spool16 = jnp.pad(state_pool, ((0, 0), (0, 0), (1, 0), (0, 0)))
    sgla = state_gla.reshape(DEPTH, BS, 256, LANES)
    cviews = [jnp.transpose(cc, (0, 1, 3, 4, 5, 2)).reshape(DEPTH * BS, 512, cc.shape[2]) for cc in caches]

    def sample_mixers(P, l):
        MSP = BS * T_PAD
        Ps = jnp.pad(P[MP:MP + MS].reshape(BS, TS, N_MIX), ((0, 0), (0, T_PAD - TS), (0, 0))).reshape(MSP, N_MIX)
        ya, yd, cnew, pnew = conv_pool(Ps, BS, T_PAD, sconv8[l], spool16[l], conv_w, pool_wb, pool_sc, l,
                                       T_PAD, TS, PAST_LEN)
        yc, sfin = gla(Ps, BS, T_PAD, wa_pad, b_a, gn, sgla[l], l, T_PAD, T_PAD, TS)
        parts = []
        for g in range(3):
            o, lse, win_s[g] = attn_sample(Ps, BS, qg3, kg3, slopes, cviews[g], l, g, TS, win_s[g])
            parts += [o[:, :TS].reshape(MS, 256), lse[:, :TS].reshape(MS, 256)]
        yb = attn_merge(parts, MS)

        def slot(y, padded=True):
            if padded:
                y = y.reshape(BS, T_PAD, -1)[:, :TS].reshape(MS, -1)
            return jnp.pad(y, ((0, SAMPLE_SLOT - MS), (0, 0)))

        ys = (slot(ya), slot(yb, False), slot(yc), slot(yd))
        return ys, (cnew, sfin.reshape(BS, 4, GLA_DK, LANES), pnew[:, 1:])

    def mixers(P, l):
        ys_s, st_sample = sample_mixers(P, l)
        ys, st_prompt = prompt_mixers(P, l, ys_s)
        return ys, (st_prompt, st_sample)

    x = (x_prompt.reshape(MP, D_MODEL), jnp.pad(x_sample.reshape(MS, D_MODEL), ((0, SAMPLE_SLOT - MS), (0, 0))))
    xn = norm(x[0], x[1], W['n_ff1'], 0, 1024)
    st_p, st_s = [], []
    for l in range(DEPTH):
        x, xn, (sp, ss) = _layer(x, xn, W, l, TILES, mixers)
        st_p.append(sp)
        st_s.append(ss)
    yp, ys = x[0], x[1][:MS]

    def stack(sts, f):
        return jnp.stack([f(s) for s in sts])

    def window_out(buf, B):
        n = buf.shape[-1]
        return jnp.transpose(buf.reshape(DEPTH, B, 2, 4, HEAD_DIM, n), (0, 1, 5, 2, 3, 4))

    outs = [yp.reshape(BP, S, D_MODEL), ys.reshape(BS, TS, D_MODEL),
            stack(st_p, lambda s: s[0]), stack(st_s, lambda s: s[0])]
    for g in range(3):
        outs += [window_out(win_p[g], BP), window_out(win_s[g], BS)]
    outs += [stack(st_p, lambda s: s[1]), stack(st_s, lambda s: s[1]),
             stack(st_p, lambda s: s[2]), stack(st_s, lambda s: s[2])]
    return tuple(outs)
```

```python
import functools

import numpy as np
import jax
import jax.numpy as jnp
from jax import lax
from jax.experimental import pallas as pl
from jax.experimental.pallas import tpu as pltpu

BF16 = jnp.bfloat16
F32 = jnp.float32

D_MODEL = 2048
DEPTH = 4
PAST_LEN = 16384
D_FF = 5504
EPS = 1e-6
N_BRANCH = 4
W_CONV = 512
ATTN_GROUPS = ((128, 1), (512, 4), (2048, 16))
N_ATTN_HEADS = 12
HEAD_DIM = 64
GLA_DK = 64
GLA_TAU = 16.0
GLA_CHUNK = 64
POOL_WINDOWS = (2, 4, 8, 16)
W_POOL = 512
NK = 128
LANES = 128
NEG = -1e30

COL_CB, COL_CC, COL_CH, COL_GV, COL_GR, COL_POOL = 0, 512, 1024, 1536, 2048, 2560
COL_AQ, COL_AK, COL_AV = 3072, 3840, 4608
COL_GQ, COL_GK, COL_LR = 5376, 5632, 5888
N_MIX = 6144
N_IN = 14096
N_GATE = N_BRANCH * D_MODEL
PREP_ROWS = 256
MIX_TILE_SRC = ([256 * j for j in range(6)] + [4352, 4608, 4864, 5120, 5392, 5648]
                + [1536 + 256 * j for j in range(9)] + [3840, 4096, 5376])
SRC_ALIGN = 16
assert all(r % SRC_ALIGN == 0 for r in MIX_TILE_SRC)

VMEM_LIMIT = 56 * 1024 * 1024


def _cparams(sem):
    return pltpu.CompilerParams(dimension_semantics=sem, vmem_limit_bytes=VMEM_LIMIT)


def _rms(x):
    return x * lax.rsqrt(jnp.mean(x * x, axis=-1, keepdims=True) + EPS)


def _row_tiles(M, tm):
    n = pl.cdiv(M, tm)
    return n, M - (n - 1) * tm


def _on_tile_rows(m, tiles, tm, fn):
    n, tail = tiles
    if tail == tm:
        fn(slice(0, tm))
        return
    pl.when(m < n - 1)(lambda: fn(slice(0, tm)))
    pl.when(m == n - 1)(lambda: fn(slice(0, tail)))


def _split_rows_specs(tm, n_tiles, slot):
    return (pl.BlockSpec((tm, D_MODEL), lambda m: (jnp.minimum(m, n_tiles - 2), 0)),
            pl.BlockSpec((slot, D_MODEL), lambda m: (0, 0)))


def _norm_kernel(xp_ref, xs_ref, g_ref, xn_ref, *, tiles, tm):
    n, tail = tiles
    m = pl.program_id(0)

    @pl.when(m < n - 1)
    def _():
        xn_ref[...] = (_rms(xp_ref[...]) * g_ref[...]).astype(BF16)

    @pl.when(m == n - 1)
    def _():
        xn_ref[0:tail, :] = (_rms(xs_ref[...]) * g_ref[...]).astype(BF16)


def norm(xp, xs, gain, l, tm):
    MP, slot = xp.shape[0], xs.shape[0]
    assert MP % tm == 0 and slot < tm
    tiles = (MP // tm + 1, slot)
    return pl.pallas_call(
        functools.partial(_norm_kernel, tiles=tiles, tm=tm),
        grid=(tiles[0],),
        in_specs=[*_split_rows_specs(tm, tiles[0], slot),
                  pl.BlockSpec((None, 1, D_MODEL), lambda m: (l, 0, 0))],
        out_specs=pl.BlockSpec((tm, D_MODEL), lambda m: (m, 0)),
        out_shape=jax.ShapeDtypeStruct((MP + slot, D_MODEL), BF16),
        compiler_params=_cparams(("parallel",)),
    )(xp, xs, gain)


SUB_ROWS = 208


def _sub_blocks(rows):
    n_rows = rows.stop - rows.start
    sub = SUB_ROWS if n_rows % SUB_ROWS == 0 else n_rows
    return [slice(r0, r0 + sub) for r0 in range(rows.start, rows.stop, sub)]


def _ffn_up_kernel(xn_ref, wg_ref, wu_ref, h_ref, wg_s, wu_s, *, tiles, tm):
    m = pl.program_id(1)

    @pl.when(m == 0)
    def _():
        wg_s[...] = wg_ref[...].astype(BF16)
        wu_s[...] = wu_ref[...].astype(BF16)

    def rows_fn(rows):
        for r in _sub_blocks(rows):
            xn = xn_ref[r, :]
            a = jnp.dot(xn, wg_s[...], preferred_element_type=F32)
            b = jnp.dot(xn, wu_s[...], preferred_element_type=F32)
            h_ref[r, :] = (a * jax.nn.sigmoid(a) * b).astype(BF16)

    _on_tile_rows(m, tiles, tm, rows_fn)


def ffn_up(xn, wg, wu, l, tm, tn):
    M = xn.shape[0]
    tiles = _row_tiles(M, tm)
    return pl.pallas_call(
        functools.partial(_ffn_up_kernel, tiles=tiles, tm=tm),
        grid=(pl.cdiv(D_FF, tn), tiles[0]),
        in_specs=[
            pl.BlockSpec((tm, D_MODEL), lambda n, m: (m, 0)),
            pl.BlockSpec((None, D_MODEL, tn), lambda n, m: (l, 0, n)),
            pl.BlockSpec((None, D_MODEL, tn), lambda n, m: (l, 0, n)),
        ],
        out_specs=pl.BlockSpec((tm, tn), lambda n, m: (m, n)),
        out_shape=jax.ShapeDtypeStruct((M, D_FF), BF16),
        scratch_shapes=[pltpu.VMEM((D_MODEL, tn), BF16), pltpu.VMEM((D_MODEL, tn), BF16)],
        compiler_params=_cparams(("parallel", "arbitrary")),
    )(xn, wg, wu)


W_CHUNKS = 8


def _matmul_res_kernel(*refs, scale, with_norm, tiles, tm, x_split, out_split, ck):
    refs = list(refs)
    x_refs = [refs.pop(0) for _ in range(2 if x_split else 1)]
    a_ref, w_ref = refs.pop(0), refs.pop(0)
    g_ref = refs.pop(0) if with_norm else None
    o_refs = [refs.pop(0) for _ in range(2 if out_split else 1)]
    xn_ref = refs.pop(0) if with_norm else None
    w_s = refs.pop(0)
    n, tail = tiles
    step = pl.program_id(0)
    m = step - W_CHUNKS

    @pl.when(step < W_CHUNKS)
    def _():
        w_s[pl.ds(pl.multiple_of(step * ck, 16), ck), :] = w_ref[...].astype(BF16)

    def rows_fn(rows, last):
        x_ref = x_refs[-1] if last else x_refs[0]
        o_ref = o_refs[-1] if last else o_refs[0]
        y = x_ref[rows, :] + scale * jnp.dot(a_ref[rows, :], w_s[...], preferred_element_type=F32)
        if with_norm:
            xn_ref[rows, :] = (_rms(y) * g_ref[...]).astype(BF16)
        o_ref[rows, :] = y

    pl.when((m >= 0) & (m < n - 1))(lambda: rows_fn(slice(0, tm), False))
    pl.when(m == n - 1)(lambda: rows_fn(slice(0, tail), True))


def matmul_res(x, a, w, l, scale, tm, next_gain=None, next_l=0, out_split=False):
    M, K = a.shape
    with_norm = next_gain is not None
    x_split = isinstance(x, tuple)
    tiles = _row_tiles(M, tm)
    assert tiles[1] == SAMPLE_SLOT and (M - SAMPLE_SLOT) % tm == 0 and K % (16 * W_CHUNKS) == 0
    ck = K // W_CHUNKS

    def tile(step):
        return jnp.maximum(step - W_CHUNKS, 0)

    row = pl.BlockSpec((tm, D_MODEL), lambda s: (tile(s), 0))
    split = (pl.BlockSpec((tm, D_MODEL), lambda s: (jnp.minimum(tile(s), tiles[0] - 2), 0)),
             pl.BlockSpec((SAMPLE_SLOT, D_MODEL), lambda s: (0, 0)))
    in_specs = (list(split) if x_split else [row]) + [
        pl.BlockSpec((tm, K), lambda s: (tile(s), 0)),
        pl.BlockSpec((None, ck, D_MODEL), lambda s: (l, jnp.minimum(s, W_CHUNKS - 1), 0))]
    args = (list(x) if x_split else [x]) + [a, w]
    if out_split:
        out_specs = list(split)
        out_shape = [jax.ShapeDtypeStruct((M - SAMPLE_SLOT, D_MODEL), F32),
                     jax.ShapeDtypeStruct((SAMPLE_SLOT, D_MODEL), F32)]
    else:
        out_specs, out_shape = [row], [jax.ShapeDtypeStruct((M, D_MODEL), F32)]
    if with_norm:
        in_specs.append(pl.BlockSpec((None, 1, D_MODEL), lambda s: (next_l, 0, 0)))
        args.append(next_gain)
        out_specs.append(row)
        out_shape.append(jax.ShapeDtypeStruct((M, D_MODEL), BF16))
    res = pl.pallas_call(
        functools.partial(_matmul_res_kernel, scale=scale, with_norm=with_norm, tiles=tiles, tm=tm,
                          x_split=x_split, out_split=out_split, ck=ck),
        grid=(W_CHUNKS + tiles[0],),
        in_specs=in_specs,
        out_specs=out_specs,
        out_shape=out_shape,
        scratch_shapes=[pltpu.VMEM((K, D_MODEL), BF16)],
        compiler_params=_cparams(("arbitrary",)),
    )(*args)
    y = tuple(res[:2]) if out_split else res[0]
    return y, (res[-1] if with_norm else None)


def _dot_nt(a, b):
    return lax.dot_general(a, b, (((1,), (1,)), ((), ())), preferred_element_type=F32)


def _w_in_rows(l, row_of):
    return pl.BlockSpec((None, pl.Element(PREP_ROWS), pl.Element(D_MODEL)),
                        lambda *idx: (l, row_of(*idx) * SRC_ALIGN, 0))


MIX_SLABS = 4


def _mix_in_kernel(src_ref, xn_ref, *rest, tiles, tm):
    del src_ref
    w_refs, (p_ref, w_s) = rest[:MIX_SLABS], rest[MIX_SLABS:]
    m = pl.program_id(1)

    @pl.when(m == 0)
    def _():
        for j, w_ref in enumerate(w_refs):
            w_s[j * PREP_ROWS:(j + 1) * PREP_ROWS, :] = w_ref[...].astype(BF16)

    def rows_fn(rows):
        p_ref[rows, :] = _dot_nt(xn_ref[rows, :], w_s[...])

    _on_tile_rows(m, tiles, tm, rows_fn)


def mix_in(xn, w_in_t, l, tm):
    M = xn.shape[0]
    tn = MIX_SLABS * PREP_ROWS
    tiles = _row_tiles(M, tm)
    src = jnp.asarray(MIX_TILE_SRC, jnp.int32) // SRC_ALIGN

    def slab(j):
        return _w_in_rows(l, lambda n, m, src: src[MIX_SLABS * n + j])

    return pl.pallas_call(
        functools.partial(_mix_in_kernel, tiles=tiles, tm=tm),
        grid_spec=pltpu.PrefetchScalarGridSpec(
            num_scalar_prefetch=1,
            grid=(N_MIX // tn, tiles[0]),
            in_specs=[pl.BlockSpec((tm, D_MODEL), lambda n, m, src: (m, 0))] + [slab(j) for j in range(MIX_SLABS)],
            out_specs=pl.BlockSpec((tm, tn), lambda n, m, src: (m, n)),
            scratch_shapes=[pltpu.VMEM((tn, D_MODEL), BF16)],
        ),
        out_shape=jax.ShapeDtypeStruct((M, N_MIX), F32),
        compiler_params=_cparams(("parallel", "arbitrary")),
    )(src, xn, *([w_in_t] * MIX_SLABS))


CONV_HIST = 8
POOL_HIST16 = 16


def _conv_pool_kernel(cb_ref, cc_ref, ch_ref, pin_ref, cbuf_ref, pbuf_ref, cw_ref, pw_ref, ps_ref, *rest,
                      tt, t_true, pos0, n_seq, slot_rows):
    if slot_rows:
        ya_slot_ref, yd_slot_ref, ya_ref, yd_ref = rest[:4]

        @pl.when((pl.program_id(0) == n_seq) & (pl.program_id(1) == 0))
        def _():
            ya_ref[0:slot_rows, :] = ya_slot_ref[...]
            yd_ref[0:slot_rows, :] = yd_slot_ref[...]

        pl.when(pl.program_id(0) < n_seq)(
            lambda: _conv_pool_body(cb_ref, cc_ref, ch_ref, pin_ref, cbuf_ref, pbuf_ref, cw_ref, pw_ref, ps_ref,
                                    *rest[2:], tt=tt, t_true=t_true, pos0=pos0))
    else:
        _conv_pool_body(cb_ref, cc_ref, ch_ref, pin_ref, cbuf_ref, pbuf_ref, cw_ref, pw_ref, ps_ref, *rest,
                        tt=tt, t_true=t_true, pos0=pos0)


def _conv_pool_body(cb_ref, cc_ref, ch_ref, pin_ref, cbuf_ref, pbuf_ref, cw_ref, pw_ref, ps_ref,
                    ya_ref, yd_ref, cnew_ref, pnew_ref, uext, pext, *, tt, t_true, pos0):
    t = pl.program_id(1)

    @pl.when(t == 0)
    def _():
        uext[0:CONV_HIST, :] = cbuf_ref[...]
        pext[0:POOL_HIST16, :] = pbuf_ref[...]

    u = cc_ref[...] * ch_ref[...]
    uext[CONV_HIST:CONV_HIST + tt, :] = u
    w = cw_ref[...]
    z = uext[CONV_HIST - 2:CONV_HIST - 2 + tt, :] * w[0:1, :]
    z = z + uext[CONV_HIST - 1:CONV_HIST - 1 + tt, :] * w[1:2, :]
    z = z + u * w[2:3, :]
    ya_ref[...] = (cb_ref[...] * z).astype(BF16)
    cnew_ref[...] = uext[CONV_HIST + t_true - 2:CONV_HIST + t_true, :]
    uext[0:CONV_HIST, :] = uext[tt:tt + CONV_HIST, :]

    pin = pin_ref[...]
    pext[POOL_HIST16:POOL_HIST16 + tt, :] = pin
    pos = (pos0 + t * tt + lax.broadcasted_iota(jnp.int32, (tt, 1), 0)).astype(F32)
    for g, win in enumerate(POOL_WINDOWS):
        ls = slice(g * LANES, (g + 1) * LANES)
        acc = pin[:, ls]
        for i in range(1, win):
            acc = acc + pext[POOL_HIST16 - i:POOL_HIST16 - i + tt, ls]
        cnt = jnp.minimum(float(win), pos + 1.0)
        d = acc / cnt - pin[:, ls]
        y = jnp.dot(d.astype(BF16), pw_ref[g], preferred_element_type=F32) * ps_ref[:, ls]
        yd_ref[:, ls] = y.astype(BF16)
    pnew_ref[...] = pext[t_true:t_true + POOL_HIST16, :]
    pext[0:POOL_HIST16, :] = pext[tt:tt + POOL_HIST16, :]


def _seq_tiles(B, nt, with_slot):
    if not with_slot:
        return B, (lambda b, t: b * nt + t), (lambda b, t: b * nt + t), (lambda b: b)
    return (B + 1,
            lambda b, t: jnp.where(b < B, b * nt + t, B * nt - 1),
            lambda b, t: jnp.where(b < B, b * nt + t, B * nt),
            lambda b: jnp.minimum(b, B - 1))


def conv_pool(P, B, T, cbuf8, pbuf16, conv_w, pool_w, pool_scale, l, tt, t_true, pos0, slot=None):
    nt = T // tt
    slot_rows = slot[0].shape[0] if slot else 0
    rows_b, in_blk, out_blk, seq_blk = _seq_tiles(B, nt, bool(slot))

    def col(c):
        return pl.BlockSpec((tt, 512), lambda b, t: (in_blk(b, t), c // 512))

    slot_spec = pl.BlockSpec((slot_rows, 512), lambda b, t: (0, 0))
    return pl.pallas_call(
        functools.partial(_conv_pool_kernel, tt=tt, t_true=t_true, pos0=pos0, n_seq=B, slot_rows=slot_rows),
        grid=(rows_b, nt),
        in_specs=[
            col(COL_CB), col(COL_CC), col(COL_CH), col(COL_POOL),
            pl.BlockSpec((None, CONV_HIST, 512), lambda b, t: (seq_blk(b), 0, 0)),
            pl.BlockSpec((None, POOL_HIST16, 512), lambda b, t: (seq_blk(b), 0, 0)),
            pl.BlockSpec((None, 3, 512), lambda b, t: (l, 0, 0)),
            pl.BlockSpec((None, 4, LANES, LANES), lambda b, t: (l, 0, 0, 0)),
            pl.BlockSpec((None, 1, 512), lambda b, t: (l, 0, 0)),
        ] + ([slot_spec, slot_spec] if slot else []),
        out_specs=[
            pl.BlockSpec((tt, 512), lambda b, t: (out_blk(b, t), 0)),
            pl.BlockSpec((tt, 512), lambda b, t: (out_blk(b, t), 0)),
            pl.BlockSpec((None, 2, 512), lambda b, t: (seq_blk(b), 0, 0)),
            pl.BlockSpec((None, POOL_HIST16, 512), lambda b, t: (seq_blk(b), 0, 0)),
        ],
        out_shape=[
            jax.ShapeDtypeStruct((B * T + slot_rows, 512), BF16),
            jax.ShapeDtypeStruct((B * T + slot_rows, 512), BF16),
            jax.ShapeDtypeStruct((B, 2, 512), F32),
            jax.ShapeDtypeStruct((B, POOL_HIST16, 512), F32),
        ],
        scratch_shapes=[pltpu.VMEM((CONV_HIST + tt, 512), F32), pltpu.VMEM((POOL_HIST16 + tt, 512), F32)],
        compiler_params=_cparams(("arbitrary", "arbitrary")),
    )(P, P, P, P, cbuf8, pbuf16, conv_w, pool_w, pool_scale, *(slot or ()))


GLA_UNROLL = 4

def _gla_kernel(q_ref, k_ref, v_ref, r_ref, lr_ref, wa_ref, ba_ref, gn_ref, s0_ref, *rest,
                tt, C, c_true, n_seq, slot_rows):
    if slot_rows:
        yc_slot_ref, yc_ref = rest[:2]

        @pl.when((pl.program_id(0) == n_seq) & (pl.program_id(1) == 0))
        def _():
            yc_ref[0:slot_rows, :] = yc_slot_ref[...]

        pl.when(pl.program_id(0) < n_seq)(
            lambda: _gla_body(q_ref, k_ref, v_ref, r_ref, lr_ref, wa_ref, ba_ref, gn_ref, s0_ref, *rest[1:],
                              tt=tt, C=C, c_true=c_true))
    else:
        _gla_body(q_ref, k_ref, v_ref, r_ref, lr_ref, wa_ref, ba_ref, gn_ref, s0_ref, *rest,
                  tt=tt, C=C, c_true=c_true)


def _gla_body(q_ref, k_ref, v_ref, r_ref, lr_ref, wa_ref, ba_ref, gn_ref, s0_ref,
              yc_ref, sfin_ref, s_scr, b_scr, sprev_scr, *, tt, C, c_true):
    assert c_true == C or tt == C
    t = pl.program_id(1)

    @pl.when(t == 0)
    def _():
        s_scr[...] = s0_ref[...]

    row = lax.broadcasted_iota(jnp.int32, (C, 1), 0)
    lane = lax.broadcasted_iota(jnp.int32, (1, LANES), 1)
    head0 = lane < GLA_DK
    srow = lax.broadcasted_iota(jnp.int32, (LANES, 1), 0)
    eye = srow == lane
    gn = gn_ref[...]
    nchunk = tt // C

    def chunk_rows(c):
        return pl.ds(pl.multiple_of(c * C, C), C)

    def v_head(rows, h):
        v_h = v_ref[rows, h * LANES:(h + 1) * LANES]
        if c_true < C:
            v_h = jnp.where(row < c_true, v_h, 0.0)
        return v_h

    la = jnp.dot(lr_ref[...].astype(BF16), wa_ref[...], preferred_element_type=F32) + ba_ref[...]
    la = jax.nn.log_sigmoid(la) / GLA_TAU
    if c_true < C:
        la = jnp.where(row < c_true, la, 0.0)
    cb = 2 * C if nchunk % 2 == 0 else C
    bi = lax.broadcasted_iota(jnp.int32, (cb, 1), 0)
    bj = lax.broadcasted_iota(jnp.int32, (1, cb), 1)
    ones_tri = (((bi >= C) == (bj >= C)) & (bj <= bi)).astype(BF16)
    for blk in range(tt // cb):
        part = la[blk * cb:(blk + 1) * cb, :]
        acc = None
        for _ in range(3):
            term = part.astype(BF16)
            d = jnp.dot(ones_tri, term, preferred_element_type=F32)
            acc = d if acc is None else acc + d
            part = part - term.astype(F32)
        b_scr[blk * cb:(blk + 1) * cb, :] = acc

    def by_head(x):
        return jnp.concatenate([jnp.where(head0, x, 0.0), jnp.where(head0, 0.0, x)], axis=0)

    def v_pair(rows, p):
        return jnp.concatenate([v_head(rows, 2 * p), v_head(rows, 2 * p + 1)], axis=0).astype(BF16)

    def scan(c, carry):
        rows = chunk_rows(c)
        b = b_scr[rows, :]
        b_last = b[C - 1:C, :]
        kend = k_ref[rows, :] * jnp.exp(b_last - b)
        for p in range(2):
            ls = slice(p * LANES, (p + 1) * LANES)
            s_p = s_scr[ls, :]
            sprev_scr[c, ls, :] = s_p.astype(BF16)
            dec = jnp.exp(jnp.sum(jnp.where(eye, b_last[:, ls], 0.0), axis=1, keepdims=True))
            ds = lax.dot_general(by_head(kend[:, ls]).astype(BF16), v_pair(rows, p), (((0,), (0,)), ((), ())),
                                 preferred_element_type=F32)
            s_scr[ls, :] = dec * s_p + ds
        return carry

    lax.fori_loop(0, nchunk, scan, 0, unroll=min(2, nchunk))

    ri = lax.broadcasted_iota(jnp.int32, (2 * C, 1), 0)
    ci = lax.broadcasted_iota(jnp.int32, (1, 2 * C), 1)
    same_head_causal = ((ri >= C) == (ci >= C)) & ((ri % C) >= (ci % C))

    def emit(c, carry):
        rows = chunk_rows(c)
        b = b_scr[rows, :]
        qt = q_ref[rows, :] * (GLA_DK ** -0.5) * jnp.exp(b)
        kt = k_ref[rows, :] * jnp.exp(-b)
        for p in range(2):
            ls = slice(p * LANES, (p + 1) * LANES)
            qm = by_head(qt[:, ls]).astype(BF16)
            kt2 = jnp.concatenate([kt[:, ls], kt[:, ls]], axis=0).astype(BF16)
            a = jnp.where(same_head_causal, _dot_nt(qm, kt2), 0.0)
            o = jnp.dot(a.astype(BF16), v_pair(rows, p), preferred_element_type=F32)
            o = o + jnp.dot(qm, sprev_scr[c, ls, :], preferred_element_type=F32)
            on = _rms(o) * gn
            for hh in range(2):
                hs = slice((2 * p + hh) * LANES, (2 * p + hh + 1) * LANES)
                r_h = r_ref[rows, hs]
                yc_ref[rows, hs] = (on[hh * C:(hh + 1) * C, :] * (r_h * jax.nn.sigmoid(r_h))).astype(BF16)
        return carry

    lax.fori_loop(0, nchunk, emit, 0, unroll=min(GLA_UNROLL, nchunk))
    sfin_ref[...] = s_scr[...]


def gla(P, B, T, wa_pad, b_a, gla_norm, s0, l, tt, C, c_true, slot=None):
    nt = T // tt
    slot_rows = slot.shape[0] if slot is not None else 0
    rows_b, in_blk, out_blk, seq_blk = _seq_tiles(B, nt, slot is not None)
    return pl.pallas_call(
        functools.partial(_gla_kernel, tt=tt, C=C, c_true=c_true, n_seq=B, slot_rows=slot_rows),
        grid=(rows_b, nt),
        in_specs=[
            pl.BlockSpec((tt, 256), lambda b, t: (in_blk(b, t), COL_GQ // 256)),
            pl.BlockSpec((tt, 256), lambda b, t: (in_blk(b, t), COL_GK // 256)),
            pl.BlockSpec((tt, 512), lambda b, t: (in_blk(b, t), COL_GV // 512)),
            pl.BlockSpec((tt, 512), lambda b, t: (in_blk(b, t), COL_GR // 512)),
            pl.BlockSpec((tt, LANES), lambda b, t: (in_blk(b, t), COL_LR // LANES)),
            pl.BlockSpec((None, LANES, 256), lambda b, t: (l, 0, 0)),
            pl.BlockSpec((None, 1, 256), lambda b, t: (l, 0, 0)),
            pl.BlockSpec((None, 1, LANES), lambda b, t: (l, 0, 0)),
            pl.BlockSpec((None, 256, LANES), lambda b, t: (seq_blk(b), 0, 0)),
        ] + ([pl.BlockSpec((slot_rows, 512), lambda b, t: (0, 0))] if slot is not None else []),
        out_specs=[
            pl.BlockSpec((tt, 512), lambda b, t: (out_blk(b, t), 0)),
            pl.BlockSpec((None, 256, LANES), lambda b, t: (seq_blk(b), 0, 0)),
        ],
        out_shape=[jax.ShapeDtypeStruct((B * T + slot_rows, 512), BF16),
                   jax.ShapeDtypeStruct((B, 256, LANES), F32)],
        scratch_shapes=[pltpu.VMEM((256, LANES), F32), pltpu.VMEM((tt, 256), F32),
                        pltpu.VMEM((tt // C, 256, LANES), BF16)],
        compiler_params=_cparams(("arbitrary", "arbitrary")),
    )(P, P, P, P, P, wa_pad, b_a, gla_norm, s0, *(() if slot is None else (slot,)))


def _head_norm(x, gain, head0):
    x2 = x * x
    s0 = jnp.sum(jnp.where(head0, x2, 0.0), axis=-1, keepdims=True)
    s1 = jnp.sum(jnp.where(head0, 0.0, x2), axis=-1, keepdims=True)
    ms = jnp.where(head0, s0, s1) / float(HEAD_DIM)
    return x * lax.rsqrt(ms + EPS) * gain


NORM_ROWS = 512
ATTN_UNROLL = 8
QK_SCALE = 0.125


def _attn_prompt_kernel(q_ref, k_ref, v_ref, qg_ref, kg_ref, sl_ref, *rest, S, dil, n_keep, head_base, chained):
    if chained:
        rest = rest[1:]
    o_ref, lse_ref, kv_ref, tmp, qd, kd, vd, od, ld, bias = rest
    msub = S // dil
    nb = msub // NK
    kstride = msub + NK
    lane = lax.broadcasted_iota(jnp.int32, (1, LANES), 1)
    head0 = lane < HEAD_DIM

    def norm_into_tmp(src_ref, g_ref):
        def body(i, carry):
            rows = pl.ds(pl.multiple_of(i * NORM_ROWS, NORM_ROWS), NORM_ROWS)
            tmp[rows, :] = _head_norm(src_ref[rows, :], g_ref[...], head0)
            return carry
        lax.fori_loop(0, S // NORM_ROWS, body, 0)

    norm_into_tmp(q_ref, qg_ref)
    for r in range(dil):
        qd[r * msub:(r + 1) * msub, :] = (tmp[pl.ds(r, msub, stride=dil), :] * QK_SCALE).astype(BF16)
    norm_into_tmp(k_ref, kg_ref)
    kv_ref[0] = tmp[S - n_keep:S, :].T
    kv_ref[1] = v_ref[S - n_keep:S, :].T
    zeros = jnp.zeros((NK, LANES), BF16)
    for r in range(dil):
        kd[r * kstride:r * kstride + NK, :] = zeros
        vd[r * kstride:r * kstride + NK, :] = zeros
        kd[r * kstride + NK:(r + 1) * kstride, :] = tmp[pl.ds(r, msub, stride=dil), :].astype(BF16)
        vd[r * kstride + NK:(r + 1) * kstride, :] = v_ref[pl.ds(r, msub, stride=dil), :].astype(BF16)

    qi = lax.broadcasted_iota(jnp.int32, (NK, 2 * NK), 0)
    ki = lax.broadcasted_iota(jnp.int32, (NK, 2 * NK), 1)
    dist = qi - ki + NK
    band = (dist >= 0) & (dist <= NK)
    alibi = (dist * dil).astype(F32)
    cur = ki >= NK
    for hh in range(2):
        slope = sl_ref[head_base + 2 * pl.program_id(1) + hh]
        bias[2 * hh] = jnp.where(band, -(slope * alibi), NEG)
        bias[2 * hh + 1] = jnp.where(band & cur, -(slope * alibi), NEG)

    def block(j, carry):
        r = j // nb
        n = j - r * nb
        first = jnp.where(n == 0, 1, 0)
        qb = pl.multiple_of(j * NK, NK)
        kb = pl.multiple_of(j * NK + r * NK, NK)
        q2 = qd[pl.ds(qb, NK), :]
        k2 = kd[pl.ds(kb, 2 * NK), :]
        v2 = vd[pl.ds(kb, 2 * NK), :]
        outs, lses = [], []
        for hh in range(2):
            hmask = head0 if hh == 0 else jnp.logical_not(head0)
            qm = jnp.where(hmask, q2, jnp.zeros_like(q2))
            s = lax.dot_general(qm, k2, (((1,), (1,)), ((), ())), preferred_element_type=F32)
            s = s + bias[2 * hh + first]
            m = jnp.max(s, axis=-1, keepdims=True)
            p = jnp.exp(s - m)
            lsum = jnp.sum(p, axis=-1, keepdims=True)
            outs.append(jnp.dot(p.astype(BF16), v2, preferred_element_type=F32) / lsum)
            lses.append(m + jnp.log(lsum))
        od[pl.ds(qb, NK), :] = jnp.where(head0, outs[0], outs[1])
        ld[pl.ds(qb, NK), :] = jnp.where(head0, lses[0], lses[1])
        return carry

    lax.fori_loop(0, dil * nb, block, 0, unroll=ATTN_UNROLL)

    for r in range(dil):
        o_ref[pl.ds(r, msub, stride=dil), :] = od[r * msub:(r + 1) * msub, :]
        lse_ref[pl.ds(r, msub, stride=dil), :] = ld[r * msub:(r + 1) * msub, :]


def attn_prompt(P, B, S, q_gain, k_gain, slopes, l, g, win_prev):
    window, dil = ATTN_GROUPS[g]
    n_keep = min(window, S)
    chained = win_prev is not None

    def col(c):
        return pl.BlockSpec((S, LANES), lambda b, p: (b, c // LANES + 2 * g + p))

    def par(stack):
        return pl.BlockSpec((None, 1, LANES), lambda b, p: (stack * 6 + 2 * g + p, 0, 0))

    in_specs = [col(COL_AQ), col(COL_AK), col(COL_AV), par(l), par(l), pl.BlockSpec(memory_space=pltpu.SMEM)]
    args = [P, P, P, q_gain, k_gain, slopes]
    if chained:
        in_specs.append(pl.BlockSpec(memory_space=pl.ANY))
        args.append(win_prev)
    return pl.pallas_call(
        functools.partial(_attn_prompt_kernel, S=S, dil=dil, n_keep=n_keep, head_base=4 * g, chained=chained),
        grid=(B, 2),
        in_specs=in_specs,
        out_specs=[
            pl.BlockSpec((None, S, LANES), lambda b, p: (b, 0, p)),
            pl.BlockSpec((None, S, LANES), lambda b, p: (b, 0, p)),
            pl.BlockSpec((None, 2, LANES, n_keep), lambda b, p: (l * B + b, 0, p, 0)),
        ],
        out_shape=[
            jax.ShapeDtypeStruct((B, S, 256), F32),
            jax.ShapeDtypeStruct((B, S, 256), F32),
            jax.ShapeDtypeStruct((DEPTH * B, 2, 256, n_keep), F32),
        ],
        input_output_aliases={6: 2} if chained else {},
        scratch_shapes=[
            pltpu.VMEM((S, LANES), F32),
            pltpu.VMEM((S, LANES), BF16),
            pltpu.VMEM((S + dil * NK, LANES), BF16),
            pltpu.VMEM((S + dil * NK, LANES), BF16),
            pltpu.VMEM((S, LANES), F32),
            pltpu.VMEM((S, LANES), F32),
            pltpu.VMEM((4, NK, 2 * NK), F32),
        ],
        compiler_params=_cparams(("parallel", "arbitrary")),
    )(*args)


T_PAD = 8


def _attn_sample_kernel(q_ref, k_ref, v_ref, qg_ref, kg_ref, sl_ref, cache_ref, *rest,
                        L, dil, t_true, head_base, chained):
    if chained:
        rest = rest[1:]
    o_ref, lse_ref, cnew_ref = rest
    lane = lax.broadcasted_iota(jnp.int32, (1, LANES), 1)
    head0 = lane < HEAD_DIM
    trow8 = lax.broadcasted_iota(jnp.int32, (T_PAD, LANES), 0)
    lane8 = lax.broadcasted_iota(jnp.int32, (T_PAD, LANES), 1)
    place = ((lane8 == LANES - t_true + trow8) & (trow8 < t_true)).astype(F32)
    is_new = lane >= LANES - t_true

    def rolled_with_new(rows, new):
        shifted = pltpu.roll(cache_ref[rows, :], L - t_true, 1)
        tail = lax.dot_general(new, place, (((0,), (0,)), ((), ())), preferred_element_type=F32,
                               precision=lax.Precision.HIGHEST)
        if L > LANES:
            cnew_ref[rows, 0:L - LANES] = shifted[:, 0:L - LANES]
        cnew_ref[rows, L - LANES:L] = jnp.where(is_new, tail, shifted[:, L - LANES:L])

    trow = lax.broadcasted_iota(jnp.int32, (T_PAD, 1), 0)
    scol = lax.broadcasted_iota(jnp.int32, (1, L), 1)
    dist = L + trow - scol
    valid_c = ((dist & (dil - 1)) == 0) & (dist <= NK * dil)
    alibi_c = dist.astype(F32)

    for p in range(2):
        ls = slice(p * LANES, (p + 1) * LANES)
        qn = _head_norm(q_ref[:, ls], qg_ref[:, ls], head0)
        kn = _head_norm(k_ref[:, ls], kg_ref[:, ls], head0)
        vn = v_ref[:, ls]
        krows = slice(p * LANES, (p + 1) * LANES)
        vrows = slice(256 + p * LANES, 256 + (p + 1) * LANES)
        rolled_with_new(krows, kn)
        rolled_with_new(vrows, vn)
        kc_t = cache_ref[krows, :].astype(BF16)
        vc_t = cache_ref[vrows, :].astype(BF16)
        knr = kn.astype(BF16).astype(F32)
        vnr = vn.astype(BF16).astype(F32)
        outs, lses = [], []
        for hh in range(2):
            hmask = head0 if hh == 0 else jnp.logical_not(head0)
            qm = jnp.where(hmask, qn, 0.0).astype(BF16)
            qmr = qm.astype(F32)
            slope = sl_ref[head_base + 2 * p + hh]
            sc = jnp.dot(qm, kc_t, preferred_element_type=F32)
            sc = sc / float(np.sqrt(HEAD_DIM)) - slope * alibi_c
            sc = jnp.where(valid_c, sc, NEG)
            m = jnp.max(sc, axis=-1, keepdims=True)
            sn = []
            for u in range(t_true):
                du = trow - u
                su = jnp.sum(qmr * knr[u:u + 1, :], axis=-1, keepdims=True) / float(np.sqrt(HEAD_DIM))
                su = su - slope * du.astype(F32)
                su = jnp.where((du >= 0) & ((du & (dil - 1)) == 0), su, NEG)
                sn.append(su)
                m = jnp.maximum(m, su)
            pc = jnp.exp(sc - m)
            lsum = jnp.sum(pc, axis=-1, keepdims=True)
            acc = _dot_nt(pc.astype(BF16), vc_t)
            for u in range(t_true):
                pu = jnp.exp(sn[u] - m)
                lsum = lsum + pu
                acc = acc + pu.astype(BF16).astype(F32) * vnr[u:u + 1, :]
            outs.append(acc / lsum)
            lses.append(m + jnp.log(lsum))
        o_ref[:, ls] = jnp.where(head0, outs[0], outs[1])
        lse_ref[:, ls] = jnp.where(head0, lses[0], lses[1])


def attn_sample(P, B, q_gain, k_gain, slopes, cache_t, l, g, t_true, win_prev):
    window, dil = ATTN_GROUPS[g]
    L = cache_t.shape[2]
    chained = win_prev is not None

    def col(c):
        return pl.BlockSpec((T_PAD, 256), lambda b: (b, c // 256 + g))

    def par(stack):
        return pl.BlockSpec((None, 1, 256), lambda b: (stack * 3 + g, 0, 0))

    in_specs = [col(COL_AQ), col(COL_AK), col(COL_AV), par(l), par(l), pl.BlockSpec(memory_space=pltpu.SMEM),
                pl.BlockSpec((None, 512, L), lambda b: (l * B + b, 0, 0))]
    args = [P, P, P, q_gain, k_gain, slopes, cache_t]
    if chained:
        in_specs.append(pl.BlockSpec(memory_space=pl.ANY))
        args.append(win_prev)
    return pl.pallas_call(
        functools.partial(_attn_sample_kernel, L=L, dil=dil, t_true=t_true, head_base=4 * g, chained=chained),
        grid=(B,),
        in_specs=in_specs,
        out_specs=[
            pl.BlockSpec((None, T_PAD, 256), lambda b: (b, 0, 0)),
            pl.BlockSpec((None, T_PAD, 256), lambda b: (b, 0, 0)),
            pl.BlockSpec((None, 512, L), lambda b: (l * B + b, 0, 0)),
        ],
        out_shape=[
            jax.ShapeDtypeStruct((B, T_PAD, 256), F32),
            jax.ShapeDtypeStruct((B, T_PAD, 256), F32),
            jax.ShapeDtypeStruct((DEPTH * B, 512, L), F32),
        ],
        input_output_aliases={7: 2} if chained else {},
        compiler_params=_cparams(("parallel",)),
    )(*args)


def _attn_merge_kernel(o0, l0, o1, l1, o2, l2, *rest, n_tiles, slot_rows):
    def merge_tile(y_ref):
        a0, a1, a2 = l0[...], l1[...], l2[...]
        m = jnp.maximum(jnp.maximum(a0, a1), a2)
        e0, e1, e2 = jnp.exp(a0 - m), jnp.exp(a1 - m), jnp.exp(a2 - m)
        den = e0 + e1 + e2
        y = (e0 / den) * o0[...] + (e1 / den) * o1[...] + (e2 / den) * o2[...]
        y_ref[...] = y.astype(BF16)

    if slot_rows:
        slot_ref, y_ref = rest
        pl.when(pl.program_id(0) < n_tiles)(lambda: merge_tile(y_ref))

        @pl.when(pl.program_id(0) == n_tiles)
        def _():
            y_ref[0:slot_rows, :] = slot_ref[...]
    else:
        merge_tile(*rest)


def attn_merge(parts, tm, slot=None):
    M = parts[0].shape[0]
    n_tiles = M // tm
    slot_rows = slot.shape[0] if slot is not None else 0
    spec = pl.BlockSpec((tm, 256), lambda m: (jnp.minimum(m, n_tiles - 1), 0))
    return pl.pallas_call(
        functools.partial(_attn_merge_kernel, n_tiles=n_tiles, slot_rows=slot_rows),
        grid=(n_tiles + (1 if slot_rows else 0),),
        in_specs=[spec] * 6 + ([pl.BlockSpec((slot_rows, 256), lambda m: (0, 0))] if slot_rows else []),
        out_specs=pl.BlockSpec((tm, 256), lambda m: (m, 0)),
        out_shape=jax.ShapeDtypeStruct((M + slot_rows, 256), BF16),
        compiler_params=_cparams(("arbitrary",)),
    )(*parts, *(() if slot is None else (slot,)))


GATE_SRC0 = 5904
assert GATE_SRC0 % SRC_ALIGN == 0 and D_MODEL % SRC_ALIGN == 0


def _merge_kernel(xn_ref, ya_ref, yb_ref, yc_ref, yd_ref, g0, g1, g2, g3, u0, u1, u2, u3, o_ref,
                  gs0, gs1, gs2, gs3, us0, us1, us2, us3, *, tiles, tm):
    branches = ((ya_ref, g0, u0, gs0, us0), (yb_ref, g1, u1, gs1, us1),
                (yc_ref, g2, u2, gs2, us2), (yd_ref, g3, u3, gs3, us3))
    m = pl.program_id(1)

    @pl.when(m == 0)
    def _():
        for _, g_ref, u_ref, g_s, u_s in branches:
            g_s[...] = g_ref[...].astype(BF16)
            u_s[...] = u_ref[...].astype(BF16)

    def rows_fn(rows):
        xn = xn_ref[rows, :]
        acc = None
        for y_ref, _, _, g_s, u_s in branches:
            gate = jax.nn.sigmoid(_dot_nt(xn, g_s[...]))
            term = gate * jnp.dot(y_ref[rows, :], u_s[...], preferred_element_type=F32)
            acc = term if acc is None else acc + term
        o_ref[rows, :] = acc.astype(BF16)

    _on_tile_rows(m, tiles, tm, rows_fn)


def merge(xn, ya, yb, yc, yd, w_in_t, ups, l, tm):
    M = xn.shape[0]
    tiles = _row_tiles(M, tm)
    tn = PREP_ROWS
    widths = (512, 256, 512, 512)

    def act(width):
        return pl.BlockSpec((tm, width), lambda n, m: (m, 0))

    def gate(b):
        base = (GATE_SRC0 + b * D_MODEL) // SRC_ALIGN
        return _w_in_rows(l, lambda n, m: base + n * (tn // SRC_ALIGN))

    def up(width):
        return pl.BlockSpec((None, width, tn), lambda n, m: (l, 0, n))

    return pl.pallas_call(
        functools.partial(_merge_kernel, tiles=tiles, tm=tm),
        grid=(D_MODEL // tn, tiles[0]),
        in_specs=[act(D_MODEL)] + [act(w) for w in widths] + [gate(b) for b in range(N_BRANCH)]
                 + [up(w) for w in widths],
        out_specs=pl.BlockSpec((tm, tn), lambda n, m: (m, n)),
        out_shape=jax.ShapeDtypeStruct((M, D_MODEL), BF16),
        scratch_shapes=[pltpu.VMEM((tn, D_MODEL), BF16)] * N_BRANCH + [pltpu.VMEM((w, tn), BF16) for w in widths],
        compiler_params=_cparams(("parallel", "arbitrary")),
    )(xn, ya, yb, yc, yd, w_in_t, w_in_t, w_in_t, w_in_t, *ups)


def _layer(x, xn, W, l, T, mixers):
    h = ffn_up(xn, W['ff1_gate'], W['ff1_up'], l, T['up_m'], T['up_n'])
    x, xn = matmul_res(x, h, W['ff1_down'], l, 0.5, T['down_m'], W['n_mix'], l)
    P = mix_in(xn, W['w_in_t'], l, T['mix_m'])
    (ya, yb, yc, yd), states = mixers(P, l)
    mg = merge(xn, ya, yb, yc, yd, W['w_in_t'], (W['up_a'], W['up_b'], W['up_c'], W['up_d']), l, T['merge_m'])
    x, xn = matmul_res(x, mg, W['w_out'], l, 1.0, T['out_m'], W['n_ff2'], l)
    h = ffn_up(xn, W['ff2_gate'], W['ff2_up'], l, T['up_m'], T['up_n'])
    if l + 1 < DEPTH:
        x, xn = matmul_res(x, h, W['ff2_down'], l, 0.5, T['down_m'], W['n_ff1'], l + 1)
    else:
        x, xn = matmul_res(x, h, W['ff2_down'], l, 0.5, T['down_m'], out_split=True)
    return x, xn, states


SAMPLE_SLOT = 128
TILES = dict(up_m=2080, up_n=512, mix_m=1040, down_m=256, out_m=512, merge_m=1040)


def kernel(x_prompt, x_sample, state_conv, cache_w128_kv, cache_w512_kv, cache_w2048_kv, state_gla, state_pool,
           norm_ff1, ff1_gate, ff1_up, ff1_down, norm_mix, w_in, conv_w, attn_q_gain, attn_k_gain,
           gla_w_a2, gla_b_a, gla_norm, pool_w, pool_scale, w_up_conv, w_up_attn, w_up_gla, w_up_pool, w_out,
           norm_ff2, ff2_gate, ff2_up, ff2_down):
    BP, S, _ = x_prompt.shape
    BS, TS, _ = x_sample.shape
    caches = (cache_w128_kv, cache_w512_kv, cache_w2048_kv)

    W = {
        'n_ff1': norm_ff1.reshape(DEPTH, 1, D_MODEL), 'n_mix': norm_mix.reshape(DEPTH, 1, D_MODEL),
        'n_ff2': norm_ff2.reshape(DEPTH, 1, D_MODEL),
        'ff1_gate': ff1_gate, 'ff1_up': ff1_up, 'ff1_down': ff1_down,
        'ff2_gate': ff2_gate, 'ff2_up': ff2_up, 'ff2_down': ff2_down,
        'w_in_t': jnp.transpose(w_in, (0, 2, 1)),
        'up_a': w_up_conv, 'up_b': w_up_attn, 'up_c': w_up_gla, 'up_d': w_up_pool, 'w_out': w_out,
    }
    wa_pad = jnp.pad(gla_w_a2, ((0, 0), (0, LANES - gla_w_a2.shape[1]), (0, 0))).astype(BF16)
    b_a = gla_b_a.reshape(DEPTH, 1, 256)
    gn = gla_norm.reshape(DEPTH, 1, LANES)
    pool_wb = pool_w.astype(BF16)
    pool_sc = pool_scale.reshape(DEPTH, 1, W_POOL)
    i = jnp.arange(1, N_ATTN_HEADS + 1, dtype=F32)
    slopes = jnp.exp2(-8.0 * i / N_ATTN_HEADS)
    qg6 = attn_q_gain.reshape(DEPTH * 6, 1, LANES)
    kg6 = attn_k_gain.reshape(DEPTH * 6, 1, LANES)
    qg3 = attn_q_gain.reshape(DEPTH * 3, 1, 256)
    kg3 = attn_k_gain.reshape(DEPTH * 3, 1, 256)

    zc = jnp.zeros((BP, CONV_HIST, W_CONV), F32)
    zp = jnp.zeros((BP, POOL_HIST16, W_POOL), F32)
    zs = jnp.zeros((BP, 256, LANES), F32)

    win_p = [jnp.zeros((DEPTH * BP, 2, 256, min(w, S)), F32) for w, _ in ATTN_GROUPS]
    win_s = [jnp.zeros((DEPTH * BS, 512, cc.shape[2]), F32) for cc in caches]

    MP, MS = BP * S, BS * TS
    M_ALL = MP + SAMPLE_SLOT
    assert MS <= SAMPLE_SLOT

    def prompt_mixers(P, l, slot):
        ya, yd, cnew, pnew = conv_pool(P, BP, S, zc, zp, conv_w, pool_wb, pool_sc, l, 512, 512, 0,
                                       slot=(slot[0], slot[3]))
        yc, sfin = gla(P, BP, S, wa_pad, b_a, gn, zs, l, 512, GLA_CHUNK, GLA_CHUNK, slot=slot[2])
        parts = []
        for g in range(3):
            o, lse, win_p[g] = attn_prompt(P, BP, S, qg6, kg6, slopes, l, g, win_p[g])
            parts += [o.reshape(MP, 256), lse.reshape(MP, 256)]
        yb = attn_merge(parts, 1024, slot=slot[1])
        return (ya, yb, yc, yd), (cnew, sfin.reshape(BP, 4, GLA_DK, LANES), pnew[:, 1:])

    sconv8 = jnp.pad(state_conv, ((0, 0), (0, 0), (CONV_HIST - 2, 0), (0, 0)))
    spool16 = jnp.pad(state_pool, ((0, 0), (0, 0), (1, 0), (0, 0)))
    sgla = state_gla.reshape(DEPTH, BS, 256, LANES)
    cviews = [jnp.transpose(cc, (0, 1, 3, 4, 5, 2)).reshape(DEPTH * BS, 512, cc.shape[2]) for cc in caches]

    def sample_mixers(P, l):
        MSP = BS * T_PAD
        Ps = jnp.pad(P[MP:MP + MS].reshape(BS, TS, N_MIX), ((0, 0), (0, T_PAD - TS), (0, 0))).reshape(MSP, N_MIX)
        ya, yd, cnew, pnew = conv_pool(Ps, BS, T_PAD, sconv8[l], spool16[l], conv_w, pool_wb, pool_sc, l,
                                       T_PAD, TS, PAST_LEN)
        yc, sfin = gla(Ps, BS, T_PAD, wa_pad, b_a, gn, sgla[l], l, T_PAD, T_PAD, TS)
        parts = []
        for g in range(3):
            o, lse, win_s[g] = attn_sample(Ps, BS, qg3, kg3, slopes, cviews[g], l, g, TS, win_s[g])
            parts += [o[:, :TS].reshape(MS, 256), lse[:, :TS].reshape(MS, 256)]
        yb = attn_merge(parts, MS)

        def slot(y, padded=True):
            if padded:
                y = y.reshape(BS, T_PAD, -1)[:, :TS].reshape(MS, -1)
            return jnp.pad(y, ((0, SAMPLE_SLOT - MS), (0, 0)))

        ys = (slot(ya), slot(yb, False), slot(yc), slot(yd))
        return ys, (cnew, sfin.reshape(BS, 4, GLA_DK, LANES), pnew[:, 1:])

    def mixers(P, l):
        ys_s, st_sample = sample_mixers(P, l)
        ys, st_prompt = prompt_mixers(P, l, ys_s)
        return ys, (st_prompt, st_sample)

    x = (x_prompt.reshape(MP, D_MODEL), jnp.pad(x_sample.reshape(MS, D_MODEL), ((0, SAMPLE_SLOT - MS), (0, 0))))
    xn = norm(x[0], x[1], W['n_ff1'], 0, 1024)
    st_p, st_s = [], []
    for l in range(DEPTH):
        x, xn, (sp, ss) = _layer(x, xn, W, l, TILES, mixers)
        st_p.append(sp)
        st_s.append(ss)
    yp, ys = x[0], x[1][:MS]

    def stack(sts, f):
        return jnp.stack([f(s) for s in sts])

    def window_out(buf, B):
        n = buf.shape[-1]
        return jnp.transpose(buf.reshape(DEPTH, B, 2, 4, HEAD_DIM, n), (0, 1, 5, 2, 3, 4))

    outs = [yp.reshape(BP, S, D_MODEL), ys.reshape(BS, TS, D_MODEL),
            stack(st_p, lambda s: s[0]), stack(st_s, lambda s: s[0])]
    for g in range(3):
        outs += [window_out(win_p[g], BP), window_out(win_s[g], BS)]
    outs += [stack(st_p, lambda s: s[1]), stack(st_s, lambda s: s[1]),
             stack(st_p, lambda s: s[2]), stack(st_s, lambda s: s[2])]
    return tuple(outs)
```

```python
import functools

import numpy as np
import jax
import jax.numpy as jnp
from jax import lax
from jax.experimental import pallas as pl
from jax.experimental.pallas import tpu as pltpu

BF16 = jnp.bfloat16
F32 = jnp.float32

D_MODEL = 2048
DEPTH = 4
PAST_LEN = 16384
D_FF = 5504
EPS = 1e-6
N_BRANCH = 4
W_CONV = 512
ATTN_GROUPS = ((128, 1), (512, 4), (2048, 16))
N_ATTN_HEADS = 12
HEAD_DIM = 64
GLA_DK = 64
GLA_TAU = 16.0
GLA_CHUNK = 64
POOL_WINDOWS = (2, 4, 8, 16)
W_POOL = 512
NK = 128
LANES = 128
NEG = -1e30

COL_CB, COL_CC, COL_CH, COL_GV, COL_GR, COL_POOL = 0, 512, 1024, 1536, 2048, 2560
COL_AQ, COL_AK, COL_AV = 3072, 3840, 4608
COL_GQ, COL_GK, COL_LR = 5376, 5632, 5888
N_MIX = 6144
N_IN = 14096
PREP_ROWS = 256
MIX_TILE_SRC = ([256 * j for j in range(6)] + [4352, 4608, 4864, 5120, 5392, 5648]
                + [1536 + 256 * j for j in range(9)] + [3840, 4096, 5376])
SRC_ALIGN = 16
assert all(r % SRC_ALIGN == 0 for r in MIX_TILE_SRC)

VMEM_LIMIT = 56 * 1024 * 1024


def _cparams(sem):
    return pltpu.CompilerParams(dimension_semantics=sem, vmem_limit_bytes=VMEM_LIMIT)


def _rms(x):
    return x * lax.rsqrt(jnp.mean(x * x, axis=-1, keepdims=True) + EPS)


def _row_tiles(M, tm):
    n = pl.cdiv(M, tm)
    return n, M - (n - 1) * tm


def _on_tile_rows(m, tiles, tm, fn):
    n, tail = tiles
    if tail == tm:
        fn(slice(0, tm))
        return
    pl.when(m < n - 1)(lambda: fn(slice(0, tm)))
    pl.when(m == n - 1)(lambda: fn(slice(0, tail)))


def _split_rows_specs(tm, n_tiles, slot):
    return (pl.BlockSpec((tm, D_MODEL), lambda m: (jnp.minimum(m, n_tiles - 2), 0)),
            pl.BlockSpec((slot, D_MODEL), lambda m: (0, 0)))


def _norm_kernel(xp_ref, xs_ref, g_ref, xn_ref, *, tiles, tm):
    n, tail = tiles
    m = pl.program_id(0)

    @pl.when(m < n - 1)
    def _():
        xn_ref[...] = (_rms(xp_ref[...]) * g_ref[...]).astype(BF16)

    @pl.when(m == n - 1)
    def _():
        xn_ref[0:tail, :] = (_rms(xs_ref[...]) * g_ref[...]).astype(BF16)


def norm(xp, xs, gain, l, tm):
    MP, slot = xp.shape[0], xs.shape[0]
    assert MP % tm == 0 and slot < tm
    tiles = (MP // tm + 1, slot)
    return pl.pallas_call(
        functools.partial(_norm_kernel, tiles=tiles, tm=tm),
        grid=(tiles[0],),
        in_specs=[*_split_rows_specs(tm, tiles[0], slot),
                  pl.BlockSpec((None, 1, D_MODEL), lambda m: (l, 0, 0))],
        out_specs=pl.BlockSpec((tm, D_MODEL), lambda m: (m, 0)),
        out_shape=jax.ShapeDtypeStruct((MP + slot, D_MODEL), BF16),
        compiler_params=_cparams(("parallel",)),
    )(xp, xs, gain)


SUB_ROWS = 208


def _sub_blocks(rows):
    n_rows = rows.stop - rows.start
    sub = SUB_ROWS if n_rows % SUB_ROWS == 0 else n_rows
    return [slice(r0, r0 + sub) for r0 in range(rows.start, rows.stop, sub)]


def _ffn_up_kernel(xn_ref, wg_ref, wu_ref, h_ref, wg_s, wu_s, *, tiles, tm):
    m = pl.program_id(1)

    @pl.when(m == 0)
    def _():
        wg_s[...] = wg_ref[...].astype(BF16)
        wu_s[...] = wu_ref[...].astype(BF16)

    def rows_fn(rows):
        for r in _sub_blocks(rows):
            xn = xn_ref[r, :]
            a = jnp.dot(xn, wg_s[...], preferred_element_type=F32)
            b = jnp.dot(xn, wu_s[...], preferred_element_type=F32)
            h_ref[r, :] = (a * jax.nn.sigmoid(a) * b).astype(BF16)

    _on_tile_rows(m, tiles, tm, rows_fn)


def ffn_up(xn, wg, wu, l, tm, tn):
    M = xn.shape[0]
    tiles = _row_tiles(M, tm)
    return pl.pallas_call(
        functools.partial(_ffn_up_kernel, tiles=tiles, tm=tm),
        grid=(pl.cdiv(D_FF, tn), tiles[0]),
        in_specs=[
            pl.BlockSpec((tm, D_MODEL), lambda n, m: (m, 0)),
            pl.BlockSpec((None, D_MODEL, tn), lambda n, m: (l, 0, n)),
            pl.BlockSpec((None, D_MODEL, tn), lambda n, m: (l, 0, n)),
        ],
        out_specs=pl.BlockSpec((tm, tn), lambda n, m: (m, n)),
        out_shape=jax.ShapeDtypeStruct((M, D_FF), BF16),
        scratch_shapes=[pltpu.VMEM((D_MODEL, tn), BF16), pltpu.VMEM((D_MODEL, tn), BF16)],
        compiler_params=_cparams(("parallel", "arbitrary")),
    )(xn, wg, wu)


W_CHUNKS = 8


def _matmul_res_kernel(*refs, scale, with_norm, tiles, tm, x_split, out_split, ck):
    refs = list(refs)
    x_refs = [refs.pop(0) for _ in range(2 if x_split else 1)]
    a_ref, w_ref = refs.pop(0), refs.pop(0)
    g_ref = refs.pop(0) if with_norm else None
    o_refs = [refs.pop(0) for _ in range(2 if out_split else 1)]
    xn_ref = refs.pop(0) if with_norm else None
    w_s = refs.pop(0)
    n, tail = tiles
    step = pl.program_id(0)
    m = step - W_CHUNKS

    @pl.when(step < W_CHUNKS)
    def _():
        w_s[pl.ds(pl.multiple_of(step * ck, 16), ck), :] = w_ref[...].astype(BF16)

    def rows_fn(rows, last):
        x_ref = x_refs[-1] if last else x_refs[0]
        o_ref = o_refs[-1] if last else o_refs[0]
        y = x_ref[rows, :] + scale * jnp.dot(a_ref[rows, :], w_s[...], preferred_element_type=F32)
        if with_norm:
            xn_ref[rows, :] = (_rms(y) * g_ref[...]).astype(BF16)
        o_ref[rows, :] = y

    pl.when((m >= 0) & (m < n - 1))(lambda: rows_fn(slice(0, tm), False))
    pl.when(m == n - 1)(lambda: rows_fn(slice(0, tail), True))


def matmul_res(x, a, w, l, scale, tm, next_gain=None, next_l=0, out_split=False):
    M, K = a.shape
    with_norm = next_gain is not None
    x_split = isinstance(x, tuple)
    tiles = _row_tiles(M, tm)
    assert tiles[1] == SAMPLE_SLOT and (M - SAMPLE_SLOT) % tm == 0 and K % (16 * W_CHUNKS) == 0
    ck = K // W_CHUNKS

    def tile(step):
        return jnp.maximum(step - W_CHUNKS, 0)

    row = pl.BlockSpec((tm, D_MODEL), lambda s: (tile(s), 0))
    split = (pl.BlockSpec((tm, D_MODEL), lambda s: (jnp.minimum(tile(s), tiles[0] - 2), 0)),
             pl.BlockSpec((SAMPLE_SLOT, D_MODEL), lambda s: (0, 0)))
    in_specs = (list(split) if x_split else [row]) + [
        pl.BlockSpec((tm, K), lambda s: (tile(s), 0)),
        pl.BlockSpec((None, ck, D_MODEL), lambda s: (l, jnp.minimum(s, W_CHUNKS - 1), 0))]
    args = (list(x) if x_split else [x]) + [a, w]
    if out_split:
        out_specs = list(split)
        out_shape = [jax.ShapeDtypeStruct((M - SAMPLE_SLOT, D_MODEL), F32),
                     jax.ShapeDtypeStruct((SAMPLE_SLOT, D_MODEL), F32)]
    else:
        out_specs, out_shape = [row], [jax.ShapeDtypeStruct((M, D_MODEL), F32)]
    if with_norm:
        in_specs.append(pl.BlockSpec((None, 1, D_MODEL), lambda s: (next_l, 0, 0)))
        args.append(next_gain)
        out_specs.append(row)
        out_shape.append(jax.ShapeDtypeStruct((M, D_MODEL), BF16))
    res = pl.pallas_call(
        functools.partial(_matmul_res_kernel, scale=scale, with_norm=with_norm, tiles=tiles, tm=tm,
                          x_split=x_split, out_split=out_split, ck=ck),
        grid=(W_CHUNKS + tiles[0],),
        in_specs=in_specs,
        out_specs=out_specs,
        out_shape=out_shape,
        scratch_shapes=[pltpu.VMEM((K, D_MODEL), BF16)],
        compiler_params=_cparams(("arbitrary",)),
    )(*args)
    y = tuple(res[:2]) if out_split else res[0]
    return y, (res[-1] if with_norm else None)


def _dot_nt(a, b):
    return lax.dot_general(a, b, (((1,), (1,)), ((), ())), preferred_element_type=F32)


def _w_in_rows(l, row_of):
    return pl.BlockSpec((None, pl.Element(PREP_ROWS), pl.Element(D_MODEL)),
                        lambda *idx: (l, row_of(*idx) * SRC_ALIGN, 0))


MIX_SLABS = 4


def _mix_in_kernel(src_ref, xn_ref, *rest, tiles, tm):
    del src_ref
    w_refs, (p_ref, w_s) = rest[:MIX_SLABS], rest[MIX_SLABS:]
    m = pl.program_id(1)

    @pl.when(m == 0)
    def _():
        for j, w_ref in enumerate(w_refs):
            w_s[j * PREP_ROWS:(j + 1) * PREP_ROWS, :] = w_ref[...].astype(BF16)

    def rows_fn(rows):
        p_ref[rows, :] = _dot_nt(xn_ref[rows, :], w_s[...])

    _on_tile_rows(m, tiles, tm, rows_fn)


def mix_in(xn, w_in_t, l, tm):
    M = xn.shape[0]
    tn = MIX_SLABS * PREP_ROWS
    tiles = _row_tiles(M, tm)
    src = jnp.asarray(MIX_TILE_SRC, jnp.int32) // SRC_ALIGN

    def slab(j):
        return _w_in_rows(l, lambda n, m, src: src[MIX_SLABS * n + j])

    return pl.pallas_call(
        functools.partial(_mix_in_kernel, tiles=tiles, tm=tm),
        grid_spec=pltpu.PrefetchScalarGridSpec(
            num_scalar_prefetch=1,
            grid=(N_MIX // tn, tiles[0]),
            in_specs=[pl.BlockSpec((tm, D_MODEL), lambda n, m, src: (m, 0))] + [slab(j) for j in range(MIX_SLABS)],
            out_specs=pl.BlockSpec((tm, tn), lambda n, m, src: (m, n)),
            scratch_shapes=[pltpu.VMEM((tn, D_MODEL), BF16)],
        ),
        out_shape=jax.ShapeDtypeStruct((M, N_MIX), F32),
        compiler_params=_cparams(("parallel", "arbitrary")),
    )(src, xn, *([w_in_t] * MIX_SLABS))


CONV_HIST = 8
POOL_HIST16 = 16


def _conv_pool_kernel(cb_ref, cc_ref, ch_ref, pin_ref, cbuf_ref, pbuf_ref, cw_ref, pw_ref, ps_ref, *rest,
                      tt, t_true, pos0, n_seq, slot_rows):
    if slot_rows:
        ya_slot_ref, yd_slot_ref, ya_ref, yd_ref = rest[:4]

        @pl.when((pl.program_id(0) == n_seq) & (pl.program_id(1) == 0))
        def _():
            ya_ref[0:slot_rows, :] = ya_slot_ref[...]
            yd_ref[0:slot_rows, :] = yd_slot_ref[...]

        pl.when(pl.program_id(0) < n_seq)(
            lambda: _conv_pool_body(cb_ref, cc_ref, ch_ref, pin_ref, cbuf_ref, pbuf_ref, cw_ref, pw_ref, ps_ref,
                                    *rest[2:], tt=tt, t_true=t_true, pos0=pos0))
    else:
        _conv_pool_body(cb_ref, cc_ref, ch_ref, pin_ref, cbuf_ref, pbuf_ref, cw_ref, pw_ref, ps_ref, *rest,
                        tt=tt, t_true=t_true, pos0=pos0)


def _conv_pool_body(cb_ref, cc_ref, ch_ref, pin_ref, cbuf_ref, pbuf_ref, cw_ref, pw_ref, ps_ref,
                    ya_ref, yd_ref, cnew_ref, pnew_ref, uext, pext, *, tt, t_true, pos0):
    t = pl.program_id(1)

    @pl.when(t == 0)
    def _():
        uext[0:CONV_HIST, :] = cbuf_ref[...]
        pext[0:POOL_HIST16, :] = pbuf_ref[...]

    u = cc_ref[...] * ch_ref[...]
    uext[CONV_HIST:CONV_HIST + tt, :] = u
    w = cw_ref[...]
    z = uext[CONV_HIST - 2:CONV_HIST - 2 + tt, :] * w[0:1, :]
    z = z + uext[CONV_HIST - 1:CONV_HIST - 1 + tt, :] * w[1:2, :]
    z = z + u * w[2:3, :]
    ya_ref[...] = (cb_ref[...] * z).astype(BF16)
    cnew_ref[...] = uext[CONV_HIST + t_true - 2:CONV_HIST + t_true, :]
    uext[0:CONV_HIST, :] = uext[tt:tt + CONV_HIST, :]

    pin = pin_ref[...]
    pext[POOL_HIST16:POOL_HIST16 + tt, :] = pin
    pos = (pos0 + t * tt + lax.broadcasted_iota(jnp.int32, (tt, 1), 0)).astype(F32)
    for g, win in enumerate(POOL_WINDOWS):
        ls = slice(g * LANES, (g + 1) * LANES)
        acc = pin[:, ls]
        for i in range(1, win):
            acc = acc + pext[POOL_HIST16 - i:POOL_HIST16 - i + tt, ls]
        cnt = jnp.minimum(float(win), pos + 1.0)
        d = acc / cnt - pin[:, ls]
        y = jnp.dot(d.astype(BF16), pw_ref[g], preferred_element_type=F32) * ps_ref[:, ls]
        yd_ref[:, ls] = y.astype(BF16)
    pnew_ref[...] = pext[t_true:t_true + POOL_HIST16, :]
    pext[0:POOL_HIST16, :] = pext[tt:tt + POOL_HIST16, :]


def _seq_tiles(B, nt, with_slot):
    if not with_slot:
        return B, (lambda b, t: b * nt + t), (lambda b, t: b * nt + t), (lambda b: b)
    return (B + 1,
            lambda b, t: jnp.where(b < B, b * nt + t, B * nt - 1),
            lambda b, t: jnp.where(b < B, b * nt + t, B * nt),
            lambda b: jnp.minimum(b, B - 1))


def conv_pool(P, B, T, cbuf8, pbuf16, conv_w, pool_w, pool_scale, l, tt, t_true, pos0, slot=None):
    nt = T // tt
    slot_rows = slot[0].shape[0] if slot else 0
    rows_b, in_blk, out_blk, seq_blk = _seq_tiles(B, nt, bool(slot))

    def col(c):
        return pl.BlockSpec((tt, 512), lambda b, t: (in_blk(b, t), c // 512))

    slot_spec = pl.BlockSpec((slot_rows, 512), lambda b, t: (0, 0))
    return pl.pallas_call(
        functools.partial(_conv_pool_kernel, tt=tt, t_true=t_true, pos0=pos0, n_seq=B, slot_rows=slot_rows),
        grid=(rows_b, nt),
        in_specs=[
            col(COL_CB), col(COL_CC), col(COL_CH), col(COL_POOL),
            pl.BlockSpec((None, CONV_HIST, 512), lambda b, t: (seq_blk(b), 0, 0)),
            pl.BlockSpec((None, POOL_HIST16, 512), lambda b, t: (seq_blk(b), 0, 0)),
            pl.BlockSpec((None, 3, 512), lambda b, t: (l, 0, 0)),
            pl.BlockSpec((None, 4, LANES, LANES), lambda b, t: (l, 0, 0, 0)),
            pl.BlockSpec((None, 1, 512), lambda b, t: (l, 0, 0)),
        ] + ([slot_spec, slot_spec] if slot else []),
        out_specs=[
            pl.BlockSpec((tt, 512), lambda b, t: (out_blk(b, t), 0)),
            pl.BlockSpec((tt, 512), lambda b, t: (out_blk(b, t), 0)),
            pl.BlockSpec((None, 2, 512), lambda b, t: (seq_blk(b), 0, 0)),
            pl.BlockSpec((None, POOL_HIST16, 512), lambda b, t: (seq_blk(b), 0, 0)),
        ],
        out_shape=[
            jax.ShapeDtypeStruct((B * T + slot_rows, 512), BF16),
            jax.ShapeDtypeStruct((B * T + slot_rows, 512), BF16),
            jax.ShapeDtypeStruct((B, 2, 512), F32),
            jax.ShapeDtypeStruct((B, POOL_HIST16, 512), F32),
        ],
        scratch_shapes=[pltpu.VMEM((CONV_HIST + tt, 512), F32), pltpu.VMEM((POOL_HIST16 + tt, 512), F32)],
        compiler_params=_cparams(("arbitrary", "arbitrary")),
    )(P, P, P, P, cbuf8, pbuf16, conv_w, pool_w, pool_scale, *(slot or ()))


GLA_UNROLL = 4

def _gla_kernel(q_ref, k_ref, v_ref, r_ref, lr_ref, wa_ref, ba_ref, gn_ref, s0_ref, *rest,
                tt, C, c_true, n_seq, slot_rows):
    if slot_rows:
        yc_slot_ref, yc_ref = rest[:2]

        @pl.when((pl.program_id(0) == n_seq) & (pl.program_id(1) == 0))
        def _():
            yc_ref[0:slot_rows, :] = yc_slot_ref[...]

        pl.when(pl.program_id(0) < n_seq)(
            lambda: _gla_body(q_ref, k_ref, v_ref, r_ref, lr_ref, wa_ref, ba_ref, gn_ref, s0_ref, *rest[1:],
                              tt=tt, C=C, c_true=c_true))
    else:
        _gla_body(q_ref, k_ref, v_ref, r_ref, lr_ref, wa_ref, ba_ref, gn_ref, s0_ref, *rest,
                  tt=tt, C=C, c_true=c_true)


def _gla_body(q_ref, k_ref, v_ref, r_ref, lr_ref, wa_ref, ba_ref, gn_ref, s0_ref,
              yc_ref, sfin_ref, s_scr, b_scr, sprev_scr, *, tt, C, c_true):
    assert c_true == C or tt == C
    t = pl.program_id(1)

    @pl.when(t == 0)
    def _():
        s_scr[...] = s0_ref[...]

    row = lax.broadcasted_iota(jnp.int32, (C, 1), 0)
    lane = lax.broadcasted_iota(jnp.int32, (1, LANES), 1)
    head0 = lane < GLA_DK
    srow = lax.broadcasted_iota(jnp.int32, (LANES, 1), 0)
    eye = srow == lane
    gn = gn_ref[...]
    nchunk = tt // C

    def chunk_rows(c):
        return pl.ds(pl.multiple_of(c * C, C), C)

    def v_head(rows, h):
        v_h = v_ref[rows, h * LANES:(h + 1) * LANES]
        if c_true < C:
            v_h = jnp.where(row < c_true, v_h, 0.0)
        return v_h

    la = jnp.dot(lr_ref[...].astype(BF16), wa_ref[...], preferred_element_type=F32) + ba_ref[...]
    la = jax.nn.log_sigmoid(la) / GLA_TAU
    if c_true < C:
        la = jnp.where(row < c_true, la, 0.0)
    cb = 2 * C if nchunk % 2 == 0 else C
    bi = lax.broadcasted_iota(jnp.int32, (cb, 1), 0)
    bj = lax.broadcasted_iota(jnp.int32, (1, cb), 1)
    ones_tri = (((bi >= C) == (bj >= C)) & (bj <= bi)).astype(BF16)
    for blk in range(tt // cb):
        part = la[blk * cb:(blk + 1) * cb, :]
        acc = None
        for _ in range(3):
            term = part.astype(BF16)
            d = jnp.dot(ones_tri, term, preferred_element_type=F32)
            acc = d if acc is None else acc + d
            part = part - term.astype(F32)
        b_scr[blk * cb:(blk + 1) * cb, :] = acc

    def by_head(x):
        return jnp.concatenate([jnp.where(head0, x, 0.0), jnp.where(head0, 0.0, x)], axis=0)

    def v_pair(rows, p):
        return jnp.concatenate([v_head(rows, 2 * p), v_head(rows, 2 * p + 1)], axis=0).astype(BF16)

    def scan(c, carry):
        rows = chunk_rows(c)
        b = b_scr[rows, :]
        b_last = b[C - 1:C, :]
        kend = k_ref[rows, :] * jnp.exp(b_last - b)
        for p in range(2):
            ls = slice(p * LANES, (p + 1) * LANES)
            s_p = s_scr[ls, :]
            sprev_scr[c, ls, :] = s_p.astype(BF16)
            dec = jnp.exp(jnp.sum(jnp.where(eye, b_last[:, ls], 0.0), axis=1, keepdims=True))
            ds = lax.dot_general(by_head(kend[:, ls]).astype(BF16), v_pair(rows, p), (((0,), (0,)), ((), ())),
                                 preferred_element_type=F32)
            s_scr[ls, :] = dec * s_p + ds
        return carry

    lax.fori_loop(0, nchunk, scan, 0, unroll=min(GLA_UNROLL, nchunk))

    ri = lax.broadcasted_iota(jnp.int32, (2 * C, 1), 0)
    ci = lax.broadcasted_iota(jnp.int32, (1, 2 * C), 1)
    same_head_causal = ((ri >= C) == (ci >= C)) & ((ri % C) >= (ci % C))

    def emit(c, carry):
        rows = chunk_rows(c)
        b = b_scr[rows, :]
        qt = q_ref[rows, :] * (GLA_DK ** -0.5) * jnp.exp(b)
        kt = k_ref[rows, :] * jnp.exp(-b)
        for p in range(2):
            ls = slice(p * LANES, (p + 1) * LANES)
            qm = by_head(qt[:, ls]).astype(BF16)
            kt2 = jnp.concatenate([kt[:, ls], kt[:, ls]], axis=0).astype(BF16)
            a = jnp.where(same_head_causal, _dot_nt(qm, kt2), 0.0)
            o = jnp.dot(a.astype(BF16), v_pair(rows, p), preferred_element_type=F32)
            o = o + jnp.dot(qm, sprev_scr[c, ls, :], preferred_element_type=F32)
            on = _rms(o) * gn
            for hh in range(2):
                hs = slice((2 * p + hh) * LANES, (2 * p + hh + 1) * LANES)
                r_h = r_ref[rows, hs]
                yc_ref[rows, hs] = (on[hh * C:(hh + 1) * C, :] * (r_h * jax.nn.sigmoid(r_h))).astype(BF16)
        return carry

    lax.fori_loop(0, nchunk, emit, 0, unroll=min(GLA_UNROLL, nchunk))
    sfin_ref[...] = s_scr[...]


def gla(P, B, T, wa_pad, b_a, gla_norm, s0, l, tt, C, c_true, slot=None):
    nt = T // tt
    slot_rows = slot.shape[0] if slot is not None else 0
    rows_b, in_blk, out_blk, seq_blk = _seq_tiles(B, nt, slot is not None)
    return pl.pallas_call(
        functools.partial(_gla_kernel, tt=tt, C=C, c_true=c_true, n_seq=B, slot_rows=slot_rows),
        grid=(rows_b, nt),
        in_specs=[
            pl.BlockSpec((tt, 256), lambda b, t: (in_blk(b, t), COL_GQ // 256)),
            pl.BlockSpec((tt, 256), lambda b, t: (in_blk(b, t), COL_GK // 256)),
            pl.BlockSpec((tt, 512), lambda b, t: (in_blk(b, t), COL_GV // 512)),
            pl.BlockSpec((tt, 512), lambda b, t: (in_blk(b, t), COL_GR // 512)),
            pl.BlockSpec((tt, LANES), lambda b, t: (in_blk(b, t), COL_LR // LANES)),
            pl.BlockSpec((None, LANES, 256), lambda b, t: (l, 0, 0)),
            pl.BlockSpec((None, 1, 256), lambda b, t: (l, 0, 0)),
            pl.BlockSpec((None, 1, LANES), lambda b, t: (l, 0, 0)),
            pl.BlockSpec((None, 256, LANES), lambda b, t: (seq_blk(b), 0, 0)),
        ] + ([pl.BlockSpec((slot_rows, 512), lambda b, t: (0, 0))] if slot is not None else []),
        out_specs=[
            pl.BlockSpec((tt, 512), lambda b, t: (out_blk(b, t), 0)),
            pl.BlockSpec((None, 256, LANES), lambda b, t: (seq_blk(b), 0, 0)),
        ],
        out_shape=[jax.ShapeDtypeStruct((B * T + slot_rows, 512), BF16),
                   jax.ShapeDtypeStruct((B, 256, LANES), F32)],
        scratch_shapes=[pltpu.VMEM((256, LANES), F32), pltpu.VMEM((tt, 256), F32),
                        pltpu.VMEM((tt // C, 256, LANES), BF16)],
        compiler_params=_cparams(("arbitrary", "arbitrary")),
    )(P, P, P, P, P, wa_pad, b_a, gla_norm, s0, *(() if slot is None else (slot,)))


def _head_norm(x, gain, head0):
    x2 = x * x
    s0 = jnp.sum(jnp.where(head0, x2, 0.0), axis=-1, keepdims=True)
    s1 = jnp.sum(jnp.where(head0, 0.0, x2), axis=-1, keepdims=True)
    ms = jnp.where(head0, s0, s1) / float(HEAD_DIM)
    return x * lax.rsqrt(ms + EPS) * gain


NORM_ROWS = 512
ATTN_UNROLL = 8
QK_SCALE = 0.125


def _attn_prompt_kernel(q_ref, k_ref, v_ref, qg_ref, kg_ref, sl_ref, win_prev_ref,
                        o_ref, lse_ref, kv_ref, tmp, qd, kd, vd, od, ld, bias, *, S, dil, n_keep, head_base):
    del win_prev_ref
    msub = S // dil
    nb = msub // NK
    kstride = msub + NK
    lane = lax.broadcasted_iota(jnp.int32, (1, LANES), 1)
    head0 = lane < HEAD_DIM

    def norm_into_tmp(src_ref, g_ref):
        def body(i, carry):
            rows = pl.ds(pl.multiple_of(i * NORM_ROWS, NORM_ROWS), NORM_ROWS)
            tmp[rows, :] = _head_norm(src_ref[rows, :], g_ref[...], head0)
            return carry
        lax.fori_loop(0, S // NORM_ROWS, body, 0)

    norm_into_tmp(q_ref, qg_ref)
    for r in range(dil):
        qd[r * msub:(r + 1) * msub, :] = (tmp[pl.ds(r, msub, stride=dil), :] * QK_SCALE).astype(BF16)
    norm_into_tmp(k_ref, kg_ref)
    kv_ref[0] = tmp[S - n_keep:S, :].T
    kv_ref[1] = v_ref[S - n_keep:S, :].T
    zeros = jnp.zeros((NK, LANES), BF16)
    for r in range(dil):
        kd[r * kstride:r * kstride + NK, :] = zeros
        vd[r * kstride:r * kstride + NK, :] = zeros
        kd[r * kstride + NK:(r + 1) * kstride, :] = tmp[pl.ds(r, msub, stride=dil), :].astype(BF16)
        vd[r * kstride + NK:(r + 1) * kstride, :] = v_ref[pl.ds(r, msub, stride=dil), :].astype(BF16)

    qi = lax.broadcasted_iota(jnp.int32, (NK, 2 * NK), 0)
    ki = lax.broadcasted_iota(jnp.int32, (NK, 2 * NK), 1)
    dist = qi - ki + NK
    band = (dist >= 0) & (dist <= NK)
    alibi = (dist * dil).astype(F32)
    cur = ki >= NK
    for hh in range(2):
        slope = sl_ref[head_base + 2 * pl.program_id(1) + hh]
        bias[2 * hh] = jnp.where(band, -(slope * alibi), NEG)
        bias[2 * hh + 1] = jnp.where(band & cur, -(slope * alibi), NEG)

    def block(j, carry):
        r = j // nb
        n = j - r * nb
        first = jnp.where(n == 0, 1, 0)
        qb = pl.multiple_of(j * NK, NK)
        kb = pl.multiple_of(j * NK + r * NK, NK)
        q2 = qd[pl.ds(qb, NK), :]
        k2 = kd[pl.ds(kb, 2 * NK), :]
        v2 = vd[pl.ds(kb, 2 * NK), :]
        outs, lses = [], []
        for hh in range(2):
            hmask = head0 if hh == 0 else jnp.logical_not(head0)
            qm = jnp.where(hmask, q2, jnp.zeros_like(q2))
            s = lax.dot_general(qm, k2, (((1,), (1,)), ((), ())), preferred_element_type=F32)
            s = s + bias[2 * hh + first]
            m = jnp.max(s, axis=-1, keepdims=True)
            p = jnp.exp(s - m)
            lsum = jnp.sum(p, axis=-1, keepdims=True)
            outs.append(jnp.dot(p.astype(BF16), v2, preferred_element_type=F32) / lsum)
            lses.append(m + jnp.log(lsum))
        od[pl.ds(qb, NK), :] = jnp.where(head0, outs[0], outs[1])
        ld[pl.ds(qb, NK), :] = jnp.where(head0, lses[0], lses[1])
        return carry

    lax.fori_loop(0, dil * nb, block, 0, unroll=ATTN_UNROLL)

    for r in range(dil):
        o_ref[pl.ds(r, msub, stride=dil), :] = od[r * msub:(r + 1) * msub, :]
        lse_ref[pl.ds(r, msub, stride=dil), :] = ld[r * msub:(r + 1) * msub, :]


def attn_prompt(P, B, S, q_gain, k_gain, slopes, l, g, win_prev):
    window, dil = ATTN_GROUPS[g]
    n_keep = min(window, S)

    def col(c):
        return pl.BlockSpec((S, LANES), lambda b, p: (b, c // LANES + 2 * g + p))

    def par(stack):
        return pl.BlockSpec((None, 1, LANES), lambda b, p: (stack * 6 + 2 * g + p, 0, 0))

    return pl.pallas_call(
        functools.partial(_attn_prompt_kernel, S=S, dil=dil, n_keep=n_keep, head_base=4 * g),
        grid=(B, 2),
        in_specs=[col(COL_AQ), col(COL_AK), col(COL_AV), par(l), par(l), pl.BlockSpec(memory_space=pltpu.SMEM),
                  pl.BlockSpec(memory_space=pl.ANY)],
        out_specs=[
            pl.BlockSpec((None, S, LANES), lambda b, p: (b, 0, p)),
            pl.BlockSpec((None, S, LANES), lambda b, p: (b, 0, p)),
            pl.BlockSpec((None, 2, LANES, n_keep), lambda b, p: (l * B + b, 0, p, 0)),
        ],
        out_shape=[
            jax.ShapeDtypeStruct((B, S, 256), F32),
            jax.ShapeDtypeStruct((B, S, 256), F32),
            jax.ShapeDtypeStruct((DEPTH * B, 2, 256, n_keep), F32),
        ],
        input_output_aliases={6: 2},
        scratch_shapes=[
            pltpu.VMEM((S, LANES), F32),
            pltpu.VMEM((S, LANES), BF16),
            pltpu.VMEM((S + dil * NK, LANES), BF16),
            pltpu.VMEM((S + dil * NK, LANES), BF16),
            pltpu.VMEM((S, LANES), F32),
            pltpu.VMEM((S, LANES), F32),
            pltpu.VMEM((4, NK, 2 * NK), F32),
        ],
        compiler_params=_cparams(("parallel", "arbitrary")),
    )(P, P, P, q_gain, k_gain, slopes, win_prev)


T_PAD = 8


def _attn_sample_kernel(q_ref, k_ref, v_ref, qg_ref, kg_ref, sl_ref, cache_ref, win_prev_ref,
                        o_ref, lse_ref, cnew_ref, *, L, dil, t_true, head_base):
    del win_prev_ref
    lane = lax.broadcasted_iota(jnp.int32, (1, LANES), 1)
    head0 = lane < HEAD_DIM
    trow8 = lax.broadcasted_iota(jnp.int32, (T_PAD, LANES), 0)
    lane8 = lax.broadcasted_iota(jnp.int32, (T_PAD, LANES), 1)
    place = ((lane8 == LANES - t_true + trow8) & (trow8 < t_true)).astype(F32)
    is_new = lane >= LANES - t_true

    def rolled_with_new(rows, new):
        shifted = pltpu.roll(cache_ref[rows, :], L - t_true, 1)
        tail = lax.dot_general(new, place, (((0,), (0,)), ((), ())), preferred_element_type=F32,
                               precision=lax.Precision.HIGHEST)
        if L > LANES:
            cnew_ref[rows, 0:L - LANES] = shifted[:, 0:L - LANES]
        cnew_ref[rows, L - LANES:L] = jnp.where(is_new, tail, shifted[:, L - LANES:L])

    trow = lax.broadcasted_iota(jnp.int32, (T_PAD, 1), 0)
    scol = lax.broadcasted_iota(jnp.int32, (1, L), 1)
    dist = L + trow - scol
    valid_c = ((dist & (dil - 1)) == 0) & (dist <= NK * dil)
    alibi_c = dist.astype(F32)

    for p in range(2):
        ls = slice(p * LANES, (p + 1) * LANES)
        qn = _head_norm(q_ref[:, ls], qg_ref[:, ls], head0)
        kn = _head_norm(k_ref[:, ls], kg_ref[:, ls], head0)
        vn = v_ref[:, ls]
        krows = slice(p * LANES, (p + 1) * LANES)
        vrows = slice(256 + p * LANES, 256 + (p + 1) * LANES)
        rolled_with_new(krows, kn)
        rolled_with_new(vrows, vn)
        kc_t = cache_ref[krows, :].astype(BF16)
        vc_t = cache_ref[vrows, :].astype(BF16)
        knr = kn.astype(BF16).astype(F32)
        vnr = vn.astype(BF16).astype(F32)
        outs, lses = [], []
        for hh in range(2):
            hmask = head0 if hh == 0 else jnp.logical_not(head0)
            qm = jnp.where(hmask, qn, 0.0).astype(BF16)
            qmr = qm.astype(F32)
            slope = sl_ref[head_base + 2 * p + hh]
            sc = jnp.dot(qm, kc_t, preferred_element_type=F32)
            sc = sc / float(np.sqrt(HEAD_DIM)) - slope * alibi_c
            sc = jnp.where(valid_c, sc, NEG)
            m = jnp.max(sc, axis=-1, keepdims=True)
            sn = []
            for u in range(t_true):
                du = trow - u
                su = jnp.sum(qmr * knr[u:u + 1, :], axis=-1, keepdims=True) / float(np.sqrt(HEAD_DIM))
                su = su - slope * du.astype(F32)
                su = jnp.where((du >= 0) & ((du & (dil - 1)) == 0), su, NEG)
                sn.append(su)
                m = jnp.maximum(m, su)
            pc = jnp.exp(sc - m)
            lsum = jnp.sum(pc, axis=-1, keepdims=True)
            acc = _dot_nt(pc.astype(BF16), vc_t)
            for u in range(t_true):
                pu = jnp.exp(sn[u] - m)
                lsum = lsum + pu
                acc = acc + pu.astype(BF16).astype(F32) * vnr[u:u + 1, :]
            outs.append(acc / lsum)
            lses.append(m + jnp.log(lsum))
        o_ref[:, ls] = jnp.where(head0, outs[0], outs[1])
        lse_ref[:, ls] = jnp.where(head0, lses[0], lses[1])


def attn_sample(P, B, q_gain, k_gain, slopes, cache_t, l, g, t_true, win_prev):
    window, dil = ATTN_GROUPS[g]
    L = cache_t.shape[2]

    def col(c):
        return pl.BlockSpec((T_PAD, 256), lambda b: (b, c // 256 + g))

    def par(stack):
        return pl.BlockSpec((None, 1, 256), lambda b: (stack * 3 + g, 0, 0))

    return pl.pallas_call(
        functools.partial(_attn_sample_kernel, L=L, dil=dil, t_true=t_true, head_base=4 * g),
        grid=(B,),
        in_specs=[col(COL_AQ), col(COL_AK), col(COL_AV), par(l), par(l), pl.BlockSpec(memory_space=pltpu.SMEM),
                  pl.BlockSpec((None, 512, L), lambda b: (l * B + b, 0, 0)),
                  pl.BlockSpec(memory_space=pl.ANY)],
        out_specs=[
            pl.BlockSpec((None, T_PAD, 256), lambda b: (b, 0, 0)),
            pl.BlockSpec((None, T_PAD, 256), lambda b: (b, 0, 0)),
            pl.BlockSpec((None, 512, L), lambda b: (l * B + b, 0, 0)),
        ],
        out_shape=[
            jax.ShapeDtypeStruct((B, T_PAD, 256), F32),
            jax.ShapeDtypeStruct((B, T_PAD, 256), F32),
            jax.ShapeDtypeStruct((DEPTH * B, 512, L), F32),
        ],
        input_output_aliases={7: 2},
        compiler_params=_cparams(("parallel",)),
    )(P, P, P, q_gain, k_gain, slopes, cache_t, win_prev)


def _attn_merge_kernel(o0, l0, o1, l1, o2, l2, *rest, n_tiles, slot_rows):
    def merge_tile(y_ref):
        a0, a1, a2 = l0[...], l1[...], l2[...]
        m = jnp.maximum(jnp.maximum(a0, a1), a2)
        e0, e1, e2 = jnp.exp(a0 - m), jnp.exp(a1 - m), jnp.exp(a2 - m)
        den = e0 + e1 + e2
        y = (e0 / den) * o0[...] + (e1 / den) * o1[...] + (e2 / den) * o2[...]
        y_ref[...] = y.astype(BF16)

    if slot_rows:
        slot_ref, y_ref = rest
        pl.when(pl.program_id(0) < n_tiles)(lambda: merge_tile(y_ref))

        @pl.when(pl.program_id(0) == n_tiles)
        def _():
            y_ref[0:slot_rows, :] = slot_ref[...]
    else:
        merge_tile(*rest)


def attn_merge(parts, tm, slot=None):
    M = parts[0].shape[0]
    n_tiles = M // tm
    slot_rows = slot.shape[0] if slot is not None else 0
    spec = pl.BlockSpec((tm, 256), lambda m: (jnp.minimum(m, n_tiles - 1), 0))
    return pl.pallas_call(
        functools.partial(_attn_merge_kernel, n_tiles=n_tiles, slot_rows=slot_rows),
        grid=(n_tiles + (1 if slot_rows else 0),),
        in_specs=[spec] * 6 + ([pl.BlockSpec((slot_rows, 256), lambda m: (0, 0))] if slot_rows else []),
        out_specs=pl.BlockSpec((tm, 256), lambda m: (m, 0)),
        out_shape=jax.ShapeDtypeStruct((M + slot_rows, 256), BF16),
        compiler_params=_cparams(("arbitrary",)),
    )(*parts, *(() if slot is None else (slot,)))


GATE_SRC0 = 5904
assert GATE_SRC0 % SRC_ALIGN == 0 and D_MODEL % SRC_ALIGN == 0


def _merge_kernel(xn_ref, ya_ref, yb_ref, yc_ref, yd_ref, g0, g1, g2, g3, u0, u1, u2, u3, o_ref,
                  gs0, gs1, gs2, gs3, us0, us1, us2, us3, *, tiles, tm):
    branches = ((ya_ref, g0, u0, gs0, us0), (yb_ref, g1, u1, gs1, us1),
                (yc_ref, g2, u2, gs2, us2), (yd_ref, g3, u3, gs3, us3))
    m = pl.program_id(1)

    @pl.when(m == 0)
    def _():
        for _, g_ref, u_ref, g_s, u_s in branches:
            g_s[...] = g_ref[...].astype(BF16)
            u_s[...] = u_ref[...].astype(BF16)

    def rows_fn(rows):
        xn = xn_ref[rows, :]
        acc = None
        for y_ref, _, _, g_s, u_s in branches:
            gate = jax.nn.sigmoid(_dot_nt(xn, g_s[...]))
            term = gate * jnp.dot(y_ref[rows, :], u_s[...], preferred_element_type=F32)
            acc = term if acc is None else acc + term
        o_ref[rows, :] = acc.astype(BF16)

    _on_tile_rows(m, tiles, tm, rows_fn)


def merge(xn, ya, yb, yc, yd, w_in_t, ups, l, tm):
    M = xn.shape[0]
    tiles = _row_tiles(M, tm)
    tn = PREP_ROWS
    widths = (512, 256, 512, 512)

    def act(width):
        return pl.BlockSpec((tm, width), lambda n, m: (m, 0))

    def gate(b):
        base = (GATE_SRC0 + b * D_MODEL) // SRC_ALIGN
        return _w_in_rows(l, lambda n, m: base + n * (tn // SRC_ALIGN))

    def up(width):
        return pl.BlockSpec((None, width, tn), lambda n, m: (l, 0, n))

    return pl.pallas_call(
        functools.partial(_merge_kernel, tiles=tiles, tm=tm),
        grid=(D_MODEL // tn, tiles[0]),
        in_specs=[act(D_MODEL)] + [act(w) for w in widths] + [gate(b) for b in range(N_BRANCH)]
                 + [up(w) for w in widths],
        out_specs=pl.BlockSpec((tm, tn), lambda n, m: (m, n)),
        out_shape=jax.ShapeDtypeStruct((M, D_MODEL), BF16),
        scratch_shapes=[pltpu.VMEM((tn, D_MODEL), BF16)] * N_BRANCH + [pltpu.VMEM((w, tn), BF16) for w in widths],
        compiler_params=_cparams(("parallel", "arbitrary")),
    )(xn, ya, yb, yc, yd, w_in_t, w_in_t, w_in_t, w_in_t, *ups)


def _layer(x, xn, W, l, T, mixers):
    h = ffn_up(xn, W['ff1_gate'], W['ff1_up'], l, T['up_m'], T['up_n'])
    x, xn = matmul_res(x, h, W['ff1_down'], l, 0.5, T['down_m'], W['n_mix'], l)
    P = mix_in(xn, W['w_in_t'], l, T['mix_m'])
    (ya, yb, yc, yd), states = mixers(P, l)
    mg = merge(xn, ya, yb, yc, yd, W['w_in_t'], (W['up_a'], W['up_b'], W['up_c'], W['up_d']), l, T['merge_m'])
    x, xn = matmul_res(x, mg, W['w_out'], l, 1.0, T['out_m'], W['n_ff2'], l)
    h = ffn_up(xn, W['ff2_gate'], W['ff2_up'], l, T['up_m'], T['up_n'])
    if l + 1 < DEPTH:
        x, xn = matmul_res(x, h, W['ff2_down'], l, 0.5, T['down_m'], W['n_ff1'], l + 1)
    else:
        x, xn = matmul_res(x, h, W['ff2_down'], l, 0.5, T['down_m'], out_split=True)
    return x, xn, states


SAMPLE_SLOT = 128
TILES = dict(up_m=2080, up_n=512, mix_m=1040, down_m=256, out_m=512, merge_m=1040)


def kernel(x_prompt, x_sample, state_conv, cache_w128_kv, cache_w512_kv, cache_w2048_kv, state_gla, state_pool,
           norm_ff1, ff1_gate, ff1_up, ff1_down, norm_mix, w_in, conv_w, attn_q_gain, attn_k_gain,
           gla_w_a2, gla_b_a, gla_norm, pool_w, pool_scale, w_up_conv, w_up_attn, w_up_gla, w_up_pool, w_out,
           norm_ff2, ff2_gate, ff2_up, ff2_down):
    BP, S, _ = x_prompt.shape
    BS, TS, _ = x_sample.shape
    caches = (cache_w128_kv, cache_w512_kv, cache_w2048_kv)

    W = {
        'n_ff1': norm_ff1.reshape(DEPTH, 1, D_MODEL), 'n_mix': norm_mix.reshape(DEPTH, 1, D_MODEL),
        'n_ff2': norm_ff2.reshape(DEPTH, 1, D_MODEL),
        'ff1_gate': ff1_gate, 'ff1_up': ff1_up, 'ff1_down': ff1_down,
        'ff2_gate': ff2_gate, 'ff2_up': ff2_up, 'ff2_down': ff2_down,
        'w_in_t': jnp.transpose(w_in, (0, 2, 1)),
        'up_a': w_up_conv, 'up_b': w_up_attn, 'up_c': w_up_gla, 'up_d': w_up_pool, 'w_out': w_out,
    }
    wa_pad = jnp.pad(gla_w_a2, ((0, 0), (0, LANES - gla_w_a2.shape[1]), (0, 0))).astype(BF16)
    b_a = gla_b_a.reshape(DEPTH, 1, 256)
    gn = gla_norm.reshape(DEPTH, 1, LANES)
    pool_wb = pool_w.astype(BF16)
    pool_sc = pool_scale.reshape(DEPTH, 1, W_POOL)
    i = jnp.arange(1, N_ATTN_HEADS + 1, dtype=F32)
    slopes = jnp.exp2(-8.0 * i / N_ATTN_HEADS)
    qg6 = attn_q_gain.reshape(DEPTH * 6, 1, LANES)
    kg6 = attn_k_gain.reshape(DEPTH * 6, 1, LANES)
    qg3 = attn_q_gain.reshape(DEPTH * 3, 1, 256)
    kg3 = attn_k_gain.reshape(DEPTH * 3, 1, 256)

    zc = jnp.zeros((BP, CONV_HIST, W_CONV), F32)
    zp = jnp.zeros((BP, POOL_HIST16, W_POOL), F32)
    zs = jnp.zeros((BP, 256, LANES), F32)

    win_p = [jnp.zeros((DEPTH * BP, 2, 256, min(w, S)), F32) for w, _ in ATTN_GROUPS]
    win_s = [jnp.zeros((DEPTH * BS, 512, cc.shape[2]), F32) for cc in caches]

    MP, MS = BP * S, BS * TS
    assert MS <= SAMPLE_SLOT

    def prompt_mixers(P, l, slot):
        ya, yd, cnew, pnew = conv_pool(P, BP, S, zc, zp, conv_w, pool_wb, pool_sc, l, 512, 512, 0,
                                       slot=(slot[0], slot[3]))
        yc, sfin = gla(P, BP, S, wa_pad, b_a, gn, zs, l, 512, GLA_CHUNK, GLA_CHUNK, slot=slot[2])
        parts = []
        for g in range(3):
            o, lse, win_p[g] = attn_prompt(P, BP, S, qg6, kg6, slopes, l, g, win_p[g])
            parts += [o.reshape(MP, 256), lse.reshape(MP, 256)]
        yb = attn_merge(parts, 1024, slot=slot[1])
        return (ya, yb, yc, yd), (cnew, sfin.reshape(BP, 4, GLA_DK, LANES), pnew[:, 1:])

    sconv8 = jnp.pad(state_conv, ((0, 0), (0, 0), (CONV_HIST - 2, 0), (0, 0)))
    spool16 = jnp.pad(state_pool, ((0, 0), (0, 0), (1, 0), (0, 0)))
    sgla = state_gla.reshape(DEPTH, BS, 256, LANES)
    cviews = [jnp.transpose(cc, (0, 1, 3, 4, 5, 2)).reshape(DEPTH * BS, 512, cc.shape[2]) for cc in caches]

    def sample_mixers(P, l):
        MSP = BS * T_PAD
        Ps = jnp.pad(P[MP:MP + MS].reshape(BS, TS, N_MIX), ((0, 0), (0, T_PAD - TS), (0, 0))).reshape(MSP, N_MIX)
        ya, yd, cnew, pnew = conv_pool(Ps, BS, T_PAD, sconv8[l], spool16[l], conv_w, pool_wb, pool_sc, l,
                                       T_PAD, TS, PAST_LEN)
        yc, sfin = gla(Ps, BS, T_PAD, wa_pad, b_a, gn, sgla[l], l, T_PAD, T_PAD, TS)
        parts = []
        for g in range(3):
            o, lse, win_s[g] = attn_sample(Ps, BS, qg3, kg3, slopes, cviews[g], l, g, TS, win_s[g])
            parts += [o[:, :TS].reshape(MS, 256), lse[:, :TS].reshape(MS, 256)]
        yb = attn_merge(parts, MS)

        def slot(y, padded=True):
            if padded:
                y = y.reshape(BS, T_PAD, -1)[:, :TS].reshape(MS, -1)
            return jnp.pad(y, ((0, SAMPLE_SLOT - MS), (0, 0)))

        ys = (slot(ya), slot(yb, False), slot(yc), slot(yd))
        return ys, (cnew, sfin.reshape(BS, 4, GLA_DK, LANES), pnew[:, 1:])

    def mixers(P, l):
        ys_s, st_sample = sample_mixers(P, l)
        ys, st_prompt = prompt_mixers(P, l, ys_s)
        return ys, (st_prompt, st_sample)

    x = (x_prompt.reshape(MP, D_MODEL), jnp.pad(x_sample.reshape(MS, D_MODEL), ((0, SAMPLE_SLOT - MS), (0, 0))))
    xn = norm(x[0], x[1], W['n_ff1'], 0, 1024)
    st_p, st_s = [], []
    for l in range(DEPTH):
        x, xn, (sp, ss) = _layer(x, xn, W, l, TILES, mixers)
        st_p.append(sp)
        st_s.append(ss)
    yp, ys = x[0], x[1][:MS]

    def stack(sts, f):
        return jnp.stack([f(s) for s in sts])

    def window_out(buf, B):
        n = buf.shape[-1]
        return jnp.transpose(buf.reshape(DEPTH, B, 2, 4, HEAD_DIM, n), (0, 1, 5, 2, 3, 4))

    outs = [yp.reshape(BP, S, D_MODEL), ys.reshape(BS, TS, D_MODEL),
            stack(st_p, lambda s: s[0]), stack(st_s, lambda s: s[0])]
    for g in range(3):
        outs += [window_out(win_p[g], BP), window_out(win_s[g], BS)]
    outs += [stack(st_p, lambda s: s[1]), stack(st_s, lambda s: s[1]),
             stack(st_p, lambda s: s[2]), stack(st_s, lambda s: s[2])]
    return tuple(outs)
```

```python
import functools

import numpy as np
import jax
import jax.numpy as jnp
from jax import lax
from jax.experimental import pallas as pl
from jax.experimental.pallas import tpu as pltpu

BF16 = jnp.bfloat16
F32 = jnp.float32

D_MODEL = 2048
DEPTH = 4
PAST_LEN = 16384
D_FF = 5504
EPS = 1e-6
N_BRANCH = 4
W_CONV = 512
ATTN_GROUPS = ((128, 1), (512, 4), (2048, 16))
N_ATTN_HEADS = 12
HEAD_DIM = 64
GLA_DK = 64
GLA_TAU = 16.0
GLA_CHUNK = 64
POOL_WINDOWS = (2, 4, 8, 16)
W_POOL = 512
NK = 128
LANES = 128
NEG = -1e30

COL_CB, COL_CC, COL_CH, COL_GV, COL_GR, COL_POOL = 0, 512, 1024, 1536, 2048, 2560
COL_AQ, COL_AK, COL_AV = 3072, 3840, 4608
COL_GQ, COL_GK, COL_LR = 5376, 5632, 5888
N_MIX = 6144
N_IN = 14096
PREP_ROWS = 256
MIX_TILE_SRC = ([256 * j for j in range(6)] + [4352, 4608, 4864, 5120, 5392, 5648]
                + [1536 + 256 * j for j in range(9)] + [3840, 4096, 5376])
SRC_ALIGN = 16
assert all(r % SRC_ALIGN == 0 for r in MIX_TILE_SRC)

VMEM_LIMIT = 56 * 1024 * 1024


def _cparams(sem):
    return pltpu.CompilerParams(dimension_semantics=sem, vmem_limit_bytes=VMEM_LIMIT)


def _rms(x):
    return x * lax.rsqrt(jnp.mean(x * x, axis=-1, keepdims=True) + EPS)


def _row_tiles(M, tm):
    n = pl.cdiv(M, tm)
    return n, M - (n - 1) * tm


def _on_tile_rows(m, tiles, tm, fn):
    n, tail = tiles
    if tail == tm:
        fn(slice(0, tm))
        return
    pl.when(m < n - 1)(lambda: fn(slice(0, tm)))
    pl.when(m == n - 1)(lambda: fn(slice(0, tail)))


def _split_rows_specs(tm, n_tiles, slot):
    return (pl.BlockSpec((tm, D_MODEL), lambda m: (jnp.minimum(m, n_tiles - 2), 0)),
            pl.BlockSpec((slot, D_MODEL), lambda m: (0, 0)))


def _norm_kernel(xp_ref, xs_ref, g_ref, xn_ref, *, tiles, tm):
    n, tail = tiles
    m = pl.program_id(0)

    @pl.when(m < n - 1)
    def _():
        xn_ref[...] = (_rms(xp_ref[...]) * g_ref[...]).astype(BF16)

    @pl.when(m == n - 1)
    def _():
        xn_ref[0:tail, :] = (_rms(xs_ref[...]) * g_ref[...]).astype(BF16)


def norm(xp, xs, gain, l, tm):
    MP, slot = xp.shape[0], xs.shape[0]
    assert MP % tm == 0 and slot < tm
    tiles = (MP // tm + 1, slot)
    return pl.pallas_call(
        functools.partial(_norm_kernel, tiles=tiles, tm=tm),
        grid=(tiles[0],),
        in_specs=[*_split_rows_specs(tm, tiles[0], slot),
                  pl.BlockSpec((None, 1, D_MODEL), lambda m: (l, 0, 0))],
        out_specs=pl.BlockSpec((tm, D_MODEL), lambda m: (m, 0)),
        out_shape=jax.ShapeDtypeStruct((MP + slot, D_MODEL), BF16),
        compiler_params=_cparams(("parallel",)),
    )(xp, xs, gain)


SUB_ROWS = 208


def _sub_blocks(rows):
    n_rows = rows.stop - rows.start
    sub = SUB_ROWS if n_rows % SUB_ROWS == 0 else n_rows
    return [slice(r0, r0 + sub) for r0 in range(rows.start, rows.stop, sub)]


def _ffn_up_kernel(xn_ref, wg_ref, wu_ref, h_ref, wg_s, wu_s, *, tiles, tm):
    m = pl.program_id(1)

    @pl.when(m == 0)
    def _():
        wg_s[...] = wg_ref[...].astype(BF16)
        wu_s[...] = wu_ref[...].astype(BF16)

    def rows_fn(rows):
        for r in _sub_blocks(rows):
            xn = xn_ref[r, :]
            a = jnp.dot(xn, wg_s[...], preferred_element_type=F32)
            b = jnp.dot(xn, wu_s[...], preferred_element_type=F32)
            h_ref[r, :] = (a * jax.nn.sigmoid(a) * b).astype(BF16)

    _on_tile_rows(m, tiles, tm, rows_fn)


def ffn_up(xn, wg, wu, l, tm, tn):
    M = xn.shape[0]
    tiles = _row_tiles(M, tm)
    return pl.pallas_call(
        functools.partial(_ffn_up_kernel, tiles=tiles, tm=tm),
        grid=(pl.cdiv(D_FF, tn), tiles[0]),
        in_specs=[
            pl.BlockSpec((tm, D_MODEL), lambda n, m: (m, 0)),
            pl.BlockSpec((None, D_MODEL, tn), lambda n, m: (l, 0, n)),
            pl.BlockSpec((None, D_MODEL, tn), lambda n, m: (l, 0, n)),
        ],
        out_specs=pl.BlockSpec((tm, tn), lambda n, m: (m, n)),
        out_shape=jax.ShapeDtypeStruct((M, D_FF), BF16),
        scratch_shapes=[pltpu.VMEM((D_MODEL, tn), BF16), pltpu.VMEM((D_MODEL, tn), BF16)],
        compiler_params=_cparams(("parallel", "arbitrary")),
    )(xn, wg, wu)


W_CHUNKS = 8


def _matmul_res_kernel(*refs, scale, with_norm, tiles, tm, x_split, out_split, ck):
    refs = list(refs)
    x_refs = [refs.pop(0) for _ in range(2 if x_split else 1)]
    a_ref, w_ref = refs.pop(0), refs.pop(0)
    g_ref = refs.pop(0) if with_norm else None
    o_refs = [refs.pop(0) for _ in range(2 if out_split else 1)]
    xn_ref = refs.pop(0) if with_norm else None
    w_s = refs.pop(0)
    n, tail = tiles
    step = pl.program_id(0)
    m = step - W_CHUNKS

    @pl.when(step < W_CHUNKS)
    def _():
        w_s[pl.ds(pl.multiple_of(step * ck, 16), ck), :] = w_ref[...].astype(BF16)

    def rows_fn(rows, last):
        x_ref = x_refs[-1] if last else x_refs[0]
        o_ref = o_refs[-1] if last else o_refs[0]
        y = x_ref[rows, :] + scale * jnp.dot(a_ref[rows, :], w_s[...], preferred_element_type=F32)
        if with_norm:
            xn_ref[rows, :] = (_rms(y) * g_ref[...]).astype(BF16)
        o_ref[rows, :] = y

    pl.when((m >= 0) & (m < n - 1))(lambda: rows_fn(slice(0, tm), False))
    pl.when(m == n - 1)(lambda: rows_fn(slice(0, tail), True))


def matmul_res(x, a, w, l, scale, tm, next_gain=None, next_l=0, out_split=False):
    M, K = a.shape
    with_norm = next_gain is not None
    x_split = isinstance(x, tuple)
    tiles = _row_tiles(M, tm)
    assert tiles[1] == SAMPLE_SLOT and (M - SAMPLE_SLOT) % tm == 0 and K % (16 * W_CHUNKS) == 0
    ck = K // W_CHUNKS

    def tile(step):
        return jnp.maximum(step - W_CHUNKS, 0)

    row = pl.BlockSpec((tm, D_MODEL), lambda s: (tile(s), 0))
    split = (pl.BlockSpec((tm, D_MODEL), lambda s: (jnp.minimum(tile(s), tiles[0] - 2), 0)),
             pl.BlockSpec((SAMPLE_SLOT, D_MODEL), lambda s: (0, 0)))
    in_specs = (list(split) if x_split else [row]) + [
        pl.BlockSpec((tm, K), lambda s: (tile(s), 0)),
        pl.BlockSpec((None, ck, D_MODEL), lambda s: (l, jnp.minimum(s, W_CHUNKS - 1), 0))]
    args = (list(x) if x_split else [x]) + [a, w]
    if out_split:
        out_specs = list(split)
        out_shape = [jax.ShapeDtypeStruct((M - SAMPLE_SLOT, D_MODEL), F32),
                     jax.ShapeDtypeStruct((SAMPLE_SLOT, D_MODEL), F32)]
    else:
        out_specs, out_shape = [row], [jax.ShapeDtypeStruct((M, D_MODEL), F32)]
    if with_norm:
        in_specs.append(pl.BlockSpec((None, 1, D_MODEL), lambda s: (next_l, 0, 0)))
        args.append(next_gain)
        out_specs.append(row)
        out_shape.append(jax.ShapeDtypeStruct((M, D_MODEL), BF16))
    res = pl.pallas_call(
        functools.partial(_matmul_res_kernel, scale=scale, with_norm=with_norm, tiles=tiles, tm=tm,
                          x_split=x_split, out_split=out_split, ck=ck),
        grid=(W_CHUNKS + tiles[0],),
        in_specs=in_specs,
        out_specs=out_specs,
        out_shape=out_shape,
        scratch_shapes=[pltpu.VMEM((K, D_MODEL), BF16)],
        compiler_params=_cparams(("arbitrary",)),
    )(*args)
    y = tuple(res[:2]) if out_split else res[0]
    return y, (res[-1] if with_norm else None)


def _dot_nt(a, b):
    return lax.dot_general(a, b, (((1,), (1,)), ((), ())), preferred_element_type=F32)


def _w_in_rows(l, row_of):
    return pl.BlockSpec((None, pl.Element(PREP_ROWS), pl.Element(D_MODEL)),
                        lambda *idx: (l, row_of(*idx) * SRC_ALIGN, 0))


MIX_SLABS = 4


def _mix_in_kernel(src_ref, xn_ref, *rest, tiles, tm):
    del src_ref
    w_refs, (p_ref, w_s) = rest[:MIX_SLABS], rest[MIX_SLABS:]
    m = pl.program_id(1)

    @pl.when(m == 0)
    def _():
        for j, w_ref in enumerate(w_refs):
            w_s[j * PREP_ROWS:(j + 1) * PREP_ROWS, :] = w_ref[...].astype(BF16)

    def rows_fn(rows):
        p_ref[rows, :] = _dot_nt(xn_ref[rows, :], w_s[...])

    _on_tile_rows(m, tiles, tm, rows_fn)


def mix_in(xn, w_in_t, l, tm):
    M = xn.shape[0]
    tn = MIX_SLABS * PREP_ROWS
    tiles = _row_tiles(M, tm)
    src = jnp.asarray(MIX_TILE_SRC, jnp.int32) // SRC_ALIGN

    def slab(j):
        return _w_in_rows(l, lambda n, m, src: src[MIX_SLABS * n + j])

    return pl.pallas_call(
        functools.partial(_mix_in_kernel, tiles=tiles, tm=tm),
        grid_spec=pltpu.PrefetchScalarGridSpec(
            num_scalar_prefetch=1,
            grid=(N_MIX // tn, tiles[0]),
            in_specs=[pl.BlockSpec((tm, D_MODEL), lambda n, m, src: (m, 0))] + [slab(j) for j in range(MIX_SLABS)],
            out_specs=pl.BlockSpec((tm, tn), lambda n, m, src: (m, n)),
            scratch_shapes=[pltpu.VMEM((tn, D_MODEL), BF16)],
        ),
        out_shape=jax.ShapeDtypeStruct((M, N_MIX), F32),
        compiler_params=_cparams(("parallel", "arbitrary")),
    )(src, xn, *([w_in_t] * MIX_SLABS))


CONV_HIST = 8
POOL_HIST16 = 16


def _conv_pool_kernel(cb_ref, cc_ref, ch_ref, pin_ref, cbuf_ref, pbuf_ref, cw_ref, pw_ref, ps_ref, *rest,
                      tt, t_true, pos0, n_seq, slot_rows):
    if slot_rows:
        ya_slot_ref, yd_slot_ref, ya_ref, yd_ref = rest[:4]

        @pl.when((pl.program_id(0) == n_seq) & (pl.program_id(1) == 0))
        def _():
            ya_ref[0:slot_rows, :] = ya_slot_ref[...]
            yd_ref[0:slot_rows, :] = yd_slot_ref[...]

        pl.when(pl.program_id(0) < n_seq)(
            lambda: _conv_pool_body(cb_ref, cc_ref, ch_ref, pin_ref, cbuf_ref, pbuf_ref, cw_ref, pw_ref, ps_ref,
                                    *rest[2:], tt=tt, t_true=t_true, pos0=pos0))
    else:
        _conv_pool_body(cb_ref, cc_ref, ch_ref, pin_ref, cbuf_ref, pbuf_ref, cw_ref, pw_ref, ps_ref, *rest,
                        tt=tt, t_true=t_true, pos0=pos0)


def _conv_pool_body(cb_ref, cc_ref, ch_ref, pin_ref, cbuf_ref, pbuf_ref, cw_ref, pw_ref, ps_ref,
                    ya_ref, yd_ref, cnew_ref, pnew_ref, uext, pext, *, tt, t_true, pos0):
    t = pl.program_id(1)

    @pl.when(t == 0)
    def _():
        uext[0:CONV_HIST, :] = cbuf_ref[...]
        pext[0:POOL_HIST16, :] = pbuf_ref[...]

    u = cc_ref[...] * ch_ref[...]
    uext[CONV_HIST:CONV_HIST + tt, :] = u
    w = cw_ref[...]
    z = uext[CONV_HIST - 2:CONV_HIST - 2 + tt, :] * w[0:1, :]
    z = z + uext[CONV_HIST - 1:CONV_HIST - 1 + tt, :] * w[1:2, :]
    z = z + u * w[2:3, :]
    ya_ref[...] = (cb_ref[...] * z).astype(BF16)
    cnew_ref[...] = uext[CONV_HIST + t_true - 2:CONV_HIST + t_true, :]
    uext[0:CONV_HIST, :] = uext[tt:tt + CONV_HIST, :]

    pin = pin_ref[...]
    pext[POOL_HIST16:POOL_HIST16 + tt, :] = pin
    pos = (pos0 + t * tt + lax.broadcasted_iota(jnp.int32, (tt, 1), 0)).astype(F32)
    for g, win in enumerate(POOL_WINDOWS):
        ls = slice(g * LANES, (g + 1) * LANES)
        acc = pin[:, ls]
        for i in range(1, win):
            acc = acc + pext[POOL_HIST16 - i:POOL_HIST16 - i + tt, ls]
        cnt = jnp.minimum(float(win), pos + 1.0)
        d = acc / cnt - pin[:, ls]
        y = jnp.dot(d.astype(BF16), pw_ref[g], preferred_element_type=F32) * ps_ref[:, ls]
        yd_ref[:, ls] = y.astype(BF16)
    pnew_ref[...] = pext[t_true:t_true + POOL_HIST16, :]
    pext[0:POOL_HIST16, :] = pext[tt:tt + POOL_HIST16, :]


def _seq_tiles(B, nt, with_slot):
    if not with_slot:
        return B, (lambda b, t: b * nt + t), (lambda b, t: b * nt + t), (lambda b: b)
    return (B + 1,
            lambda b, t: jnp.where(b < B, b * nt + t, B * nt - 1),
            lambda b, t: jnp.where(b < B, b * nt + t, B * nt),
            lambda b: jnp.minimum(b, B - 1))


def conv_pool(P, B, T, cbuf8, pbuf16, conv_w, pool_w, pool_scale, l, tt, t_true, pos0, slot=None):
    nt = T // tt
    slot_rows = slot[0].shape[0] if slot else 0
    rows_b, in_blk, out_blk, seq_blk = _seq_tiles(B, nt, bool(slot))

    def col(c):
        return pl.BlockSpec((tt, 512), lambda b, t: (in_blk(b, t), c // 512))

    slot_spec = pl.BlockSpec((slot_rows, 512), lambda b, t: (0, 0))
    return pl.pallas_call(
        functools.partial(_conv_pool_kernel, tt=tt, t_true=t_true, pos0=pos0, n_seq=B, slot_rows=slot_rows),
        grid=(rows_b, nt),
        in_specs=[
            col(COL_CB), col(COL_CC), col(COL_CH), col(COL_POOL),
            pl.BlockSpec((None, CONV_HIST, 512), lambda b, t: (seq_blk(b), 0, 0)),
            pl.BlockSpec((None, POOL_HIST16, 512), lambda b, t: (seq_blk(b), 0, 0)),
            pl.BlockSpec((None, 3, 512), lambda b, t: (l, 0, 0)),
            pl.BlockSpec((None, 4, LANES, LANES), lambda b, t: (l, 0, 0, 0)),
            pl.BlockSpec((None, 1, 512), lambda b, t: (l, 0, 0)),
        ] + ([slot_spec, slot_spec] if slot else []),
        out_specs=[
            pl.BlockSpec((tt, 512), lambda b, t: (out_blk(b, t), 0)),
            pl.BlockSpec((tt, 512), lambda b, t: (out_blk(b, t), 0)),
            pl.BlockSpec((None, 2, 512), lambda b, t: (seq_blk(b), 0, 0)),
            pl.BlockSpec((None, POOL_HIST16, 512), lambda b, t: (seq_blk(b), 0, 0)),
        ],
        out_shape=[
            jax.ShapeDtypeStruct((B * T + slot_rows, 512), BF16),
            jax.ShapeDtypeStruct((B * T + slot_rows, 512), BF16),
            jax.ShapeDtypeStruct((B, 2, 512), F32),
            jax.ShapeDtypeStruct((B, POOL_HIST16, 512), F32),
        ],
        scratch_shapes=[pltpu.VMEM((CONV_HIST + tt, 512), F32), pltpu.VMEM((POOL_HIST16 + tt, 512), F32)],
        compiler_params=_cparams(("arbitrary", "arbitrary")),
    )(P, P, P, P, cbuf8, pbuf16, conv_w, pool_w, pool_scale, *(slot or ()))


GLA_UNROLL = 4

def _gla_kernel(q_ref, k_ref, v_ref, r_ref, lr_ref, wa_ref, ba_ref, gn_ref, s0_ref, *rest,
                tt, C, c_true, n_seq, slot_rows):
    if slot_rows:
        yc_slot_ref, yc_ref = rest[:2]

        @pl.when((pl.program_id(0) == n_seq) & (pl.program_id(1) == 0))
        def _():
            yc_ref[0:slot_rows, :] = yc_slot_ref[...]

        pl.when(pl.program_id(0) < n_seq)(
            lambda: _gla_body(q_ref, k_ref, v_ref, r_ref, lr_ref, wa_ref, ba_ref, gn_ref, s0_ref, *rest[1:],
                              tt=tt, C=C, c_true=c_true))
    else:
        _gla_body(q_ref, k_ref, v_ref, r_ref, lr_ref, wa_ref, ba_ref, gn_ref, s0_ref, *rest,
                  tt=tt, C=C, c_true=c_true)


def _gla_body(q_ref, k_ref, v_ref, r_ref, lr_ref, wa_ref, ba_ref, gn_ref, s0_ref,
              yc_ref, sfin_ref, s_scr, b_scr, sprev_scr, *, tt, C, c_true):
    assert c_true == C or tt == C
    t = pl.program_id(1)

    @pl.when(t == 0)
    def _():
        s_scr[...] = s0_ref[...]

    row = lax.broadcasted_iota(jnp.int32, (C, 1), 0)
    lane = lax.broadcasted_iota(jnp.int32, (1, LANES), 1)
    head0 = lane < GLA_DK
    srow = lax.broadcasted_iota(jnp.int32, (LANES, 1), 0)
    eye = srow == lane
    gn = gn_ref[...]
    nchunk = tt // C

    def chunk_rows(c):
        return pl.ds(pl.multiple_of(c * C, C), C)

    def v_head(rows, h):
        v_h = v_ref[rows, h * LANES:(h + 1) * LANES]
        if c_true < C:
            v_h = jnp.where(row < c_true, v_h, 0.0)
        return v_h

    la = jnp.dot(lr_ref[...].astype(BF16), wa_ref[...], preferred_element_type=F32) + ba_ref[...]
    la = jax.nn.log_sigmoid(la) / GLA_TAU
    if c_true < C:
        la = jnp.where(row < c_true, la, 0.0)
    cb = 2 * C if nchunk % 2 == 0 else C
    bi = lax.broadcasted_iota(jnp.int32, (cb, 1), 0)
    bj = lax.broadcasted_iota(jnp.int32, (1, cb), 1)
    ones_tri = (((bi >= C) == (bj >= C)) & (bj <= bi)).astype(BF16)
    for blk in range(tt // cb):
        part = la[blk * cb:(blk + 1) * cb, :]
        acc = None
        for _ in range(3):
            term = part.astype(BF16)
            d = jnp.dot(ones_tri, term, preferred_element_type=F32)
            acc = d if acc is None else acc + d
            part = part - term.astype(F32)
        b_scr[blk * cb:(blk + 1) * cb, :] = acc

    def by_head(x):
        return jnp.concatenate([jnp.where(head0, x, 0.0), jnp.where(head0, 0.0, x)], axis=0)

    def v_pair(rows, p):
        return jnp.concatenate([v_head(rows, 2 * p), v_head(rows, 2 * p + 1)], axis=0).astype(BF16)

    def scan(c, carry):
        rows = chunk_rows(c)
        b = b_scr[rows, :]
        b_last = b[C - 1:C, :]
        kend = k_ref[rows, :] * jnp.exp(b_last - b)
        for p in range(2):
            ls = slice(p * LANES, (p + 1) * LANES)
            s_p = s_scr[ls, :]
            sprev_scr[c, ls, :] = s_p.astype(BF16)
            dec = jnp.exp(jnp.sum(jnp.where(eye, b_last[:, ls], 0.0), axis=1, keepdims=True))
            ds = lax.dot_general(by_head(kend[:, ls]).astype(BF16), v_pair(rows, p), (((0,), (0,)), ((), ())),
                                 preferred_element_type=F32)
            s_scr[ls, :] = dec * s_p + ds
        return carry

    lax.fori_loop(0, nchunk, scan, 0, unroll=min(GLA_UNROLL, nchunk))

    ri = lax.broadcasted_iota(jnp.int32, (2 * C, 1), 0)
    ci = lax.broadcasted_iota(jnp.int32, (1, 2 * C), 1)
    same_head_causal = ((ri >= C) == (ci >= C)) & ((ri % C) >= (ci % C))

    def emit(c, carry):
        rows = chunk_rows(c)
        b = b_scr[rows, :]
        qt = q_ref[rows, :] * (GLA_DK ** -0.5) * jnp.exp(b)
        kt = k_ref[rows, :] * jnp.exp(-b)
        for p in range(2):
            ls = slice(p * LANES, (p + 1) * LANES)
            qm = by_head(qt[:, ls]).astype(BF16)
            kt2 = jnp.concatenate([kt[:, ls], kt[:, ls]], axis=0).astype(BF16)
            a = jnp.where(same_head_causal, _dot_nt(qm, kt2), 0.0)
            o = jnp.dot(a.astype(BF16), v_pair(rows, p), preferred_element_type=F32)
            o = o + jnp.dot(qm, sprev_scr[c, ls, :], preferred_element_type=F32)
            on = _rms(o) * gn
            for hh in range(2):
                hs = slice((2 * p + hh) * LANES, (2 * p + hh + 1) * LANES)
                r_h = r_ref[rows, hs]
                yc_ref[rows, hs] = (on[hh * C:(hh + 1) * C, :] * (r_h * jax.nn.sigmoid(r_h))).astype(BF16)
        return carry

    lax.fori_loop(0, nchunk, emit, 0, unroll=min(GLA_UNROLL, nchunk))
    sfin_ref[...] = s_scr[...]


def gla(P, B, T, wa_pad, b_a, gla_norm, s0, l, tt, C, c_true, slot=None):
    nt = T // tt
    slot_rows = slot.shape[0] if slot is not None else 0
    rows_b, in_blk, out_blk, seq_blk = _seq_tiles(B, nt, slot is not None)
    return pl.pallas_call(
        functools.partial(_gla_kernel, tt=tt, C=C, c_true=c_true, n_seq=B, slot_rows=slot_rows),
        grid=(rows_b, nt),
        in_specs=[
            pl.BlockSpec((tt, 256), lambda b, t: (in_blk(b, t), COL_GQ // 256)),
            pl.BlockSpec((tt, 256), lambda b, t: (in_blk(b, t), COL_GK // 256)),
            pl.BlockSpec((tt, 512), lambda b, t: (in_blk(b, t), COL_GV // 512)),
            pl.BlockSpec((tt, 512), lambda b, t: (in_blk(b, t), COL_GR // 512)),
            pl.BlockSpec((tt, LANES), lambda b, t: (in_blk(b, t), COL_LR // LANES)),
            pl.BlockSpec((None, LANES, 256), lambda b, t: (l, 0, 0)),
            pl.BlockSpec((None, 1, 256), lambda b, t: (l, 0, 0)),
            pl.BlockSpec((None, 1, LANES), lambda b, t: (l, 0, 0)),
            pl.BlockSpec((None, 256, LANES), lambda b, t: (seq_blk(b), 0, 0)),
        ] + ([pl.BlockSpec((slot_rows, 512), lambda b, t: (0, 0))] if slot is not None else []),
        out_specs=[
            pl.BlockSpec((tt, 512), lambda b, t: (out_blk(b, t), 0)),
            pl.BlockSpec((None, 256, LANES), lambda b, t: (seq_blk(b), 0, 0)),
        ],
        out_shape=[jax.ShapeDtypeStruct((B * T + slot_rows, 512), BF16),
                   jax.ShapeDtypeStruct((B, 256, LANES), F32)],
        scratch_shapes=[pltpu.VMEM((256, LANES), F32), pltpu.VMEM((tt, 256), F32),
                        pltpu.VMEM((tt // C, 256, LANES), BF16)],
        compiler_params=_cparams(("arbitrary", "arbitrary")),
    )(P, P, P, P, P, wa_pad, b_a, gla_norm, s0, *(() if slot is None else (slot,)))


def _head_norm(x, gain, head0):
    x2 = x * x
    s0 = jnp.sum(jnp.where(head0, x2, 0.0), axis=-1, keepdims=True)
    s1 = jnp.sum(jnp.where(head0, 0.0, x2), axis=-1, keepdims=True)
    ms = jnp.where(head0, s0, s1) / float(HEAD_DIM)
    return x * lax.rsqrt(ms + EPS) * gain


def _head_norm_mxu(x, gain, same_head):
    x2 = x * x
    hi = x2.astype(BF16)
    lo = (x2 - hi.astype(F32)).astype(BF16)
    ss = jnp.dot(hi, same_head, preferred_element_type=F32) + jnp.dot(lo, same_head, preferred_element_type=F32)
    return x * lax.rsqrt(ss / float(HEAD_DIM) + EPS) * gain


NORM_ROWS = 512
ATTN_UNROLL = 8
QK_SCALE = 0.125


def _attn_prompt_kernel(q_ref, k_ref, v_ref, qg_ref, kg_ref, sl_ref, win_prev_ref,
                        o_ref, lse_ref, kv_ref, tmp, qd, kd, vd, od, ld, bias, *, S, dil, n_keep, head_base):
    del win_prev_ref
    msub = S // dil
    nb = msub // NK
    kstride = msub + NK
    lane = lax.broadcasted_iota(jnp.int32, (1, LANES), 1)
    head0 = lane < HEAD_DIM
    same_head = ((lax.broadcasted_iota(jnp.int32, (LANES, 1), 0) < HEAD_DIM) == head0).astype(BF16)

    def norm_into_tmp(src_ref, g_ref):
        def body(i, carry):
            rows = pl.ds(pl.multiple_of(i * NORM_ROWS, NORM_ROWS), NORM_ROWS)
            tmp[rows, :] = _head_norm_mxu(src_ref[rows, :], g_ref[...], same_head)
            return carry
        lax.fori_loop(0, S // NORM_ROWS, body, 0, unroll=2)

    norm_into_tmp(q_ref, qg_ref)
    for r in range(dil):
        qd[r * msub:(r + 1) * msub, :] = (tmp[pl.ds(r, msub, stride=dil), :] * QK_SCALE).astype(BF16)
    norm_into_tmp(k_ref, kg_ref)
    kv_ref[0] = tmp[S - n_keep:S, :].T
    kv_ref[1] = v_ref[S - n_keep:S, :].T
    zeros = jnp.zeros((NK, LANES), BF16)
    for r in range(dil):
        kd[r * kstride:r * kstride + NK, :] = zeros
        vd[r * kstride:r * kstride + NK, :] = zeros
        kd[r * kstride + NK:(r + 1) * kstride, :] = tmp[pl.ds(r, msub, stride=dil), :].astype(BF16)
        vd[r * kstride + NK:(r + 1) * kstride, :] = v_ref[pl.ds(r, msub, stride=dil), :].astype(BF16)

    qi = lax.broadcasted_iota(jnp.int32, (NK, 2 * NK), 0)
    ki = lax.broadcasted_iota(jnp.int32, (NK, 2 * NK), 1)
    dist = qi - ki + NK
    band = (dist >= 0) & (dist <= NK)
    alibi = (dist * dil).astype(F32)
    cur = ki >= NK
    for hh in range(2):
        slope = sl_ref[head_base + 2 * pl.program_id(1) + hh]
        bias[2 * hh] = jnp.where(band, -(slope * alibi), NEG)
        bias[2 * hh + 1] = jnp.where(band & cur, -(slope * alibi), NEG)

    def block(j, carry):
        r = j // nb
        n = j - r * nb
        first = jnp.where(n == 0, 1, 0)
        qb = pl.multiple_of(j * NK, NK)
        kb = pl.multiple_of(j * NK + r * NK, NK)
        q2 = qd[pl.ds(qb, NK), :]
        k2 = kd[pl.ds(kb, 2 * NK), :]
        v2 = vd[pl.ds(kb, 2 * NK), :]
        outs, lses = [], []
        for hh in range(2):
            hmask = head0 if hh == 0 else jnp.logical_not(head0)
            qm = jnp.where(hmask, q2, jnp.zeros_like(q2))
            s = lax.dot_general(qm, k2, (((1,), (1,)), ((), ())), preferred_element_type=F32)
            s = s + bias[2 * hh + first]
            m = jnp.max(s, axis=-1, keepdims=True)
            p = jnp.exp(s - m)
            lsum = jnp.sum(p, axis=-1, keepdims=True)
            outs.append(jnp.dot(p.astype(BF16), v2, preferred_element_type=F32) / lsum)
            lses.append(m + jnp.log(lsum))
        od[pl.ds(qb, NK), :] = jnp.where(head0, outs[0], outs[1])
        ld[pl.ds(qb, NK), :] = jnp.where(head0, lses[0], lses[1])
        return carry

    lax.fori_loop(0, dil * nb, block, 0, unroll=ATTN_UNROLL)

    for r in range(dil):
        o_ref[pl.ds(r, msub, stride=dil), :] = od[r * msub:(r + 1) * msub, :]
        lse_ref[pl.ds(r, msub, stride=dil), :] = ld[r * msub:(r + 1) * msub, :]


def attn_prompt(P, B, S, q_gain, k_gain, slopes, l, g, win_prev):
    window, dil = ATTN_GROUPS[g]
    n_keep = min(window, S)

    def col(c):
        return pl.BlockSpec((S, LANES), lambda b, p: (b, c // LANES + 2 * g + p))

    def par(stack):
        return pl.BlockSpec((None, 1, LANES), lambda b, p: (stack * 6 + 2 * g + p, 0, 0))

    return pl.pallas_call(
        functools.partial(_attn_prompt_kernel, S=S, dil=dil, n_keep=n_keep, head_base=4 * g),
        grid=(B, 2),
        in_specs=[col(COL_AQ), col(COL_AK), col(COL_AV), par(l), par(l), pl.BlockSpec(memory_space=pltpu.SMEM),
                  pl.BlockSpec(memory_space=pl.ANY)],
        out_specs=[
            pl.BlockSpec((None, S, LANES), lambda b, p: (b, 0, p)),
            pl.BlockSpec((None, S, LANES), lambda b, p: (b, 0, p)),
            pl.BlockSpec((None, 2, LANES, n_keep), lambda b, p: (l * B + b, 0, p, 0)),
        ],
        out_shape=[
            jax.ShapeDtypeStruct((B, S, 256), F32),
            jax.ShapeDtypeStruct((B, S, 256), F32),
            jax.ShapeDtypeStruct((DEPTH * B, 2, 256, n_keep), F32),
        ],
        input_output_aliases={6: 2},
        scratch_shapes=[
            pltpu.VMEM((S, LANES), F32),
            pltpu.VMEM((S, LANES), BF16),
            pltpu.VMEM((S + dil * NK, LANES), BF16),
            pltpu.VMEM((S + dil * NK, LANES), BF16),
            pltpu.VMEM((S, LANES), F32),
            pltpu.VMEM((S, LANES), F32),
            pltpu.VMEM((4, NK, 2 * NK), F32),
        ],
        compiler_params=_cparams(("parallel", "arbitrary")),
    )(P, P, P, q_gain, k_gain, slopes, win_prev)


T_PAD = 8


def _attn_sample_kernel(q_ref, k_ref, v_ref, qg_ref, kg_ref, sl_ref, cache_ref, win_prev_ref,
                        o_ref, lse_ref, cnew_ref, *, L, dil, t_true, head_base):
    del win_prev_ref
    lane = lax.broadcasted_iota(jnp.int32, (1, LANES), 1)
    head0 = lane < HEAD_DIM
    trow8 = lax.broadcasted_iota(jnp.int32, (T_PAD, LANES), 0)
    lane8 = lax.broadcasted_iota(jnp.int32, (T_PAD, LANES), 1)
    place = ((lane8 == LANES - t_true + trow8) & (trow8 < t_true)).astype(F32)
    is_new = lane >= LANES - t_true

    def rolled_with_new(rows, new):
        shifted = pltpu.roll(cache_ref[rows, :], L - t_true, 1)
        tail = lax.dot_general(new, place, (((0,), (0,)), ((), ())), preferred_element_type=F32,
                               precision=lax.Precision.HIGHEST)
        if L > LANES:
            cnew_ref[rows, 0:L - LANES] = shifted[:, 0:L - LANES]
        cnew_ref[rows, L - LANES:L] = jnp.where(is_new, tail, shifted[:, L - LANES:L])

    trow = lax.broadcasted_iota(jnp.int32, (T_PAD, 1), 0)
    scol = lax.broadcasted_iota(jnp.int32, (1, L), 1)
    dist = L + trow - scol
    valid_c = ((dist & (dil - 1)) == 0) & (dist <= NK * dil)
    alibi_c = dist.astype(F32)

    for p in range(2):
        ls = slice(p * LANES, (p + 1) * LANES)
        qn = _head_norm(q_ref[:, ls], qg_ref[:, ls], head0)
        kn = _head_norm(k_ref[:, ls], kg_ref[:, ls], head0)
        vn = v_ref[:, ls]
        krows = slice(p * LANES, (p + 1) * LANES)
        vrows = slice(256 + p * LANES, 256 + (p + 1) * LANES)
        rolled_with_new(krows, kn)
        rolled_with_new(vrows, vn)
        kc_t = cache_ref[krows, :].astype(BF16)
        vc_t = cache_ref[vrows, :].astype(BF16)
        knr = kn.astype(BF16).astype(F32)
        vnr = vn.astype(BF16).astype(F32)
        outs, lses = [], []
        for hh in range(2):
            hmask = head0 if hh == 0 else jnp.logical_not(head0)
            qm = jnp.where(hmask, qn, 0.0).astype(BF16)
            qmr = qm.astype(F32)
            slope = sl_ref[head_base + 2 * p + hh]
            sc = jnp.dot(qm, kc_t, preferred_element_type=F32)
            sc = sc / float(np.sqrt(HEAD_DIM)) - slope * alibi_c
            sc = jnp.where(valid_c, sc, NEG)
            m = jnp.max(sc, axis=-1, keepdims=True)
            sn = []
            for u in range(t_true):
                du = trow - u
                su = jnp.sum(qmr * knr[u:u + 1, :], axis=-1, keepdims=True) / float(np.sqrt(HEAD_DIM))
                su = su - slope * du.astype(F32)
                su = jnp.where((du >= 0) & ((du & (dil - 1)) == 0), su, NEG)
                sn.append(su)
                m = jnp.maximum(m, su)
            pc = jnp.exp(sc - m)
            lsum = jnp.sum(pc, axis=-1, keepdims=True)
            acc = _dot_nt(pc.astype(BF16), vc_t)
            for u in range(t_true):
                pu = jnp.exp(sn[u] - m)
                lsum = lsum + pu
                acc = acc + pu.astype(BF16).astype(F32) * vnr[u:u + 1, :]
            outs.append(acc / lsum)
            lses.append(m + jnp.log(lsum))
        o_ref[:, ls] = jnp.where(head0, outs[0], outs[1])
        lse_ref[:, ls] = jnp.where(head0, lses[0], lses[1])


def attn_sample(P, B, q_gain, k_gain, slopes, cache_t, l, g, t_true, win_prev):
    window, dil = ATTN_GROUPS[g]
    L = cache_t.shape[2]

    def col(c):
        return pl.BlockSpec((T_PAD, 256), lambda b: (b, c // 256 + g))

    def par(stack):
        return pl.BlockSpec((None, 1, 256), lambda b: (stack * 3 + g, 0, 0))

    return pl.pallas_call(
        functools.partial(_attn_sample_kernel, L=L, dil=dil, t_true=t_true, head_base=4 * g),
        grid=(B,),
        in_specs=[col(COL_AQ), col(COL_AK), col(COL_AV), par(l), par(l), pl.BlockSpec(memory_space=pltpu.SMEM),
                  pl.BlockSpec((None, 512, L), lambda b: (l * B + b, 0, 0)),
                  pl.BlockSpec(memory_space=pl.ANY)],
        out_specs=[
            pl.BlockSpec((None, T_PAD, 256), lambda b: (b, 0, 0)),
            pl.BlockSpec((None, T_PAD, 256), lambda b: (b, 0, 0)),
            pl.BlockSpec((None, 512, L), lambda b: (l * B + b, 0, 0)),
        ],
        out_shape=[
            jax.ShapeDtypeStruct((B, T_PAD, 256), F32),
            jax.ShapeDtypeStruct((B, T_PAD, 256), F32),
            jax.ShapeDtypeStruct((DEPTH * B, 512, L), F32),
        ],
        input_output_aliases={7: 2},
        compiler_params=_cparams(("parallel",)),
    )(P, P, P, q_gain, k_gain, slopes, cache_t, win_prev)


def _attn_merge_kernel(o0, l0, o1, l1, o2, l2, *rest, n_tiles, slot_rows):
    def merge_tile(y_ref):
        a0, a1, a2 = l0[...], l1[...], l2[...]
        m = jnp.maximum(jnp.maximum(a0, a1), a2)
        e0, e1, e2 = jnp.exp(a0 - m), jnp.exp(a1 - m), jnp.exp(a2 - m)
        den = e0 + e1 + e2
        y = (e0 / den) * o0[...] + (e1 / den) * o1[...] + (e2 / den) * o2[...]
        y_ref[...] = y.astype(BF16)

    if slot_rows:
        slot_ref, y_ref = rest
        pl.when(pl.program_id(0) < n_tiles)(lambda: merge_tile(y_ref))

        @pl.when(pl.program_id(0) == n_tiles)
        def _():
            y_ref[0:slot_rows, :] = slot_ref[...]
    else:
        merge_tile(*rest)


def attn_merge(parts, tm, slot=None):
    M = parts[0].shape[0]
    n_tiles = M // tm
    slot_rows = slot.shape[0] if slot is not None else 0
    spec = pl.BlockSpec((tm, 256), lambda m: (jnp.minimum(m, n_tiles - 1), 0))
    return pl.pallas_call(
        functools.partial(_attn_merge_kernel, n_tiles=n_tiles, slot_rows=slot_rows),
        grid=(n_tiles + (1 if slot_rows else 0),),
        in_specs=[spec] * 6 + ([pl.BlockSpec((slot_rows, 256), lambda m: (0, 0))] if slot_rows else []),
        out_specs=pl.BlockSpec((tm, 256), lambda m: (m, 0)),
        out_shape=jax.ShapeDtypeStruct((M + slot_rows, 256), BF16),
        compiler_params=_cparams(("arbitrary",)),
    )(*parts, *(() if slot is None else (slot,)))


GATE_SRC0 = 5904
assert GATE_SRC0 % SRC_ALIGN == 0 and D_MODEL % SRC_ALIGN == 0


def _merge_kernel(xn_ref, ya_ref, yb_ref, yc_ref, yd_ref, g0, g1, g2, g3, u0, u1, u2, u3, o_ref,
                  gs0, gs1, gs2, gs3, us0, us1, us2, us3, *, tiles, tm):
    branches = ((ya_ref, g0, u0, gs0, us0), (yb_ref, g1, u1, gs1, us1),
                (yc_ref, g2, u2, gs2, us2), (yd_ref, g3, u3, gs3, us3))
    m = pl.program_id(1)

    @pl.when(m == 0)
    def _():
        for _, g_ref, u_ref, g_s, u_s in branches:
            g_s[...] = g_ref[...].astype(BF16)
            u_s[...] = u_ref[...].astype(BF16)

    def rows_fn(rows):
        xn = xn_ref[rows, :]
        acc = None
        for y_ref, _, _, g_s, u_s in branches:
            gate = jax.nn.sigmoid(_dot_nt(xn, g_s[...]))
            term = gate * jnp.dot(y_ref[rows, :], u_s[...], preferred_element_type=F32)
            acc = term if acc is None else acc + term
        o_ref[rows, :] = acc.astype(BF16)

    _on_tile_rows(m, tiles, tm, rows_fn)


def merge(xn, ya, yb, yc, yd, w_in_t, ups, l, tm):
    M = xn.shape[0]
    tiles = _row_tiles(M, tm)
    tn = PREP_ROWS
    widths = (512, 256, 512, 512)

    def act(width):
        return pl.BlockSpec((tm, width), lambda n, m: (m, 0))

    def gate(b):
        base = (GATE_SRC0 + b * D_MODEL) // SRC_ALIGN
        return _w_in_rows(l, lambda n, m: base + n * (tn // SRC_ALIGN))

    def up(width):
        return pl.BlockSpec((None, width, tn), lambda n, m: (l, 0, n))

    return pl.pallas_call(
        functools.partial(_merge_kernel, tiles=tiles, tm=tm),
        grid=(D_MODEL // tn, tiles[0]),
        in_specs=[act(D_MODEL)] + [act(w) for w in widths] + [gate(b) for b in range(N_BRANCH)]
                 + [up(w) for w in widths],
        out_specs=pl.BlockSpec((tm, tn), lambda n, m: (m, n)),
        out_shape=jax.ShapeDtypeStruct((M, D_MODEL), BF16),
        scratch_shapes=[pltpu.VMEM((tn, D_MODEL), BF16)] * N_BRANCH + [pltpu.VMEM((w, tn), BF16) for w in widths],
        compiler_params=_cparams(("parallel", "arbitrary")),
    )(xn, ya, yb, yc, yd, w_in_t, w_in_t, w_in_t, w_in_t, *ups)


def _layer(x, xn, W, l, T, mixers):
    h = ffn_up(xn, W['ff1_gate'], W['ff1_up'], l, T['up_m'], T['up_n'])
    x, xn = matmul_res(x, h, W['ff1_down'], l, 0.5, T['down_m'], W['n_mix'], l)
    P = mix_in(xn, W['w_in_t'], l, T['mix_m'])
    (ya, yb, yc, yd), states = mixers(P, l)
    mg = merge(xn, ya, yb, yc, yd, W['w_in_t'], (W['up_a'], W['up_b'], W['up_c'], W['up_d']), l, T['merge_m'])
    x, xn = matmul_res(x, mg, W['w_out'], l, 1.0, T['out_m'], W['n_ff2'], l)
    h = ffn_up(xn, W['ff2_gate'], W['ff2_up'], l, T['up_m'], T['up_n'])
    if l + 1 < DEPTH:
        x, xn = matmul_res(x, h, W['ff2_down'], l, 0.5, T['down_m'], W['n_ff1'], l + 1)
    else:
        x, xn = matmul_res(x, h, W['ff2_down'], l, 0.5, T['down_m'], out_split=True)
    return x, xn, states


SAMPLE_SLOT = 128
TILES = dict(up_m=2080, up_n=512, mix_m=1040, down_m=256, out_m=512, merge_m=1040)


def kernel(x_prompt, x_sample, state_conv, cache_w128_kv, cache_w512_kv, cache_w2048_kv, state_gla, state_pool,
           norm_ff1, ff1_gate, ff1_up, ff1_down, norm_mix, w_in, conv_w, attn_q_gain, attn_k_gain,
           gla_w_a2, gla_b_a, gla_norm, pool_w, pool_scale, w_up_conv, w_up_attn, w_up_gla, w_up_pool, w_out,
           norm_ff2, ff2_gate, ff2_up, ff2_down):
    BP, S, _ = x_prompt.shape
    BS, TS, _ = x_sample.shape
    caches = (cache_w128_kv, cache_w512_kv, cache_w2048_kv)

    W = {
        'n_ff1': norm_ff1.reshape(DEPTH, 1, D_MODEL), 'n_mix': norm_mix.reshape(DEPTH, 1, D_MODEL),
        'n_ff2': norm_ff2.reshape(DEPTH, 1, D_MODEL),
        'ff1_gate': ff1_gate, 'ff1_up': ff1_up, 'ff1_down': ff1_down,
        'ff2_gate': ff2_gate, 'ff2_up': ff2_up, 'ff2_down': ff2_down,
        'w_in_t': jnp.transpose(w_in, (0, 2, 1)),
        'up_a': w_up_conv, 'up_b': w_up_attn, 'up_c': w_up_gla, 'up_d': w_up_pool, 'w_out': w_out,
    }
    wa_pad = jnp.pad(gla_w_a2, ((0, 0), (0, LANES - gla_w_a2.shape[1]), (0, 0))).astype(BF16)
    b_a = gla_b_a.reshape(DEPTH, 1, 256)
    gn = gla_norm.reshape(DEPTH, 1, LANES)
    pool_wb = pool_w.astype(BF16)
    pool_sc = pool_scale.reshape(DEPTH, 1, W_POOL)
    i = jnp.arange(1, N_ATTN_HEADS + 1, dtype=F32)
    slopes = jnp.exp2(-8.0 * i / N_ATTN_HEADS)
    qg6 = attn_q_gain.reshape(DEPTH * 6, 1, LANES)
    kg6 = attn_k_gain.reshape(DEPTH * 6, 1, LANES)
    qg3 = attn_q_gain.reshape(DEPTH * 3, 1, 256)
    kg3 = attn_k_gain.reshape(DEPTH * 3, 1, 256)

    zc = jnp.zeros((BP, CONV_HIST, W_CONV), F32)
    zp = jnp.zeros((BP, POOL_HIST16, W_POOL), F32)
    zs = jnp.zeros((BP, 256, LANES), F32)

    win_p = [jnp.zeros((DEPTH * BP, 2, 256, min(w, S)), F32) for w, _ in ATTN_GROUPS]
    win_s = [jnp.zeros((DEPTH * BS, 512, cc.shape[2]), F32) for cc in caches]

    MP, MS = BP * S, BS * TS
    assert MS <= SAMPLE_SLOT

    def prompt_mixers(P, l, slot):
        ya, yd, cnew, pnew = conv_pool(P, BP, S, zc, zp, conv_w, pool_wb, pool_sc, l, 512, 512, 0,
                                       slot=(slot[0], slot[3]))
        yc, sfin = gla(P, BP, S, wa_pad, b_a, gn, zs, l, 512, GLA_CHUNK, GLA_CHUNK, slot=slot[2])
        parts = []
        for g in range(3):
            o, lse, win_p[g] = attn_prompt(P, BP, S, qg6, kg6, slopes, l, g, win_p[g])
            parts += [o.reshape(MP, 256), lse.reshape(MP, 256)]
        yb = attn_merge(parts, 1024, slot=slot[1])
        return (ya, yb, yc, yd), (cnew, sfin.reshape(BP, 4, GLA_DK, LANES), pnew[:, 1:])

    sconv8 = jnp.pad(state_conv, ((0, 0), (0, 0), (CONV_HIST - 2, 0), (0, 0)))
    spool16 = jnp.pad(state_pool, ((0, 0), (0, 0), (1, 0), (0, 0)))
    sgla = state_gla.reshape(DEPTH, BS, 256, LANES)
    cviews = [jnp.transpose(cc, (0, 1, 3, 4, 5, 2)).reshape(DEPTH * BS, 512, cc.shape[2]) for cc in caches]

    def sample_mixers(P, l):
        MSP = BS * T_PAD
        Ps = jnp.pad(P[MP:MP + MS].reshape(BS, TS, N_MIX), ((0, 0), (0, T_PAD - TS), (0, 0))).reshape(MSP, N_MIX)
        ya, yd, cnew, pnew = conv_pool(Ps, BS, T_PAD, sconv8[l], spool16[l], conv_w, pool_wb, pool_sc, l,
                                       T_PAD, TS, PAST_LEN)
        yc, sfin = gla(Ps, BS, T_PAD, wa_pad, b_a, gn, sgla[l], l, T_PAD, T_PAD, TS)
        parts = []
        for g in range(3):
            o, lse, win_s[g] = attn_sample(Ps, BS, qg3, kg3, slopes, cviews[g], l, g, TS, win_s[g])
            parts += [o[:, :TS].reshape(MS, 256), lse[:, :TS].reshape(MS, 256)]
        yb = attn_merge(parts, MS)

        def slot(y, padded=True):
            if padded:
                y = y.reshape(BS, T_PAD, -1)[:, :TS].reshape(MS, -1)
            return jnp.pad(y, ((0, SAMPLE_SLOT - MS), (0, 0)))

        ys = (slot(ya), slot(yb, False), slot(yc), slot(yd))
        return ys, (cnew, sfin.reshape(BS, 4, GLA_DK, LANES), pnew[:, 1:])

    def mixers(P, l):
        ys_s, st_sample = sample_mixers(P, l)
        ys, st_prompt = prompt_mixers(P, l, ys_s)
        return ys, (st_prompt, st_sample)

    x = (x_prompt.reshape(MP, D_MODEL), jnp.pad(x_sample.reshape(MS, D_MODEL), ((0, SAMPLE_SLOT - MS), (0, 0))))
    xn = norm(x[0], x[1], W['n_ff1'], 0, 1024)
    st_p, st_s = [], []
    for l in range(DEPTH):
        x, xn, (sp, ss) = _layer(x, xn, W, l, TILES, mixers)
        st_p.append(sp)
        st_s.append(ss)
    yp, ys = x[0], x[1][:MS]

    def stack(sts, f):
        return jnp.stack([f(s) for s in sts])

    def window_out(buf, B):
        n = buf.shape[-1]
        return jnp.transpose(buf.reshape(DEPTH, B, 2, 4, HEAD_DIM, n), (0, 1, 5, 2, 3, 4))

    outs = [yp.reshape(BP, S, D_MODEL), ys.reshape(BS, TS, D_MODEL),
            stack(st_p, lambda s: s[0]), stack(st_s, lambda s: s[0])]
    for g in range(3):
        outs += [window_out(win_p[g], BP), window_out(win_s[g], BS)]
    outs += [stack(st_p, lambda s: s[1]), stack(st_s, lambda s: s[1]),
             stack(st_p, lambda s: s[2]), stack(st_s, lambda s: s[2])]
    return tuple(outs)
```

```python
import functools

import numpy as np
import jax
import jax.numpy as jnp
from jax import lax
from jax.experimental import pallas as pl
from jax.experimental.pallas import tpu as pltpu

BF16 = jnp.bfloat16
F32 = jnp.float32

D_MODEL = 2048
DEPTH = 4
PAST_LEN = 16384
D_FF = 5504
EPS = 1e-6
N_BRANCH = 4
W_CONV = 512
ATTN_GROUPS = ((128, 1), (512, 4), (2048, 16))
N_ATTN_HEADS = 12
HEAD_DIM = 64
GLA_DK = 64
GLA_TAU = 16.0
GLA_CHUNK = 64
POOL_WINDOWS = (2, 4, 8, 16)
W_POOL = 512
NK = 128
LANES = 128
NEG = -1e30

COL_CB, COL_CC, COL_CH, COL_GV, COL_GR, COL_POOL = 0, 512, 1024, 1536, 2048, 2560
COL_AQ, COL_AK, COL_AV = 3072, 3840, 4608
COL_GQ, COL_GK, COL_LR = 5376, 5632, 5888
N_MIX = 6144
N_IN = 14096
PREP_ROWS = 256
MIX_TILE_SRC = ([256 * j for j in range(6)] + [4352, 4608, 4864, 5120, 5392, 5648]
                + [1536 + 256 * j for j in range(9)] + [3840, 4096, 5376])
SRC_ALIGN = 16
assert all(r % SRC_ALIGN == 0 for r in MIX_TILE_SRC)

VMEM_LIMIT = 56 * 1024 * 1024


def _cparams(sem):
    return pltpu.CompilerParams(dimension_semantics=sem, vmem_limit_bytes=VMEM_LIMIT)


def _rms(x):
    return x * lax.rsqrt(jnp.mean(x * x, axis=-1, keepdims=True) + EPS)


def _row_tiles(M, tm):
    n = pl.cdiv(M, tm)
    return n, M - (n - 1) * tm


def _split_rows_specs(tm, n_tiles, slot):
    return (pl.BlockSpec((tm, D_MODEL), lambda m: (jnp.minimum(m, n_tiles - 2), 0)),
            pl.BlockSpec((slot, D_MODEL), lambda m: (0, 0)))


def _norm_kernel(xp_ref, xs_ref, g_ref, xn_ref, *, tiles, tm):
    n, tail = tiles
    m = pl.program_id(0)

    @pl.when(m < n - 1)
    def _():
        xn_ref[...] = (_rms(xp_ref[...]) * g_ref[...]).astype(BF16)

    @pl.when(m == n - 1)
    def _():
        xn_ref[0:tail, :] = (_rms(xs_ref[...]) * g_ref[...]).astype(BF16)


def norm(xp, xs, gain, l, tm):
    MP, slot = xp.shape[0], xs.shape[0]
    assert MP % tm == 0 and slot < tm
    tiles = (MP // tm + 1, slot)
    return pl.pallas_call(
        functools.partial(_norm_kernel, tiles=tiles, tm=tm),
        grid=(tiles[0],),
        in_specs=[*_split_rows_specs(tm, tiles[0], slot),
                  pl.BlockSpec((None, 1, D_MODEL), lambda m: (l, 0, 0))],
        out_specs=pl.BlockSpec((tm, D_MODEL), lambda m: (m, 0)),
        out_shape=jax.ShapeDtypeStruct((MP + slot, D_MODEL), BF16),
        compiler_params=_cparams(("parallel",)),
    )(xp, xs, gain)


SUB_ROWS = 208


def _sub_blocks(tm):
    assert tm % SUB_ROWS == 0
    return [slice(r0, r0 + SUB_ROWS) for r0 in range(0, tm, SUB_ROWS)]


def _ffn_up_kernel(xn_ref, wg_ref, wu_ref, h_ref, wg_s, wu_s, *, tm):
    @pl.when(pl.program_id(1) == 0)
    def _():
        wg_s[...] = wg_ref[...].astype(BF16)
        wu_s[...] = wu_ref[...].astype(BF16)

    for r in _sub_blocks(tm):
        xn = xn_ref[r, :]
        a = jnp.dot(xn, wg_s[...], preferred_element_type=F32)
        b = jnp.dot(xn, wu_s[...], preferred_element_type=F32)
        h_ref[r, :] = (a * jax.nn.sigmoid(a) * b).astype(BF16)


def ffn_up(xn, wg, wu, l, tm, tn):
    M = xn.shape[0]
    assert M % tm == 0
    return pl.pallas_call(
        functools.partial(_ffn_up_kernel, tm=tm),
        grid=(pl.cdiv(D_FF, tn), M // tm),
        in_specs=[
            pl.BlockSpec((tm, D_MODEL), lambda n, m: (m, 0)),
            pl.BlockSpec((None, D_MODEL, tn), lambda n, m: (l, 0, n)),
            pl.BlockSpec((None, D_MODEL, tn), lambda n, m: (l, 0, n)),
        ],
        out_specs=pl.BlockSpec((tm, tn), lambda n, m: (m, n)),
        out_shape=jax.ShapeDtypeStruct((M, D_FF), BF16),
        scratch_shapes=[pltpu.VMEM((D_MODEL, tn), BF16), pltpu.VMEM((D_MODEL, tn), BF16)],
        compiler_params=_cparams(("parallel", "arbitrary")),
    )(xn, wg, wu)


W_CHUNKS = 8


def _matmul_res_kernel(*refs, scale, with_norm, tiles, tm, x_split, out_split, ck):
    refs = list(refs)
    x_refs = [refs.pop(0) for _ in range(2 if x_split else 1)]
    a_ref, w_ref = refs.pop(0), refs.pop(0)
    g_ref = refs.pop(0) if with_norm else None
    o_refs = [refs.pop(0) for _ in range(2 if out_split else 1)]
    xn_ref = refs.pop(0) if with_norm else None
    w_s = refs.pop(0)
    n, tail = tiles
    step = pl.program_id(0)
    m = step - W_CHUNKS

    @pl.when(step < W_CHUNKS)
    def _():
        w_s[pl.ds(pl.multiple_of(step * ck, 16), ck), :] = w_ref[...].astype(BF16)

    def rows_fn(rows, last):
        x_ref = x_refs[-1] if last else x_refs[0]
        o_ref = o_refs[-1] if last else o_refs[0]
        y = x_ref[rows, :] + scale * jnp.dot(a_ref[rows, :], w_s[...], preferred_element_type=F32)
        if with_norm:
            xn_ref[rows, :] = (_rms(y) * g_ref[...]).astype(BF16)
        o_ref[rows, :] = y

    pl.when((m >= 0) & (m < n - 1))(lambda: rows_fn(slice(0, tm), False))
    pl.when(m == n - 1)(lambda: rows_fn(slice(0, tail), True))


def matmul_res(x, a, w, l, scale, tm, next_gain=None, next_l=0, out_split=False):
    M, K = a.shape
    with_norm = next_gain is not None
    x_split = isinstance(x, tuple)
    tiles = _row_tiles(M, tm)
    assert tiles[1] == SAMPLE_SLOT and (M - SAMPLE_SLOT) % tm == 0 and K % (16 * W_CHUNKS) == 0
    ck = K // W_CHUNKS

    def tile(step):
        return jnp.maximum(step - W_CHUNKS, 0)

    row = pl.BlockSpec((tm, D_MODEL), lambda s: (tile(s), 0))
    split = (pl.BlockSpec((tm, D_MODEL), lambda s: (jnp.minimum(tile(s), tiles[0] - 2), 0)),
             pl.BlockSpec((SAMPLE_SLOT, D_MODEL), lambda s: (0, 0)))
    in_specs = (list(split) if x_split else [row]) + [
        pl.BlockSpec((tm, K), lambda s: (tile(s), 0)),
        pl.BlockSpec((None, ck, D_MODEL), lambda s: (l, jnp.minimum(s, W_CHUNKS - 1), 0))]
    args = (list(x) if x_split else [x]) + [a, w]
    if out_split:
        out_specs = list(split)
        out_shape = [jax.ShapeDtypeStruct((M - SAMPLE_SLOT, D_MODEL), F32),
                     jax.ShapeDtypeStruct((SAMPLE_SLOT, D_MODEL), F32)]
    else:
        out_specs, out_shape = [row], [jax.ShapeDtypeStruct((M, D_MODEL), F32)]
    if with_norm:
        in_specs.append(pl.BlockSpec((None, 1, D_MODEL), lambda s: (next_l, 0, 0)))
        args.append(next_gain)
        out_specs.append(row)
        out_shape.append(jax.ShapeDtypeStruct((M, D_MODEL), BF16))
    res = pl.pallas_call(
        functools.partial(_matmul_res_kernel, scale=scale, with_norm=with_norm, tiles=tiles, tm=tm,
                          x_split=x_split, out_split=out_split, ck=ck),
        grid=(W_CHUNKS + tiles[0],),
        in_specs=in_specs,
        out_specs=out_specs,
        out_shape=out_shape,
        scratch_shapes=[pltpu.VMEM((K, D_MODEL), BF16)],
        compiler_params=_cparams(("arbitrary",)),
    )(*args)
    y = tuple(res[:2]) if out_split else res[0]
    return y, (res[-1] if with_norm else None)


def _dot_nt(a, b):
    return lax.dot_general(a, b, (((1,), (1,)), ((), ())), preferred_element_type=F32)


def _w_in_rows(l, row_of):
    return pl.BlockSpec((None, pl.Element(PREP_ROWS), pl.Element(D_MODEL)),
                        lambda *idx: (l, row_of(*idx) * SRC_ALIGN, 0))


MIX_SLABS = 4


def _mix_in_kernel(src_ref, xn_ref, *rest):
    del src_ref
    w_refs, (p_ref, w_s) = rest[:MIX_SLABS], rest[MIX_SLABS:]

    @pl.when(pl.program_id(1) == 0)
    def _():
        for j, w_ref in enumerate(w_refs):
            w_s[j * PREP_ROWS:(j + 1) * PREP_ROWS, :] = w_ref[...].astype(BF16)

    p_ref[...] = _dot_nt(xn_ref[...], w_s[...])


def mix_in(xn, w_in_t, l, tm):
    M = xn.shape[0]
    assert M % tm == 0 and w_in_t.shape[1] == N_IN
    tn = MIX_SLABS * PREP_ROWS
    src = jnp.asarray(MIX_TILE_SRC, jnp.int32) // SRC_ALIGN

    def slab(j):
        return _w_in_rows(l, lambda n, m, src: src[MIX_SLABS * n + j])

    return pl.pallas_call(
        _mix_in_kernel,
        grid_spec=pltpu.PrefetchScalarGridSpec(
            num_scalar_prefetch=1,
            grid=(N_MIX // tn, M // tm),
            in_specs=[pl.BlockSpec((tm, D_MODEL), lambda n, m, src: (m, 0))] + [slab(j) for j in range(MIX_SLABS)],
            out_specs=pl.BlockSpec((tm, tn), lambda n, m, src: (m, n)),
            scratch_shapes=[pltpu.VMEM((tn, D_MODEL), BF16)],
        ),
        out_shape=jax.ShapeDtypeStruct((M, N_MIX), F32),
        compiler_params=_cparams(("parallel", "arbitrary")),
    )(src, xn, *([w_in_t] * MIX_SLABS))


CONV_HIST = 8
POOL_HIST16 = 16


def _conv_pool_kernel(cb_ref, cc_ref, ch_ref, pin_ref, cbuf_ref, pbuf_ref, cw_ref, pw_ref, ps_ref, *rest,
                      tt, t_true, pos0, n_seq, slot_rows):
    if slot_rows:
        ya_slot_ref, yd_slot_ref, ya_ref, yd_ref = rest[:4]

        @pl.when((pl.program_id(0) == n_seq) & (pl.program_id(1) == 0))
        def _():
            ya_ref[0:slot_rows, :] = ya_slot_ref[...]
            yd_ref[0:slot_rows, :] = yd_slot_ref[...]

        pl.when(pl.program_id(0) < n_seq)(
            lambda: _conv_pool_body(cb_ref, cc_ref, ch_ref, pin_ref, cbuf_ref, pbuf_ref, cw_ref, pw_ref, ps_ref,
                                    *rest[2:], tt=tt, t_true=t_true, pos0=pos0))
    else:
        _conv_pool_body(cb_ref, cc_ref, ch_ref, pin_ref, cbuf_ref, pbuf_ref, cw_ref, pw_ref, ps_ref, *rest,
                        tt=tt, t_true=t_true, pos0=pos0)


def _conv_pool_body(cb_ref, cc_ref, ch_ref, pin_ref, cbuf_ref, pbuf_ref, cw_ref, pw_ref, ps_ref,
                    ya_ref, yd_ref, cnew_ref, pnew_ref, uext, pext, *, tt, t_true, pos0):
    t = pl.program_id(1)

    @pl.when(t == 0)
    def _():
        uext[0:CONV_HIST, :] = cbuf_ref[...]
        pext[0:POOL_HIST16, :] = pbuf_ref[...]

    u = cc_ref[...] * ch_ref[...]
    uext[CONV_HIST:CONV_HIST + tt, :] = u
    w = cw_ref[...]
    z = uext[CONV_HIST - 2:CONV_HIST - 2 + tt, :] * w[0:1, :]
    z = z + uext[CONV_HIST - 1:CONV_HIST - 1 + tt, :] * w[1:2, :]
    z = z + u * w[2:3, :]
    ya_ref[...] = (cb_ref[...] * z).astype(BF16)
    cnew_ref[...] = uext[CONV_HIST + t_true - 2:CONV_HIST + t_true, :]
    uext[0:CONV_HIST, :] = uext[tt:tt + CONV_HIST, :]

    pin = pin_ref[...]
    pext[POOL_HIST16:POOL_HIST16 + tt, :] = pin
    pos = (pos0 + t * tt + lax.broadcasted_iota(jnp.int32, (tt, 1), 0)).astype(F32)
    for g, win in enumerate(POOL_WINDOWS):
        ls = slice(g * LANES, (g + 1) * LANES)
        acc = pin[:, ls]
        for i in range(1, win):
            acc = acc + pext[POOL_HIST16 - i:POOL_HIST16 - i + tt, ls]
        cnt = jnp.minimum(float(win), pos + 1.0)
        d = acc / cnt - pin[:, ls]
        y = jnp.dot(d.astype(BF16), pw_ref[g], preferred_element_type=F32) * ps_ref[:, ls]
        yd_ref[:, ls] = y.astype(BF16)
    pnew_ref[...] = pext[t_true:t_true + POOL_HIST16, :]
    pext[0:POOL_HIST16, :] = pext[tt:tt + POOL_HIST16, :]


def _seq_tiles(B, nt, with_slot):
    if not with_slot:
        return B, (lambda b, t: b * nt + t), (lambda b, t: b * nt + t), (lambda b: b)
    return (B + 1,
            lambda b, t: jnp.where(b < B, b * nt + t, B * nt - 1),
            lambda b, t: jnp.where(b < B, b * nt + t, B * nt),
            lambda b: jnp.minimum(b, B - 1))


def conv_pool(P, B, T, cbuf8, pbuf16, conv_w, pool_w, pool_scale, l, tt, t_true, pos0, slot=None):
    nt = T // tt
    slot_rows = slot[0].shape[0] if slot else 0
    rows_b, in_blk, out_blk, seq_blk = _seq_tiles(B, nt, bool(slot))

    def col(c):
        return pl.BlockSpec((tt, 512), lambda b, t: (in_blk(b, t), c // 512))

    slot_spec = pl.BlockSpec((slot_rows, 512), lambda b, t: (0, 0))
    return pl.pallas_call(
        functools.partial(_conv_pool_kernel, tt=tt, t_true=t_true, pos0=pos0, n_seq=B, slot_rows=slot_rows),
        grid=(rows_b, nt),
        in_specs=[
            col(COL_CB), col(COL_CC), col(COL_CH), col(COL_POOL),
            pl.BlockSpec((None, CONV_HIST, 512), lambda b, t: (seq_blk(b), 0, 0)),
            pl.BlockSpec((None, POOL_HIST16, 512), lambda b, t: (seq_blk(b), 0, 0)),
            pl.BlockSpec((None, 3, 512), lambda b, t: (l, 0, 0)),
            pl.BlockSpec((None, 4, LANES, LANES), lambda b, t: (l, 0, 0, 0)),
            pl.BlockSpec((None, 1, 512), lambda b, t: (l, 0, 0)),
        ] + ([slot_spec, slot_spec] if slot else []),
        out_specs=[
            pl.BlockSpec((tt, 512), lambda b, t: (out_blk(b, t), 0)),
            pl.BlockSpec((tt, 512), lambda b, t: (out_blk(b, t), 0)),
            pl.BlockSpec((None, 2, 512), lambda b, t: (seq_blk(b), 0, 0)),
            pl.BlockSpec((None, POOL_HIST16, 512), lambda b, t: (seq_blk(b), 0, 0)),
        ],
        out_shape=[
            jax.ShapeDtypeStruct((B * T + slot_rows, 512), BF16),
            jax.ShapeDtypeStruct((B * T + slot_rows, 512), BF16),
            jax.ShapeDtypeStruct((B, 2, 512), F32),
            jax.ShapeDtypeStruct((B, POOL_HIST16, 512), F32),
        ],
        scratch_shapes=[pltpu.VMEM((CONV_HIST + tt, 512), F32), pltpu.VMEM((POOL_HIST16 + tt, 512), F32)],
        compiler_params=_cparams(("arbitrary", "arbitrary")),
    )(P, P, P, P, cbuf8, pbuf16, conv_w, pool_w, pool_scale, *(slot or ()))


GLA_UNROLL = 8

def _gla_kernel(q_ref, k_ref, v_ref, r_ref, lr_ref, wa_ref, ba_ref, gn_ref, s0_ref, *rest,
                tt, C, c_true, n_seq, slot_rows):
    if slot_rows:
        yc_slot_ref, yc_ref = rest[:2]

        @pl.when((pl.program_id(0) == n_seq) & (pl.program_id(1) == 0))
        def _():
            yc_ref[0:slot_rows, :] = yc_slot_ref[...]

        pl.when(pl.program_id(0) < n_seq)(
            lambda: _gla_body(q_ref, k_ref, v_ref, r_ref, lr_ref, wa_ref, ba_ref, gn_ref, s0_ref, *rest[1:],
                              tt=tt, C=C, c_true=c_true))
    else:
        _gla_body(q_ref, k_ref, v_ref, r_ref, lr_ref, wa_ref, ba_ref, gn_ref, s0_ref, *rest,
                  tt=tt, C=C, c_true=c_true)


def _gla_body(q_ref, k_ref, v_ref, r_ref, lr_ref, wa_ref, ba_ref, gn_ref, s0_ref,
              yc_ref, sfin_ref, s_scr, b_scr, sprev_scr, *, tt, C, c_true):
    assert c_true == C or tt == C
    t = pl.program_id(1)

    @pl.when(t == 0)
    def _():
        s_scr[...] = s0_ref[...]

    row = lax.broadcasted_iota(jnp.int32, (C, 1), 0)
    lane = lax.broadcasted_iota(jnp.int32, (1, LANES), 1)
    head0 = lane < GLA_DK
    srow = lax.broadcasted_iota(jnp.int32, (LANES, 1), 0)
    eye = srow == lane
    gn = gn_ref[...]
    nchunk = tt // C

    def chunk_rows(c):
        return pl.ds(pl.multiple_of(c * C, C), C)

    def v_head(rows, h):
        v_h = v_ref[rows, h * LANES:(h + 1) * LANES]
        if c_true < C:
            v_h = jnp.where(row < c_true, v_h, 0.0)
        return v_h

    la = jnp.dot(lr_ref[...].astype(BF16), wa_ref[...], preferred_element_type=F32) + ba_ref[...]
    la = jax.nn.log_sigmoid(la) / GLA_TAU
    if c_true < C:
        la = jnp.where(row < c_true, la, 0.0)
    cb = 2 * C if nchunk % 2 == 0 else C
    bi = lax.broadcasted_iota(jnp.int32, (cb, 1), 0)
    bj = lax.broadcasted_iota(jnp.int32, (1, cb), 1)
    ones_tri = (((bi >= C) == (bj >= C)) & (bj <= bi)).astype(BF16)
    for blk in range(tt // cb):
        part = la[blk * cb:(blk + 1) * cb, :]
        acc = None
        for _ in range(3):
            term = part.astype(BF16)
            d = jnp.dot(ones_tri, term, preferred_element_type=F32)
            acc = d if acc is None else acc + d
            part = part - term.astype(F32)
        b_scr[blk * cb:(blk + 1) * cb, :] = acc

    def by_head(x):
        return jnp.concatenate([jnp.where(head0, x, 0.0), jnp.where(head0, 0.0, x)], axis=0)

    def v_pair(rows, p):
        return jnp.concatenate([v_head(rows, 2 * p), v_head(rows, 2 * p + 1)], axis=0).astype(BF16)

    def scan(c, carry):
        rows = chunk_rows(c)
        b = b_scr[rows, :]
        b_last = b[C - 1:C, :]
        kend = k_ref[rows, :] * jnp.exp(b_last - b)
        for p in range(2):
            ls = slice(p * LANES, (p + 1) * LANES)
            s_p = s_scr[ls, :]
            sprev_scr[c, ls, :] = s_p.astype(BF16)
            dec = jnp.exp(jnp.sum(jnp.where(eye, b_last[:, ls], 0.0), axis=1, keepdims=True))
            ds = lax.dot_general(by_head(kend[:, ls]).astype(BF16), v_pair(rows, p), (((0,), (0,)), ((), ())),
                                 preferred_element_type=F32)
            s_scr[ls, :] = dec * s_p + ds
        return carry

    lax.fori_loop(0, nchunk, scan, 0, unroll=min(GLA_UNROLL, nchunk))

    ri = lax.broadcasted_iota(jnp.int32, (2 * C, 1), 0)
    ci = lax.broadcasted_iota(jnp.int32, (1, 2 * C), 1)
    same_head_causal = ((ri >= C) == (ci >= C)) & ((ri % C) >= (ci % C))

    def emit(c, carry):
        rows = chunk_rows(c)
        b = b_scr[rows, :]
        qt = q_ref[rows, :] * (GLA_DK ** -0.5) * jnp.exp(b)
        kt = k_ref[rows, :] * jnp.exp(-b)
        for p in range(2):
            ls = slice(p * LANES, (p + 1) * LANES)
            qm = by_head(qt[:, ls]).astype(BF16)
            kt2 = jnp.concatenate([kt[:, ls], kt[:, ls]], axis=0).astype(BF16)
            a = jnp.where(same_head_causal, _dot_nt(qm, kt2), 0.0)
            o = jnp.dot(a.astype(BF16), v_pair(rows, p), preferred_element_type=F32)
            o = o + jnp.dot(qm, sprev_scr[c, ls, :], preferred_element_type=F32)
            on = _rms(o) * gn
            for hh in range(2):
                hs = slice((2 * p + hh) * LANES, (2 * p + hh + 1) * LANES)
                r_h = r_ref[rows, hs]
                yc_ref[rows, hs] = (on[hh * C:(hh + 1) * C, :] * (r_h * jax.nn.sigmoid(r_h))).astype(BF16)
        return carry

    lax.fori_loop(0, nchunk, emit, 0, unroll=min(GLA_UNROLL, nchunk))
    sfin_ref[...] = s_scr[...]


def gla(P, B, T, wa_pad, b_a, gla_norm, s0, l, tt, C, c_true, slot=None):
    nt = T // tt
    slot_rows = slot.shape[0] if slot is not None else 0
    rows_b, in_blk, out_blk, seq_blk = _seq_tiles(B, nt, slot is not None)
    return pl.pallas_call(
        functools.partial(_gla_kernel, tt=tt, C=C, c_true=c_true, n_seq=B, slot_rows=slot_rows),
        grid=(rows_b, nt),
        in_specs=[
            pl.BlockSpec((tt, 256), lambda b, t: (in_blk(b, t), COL_GQ // 256)),
            pl.BlockSpec((tt, 256), lambda b, t: (in_blk(b, t), COL_GK // 256)),
            pl.BlockSpec((tt, 512), lambda b, t: (in_blk(b, t), COL_GV // 512)),
            pl.BlockSpec((tt, 512), lambda b, t: (in_blk(b, t), COL_GR // 512)),
            pl.BlockSpec((tt, LANES), lambda b, t: (in_blk(b, t), COL_LR // LANES)),
            pl.BlockSpec((None, LANES, 256), lambda b, t: (l, 0, 0)),
            pl.BlockSpec((None, 1, 256), lambda b, t: (l, 0, 0)),
            pl.BlockSpec((None, 1, LANES), lambda b, t: (l, 0, 0)),
            pl.BlockSpec((None, 256, LANES), lambda b, t: (seq_blk(b), 0, 0)),
        ] + ([pl.BlockSpec((slot_rows, 512), lambda b, t: (0, 0))] if slot is not None else []),
        out_specs=[
            pl.BlockSpec((tt, 512), lambda b, t: (out_blk(b, t), 0)),
            pl.BlockSpec((None, 256, LANES), lambda b, t: (seq_blk(b), 0, 0)),
        ],
        out_shape=[jax.ShapeDtypeStruct((B * T + slot_rows, 512), BF16),
                   jax.ShapeDtypeStruct((B, 256, LANES), F32)],
        scratch_shapes=[pltpu.VMEM((256, LANES), F32), pltpu.VMEM((tt, 256), F32),
                        pltpu.VMEM((tt // C, 256, LANES), BF16)],
        compiler_params=_cparams(("arbitrary", "arbitrary")),
    )(P, P, P, P, P, wa_pad, b_a, gla_norm, s0, *(() if slot is None else (slot,)))


def _head_norm(x, gain, head0):
    x2 = x * x
    s0 = jnp.sum(jnp.where(head0, x2, 0.0), axis=-1, keepdims=True)
    s1 = jnp.sum(jnp.where(head0, 0.0, x2), axis=-1, keepdims=True)
    ms = jnp.where(head0, s0, s1) / float(HEAD_DIM)
    return x * lax.rsqrt(ms + EPS) * gain


def _head_norm_mxu(x, gain, same_head):
    x2 = x * x
    hi = x2.astype(BF16)
    lo = (x2 - hi.astype(F32)).astype(BF16)
    ss = jnp.dot(hi, same_head, preferred_element_type=F32) + jnp.dot(lo, same_head, preferred_element_type=F32)
    return x * lax.rsqrt(ss / float(HEAD_DIM) + EPS) * gain


NORM_ROWS = 512
ATTN_UNROLL = 16
QK_SCALE = 0.125


def _attn_prompt_kernel(q_ref, k_ref, v_ref, qg_ref, kg_ref, sl_ref, win_prev_ref,
                        o_ref, lse_ref, kv_ref, tmp, qd, kd, vd, od, ld, bias, *, S, dil, n_keep, head_base):
    del win_prev_ref
    msub = S // dil
    nb = msub // NK
    kstride = msub + NK
    lane = lax.broadcasted_iota(jnp.int32, (1, LANES), 1)
    head0 = lane < HEAD_DIM
    same_head = ((lax.broadcasted_iota(jnp.int32, (LANES, 1), 0) < HEAD_DIM) == head0).astype(BF16)

    def norm_into_tmp(src_ref, g_ref):
        def body(i, carry):
            rows = pl.ds(pl.multiple_of(i * NORM_ROWS, NORM_ROWS), NORM_ROWS)
            tmp[rows, :] = _head_norm_mxu(src_ref[rows, :], g_ref[...], same_head)
            return carry
        lax.fori_loop(0, S // NORM_ROWS, body, 0, unroll=2)

    norm_into_tmp(q_ref, qg_ref)
    for r in range(dil):
        qd[r * msub:(r + 1) * msub, :] = (tmp[pl.ds(r, msub, stride=dil), :] * QK_SCALE).astype(BF16)
    norm_into_tmp(k_ref, kg_ref)
    kv_ref[0] = tmp[S - n_keep:S, :].T
    kv_ref[1] = v_ref[S - n_keep:S, :].T
    zeros = jnp.zeros((NK, LANES), BF16)
    for r in range(dil):
        kd[r * kstride:r * kstride + NK, :] = zeros
        vd[r * kstride:r * kstride + NK, :] = zeros
        kd[r * kstride + NK:(r + 1) * kstride, :] = tmp[pl.ds(r, msub, stride=dil), :].astype(BF16)
        vd[r * kstride + NK:(r + 1) * kstride, :] = v_ref[pl.ds(r, msub, stride=dil), :].astype(BF16)

    qi = lax.broadcasted_iota(jnp.int32, (NK, 2 * NK), 0)
    ki = lax.broadcasted_iota(jnp.int32, (NK, 2 * NK), 1)
    dist = qi - ki + NK
    band = (dist >= 0) & (dist <= NK)
    alibi = (dist * dil).astype(F32)
    cur = ki >= NK
    for hh in range(2):
        slope = sl_ref[head_base + 2 * pl.program_id(1) + hh]
        bias[2 * hh] = jnp.where(band, -(slope * alibi), NEG)
        bias[2 * hh + 1] = jnp.where(band & cur, -(slope * alibi), NEG)

    def block(j, carry):
        r = j // nb
        n = j - r * nb
        first = jnp.where(n == 0, 1, 0)
        qb = pl.multiple_of(j * NK, NK)
        kb = pl.multiple_of(j * NK + r * NK, NK)
        q2 = qd[pl.ds(qb, NK), :]
        k2 = kd[pl.ds(kb, 2 * NK), :]
        v2 = vd[pl.ds(kb, 2 * NK), :]
        outs, lses = [], []
        for hh in range(2):
            hmask = head0 if hh == 0 else jnp.logical_not(head0)
            qm = jnp.where(hmask, q2, jnp.zeros_like(q2))
            s = lax.dot_general(qm, k2, (((1,), (1,)), ((), ())), preferred_element_type=F32)
            s = s + bias[2 * hh + first]
            m = jnp.max(s, axis=-1, keepdims=True)
            p = jnp.exp(s - m)
            lsum = jnp.sum(p, axis=-1, keepdims=True)
            outs.append(jnp.dot(p.astype(BF16), v2, preferred_element_type=F32) / lsum)
            lses.append(m + jnp.log(lsum))
        od[pl.ds(qb, NK), :] = jnp.where(head0, outs[0], outs[1])
        ld[pl.ds(qb, NK), :] = jnp.where(head0, lses[0], lses[1])
        return carry

    lax.fori_loop(0, dil * nb, block, 0, unroll=ATTN_UNROLL)

    for r in range(dil):
        o_ref[pl.ds(r, msub, stride=dil), :] = od[r * msub:(r + 1) * msub, :]
        lse_ref[pl.ds(r, msub, stride=dil), :] = ld[r * msub:(r + 1) * msub, :]


def attn_prompt(P, B, S, q_gain, k_gain, slopes, l, g, win_prev):
    window, dil = ATTN_GROUPS[g]
    n_keep = min(window, S)

    def col(c):
        return pl.BlockSpec((S, LANES), lambda b, p: (b, c // LANES + 2 * g + p))

    def par(stack):
        return pl.BlockSpec((None, 1, LANES), lambda b, p: (stack * 6 + 2 * g + p, 0, 0))

    return pl.pallas_call(
        functools.partial(_attn_prompt_kernel, S=S, dil=dil, n_keep=n_keep, head_base=4 * g),
        grid=(B, 2),
        in_specs=[col(COL_AQ), col(COL_AK), col(COL_AV), par(l), par(l), pl.BlockSpec(memory_space=pltpu.SMEM),
                  pl.BlockSpec(memory_space=pl.ANY)],
        out_specs=[
            pl.BlockSpec((None, S, LANES), lambda b, p: (b, 0, p)),
            pl.BlockSpec((None, S, LANES), lambda b, p: (b, 0, p)),
            pl.BlockSpec((None, 2, LANES, n_keep), lambda b, p: (l * B + b, 0, p, 0)),
        ],
        out_shape=[
            jax.ShapeDtypeStruct((B, S, 256), F32),
            jax.ShapeDtypeStruct((B, S, 256), F32),
            jax.ShapeDtypeStruct((DEPTH * B, 2, 256, n_keep), F32),
        ],
        input_output_aliases={6: 2},
        scratch_shapes=[
            pltpu.VMEM((S, LANES), F32),
            pltpu.VMEM((S, LANES), BF16),
            pltpu.VMEM((S + dil * NK, LANES), BF16),
            pltpu.VMEM((S + dil * NK, LANES), BF16),
            pltpu.VMEM((S, LANES), F32),
            pltpu.VMEM((S, LANES), F32),
            pltpu.VMEM((4, NK, 2 * NK), F32),
        ],
        compiler_params=_cparams(("parallel", "arbitrary")),
    )(P, P, P, q_gain, k_gain, slopes, win_prev)


T_PAD = 8


def _attn_sample_kernel(q_ref, k_ref, v_ref, qg_ref, kg_ref, sl_ref, cache_ref, win_prev_ref,
                        o_ref, lse_ref, cnew_ref, *, L, dil, t_true, head_base):
    del win_prev_ref
    lane = lax.broadcasted_iota(jnp.int32, (1, LANES), 1)
    head0 = lane < HEAD_DIM
    trow8 = lax.broadcasted_iota(jnp.int32, (T_PAD, LANES), 0)
    lane8 = lax.broadcasted_iota(jnp.int32, (T_PAD, LANES), 1)
    place = ((lane8 == LANES - t_true + trow8) & (trow8 < t_true)).astype(F32)
    is_new = lane >= LANES - t_true

    def rolled_with_new(rows, new):
        shifted = pltpu.roll(cache_ref[rows, :], L - t_true, 1)
        tail = lax.dot_general(new, place, (((0,), (0,)), ((), ())), preferred_element_type=F32,
                               precision=lax.Precision.HIGHEST)
        if L > LANES:
            cnew_ref[rows, 0:L - LANES] = shifted[:, 0:L - LANES]
        cnew_ref[rows, L - LANES:L] = jnp.where(is_new, tail, shifted[:, L - LANES:L])

    trow = lax.broadcasted_iota(jnp.int32, (T_PAD, 1), 0)
    scol = lax.broadcasted_iota(jnp.int32, (1, L), 1)
    dist = L + trow - scol
    valid_c = ((dist & (dil - 1)) == 0) & (dist <= NK * dil)
    alibi_c = dist.astype(F32)

    for p in range(2):
        ls = slice(p * LANES, (p + 1) * LANES)
        qn = _head_norm(q_ref[:, ls], qg_ref[:, ls], head0)
        kn = _head_norm(k_ref[:, ls], kg_ref[:, ls], head0)
        vn = v_ref[:, ls]
        krows = slice(p * LANES, (p + 1) * LANES)
        vrows = slice(256 + p * LANES, 256 + (p + 1) * LANES)
        rolled_with_new(krows, kn)
        rolled_with_new(vrows, vn)
        kc_t = cache_ref[krows, :].astype(BF16)
        vc_t = cache_ref[vrows, :].astype(BF16)
        knr = kn.astype(BF16).astype(F32)
        vnr = vn.astype(BF16).astype(F32)
        outs, lses = [], []
        for hh in range(2):
            hmask = head0 if hh == 0 else jnp.logical_not(head0)
            qm = jnp.where(hmask, qn, 0.0).astype(BF16)
            qmr = qm.astype(F32)
            slope = sl_ref[head_base + 2 * p + hh]
            sc = jnp.dot(qm, kc_t, preferred_element_type=F32)
            sc = sc / float(np.sqrt(HEAD_DIM)) - slope * alibi_c
            sc = jnp.where(valid_c, sc, NEG)
            m = jnp.max(sc, axis=-1, keepdims=True)
            sn = []
            for u in range(t_true):
                du = trow - u
                su = jnp.sum(qmr * knr[u:u + 1, :], axis=-1, keepdims=True) / float(np.sqrt(HEAD_DIM))
                su = su - slope * du.astype(F32)
                su = jnp.where((du >= 0) & ((du & (dil - 1)) == 0), su, NEG)
                sn.append(su)
                m = jnp.maximum(m, su)
            pc = jnp.exp(sc - m)
            lsum = jnp.sum(pc, axis=-1, keepdims=True)
            acc = _dot_nt(pc.astype(BF16), vc_t)
            for u in range(t_true):
                pu = jnp.exp(sn[u] - m)
                lsum = lsum + pu
                acc = acc + pu.astype(BF16).astype(F32) * vnr[u:u + 1, :]
            outs.append(acc / lsum)
            lses.append(m + jnp.log(lsum))
        o_ref[:, ls] = jnp.where(head0, outs[0], outs[1])
        lse_ref[:, ls] = jnp.where(head0, lses[0], lses[1])


def attn_sample(P, B, q_gain, k_gain, slopes, cache_t, l, g, t_true, win_prev):
    window, dil = ATTN_GROUPS[g]
    L = cache_t.shape[2]

    def col(c):
        return pl.BlockSpec((T_PAD, 256), lambda b: (b, c // 256 + g))

    def par(stack):
        return pl.BlockSpec((None, 1, 256), lambda b: (stack * 3 + g, 0, 0))

    return pl.pallas_call(
        functools.partial(_attn_sample_kernel, L=L, dil=dil, t_true=t_true, head_base=4 * g),
        grid=(B,),
        in_specs=[col(COL_AQ), col(COL_AK), col(COL_AV), par(l), par(l), pl.BlockSpec(memory_space=pltpu.SMEM),
                  pl.BlockSpec((None, 512, L), lambda b: (l * B + b, 0, 0)),
                  pl.BlockSpec(memory_space=pl.ANY)],
        out_specs=[
            pl.BlockSpec((None, T_PAD, 256), lambda b: (b, 0, 0)),
            pl.BlockSpec((None, T_PAD, 256), lambda b: (b, 0, 0)),
            pl.BlockSpec((None, 512, L), lambda b: (l * B + b, 0, 0)),
        ],
        out_shape=[
            jax.ShapeDtypeStruct((B, T_PAD, 256), F32),
            jax.ShapeDtypeStruct((B, T_PAD, 256), F32),
            jax.ShapeDtypeStruct((DEPTH * B, 512, L), F32),
        ],
        input_output_aliases={7: 2},
        compiler_params=_cparams(("parallel",)),
    )(P, P, P, q_gain, k_gain, slopes, cache_t, win_prev)


def _attn_merge_kernel(o0, l0, o1, l1, o2, l2, *rest, n_tiles, slot_rows):
    def merge_tile(y_ref):
        a0, a1, a2 = l0[...], l1[...], l2[...]
        m = jnp.maximum(jnp.maximum(a0, a1), a2)
        e0, e1, e2 = jnp.exp(a0 - m), jnp.exp(a1 - m), jnp.exp(a2 - m)
        den = e0 + e1 + e2
        y = (e0 / den) * o0[...] + (e1 / den) * o1[...] + (e2 / den) * o2[...]
        y_ref[...] = y.astype(BF16)

    if slot_rows:
        slot_ref, y_ref = rest
        pl.when(pl.program_id(0) < n_tiles)(lambda: merge_tile(y_ref))

        @pl.when(pl.program_id(0) == n_tiles)
        def _():
            y_ref[0:slot_rows, :] = slot_ref[...]
    else:
        merge_tile(*rest)


def attn_merge(parts, tm, slot=None):
    M = parts[0].shape[0]
    n_tiles = M // tm
    slot_rows = slot.shape[0] if slot is not None else 0
    spec = pl.BlockSpec((tm, 256), lambda m: (jnp.minimum(m, n_tiles - 1), 0))
    return pl.pallas_call(
        functools.partial(_attn_merge_kernel, n_tiles=n_tiles, slot_rows=slot_rows),
        grid=(n_tiles + (1 if slot_rows else 0),),
        in_specs=[spec] * 6 + ([pl.BlockSpec((slot_rows, 256), lambda m: (0, 0))] if slot_rows else []),
        out_specs=pl.BlockSpec((tm, 256), lambda m: (m, 0)),
        out_shape=jax.ShapeDtypeStruct((M + slot_rows, 256), BF16),
        compiler_params=_cparams(("arbitrary",)),
    )(*parts, *(() if slot is None else (slot,)))


GATE_SRC0 = 5904
assert GATE_SRC0 % SRC_ALIGN == 0 and D_MODEL % SRC_ALIGN == 0


def _merge_kernel(xn_ref, ya_ref, yb_ref, yc_ref, yd_ref, g0, g1, g2, g3, u0, u1, u2, u3, o_ref,
                  gs0, gs1, gs2, gs3, us0, us1, us2, us3):
    branches = ((ya_ref, g0, u0, gs0, us0), (yb_ref, g1, u1, gs1, us1),
                (yc_ref, g2, u2, gs2, us2), (yd_ref, g3, u3, gs3, us3))

    @pl.when(pl.program_id(1) == 0)
    def _():
        for _, g_ref, u_ref, g_s, u_s in branches:
            g_s[...] = g_ref[...].astype(BF16)
            u_s[...] = u_ref[...].astype(BF16)

    xn = xn_ref[...]
    acc = None
    for y_ref, _, _, g_s, u_s in branches:
        gate = jax.nn.sigmoid(_dot_nt(xn, g_s[...]))
        term = gate * jnp.dot(y_ref[...], u_s[...], preferred_element_type=F32)
        acc = term if acc is None else acc + term
    o_ref[...] = acc.astype(BF16)


def merge(xn, ya, yb, yc, yd, w_in_t, ups, l, tm):
    M = xn.shape[0]
    assert M % tm == 0
    tn = PREP_ROWS
    widths = (512, 256, 512, 512)

    def act(width):
        return pl.BlockSpec((tm, width), lambda n, m: (m, 0))

    def gate(b):
        base = (GATE_SRC0 + b * D_MODEL) // SRC_ALIGN
        return _w_in_rows(l, lambda n, m: base + n * (tn // SRC_ALIGN))

    def up(width):
        return pl.BlockSpec((None, width, tn), lambda n, m: (l, 0, n))

    return pl.pallas_call(
        _merge_kernel,
        grid=(D_MODEL // tn, M // tm),
        in_specs=[act(D_MODEL)] + [act(w) for w in widths] + [gate(b) for b in range(N_BRANCH)]
                 + [up(w) for w in widths],
        out_specs=pl.BlockSpec((tm, tn), lambda n, m: (m, n)),
        out_shape=jax.ShapeDtypeStruct((M, D_MODEL), BF16),
        scratch_shapes=[pltpu.VMEM((tn, D_MODEL), BF16)] * N_BRANCH + [pltpu.VMEM((w, tn), BF16) for w in widths],
        compiler_params=_cparams(("parallel", "arbitrary")),
    )(xn, ya, yb, yc, yd, w_in_t, w_in_t, w_in_t, w_in_t, *ups)


def _layer(x, xn, W, l, T, mixers):
    h = ffn_up(xn, W['ff1_gate'], W['ff1_up'], l, T['up_m'], T['up_n'])
    x, xn = matmul_res(x, h, W['ff1_down'], l, 0.5, T['down_m'], W['n_mix'], l)
    P = mix_in(xn, W['w_in_t'], l, T['mix_m'])
    (ya, yb, yc, yd), states = mixers(P, l)
    mg = merge(xn, ya, yb, yc, yd, W['w_in_t'], (W['up_a'], W['up_b'], W['up_c'], W['up_d']), l, T['merge_m'])
    x, xn = matmul_res(x, mg, W['w_out'], l, 1.0, T['out_m'], W['n_ff2'], l)
    h = ffn_up(xn, W['ff2_gate'], W['ff2_up'], l, T['up_m'], T['up_n'])
    if l + 1 < DEPTH:
        x, xn = matmul_res(x, h, W['ff2_down'], l, 0.5, T['down_m'], W['n_ff1'], l + 1)
    else:
        x, xn = matmul_res(x, h, W['ff2_down'], l, 0.5, T['down_m'], out_split=True)
    return x, xn, states


SAMPLE_SLOT = 128
TILES = dict(up_m=2080, up_n=512, mix_m=1040, down_m=256, out_m=512, merge_m=1040)


def kernel(x_prompt, x_sample, state_conv, cache_w128_kv, cache_w512_kv, cache_w2048_kv, state_gla, state_pool,
           norm_ff1, ff1_gate, ff1_up, ff1_down, norm_mix, w_in, conv_w, attn_q_gain, attn_k_gain,
           gla_w_a2, gla_b_a, gla_norm, pool_w, pool_scale, w_up_conv, w_up_attn, w_up_gla, w_up_pool, w_out,
           norm_ff2, ff2_gate, ff2_up, ff2_down):
    BP, S, _ = x_prompt.shape
    BS, TS, _ = x_sample.shape
    caches = (cache_w128_kv, cache_w512_kv, cache_w2048_kv)

    W = {
        'n_ff1': norm_ff1.reshape(DEPTH, 1, D_MODEL), 'n_mix': norm_mix.reshape(DEPTH, 1, D_MODEL),
        'n_ff2': norm_ff2.reshape(DEPTH, 1, D_MODEL),
        'ff1_gate': ff1_gate, 'ff1_up': ff1_up, 'ff1_down': ff1_down,
        'ff2_gate': ff2_gate, 'ff2_up': ff2_up, 'ff2_down': ff2_down,
        'w_in_t': jnp.transpose(w_in, (0, 2, 1)),
        'up_a': w_up_conv, 'up_b': w_up_attn, 'up_c': w_up_gla, 'up_d': w_up_pool, 'w_out': w_out,
    }
    wa_pad = jnp.pad(gla_w_a2, ((0, 0), (0, LANES - gla_w_a2.shape[1]), (0, 0))).astype(BF16)
    b_a = gla_b_a.reshape(DEPTH, 1, 256)
    gn = gla_norm.reshape(DEPTH, 1, LANES)
    pool_wb = pool_w.astype(BF16)
    pool_sc = pool_scale.reshape(DEPTH, 1, W_POOL)
    i = jnp.arange(1, N_ATTN_HEADS + 1, dtype=F32)
    slopes = jnp.exp2(-8.0 * i / N_ATTN_HEADS)
    qg6 = attn_q_gain.reshape(DEPTH * 6, 1, LANES)
    kg6 = attn_k_gain.reshape(DEPTH * 6, 1, LANES)
    qg3 = attn_q_gain.reshape(DEPTH * 3, 1, 256)
    kg3 = attn_k_gain.reshape(DEPTH * 3, 1, 256)

    zc = jnp.zeros((BP, CONV_HIST, W_CONV), F32)
    zp = jnp.zeros((BP, POOL_HIST16, W_POOL), F32)
    zs = jnp.zeros((BP, 256, LANES), F32)

    win_p = [jnp.zeros((DEPTH * BP, 2, 256, min(w, S)), F32) for w, _ in ATTN_GROUPS]
    win_s = [jnp.zeros((DEPTH * BS, 512, cc.shape[2]), F32) for cc in caches]

    MP, MS = BP * S, BS * TS
    assert MS <= SAMPLE_SLOT

    def prompt_mixers(P, l, slot):
        ya, yd, cnew, pnew = conv_pool(P, BP, S, zc, zp, conv_w, pool_wb, pool_sc, l, 512, 512, 0,
                                       slot=(slot[0], slot[3]))
        yc, sfin = gla(P, BP, S, wa_pad, b_a, gn, zs, l, 512, GLA_CHUNK, GLA_CHUNK, slot=slot[2])
        parts = []
        for g in range(3):
            o, lse, win_p[g] = attn_prompt(P, BP, S, qg6, kg6, slopes, l, g, win_p[g])
            parts += [o.reshape(MP, 256), lse.reshape(MP, 256)]
        yb = attn_merge(parts, 1024, slot=slot[1])
        return (ya, yb, yc, yd), (cnew, sfin.reshape(BP, 4, GLA_DK, LANES), pnew[:, 1:])

    sconv8 = jnp.pad(state_conv, ((0, 0), (0, 0), (CONV_HIST - 2, 0), (0, 0)))
    spool16 = jnp.pad(state_pool, ((0, 0), (0, 0), (1, 0), (0, 0)))
    sgla = state_gla.reshape(DEPTH, BS, 256, LANES)
    cviews = [jnp.transpose(cc, (0, 1, 3, 4, 5, 2)).reshape(DEPTH * BS, 512, cc.shape[2]) for cc in caches]

    def sample_mixers(P, l):
        MSP = BS * T_PAD
        Ps = jnp.pad(P[MP:MP + MS].reshape(BS, TS, N_MIX), ((0, 0), (0, T_PAD - TS), (0, 0))).reshape(MSP, N_MIX)
        ya, yd, cnew, pnew = conv_pool(Ps, BS, T_PAD, sconv8[l], spool16[l], conv_w, pool_wb, pool_sc, l,
                                       T_PAD, TS, PAST_LEN)
        yc, sfin = gla(Ps, BS, T_PAD, wa_pad, b_a, gn, sgla[l], l, T_PAD, T_PAD, TS)
        parts = []
        for g in range(3):
            o, lse, win_s[g] = attn_sample(Ps, BS, qg3, kg3, slopes, cviews[g], l, g, TS, win_s[g])
            parts += [o[:, :TS].reshape(MS, 256), lse[:, :TS].reshape(MS, 256)]
        yb = attn_merge(parts, MS)

        def slot(y, padded=True):
            if padded:
                y = y.reshape(BS, T_PAD, -1)[:, :TS].reshape(MS, -1)
            return jnp.pad(y, ((0, SAMPLE_SLOT - MS), (0, 0)))

        ys = (slot(ya), slot(yb, False), slot(yc), slot(yd))
        return ys, (cnew, sfin.reshape(BS, 4, GLA_DK, LANES), pnew[:, 1:])

    def mixers(P, l):
        ys_s, st_sample = sample_mixers(P, l)
        ys, st_prompt = prompt_mixers(P, l, ys_s)
        return ys, (st_prompt, st_sample)

    x = (x_prompt.reshape(MP, D_MODEL), jnp.pad(x_sample.reshape(MS, D_MODEL), ((0, SAMPLE_SLOT - MS), (0, 0))))
    xn = norm(x[0], x[1], W['n_ff1'], 0, 1024)
    st_p, st_s = [], []
    for l in range(DEPTH):
        x, xn, (sp, ss) = _layer(x, xn, W, l, TILES, mixers)
        st_p.append(sp)
        st_s.append(ss)
    yp, ys = x[0], x[1][:MS]

    def stack(sts, f):
        return jnp.stack([f(s) for s in sts])

    def window_out(buf, B):
        n = buf.shape[-1]
        return jnp.transpose(buf.reshape(DEPTH, B, 2, 4, HEAD_DIM, n), (0, 1, 5, 2, 3, 4))

    outs = [yp.reshape(BP, S, D_MODEL), ys.reshape(BS, TS, D_MODEL),
            stack(st_p, lambda s: s[0]), stack(st_s, lambda s: s[0])]
    for g in range(3):
        outs += [window_out(win_p[g], BP), window_out(win_s[g], BS)]
    outs += [stack(st_p, lambda s: s[1]), stack(st_s, lambda s: s[1]),
             stack(st_p, lambda s: s[2]), stack(st_s, lambda s: s[2])]
    return tuple(outs)
```

```python
import functools

import numpy as np
import jax
import jax.numpy as jnp
from jax import lax
from jax.experimental import pallas as pl
from jax.experimental.pallas import tpu as pltpu

BF16 = jnp.bfloat16
F32 = jnp.float32

D_MODEL = 2048
DEPTH = 4
PAST_LEN = 16384
D_FF = 5504
EPS = 1e-6
N_BRANCH = 4
W_CONV = 512
ATTN_GROUPS = ((128, 1), (512, 4), (2048, 16))
N_ATTN_HEADS = 12
HEAD_DIM = 64
GLA_DK = 64
GLA_TAU = 16.0
GLA_CHUNK = 64
POOL_WINDOWS = (2, 4, 8, 16)
W_POOL = 512
NK = 128
LANES = 128
NEG = -1e30

COL_CB, COL_CC, COL_CH, COL_GV, COL_GR, COL_POOL = 0, 512, 1024, 1536, 2048, 2560
COL_AQ, COL_AK, COL_AV = 3072, 3840, 4608
COL_GQ, COL_GK, COL_LR = 5376, 5632, 5888
N_MIX = 6144
N_IN = 14096
PREP_ROWS = 256
MIX_TILE_SRC = ([256 * j for j in range(6)] + [4352, 4608, 4864, 5120, 5392, 5648]
                + [1536 + 256 * j for j in range(9)] + [3840, 4096, 5376])
SRC_ALIGN = 16
assert all(r % SRC_ALIGN == 0 for r in MIX_TILE_SRC)

VMEM_LIMIT = 56 * 1024 * 1024


def _cparams(sem):
    return pltpu.CompilerParams(dimension_semantics=sem, vmem_limit_bytes=VMEM_LIMIT)


def _rms(x):
    return x * lax.rsqrt(jnp.mean(x * x, axis=-1, keepdims=True) + EPS)


def _row_tiles(M, tm):
    n = pl.cdiv(M, tm)
    return n, M - (n - 1) * tm


def _split_rows_specs(tm, n_tiles, slot):
    return (pl.BlockSpec((tm, D_MODEL), lambda m: (jnp.minimum(m, n_tiles - 2), 0)),
            pl.BlockSpec((slot, D_MODEL), lambda m: (0, 0)))


def _norm_kernel(xp_ref, xs_ref, g_ref, xn_ref, *, tiles, tm):
    n, tail = tiles
    m = pl.program_id(0)

    @pl.when(m < n - 1)
    def _():
        xn_ref[...] = (_rms(xp_ref[...]) * g_ref[...]).astype(BF16)

    @pl.when(m == n - 1)
    def _():
        xn_ref[0:tail, :] = (_rms(xs_ref[...]) * g_ref[...]).astype(BF16)


def norm(xp, xs, gain, l, tm):
    MP, slot = xp.shape[0], xs.shape[0]
    assert MP % tm == 0 and slot < tm
    tiles = (MP // tm + 1, slot)
    return pl.pallas_call(
        functools.partial(_norm_kernel, tiles=tiles, tm=tm),
        grid=(tiles[0],),
        in_specs=[*_split_rows_specs(tm, tiles[0], slot),
                  pl.BlockSpec((None, 1, D_MODEL), lambda m: (l, 0, 0))],
        out_specs=pl.BlockSpec((tm, D_MODEL), lambda m: (m, 0)),
        out_shape=jax.ShapeDtypeStruct((MP + slot, D_MODEL), BF16),
        compiler_params=_cparams(("parallel",)),
    )(xp, xs, gain)


SUB_ROWS = 208


def _sub_blocks(tm):
    assert tm % SUB_ROWS == 0
    return [slice(r0, r0 + SUB_ROWS) for r0 in range(0, tm, SUB_ROWS)]


def _ffn_up_kernel(xn_ref, wg_ref, wu_ref, h_ref, wg_s, wu_s, *, tm):
    @pl.when(pl.program_id(1) == 0)
    def _():
        wg_s[...] = wg_ref[...].astype(BF16)
        wu_s[...] = wu_ref[...].astype(BF16)

    for r in _sub_blocks(tm):
        xn = xn_ref[r, :]
        a = jnp.dot(xn, wg_s[...], preferred_element_type=F32)
        b = jnp.dot(xn, wu_s[...], preferred_element_type=F32)
        h_ref[r, :] = (a * jax.nn.sigmoid(a) * b).astype(BF16)


def ffn_up(xn, wg, wu, l, tm, tn):
    M = xn.shape[0]
    assert M % tm == 0
    return pl.pallas_call(
        functools.partial(_ffn_up_kernel, tm=tm),
        grid=(pl.cdiv(D_FF, tn), M // tm),
        in_specs=[
            pl.BlockSpec((tm, D_MODEL), lambda n, m: (m, 0)),
            pl.BlockSpec((None, D_MODEL, tn), lambda n, m: (l, 0, n)),
            pl.BlockSpec((None, D_MODEL, tn), lambda n, m: (l, 0, n)),
        ],
        out_specs=pl.BlockSpec((tm, tn), lambda n, m: (m, n)),
        out_shape=jax.ShapeDtypeStruct((M, D_FF), BF16),
        scratch_shapes=[pltpu.VMEM((D_MODEL, tn), BF16), pltpu.VMEM((D_MODEL, tn), BF16)],
        compiler_params=_cparams(("parallel", "arbitrary")),
    )(xn, wg, wu)


W_CHUNKS = 8


def _matmul_res_kernel(*refs, scale, with_norm, tiles, tm, x_split, out_split, ck):
    refs = list(refs)
    x_refs = [refs.pop(0) for _ in range(2 if x_split else 1)]
    a_ref, w_ref = refs.pop(0), refs.pop(0)
    g_ref = refs.pop(0) if with_norm else None
    o_refs = [refs.pop(0) for _ in range(2 if out_split else 1)]
    xn_ref = refs.pop(0) if with_norm else None
    w_s = refs.pop(0)
    n, tail = tiles
    step = pl.program_id(0)
    m = step - W_CHUNKS

    @pl.when(step < W_CHUNKS)
    def _():
        w_s[pl.ds(pl.multiple_of(step * ck, 16), ck), :] = w_ref[...].astype(BF16)

    def rows_fn(rows, last):
        x_ref = x_refs[-1] if last else x_refs[0]
        o_ref = o_refs[-1] if last else o_refs[0]
        y = x_ref[rows, :] + scale * jnp.dot(a_ref[rows, :], w_s[...], preferred_element_type=F32)
        if with_norm:
            xn_ref[rows, :] = (_rms(y) * g_ref[...]).astype(BF16)
        o_ref[rows, :] = y

    pl.when((m >= 0) & (m < n - 1))(lambda: rows_fn(slice(0, tm), False))
    pl.when(m == n - 1)(lambda: rows_fn(slice(0, tail), True))


def matmul_res(x, a, w, l, scale, tm, next_gain=None, next_l=0, out_split=False):
    M, K = a.shape
    with_norm = next_gain is not None
    x_split = isinstance(x, tuple)
    tiles = _row_tiles(M, tm)
    assert tiles[1] == SAMPLE_SLOT and (M - SAMPLE_SLOT) % tm == 0 and K % (16 * W_CHUNKS) == 0
    ck = K // W_CHUNKS

    def tile(step):
        return jnp.maximum(step - W_CHUNKS, 0)

    row = pl.BlockSpec((tm, D_MODEL), lambda s: (tile(s), 0))
    split = (pl.BlockSpec((tm, D_MODEL), lambda s: (jnp.minimum(tile(s), tiles[0] - 2), 0)),
             pl.BlockSpec((SAMPLE_SLOT, D_MODEL), lambda s: (0, 0)))
    in_specs = (list(split) if x_split else [row]) + [
        pl.BlockSpec((tm, K), lambda s: (tile(s), 0)),
        pl.BlockSpec((None, ck, D_MODEL), lambda s: (l, jnp.minimum(s, W_CHUNKS - 1), 0))]
    args = (list(x) if x_split else [x]) + [a, w]
    if out_split:
        out_specs = list(split)
        out_shape = [jax.ShapeDtypeStruct((M - SAMPLE_SLOT, D_MODEL), F32),
                     jax.ShapeDtypeStruct((SAMPLE_SLOT, D_MODEL), F32)]
    else:
        out_specs, out_shape = [row], [jax.ShapeDtypeStruct((M, D_MODEL), F32)]
    if with_norm:
        in_specs.append(pl.BlockSpec((None, 1, D_MODEL), lambda s: (next_l, 0, 0)))
        args.append(next_gain)
        out_specs.append(row)
        out_shape.append(jax.ShapeDtypeStruct((M, D_MODEL), BF16))
    res = pl.pallas_call(
        functools.partial(_matmul_res_kernel, scale=scale, with_norm=with_norm, tiles=tiles, tm=tm,
                          x_split=x_split, out_split=out_split, ck=ck),
        grid=(W_CHUNKS + tiles[0],),
        in_specs=in_specs,
        out_specs=out_specs,
        out_shape=out_shape,
        scratch_shapes=[pltpu.VMEM((K, D_MODEL), BF16)],
        compiler_params=_cparams(("arbitrary",)),
    )(*args)
    y = tuple(res[:2]) if out_split else res[0]
    return y, (res[-1] if with_norm else None)


def _dot_nt(a, b):
    return lax.dot_general(a, b, (((1,), (1,)), ((), ())), preferred_element_type=F32)


def _w_in_rows(l, row_of):
    return pl.BlockSpec((None, pl.Element(PREP_ROWS), pl.Element(D_MODEL)),
                        lambda *idx: (l, row_of(*idx) * SRC_ALIGN, 0))


MIX_SLABS = 4


def _mix_in_kernel(src_ref, xn_ref, *rest):
    del src_ref
    w_refs, (p_ref, w_s) = rest[:MIX_SLABS], rest[MIX_SLABS:]

    @pl.when(pl.program_id(1) == 0)
    def _():
        for j, w_ref in enumerate(w_refs):
            w_s[j * PREP_ROWS:(j + 1) * PREP_ROWS, :] = w_ref[...].astype(BF16)

    p_ref[...] = _dot_nt(xn_ref[...], w_s[...])


def mix_in(xn, w_in_t, l, tm):
    M = xn.shape[0]
    assert M % tm == 0 and w_in_t.shape[1] == N_IN
    tn = MIX_SLABS * PREP_ROWS
    src = jnp.asarray(MIX_TILE_SRC, jnp.int32) // SRC_ALIGN

    def slab(j):
        return _w_in_rows(l, lambda n, m, src: src[MIX_SLABS * n + j])

    return pl.pallas_call(
        _mix_in_kernel,
        grid_spec=pltpu.PrefetchScalarGridSpec(
            num_scalar_prefetch=1,
            grid=(N_MIX // tn, M // tm),
            in_specs=[pl.BlockSpec((tm, D_MODEL), lambda n, m, src: (m, 0))] + [slab(j) for j in range(MIX_SLABS)],
            out_specs=pl.BlockSpec((tm, tn), lambda n, m, src: (m, n)),
            scratch_shapes=[pltpu.VMEM((tn, D_MODEL), BF16)],
        ),
        out_shape=jax.ShapeDtypeStruct((M, N_MIX), F32),
        compiler_params=_cparams(("parallel", "arbitrary")),
    )(src, xn, *([w_in_t] * MIX_SLABS))


CONV_HIST = 8
POOL_HIST16 = 16


def _conv_pool_kernel(cb_ref, cc_ref, ch_ref, pin_ref, cbuf_ref, pbuf_ref, cw_ref, pw_ref, ps_ref, *rest,
                      tt, t_true, pos0, n_seq, slot_rows):
    if slot_rows:
        ya_slot_ref, yd_slot_ref, ya_ref, yd_ref = rest[:4]

        @pl.when((pl.program_id(0) == n_seq) & (pl.program_id(1) == 0))
        def _():
            ya_ref[0:slot_rows, :] = ya_slot_ref[...]
            yd_ref[0:slot_rows, :] = yd_slot_ref[...]

        pl.when(pl.program_id(0) < n_seq)(
            lambda: _conv_pool_body(cb_ref, cc_ref, ch_ref, pin_ref, cbuf_ref, pbuf_ref, cw_ref, pw_ref, ps_ref,
                                    *rest[2:], tt=tt, t_true=t_true, pos0=pos0))
    else:
        _conv_pool_body(cb_ref, cc_ref, ch_ref, pin_ref, cbuf_ref, pbuf_ref, cw_ref, pw_ref, ps_ref, *rest,
                        tt=tt, t_true=t_true, pos0=pos0)


def _conv_pool_body(cb_ref, cc_ref, ch_ref, pin_ref, cbuf_ref, pbuf_ref, cw_ref, pw_ref, ps_ref,
                    ya_ref, yd_ref, cnew_ref, pnew_ref, uext, pext, *, tt, t_true, pos0):
    t = pl.program_id(1)

    @pl.when(t == 0)
    def _():
        uext[0:CONV_HIST, :] = cbuf_ref[...]
        pext[0:POOL_HIST16, :] = pbuf_ref[...]

    u = cc_ref[...] * ch_ref[...]
    uext[CONV_HIST:CONV_HIST + tt, :] = u
    w = cw_ref[...]
    z = uext[CONV_HIST - 2:CONV_HIST - 2 + tt, :] * w[0:1, :]
    z = z + uext[CONV_HIST - 1:CONV_HIST - 1 + tt, :] * w[1:2, :]
    z = z + u * w[2:3, :]
    ya_ref[...] = (cb_ref[...] * z).astype(BF16)
    cnew_ref[...] = uext[CONV_HIST + t_true - 2:CONV_HIST + t_true, :]
    uext[0:CONV_HIST, :] = uext[tt:tt + CONV_HIST, :]

    pin = pin_ref[...]
    pext[POOL_HIST16:POOL_HIST16 + tt, :] = pin
    pos = (pos0 + t * tt + lax.broadcasted_iota(jnp.int32, (tt, 1), 0)).astype(F32)
    for g, win in enumerate(POOL_WINDOWS):
        ls = slice(g * LANES, (g + 1) * LANES)
        acc = pin[:, ls]
        for i in range(1, win):
            acc = acc + pext[POOL_HIST16 - i:POOL_HIST16 - i + tt, ls]
        cnt = jnp.minimum(float(win), pos + 1.0)
        d = acc / cnt - pin[:, ls]
        y = jnp.dot(d.astype(BF16), pw_ref[g], preferred_element_type=F32) * ps_ref[:, ls]
        yd_ref[:, ls] = y.astype(BF16)
    pnew_ref[...] = pext[t_true:t_true + POOL_HIST16, :]
    pext[0:POOL_HIST16, :] = pext[tt:tt + POOL_HIST16, :]


def _seq_tiles(B, nt, with_slot):
    if not with_slot:
        return B, (lambda b, t: b * nt + t), (lambda b, t: b * nt + t), (lambda b: b)
    return (B + 1,
            lambda b, t: jnp.where(b < B, b * nt + t, B * nt - 1),
            lambda b, t: jnp.where(b < B, b * nt + t, B * nt),
            lambda b: jnp.minimum(b, B - 1))


def conv_pool(P, B, T, cbuf8, pbuf16, conv_w, pool_w, pool_scale, l, tt, t_true, pos0, slot=None):
    nt = T // tt
    slot_rows = slot[0].shape[0] if slot else 0
    rows_b, in_blk, out_blk, seq_blk = _seq_tiles(B, nt, bool(slot))

    def col(c):
        return pl.BlockSpec((tt, 512), lambda b, t: (in_blk(b, t), c // 512))

    slot_spec = pl.BlockSpec((slot_rows, 512), lambda b, t: (0, 0))
    return pl.pallas_call(
        functools.partial(_conv_pool_kernel, tt=tt, t_true=t_true, pos0=pos0, n_seq=B, slot_rows=slot_rows),
        grid=(rows_b, nt),
        in_specs=[
            col(COL_CB), col(COL_CC), col(COL_CH), col(COL_POOL),
            pl.BlockSpec((None, CONV_HIST, 512), lambda b, t: (seq_blk(b), 0, 0)),
            pl.BlockSpec((None, POOL_HIST16, 512), lambda b, t: (seq_blk(b), 0, 0)),
            pl.BlockSpec((None, 3, 512), lambda b, t: (l, 0, 0)),
            pl.BlockSpec((None, 4, LANES, LANES), lambda b, t: (l, 0, 0, 0)),
            pl.BlockSpec((None, 1, 512), lambda b, t: (l, 0, 0)),
        ] + ([slot_spec, slot_spec] if slot else []),
        out_specs=[
            pl.BlockSpec((tt, 512), lambda b, t: (out_blk(b, t), 0)),
            pl.BlockSpec((tt, 512), lambda b, t: (out_blk(b, t), 0)),
            pl.BlockSpec((None, 2, 512), lambda b, t: (seq_blk(b), 0, 0)),
            pl.BlockSpec((None, POOL_HIST16, 512), lambda b, t: (seq_blk(b), 0, 0)),
        ],
        out_shape=[
            jax.ShapeDtypeStruct((B * T + slot_rows, 512), BF16),
            jax.ShapeDtypeStruct((B * T + slot_rows, 512), BF16),
            jax.ShapeDtypeStruct((B, 2, 512), F32),
            jax.ShapeDtypeStruct((B, POOL_HIST16, 512), F32),
        ],
        scratch_shapes=[pltpu.VMEM((CONV_HIST + tt, 512), F32), pltpu.VMEM((POOL_HIST16 + tt, 512), F32)],
        compiler_params=_cparams(("arbitrary", "arbitrary")),
    )(P, P, P, P, cbuf8, pbuf16, conv_w, pool_w, pool_scale, *(slot or ()))


GLA_UNROLL = 8

def _gla_kernel(q_ref, k_ref, v_ref, r_ref, lr_ref, wa_ref, ba_ref, gn_ref, s0_ref, *rest,
                tt, C, c_true, n_seq, slot_rows):
    if slot_rows:
        yc_slot_ref, yc_ref = rest[:2]

        @pl.when((pl.program_id(0) == n_seq) & (pl.program_id(1) == 0))
        def _():
            yc_ref[0:slot_rows, :] = yc_slot_ref[...]

        pl.when(pl.program_id(0) < n_seq)(
            lambda: _gla_body(q_ref, k_ref, v_ref, r_ref, lr_ref, wa_ref, ba_ref, gn_ref, s0_ref, *rest[1:],
                              tt=tt, C=C, c_true=c_true))
    else:
        _gla_body(q_ref, k_ref, v_ref, r_ref, lr_ref, wa_ref, ba_ref, gn_ref, s0_ref, *rest,
                  tt=tt, C=C, c_true=c_true)


def _gla_body(q_ref, k_ref, v_ref, r_ref, lr_ref, wa_ref, ba_ref, gn_ref, s0_ref,
              yc_ref, sfin_ref, s_scr, b_scr, sprev_scr, *, tt, C, c_true):
    assert c_true == C or tt == C
    t = pl.program_id(1)

    @pl.when(t == 0)
    def _():
        s_scr[...] = s0_ref[...]

    row = lax.broadcasted_iota(jnp.int32, (C, 1), 0)
    lane = lax.broadcasted_iota(jnp.int32, (1, LANES), 1)
    head0 = lane < GLA_DK
    srow = lax.broadcasted_iota(jnp.int32, (LANES, 1), 0)
    eye = srow == lane
    gn = gn_ref[...]
    nchunk = tt // C

    def chunk_rows(c):
        return pl.ds(pl.multiple_of(c * C, C), C)

    def v_head(rows, h):
        v_h = v_ref[rows, h * LANES:(h + 1) * LANES]
        if c_true < C:
            v_h = jnp.where(row < c_true, v_h, 0.0)
        return v_h

    la = jnp.dot(lr_ref[...].astype(BF16), wa_ref[...], preferred_element_type=F32) + ba_ref[...]
    la = jax.nn.log_sigmoid(la) / GLA_TAU
    if c_true < C:
        la = jnp.where(row < c_true, la, 0.0)
    cb = 2 * C if nchunk % 2 == 0 else C
    bi = lax.broadcasted_iota(jnp.int32, (cb, 1), 0)
    bj = lax.broadcasted_iota(jnp.int32, (1, cb), 1)
    ones_tri = (((bi >= C) == (bj >= C)) & (bj <= bi)).astype(BF16)
    for blk in range(tt // cb):
        part = la[blk * cb:(blk + 1) * cb, :]
        acc = None
        for _ in range(3):
            term = part.astype(BF16)
            d = jnp.dot(ones_tri, term, preferred_element_type=F32)
            acc = d if acc is None else acc + d
            part = part - term.astype(F32)
        b_scr[blk * cb:(blk + 1) * cb, :] = acc

    def by_head(x):
        return jnp.concatenate([jnp.where(head0, x, 0.0), jnp.where(head0, 0.0, x)], axis=0)

    def v_pair(rows, p):
        return jnp.concatenate([v_head(rows, 2 * p), v_head(rows, 2 * p + 1)], axis=0).astype(BF16)

    def scan(c, carry):
        rows = chunk_rows(c)
        b = b_scr[rows, :]
        b_last = b[C - 1:C, :]
        kend = k_ref[rows, :] * jnp.exp(b_last - b)
        for p in range(2):
            ls = slice(p * LANES, (p + 1) * LANES)
            s_p = s_scr[ls, :]
            sprev_scr[c, ls, :] = s_p.astype(BF16)
            dec = jnp.exp(jnp.sum(jnp.where(eye, b_last[:, ls], 0.0), axis=1, keepdims=True))
            ds = lax.dot_general(by_head(kend[:, ls]).astype(BF16), v_pair(rows, p), (((0,), (0,)), ((), ())),
                                 preferred_element_type=F32)
            s_scr[ls, :] = dec * s_p + ds
        return carry

    lax.fori_loop(0, nchunk, scan, 0, unroll=min(GLA_UNROLL, nchunk))

    ri = lax.broadcasted_iota(jnp.int32, (2 * C, 1), 0)
    ci = lax.broadcasted_iota(jnp.int32, (1, 2 * C), 1)
    same_head_causal = ((ri >= C) == (ci >= C)) & ((ri % C) >= (ci % C))

    def emit(c, carry):
        rows = chunk_rows(c)
        b = b_scr[rows, :]
        qt = q_ref[rows, :] * (GLA_DK ** -0.5) * jnp.exp(b)
        kt = k_ref[rows, :] * jnp.exp(-b)
        for p in range(2):
            ls = slice(p * LANES, (p + 1) * LANES)
            qm = by_head(qt[:, ls]).astype(BF16)
            kt2 = jnp.concatenate([kt[:, ls], kt[:, ls]], axis=0).astype(BF16)
            a = jnp.where(same_head_causal, _dot_nt(qm, kt2), 0.0)
            o = jnp.dot(a.astype(BF16), v_pair(rows, p), preferred_element_type=F32)
            o = o + jnp.dot(qm, sprev_scr[c, ls, :], preferred_element_type=F32)
            on = _rms(o) * gn
            for hh in range(2):
                hs = slice((2 * p + hh) * LANES, (2 * p + hh + 1) * LANES)
                r_h = r_ref[rows, hs]
                yc_ref[rows, hs] = (on[hh * C:(hh + 1) * C, :] * (r_h * jax.nn.sigmoid(r_h))).astype(BF16)
        return carry

    lax.fori_loop(0, nchunk, emit, 0, unroll=min(GLA_UNROLL, nchunk))
    sfin_ref[...] = s_scr[...]


def gla(P, B, T, wa_pad, b_a, gla_norm, s0, l, tt, C, c_true, slot=None):
    nt = T // tt
    slot_rows = slot.shape[0] if slot is not None else 0
    rows_b, in_blk, out_blk, seq_blk = _seq_tiles(B, nt, slot is not None)
    return pl.pallas_call(
        functools.partial(_gla_kernel, tt=tt, C=C, c_true=c_true, n_seq=B, slot_rows=slot_rows),
        grid=(rows_b, nt),
        in_specs=[
            pl.BlockSpec((tt, 256), lambda b, t: (in_blk(b, t), COL_GQ // 256)),
            pl.BlockSpec((tt, 256), lambda b, t: (in_blk(b, t), COL_GK // 256)),
            pl.BlockSpec((tt, 512), lambda b, t: (in_blk(b, t), COL_GV // 512)),
            pl.BlockSpec((tt, 512), lambda b, t: (in_blk(b, t), COL_GR // 512)),
            pl.BlockSpec((tt, LANES), lambda b, t: (in_blk(b, t), COL_LR // LANES)),
            pl.BlockSpec((None, LANES, 256), lambda b, t: (l, 0, 0)),
            pl.BlockSpec((None, 1, 256), lambda b, t: (l, 0, 0)),
            pl.BlockSpec((None, 1, LANES), lambda b, t: (l, 0, 0)),
            pl.BlockSpec((None, 256, LANES), lambda b, t: (seq_blk(b), 0, 0)),
        ] + ([pl.BlockSpec((slot_rows, 512), lambda b, t: (0, 0))] if slot is not None else []),
        out_specs=[
            pl.BlockSpec((tt, 512), lambda b, t: (out_blk(b, t), 0)),
            pl.BlockSpec((None, 256, LANES), lambda b, t: (seq_blk(b), 0, 0)),
        ],
        out_shape=[jax.ShapeDtypeStruct((B * T + slot_rows, 512), BF16),
                   jax.ShapeDtypeStruct((B, 256, LANES), F32)],
        scratch_shapes=[pltpu.VMEM((256, LANES), F32), pltpu.VMEM((tt, 256), F32),
                        pltpu.VMEM((tt // C, 256, LANES), BF16)],
        compiler_params=_cparams(("arbitrary", "arbitrary")),
    )(P, P, P, P, P, wa_pad, b_a, gla_norm, s0, *(() if slot is None else (slot,)))


def _head_norm(x, gain, head0):
    x2 = x * x
    s0 = jnp.sum(jnp.where(head0, x2, 0.0), axis=-1, keepdims=True)
    s1 = jnp.sum(jnp.where(head0, 0.0, x2), axis=-1, keepdims=True)
    ms = jnp.where(head0, s0, s1) / float(HEAD_DIM)
    return x * lax.rsqrt(ms + EPS) * gain


def _head_norm_mxu(x, gain, same_head):
    x2 = x * x
    hi = x2.astype(BF16)
    lo = (x2 - hi.astype(F32)).astype(BF16)
    ss = jnp.dot(hi, same_head, preferred_element_type=F32) + jnp.dot(lo, same_head, preferred_element_type=F32)
    return x * lax.rsqrt(ss / float(HEAD_DIM) + EPS) * gain


NORM_ROWS = 512
ATTN_UNROLL = 32
QK_SCALE = 0.125


def _attn_prompt_kernel(q_ref, k_ref, v_ref, qg_ref, kg_ref, sl_ref, win_prev_ref,
                        o_ref, lse_ref, kv_ref, tmp, qd, kd, vd, od, ld, bias, *, S, dil, n_keep, head_base):
    del win_prev_ref
    msub = S // dil
    nb = msub // NK
    kstride = msub + NK
    lane = lax.broadcasted_iota(jnp.int32, (1, LANES), 1)
    head0 = lane < HEAD_DIM
    same_head = ((lax.broadcasted_iota(jnp.int32, (LANES, 1), 0) < HEAD_DIM) == head0).astype(BF16)

    def norm_into_tmp(src_ref, g_ref):
        def body(i, carry):
            rows = pl.ds(pl.multiple_of(i * NORM_ROWS, NORM_ROWS), NORM_ROWS)
            tmp[rows, :] = _head_norm_mxu(src_ref[rows, :], g_ref[...], same_head)
            return carry
        lax.fori_loop(0, S // NORM_ROWS, body, 0, unroll=4)

    norm_into_tmp(q_ref, qg_ref)
    for r in range(dil):
        qd[r * msub:(r + 1) * msub, :] = (tmp[pl.ds(r, msub, stride=dil), :] * QK_SCALE).astype(BF16)
    norm_into_tmp(k_ref, kg_ref)
    kv_ref[0] = tmp[S - n_keep:S, :].T
    kv_ref[1] = v_ref[S - n_keep:S, :].T
    zeros = jnp.zeros((NK, LANES), BF16)
    for r in range(dil):
        kd[r * kstride:r * kstride + NK, :] = zeros
        vd[r * kstride:r * kstride + NK, :] = zeros
        kd[r * kstride + NK:(r + 1) * kstride, :] = tmp[pl.ds(r, msub, stride=dil), :].astype(BF16)
        vd[r * kstride + NK:(r + 1) * kstride, :] = v_ref[pl.ds(r, msub, stride=dil), :].astype(BF16)

    qi = lax.broadcasted_iota(jnp.int32, (NK, 2 * NK), 0)
    ki = lax.broadcasted_iota(jnp.int32, (NK, 2 * NK), 1)
    dist = qi - ki + NK
    band = (dist >= 0) & (dist <= NK)
    alibi = (dist * dil).astype(F32)
    cur = ki >= NK
    for hh in range(2):
        slope = sl_ref[head_base + 2 * pl.program_id(1) + hh]
        bias[2 * hh] = jnp.where(band, -(slope * alibi), NEG)
        bias[2 * hh + 1] = jnp.where(band & cur, -(slope * alibi), NEG)

    def block(j, carry):
        r = j // nb
        n = j - r * nb
        first = jnp.where(n == 0, 1, 0)
        qb = pl.multiple_of(j * NK, NK)
        kb = pl.multiple_of(j * NK + r * NK, NK)
        q2 = qd[pl.ds(qb, NK), :]
        k2 = kd[pl.ds(kb, 2 * NK), :]
        v2 = vd[pl.ds(kb, 2 * NK), :]
        outs, lses = [], []
        for hh in range(2):
            hmask = head0 if hh == 0 else jnp.logical_not(head0)
            qm = jnp.where(hmask, q2, jnp.zeros_like(q2))
            s = lax.dot_general(qm, k2, (((1,), (1,)), ((), ())), preferred_element_type=F32)
            s = s + bias[2 * hh + first]
            m = jnp.max(s, axis=-1, keepdims=True)
            p = jnp.exp(s - m)
            lsum = jnp.sum(p, axis=-1, keepdims=True)
            outs.append(jnp.dot(p.astype(BF16), v2, preferred_element_type=F32) / lsum)
            lses.append(m + jnp.log(lsum))
        od[pl.ds(qb, NK), :] = jnp.where(head0, outs[0], outs[1])
        ld[pl.ds(qb, NK), :] = jnp.where(head0, lses[0], lses[1])
        return carry

    lax.fori_loop(0, dil * nb, block, 0, unroll=ATTN_UNROLL)

    for r in range(dil):
        o_ref[pl.ds(r, msub, stride=dil), :] = od[r * msub:(r + 1) * msub, :]
        lse_ref[pl.ds(r, msub, stride=dil), :] = ld[r * msub:(r + 1) * msub, :]


def attn_prompt(P, B, S, q_gain, k_gain, slopes, l, g, win_prev):
    window, dil = ATTN_GROUPS[g]
    n_keep = min(window, S)

    def col(c):
        return pl.BlockSpec((S, LANES), lambda b, p: (b, c // LANES + 2 * g + p))

    def par(stack):
        return pl.BlockSpec((None, 1, LANES), lambda b, p: (stack * 6 + 2 * g + p, 0, 0))

    return pl.pallas_call(
        functools.partial(_attn_prompt_kernel, S=S, dil=dil, n_keep=n_keep, head_base=4 * g),
        grid=(B, 2),
        in_specs=[col(COL_AQ), col(COL_AK), col(COL_AV), par(l), par(l), pl.BlockSpec(memory_space=pltpu.SMEM),
                  pl.BlockSpec(memory_space=pl.ANY)],
        out_specs=[
            pl.BlockSpec((None, S, LANES), lambda b, p: (b, 0, p)),
            pl.BlockSpec((None, S, LANES), lambda b, p: (b, 0, p)),
            pl.BlockSpec((None, 2, LANES, n_keep), lambda b, p: (l * B + b, 0, p, 0)),
        ],
        out_shape=[
            jax.ShapeDtypeStruct((B, S, 256), F32),
            jax.ShapeDtypeStruct((B, S, 256), F32),
            jax.ShapeDtypeStruct((DEPTH * B, 2, 256, n_keep), F32),
        ],
        input_output_aliases={6: 2},
        scratch_shapes=[
            pltpu.VMEM((S, LANES), F32),
            pltpu.VMEM((S, LANES), BF16),
            pltpu.VMEM((S + dil * NK, LANES), BF16),
            pltpu.VMEM((S + dil * NK, LANES), BF16),
            pltpu.VMEM((S, LANES), F32),
            pltpu.VMEM((S, LANES), F32),
            pltpu.VMEM((4, NK, 2 * NK), F32),
        ],
        compiler_params=_cparams(("parallel", "arbitrary")),
    )(P, P, P, q_gain, k_gain, slopes, win_prev)


T_PAD = 8


def _attn_sample_kernel(q_ref, k_ref, v_ref, qg_ref, kg_ref, sl_ref, cache_ref, win_prev_ref,
                        o_ref, lse_ref, cnew_ref, *, L, dil, t_true, head_base):
    del win_prev_ref
    lane = lax.broadcasted_iota(jnp.int32, (1, LANES), 1)
    head0 = lane < HEAD_DIM
    trow8 = lax.broadcasted_iota(jnp.int32, (T_PAD, LANES), 0)
    lane8 = lax.broadcasted_iota(jnp.int32, (T_PAD, LANES), 1)
    place = ((lane8 == LANES - t_true + trow8) & (trow8 < t_true)).astype(F32)
    is_new = lane >= LANES - t_true

    def rolled_with_new(rows, new):
        shifted = pltpu.roll(cache_ref[rows, :], L - t_true, 1)
        tail = lax.dot_general(new, place, (((0,), (0,)), ((), ())), preferred_element_type=F32,
                               precision=lax.Precision.HIGHEST)
        if L > LANES:
            cnew_ref[rows, 0:L - LANES] = shifted[:, 0:L - LANES]
        cnew_ref[rows, L - LANES:L] = jnp.where(is_new, tail, shifted[:, L - LANES:L])

    trow = lax.broadcasted_iota(jnp.int32, (T_PAD, 1), 0)
    scol = lax.broadcasted_iota(jnp.int32, (1, L), 1)
    dist = L + trow - scol
    valid_c = ((dist & (dil - 1)) == 0) & (dist <= NK * dil)
    alibi_c = dist.astype(F32)

    for p in range(2):
        ls = slice(p * LANES, (p + 1) * LANES)
        qn = _head_norm(q_ref[:, ls], qg_ref[:, ls], head0)
        kn = _head_norm(k_ref[:, ls], kg_ref[:, ls], head0)
        vn = v_ref[:, ls]
        krows = slice(p * LANES, (p + 1) * LANES)
        vrows = slice(256 + p * LANES, 256 + (p + 1) * LANES)
        rolled_with_new(krows, kn)
        rolled_with_new(vrows, vn)
        kc_t = cache_ref[krows, :].astype(BF16)
        vc_t = cache_ref[vrows, :].astype(BF16)
        knr = kn.astype(BF16).astype(F32)
        vnr = vn.astype(BF16).astype(F32)
        outs, lses = [], []
        for hh in range(2):
            hmask = head0 if hh == 0 else jnp.logical_not(head0)
            qm = jnp.where(hmask, qn, 0.0).astype(BF16)
            qmr = qm.astype(F32)
            slope = sl_ref[head_base + 2 * p + hh]
            sc = jnp.dot(qm, kc_t, preferred_element_type=F32)
            sc = sc / float(np.sqrt(HEAD_DIM)) - slope * alibi_c
            sc = jnp.where(valid_c, sc, NEG)
            m = jnp.max(sc, axis=-1, keepdims=True)
            sn = []
            for u in range(t_true):
                du = trow - u
                su = jnp.sum(qmr * knr[u:u + 1, :], axis=-1, keepdims=True) / float(np.sqrt(HEAD_DIM))
                su = su - slope * du.astype(F32)
                su = jnp.where((du >= 0) & ((du & (dil - 1)) == 0), su, NEG)
                sn.append(su)
                m = jnp.maximum(m, su)
            pc = jnp.exp(sc - m)
            lsum = jnp.sum(pc, axis=-1, keepdims=True)
            acc = _dot_nt(pc.astype(BF16), vc_t)
            for u in range(t_true):
                pu = jnp.exp(sn[u] - m)
                lsum = lsum + pu
                acc = acc + pu.astype(BF16).astype(F32) * vnr[u:u + 1, :]
            outs.append(acc / lsum)
            lses.append(m + jnp.log(lsum))
        o_ref[:, ls] = jnp.where(head0, outs[0], outs[1])
        lse_ref[:, ls] = jnp.where(head0, lses[0], lses[1])


def attn_sample(P, B, q_gain, k_gain, slopes, cache_t, l, g, t_true, win_prev):
    window, dil = ATTN_GROUPS[g]
    L = cache_t.shape[2]

    def col(c):
        return pl.BlockSpec((T_PAD, 256), lambda b: (b, c // 256 + g))

    def par(stack):
        return pl.BlockSpec((None, 1, 256), lambda b: (stack * 3 + g, 0, 0))

    return pl.pallas_call(
        functools.partial(_attn_sample_kernel, L=L, dil=dil, t_true=t_true, head_base=4 * g),
        grid=(B,),
        in_specs=[col(COL_AQ), col(COL_AK), col(COL_AV), par(l), par(l), pl.BlockSpec(memory_space=pltpu.SMEM),
                  pl.BlockSpec((None, 512, L), lambda b: (l * B + b, 0, 0)),
                  pl.BlockSpec(memory_space=pl.ANY)],
        out_specs=[
            pl.BlockSpec((None, T_PAD, 256), lambda b: (b, 0, 0)),
            pl.BlockSpec((None, T_PAD, 256), lambda b: (b, 0, 0)),
            pl.BlockSpec((None, 512, L), lambda b: (l * B + b, 0, 0)),
        ],
        out_shape=[
            jax.ShapeDtypeStruct((B, T_PAD, 256), F32),
            jax.ShapeDtypeStruct((B, T_PAD, 256), F32),
            jax.ShapeDtypeStruct((DEPTH * B, 512, L), F32),
        ],
        input_output_aliases={7: 2},
        compiler_params=_cparams(("parallel",)),
    )(P, P, P, q_gain, k_gain, slopes, cache_t, win_prev)


def _attn_merge_kernel(o0, l0, o1, l1, o2, l2, *rest, n_tiles, slot_rows):
    def merge_tile(y_ref):
        a0, a1, a2 = l0[...], l1[...], l2[...]
        m = jnp.maximum(jnp.maximum(a0, a1), a2)
        e0, e1, e2 = jnp.exp(a0 - m), jnp.exp(a1 - m), jnp.exp(a2 - m)
        den = e0 + e1 + e2
        y = (e0 / den) * o0[...] + (e1 / den) * o1[...] + (e2 / den) * o2[...]
        y_ref[...] = y.astype(BF16)

    if slot_rows:
        slot_ref, y_ref = rest
        pl.when(pl.program_id(0) < n_tiles)(lambda: merge_tile(y_ref))

        @pl.when(pl.program_id(0) == n_tiles)
        def _():
            y_ref[0:slot_rows, :] = slot_ref[...]
    else:
        merge_tile(*rest)


def attn_merge(parts, tm, slot=None):
    M = parts[0].shape[0]
    n_tiles = M // tm
    slot_rows = slot.shape[0] if slot is not None else 0
    spec = pl.BlockSpec((tm, 256), lambda m: (jnp.minimum(m, n_tiles - 1), 0))
    return pl.pallas_call(
        functools.partial(_attn_merge_kernel, n_tiles=n_tiles, slot_rows=slot_rows),
        grid=(n_tiles + (1 if slot_rows else 0),),
        in_specs=[spec] * 6 + ([pl.BlockSpec((slot_rows, 256), lambda m: (0, 0))] if slot_rows else []),
        out_specs=pl.BlockSpec((tm, 256), lambda m: (m, 0)),
        out_shape=jax.ShapeDtypeStruct((M + slot_rows, 256), BF16),
        compiler_params=_cparams(("arbitrary",)),
    )(*parts, *(() if slot is None else (slot,)))


GATE_SRC0 = 5904
assert GATE_SRC0 % SRC_ALIGN == 0 and D_MODEL % SRC_ALIGN == 0


def _merge_kernel(xn_ref, ya_ref, yb_ref, yc_ref, yd_ref, g0, g1, g2, g3, u0, u1, u2, u3, o_ref,
                  gs0, gs1, gs2, gs3, us0, us1, us2, us3):
    branches = ((ya_ref, g0, u0, gs0, us0), (yb_ref, g1, u1, gs1, us1),
                (yc_ref, g2, u2, gs2, us2), (yd_ref, g3, u3, gs3, us3))

    @pl.when(pl.program_id(1) == 0)
    def _():
        for _, g_ref, u_ref, g_s, u_s in branches:
            g_s[...] = g_ref[...].astype(BF16)
            u_s[...] = u_ref[...].astype(BF16)

    xn = xn_ref[...]
    acc = None
    for y_ref, _, _, g_s, u_s in branches:
        gate = jax.nn.sigmoid(_dot_nt(xn, g_s[...]))
        term = gate * jnp.dot(y_ref[...], u_s[...], preferred_element_type=F32)
        acc = term if acc is None else acc + term
    o_ref[...] = acc.astype(BF16)


def merge(xn, ya, yb, yc, yd, w_in_t, ups, l, tm):
    M = xn.shape[0]
    assert M % tm == 0
    tn = PREP_ROWS
    widths = (512, 256, 512, 512)

    def act(width):
        return pl.BlockSpec((tm, width), lambda n, m: (m, 0))

    def gate(b):
        base = (GATE_SRC0 + b * D_MODEL) // SRC_ALIGN
        return _w_in_rows(l, lambda n, m: base + n * (tn // SRC_ALIGN))

    def up(width):
        return pl.BlockSpec((None, width, tn), lambda n, m: (l, 0, n))

    return pl.pallas_call(
        _merge_kernel,
        grid=(D_MODEL // tn, M // tm),
        in_specs=[act(D_MODEL)] + [act(w) for w in widths] + [gate(b) for b in range(N_BRANCH)]
                 + [up(w) for w in widths],
        out_specs=pl.BlockSpec((tm, tn), lambda n, m: (m, n)),
        out_shape=jax.ShapeDtypeStruct((M, D_MODEL), BF16),
        scratch_shapes=[pltpu.VMEM((tn, D_MODEL), BF16)] * N_BRANCH + [pltpu.VMEM((w, tn), BF16) for w in widths],
        compiler_params=_cparams(("parallel", "arbitrary")),
    )(xn, ya, yb, yc, yd, w_in_t, w_in_t, w_in_t, w_in_t, *ups)


def _layer(x, xn, W, l, T, mixers):
    h = ffn_up(xn, W['ff1_gate'], W['ff1_up'], l, T['up_m'], T['up_n'])
    x, xn = matmul_res(x, h, W['ff1_down'], l, 0.5, T['down_m'], W['n_mix'], l)
    P = mix_in(xn, W['w_in_t'], l, T['mix_m'])
    (ya, yb, yc, yd), states = mixers(P, l)
    mg = merge(xn, ya, yb, yc, yd, W['w_in_t'], (W['up_a'], W['up_b'], W['up_c'], W['up_d']), l, T['merge_m'])
    x, xn = matmul_res(x, mg, W['w_out'], l, 1.0, T['out_m'], W['n_ff2'], l)
    h = ffn_up(xn, W['ff2_gate'], W['ff2_up'], l, T['up_m'], T['up_n'])
    if l + 1 < DEPTH:
        x, xn = matmul_res(x, h, W['ff2_down'], l, 0.5, T['down_m'], W['n_ff1'], l + 1)
    else:
        x, xn = matmul_res(x, h, W['ff2_down'], l, 0.5, T['down_m'], out_split=True)
    return x, xn, states


SAMPLE_SLOT = 128
TILES = dict(up_m=2080, up_n=512, mix_m=1040, down_m=256, out_m=512, merge_m=1040)


def kernel(x_prompt, x_sample, state_conv, cache_w128_kv, cache_w512_kv, cache_w2048_kv, state_gla, state_pool,
           norm_ff1, ff1_gate, ff1_up, ff1_down, norm_mix, w_in, conv_w, attn_q_gain, attn_k_gain,
           gla_w_a2, gla_b_a, gla_norm, pool_w, pool_scale, w_up_conv, w_up_attn, w_up_gla, w_up_pool, w_out,
           norm_ff2, ff2_gate, ff2_up, ff2_down):
    BP, S, _ = x_prompt.shape
    BS, TS, _ = x_sample.shape
    caches = (cache_w128_kv, cache_w512_kv, cache_w2048_kv)

    W = {
        'n_ff1': norm_ff1.reshape(DEPTH, 1, D_MODEL), 'n_mix': norm_mix.reshape(DEPTH, 1, D_MODEL),
        'n_ff2': norm_ff2.reshape(DEPTH, 1, D_MODEL),
        'ff1_gate': ff1_gate, 'ff1_up': ff1_up, 'ff1_down': ff1_down,
        'ff2_gate': ff2_gate, 'ff2_up': ff2_up, 'ff2_down': ff2_down,
        'w_in_t': jnp.transpose(w_in, (0, 2, 1)),
        'up_a': w_up_conv, 'up_b': w_up_attn, 'up_c': w_up_gla, 'up_d': w_up_pool, 'w_out': w_out,
    }
    wa_pad = jnp.pad(gla_w_a2, ((0, 0), (0, LANES - gla_w_a2.shape[1]), (0, 0))).astype(BF16)
    b_a = gla_b_a.reshape(DEPTH, 1, 256)
    gn = gla_norm.reshape(DEPTH, 1, LANES)
    pool_wb = pool_w.astype(BF16)
    pool_sc = pool_scale.reshape(DEPTH, 1, W_POOL)
    i = jnp.arange(1, N_ATTN_HEADS + 1, dtype=F32)
    slopes = jnp.exp2(-8.0 * i / N_ATTN_HEADS)
    qg6 = attn_q_gain.reshape(DEPTH * 6, 1, LANES)
    kg6 = attn_k_gain.reshape(DEPTH * 6, 1, LANES)
    qg3 = attn_q_gain.reshape(DEPTH * 3, 1, 256)
    kg3 = attn_k_gain.reshape(DEPTH * 3, 1, 256)

    zc = jnp.zeros((BP, CONV_HIST, W_CONV), F32)
    zp = jnp.zeros((BP, POOL_HIST16, W_POOL), F32)
    zs = jnp.zeros((BP, 256, LANES), F32)

    win_p = [jnp.zeros((DEPTH * BP, 2, 256, min(w, S)), F32) for w, _ in ATTN_GROUPS]
    win_s = [jnp.zeros((DEPTH * BS, 512, cc.shape[2]), F32) for cc in caches]

    MP, MS = BP * S, BS * TS
    assert MS <= SAMPLE_SLOT

    def prompt_mixers(P, l, slot):
        ya, yd, cnew, pnew = conv_pool(P, BP, S, zc, zp, conv_w, pool_wb, pool_sc, l, 512, 512, 0,
                                       slot=(slot[0], slot[3]))
        yc, sfin = gla(P, BP, S, wa_pad, b_a, gn, zs, l, 512, GLA_CHUNK, GLA_CHUNK, slot=slot[2])
        parts = []
        for g in range(3):
            o, lse, win_p[g] = attn_prompt(P, BP, S, qg6, kg6, slopes, l, g, win_p[g])
            parts += [o.reshape(MP, 256), lse.reshape(MP, 256)]
        yb = attn_merge(parts, 1024, slot=slot[1])
        return (ya, yb, yc, yd), (cnew, sfin.reshape(BP, 4, GLA_DK, LANES), pnew[:, 1:])

    sconv8 = jnp.pad(state_conv, ((0, 0), (0, 0), (CONV_HIST - 2, 0), (0, 0)))
    spool16 = jnp.pad(state_pool, ((0, 0), (0, 0), (1, 0), (0, 0)))
    sgla = state_gla.reshape(DEPTH, BS, 256, LANES)
    cviews = [jnp.transpose(cc, (0, 1, 3, 4, 5, 2)).reshape(DEPTH * BS, 512, cc.shape[2]) for cc in caches]

    def sample_mixers(P, l):
        MSP = BS * T_PAD
        Ps = jnp.pad(P[MP:MP + MS].reshape(BS, TS, N_MIX), ((0, 0), (0, T_PAD - TS), (0, 0))).reshape(MSP, N_MIX)
        ya, yd, cnew, pnew = conv_pool(Ps, BS, T_PAD, sconv8[l], spool16[l], conv_w, pool_wb, pool_sc, l,
                                       T_PAD, TS, PAST_LEN)
        yc, sfin = gla(Ps, BS, T_PAD, wa_pad, b_a, gn, sgla[l], l, T_PAD, T_PAD, TS)
        parts = []
        for g in range(3):
            o, lse, win_s[g] = attn_sample(Ps, BS, qg3, kg3, slopes, cviews[g], l, g, TS, win_s[g])
            parts += [o[:, :TS].reshape(MS, 256), lse[:, :TS].reshape(MS, 256)]
        yb = attn_merge(parts, MS)

        def slot(y, padded=True):
            if padded:
                y = y.reshape(BS, T_PAD, -1)[:, :TS].reshape(MS, -1)
            return jnp.pad(y, ((0, SAMPLE_SLOT - MS), (0, 0)))

        ys = (slot(ya), slot(yb, False), slot(yc), slot(yd))
        return ys, (cnew, sfin.reshape(BS, 4, GLA_DK, LANES), pnew[:, 1:])

    def mixers(P, l):
        ys_s, st_sample = sample_mixers(P, l)
        ys, st_prompt = prompt_mixers(P, l, ys_s)
        return ys, (st_prompt, st_sample)

    x = (x_prompt.reshape(MP, D_MODEL), jnp.pad(x_sample.reshape(MS, D_MODEL), ((0, SAMPLE_SLOT - MS), (0, 0))))
    xn = norm(x[0], x[1], W['n_ff1'], 0, 1024)
    st_p, st_s = [], []
    for l in range(DEPTH):
        x, xn, (sp, ss) = _layer(x, xn, W, l, TILES, mixers)
        st_p.append(sp)
        st_s.append(ss)
    yp, ys = x[0], x[1][:MS]

    def stack(sts, f):
        return jnp.stack([f(s) for s in sts])

    def window_out(buf, B):
        n = buf.shape[-1]
        return jnp.transpose(buf.reshape(DEPTH, B, 2, 4, HEAD_DIM, n), (0, 1, 5, 2, 3, 4))

    outs = [yp.reshape(BP, S, D_MODEL), ys.reshape(BS, TS, D_MODEL),
            stack(st_p, lambda s: s[0]), stack(st_s, lambda s: s[0])]
    for g in range(3):
        outs += [window_out(win_p[g], BP), window_out(win_s[g], BS)]
    outs += [stack(st_p, lambda s: s[1]), stack(st_s, lambda s: s[1]),
             stack(st_p, lambda s: s[2]), stack(st_s, lambda s: s[2])]
    return tuple(outs)
```

```python
import functools

import numpy as np
import jax
import jax.numpy as jnp
from jax import lax
from jax.experimental import pallas as pl
from jax.experimental.pallas import tpu as pltpu

BF16 = jnp.bfloat16
F32 = jnp.float32

D_MODEL = 2048
DEPTH = 4
PAST_LEN = 16384
D_FF = 5504
EPS = 1e-6
N_BRANCH = 4
W_CONV = 512
ATTN_GROUPS = ((128, 1), (512, 4), (2048, 16))
N_ATTN_HEADS = 12
HEAD_DIM = 64
GLA_DK = 64
GLA_TAU = 16.0
GLA_CHUNK = 64
POOL_WINDOWS = (2, 4, 8, 16)
W_POOL = 512
NK = 128
LANES = 128
NEG = -1e30

COL_CB, COL_CC, COL_CH, COL_GV, COL_GR, COL_POOL = 0, 512, 1024, 1536, 2048, 2560
COL_AQ, COL_AK, COL_AV = 3072, 3840, 4608
COL_GQ, COL_GK, COL_LR = 5376, 5632, 5888
N_MIX = 6144
N_IN = 14096
PREP_ROWS = 256
MIX_TILE_SRC = ([256 * j for j in range(6)] + [4352, 4608, 4864, 5120, 5392, 5648]
                + [1536 + 256 * j for j in range(9)] + [3840, 4096, 5376])
SRC_ALIGN = 16
assert all(r % SRC_ALIGN == 0 for r in MIX_TILE_SRC)

VMEM_LIMIT = 56 * 1024 * 1024


def _cparams(sem):
    return pltpu.CompilerParams(dimension_semantics=sem, vmem_limit_bytes=VMEM_LIMIT)


def _rms(x):
    return x * lax.rsqrt(jnp.mean(x * x, axis=-1, keepdims=True) + EPS)


def _row_tiles(M, tm):
    n = pl.cdiv(M, tm)
    return n, M - (n - 1) * tm


def _split_rows_specs(tm, n_tiles, slot):
    return (pl.BlockSpec((tm, D_MODEL), lambda m: (jnp.minimum(m, n_tiles - 2), 0)),
            pl.BlockSpec((slot, D_MODEL), lambda m: (0, 0)))


def _norm_kernel(xp_ref, xs_ref, g_ref, xn_ref, *, tiles, tm):
    n, tail = tiles
    m = pl.program_id(0)

    @pl.when(m < n - 1)
    def _():
        xn_ref[...] = (_rms(xp_ref[...]) * g_ref[...]).astype(BF16)

    @pl.when(m == n - 1)
    def _():
        xn_ref[0:tail, :] = (_rms(xs_ref[...]) * g_ref[...]).astype(BF16)


def norm(xp, xs, gain, l, tm):
    MP, slot = xp.shape[0], xs.shape[0]
    assert MP % tm == 0 and slot < tm
    tiles = (MP // tm + 1, slot)
    return pl.pallas_call(
        functools.partial(_norm_kernel, tiles=tiles, tm=tm),
        grid=(tiles[0],),
        in_specs=[*_split_rows_specs(tm, tiles[0], slot),
                  pl.BlockSpec((None, 1, D_MODEL), lambda m: (l, 0, 0))],
        out_specs=pl.BlockSpec((tm, D_MODEL), lambda m: (m, 0)),
        out_shape=jax.ShapeDtypeStruct((MP + slot, D_MODEL), BF16),
        compiler_params=_cparams(("parallel",)),
    )(xp, xs, gain)


SUB_ROWS = 208


def _sub_blocks(tm):
    assert tm % SUB_ROWS == 0
    return [slice(r0, r0 + SUB_ROWS) for r0 in range(0, tm, SUB_ROWS)]


def _ffn_up_kernel(xn_ref, wg_ref, wu_ref, h_ref, wg_s, wu_s, *, tm):
    @pl.when(pl.program_id(1) == 0)
    def _():
        wg_s[...] = wg_ref[...].astype(BF16)
        wu_s[...] = wu_ref[...].astype(BF16)

    for r in _sub_blocks(tm):
        xn = xn_ref[r, :]
        a = jnp.dot(xn, wg_s[...], preferred_element_type=F32)
        b = jnp.dot(xn, wu_s[...], preferred_element_type=F32)
        h_ref[r, :] = (a * jax.nn.sigmoid(a) * b).astype(BF16)


def ffn_up(xn, wg, wu, l, tm, tn):
    M = xn.shape[0]
    assert M % tm == 0
    return pl.pallas_call(
        functools.partial(_ffn_up_kernel, tm=tm),
        grid=(pl.cdiv(D_FF, tn), M // tm),
        in_specs=[
            pl.BlockSpec((tm, D_MODEL), lambda n, m: (m, 0)),
            pl.BlockSpec((None, D_MODEL, tn), lambda n, m: (l, 0, n)),
            pl.BlockSpec((None, D_MODEL, tn), lambda n, m: (l, 0, n)),
        ],
        out_specs=pl.BlockSpec((tm, tn), lambda n, m: (m, n)),
        out_shape=jax.ShapeDtypeStruct((M, D_FF), BF16),
        scratch_shapes=[pltpu.VMEM((D_MODEL, tn), BF16), pltpu.VMEM((D_MODEL, tn), BF16)],
        compiler_params=_cparams(("parallel", "arbitrary")),
    )(xn, wg, wu)


W_CHUNKS = 8


def _matmul_res_kernel(*refs, scale, with_norm, tiles, tm, x_split, out_split, ck):
    refs = list(refs)
    x_refs = [refs.pop(0) for _ in range(2 if x_split else 1)]
    a_ref, w_ref = refs.pop(0), refs.pop(0)
    g_ref = refs.pop(0) if with_norm else None
    o_refs = [refs.pop(0) for _ in range(2 if out_split else 1)]
    xn_ref = refs.pop(0) if with_norm else None
    w_s = refs.pop(0)
    n, tail = tiles
    step = pl.program_id(0)
    m = step - W_CHUNKS

    @pl.when(step < W_CHUNKS)
    def _():
        w_s[pl.ds(pl.multiple_of(step * ck, 16), ck), :] = w_ref[...].astype(BF16)

    def rows_fn(rows, last):
        x_ref = x_refs[-1] if last else x_refs[0]
        o_ref = o_refs[-1] if last else o_refs[0]
        y = x_ref[rows, :] + scale * jnp.dot(a_ref[rows, :], w_s[...], preferred_element_type=F32)
        if with_norm:
            xn_ref[rows, :] = (_rms(y) * g_ref[...]).astype(BF16)
        o_ref[rows, :] = y

    pl.when((m >= 0) & (m < n - 1))(lambda: rows_fn(slice(0, tm), False))
    pl.when(m == n - 1)(lambda: rows_fn(slice(0, tail), True))


def matmul_res(x, a, w, l, scale, tm, next_gain=None, next_l=0, out_split=False):
    M, K = a.shape
    with_norm = next_gain is not None
    x_split = isinstance(x, tuple)
    tiles = _row_tiles(M, tm)
    assert tiles[1] == SAMPLE_SLOT and (M - SAMPLE_SLOT) % tm == 0 and K % (16 * W_CHUNKS) == 0
    ck = K // W_CHUNKS

    def tile(step):
        return jnp.maximum(step - W_CHUNKS, 0)

    row = pl.BlockSpec((tm, D_MODEL), lambda s: (tile(s), 0))
    split = (pl.BlockSpec((tm, D_MODEL), lambda s: (jnp.minimum(tile(s), tiles[0] - 2), 0)),
             pl.BlockSpec((SAMPLE_SLOT, D_MODEL), lambda s: (0, 0)))
    in_specs = (list(split) if x_split else [row]) + [
        pl.BlockSpec((tm, K), lambda s: (tile(s), 0)),
        pl.BlockSpec((None, ck, D_MODEL), lambda s: (l, jnp.minimum(s, W_CHUNKS - 1), 0))]
    args = (list(x) if x_split else [x]) + [a, w]
    if out_split:
        out_specs = list(split)
        out_shape = [jax.ShapeDtypeStruct((M - SAMPLE_SLOT, D_MODEL), F32),
                     jax.ShapeDtypeStruct((SAMPLE_SLOT, D_MODEL), F32)]
    else:
        out_specs, out_shape = [row], [jax.ShapeDtypeStruct((M, D_MODEL), F32)]
    if with_norm:
        in_specs.append(pl.BlockSpec((None, 1, D_MODEL), lambda s: (next_l, 0, 0)))
        args.append(next_gain)
        out_specs.append(row)
        out_shape.append(jax.ShapeDtypeStruct((M, D_MODEL), BF16))
    res = pl.pallas_call(
        functools.partial(_matmul_res_kernel, scale=scale, with_norm=with_norm, tiles=tiles, tm=tm,
                          x_split=x_split, out_split=out_split, ck=ck),
        grid=(W_CHUNKS + tiles[0],),
        in_specs=in_specs,
        out_specs=out_specs,
        out_shape=out_shape,
        scratch_shapes=[pltpu.VMEM((K, D_MODEL), BF16)],
        compiler_params=_cparams(("arbitrary",)),
    )(*args)
    y = tuple(res[:2]) if out_split else res[0]
    return y, (res[-1] if with_norm else None)


def _dot_nt(a, b):
    return lax.dot_general(a, b, (((1,), (1,)), ((), ())), preferred_element_type=F32)


def _w_in_rows(l, row_of):
    return pl.BlockSpec((None, pl.Element(PREP_ROWS), pl.Element(D_MODEL)),
                        lambda *idx: (l, row_of(*idx) * SRC_ALIGN, 0))


MIX_SLABS = 4


def _mix_in_kernel(src_ref, xn_ref, *rest):
    del src_ref
    w_refs, (p_ref, w_s) = rest[:MIX_SLABS], rest[MIX_SLABS:]

    @pl.when(pl.program_id(1) == 0)
    def _():
        for j, w_ref in enumerate(w_refs):
            w_s[j * PREP_ROWS:(j + 1) * PREP_ROWS, :] = w_ref[...].astype(BF16)

    p_ref[...] = _dot_nt(xn_ref[...], w_s[...])


def mix_in(xn, w_in_t, l, tm):
    M = xn.shape[0]
    assert M % tm == 0 and w_in_t.shape[1] == N_IN
    tn = MIX_SLABS * PREP_ROWS
    src = jnp.asarray(MIX_TILE_SRC, jnp.int32) // SRC_ALIGN

    def slab(j):
        return _w_in_rows(l, lambda n, m, src: src[MIX_SLABS * n + j])

    return pl.pallas_call(
        _mix_in_kernel,
        grid_spec=pltpu.PrefetchScalarGridSpec(
            num_scalar_prefetch=1,
            grid=(N_MIX // tn, M // tm),
            in_specs=[pl.BlockSpec((tm, D_MODEL), lambda n, m, src: (m, 0))] + [slab(j) for j in range(MIX_SLABS)],
            out_specs=pl.BlockSpec((tm, tn), lambda n, m, src: (m, n)),
            scratch_shapes=[pltpu.VMEM((tn, D_MODEL), BF16)],
        ),
        out_shape=jax.ShapeDtypeStruct((M, N_MIX), F32),
        compiler_params=_cparams(("parallel", "arbitrary")),
    )(src, xn, *([w_in_t] * MIX_SLABS))


CONV_HIST = 8
POOL_HIST16 = 16


def _conv_pool_kernel(cb_ref, cc_ref, ch_ref, pin_ref, cbuf_ref, pbuf_ref, cw_ref, pw_ref, ps_ref, *rest,
                      tt, t_true, pos0, n_seq, slot_rows):
    if slot_rows:
        ya_slot_ref, yd_slot_ref, ya_ref, yd_ref = rest[:4]

        @pl.when((pl.program_id(0) == n_seq) & (pl.program_id(1) == 0))
        def _():
            ya_ref[0:slot_rows, :] = ya_slot_ref[...]
            yd_ref[0:slot_rows, :] = yd_slot_ref[...]

        pl.when(pl.program_id(0) < n_seq)(
            lambda: _conv_pool_body(cb_ref, cc_ref, ch_ref, pin_ref, cbuf_ref, pbuf_ref, cw_ref, pw_ref, ps_ref,
                                    *rest[2:], tt=tt, t_true=t_true, pos0=pos0))
    else:
        _conv_pool_body(cb_ref, cc_ref, ch_ref, pin_ref, cbuf_ref, pbuf_ref, cw_ref, pw_ref, ps_ref, *rest,
                        tt=tt, t_true=t_true, pos0=pos0)


def _conv_pool_body(cb_ref, cc_ref, ch_ref, pin_ref, cbuf_ref, pbuf_ref, cw_ref, pw_ref, ps_ref,
                    ya_ref, yd_ref, cnew_ref, pnew_ref, uext, pext, *, tt, t_true, pos0):
    t = pl.program_id(1)

    @pl.when(t == 0)
    def _():
        uext[0:CONV_HIST, :] = cbuf_ref[...]
        pext[0:POOL_HIST16, :] = pbuf_ref[...]

    u = cc_ref[...] * ch_ref[...]
    uext[CONV_HIST:CONV_HIST + tt, :] = u
    w = cw_ref[...]
    z = uext[CONV_HIST - 2:CONV_HIST - 2 + tt, :] * w[0:1, :]
    z = z + uext[CONV_HIST - 1:CONV_HIST - 1 + tt, :] * w[1:2, :]
    z = z + u * w[2:3, :]
    ya_ref[...] = (cb_ref[...] * z).astype(BF16)
    cnew_ref[...] = uext[CONV_HIST + t_true - 2:CONV_HIST + t_true, :]
    uext[0:CONV_HIST, :] = uext[tt:tt + CONV_HIST, :]

    pin = pin_ref[...]
    pext[POOL_HIST16:POOL_HIST16 + tt, :] = pin
    pos = (pos0 + t * tt + lax.broadcasted_iota(jnp.int32, (tt, 1), 0)).astype(F32)
    for g, win in enumerate(POOL_WINDOWS):
        ls = slice(g * LANES, (g + 1) * LANES)
        acc = pin[:, ls]
        for i in range(1, win):
            acc = acc + pext[POOL_HIST16 - i:POOL_HIST16 - i + tt, ls]
        cnt = jnp.minimum(float(win), pos + 1.0)
        d = acc / cnt - pin[:, ls]
        y = jnp.dot(d.astype(BF16), pw_ref[g], preferred_element_type=F32) * ps_ref[:, ls]
        yd_ref[:, ls] = y.astype(BF16)
    pnew_ref[...] = pext[t_true:t_true + POOL_HIST16, :]
    pext[0:POOL_HIST16, :] = pext[tt:tt + POOL_HIST16, :]


def _seq_tiles(B, nt, with_slot):
    if not with_slot:
        return B, (lambda b, t: b * nt + t), (lambda b, t: b * nt + t), (lambda b: b)
    return (B + 1,
            lambda b, t: jnp.where(b < B, b * nt + t, B * nt - 1),
            lambda b, t: jnp.where(b < B, b * nt + t, B * nt),
            lambda b: jnp.minimum(b, B - 1))


def conv_pool(P, B, T, cbuf8, pbuf16, conv_w, pool_w, pool_scale, l, tt, t_true, pos0, slot=None):
    nt = T // tt
    slot_rows = slot[0].shape[0] if slot else 0
    rows_b, in_blk, out_blk, seq_blk = _seq_tiles(B, nt, bool(slot))

    def col(c):
        return pl.BlockSpec((tt, 512), lambda b, t: (in_blk(b, t), c // 512))

    slot_spec = pl.BlockSpec((slot_rows, 512), lambda b, t: (0, 0))
    return pl.pallas_call(
        functools.partial(_conv_pool_kernel, tt=tt, t_true=t_true, pos0=pos0, n_seq=B, slot_rows=slot_rows),
        grid=(rows_b, nt),
        in_specs=[
            col(COL_CB), col(COL_CC), col(COL_CH), col(COL_POOL),
            pl.BlockSpec((None, CONV_HIST, 512), lambda b, t: (seq_blk(b), 0, 0)),
            pl.BlockSpec((None, POOL_HIST16, 512), lambda b, t: (seq_blk(b), 0, 0)),
            pl.BlockSpec((None, 3, 512), lambda b, t: (l, 0, 0)),
            pl.BlockSpec((None, 4, LANES, LANES), lambda b, t: (l, 0, 0, 0)),
            pl.BlockSpec((None, 1, 512), lambda b, t: (l, 0, 0)),
        ] + ([slot_spec, slot_spec] if slot else []),
        out_specs=[
            pl.BlockSpec((tt, 512), lambda b, t: (out_blk(b, t), 0)),
            pl.BlockSpec((tt, 512), lambda b, t: (out_blk(b, t), 0)),
            pl.BlockSpec((None, 2, 512), lambda b, t: (seq_blk(b), 0, 0)),
            pl.BlockSpec((None, POOL_HIST16, 512), lambda b, t: (seq_blk(b), 0, 0)),
        ],
        out_shape=[
            jax.ShapeDtypeStruct((B * T + slot_rows, 512), BF16),
            jax.ShapeDtypeStruct((B * T + slot_rows, 512), BF16),
            jax.ShapeDtypeStruct((B, 2, 512), F32),
            jax.ShapeDtypeStruct((B, POOL_HIST16, 512), F32),
        ],
        scratch_shapes=[pltpu.VMEM((CONV_HIST + tt, 512), F32), pltpu.VMEM((POOL_HIST16 + tt, 512), F32)],
        compiler_params=_cparams(("arbitrary", "arbitrary")),
    )(P, P, P, P, cbuf8, pbuf16, conv_w, pool_w, pool_scale, *(slot or ()))


GLA_UNROLL = 8

def _gla_kernel(q_ref, k_ref, v_ref, r_ref, lr_ref, wa_ref, ba_ref, gn_ref, s0_ref, *rest,
                tt, C, c_true, n_seq, slot_rows):
    if slot_rows:
        yc_slot_ref, yc_ref = rest[:2]

        @pl.when((pl.program_id(0) == n_seq) & (pl.program_id(1) == 0))
        def _():
            yc_ref[0:slot_rows, :] = yc_slot_ref[...]

        pl.when(pl.program_id(0) < n_seq)(
            lambda: _gla_body(q_ref, k_ref, v_ref, r_ref, lr_ref, wa_ref, ba_ref, gn_ref, s0_ref, *rest[1:],
                              tt=tt, C=C, c_true=c_true))
    else:
        _gla_body(q_ref, k_ref, v_ref, r_ref, lr_ref, wa_ref, ba_ref, gn_ref, s0_ref, *rest,
                  tt=tt, C=C, c_true=c_true)


def _gla_body(q_ref, k_ref, v_ref, r_ref, lr_ref, wa_ref, ba_ref, gn_ref, s0_ref,
              yc_ref, sfin_ref, s_scr, b_scr, sprev_scr, *, tt, C, c_true):
    assert c_true == C or tt == C
    t = pl.program_id(1)

    @pl.when(t == 0)
    def _():
        s_scr[...] = s0_ref[...]

    row = lax.broadcasted_iota(jnp.int32, (C, 1), 0)
    lane = lax.broadcasted_iota(jnp.int32, (1, LANES), 1)
    head0 = lane < GLA_DK
    srow = lax.broadcasted_iota(jnp.int32, (LANES, 1), 0)
    eye = srow == lane
    gn = gn_ref[...]
    nchunk = tt // C

    def chunk_rows(c):
        return pl.ds(pl.multiple_of(c * C, C), C)

    def v_head(rows, h):
        v_h = v_ref[rows, h * LANES:(h + 1) * LANES]
        if c_true < C:
            v_h = jnp.where(row < c_true, v_h, 0.0)
        return v_h

    la = jnp.dot(lr_ref[...].astype(BF16), wa_ref[...], preferred_element_type=F32) + ba_ref[...]
    la = jax.nn.log_sigmoid(la) / GLA_TAU
    if c_true < C:
        la = jnp.where(row < c_true, la, 0.0)
    cb = 2 * C if nchunk % 2 == 0 else C
    bi = lax.broadcasted_iota(jnp.int32, (cb, 1), 0)
    bj = lax.broadcasted_iota(jnp.int32, (1, cb), 1)
    ones_tri = (((bi >= C) == (bj >= C)) & (bj <= bi)).astype(BF16)
    for blk in range(tt // cb):
        part = la[blk * cb:(blk + 1) * cb, :]
        acc = None
        for _ in range(3):
            term = part.astype(BF16)
            d = jnp.dot(ones_tri, term, preferred_element_type=F32)
            acc = d if acc is None else acc + d
            part = part - term.astype(F32)
        b_scr[blk * cb:(blk + 1) * cb, :] = acc

    def by_head(x):
        return jnp.concatenate([jnp.where(head0, x, 0.0), jnp.where(head0, 0.0, x)], axis=0)

    def v_pair(rows, p):
        return jnp.concatenate([v_head(rows, 2 * p), v_head(rows, 2 * p + 1)], axis=0).astype(BF16)

    def scan(c, carry):
        rows = chunk_rows(c)
        b = b_scr[rows, :]
        b_last = b[C - 1:C, :]
        kend = k_ref[rows, :] * jnp.exp(b_last - b)
        for p in range(2):
            ls = slice(p * LANES, (p + 1) * LANES)
            s_p = s_scr[ls, :]
            sprev_scr[c, ls, :] = s_p.astype(BF16)
            dec = jnp.exp(jnp.sum(jnp.where(eye, b_last[:, ls], 0.0), axis=1, keepdims=True))
            ds = lax.dot_general(by_head(kend[:, ls]).astype(BF16), v_pair(rows, p), (((0,), (0,)), ((), ())),
                                 preferred_element_type=F32)
            s_scr[ls, :] = dec * s_p + ds
        return carry

    lax.fori_loop(0, nchunk, scan, 0, unroll=min(GLA_UNROLL, nchunk))

    ri = lax.broadcasted_iota(jnp.int32, (2 * C, 1), 0)
    ci = lax.broadcasted_iota(jnp.int32, (1, 2 * C), 1)
    same_head_causal = ((ri >= C) == (ci >= C)) & ((ri % C) >= (ci % C))

    def emit(c, carry):
        rows = chunk_rows(c)
        b = b_scr[rows, :]
        qt = q_ref[rows, :] * (GLA_DK ** -0.5) * jnp.exp(b)
        kt = k_ref[rows, :] * jnp.exp(-b)
        for p in range(2):
            ls = slice(p * LANES, (p + 1) * LANES)
            qm = by_head(qt[:, ls]).astype(BF16)
            kt2 = jnp.concatenate([kt[:, ls], kt[:, ls]], axis=0).astype(BF16)
            a = jnp.where(same_head_causal, _dot_nt(qm, kt2), 0.0)
            o = jnp.dot(a.astype(BF16), v_pair(rows, p), preferred_element_type=F32)
            o = o + jnp.dot(qm, sprev_scr[c, ls, :], preferred_element_type=F32)
            on = _rms(o) * gn
            for hh in range(2):
                hs = slice((2 * p + hh) * LANES, (2 * p + hh + 1) * LANES)
                r_h = r_ref[rows, hs]
                yc_ref[rows, hs] = (on[hh * C:(hh + 1) * C, :] * (r_h * jax.nn.sigmoid(r_h))).astype(BF16)
        return carry

    lax.fori_loop(0, nchunk, emit, 0, unroll=min(GLA_UNROLL, nchunk))
    sfin_ref[...] = s_scr[...]


def gla(P, B, T, wa_pad, b_a, gla_norm, s0, l, tt, C, c_true, slot=None):
    nt = T // tt
    slot_rows = slot.shape[0] if slot is not None else 0
    rows_b, in_blk, out_blk, seq_blk = _seq_tiles(B, nt, slot is not None)
    return pl.pallas_call(
        functools.partial(_gla_kernel, tt=tt, C=C, c_true=c_true, n_seq=B, slot_rows=slot_rows),
        grid=(rows_b, nt),
        in_specs=[
            pl.BlockSpec((tt, 256), lambda b, t: (in_blk(b, t), COL_GQ // 256)),
            pl.BlockSpec((tt, 256), lambda b, t: (in_blk(b, t), COL_GK // 256)),
            pl.BlockSpec((tt, 512), lambda b, t: (in_blk(b, t), COL_GV // 512)),
            pl.BlockSpec((tt, 512), lambda b, t: (in_blk(b, t), COL_GR // 512)),
            pl.BlockSpec((tt, LANES), lambda b, t: (in_blk(b, t), COL_LR // LANES)),
            pl.BlockSpec((None, LANES, 256), lambda b, t: (l, 0, 0)),
            pl.BlockSpec((None, 1, 256), lambda b, t: (l, 0, 0)),
            pl.BlockSpec((None, 1, LANES), lambda b, t: (l, 0, 0)),
            pl.BlockSpec((None, 256, LANES), lambda b, t: (seq_blk(b), 0, 0)),
        ] + ([pl.BlockSpec((slot_rows, 512), lambda b, t: (0, 0))] if slot is not None else []),
        out_specs=[
            pl.BlockSpec((tt, 512), lambda b, t: (out_blk(b, t), 0)),
            pl.BlockSpec((None, 256, LANES), lambda b, t: (seq_blk(b), 0, 0)),
        ],
        out_shape=[jax.ShapeDtypeStruct((B * T + slot_rows, 512), BF16),
                   jax.ShapeDtypeStruct((B, 256, LANES), F32)],
        scratch_shapes=[pltpu.VMEM((256, LANES), F32), pltpu.VMEM((tt, 256), F32),
                        pltpu.VMEM((tt // C, 256, LANES), BF16)],
        compiler_params=_cparams(("arbitrary", "arbitrary")),
    )(P, P, P, P, P, wa_pad, b_a, gla_norm, s0, *(() if slot is None else (slot,)))


def _head_norm(x, gain, head0):
    x2 = x * x
    s0 = jnp.sum(jnp.where(head0, x2, 0.0), axis=-1, keepdims=True)
    s1 = jnp.sum(jnp.where(head0, 0.0, x2), axis=-1, keepdims=True)
    ms = jnp.where(head0, s0, s1) / float(HEAD_DIM)
    return x * lax.rsqrt(ms + EPS) * gain


def _head_norm_mxu(x, gain, same_head):
    x2 = x * x
    hi = x2.astype(BF16)
    lo = (x2 - hi.astype(F32)).astype(BF16)
    ss = jnp.dot(hi, same_head, preferred_element_type=F32) + jnp.dot(lo, same_head, preferred_element_type=F32)
    return x * lax.rsqrt(ss / float(HEAD_DIM) + EPS) * gain


NORM_ROWS = 512
ATTN_UNROLL = 32
QK_SCALE = 0.125


def _attn_prompt_kernel(q_ref, k_ref, v_ref, qg_ref, kg_ref, sl_ref, win_prev_ref,
                        o_ref, lse_ref, kv_ref, tmp, qd, kd, vd, od, ld, bias, *, S, dil, n_keep, head_base):
    del win_prev_ref
    msub = S // dil
    nb = msub // NK
    kstride = msub + NK
    lane = lax.broadcasted_iota(jnp.int32, (1, LANES), 1)
    head0 = lane < HEAD_DIM
    same_head = ((lax.broadcasted_iota(jnp.int32, (LANES, 1), 0) < HEAD_DIM) == head0).astype(BF16)

    def norm_into_tmp(src_ref, g_ref):
        def body(i, carry):
            rows = pl.ds(pl.multiple_of(i * NORM_ROWS, NORM_ROWS), NORM_ROWS)
            tmp[rows, :] = _head_norm_mxu(src_ref[rows, :], g_ref[...], same_head)
            return carry
        lax.fori_loop(0, S // NORM_ROWS, body, 0, unroll=4)

    norm_into_tmp(q_ref, qg_ref)
    for r in range(dil):
        qd[r * msub:(r + 1) * msub, :] = (tmp[pl.ds(r, msub, stride=dil), :] * QK_SCALE).astype(BF16)
    norm_into_tmp(k_ref, kg_ref)
    kv_ref[0] = tmp[S - n_keep:S, :].T
    kv_ref[1] = v_ref[S - n_keep:S, :].T
    zeros = jnp.zeros((NK, LANES), BF16)
    for r in range(dil):
        kd[r * kstride:r * kstride + NK, :] = zeros
        vd[r * kstride:r * kstride + NK, :] = zeros
        kd[r * kstride + NK:(r + 1) * kstride, :] = tmp[pl.ds(r, msub, stride=dil), :].astype(BF16)
        vd[r * kstride + NK:(r + 1) * kstride, :] = v_ref[pl.ds(r, msub, stride=dil), :].astype(BF16)

    qi = lax.broadcasted_iota(jnp.int32, (NK, 2 * NK), 0)
    ki = lax.broadcasted_iota(jnp.int32, (NK, 2 * NK), 1)
    dist = qi - ki + NK
    band = (dist >= 0) & (dist <= NK)
    alibi = (dist * dil).astype(F32)
    cur = ki >= NK
    for hh in range(2):
        slope = sl_ref[head_base + 2 * pl.program_id(1) + hh]
        bias[2 * hh] = jnp.where(band, -(slope * alibi), NEG)
        bias[2 * hh + 1] = jnp.where(band & cur, -(slope * alibi), NEG)

    def block(j, carry):
        r = j // nb
        n = j - r * nb
        first = jnp.where(n == 0, 1, 0)
        qb = pl.multiple_of(j * NK, NK)
        kb = pl.multiple_of(j * NK + r * NK, NK)
        q2 = qd[pl.ds(qb, NK), :]
        k2 = kd[pl.ds(kb, 2 * NK), :]
        v2 = vd[pl.ds(kb, 2 * NK), :]
        outs, lses = [], []
        for hh in range(2):
            hmask = head0 if hh == 0 else jnp.logical_not(head0)
            qm = jnp.where(hmask, q2, jnp.zeros_like(q2))
            s = lax.dot_general(qm, k2, (((1,), (1,)), ((), ())), preferred_element_type=F32)
            s = s + bias[2 * hh + first]
            m = jnp.max(s, axis=-1, keepdims=True)
            p = jnp.exp(s - m)
            lsum = jnp.sum(p, axis=-1, keepdims=True)
            outs.append(jnp.dot(p.astype(BF16), v2, preferred_element_type=F32) / lsum)
            lses.append(m + jnp.log(lsum))
        od[pl.ds(qb, NK), :] = jnp.where(head0, outs[0], outs[1])
        ld[pl.ds(qb, NK), :] = jnp.where(head0, lses[0], lses[1])
        return carry

    lax.fori_loop(0, dil * nb, block, 0, unroll=ATTN_UNROLL)

    for r in range(dil):
        o_ref[pl.ds(r, msub, stride=dil), :] = od[r * msub:(r + 1) * msub, :]
        lse_ref[pl.ds(r, msub, stride=dil), :] = ld[r * msub:(r + 1) * msub, :]


def attn_prompt(P, B, S, q_gain, k_gain, slopes, l, g, win_prev):
    window, dil = ATTN_GROUPS[g]
    n_keep = min(window, S)

    def col(c):
        return pl.BlockSpec((S, LANES), lambda b, p: (b, c // LANES + 2 * g + p))

    def par(stack):
        return pl.BlockSpec((None, 1, LANES), lambda b, p: (stack * 6 + 2 * g + p, 0, 0))

    return pl.pallas_call(
        functools.partial(_attn_prompt_kernel, S=S, dil=dil, n_keep=n_keep, head_base=4 * g),
        grid=(B, 2),
        in_specs=[col(COL_AQ), col(COL_AK), col(COL_AV), par(l), par(l), pl.BlockSpec(memory_space=pltpu.SMEM),
                  pl.BlockSpec(memory_space=pl.ANY)],
        out_specs=[
            pl.BlockSpec((None, S, LANES), lambda b, p: (b, 0, p)),
            pl.BlockSpec((None, S, LANES), lambda b, p: (b, 0, p)),
            pl.BlockSpec((None, 2, LANES, n_keep), lambda b, p: (l * B + b, 0, p, 0)),
        ],
        out_shape=[
            jax.ShapeDtypeStruct((B, S, 256), F32),
            jax.ShapeDtypeStruct((B, S, 256), F32),
            jax.ShapeDtypeStruct((DEPTH * B, 2, 256, n_keep), F32),
        ],
        input_output_aliases={6: 2},
        scratch_shapes=[
            pltpu.VMEM((S, LANES), F32),
            pltpu.VMEM((S, LANES), BF16),
            pltpu.VMEM((S + dil * NK, LANES), BF16),
            pltpu.VMEM((S + dil * NK, LANES), BF16),
            pltpu.VMEM((S, LANES), F32),
            pltpu.VMEM((S, LANES), F32),
            pltpu.VMEM((4, NK, 2 * NK), F32),
        ],
        compiler_params=_cparams(("parallel", "arbitrary")),
    )(P, P, P, q_gain, k_gain, slopes, win_prev)


T_PAD = 8


def _attn_sample_kernel(q_ref, k_ref, v_ref, qg_ref, kg_ref, sl_ref, cache_ref, win_prev_ref,
                        o_ref, lse_ref, cnew_ref, *, L, dil, t_true, head_base):
    del win_prev_ref
    lane = lax.broadcasted_iota(jnp.int32, (1, LANES), 1)
    head0 = lane < HEAD_DIM
    trow8 = lax.broadcasted_iota(jnp.int32, (T_PAD, LANES), 0)
    lane8 = lax.broadcasted_iota(jnp.int32, (T_PAD, LANES), 1)
    place = ((lane8 == LANES - t_true + trow8) & (trow8 < t_true)).astype(F32)
    is_new = lane >= LANES - t_true

    def rolled_with_new(rows, new):
        shifted = pltpu.roll(cache_ref[rows, :], L - t_true, 1)
        tail = lax.dot_general(new, place, (((0,), (0,)), ((), ())), preferred_element_type=F32,
                               precision=lax.Precision.HIGHEST)
        if L > LANES:
            cnew_ref[rows, 0:L - LANES] = shifted[:, 0:L - LANES]
        cnew_ref[rows, L - LANES:L] = jnp.where(is_new, tail, shifted[:, L - LANES:L])

    trow = lax.broadcasted_iota(jnp.int32, (T_PAD, 1), 0)
    scol = lax.broadcasted_iota(jnp.int32, (1, L), 1)
    dist = L + trow - scol
    valid_c = ((dist & (dil - 1)) == 0) & (dist <= NK * dil)
    alibi_c = dist.astype(F32)

    for p in range(2):
        ls = slice(p * LANES, (p + 1) * LANES)
        qn = _head_norm(q_ref[:, ls], qg_ref[:, ls], head0)
        kn = _head_norm(k_ref[:, ls], kg_ref[:, ls], head0)
        vn = v_ref[:, ls]
        krows = slice(p * LANES, (p + 1) * LANES)
        vrows = slice(256 + p * LANES, 256 + (p + 1) * LANES)
        rolled_with_new(krows, kn)
        rolled_with_new(vrows, vn)
        kc_t = cache_ref[krows, :].astype(BF16)
        vc_t = cache_ref[vrows, :].astype(BF16)
        knr = kn.astype(BF16).astype(F32)
        vnr = vn.astype(BF16).astype(F32)
        outs, lses = [], []
        for hh in range(2):
            hmask = head0 if hh == 0 else jnp.logical_not(head0)
            qm = jnp.where(hmask, qn, 0.0).astype(BF16)
            qmr = qm.astype(F32)
            slope = sl_ref[head_base + 2 * p + hh]
            sc = jnp.dot(qm, kc_t, preferred_element_type=F32)
            sc = sc / float(np.sqrt(HEAD_DIM)) - slope * alibi_c
            sc = jnp.where(valid_c, sc, NEG)
            m = jnp.max(sc, axis=-1, keepdims=True)
            sn = []
            for u in range(t_true):
                du = trow - u
                su = jnp.sum(qmr * knr[u:u + 1, :], axis=-1, keepdims=True) / float(np.sqrt(HEAD_DIM))
                su = su - slope * du.astype(F32)
                su = jnp.where((du >= 0) & ((du & (dil - 1)) == 0), su, NEG)
                sn.append(su)
                m = jnp.maximum(m, su)
            pc = jnp.exp(sc - m)
            lsum = jnp.sum(pc, axis=-1, keepdims=True)
            acc = _dot_nt(pc.astype(BF16), vc_t)
            for u in range(t_true):
                pu = jnp.exp(sn[u] - m)
                lsum = lsum + pu
                acc = acc + pu.astype(BF16).astype(F32) * vnr[u:u + 1, :]
            outs.append(acc / lsum)
            lses.append(m + jnp.log(lsum))
        o_ref[:, ls] = jnp.where(head0, outs[0], outs[1])
        lse_ref[:, ls] = jnp.where(head0, lses[0], lses[1])


def attn_sample(P, B, q_gain, k_gain, slopes, cache_t, l, g, t_true, win_prev):
    window, dil = ATTN_GROUPS[g]
    L = cache_t.shape[2]

    def col(c):
        return pl.BlockSpec((T_PAD, 256), lambda b: (b, c // 256 + g))

    def par(stack):
        return pl.BlockSpec((None, 1, 256), lambda b: (stack * 3 + g, 0, 0))

    return pl.pallas_call(
        functools.partial(_attn_sample_kernel, L=L, dil=dil, t_true=t_true, head_base=4 * g),
        grid=(B,),
        in_specs=[col(COL_AQ), col(COL_AK), col(COL_AV), par(l), par(l), pl.BlockSpec(memory_space=pltpu.SMEM),
                  pl.BlockSpec((None, 512, L), lambda b: (l * B + b, 0, 0)),
                  pl.BlockSpec(memory_space=pl.ANY)],
        out_specs=[
            pl.BlockSpec((T_PAD, 256), lambda b: (b, 0)),
            pl.BlockSpec((T_PAD, 256), lambda b: (b, 0)),
            pl.BlockSpec((None, 512, L), lambda b: (l * B + b, 0, 0)),
        ],
        out_shape=[
            jax.ShapeDtypeStruct((B * T_PAD, 256), F32),
            jax.ShapeDtypeStruct((B * T_PAD, 256), F32),
            jax.ShapeDtypeStruct((DEPTH * B, 512, L), F32),
        ],
        input_output_aliases={7: 2},
        compiler_params=_cparams(("parallel",)),
    )(P, P, P, q_gain, k_gain, slopes, cache_t, win_prev)


def _merge_groups(o0, l0, o1, l1, o2, l2):
    a0, a1, a2 = l0[...], l1[...], l2[...]
    m = jnp.maximum(jnp.maximum(a0, a1), a2)
    e0, e1, e2 = jnp.exp(a0 - m), jnp.exp(a1 - m), jnp.exp(a2 - m)
    den = e0 + e1 + e2
    return (e0 / den) * o0[...] + (e1 / den) * o1[...] + (e2 / den) * o2[...]


def _attn_merge_kernel(o0, l0, o1, l1, o2, l2, slot_ref, y_ref, *, n_tiles, slot_rows):
    @pl.when(pl.program_id(0) < n_tiles)
    def _():
        y_ref[...] = _merge_groups(o0, l0, o1, l1, o2, l2).astype(BF16)

    @pl.when(pl.program_id(0) == n_tiles)
    def _():
        y_ref[0:slot_rows, :] = slot_ref[...]


def attn_merge(parts, tm, slot):
    M = parts[0].shape[0]
    n_tiles = M // tm
    slot_rows = slot.shape[0]
    spec = pl.BlockSpec((tm, 256), lambda m: (jnp.minimum(m, n_tiles - 1), 0))
    return pl.pallas_call(
        functools.partial(_attn_merge_kernel, n_tiles=n_tiles, slot_rows=slot_rows),
        grid=(n_tiles + 1,),
        in_specs=[spec] * 6 + [pl.BlockSpec((slot_rows, 256), lambda m: (0, 0))],
        out_specs=pl.BlockSpec((tm, 256), lambda m: (m, 0)),
        out_shape=jax.ShapeDtypeStruct((M + slot_rows, 256), BF16),
        compiler_params=_cparams(("arbitrary",)),
    )(*parts, slot)


def _sample_pack_kernel(sel_ref, ya_ref, yd_ref, yc_ref, o0, l0, o1, l1, o2, l2, ya_o, yb_o, yc_o, yd_o):
    sel = sel_ref[...]

    def pack(y):
        return jnp.dot(sel, y, preferred_element_type=F32).astype(BF16)

    ya_o[...] = pack(ya_ref[...])
    yd_o[...] = pack(yd_ref[...])
    yc_o[...] = pack(yc_ref[...])
    yb_o[...] = pack(_merge_groups(o0, l0, o1, l1, o2, l2).astype(BF16))


def sample_pack(ya, yd, yc, parts, n_seq, t_true):
    rows = n_seq * T_PAD
    sel = np.zeros((SAMPLE_SLOT, rows), np.float32)
    for b in range(n_seq):
        for t in range(t_true):
            sel[b * t_true + t, b * T_PAD + t] = 1.0
    out = lambda w: jax.ShapeDtypeStruct((SAMPLE_SLOT, w), BF16)
    return pl.pallas_call(
        _sample_pack_kernel,
        out_shape=[out(512), out(256), out(512), out(512)],
        compiler_params=pltpu.CompilerParams(vmem_limit_bytes=VMEM_LIMIT),
    )(jnp.asarray(sel, BF16), ya, yd, yc, *parts)


GATE_SRC0 = 5904
assert GATE_SRC0 % SRC_ALIGN == 0 and D_MODEL % SRC_ALIGN == 0


def _merge_kernel(xn_ref, ya_ref, yb_ref, yc_ref, yd_ref, g0, g1, g2, g3, u0, u1, u2, u3, o_ref,
                  gs0, gs1, gs2, gs3, us0, us1, us2, us3):
    branches = ((ya_ref, g0, u0, gs0, us0), (yb_ref, g1, u1, gs1, us1),
                (yc_ref, g2, u2, gs2, us2), (yd_ref, g3, u3, gs3, us3))

    @pl.when(pl.program_id(1) == 0)
    def _():
        for _, g_ref, u_ref, g_s, u_s in branches:
            g_s[...] = g_ref[...].astype(BF16)
            u_s[...] = u_ref[...].astype(BF16)

    xn = xn_ref[...]
    acc = None
    for y_ref, _, _, g_s, u_s in branches:
        gate = jax.nn.sigmoid(_dot_nt(xn, g_s[...]))
        term = gate * jnp.dot(y_ref[...], u_s[...], preferred_element_type=F32)
        acc = term if acc is None else acc + term
    o_ref[...] = acc.astype(BF16)


def merge(xn, ya, yb, yc, yd, w_in_t, ups, l, tm):
    M = xn.shape[0]
    assert M % tm == 0
    tn = PREP_ROWS
    widths = (512, 256, 512, 512)

    def act(width):
        return pl.BlockSpec((tm, width), lambda n, m: (m, 0))

    def gate(b):
        base = (GATE_SRC0 + b * D_MODEL) // SRC_ALIGN
        return _w_in_rows(l, lambda n, m: base + n * (tn // SRC_ALIGN))

    def up(width):
        return pl.BlockSpec((None, width, tn), lambda n, m: (l, 0, n))

    return pl.pallas_call(
        _merge_kernel,
        grid=(D_MODEL // tn, M // tm),
        in_specs=[act(D_MODEL)] + [act(w) for w in widths] + [gate(b) for b in range(N_BRANCH)]
                 + [up(w) for w in widths],
        out_specs=pl.BlockSpec((tm, tn), lambda n, m: (m, n)),
        out_shape=jax.ShapeDtypeStruct((M, D_MODEL), BF16),
        scratch_shapes=[pltpu.VMEM((tn, D_MODEL), BF16)] * N_BRANCH + [pltpu.VMEM((w, tn), BF16) for w in widths],
        compiler_params=_cparams(("parallel", "arbitrary")),
    )(xn, ya, yb, yc, yd, w_in_t, w_in_t, w_in_t, w_in_t, *ups)


def _layer(x, xn, W, l, T, mixers):
    h = ffn_up(xn, W['ff1_gate'], W['ff1_up'], l, T['up_m'], T['up_n'])
    x, xn = matmul_res(x, h, W['ff1_down'], l, 0.5, T['down_m'], W['n_mix'], l)
    P = mix_in(xn, W['w_in_t'], l, T['mix_m'])
    (ya, yb, yc, yd), states = mixers(P, l)
    mg = merge(xn, ya, yb, yc, yd, W['w_in_t'], (W['up_a'], W['up_b'], W['up_c'], W['up_d']), l, T['merge_m'])
    x, xn = matmul_res(x, mg, W['w_out'], l, 1.0, T['out_m'], W['n_ff2'], l)
    h = ffn_up(xn, W['ff2_gate'], W['ff2_up'], l, T['up_m'], T['up_n'])
    if l + 1 < DEPTH:
        x, xn = matmul_res(x, h, W['ff2_down'], l, 0.5, T['down_m'], W['n_ff1'], l + 1)
    else:
        x, xn = matmul_res(x, h, W['ff2_down'], l, 0.5, T['down_m'], out_split=True)
    return x, xn, states


SAMPLE_SLOT = 128
TILES = dict(up_m=2080, up_n=512, mix_m=1040, down_m=256, out_m=512, merge_m=1040)


def kernel(x_prompt, x_sample, state_conv, cache_w128_kv, cache_w512_kv, cache_w2048_kv, state_gla, state_pool,
           norm_ff1, ff1_gate, ff1_up, ff1_down, norm_mix, w_in, conv_w, attn_q_gain, attn_k_gain,
           gla_w_a2, gla_b_a, gla_norm, pool_w, pool_scale, w_up_conv, w_up_attn, w_up_gla, w_up_pool, w_out,
           norm_ff2, ff2_gate, ff2_up, ff2_down):
    BP, S, _ = x_prompt.shape
    BS, TS, _ = x_sample.shape
    caches = (cache_w128_kv, cache_w512_kv, cache_w2048_kv)

    W = {
        'n_ff1': norm_ff1.reshape(DEPTH, 1, D_MODEL), 'n_mix': norm_mix.reshape(DEPTH, 1, D_MODEL),
        'n_ff2': norm_ff2.reshape(DEPTH, 1, D_MODEL),
        'ff1_gate': ff1_gate, 'ff1_up': ff1_up, 'ff1_down': ff1_down,
        'ff2_gate': ff2_gate, 'ff2_up': ff2_up, 'ff2_down': ff2_down,
        'w_in_t': jnp.transpose(w_in, (0, 2, 1)),
        'up_a': w_up_conv, 'up_b': w_up_attn, 'up_c': w_up_gla, 'up_d': w_up_pool, 'w_out': w_out,
    }
    wa_pad = jnp.pad(gla_w_a2, ((0, 0), (0, LANES - gla_w_a2.shape[1]), (0, 0))).astype(BF16)
    b_a = gla_b_a.reshape(DEPTH, 1, 256)
    gn = gla_norm.reshape(DEPTH, 1, LANES)
    pool_wb = pool_w.astype(BF16)
    pool_sc = pool_scale.reshape(DEPTH, 1, W_POOL)
    i = jnp.arange(1, N_ATTN_HEADS + 1, dtype=F32)
    slopes = jnp.exp2(-8.0 * i / N_ATTN_HEADS)
    qg6 = attn_q_gain.reshape(DEPTH * 6, 1, LANES)
    kg6 = attn_k_gain.reshape(DEPTH * 6, 1, LANES)
    qg3 = attn_q_gain.reshape(DEPTH * 3, 1, 256)
    kg3 = attn_k_gain.reshape(DEPTH * 3, 1, 256)

    zc = jnp.zeros((BP, CONV_HIST, W_CONV), F32)
    zp = jnp.zeros((BP, POOL_HIST16, W_POOL), F32)
    zs = jnp.zeros((BP, 256, LANES), F32)

    win_p = [jnp.zeros((DEPTH * BP, 2, 256, min(w, S)), F32) for w, _ in ATTN_GROUPS]
    win_s = [jnp.zeros((DEPTH * BS, 512, cc.shape[2]), F32) for cc in caches]

    MP, MS = BP * S, BS * TS
    assert MS <= SAMPLE_SLOT

    def prompt_mixers(P, l, slot):
        ya, yd, cnew, pnew = conv_pool(P, BP, S, zc, zp, conv_w, pool_wb, pool_sc, l, 512, 512, 0,
                                       slot=(slot[0], slot[3]))
        yc, sfin = gla(P, BP, S, wa_pad, b_a, gn, zs, l, 512, GLA_CHUNK, GLA_CHUNK, slot=slot[2])
        parts = []
        for g in range(3):
            o, lse, win_p[g] = attn_prompt(P, BP, S, qg6, kg6, slopes, l, g, win_p[g])
            parts += [o.reshape(MP, 256), lse.reshape(MP, 256)]
        yb = attn_merge(parts, 1024, slot=slot[1])
        return (ya, yb, yc, yd), (cnew, sfin.reshape(BP, 4, GLA_DK, LANES), pnew[:, 1:])

    sconv8 = jnp.pad(state_conv, ((0, 0), (0, 0), (CONV_HIST - 2, 0), (0, 0)))
    spool16 = jnp.pad(state_pool, ((0, 0), (0, 0), (1, 0), (0, 0)))
    sgla = state_gla.reshape(DEPTH, BS, 256, LANES)
    cviews = [jnp.transpose(cc, (0, 1, 3, 4, 5, 2)).reshape(DEPTH * BS, 512, cc.shape[2]) for cc in caches]

    def sample_mixers(P, l):
        MSP = BS * T_PAD
        Ps = jnp.pad(P[MP:MP + MS].reshape(BS, TS, N_MIX), ((0, 0), (0, T_PAD - TS), (0, 0))).reshape(MSP, N_MIX)
        ya, yd, cnew, pnew = conv_pool(Ps, BS, T_PAD, sconv8[l], spool16[l], conv_w, pool_wb, pool_sc, l,
                                       T_PAD, TS, PAST_LEN)
        yc, sfin = gla(Ps, BS, T_PAD, wa_pad, b_a, gn, sgla[l], l, T_PAD, T_PAD, TS)
        parts = []
        for g in range(3):
            o, lse, win_s[g] = attn_sample(Ps, BS, qg3, kg3, slopes, cviews[g], l, g, TS, win_s[g])
            parts += [o, lse]
        ys = sample_pack(ya, yd, yc, parts, BS, TS)
        return ys, (cnew, sfin.reshape(BS, 4, GLA_DK, LANES), pnew[:, 1:])

    def mixers(P, l):
        ys_s, st_sample = sample_mixers(P, l)
        ys, st_prompt = prompt_mixers(P, l, ys_s)
        return ys, (st_prompt, st_sample)

    x = (x_prompt.reshape(MP, D_MODEL), jnp.pad(x_sample.reshape(MS, D_MODEL), ((0, SAMPLE_SLOT - MS), (0, 0))))
    xn = norm(x[0], x[1], W['n_ff1'], 0, 1024)
    st_p, st_s = [], []
    for l in range(DEPTH):
        x, xn, (sp, ss) = _layer(x, xn, W, l, TILES, mixers)
        st_p.append(sp)
        st_s.append(ss)
    yp, ys = x[0], x[1][:MS]

    def stack(sts, f):
        return jnp.stack([f(s) for s in sts])

    def window_out(buf, B):
        n = buf.shape[-1]
        return jnp.transpose(buf.reshape(DEPTH, B, 2, 4, HEAD_DIM, n), (0, 1, 5, 2, 3, 4))

    outs = [yp.reshape(BP, S, D_MODEL), ys.reshape(BS, TS, D_MODEL),
            stack(st_p, lambda s: s[0]), stack(st_s, lambda s: s[0])]
    for g in range(3):
        outs += [window_out(win_p[g], BP), window_out(win_s[g], BS)]
    outs += [stack(st_p, lambda s: s[1]), stack(st_s, lambda s: s[1]),
             stack(st_p, lambda s: s[2]), stack(st_s, lambda s: s[2])]
    return tuple(outs)
```

```python
import functools

import numpy as np
import jax
import jax.numpy as jnp
from jax import lax
from jax.experimental import pallas as pl
from jax.experimental.pallas import tpu as pltpu

BF16 = jnp.bfloat16
F32 = jnp.float32

D_MODEL = 2048
DEPTH = 4
PAST_LEN = 16384
D_FF = 5504
EPS = 1e-6
N_BRANCH = 4
W_CONV = 512
ATTN_GROUPS = ((128, 1), (512, 4), (2048, 16))
N_ATTN_HEADS = 12
HEAD_DIM = 64
GLA_DK = 64
GLA_TAU = 16.0
GLA_CHUNK = 64
POOL_WINDOWS = (2, 4, 8, 16)
W_POOL = 512
NK = 128
LANES = 128
NEG = -1e30

COL_CB, COL_CC, COL_CH, COL_GV, COL_GR, COL_POOL = 0, 512, 1024, 1536, 2048, 2560
COL_AQ, COL_AK, COL_AV = 3072, 3840, 4608
COL_GQ, COL_GK, COL_LR = 5376, 5632, 5888
N_MIX = 6144
N_IN = 14096
PREP_ROWS = 256
MIX_TILE_SRC = ([256 * j for j in range(6)] + [4352, 4608, 4864, 5120, 5392, 5648]
                + [1536 + 256 * j for j in range(9)] + [3840, 4096, 5376])
SRC_ALIGN = 16
assert all(r % SRC_ALIGN == 0 for r in MIX_TILE_SRC)

VMEM_LIMIT = 56 * 1024 * 1024


def _cparams(sem):
    return pltpu.CompilerParams(dimension_semantics=sem, vmem_limit_bytes=VMEM_LIMIT)


def _rms(x):
    return x * lax.rsqrt(jnp.mean(x * x, axis=-1, keepdims=True) + EPS)


def _row_tiles(M, tm):
    n = pl.cdiv(M, tm)
    return n, M - (n - 1) * tm


def _split_rows_specs(tm, n_tiles, slot):
    return (pl.BlockSpec((tm, D_MODEL), lambda m: (jnp.minimum(m, n_tiles - 2), 0)),
            pl.BlockSpec((slot, D_MODEL), lambda m: (0, 0)))


def _norm_kernel(xp_ref, xs_ref, g_ref, xn_ref, *, tiles, tm):
    n, tail = tiles
    m = pl.program_id(0)

    @pl.when(m < n - 1)
    def _():
        xn_ref[...] = (_rms(xp_ref[...]) * g_ref[...]).astype(BF16)

    @pl.when(m == n - 1)
    def _():
        xn_ref[0:tail, :] = (_rms(xs_ref[...]) * g_ref[...]).astype(BF16)


def norm(xp, xs, gain, l, tm):
    MP, slot = xp.shape[0], xs.shape[0]
    assert MP % tm == 0 and slot < tm
    tiles = (MP // tm + 1, slot)
    return pl.pallas_call(
        functools.partial(_norm_kernel, tiles=tiles, tm=tm),
        grid=(tiles[0],),
        in_specs=[*_split_rows_specs(tm, tiles[0], slot),
                  pl.BlockSpec((None, 1, D_MODEL), lambda m: (l, 0, 0))],
        out_specs=pl.BlockSpec((tm, D_MODEL), lambda m: (m, 0)),
        out_shape=jax.ShapeDtypeStruct((MP + slot, D_MODEL), BF16),
        compiler_params=_cparams(("parallel",)),
    )(xp, xs, gain)


SUB_ROWS = 208


def _sub_blocks(tm):
    assert tm % SUB_ROWS == 0
    return [slice(r0, r0 + SUB_ROWS) for r0 in range(0, tm, SUB_ROWS)]


def _ffn_up_kernel(xn_ref, wg_ref, wu_ref, h_ref, wg_s, wu_s, *, tm):
    @pl.when(pl.program_id(1) == 0)
    def _():
        wg_s[...] = wg_ref[...].astype(BF16)
        wu_s[...] = wu_ref[...].astype(BF16)

    for r in _sub_blocks(tm):
        xn = xn_ref[r, :]
        a = jnp.dot(xn, wg_s[...], preferred_element_type=F32)
        b = jnp.dot(xn, wu_s[...], preferred_element_type=F32)
        h_ref[r, :] = (a * jax.nn.sigmoid(a) * b).astype(BF16)


def ffn_up(xn, wg, wu, l, tm, tn):
    M = xn.shape[0]
    assert M % tm == 0
    return pl.pallas_call(
        functools.partial(_ffn_up_kernel, tm=tm),
        grid=(pl.cdiv(D_FF, tn), M // tm),
        in_specs=[
            pl.BlockSpec((tm, D_MODEL), lambda n, m: (m, 0)),
            pl.BlockSpec((None, D_MODEL, tn), lambda n, m: (l, 0, n)),
            pl.BlockSpec((None, D_MODEL, tn), lambda n, m: (l, 0, n)),
        ],
        out_specs=pl.BlockSpec((tm, tn), lambda n, m: (m, n)),
        out_shape=jax.ShapeDtypeStruct((M, D_FF), BF16),
        scratch_shapes=[pltpu.VMEM((D_MODEL, tn), BF16), pltpu.VMEM((D_MODEL, tn), BF16)],
        compiler_params=_cparams(("parallel", "arbitrary")),
    )(xn, wg, wu)


W_CHUNKS = 8


def _matmul_res_kernel(*refs, scale, with_norm, tiles, tm, x_split, out_split, ck):
    refs = list(refs)
    x_refs = [refs.pop(0) for _ in range(2 if x_split else 1)]
    a_ref, w_ref = refs.pop(0), refs.pop(0)
    g_ref = refs.pop(0) if with_norm else None
    o_refs = [refs.pop(0) for _ in range(2 if out_split else 1)]
    xn_ref = refs.pop(0) if with_norm else None
    w_s = refs.pop(0)
    n, tail = tiles
    step = pl.program_id(0)
    m = step - W_CHUNKS

    @pl.when(step < W_CHUNKS)
    def _():
        w_s[pl.ds(pl.multiple_of(step * ck, 16), ck), :] = w_ref[...].astype(BF16)

    def rows_fn(rows, last):
        x_ref = x_refs[-1] if last else x_refs[0]
        o_ref = o_refs[-1] if last else o_refs[0]
        y = x_ref[rows, :] + scale * jnp.dot(a_ref[rows, :], w_s[...], preferred_element_type=F32)
        if with_norm:
            xn_ref[rows, :] = (_rms(y) * g_ref[...]).astype(BF16)
        o_ref[rows, :] = y

    pl.when((m >= 0) & (m < n - 1))(lambda: rows_fn(slice(0, tm), False))
    pl.when(m == n - 1)(lambda: rows_fn(slice(0, tail), True))


def matmul_res(x, a, w, l, scale, tm, next_gain=None, next_l=0, out_split=False):
    M, K = a.shape
    with_norm = next_gain is not None
    x_split = isinstance(x, tuple)
    tiles = _row_tiles(M, tm)
    assert tiles[1] == SAMPLE_SLOT and (M - SAMPLE_SLOT) % tm == 0 and K % (16 * W_CHUNKS) == 0
    ck = K // W_CHUNKS

    def tile(step):
        return jnp.maximum(step - W_CHUNKS, 0)

    row = pl.BlockSpec((tm, D_MODEL), lambda s: (tile(s), 0))
    split = (pl.BlockSpec((tm, D_MODEL), lambda s: (jnp.minimum(tile(s), tiles[0] - 2), 0)),
             pl.BlockSpec((SAMPLE_SLOT, D_MODEL), lambda s: (0, 0)))
    in_specs = (list(split) if x_split else [row]) + [
        pl.BlockSpec((tm, K), lambda s: (tile(s), 0)),
        pl.BlockSpec((None, ck, D_MODEL), lambda s: (l, jnp.minimum(s, W_CHUNKS - 1), 0))]
    args = (list(x) if x_split else [x]) + [a, w]
    if out_split:
        out_specs = list(split)
        out_shape = [jax.ShapeDtypeStruct((M - SAMPLE_SLOT, D_MODEL), F32),
                     jax.ShapeDtypeStruct((SAMPLE_SLOT, D_MODEL), F32)]
    else:
        out_specs, out_shape = [row], [jax.ShapeDtypeStruct((M, D_MODEL), F32)]
    if with_norm:
        in_specs.append(pl.BlockSpec((None, 1, D_MODEL), lambda s: (next_l, 0, 0)))
        args.append(next_gain)
        out_specs.append(row)
        out_shape.append(jax.ShapeDtypeStruct((M, D_MODEL), BF16))
    res = pl.pallas_call(
        functools.partial(_matmul_res_kernel, scale=scale, with_norm=with_norm, tiles=tiles, tm=tm,
                          x_split=x_split, out_split=out_split, ck=ck),
        grid=(W_CHUNKS + tiles[0],),
        in_specs=in_specs,
        out_specs=out_specs,
        out_shape=out_shape,
        scratch_shapes=[pltpu.VMEM((K, D_MODEL), BF16)],
        compiler_params=_cparams(("arbitrary",)),
    )(*args)
    y = tuple(res[:2]) if out_split else res[0]
    return y, (res[-1] if with_norm else None)


def _dot_nt(a, b):
    return lax.dot_general(a, b, (((1,), (1,)), ((), ())), preferred_element_type=F32)


def _w_in_rows(l, row_of):
    return pl.BlockSpec((None, pl.Element(PREP_ROWS), pl.Element(D_MODEL)),
                        lambda *idx: (l, row_of(*idx) * SRC_ALIGN, 0))


MIX_SLABS = 4


def _mix_in_kernel(src_ref, xn_ref, *rest):
    del src_ref
    w_refs, (p_ref, w_s) = rest[:MIX_SLABS], rest[MIX_SLABS:]

    @pl.when(pl.program_id(1) == 0)
    def _():
        for j, w_ref in enumerate(w_refs):
            w_s[j * PREP_ROWS:(j + 1) * PREP_ROWS, :] = w_ref[...].astype(BF16)

    p_ref[...] = _dot_nt(xn_ref[...], w_s[...])


def mix_in(xn, w_in_t, l, tm):
    M = xn.shape[0]
    assert M % tm == 0 and w_in_t.shape[1] == N_IN
    tn = MIX_SLABS * PREP_ROWS
    src = jnp.asarray(MIX_TILE_SRC, jnp.int32) // SRC_ALIGN

    def slab(j):
        return _w_in_rows(l, lambda n, m, src: src[MIX_SLABS * n + j])

    return pl.pallas_call(
        _mix_in_kernel,
        grid_spec=pltpu.PrefetchScalarGridSpec(
            num_scalar_prefetch=1,
            grid=(N_MIX // tn, M // tm),
            in_specs=[pl.BlockSpec((tm, D_MODEL), lambda n, m, src: (m, 0))] + [slab(j) for j in range(MIX_SLABS)],
            out_specs=pl.BlockSpec((tm, tn), lambda n, m, src: (m, n)),
            scratch_shapes=[pltpu.VMEM((tn, D_MODEL), BF16)],
        ),
        out_shape=jax.ShapeDtypeStruct((M, N_MIX), F32),
        compiler_params=_cparams(("parallel", "arbitrary")),
    )(src, xn, *([w_in_t] * MIX_SLABS))


CONV_HIST = 8
POOL_HIST16 = 16
POOL_PAD = 8


def _conv_pool_kernel(cb_ref, cc_ref, ch_ref, pin_ref, cbuf_ref, pbuf_ref, cw_ref, pw_ref, ps_ref, *rest,
                      tt, t_true, pos0, n_seq, slot_rows):
    if slot_rows:
        ya_slot_ref, yd_slot_ref, ya_ref, yd_ref = rest[:4]

        @pl.when((pl.program_id(0) == n_seq) & (pl.program_id(1) == 0))
        def _():
            ya_ref[0:slot_rows, :] = ya_slot_ref[...]
            yd_ref[0:slot_rows, :] = yd_slot_ref[...]

        pl.when(pl.program_id(0) < n_seq)(
            lambda: _conv_pool_body(cb_ref, cc_ref, ch_ref, pin_ref, cbuf_ref, pbuf_ref, cw_ref, pw_ref, ps_ref,
                                    *rest[2:], tt=tt, t_true=t_true, pos0=pos0))
    else:
        _conv_pool_body(cb_ref, cc_ref, ch_ref, pin_ref, cbuf_ref, pbuf_ref, cw_ref, pw_ref, ps_ref, *rest,
                        tt=tt, t_true=t_true, pos0=pos0)


def _conv_pool_body(cb_ref, cc_ref, ch_ref, pin_ref, cbuf_ref, pbuf_ref, cw_ref, pw_ref, ps_ref,
                    ya_ref, yd_ref, cnew_ref, pnew_ref, uext, pext, sb2, sb4, sb8, *, tt, t_true, pos0):
    t = pl.program_id(1)

    @pl.when(t == 0)
    def _():
        uext[0:CONV_HIST, :] = cbuf_ref[...]
        zeros = jnp.zeros((POOL_PAD, W_POOL), F32)
        pext[0:POOL_PAD, :] = zeros
        sb2[0:POOL_PAD, :] = zeros
        sb4[0:POOL_PAD, :] = zeros
        pext[POOL_PAD:POOL_PAD + POOL_HIST16, :] = pbuf_ref[...]

    u = cc_ref[...] * ch_ref[...]
    uext[CONV_HIST:CONV_HIST + tt, :] = u
    w = cw_ref[...]
    z = uext[CONV_HIST - 2:CONV_HIST - 2 + tt, :] * w[0:1, :]
    z = z + uext[CONV_HIST - 1:CONV_HIST - 1 + tt, :] * w[1:2, :]
    z = z + u * w[2:3, :]
    ya_ref[...] = (cb_ref[...] * z).astype(BF16)
    cnew_ref[...] = uext[CONV_HIST + t_true - 2:CONV_HIST + t_true, :]
    uext[0:CONV_HIST, :] = uext[tt:tt + CONV_HIST, :]

    pin = pin_ref[...]
    d0 = POOL_PAD + POOL_HIST16
    n = POOL_HIST16 + tt
    pext[d0:d0 + tt, :] = pin
    rows = slice(POOL_PAD, POOL_PAD + n)

    def back(k):
        return slice(POOL_PAD - k, POOL_PAD - k + n)

    sb2[rows, :] = pext[rows, :] + pext[back(1), :]
    sb4[rows, LANES:] = sb2[rows, LANES:] + sb2[back(2), LANES:]
    sb8[rows, 2 * LANES:] = sb4[rows, 2 * LANES:] + sb4[back(4), 2 * LANES:]
    data = slice(d0, d0 + tt)
    sums = (sb2[data, 0:LANES], sb4[data, LANES:2 * LANES], sb8[data, 2 * LANES:3 * LANES],
            sb8[data, 3 * LANES:] + sb8[d0 - 8:d0 - 8 + tt, 3 * LANES:])
    pos = (pos0 + t * tt + lax.broadcasted_iota(jnp.int32, (tt, 1), 0)).astype(F32)
    for g, win in enumerate(POOL_WINDOWS):
        ls = slice(g * LANES, (g + 1) * LANES)
        cnt = jnp.minimum(float(win), pos + 1.0)
        d = sums[g] / cnt - pin[:, ls]
        y = jnp.dot(d.astype(BF16), pw_ref[g], preferred_element_type=F32) * ps_ref[:, ls]
        yd_ref[:, ls] = y.astype(BF16)
    pnew_ref[...] = pext[POOL_PAD + t_true:POOL_PAD + t_true + POOL_HIST16, :]
    pext[POOL_PAD:d0, :] = pext[POOL_PAD + tt:d0 + tt, :]


def _seq_tiles(B, nt, with_slot):
    if not with_slot:
        return B, (lambda b, t: b * nt + t), (lambda b, t: b * nt + t), (lambda b: b)
    return (B + 1,
            lambda b, t: jnp.where(b < B, b * nt + t, B * nt - 1),
            lambda b, t: jnp.where(b < B, b * nt + t, B * nt),
            lambda b: jnp.minimum(b, B - 1))


def conv_pool(P, B, T, cbuf8, pbuf16, conv_w, pool_w, pool_scale, l, tt, t_true, pos0, slot=None):
    nt = T // tt
    slot_rows = slot[0].shape[0] if slot else 0
    rows_b, in_blk, out_blk, seq_blk = _seq_tiles(B, nt, bool(slot))

    def col(c):
        return pl.BlockSpec((tt, 512), lambda b, t: (in_blk(b, t), c // 512))

    slot_spec = pl.BlockSpec((slot_rows, 512), lambda b, t: (0, 0))
    return pl.pallas_call(
        functools.partial(_conv_pool_kernel, tt=tt, t_true=t_true, pos0=pos0, n_seq=B, slot_rows=slot_rows),
        grid=(rows_b, nt),
        in_specs=[
            col(COL_CB), col(COL_CC), col(COL_CH), col(COL_POOL),
            pl.BlockSpec((None, CONV_HIST, 512), lambda b, t: (seq_blk(b), 0, 0)),
            pl.BlockSpec((None, POOL_HIST16, 512), lambda b, t: (seq_blk(b), 0, 0)),
            pl.BlockSpec((None, 3, 512), lambda b, t: (l, 0, 0)),
            pl.BlockSpec((None, 4, LANES, LANES), lambda b, t: (l, 0, 0, 0)),
            pl.BlockSpec((None, 1, 512), lambda b, t: (l, 0, 0)),
        ] + ([slot_spec, slot_spec] if slot else []),
        out_specs=[
            pl.BlockSpec((tt, 512), lambda b, t: (out_blk(b, t), 0)),
            pl.BlockSpec((tt, 512), lambda b, t: (out_blk(b, t), 0)),
            pl.BlockSpec((None, 2, 512), lambda b, t: (seq_blk(b), 0, 0)),
            pl.BlockSpec((None, POOL_HIST16, 512), lambda b, t: (seq_blk(b), 0, 0)),
        ],
        out_shape=[
            jax.ShapeDtypeStruct((B * T + slot_rows, 512), BF16),
            jax.ShapeDtypeStruct((B * T + slot_rows, 512), BF16),
            jax.ShapeDtypeStruct((B, 2, 512), F32),
            jax.ShapeDtypeStruct((B, POOL_HIST16, 512), F32),
        ],
        scratch_shapes=[pltpu.VMEM((CONV_HIST + tt, 512), F32)]
                       + [pltpu.VMEM((POOL_PAD + POOL_HIST16 + tt, 512), F32)] * 4,
        compiler_params=_cparams(("arbitrary", "arbitrary")),
    )(P, P, P, P, cbuf8, pbuf16, conv_w, pool_w, pool_scale, *(slot or ()))


GLA_UNROLL = 8

def _gla_kernel(q_ref, k_ref, v_ref, r_ref, lr_ref, wa_ref, ba_ref, gn_ref, s0_ref, *rest,
                tt, C, c_true, n_seq, slot_rows):
    if slot_rows:
        yc_slot_ref, yc_ref = rest[:2]

        @pl.when((pl.program_id(0) == n_seq) & (pl.program_id(1) == 0))
        def _():
            yc_ref[0:slot_rows, :] = yc_slot_ref[...]

        pl.when(pl.program_id(0) < n_seq)(
            lambda: _gla_body(q_ref, k_ref, v_ref, r_ref, lr_ref, wa_ref, ba_ref, gn_ref, s0_ref, *rest[1:],
                              tt=tt, C=C, c_true=c_true))
    else:
        _gla_body(q_ref, k_ref, v_ref, r_ref, lr_ref, wa_ref, ba_ref, gn_ref, s0_ref, *rest,
                  tt=tt, C=C, c_true=c_true)


def _gla_body(q_ref, k_ref, v_ref, r_ref, lr_ref, wa_ref, ba_ref, gn_ref, s0_ref,
              yc_ref, sfin_ref, s_scr, b_scr, sprev_scr, *, tt, C, c_true):
    assert c_true == C or tt == C
    t = pl.program_id(1)

    @pl.when(t == 0)
    def _():
        s_scr[...] = s0_ref[...]

    row = lax.broadcasted_iota(jnp.int32, (C, 1), 0)
    lane = lax.broadcasted_iota(jnp.int32, (1, LANES), 1)
    head0 = lane < GLA_DK
    srow = lax.broadcasted_iota(jnp.int32, (LANES, 1), 0)
    eye = srow == lane
    gn = gn_ref[...]
    nchunk = tt // C

    def chunk_rows(c):
        return pl.ds(pl.multiple_of(c * C, C), C)

    def v_head(rows, h):
        v_h = v_ref[rows, h * LANES:(h + 1) * LANES]
        if c_true < C:
            v_h = jnp.where(row < c_true, v_h, 0.0)
        return v_h

    la = jnp.dot(lr_ref[...].astype(BF16), wa_ref[...], preferred_element_type=F32) + ba_ref[...]
    la = jax.nn.log_sigmoid(la) / GLA_TAU
    if c_true < C:
        la = jnp.where(row < c_true, la, 0.0)
    cb = 2 * C if nchunk % 2 == 0 else C
    bi = lax.broadcasted_iota(jnp.int32, (cb, 1), 0)
    bj = lax.broadcasted_iota(jnp.int32, (1, cb), 1)
    ones_tri = (((bi >= C) == (bj >= C)) & (bj <= bi)).astype(BF16)
    for blk in range(tt // cb):
        part = la[blk * cb:(blk + 1) * cb, :]
        acc = None
        for _ in range(3):
            term = part.astype(BF16)
            d = jnp.dot(ones_tri, term, preferred_element_type=F32)
            acc = d if acc is None else acc + d
            part = part - term.astype(F32)
        b_scr[blk * cb:(blk + 1) * cb, :] = acc

    def by_head(x):
        return jnp.concatenate([jnp.where(head0, x, 0.0), jnp.where(head0, 0.0, x)], axis=0)

    def v_pair(rows, p):
        return jnp.concatenate([v_head(rows, 2 * p), v_head(rows, 2 * p + 1)], axis=0).astype(BF16)

    def scan(c, carry):
        rows = chunk_rows(c)
        b = b_scr[rows, :]
        b_last = b[C - 1:C, :]
        kend = k_ref[rows, :] * jnp.exp(b_last - b)
        for p in range(2):
            ls = slice(p * LANES, (p + 1) * LANES)
            s_p = s_scr[ls, :]
            sprev_scr[c, ls, :] = s_p.astype(BF16)
            dec = jnp.exp(jnp.sum(jnp.where(eye, b_last[:, ls], 0.0), axis=1, keepdims=True))
            ds = lax.dot_general(by_head(kend[:, ls]).astype(BF16), v_pair(rows, p), (((0,), (0,)), ((), ())),
                                 preferred_element_type=F32)
            s_scr[ls, :] = dec * s_p + ds
        return carry

    lax.fori_loop(0, nchunk, scan, 0, unroll=min(GLA_UNROLL, nchunk))

    ri = lax.broadcasted_iota(jnp.int32, (2 * C, 1), 0)
    ci = lax.broadcasted_iota(jnp.int32, (1, 2 * C), 1)
    same_head_causal = ((ri >= C) == (ci >= C)) & ((ri % C) >= (ci % C))

    def emit(c, carry):
        rows = chunk_rows(c)
        b = b_scr[rows, :]
        qt = q_ref[rows, :] * (GLA_DK ** -0.5) * jnp.exp(b)
        kt = k_ref[rows, :] * jnp.exp(-b)
        for p in range(2):
            ls = slice(p * LANES, (p + 1) * LANES)
            qm = by_head(qt[:, ls]).astype(BF16)
            kt2 = jnp.concatenate([kt[:, ls], kt[:, ls]], axis=0).astype(BF16)
            a = jnp.where(same_head_causal, _dot_nt(qm, kt2), 0.0)
            o = jnp.dot(a.astype(BF16), v_pair(rows, p), preferred_element_type=F32)
            o = o + jnp.dot(qm, sprev_scr[c, ls, :], preferred_element_type=F32)
            on = _rms(o) * gn
            for hh in range(2):
                hs = slice((2 * p + hh) * LANES, (2 * p + hh + 1) * LANES)
                r_h = r_ref[rows, hs]
                yc_ref[rows, hs] = (on[hh * C:(hh + 1) * C, :] * (r_h * jax.nn.sigmoid(r_h))).astype(BF16)
        return carry

    lax.fori_loop(0, nchunk, emit, 0, unroll=min(GLA_UNROLL, nchunk))
    sfin_ref[...] = s_scr[...]


def gla(P, B, T, wa_pad, b_a, gla_norm, s0, l, tt, C, c_true, slot=None):
    nt = T // tt
    slot_rows = slot.shape[0] if slot is not None else 0
    rows_b, in_blk, out_blk, seq_blk = _seq_tiles(B, nt, slot is not None)
    return pl.pallas_call(
        functools.partial(_gla_kernel, tt=tt, C=C, c_true=c_true, n_seq=B, slot_rows=slot_rows),
        grid=(rows_b, nt),
        in_specs=[
            pl.BlockSpec((tt, 256), lambda b, t: (in_blk(b, t), COL_GQ // 256)),
            pl.BlockSpec((tt, 256), lambda b, t: (in_blk(b, t), COL_GK // 256)),
            pl.BlockSpec((tt, 512), lambda b, t: (in_blk(b, t), COL_GV // 512)),
            pl.BlockSpec((tt, 512), lambda b, t: (in_blk(b, t), COL_GR // 512)),
            pl.BlockSpec((tt, LANES), lambda b, t: (in_blk(b, t), COL_LR // LANES)),
            pl.BlockSpec((None, LANES, 256), lambda b, t: (l, 0, 0)),
            pl.BlockSpec((None, 1, 256), lambda b, t: (l, 0, 0)),
            pl.BlockSpec((None, 1, LANES), lambda b, t: (l, 0, 0)),
            pl.BlockSpec((None, 256, LANES), lambda b, t: (seq_blk(b), 0, 0)),
        ] + ([pl.BlockSpec((slot_rows, 512), lambda b, t: (0, 0))] if slot is not None else []),
        out_specs=[
            pl.BlockSpec((tt, 512), lambda b, t: (out_blk(b, t), 0)),
            pl.BlockSpec((None, 256, LANES), lambda b, t: (seq_blk(b), 0, 0)),
        ],
        out_shape=[jax.ShapeDtypeStruct((B * T + slot_rows, 512), BF16),
                   jax.ShapeDtypeStruct((B, 256, LANES), F32)],
        scratch_shapes=[pltpu.VMEM((256, LANES), F32), pltpu.VMEM((tt, 256), F32),
                        pltpu.VMEM((tt // C, 256, LANES), BF16)],
        compiler_params=_cparams(("arbitrary", "arbitrary")),
    )(P, P, P, P, P, wa_pad, b_a, gla_norm, s0, *(() if slot is None else (slot,)))


def _head_norm(x, gain, head0):
    x2 = x * x
    s0 = jnp.sum(jnp.where(head0, x2, 0.0), axis=-1, keepdims=True)
    s1 = jnp.sum(jnp.where(head0, 0.0, x2), axis=-1, keepdims=True)
    ms = jnp.where(head0, s0, s1) / float(HEAD_DIM)
    return x * lax.rsqrt(ms + EPS) * gain


def _head_norm_mxu(x, gain, same_head):
    x2 = x * x
    hi = x2.astype(BF16)
    lo = (x2 - hi.astype(F32)).astype(BF16)
    ss = jnp.dot(hi, same_head, preferred_element_type=F32) + jnp.dot(lo, same_head, preferred_element_type=F32)
    return x * lax.rsqrt(ss / float(HEAD_DIM) + EPS) * gain


NORM_ROWS = 512
ATTN_UNROLL = 32
QK_SCALE = 0.125


def _attn_prompt_kernel(q_ref, k_ref, v_ref, qg_ref, kg_ref, sl_ref, win_prev_ref,
                        o_ref, lse_ref, kv_ref, tmp, qd, kd, vd, od, ld, bias, *, S, dil, n_keep, head_base):
    del win_prev_ref
    msub = S // dil
    nb = msub // NK
    kstride = msub + NK
    lane = lax.broadcasted_iota(jnp.int32, (1, LANES), 1)
    head0 = lane < HEAD_DIM
    same_head = ((lax.broadcasted_iota(jnp.int32, (LANES, 1), 0) < HEAD_DIM) == head0).astype(BF16)

    def norm_into_tmp(src_ref, g_ref):
        def body(i, carry):
            rows = pl.ds(pl.multiple_of(i * NORM_ROWS, NORM_ROWS), NORM_ROWS)
            tmp[rows, :] = _head_norm_mxu(src_ref[rows, :], g_ref[...], same_head)
            return carry
        lax.fori_loop(0, S // NORM_ROWS, body, 0, unroll=4)

    norm_into_tmp(q_ref, qg_ref)
    for r in range(dil):
        qd[r * msub:(r + 1) * msub, :] = (tmp[pl.ds(r, msub, stride=dil), :] * QK_SCALE).astype(BF16)
    norm_into_tmp(k_ref, kg_ref)
    kv_ref[0] = tmp[S - n_keep:S, :].T
    kv_ref[1] = v_ref[S - n_keep:S, :].T
    zeros = jnp.zeros((NK, LANES), BF16)
    for r in range(dil):
        kd[r * kstride:r * kstride + NK, :] = zeros
        vd[r * kstride:r * kstride + NK, :] = zeros
        kd[r * kstride + NK:(r + 1) * kstride, :] = tmp[pl.ds(r, msub, stride=dil), :].astype(BF16)
        vd[r * kstride + NK:(r + 1) * kstride, :] = v_ref[pl.ds(r, msub, stride=dil), :].astype(BF16)

    qi = lax.broadcasted_iota(jnp.int32, (NK, 2 * NK), 0)
    ki = lax.broadcasted_iota(jnp.int32, (NK, 2 * NK), 1)
    dist = qi - ki + NK
    band = (dist >= 0) & (dist <= NK)
    alibi = (dist * dil).astype(F32)
    cur = ki >= NK
    for hh in range(2):
        slope = sl_ref[head_base + 2 * pl.program_id(1) + hh]
        bias[2 * hh] = jnp.where(band, -(slope * alibi), NEG)
        bias[2 * hh + 1] = jnp.where(band & cur, -(slope * alibi), NEG)

    def block(j, carry):
        r = j // nb
        n = j - r * nb
        first = jnp.where(n == 0, 1, 0)
        qb = pl.multiple_of(j * NK, NK)
        kb = pl.multiple_of(j * NK + r * NK, NK)
        q2 = qd[pl.ds(qb, NK), :]
        k2 = kd[pl.ds(kb, 2 * NK), :]
        v2 = vd[pl.ds(kb, 2 * NK), :]
        outs, lses = [], []
        for hh in range(2):
            hmask = head0 if hh == 0 else jnp.logical_not(head0)
            qm = jnp.where(hmask, q2, jnp.zeros_like(q2))
            s = lax.dot_general(qm, k2, (((1,), (1,)), ((), ())), preferred_element_type=F32)
            s = s + bias[2 * hh + first]
            m = jnp.max(s, axis=-1, keepdims=True)
            p = jnp.exp(s - m)
            lsum = jnp.sum(p, axis=-1, keepdims=True)
            outs.append(jnp.dot(p.astype(BF16), v2, preferred_element_type=F32) / lsum)
            lses.append(m + jnp.log(lsum))
        od[pl.ds(qb, NK), :] = jnp.where(head0, outs[0], outs[1])
        ld[pl.ds(qb, NK), :] = jnp.where(head0, lses[0], lses[1])
        return carry

    lax.fori_loop(0, dil * nb, block, 0, unroll=ATTN_UNROLL)

    for r in range(dil):
        o_ref[pl.ds(r, msub, stride=dil), :] = od[r * msub:(r + 1) * msub, :]
        lse_ref[pl.ds(r, msub, stride=dil), :] = ld[r * msub:(r + 1) * msub, :]


def attn_prompt(P, B, S, q_gain, k_gain, slopes, l, g, win_prev):
    window, dil = ATTN_GROUPS[g]
    n_keep = min(window, S)

    def col(c):
        return pl.BlockSpec((S, LANES), lambda b, p: (b, c // LANES + 2 * g + p))

    def par(stack):
        return pl.BlockSpec((None, 1, LANES), lambda b, p: (stack * 6 + 2 * g + p, 0, 0))

    return pl.pallas_call(
        functools.partial(_attn_prompt_kernel, S=S, dil=dil, n_keep=n_keep, head_base=4 * g),
        grid=(B, 2),
        in_specs=[col(COL_AQ), col(COL_AK), col(COL_AV), par(l), par(l), pl.BlockSpec(memory_space=pltpu.SMEM),
                  pl.BlockSpec(memory_space=pl.ANY)],
        out_specs=[
            pl.BlockSpec((None, S, LANES), lambda b, p: (b, 0, p)),
            pl.BlockSpec((None, S, LANES), lambda b, p: (b, 0, p)),
            pl.BlockSpec((None, 2, LANES, n_keep), lambda b, p: (l * B + b, 0, p, 0)),
        ],
        out_shape=[
            jax.ShapeDtypeStruct((B, S, 256), F32),
            jax.ShapeDtypeStruct((B, S, 256), F32),
            jax.ShapeDtypeStruct((DEPTH * B, 2, 256, n_keep), F32),
        ],
        input_output_aliases={6: 2},
        scratch_shapes=[
            pltpu.VMEM((S, LANES), F32),
            pltpu.VMEM((S, LANES), BF16),
            pltpu.VMEM((S + dil * NK, LANES), BF16),
            pltpu.VMEM((S + dil * NK, LANES), BF16),
            pltpu.VMEM((S, LANES), F32),
            pltpu.VMEM((S, LANES), F32),
            pltpu.VMEM((4, NK, 2 * NK), F32),
        ],
        compiler_params=_cparams(("parallel", "arbitrary")),
    )(P, P, P, q_gain, k_gain, slopes, win_prev)


T_PAD = 8


def _attn_sample_kernel(q_ref, k_ref, v_ref, qg_ref, kg_ref, sl_ref, cache_ref, win_prev_ref,
                        o_ref, lse_ref, cnew_ref, *, L, dil, t_true, head_base):
    del win_prev_ref
    lane = lax.broadcasted_iota(jnp.int32, (1, LANES), 1)
    head0 = lane < HEAD_DIM
    trow8 = lax.broadcasted_iota(jnp.int32, (T_PAD, LANES), 0)
    lane8 = lax.broadcasted_iota(jnp.int32, (T_PAD, LANES), 1)
    place = ((lane8 == LANES - t_true + trow8) & (trow8 < t_true)).astype(F32)
    is_new = lane >= LANES - t_true

    def rolled_with_new(rows, new):
        shifted = pltpu.roll(cache_ref[rows, :], L - t_true, 1)
        tail = lax.dot_general(new, place, (((0,), (0,)), ((), ())), preferred_element_type=F32,
                               precision=lax.Precision.HIGHEST)
        if L > LANES:
            cnew_ref[rows, 0:L - LANES] = shifted[:, 0:L - LANES]
        cnew_ref[rows, L - LANES:L] = jnp.where(is_new, tail, shifted[:, L - LANES:L])

    trow = lax.broadcasted_iota(jnp.int32, (T_PAD, 1), 0)
    scol = lax.broadcasted_iota(jnp.int32, (1, L), 1)
    dist = L + trow - scol
    valid_c = ((dist & (dil - 1)) == 0) & (dist <= NK * dil)
    alibi_c = dist.astype(F32)

    for p in range(2):
        ls = slice(p * LANES, (p + 1) * LANES)
        qn = _head_norm(q_ref[:, ls], qg_ref[:, ls], head0)
        kn = _head_norm(k_ref[:, ls], kg_ref[:, ls], head0)
        vn = v_ref[:, ls]
        krows = slice(p * LANES, (p + 1) * LANES)
        vrows = slice(256 + p * LANES, 256 + (p + 1) * LANES)
        rolled_with_new(krows, kn)
        rolled_with_new(vrows, vn)
        kc_t = cache_ref[krows, :].astype(BF16)
        vc_t = cache_ref[vrows, :].astype(BF16)
        knr = kn.astype(BF16).astype(F32)
        vnr = vn.astype(BF16).astype(F32)
        outs, lses = [], []
        for hh in range(2):
            hmask = head0 if hh == 0 else jnp.logical_not(head0)
            qm = jnp.where(hmask, qn, 0.0).astype(BF16)
            qmr = qm.astype(F32)
            slope = sl_ref[head_base + 2 * p + hh]
            sc = jnp.dot(qm, kc_t, preferred_element_type=F32)
            sc = sc / float(np.sqrt(HEAD_DIM)) - slope * alibi_c
            sc = jnp.where(valid_c, sc, NEG)
            m = jnp.max(sc, axis=-1, keepdims=True)
            sn = []
            for u in range(t_true):
                du = trow - u
                su = jnp.sum(qmr * knr[u:u + 1, :], axis=-1, keepdims=True) / float(np.sqrt(HEAD_DIM))
                su = su - slope * du.astype(F32)
                su = jnp.where((du >= 0) & ((du & (dil - 1)) == 0), su, NEG)
                sn.append(su)
                m = jnp.maximum(m, su)
            pc = jnp.exp(sc - m)
            lsum = jnp.sum(pc, axis=-1, keepdims=True)
            acc = _dot_nt(pc.astype(BF16), vc_t)
            for u in range(t_true):
                pu = jnp.exp(sn[u] - m)
                lsum = lsum + pu
                acc = acc + pu.astype(BF16).astype(F32) * vnr[u:u + 1, :]
            outs.append(acc / lsum)
            lses.append(m + jnp.log(lsum))
        o_ref[:, ls] = jnp.where(head0, outs[0], outs[1])
        lse_ref[:, ls] = jnp.where(head0, lses[0], lses[1])


def attn_sample(P, B, q_gain, k_gain, slopes, cache_t, l, g, t_true, win_prev):
    window, dil = ATTN_GROUPS[g]
    L = cache_t.shape[2]

    def col(c):
        return pl.BlockSpec((T_PAD, 256), lambda b: (b, c // 256 + g))

    def par(stack):
        return pl.BlockSpec((None, 1, 256), lambda b: (stack * 3 + g, 0, 0))

    return pl.pallas_call(
        functools.partial(_attn_sample_kernel, L=L, dil=dil, t_true=t_true, head_base=4 * g),
        grid=(B,),
        in_specs=[col(COL_AQ), col(COL_AK), col(COL_AV), par(l), par(l), pl.BlockSpec(memory_space=pltpu.SMEM),
                  pl.BlockSpec((None, 512, L), lambda b: (l * B + b, 0, 0)),
                  pl.BlockSpec(memory_space=pl.ANY)],
        out_specs=[
            pl.BlockSpec((T_PAD, 256), lambda b: (b, 0)),
            pl.BlockSpec((T_PAD, 256), lambda b: (b, 0)),
            pl.BlockSpec((None, 512, L), lambda b: (l * B + b, 0, 0)),
        ],
        out_shape=[
            jax.ShapeDtypeStruct((B * T_PAD, 256), F32),
            jax.ShapeDtypeStruct((B * T_PAD, 256), F32),
            jax.ShapeDtypeStruct((DEPTH * B, 512, L), F32),
        ],
        input_output_aliases={7: 2},
        compiler_params=_cparams(("parallel",)),
    )(P, P, P, q_gain, k_gain, slopes, cache_t, win_prev)


def _merge_groups(o0, l0, o1, l1, o2, l2):
    a0, a1, a2 = l0[...], l1[...], l2[...]
    m = jnp.maximum(jnp.maximum(a0, a1), a2)
    e0, e1, e2 = jnp.exp(a0 - m), jnp.exp(a1 - m), jnp.exp(a2 - m)
    den = e0 + e1 + e2
    return (e0 / den) * o0[...] + (e1 / den) * o1[...] + (e2 / den) * o2[...]


def _attn_merge_kernel(o0, l0, o1, l1, o2, l2, slot_ref, y_ref, *, n_tiles, slot_rows):
    @pl.when(pl.program_id(0) < n_tiles)
    def _():
        y_ref[...] = _merge_groups(o0, l0, o1, l1, o2, l2).astype(BF16)

    @pl.when(pl.program_id(0) == n_tiles)
    def _():
        y_ref[0:slot_rows, :] = slot_ref[...]


def attn_merge(parts, tm, slot):
    M = parts[0].shape[0]
    n_tiles = M // tm
    slot_rows = slot.shape[0]
    spec = pl.BlockSpec((tm, 256), lambda m: (jnp.minimum(m, n_tiles - 1), 0))
    return pl.pallas_call(
        functools.partial(_attn_merge_kernel, n_tiles=n_tiles, slot_rows=slot_rows),
        grid=(n_tiles + 1,),
        in_specs=[spec] * 6 + [pl.BlockSpec((slot_rows, 256), lambda m: (0, 0))],
        out_specs=pl.BlockSpec((tm, 256), lambda m: (m, 0)),
        out_shape=jax.ShapeDtypeStruct((M + slot_rows, 256), BF16),
        compiler_params=_cparams(("arbitrary",)),
    )(*parts, slot)


def _sample_pack_kernel(sel_ref, ya_ref, yd_ref, yc_ref, o0, l0, o1, l1, o2, l2, ya_o, yb_o, yc_o, yd_o):
    sel = sel_ref[...]

    def pack(y):
        return jnp.dot(sel, y, preferred_element_type=F32).astype(BF16)

    ya_o[...] = pack(ya_ref[...])
    yd_o[...] = pack(yd_ref[...])
    yc_o[...] = pack(yc_ref[...])
    yb_o[...] = pack(_merge_groups(o0, l0, o1, l1, o2, l2).astype(BF16))


def sample_pack(ya, yd, yc, parts, n_seq, t_true):
    rows = n_seq * T_PAD
    sel = np.zeros((SAMPLE_SLOT, rows), np.float32)
    for b in range(n_seq):
        for t in range(t_true):
            sel[b * t_true + t, b * T_PAD + t] = 1.0
    out = lambda w: jax.ShapeDtypeStruct((SAMPLE_SLOT, w), BF16)
    return pl.pallas_call(
        _sample_pack_kernel,
        out_shape=[out(512), out(256), out(512), out(512)],
        compiler_params=pltpu.CompilerParams(vmem_limit_bytes=VMEM_LIMIT),
    )(jnp.asarray(sel, BF16), ya, yd, yc, *parts)


GATE_SRC0 = 5904
assert GATE_SRC0 % SRC_ALIGN == 0 and D_MODEL % SRC_ALIGN == 0


def _merge_kernel(xn_ref, ya_ref, yb_ref, yc_ref, yd_ref, g0, g1, g2, g3, u0, u1, u2, u3, o_ref,
                  gs0, gs1, gs2, gs3, us0, us1, us2, us3):
    branches = ((ya_ref, g0, u0, gs0, us0), (yb_ref, g1, u1, gs1, us1),
                (yc_ref, g2, u2, gs2, us2), (yd_ref, g3, u3, gs3, us3))

    @pl.when(pl.program_id(1) == 0)
    def _():
        for _, g_ref, u_ref, g_s, u_s in branches:
            g_s[...] = g_ref[...].astype(BF16)
            u_s[...] = u_ref[...].astype(BF16)

    xn = xn_ref[...]
    acc = None
    for y_ref, _, _, g_s, u_s in branches:
        gate = jax.nn.sigmoid(_dot_nt(xn, g_s[...]))
        term = gate * jnp.dot(y_ref[...], u_s[...], preferred_element_type=F32)
        acc = term if acc is None else acc + term
    o_ref[...] = acc.astype(BF16)


def merge(xn, ya, yb, yc, yd, w_in_t, ups, l, tm):
    M = xn.shape[0]
    assert M % tm == 0
    tn = PREP_ROWS
    widths = (512, 256, 512, 512)

    def act(width):
        return pl.BlockSpec((tm, width), lambda n, m: (m, 0))

    def gate(b):
        base = (GATE_SRC0 + b * D_MODEL) // SRC_ALIGN
        return _w_in_rows(l, lambda n, m: base + n * (tn // SRC_ALIGN))

    def up(width):
        return pl.BlockSpec((None, width, tn), lambda n, m: (l, 0, n))

    return pl.pallas_call(
        _merge_kernel,
        grid=(D_MODEL // tn, M // tm),
        in_specs=[act(D_MODEL)] + [act(w) for w in widths] + [gate(b) for b in range(N_BRANCH)]
                 + [up(w) for w in widths],
        out_specs=pl.BlockSpec((tm, tn), lambda n, m: (m, n)),
        out_shape=jax.ShapeDtypeStruct((M, D_MODEL), BF16),
        scratch_shapes=[pltpu.VMEM((tn, D_MODEL), BF16)] * N_BRANCH + [pltpu.VMEM((w, tn), BF16) for w in widths],
        compiler_params=_cparams(("parallel", "arbitrary")),
    )(xn, ya, yb, yc, yd, w_in_t, w_in_t, w_in_t, w_in_t, *ups)


def _layer(x, xn, W, l, T, mixers):
    h = ffn_up(xn, W['ff1_gate'], W['ff1_up'], l, T['up_m'], T['up_n'])
    x, xn = matmul_res(x, h, W['ff1_down'], l, 0.5, T['down_m'], W['n_mix'], l)
    P = mix_in(xn, W['w_in_t'], l, T['mix_m'])
    (ya, yb, yc, yd), states = mixers(P, l)
    mg = merge(xn, ya, yb, yc, yd, W['w_in_t'], (W['up_a'], W['up_b'], W['up_c'], W['up_d']), l, T['merge_m'])
    x, xn = matmul_res(x, mg, W['w_out'], l, 1.0, T['out_m'], W['n_ff2'], l)
    h = ffn_up(xn, W['ff2_gate'], W['ff2_up'], l, T['up_m'], T['up_n'])
    if l + 1 < DEPTH:
        x, xn = matmul_res(x, h, W['ff2_down'], l, 0.5, T['down_m'], W['n_ff1'], l + 1)
    else:
        x, xn = matmul_res(x, h, W['ff2_down'], l, 0.5, T['down_m'], out_split=True)
    return x, xn, states


SAMPLE_SLOT = 128
TILES = dict(up_m=2080, up_n=512, mix_m=1040, down_m=256, out_m=512, merge_m=1040)


def kernel(x_prompt, x_sample, state_conv, cache_w128_kv, cache_w512_kv, cache_w2048_kv, state_gla, state_pool,
           norm_ff1, ff1_gate, ff1_up, ff1_down, norm_mix, w_in, conv_w, attn_q_gain, attn_k_gain,
           gla_w_a2, gla_b_a, gla_norm, pool_w, pool_scale, w_up_conv, w_up_attn, w_up_gla, w_up_pool, w_out,
           norm_ff2, ff2_gate, ff2_up, ff2_down):
    BP, S, _ = x_prompt.shape
    BS, TS, _ = x_sample.shape
    caches = (cache_w128_kv, cache_w512_kv, cache_w2048_kv)

    W = {
        'n_ff1': norm_ff1.reshape(DEPTH, 1, D_MODEL), 'n_mix': norm_mix.reshape(DEPTH, 1, D_MODEL),
        'n_ff2': norm_ff2.reshape(DEPTH, 1, D_MODEL),
        'ff1_gate': ff1_gate, 'ff1_up': ff1_up, 'ff1_down': ff1_down,
        'ff2_gate': ff2_gate, 'ff2_up': ff2_up, 'ff2_down': ff2_down,
        'w_in_t': jnp.transpose(w_in, (0, 2, 1)),
        'up_a': w_up_conv, 'up_b': w_up_attn, 'up_c': w_up_gla, 'up_d': w_up_pool, 'w_out': w_out,
    }
    wa_pad = jnp.pad(gla_w_a2, ((0, 0), (0, LANES - gla_w_a2.shape[1]), (0, 0))).astype(BF16)
    b_a = gla_b_a.reshape(DEPTH, 1, 256)
    gn = gla_norm.reshape(DEPTH, 1, LANES)
    pool_wb = pool_w.astype(BF16)
    pool_sc = pool_scale.reshape(DEPTH, 1, W_POOL)
    i = jnp.arange(1, N_ATTN_HEADS + 1, dtype=F32)
    slopes = jnp.exp2(-8.0 * i / N_ATTN_HEADS)
    qg6 = attn_q_gain.reshape(DEPTH * 6, 1, LANES)
    kg6 = attn_k_gain.reshape(DEPTH * 6, 1, LANES)
    qg3 = attn_q_gain.reshape(DEPTH * 3, 1, 256)
    kg3 = attn_k_gain.reshape(DEPTH * 3, 1, 256)

    zc = jnp.zeros((BP, CONV_HIST, W_CONV), F32)
    zp = jnp.zeros((BP, POOL_HIST16, W_POOL), F32)
    zs = jnp.zeros((BP, 256, LANES), F32)

    win_p = [jnp.zeros((DEPTH * BP, 2, 256, min(w, S)), F32) for w, _ in ATTN_GROUPS]
    win_s = [jnp.zeros((DEPTH * BS, 512, cc.shape[2]), F32) for cc in caches]

    MP, MS = BP * S, BS * TS
    assert MS <= SAMPLE_SLOT

    def prompt_mixers(P, l, slot):
        ya, yd, cnew, pnew = conv_pool(P, BP, S, zc, zp, conv_w, pool_wb, pool_sc, l, 512, 512, 0,
                                       slot=(slot[0], slot[3]))
        yc, sfin = gla(P, BP, S, wa_pad, b_a, gn, zs, l, 512, GLA_CHUNK, GLA_CHUNK, slot=slot[2])
        parts = []
        for g in range(3):
            o, lse, win_p[g] = attn_prompt(P, BP, S, qg6, kg6, slopes, l, g, win_p[g])
            parts += [o.reshape(MP, 256), lse.reshape(MP, 256)]
        yb = attn_merge(parts, 1024, slot=slot[1])
        return (ya, yb, yc, yd), (cnew, sfin.reshape(BP, 4, GLA_DK, LANES), pnew[:, 1:])

    sconv8 = jnp.pad(state_conv, ((0, 0), (0, 0), (CONV_HIST - 2, 0), (0, 0)))
    spool16 = jnp.pad(state_pool, ((0, 0), (0, 0), (1, 0), (0, 0)))
    sgla = state_gla.reshape(DEPTH, BS, 256, LANES)
    cviews = [jnp.transpose(cc, (0, 1, 3, 4, 5, 2)).reshape(DEPTH * BS, 512, cc.shape[2]) for cc in caches]

    def sample_mixers(P, l):
        MSP = BS * T_PAD
        Ps = jnp.pad(P[MP:MP + MS].reshape(BS, TS, N_MIX), ((0, 0), (0, T_PAD - TS), (0, 0))).reshape(MSP, N_MIX)
        ya, yd, cnew, pnew = conv_pool(Ps, BS, T_PAD, sconv8[l], spool16[l], conv_w, pool_wb, pool_sc, l,
                                       T_PAD, TS, PAST_LEN)
        yc, sfin = gla(Ps, BS, T_PAD, wa_pad, b_a, gn, sgla[l], l, T_PAD, T_PAD, TS)
        parts = []
        for g in range(3):
            o, lse, win_s[g] = attn_sample(Ps, BS, qg3, kg3, slopes, cviews[g], l, g, TS, win_s[g])
            parts += [o, lse]
        ys = sample_pack(ya, yd, yc, parts, BS, TS)
        return ys, (cnew, sfin.reshape(BS, 4, GLA_DK, LANES), pnew[:, 1:])

    def mixers(P, l):
        ys_s, st_sample = sample_mixers(P, l)
        ys, st_prompt = prompt_mixers(P, l, ys_s)
        return ys, (st_prompt, st_sample)

    x = (x_prompt.reshape(MP, D_MODEL), jnp.pad(x_sample.reshape(MS, D_MODEL), ((0, SAMPLE_SLOT - MS), (0, 0))))
    xn = norm(x[0], x[1], W['n_ff1'], 0, 1024)
    st_p, st_s = [], []
    for l in range(DEPTH):
        x, xn, (sp, ss) = _layer(x, xn, W, l, TILES, mixers)
        st_p.append(sp)
        st_s.append(ss)
    yp, ys = x[0], x[1][:MS]

    def stack(sts, f):
        return jnp.stack([f(s) for s in sts])

    def window_out(buf, B):
        n = buf.shape[-1]
        return jnp.transpose(buf.reshape(DEPTH, B, 2, 4, HEAD_DIM, n), (0, 1, 5, 2, 3, 4))

    outs = [yp.reshape(BP, S, D_MODEL), ys.reshape(BS, TS, D_MODEL),
            stack(st_p, lambda s: s[0]), stack(st_s, lambda s: s[0])]
    for g in range(3):
        outs += [window_out(win_p[g], BP), window_out(win_s[g], BS)]
    outs += [stack(st_p, lambda s: s[1]), stack(st_s, lambda s: s[1]),
             stack(st_p, lambda s: s[2]), stack(st_s, lambda s: s[2])]
    return tuple(outs)
```

```python
import functools

import numpy as np
import jax
import jax.numpy as jnp
from jax import lax
from jax.experimental import pallas as pl
from jax.experimental.pallas import tpu as pltpu

BF16 = jnp.bfloat16
F32 = jnp.float32

D_MODEL = 2048
DEPTH = 4
PAST_LEN = 16384
D_FF = 5504
EPS = 1e-6
N_BRANCH = 4
W_CONV = 512
ATTN_GROUPS = ((128, 1), (512, 4), (2048, 16))
N_ATTN_HEADS = 12
HEAD_DIM = 64
GLA_DK = 64
GLA_TAU = 16.0
GLA_CHUNK = 64
POOL_WINDOWS = (2, 4, 8, 16)
W_POOL = 512
NK = 128
LANES = 128
NEG = -1e30

COL_CB, COL_CC, COL_CH, COL_GV, COL_GR, COL_POOL = 0, 512, 1024, 1536, 2048, 2560
COL_AQ, COL_AK, COL_AV = 3072, 3840, 4608
COL_GQ, COL_GK, COL_LR = 5376, 5632, 5888
N_MIX = 6144
N_IN = 14096
PREP_ROWS = 256
MIX_TILE_SRC = ([256 * j for j in range(6)] + [4352, 4608, 4864, 5120, 5392, 5648]
                + [1536 + 256 * j for j in range(9)] + [3840, 4096, 5376])
SRC_ALIGN = 16
assert all(r % SRC_ALIGN == 0 for r in MIX_TILE_SRC)

VMEM_LIMIT = 56 * 1024 * 1024


def _cparams(sem):
    return pltpu.CompilerParams(dimension_semantics=sem, vmem_limit_bytes=VMEM_LIMIT)


def _rms(x):
    return x * lax.rsqrt(jnp.mean(x * x, axis=-1, keepdims=True) + EPS)


def _row_tiles(M, tm):
    n = pl.cdiv(M, tm)
    return n, M - (n - 1) * tm


def _split_rows_specs(tm, n_tiles, slot):
    return (pl.BlockSpec((tm, D_MODEL), lambda m: (jnp.minimum(m, n_tiles - 2), 0)),
            pl.BlockSpec((slot, D_MODEL), lambda m: (0, 0)))


def _norm_kernel(xp_ref, xs_ref, g_ref, xn_ref, *, tiles, tm):
    n, tail = tiles
    m = pl.program_id(0)

    @pl.when(m < n - 1)
    def _():
        xn_ref[...] = (_rms(xp_ref[...]) * g_ref[...]).astype(BF16)

    @pl.when(m == n - 1)
    def _():
        xn_ref[0:tail, :] = (_rms(xs_ref[...]) * g_ref[...]).astype(BF16)


def norm(xp, xs, gain, l, tm):
    MP, slot = xp.shape[0], xs.shape[0]
    assert MP % tm == 0 and slot < tm
    tiles = (MP // tm + 1, slot)
    return pl.pallas_call(
        functools.partial(_norm_kernel, tiles=tiles, tm=tm),
        grid=(tiles[0],),
        in_specs=[*_split_rows_specs(tm, tiles[0], slot),
                  pl.BlockSpec((None, 1, D_MODEL), lambda m: (l, 0, 0))],
        out_specs=pl.BlockSpec((tm, D_MODEL), lambda m: (m, 0)),
        out_shape=jax.ShapeDtypeStruct((MP + slot, D_MODEL), BF16),
        compiler_params=_cparams(("parallel",)),
    )(xp, xs, gain)


SUB_ROWS = 208


def _sub_blocks(tm):
    assert tm % SUB_ROWS == 0
    return [slice(r0, r0 + SUB_ROWS) for r0 in range(0, tm, SUB_ROWS)]


def _ffn_up_kernel(xn_ref, wg_ref, wu_ref, h_ref, wg_s, wu_s, *, tm):
    @pl.when(pl.program_id(1) == 0)
    def _():
        wg_s[...] = wg_ref[...].astype(BF16)
        wu_s[...] = wu_ref[...].astype(BF16)

    for r in _sub_blocks(tm):
        xn = xn_ref[r, :]
        a = jnp.dot(xn, wg_s[...], preferred_element_type=F32)
        b = jnp.dot(xn, wu_s[...], preferred_element_type=F32)
        h_ref[r, :] = (a * jax.nn.sigmoid(a) * b).astype(BF16)


def ffn_up(xn, wg, wu, l, tm, tn):
    M = xn.shape[0]
    assert M % tm == 0
    return pl.pallas_call(
        functools.partial(_ffn_up_kernel, tm=tm),
        grid=(pl.cdiv(D_FF, tn), M // tm),
        in_specs=[
            pl.BlockSpec((tm, D_MODEL), lambda n, m: (m, 0)),
            pl.BlockSpec((None, D_MODEL, tn), lambda n, m: (l, 0, n)),
            pl.BlockSpec((None, D_MODEL, tn), lambda n, m: (l, 0, n)),
        ],
        out_specs=pl.BlockSpec((tm, tn), lambda n, m: (m, n)),
        out_shape=jax.ShapeDtypeStruct((M, D_FF), BF16),
        scratch_shapes=[pltpu.VMEM((D_MODEL, tn), BF16), pltpu.VMEM((D_MODEL, tn), BF16)],
        compiler_params=_cparams(("parallel", "arbitrary")),
    )(xn, wg, wu)


W_CHUNKS = 8


def _matmul_res_kernel(*refs, scale, with_norm, tiles, tm, x_split, out_split, ck):
    refs = list(refs)
    x_refs = [refs.pop(0) for _ in range(2 if x_split else 1)]
    a_ref, w_ref = refs.pop(0), refs.pop(0)
    g_ref = refs.pop(0) if with_norm else None
    o_refs = [refs.pop(0) for _ in range(2 if out_split else 1)]
    xn_ref = refs.pop(0) if with_norm else None
    w_s = refs.pop(0)
    n, tail = tiles
    step = pl.program_id(0)
    m = step - W_CHUNKS

    @pl.when(step < W_CHUNKS)
    def _():
        w_s[pl.ds(pl.multiple_of(step * ck, 16), ck), :] = w_ref[...].astype(BF16)

    def rows_fn(rows, last):
        x_ref = x_refs[-1] if last else x_refs[0]
        o_ref = o_refs[-1] if last else o_refs[0]
        y = x_ref[rows, :] + scale * jnp.dot(a_ref[rows, :], w_s[...], preferred_element_type=F32)
        if with_norm:
            xn_ref[rows, :] = (_rms(y) * g_ref[...]).astype(BF16)
        o_ref[rows, :] = y

    pl.when((m >= 0) & (m < n - 1))(lambda: rows_fn(slice(0, tm), False))
    pl.when(m == n - 1)(lambda: rows_fn(slice(0, tail), True))


def matmul_res(x, a, w, l, scale, tm, next_gain=None, next_l=0, out_split=False):
    M, K = a.shape
    with_norm = next_gain is not None
    x_split = isinstance(x, tuple)
    tiles = _row_tiles(M, tm)
    assert tiles[1] == SAMPLE_SLOT and (M - SAMPLE_SLOT) % tm == 0 and K % (16 * W_CHUNKS) == 0
    ck = K // W_CHUNKS

    def tile(step):
        return jnp.maximum(step - W_CHUNKS, 0)

    row = pl.BlockSpec((tm, D_MODEL), lambda s: (tile(s), 0))
    split = (pl.BlockSpec((tm, D_MODEL), lambda s: (jnp.minimum(tile(s), tiles[0] - 2), 0)),
             pl.BlockSpec((SAMPLE_SLOT, D_MODEL), lambda s: (0, 0)))
    in_specs = (list(split) if x_split else [row]) + [
        pl.BlockSpec((tm, K), lambda s: (tile(s), 0)),
        pl.BlockSpec((None, ck, D_MODEL), lambda s: (l, jnp.minimum(s, W_CHUNKS - 1), 0))]
    args = (list(x) if x_split else [x]) + [a, w]
    if out_split:
        out_specs = list(split)
        out_shape = [jax.ShapeDtypeStruct((M - SAMPLE_SLOT, D_MODEL), F32),
                     jax.ShapeDtypeStruct((SAMPLE_SLOT, D_MODEL), F32)]
    else:
        out_specs, out_shape = [row], [jax.ShapeDtypeStruct((M, D_MODEL), F32)]
    if with_norm:
        in_specs.append(pl.BlockSpec((None, 1, D_MODEL), lambda s: (next_l, 0, 0)))
        args.append(next_gain)
        out_specs.append(row)
        out_shape.append(jax.ShapeDtypeStruct((M, D_MODEL), BF16))
    res = pl.pallas_call(
        functools.partial(_matmul_res_kernel, scale=scale, with_norm=with_norm, tiles=tiles, tm=tm,
                          x_split=x_split, out_split=out_split, ck=ck),
        grid=(W_CHUNKS + tiles[0],),
        in_specs=in_specs,
        out_specs=out_specs,
        out_shape=out_shape,
        scratch_shapes=[pltpu.VMEM((K, D_MODEL), BF16)],
        compiler_params=_cparams(("arbitrary",)),
    )(*args)
    y = tuple(res[:2]) if out_split else res[0]
    return y, (res[-1] if with_norm else None)


def _dot_nt(a, b):
    return lax.dot_general(a, b, (((1,), (1,)), ((), ())), preferred_element_type=F32)


def _w_in_rows(l, row_of):
    return pl.BlockSpec((None, pl.Element(PREP_ROWS), pl.Element(D_MODEL)),
                        lambda *idx: (l, row_of(*idx) * SRC_ALIGN, 0))


MIX_SLABS = 4


def _mix_in_kernel(src_ref, xn_ref, *rest):
    del src_ref
    w_refs, (p_ref, w_s) = rest[:MIX_SLABS], rest[MIX_SLABS:]

    @pl.when(pl.program_id(1) == 0)
    def _():
        for j, w_ref in enumerate(w_refs):
            w_s[j * PREP_ROWS:(j + 1) * PREP_ROWS, :] = w_ref[...].astype(BF16)

    p_ref[...] = _dot_nt(xn_ref[...], w_s[...])


def mix_in(xn, w_in_t, l, tm):
    M = xn.shape[0]
    assert M % tm == 0 and w_in_t.shape[1] == N_IN
    tn = MIX_SLABS * PREP_ROWS
    src = jnp.asarray(MIX_TILE_SRC, jnp.int32) // SRC_ALIGN

    def slab(j):
        return _w_in_rows(l, lambda n, m, src: src[MIX_SLABS * n + j])

    return pl.pallas_call(
        _mix_in_kernel,
        grid_spec=pltpu.PrefetchScalarGridSpec(
            num_scalar_prefetch=1,
            grid=(N_MIX // tn, M // tm),
            in_specs=[pl.BlockSpec((tm, D_MODEL), lambda n, m, src: (m, 0))] + [slab(j) for j in range(MIX_SLABS)],
            out_specs=pl.BlockSpec((tm, tn), lambda n, m, src: (m, n)),
            scratch_shapes=[pltpu.VMEM((tn, D_MODEL), BF16)],
        ),
        out_shape=jax.ShapeDtypeStruct((M, N_MIX), F32),
        compiler_params=_cparams(("parallel", "arbitrary")),
    )(src, xn, *([w_in_t] * MIX_SLABS))


CONV_HIST = 8
POOL_HIST16 = 16
POOL_PAD = 8


def _conv_pool_kernel(cb_ref, cc_ref, ch_ref, pin_ref, cbuf_ref, pbuf_ref, cw_ref, pw_ref, ps_ref, *rest,
                      tt, t_true, pos0, n_seq, slot_rows):
    if slot_rows:
        ya_slot_ref, yd_slot_ref, ya_ref, yd_ref = rest[:4]

        @pl.when((pl.program_id(0) == n_seq) & (pl.program_id(1) == 0))
        def _():
            ya_ref[0:slot_rows, :] = ya_slot_ref[...]
            yd_ref[0:slot_rows, :] = yd_slot_ref[...]

        pl.when(pl.program_id(0) < n_seq)(
            lambda: _conv_pool_body(cb_ref, cc_ref, ch_ref, pin_ref, cbuf_ref, pbuf_ref, cw_ref, pw_ref, ps_ref,
                                    *rest[2:], tt=tt, t_true=t_true, pos0=pos0))
    else:
        _conv_pool_body(cb_ref, cc_ref, ch_ref, pin_ref, cbuf_ref, pbuf_ref, cw_ref, pw_ref, ps_ref, *rest,
                        tt=tt, t_true=t_true, pos0=pos0)


def _conv_pool_body(cb_ref, cc_ref, ch_ref, pin_ref, cbuf_ref, pbuf_ref, cw_ref, pw_ref, ps_ref,
                    ya_ref, yd_ref, cnew_ref, pnew_ref, uext, pext, sb2, sb4, sb8, *, tt, t_true, pos0):
    t = pl.program_id(1)

    @pl.when(t == 0)
    def _():
        uext[0:CONV_HIST, :] = cbuf_ref[...]
        zeros = jnp.zeros((POOL_PAD, W_POOL), F32)
        pext[0:POOL_PAD, :] = zeros
        sb2[0:POOL_PAD, :] = zeros
        sb4[0:POOL_PAD, :] = zeros
        pext[POOL_PAD:POOL_PAD + POOL_HIST16, :] = pbuf_ref[...]

    u = cc_ref[...] * ch_ref[...]
    uext[CONV_HIST:CONV_HIST + tt, :] = u
    w = cw_ref[...]
    z = uext[CONV_HIST - 2:CONV_HIST - 2 + tt, :] * w[0:1, :]
    z = z + uext[CONV_HIST - 1:CONV_HIST - 1 + tt, :] * w[1:2, :]
    z = z + u * w[2:3, :]
    ya_ref[...] = (cb_ref[...] * z).astype(BF16)
    cnew_ref[...] = uext[CONV_HIST + t_true - 2:CONV_HIST + t_true, :]
    uext[0:CONV_HIST, :] = uext[tt:tt + CONV_HIST, :]

    pin = pin_ref[...]
    d0 = POOL_PAD + POOL_HIST16
    n = POOL_HIST16 + tt
    pext[d0:d0 + tt, :] = pin
    rows = slice(POOL_PAD, POOL_PAD + n)

    def back(k):
        return slice(POOL_PAD - k, POOL_PAD - k + n)

    sb2[rows, :] = pext[rows, :] + pext[back(1), :]
    sb4[rows, LANES:] = sb2[rows, LANES:] + sb2[back(2), LANES:]
    sb8[rows, 2 * LANES:] = sb4[rows, 2 * LANES:] + sb4[back(4), 2 * LANES:]
    data = slice(d0, d0 + tt)
    sums = (sb2[data, 0:LANES], sb4[data, LANES:2 * LANES], sb8[data, 2 * LANES:3 * LANES],
            sb8[data, 3 * LANES:] + sb8[d0 - 8:d0 - 8 + tt, 3 * LANES:])
    pos = (pos0 + t * tt + lax.broadcasted_iota(jnp.int32, (tt, 1), 0)).astype(F32)
    for g, win in enumerate(POOL_WINDOWS):
        ls = slice(g * LANES, (g + 1) * LANES)
        cnt = jnp.minimum(float(win), pos + 1.0)
        d = sums[g] / cnt - pin[:, ls]
        y = jnp.dot(d.astype(BF16), pw_ref[g], preferred_element_type=F32) * ps_ref[:, ls]
        yd_ref[:, ls] = y.astype(BF16)
    pnew_ref[...] = pext[POOL_PAD + t_true:POOL_PAD + t_true + POOL_HIST16, :]
    pext[POOL_PAD:d0, :] = pext[POOL_PAD + tt:d0 + tt, :]


def _seq_tiles(B, nt, with_slot):
    if not with_slot:
        return B, (lambda b, t: b * nt + t), (lambda b, t: b * nt + t), (lambda b: b)
    return (B + 1,
            lambda b, t: jnp.where(b < B, b * nt + t, B * nt - 1),
            lambda b, t: jnp.where(b < B, b * nt + t, B * nt),
            lambda b: jnp.minimum(b, B - 1))


def conv_pool(P, B, T, cbuf8, pbuf16, conv_w, pool_w, pool_scale, l, tt, t_true, pos0, slot=None):
    nt = T // tt
    slot_rows = slot[0].shape[0] if slot else 0
    rows_b, in_blk, out_blk, seq_blk = _seq_tiles(B, nt, bool(slot))

    def col(c):
        return pl.BlockSpec((tt, 512), lambda b, t: (in_blk(b, t), c // 512))

    slot_spec = pl.BlockSpec((slot_rows, 512), lambda b, t: (0, 0))
    return pl.pallas_call(
        functools.partial(_conv_pool_kernel, tt=tt, t_true=t_true, pos0=pos0, n_seq=B, slot_rows=slot_rows),
        grid=(rows_b, nt),
        in_specs=[
            col(COL_CB), col(COL_CC), col(COL_CH), col(COL_POOL),
            pl.BlockSpec((None, CONV_HIST, 512), lambda b, t: (seq_blk(b), 0, 0)),
            pl.BlockSpec((None, POOL_HIST16, 512), lambda b, t: (seq_blk(b), 0, 0)),
            pl.BlockSpec((None, 3, 512), lambda b, t: (l, 0, 0)),
            pl.BlockSpec((None, 4, LANES, LANES), lambda b, t: (l, 0, 0, 0)),
            pl.BlockSpec((None, 1, 512), lambda b, t: (l, 0, 0)),
        ] + ([slot_spec, slot_spec] if slot else []),
        out_specs=[
            pl.BlockSpec((tt, 512), lambda b, t: (out_blk(b, t), 0)),
            pl.BlockSpec((tt, 512), lambda b, t: (out_blk(b, t), 0)),
            pl.BlockSpec((None, 2, 512), lambda b, t: (seq_blk(b), 0, 0)),
            pl.BlockSpec((None, POOL_HIST16, 512), lambda b, t: (seq_blk(b), 0, 0)),
        ],
        out_shape=[
            jax.ShapeDtypeStruct((B * T + slot_rows, 512), BF16),
            jax.ShapeDtypeStruct((B * T + slot_rows, 512), BF16),
            jax.ShapeDtypeStruct((B, 2, 512), F32),
            jax.ShapeDtypeStruct((B, POOL_HIST16, 512), F32),
        ],
        scratch_shapes=[pltpu.VMEM((CONV_HIST + tt, 512), F32)]
                       + [pltpu.VMEM((POOL_PAD + POOL_HIST16 + tt, 512), F32)] * 4,
        compiler_params=_cparams(("arbitrary", "arbitrary")),
    )(P, P, P, P, cbuf8, pbuf16, conv_w, pool_w, pool_scale, *(slot or ()))


GLA_UNROLL = 8

def _gla_kernel(q_ref, k_ref, v_ref, r_ref, lr_ref, wa_ref, ba_ref, gn_ref, s0_ref, *rest,
                tt, C, c_true, n_seq, slot_rows):
    if slot_rows:
        yc_slot_ref, yc_ref = rest[:2]

        @pl.when((pl.program_id(0) == n_seq) & (pl.program_id(1) == 0))
        def _():
            yc_ref[0:slot_rows, :] = yc_slot_ref[...]

        pl.when(pl.program_id(0) < n_seq)(
            lambda: _gla_body(q_ref, k_ref, v_ref, r_ref, lr_ref, wa_ref, ba_ref, gn_ref, s0_ref, *rest[1:],
                              tt=tt, C=C, c_true=c_true))
    else:
        _gla_body(q_ref, k_ref, v_ref, r_ref, lr_ref, wa_ref, ba_ref, gn_ref, s0_ref, *rest,
                  tt=tt, C=C, c_true=c_true)


def _gla_body(q_ref, k_ref, v_ref, r_ref, lr_ref, wa_ref, ba_ref, gn_ref, s0_ref,
              yc_ref, sfin_ref, s_scr, b_scr, sprev_scr, *, tt, C, c_true):
    assert c_true == C or tt == C
    t = pl.program_id(1)

    @pl.when(t == 0)
    def _():
        s_scr[...] = s0_ref[...]

    row = lax.broadcasted_iota(jnp.int32, (C, 1), 0)
    lane = lax.broadcasted_iota(jnp.int32, (1, LANES), 1)
    head0 = lane < GLA_DK
    srow = lax.broadcasted_iota(jnp.int32, (LANES, 1), 0)
    eye = srow == lane
    gn = gn_ref[...]
    nchunk = tt // C

    def chunk_rows(c):
        return pl.ds(pl.multiple_of(c * C, C), C)

    def v_head(rows, h):
        v_h = v_ref[rows, h * LANES:(h + 1) * LANES]
        if c_true < C:
            v_h = jnp.where(row < c_true, v_h, 0.0)
        return v_h

    la = jnp.dot(lr_ref[...].astype(BF16), wa_ref[...], preferred_element_type=F32) + ba_ref[...]
    la = jax.nn.log_sigmoid(la) / GLA_TAU
    if c_true < C:
        la = jnp.where(row < c_true, la, 0.0)
    cb = 2 * C if nchunk % 2 == 0 else C
    bi = lax.broadcasted_iota(jnp.int32, (cb, 1), 0)
    bj = lax.broadcasted_iota(jnp.int32, (1, cb), 1)
    ones_tri = (((bi >= C) == (bj >= C)) & (bj <= bi)).astype(BF16)
    for blk in range(tt // cb):
        part = la[blk * cb:(blk + 1) * cb, :]
        acc = None
        for _ in range(3):
            term = part.astype(BF16)
            d = jnp.dot(ones_tri, term, preferred_element_type=F32)
            acc = d if acc is None else acc + d
            part = part - term.astype(F32)
        b_scr[blk * cb:(blk + 1) * cb, :] = acc

    def by_head(x):
        return jnp.concatenate([jnp.where(head0, x, 0.0), jnp.where(head0, 0.0, x)], axis=0)

    def v_pair(rows, p):
        return jnp.concatenate([v_head(rows, 2 * p), v_head(rows, 2 * p + 1)], axis=0).astype(BF16)

    def scan(c, carry):
        rows = chunk_rows(c)
        b = b_scr[rows, :]
        b_last = b[C - 1:C, :]
        kend = k_ref[rows, :] * jnp.exp(b_last - b)
        for p in range(2):
            ls = slice(p * LANES, (p + 1) * LANES)
            s_p = s_scr[ls, :]
            sprev_scr[c, ls, :] = s_p.astype(BF16)
            dec = jnp.exp(jnp.sum(jnp.where(eye, b_last[:, ls], 0.0), axis=1, keepdims=True))
            ds = lax.dot_general(by_head(kend[:, ls]).astype(BF16), v_pair(rows, p), (((0,), (0,)), ((), ())),
                                 preferred_element_type=F32)
            s_scr[ls, :] = dec * s_p + ds
        return carry

    lax.fori_loop(0, nchunk, scan, 0, unroll=min(GLA_UNROLL, nchunk))

    ri = lax.broadcasted_iota(jnp.int32, (2 * C, 1), 0)
    ci = lax.broadcasted_iota(jnp.int32, (1, 2 * C), 1)
    same_head_causal = ((ri >= C) == (ci >= C)) & ((ri % C) >= (ci % C))

    def emit(c, carry):
        rows = chunk_rows(c)
        b = b_scr[rows, :]
        qt = q_ref[rows, :] * (GLA_DK ** -0.5) * jnp.exp(b)
        kt = k_ref[rows, :] * jnp.exp(-b)
        for p in range(2):
            ls = slice(p * LANES, (p + 1) * LANES)
            qm = by_head(qt[:, ls]).astype(BF16)
            kt2 = jnp.concatenate([kt[:, ls], kt[:, ls]], axis=0).astype(BF16)
            a = jnp.where(same_head_causal, _dot_nt(qm, kt2), 0.0)
            o = jnp.dot(a.astype(BF16), v_pair(rows, p), preferred_element_type=F32)
            o = o + jnp.dot(qm, sprev_scr[c, ls, :], preferred_element_type=F32)
            on = _rms(o) * gn
            for hh in range(2):
                hs = slice((2 * p + hh) * LANES, (2 * p + hh + 1) * LANES)
                r_h = r_ref[rows, hs]
                yc_ref[rows, hs] = (on[hh * C:(hh + 1) * C, :] * (r_h * jax.nn.sigmoid(r_h))).astype(BF16)
        return carry

    lax.fori_loop(0, nchunk, emit, 0, unroll=min(GLA_UNROLL, nchunk))
    sfin_ref[...] = s_scr[...]


def gla(P, B, T, wa_pad, b_a, gla_norm, s0, l, tt, C, c_true, slot=None):
    nt = T // tt
    slot_rows = slot.shape[0] if slot is not None else 0
    rows_b, in_blk, out_blk, seq_blk = _seq_tiles(B, nt, slot is not None)
    return pl.pallas_call(
        functools.partial(_gla_kernel, tt=tt, C=C, c_true=c_true, n_seq=B, slot_rows=slot_rows),
        grid=(rows_b, nt),
        in_specs=[
            pl.BlockSpec((tt, 256), lambda b, t: (in_blk(b, t), COL_GQ // 256)),
            pl.BlockSpec((tt, 256), lambda b, t: (in_blk(b, t), COL_GK // 256)),
            pl.BlockSpec((tt, 512), lambda b, t: (in_blk(b, t), COL_GV // 512)),
            pl.BlockSpec((tt, 512), lambda b, t: (in_blk(b, t), COL_GR // 512)),
            pl.BlockSpec((tt, LANES), lambda b, t: (in_blk(b, t), COL_LR // LANES)),
            pl.BlockSpec((None, LANES, 256), lambda b, t: (l, 0, 0)),
            pl.BlockSpec((None, 1, 256), lambda b, t: (l, 0, 0)),
            pl.BlockSpec((None, 1, LANES), lambda b, t: (l, 0, 0)),
            pl.BlockSpec((None, 256, LANES), lambda b, t: (seq_blk(b), 0, 0)),
        ] + ([pl.BlockSpec((slot_rows, 512), lambda b, t: (0, 0))] if slot is not None else []),
        out_specs=[
            pl.BlockSpec((tt, 512), lambda b, t: (out_blk(b, t), 0)),
            pl.BlockSpec((None, 256, LANES), lambda b, t: (seq_blk(b), 0, 0)),
        ],
        out_shape=[jax.ShapeDtypeStruct((B * T + slot_rows, 512), BF16),
                   jax.ShapeDtypeStruct((B, 256, LANES), F32)],
        scratch_shapes=[pltpu.VMEM((256, LANES), F32), pltpu.VMEM((tt, 256), F32),
                        pltpu.VMEM((tt // C, 256, LANES), BF16)],
        compiler_params=_cparams(("arbitrary", "arbitrary")),
    )(P, P, P, P, P, wa_pad, b_a, gla_norm, s0, *(() if slot is None else (slot,)))


def _head_norm(x, gain, head0):
    x2 = x * x
    s0 = jnp.sum(jnp.where(head0, x2, 0.0), axis=-1, keepdims=True)
    s1 = jnp.sum(jnp.where(head0, 0.0, x2), axis=-1, keepdims=True)
    ms = jnp.where(head0, s0, s1) / float(HEAD_DIM)
    return x * lax.rsqrt(ms + EPS) * gain


def _head_norm_mxu(x, gain, same_head):
    x2 = x * x
    hi = x2.astype(BF16)
    lo = (x2 - hi.astype(F32)).astype(BF16)
    ss = jnp.dot(hi, same_head, preferred_element_type=F32) + jnp.dot(lo, same_head, preferred_element_type=F32)
    return x * lax.rsqrt(ss / float(HEAD_DIM) + EPS) * gain


NORM_ROWS = 512
ATTN_UNROLL = 32
QK_SCALE = 0.125


def _attn_prompt_kernel(q_ref, k_ref, v_ref, qg_ref, kg_ref, sl_ref, win_prev_ref,
                        o_ref, lse_ref, kv_ref, tmp, qd, kd, vd, od, ld, bias, *, S, dil, n_keep, head_base):
    del win_prev_ref
    msub = S // dil
    nb = msub // NK
    kstride = msub + NK
    lane = lax.broadcasted_iota(jnp.int32, (1, LANES), 1)
    head0 = lane < HEAD_DIM
    same_head = ((lax.broadcasted_iota(jnp.int32, (LANES, 1), 0) < HEAD_DIM) == head0).astype(BF16)

    def norm_into_tmp(src_ref, g_ref):
        def body(i, carry):
            rows = pl.ds(pl.multiple_of(i * NORM_ROWS, NORM_ROWS), NORM_ROWS)
            tmp[rows, :] = _head_norm_mxu(src_ref[rows, :], g_ref[...], same_head)
            return carry
        lax.fori_loop(0, S // NORM_ROWS, body, 0, unroll=4)

    norm_into_tmp(q_ref, qg_ref)
    for r in range(dil):
        qd[r * msub:(r + 1) * msub, :] = (tmp[pl.ds(r, msub, stride=dil), :] * QK_SCALE).astype(BF16)
    norm_into_tmp(k_ref, kg_ref)
    kv_ref[0] = tmp[S - n_keep:S, :].T
    kv_ref[1] = v_ref[S - n_keep:S, :].T
    zeros = jnp.zeros((NK, LANES), BF16)
    for r in range(dil):
        kd[r * kstride:r * kstride + NK, :] = zeros
        vd[r * kstride:r * kstride + NK, :] = zeros
        kd[r * kstride + NK:(r + 1) * kstride, :] = tmp[pl.ds(r, msub, stride=dil), :].astype(BF16)
        vd[r * kstride + NK:(r + 1) * kstride, :] = v_ref[pl.ds(r, msub, stride=dil), :].astype(BF16)

    qi = lax.broadcasted_iota(jnp.int32, (NK, 2 * NK), 0)
    ki = lax.broadcasted_iota(jnp.int32, (NK, 2 * NK), 1)
    dist = qi - ki + NK
    band = (dist >= 0) & (dist <= NK)
    alibi = (dist * dil).astype(F32)
    cur = ki >= NK
    for hh in range(2):
        slope = sl_ref[head_base + 2 * pl.program_id(1) + hh]
        bias[2 * hh] = jnp.where(band, -(slope * alibi), NEG)
        bias[2 * hh + 1] = jnp.where(band & cur, -(slope * alibi), NEG)

    def block(j, carry):
        r = j // nb
        n = j - r * nb
        first = jnp.where(n == 0, 1, 0)
        qb = pl.multiple_of(j * NK, NK)
        kb = pl.multiple_of(j * NK + r * NK, NK)
        q2 = qd[pl.ds(qb, NK), :]
        k2 = kd[pl.ds(kb, 2 * NK), :]
        v2 = vd[pl.ds(kb, 2 * NK), :]
        outs, lses = [], []
        for hh in range(2):
            hmask = head0 if hh == 0 else jnp.logical_not(head0)
            qm = jnp.where(hmask, q2, jnp.zeros_like(q2))
            s = lax.dot_general(qm, k2, (((1,), (1,)), ((), ())), preferred_element_type=F32)
            s = s + bias[2 * hh + first]
            m = jnp.max(s, axis=-1, keepdims=True)
            p = jnp.exp(s - m)
            lsum = jnp.sum(p, axis=-1, keepdims=True)
            outs.append(jnp.dot(p.astype(BF16), v2, preferred_element_type=F32) / lsum)
            lses.append(m + jnp.log(lsum))
        od[pl.ds(qb, NK), :] = jnp.where(head0, outs[0], outs[1])
        ld[pl.ds(qb, NK), :] = jnp.where(head0, lses[0], lses[1])
        return carry

    lax.fori_loop(0, dil * nb, block, 0, unroll=ATTN_UNROLL)

    for r in range(dil):
        o_ref[pl.ds(r, msub, stride=dil), :] = od[r * msub:(r + 1) * msub, :]
        lse_ref[pl.ds(r, msub, stride=dil), :] = ld[r * msub:(r + 1) * msub, :]


def attn_prompt(P, B, S, q_gain, k_gain, slopes, l, g, win_prev):
    window, dil = ATTN_GROUPS[g]
    n_keep = min(window, S)

    def col(c):
        return pl.BlockSpec((S, LANES), lambda b, p: (b, c // LANES + 2 * g + p))

    def par(stack):
        return pl.BlockSpec((None, 1, LANES), lambda b, p: (stack * 6 + 2 * g + p, 0, 0))

    return pl.pallas_call(
        functools.partial(_attn_prompt_kernel, S=S, dil=dil, n_keep=n_keep, head_base=4 * g),
        grid=(B, 2),
        in_specs=[col(COL_AQ), col(COL_AK), col(COL_AV), par(l), par(l), pl.BlockSpec(memory_space=pltpu.SMEM),
                  pl.BlockSpec(memory_space=pl.ANY)],
        out_specs=[
            pl.BlockSpec((None, S, LANES), lambda b, p: (b, 0, p)),
            pl.BlockSpec((None, S, LANES), lambda b, p: (b, 0, p)),
            pl.BlockSpec((None, 2, LANES, n_keep), lambda b, p: (l * B + b, 0, p, 0)),
        ],
        out_shape=[
            jax.ShapeDtypeStruct((B, S, 256), F32),
            jax.ShapeDtypeStruct((B, S, 256), F32),
            jax.ShapeDtypeStruct((DEPTH * B, 2, 256, n_keep), F32),
        ],
        input_output_aliases={6: 2},
        scratch_shapes=[
            pltpu.VMEM((S, LANES), F32),
            pltpu.VMEM((S, LANES), BF16),
            pltpu.VMEM((S + dil * NK, LANES), BF16),
            pltpu.VMEM((S + dil * NK, LANES), BF16),
            pltpu.VMEM((S, LANES), F32),
            pltpu.VMEM((S, LANES), F32),
            pltpu.VMEM((4, NK, 2 * NK), F32),
        ],
        compiler_params=_cparams(("parallel", "arbitrary")),
    )(P, P, P, q_gain, k_gain, slopes, win_prev)


T_PAD = 8


def _attn_sample_kernel(q_ref, k_ref, v_ref, qg_ref, kg_ref, sl_ref, cache_ref, win_prev_ref,
                        o_ref, lse_ref, cnew_ref, *, L, dil, t_true, head_base):
    del win_prev_ref
    lane = lax.broadcasted_iota(jnp.int32, (1, LANES), 1)
    head0 = lane < HEAD_DIM
    trow8 = lax.broadcasted_iota(jnp.int32, (T_PAD, LANES), 0)
    lane8 = lax.broadcasted_iota(jnp.int32, (T_PAD, LANES), 1)
    place = ((lane8 == LANES - t_true + trow8) & (trow8 < t_true)).astype(F32)
    is_new = lane >= LANES - t_true

    def rolled_with_new(rows, new):
        shifted = pltpu.roll(cache_ref[rows, :], L - t_true, 1)
        tail = lax.dot_general(new, place, (((0,), (0,)), ((), ())), preferred_element_type=F32,
                               precision=lax.Precision.HIGHEST)
        if L > LANES:
            cnew_ref[rows, 0:L - LANES] = shifted[:, 0:L - LANES]
        cnew_ref[rows, L - LANES:L] = jnp.where(is_new, tail, shifted[:, L - LANES:L])

    trow = lax.broadcasted_iota(jnp.int32, (T_PAD, 1), 0)
    scol = lax.broadcasted_iota(jnp.int32, (1, L), 1)
    dist = L + trow - scol
    valid_c = ((dist & (dil - 1)) == 0) & (dist <= NK * dil)
    alibi_c = dist.astype(F32)

    for p in range(2):
        ls = slice(p * LANES, (p + 1) * LANES)
        qn = _head_norm(q_ref[:, ls], qg_ref[:, ls], head0)
        kn = _head_norm(k_ref[:, ls], kg_ref[:, ls], head0)
        vn = v_ref[:, ls]
        krows = slice(p * LANES, (p + 1) * LANES)
        vrows = slice(256 + p * LANES, 256 + (p + 1) * LANES)
        rolled_with_new(krows, kn)
        rolled_with_new(vrows, vn)
        kc_t = cache_ref[krows, :].astype(BF16)
        vc_t = cache_ref[vrows, :].astype(BF16)
        knr = kn.astype(BF16).astype(F32)
        vnr = vn.astype(BF16).astype(F32)
        outs, lses = [], []
        for hh in range(2):
            hmask = head0 if hh == 0 else jnp.logical_not(head0)
            qm = jnp.where(hmask, qn, 0.0).astype(BF16)
            qmr = qm.astype(F32)
            slope = sl_ref[head_base + 2 * p + hh]
            sc = jnp.dot(qm, kc_t, preferred_element_type=F32)
            sc = sc / float(np.sqrt(HEAD_DIM)) - slope * alibi_c
            sc = jnp.where(valid_c, sc, NEG)
            m = jnp.max(sc, axis=-1, keepdims=True)
            sn = []
            for u in range(t_true):
                du = trow - u
                su = jnp.sum(qmr * knr[u:u + 1, :], axis=-1, keepdims=True) / float(np.sqrt(HEAD_DIM))
                su = su - slope * du.astype(F32)
                su = jnp.where((du >= 0) & ((du & (dil - 1)) == 0), su, NEG)
                sn.append(su)
                m = jnp.maximum(m, su)
            pc = jnp.exp(sc - m)
            lsum = jnp.sum(pc, axis=-1, keepdims=True)
            acc = _dot_nt(pc.astype(BF16), vc_t)
            for u in range(t_true):
                pu = jnp.exp(sn[u] - m)
                lsum = lsum + pu
                acc = acc + pu.astype(BF16).astype(F32) * vnr[u:u + 1, :]
            outs.append(acc / lsum)
            lses.append(m + jnp.log(lsum))
        o_ref[:, ls] = jnp.where(head0, outs[0], outs[1])
        lse_ref[:, ls] = jnp.where(head0, lses[0], lses[1])


def attn_sample(P, B, q_gain, k_gain, slopes, cache_t, l, g, t_true, win_prev):
    window, dil = ATTN_GROUPS[g]
    L = cache_t.shape[2]

    def col(c):
        return pl.BlockSpec((T_PAD, 256), lambda b: (b, c // 256 + g))

    def par(stack):
        return pl.BlockSpec((None, 1, 256), lambda b: (stack * 3 + g, 0, 0))

    return pl.pallas_call(
        functools.partial(_attn_sample_kernel, L=L, dil=dil, t_true=t_true, head_base=4 * g),
        grid=(B,),
        in_specs=[col(COL_AQ), col(COL_AK), col(COL_AV), par(l), par(l), pl.BlockSpec(memory_space=pltpu.SMEM),
                  pl.BlockSpec((None, 512, L), lambda b: (l * B + b, 0, 0)),
                  pl.BlockSpec(memory_space=pl.ANY)],
        out_specs=[
            pl.BlockSpec((T_PAD, 256), lambda b: (b, 0)),
            pl.BlockSpec((T_PAD, 256), lambda b: (b, 0)),
            pl.BlockSpec((None, 512, L), lambda b: (l * B + b, 0, 0)),
        ],
        out_shape=[
            jax.ShapeDtypeStruct((B * T_PAD, 256), F32),
            jax.ShapeDtypeStruct((B * T_PAD, 256), F32),
            jax.ShapeDtypeStruct((DEPTH * B, 512, L), F32),
        ],
        input_output_aliases={7: 2},
        compiler_params=_cparams(("parallel",)),
    )(P, P, P, q_gain, k_gain, slopes, cache_t, win_prev)


def _merge_groups(o0, l0, o1, l1, o2, l2):
    a0, a1, a2 = l0[...], l1[...], l2[...]
    m = jnp.maximum(jnp.maximum(a0, a1), a2)
    e0, e1, e2 = jnp.exp(a0 - m), jnp.exp(a1 - m), jnp.exp(a2 - m)
    den = e0 + e1 + e2
    return (e0 / den) * o0[...] + (e1 / den) * o1[...] + (e2 / den) * o2[...]


def _attn_merge_kernel(o0, l0, o1, l1, o2, l2, slot_ref, y_ref, *, n_tiles, slot_rows):
    @pl.when(pl.program_id(0) < n_tiles)
    def _():
        y_ref[...] = _merge_groups(o0, l0, o1, l1, o2, l2).astype(BF16)

    @pl.when(pl.program_id(0) == n_tiles)
    def _():
        y_ref[0:slot_rows, :] = slot_ref[...]


def attn_merge(parts, tm, slot):
    M = parts[0].shape[0]
    n_tiles = M // tm
    slot_rows = slot.shape[0]
    spec = pl.BlockSpec((tm, 256), lambda m: (jnp.minimum(m, n_tiles - 1), 0))
    return pl.pallas_call(
        functools.partial(_attn_merge_kernel, n_tiles=n_tiles, slot_rows=slot_rows),
        grid=(n_tiles + 1,),
        in_specs=[spec] * 6 + [pl.BlockSpec((slot_rows, 256), lambda m: (0, 0))],
        out_specs=pl.BlockSpec((tm, 256), lambda m: (m, 0)),
        out_shape=jax.ShapeDtypeStruct((M + slot_rows, 256), BF16),
        compiler_params=_cparams(("arbitrary",)),
    )(*parts, slot)


def _sample_pack_kernel(sel_ref, ya_ref, yd_ref, yc_ref, o0, l0, o1, l1, o2, l2, ya_o, yb_o, yc_o, yd_o):
    sel = sel_ref[...]

    def pack(y):
        return jnp.dot(sel, y, preferred_element_type=F32).astype(BF16)

    ya_o[...] = pack(ya_ref[...])
    yd_o[...] = pack(yd_ref[...])
    yc_o[...] = pack(yc_ref[...])
    yb_o[...] = pack(_merge_groups(o0, l0, o1, l1, o2, l2).astype(BF16))


def sample_pack(ya, yd, yc, parts, n_seq, t_true):
    rows = n_seq * T_PAD
    sel = np.zeros((SAMPLE_SLOT, rows), np.float32)
    for b in range(n_seq):
        for t in range(t_true):
            sel[b * t_true + t, b * T_PAD + t] = 1.0
    out = lambda w: jax.ShapeDtypeStruct((SAMPLE_SLOT, w), BF16)
    return pl.pallas_call(
        _sample_pack_kernel,
        out_shape=[out(512), out(256), out(512), out(512)],
        compiler_params=pltpu.CompilerParams(vmem_limit_bytes=VMEM_LIMIT),
    )(jnp.asarray(sel, BF16), ya, yd, yc, *parts)


GATE_SRC0 = 5904
assert GATE_SRC0 % SRC_ALIGN == 0 and D_MODEL % SRC_ALIGN == 0


def _merge_kernel(xn_ref, ya_ref, yb_ref, yc_ref, yd_ref, g0, g1, g2, g3, u0, u1, u2, u3, o_ref,
                  gs0, gs1, gs2, gs3, us0, us1, us2, us3):
    branches = ((ya_ref, g0, u0, gs0, us0), (yb_ref, g1, u1, gs1, us1),
                (yc_ref, g2, u2, gs2, us2), (yd_ref, g3, u3, gs3, us3))

    @pl.when(pl.program_id(1) == 0)
    def _():
        for _, g_ref, u_ref, g_s, u_s in branches:
            g_s[...] = g_ref[...].astype(BF16)
            u_s[...] = u_ref[...].astype(BF16)

    xn = xn_ref[...]
    acc = None
    for y_ref, _, _, g_s, u_s in branches:
        gate = jax.nn.sigmoid(_dot_nt(xn, g_s[...]))
        term = gate * jnp.dot(y_ref[...], u_s[...], preferred_element_type=F32)
        acc = term if acc is None else acc + term
    o_ref[...] = acc.astype(BF16)


def merge(xn, ya, yb, yc, yd, w_in_t, ups, l, tm):
    M = xn.shape[0]
    assert M % tm == 0
    tn = PREP_ROWS
    widths = (512, 256, 512, 512)

    def act(width):
        return pl.BlockSpec((tm, width), lambda n, m: (m, 0))

    def gate(b):
        base = (GATE_SRC0 + b * D_MODEL) // SRC_ALIGN
        return _w_in_rows(l, lambda n, m: base + n * (tn // SRC_ALIGN))

    def up(width):
        return pl.BlockSpec((None, width, tn), lambda n, m: (l, 0, n))

    return pl.pallas_call(
        _merge_kernel,
        grid=(D_MODEL // tn, M // tm),
        in_specs=[act(D_MODEL)] + [act(w) for w in widths] + [gate(b) for b in range(N_BRANCH)]
                 + [up(w) for w in widths],
        out_specs=pl.BlockSpec((tm, tn), lambda n, m: (m, n)),
        out_shape=jax.ShapeDtypeStruct((M, D_MODEL), BF16),
        scratch_shapes=[pltpu.VMEM((tn, D_MODEL), BF16)] * N_BRANCH + [pltpu.VMEM((w, tn), BF16) for w in widths],
        compiler_params=_cparams(("parallel", "arbitrary")),
    )(xn, ya, yb, yc, yd, w_in_t, w_in_t, w_in_t, w_in_t, *ups)


def _layer(x, xn, W, l, T, mixers):
    h = ffn_up(xn, W['ff1_gate'], W['ff1_up'], l, T['up_m'], T['up_n'])
    x, xn = matmul_res(x, h, W['ff1_down'], l, 0.5, T['down_m'], W['n_mix'], l)
    P = mix_in(xn, W['w_in_t'], l, T['mix_m'])
    (ya, yb, yc, yd), states = mixers(P, l)
    mg = merge(xn, ya, yb, yc, yd, W['w_in_t'], (W['up_a'], W['up_b'], W['up_c'], W['up_d']), l, T['merge_m'])
    x, xn = matmul_res(x, mg, W['w_out'], l, 1.0, T['out_m'], W['n_ff2'], l)
    h = ffn_up(xn, W['ff2_gate'], W['ff2_up'], l, T['up_m'], T['up_n'])
    if l + 1 < DEPTH:
        x, xn = matmul_res(x, h, W['ff2_down'], l, 0.5, T['down_m'], W['n_ff1'], l + 1)
    else:
        x, xn = matmul_res(x, h, W['ff2_down'], l, 0.5, T['down_m'], out_split=True)
    return x, xn, states


SAMPLE_SLOT = 128
TILES = dict(up_m=2080, up_n=512, mix_m=1040, down_m=256, out_m=512, merge_m=1040)


def kernel(x_prompt, x_sample, state_conv, cache_w128_kv, cache_w512_kv, cache_w2048_kv, state_gla, state_pool,
           norm_ff1, ff1_gate, ff1_up, ff1_down, norm_mix, w_in, conv_w, attn_q_gain, attn_k_gain,
           gla_w_a2, gla_b_a, gla_norm, pool_w, pool_scale, w_up_conv, w_up_attn, w_up_gla, w_up_pool, w_out,
           norm_ff2, ff2_gate, ff2_up, ff2_down):
    BP, S, _ = x_prompt.shape
    BS, TS, _ = x_sample.shape
    caches = (cache_w128_kv, cache_w512_kv, cache_w2048_kv)

    W = {
        'n_ff1': norm_ff1.reshape(DEPTH, 1, D_MODEL), 'n_mix': norm_mix.reshape(DEPTH, 1, D_MODEL),
        'n_ff2': norm_ff2.reshape(DEPTH, 1, D_MODEL),
        'ff1_gate': ff1_gate, 'ff1_up': ff1_up, 'ff1_down': ff1_down,
        'ff2_gate': ff2_gate, 'ff2_up': ff2_up, 'ff2_down': ff2_down,
        'w_in_t': jnp.transpose(w_in, (0, 2, 1)),
        'up_a': w_up_conv, 'up_b': w_up_attn, 'up_c': w_up_gla, 'up_d': w_up_pool, 'w_out': w_out,
    }
    wa_pad = jnp.pad(gla_w_a2, ((0, 0), (0, LANES - gla_w_a2.shape[1]), (0, 0))).astype(BF16)
    b_a = gla_b_a.reshape(DEPTH, 1, 256)
    gn = gla_norm.reshape(DEPTH, 1, LANES)
    pool_wb = pool_w.astype(BF16)
    pool_sc = pool_scale.reshape(DEPTH, 1, W_POOL)
    i = jnp.arange(1, N_ATTN_HEADS + 1, dtype=F32)
    slopes = jnp.exp2(-8.0 * i / N_ATTN_HEADS)
    qg6 = attn_q_gain.reshape(DEPTH * 6, 1, LANES)
    kg6 = attn_k_gain.reshape(DEPTH * 6, 1, LANES)
    qg3 = attn_q_gain.reshape(DEPTH * 3, 1, 256)
    kg3 = attn_k_gain.reshape(DEPTH * 3, 1, 256)

    zc = jnp.zeros((BP, CONV_HIST, W_CONV), F32)
    zp = jnp.zeros((BP, POOL_HIST16, W_POOL), F32)
    zs = jnp.zeros((BP, 256, LANES), F32)

    win_p = [jnp.zeros((DEPTH * BP, 2, 256, min(w, S)), F32) for w, _ in ATTN_GROUPS]
    win_s = [jnp.zeros((DEPTH * BS, 512, cc.shape[2]), F32) for cc in caches]

    MP, MS = BP * S, BS * TS
    assert MS <= SAMPLE_SLOT

    def prompt_mixers(P, l, slot):
        ya, yd, cnew, pnew = conv_pool(P, BP, S, zc, zp, conv_w, pool_wb, pool_sc, l, 1024, 1024, 0,
                                       slot=(slot[0], slot[3]))
        yc, sfin = gla(P, BP, S, wa_pad, b_a, gn, zs, l, 512, GLA_CHUNK, GLA_CHUNK, slot=slot[2])
        parts = []
        for g in range(3):
            o, lse, win_p[g] = attn_prompt(P, BP, S, qg6, kg6, slopes, l, g, win_p[g])
            parts += [o.reshape(MP, 256), lse.reshape(MP, 256)]
        yb = attn_merge(parts, 2048, slot=slot[1])
        return (ya, yb, yc, yd), (cnew, sfin.reshape(BP, 4, GLA_DK, LANES), pnew[:, 1:])

    sconv8 = jnp.pad(state_conv, ((0, 0), (0, 0), (CONV_HIST - 2, 0), (0, 0)))
    spool16 = jnp.pad(state_pool, ((0, 0), (0, 0), (1, 0), (0, 0)))
    sgla = state_gla.reshape(DEPTH, BS, 256, LANES)
    cviews = [jnp.transpose(cc, (0, 1, 3, 4, 5, 2)).reshape(DEPTH * BS, 512, cc.shape[2]) for cc in caches]

    def sample_mixers(P, l):
        MSP = BS * T_PAD
        Ps = jnp.pad(P[MP:MP + MS].reshape(BS, TS, N_MIX), ((0, 0), (0, T_PAD - TS), (0, 0))).reshape(MSP, N_MIX)
        ya, yd, cnew, pnew = conv_pool(Ps, BS, T_PAD, sconv8[l], spool16[l], conv_w, pool_wb, pool_sc, l,
                                       T_PAD, TS, PAST_LEN)
        yc, sfin = gla(Ps, BS, T_PAD, wa_pad, b_a, gn, sgla[l], l, T_PAD, T_PAD, TS)
        parts = []
        for g in range(3):
            o, lse, win_s[g] = attn_sample(Ps, BS, qg3, kg3, slopes, cviews[g], l, g, TS, win_s[g])
            parts += [o, lse]
        ys = sample_pack(ya, yd, yc, parts, BS, TS)
        return ys, (cnew, sfin.reshape(BS, 4, GLA_DK, LANES), pnew[:, 1:])

    def mixers(P, l):
        ys_s, st_sample = sample_mixers(P, l)
        ys, st_prompt = prompt_mixers(P, l, ys_s)
        return ys, (st_prompt, st_sample)

    x = (x_prompt.reshape(MP, D_MODEL), jnp.pad(x_sample.reshape(MS, D_MODEL), ((0, SAMPLE_SLOT - MS), (0, 0))))
    xn = norm(x[0], x[1], W['n_ff1'], 0, 1024)
    st_p, st_s = [], []
    for l in range(DEPTH):
        x, xn, (sp, ss) = _layer(x, xn, W, l, TILES, mixers)
        st_p.append(sp)
        st_s.append(ss)
    yp, ys = x[0], x[1][:MS]

    def stack(sts, f):
        return jnp.stack([f(s) for s in sts])

    def window_out(buf, B):
        n = buf.shape[-1]
        return jnp.transpose(buf.reshape(DEPTH, B, 2, 4, HEAD_DIM, n), (0, 1, 5, 2, 3, 4))

    outs = [yp.reshape(BP, S, D_MODEL), ys.reshape(BS, TS, D_MODEL),
            stack(st_p, lambda s: s[0]), stack(st_s, lambda s: s[0])]
    for g in range(3):
        outs += [window_out(win_p[g], BP), window_out(win_s[g], BS)]
    outs += [stack(st_p, lambda s: s[1]), stack(st_s, lambda s: s[1]),
             stack(st_p, lambda s: s[2]), stack(st_s, lambda s: s[2])]
    return tuple(outs)
```
